```python
import jax
import jax.numpy as jnp
from jax import lax
import numpy as np

D_MODEL = 1024
BATCH = 1
SEQ = 16384
DEPTH = 2

GRID_W = 64
CTX_LEN = 256
N_MOD = 6
EPS = 1e-6
NEG = -1e30

CONV_CH = 256
CONV_W = 3
DN_HEADS = 6
DN_HEAD_DIM = 64
DN_DIM = DN_HEADS * DN_HEAD_DIM
DN_CONV_W = 3
DN_CHUNK = 64
ATT_HEADS = 6
ATT_KV_HEADS = 2
ATT_GROUP = ATT_HEADS // ATT_KV_HEADS
HEAD_DIM = 64
ATT_DIM = ATT_HEADS * HEAD_DIM
ATT_KV_DIM = ATT_KV_HEADS * HEAD_DIM
WINDOW = 128
ATT_BLOCK = 128
ROPE_BASE = 10000.0
AXIS_DIM = HEAD_DIM // 2

MIX_DIM = CONV_CH + DN_DIM + ATT_DIM
IN_SPLITS = (CONV_CH, CONV_CH, CONV_CH, 3 * DN_DIM, DN_DIM, 2 * DN_HEADS, 2 * DN_HEADS, ATT_DIM, ATT_KV_DIM, ATT_KV_DIM)
IN_COLS = 3 * CONV_CH + 4 * DN_DIM + 4 * DN_HEADS + ATT_DIM + 2 * ATT_KV_DIM

D_FF = 2816
N_EXPERTS = 8
TOP_K = 2
D_FF_EXPERT = 3584
MOE_BLOCK = 128
N_DENSE = (DEPTH + 1) // 2
N_MOE = DEPTH // 2

kernel_name = 'hybrid_conv_deltanet_swa_moe_dit'

F32 = jnp.float32


def rmsnorm(x, g=None):
    xf = x.astype(F32)
    y = xf * lax.rsqrt(jnp.mean(xf * xf, axis=-1, keepdims=True) + EPS)
    if g is not None:
        y = y * g.astype(F32)
    return y.astype(x.dtype)


def modulate(h, shift, scale):
    return h * (1 + scale) + shift


def l2norm(t):
    return t * lax.rsqrt(jnp.sum(t * t, axis=-1, keepdims=True) + 1e-6)


def split_proj(p):
    idx, acc = [], 0
    for s in IN_SPLITS[:-1]:
        acc += s
        idx.append(acc)
    return jnp.split(p, idx, axis=-1)


def dwconv_centred(u, w):
    k = w.shape[0]
    return lax.conv_general_dilated(u, w[:, None, :].astype(u.dtype), window_strides=(1,),
                                    padding=[(k // 2, k // 2)], dimension_numbers=('NWC', 'WIO', 'NWC'),
                                    feature_group_count=u.shape[-1])


def axial_rope_tables(n):
    rows = n // GRID_W
    r = jnp.broadcast_to(jnp.arange(rows, dtype=F32)[:, None], (rows, GRID_W)).reshape(n)
    col = jnp.broadcast_to(jnp.arange(GRID_W, dtype=F32)[None, :], (rows, GRID_W)).reshape(n)
    inv = ROPE_BASE ** (-jnp.arange(0, AXIS_DIM, 2, dtype=F32) / AXIS_DIM)
    ang = jnp.stack([r[:, None] * inv, col[:, None] * inv], axis=1)
    return jnp.cos(ang), jnp.sin(ang)


def apply_axial_rope(x, cos, sin):
    b, n, h, d = x.shape
    xr = x.astype(F32).reshape(b, n, h, 2, 2, AXIS_DIM // 2)
    x1, x2 = xr[..., 0, :], xr[..., 1, :]
    cs, sn = cos[None, :, None], sin[None, :, None]
    out = jnp.stack([x1 * cs - x2 * sn, x2 * cs + x1 * sn], axis=-2)
    return out.reshape(b, n, h, d).astype(x.dtype)


def window_attention(q, k, v, kc, vc, sink):
    b, n = q.shape[:2]
    nb = n // ATT_BLOCK
    qb = q.astype(F32).reshape(b, nb, ATT_BLOCK, ATT_KV_HEADS, ATT_GROUP, HEAD_DIM) * HEAD_DIM ** -0.5

    def band(t):
        tp = jnp.pad(t.astype(F32), ((0, 0), (ATT_BLOCK, ATT_BLOCK), (0, 0), (0, 0)))
        tp = tp.reshape(b, nb + 2, ATT_BLOCK, ATT_KV_HEADS, HEAD_DIM)
        return jnp.concatenate([tp[:, :-2], tp[:, 1:-1], tp[:, 2:]], axis=2)

    kw, vw = band(k), band(v)
    s_loc = jnp.einsum('bnqhgd,bnkhd->bnhgqk', qb, kw)
    qpos = jnp.arange(nb)[:, None] * ATT_BLOCK + jnp.arange(ATT_BLOCK)[None]
    kpos = (jnp.arange(nb)[:, None] - 1) * ATT_BLOCK + jnp.arange(3 * ATT_BLOCK)[None]
    valid = ((jnp.abs(qpos[:, :, None] - kpos[:, None, :]) <= WINDOW)
             & (kpos[:, None, :] >= 0) & (kpos[:, None, :] < n))
    s_loc = jnp.where(valid[None, :, None, None], s_loc, NEG)
    s_ctx = jnp.einsum('bnqhgd,blhd->bnhgql', qb, kc.astype(F32))
    s_sink = sink.astype(F32).reshape(1, 1, ATT_KV_HEADS, ATT_GROUP, 1, 1)
    m = jnp.maximum(jnp.maximum(s_loc.max(-1, keepdims=True), s_ctx.max(-1, keepdims=True)), s_sink)
    p_loc = jnp.exp(s_loc - m)
    p_ctx = jnp.exp(s_ctx - m)
    denom = p_loc.sum(-1, keepdims=True) + p_ctx.sum(-1, keepdims=True) + jnp.exp(s_sink - m)
    o = (jnp.einsum('bnhgqk,bnkhd->bnhgqd', p_loc, vw)
         + jnp.einsum('bnhgql,blhd->bnhgqd', p_ctx, vc.astype(F32))) / denom
    return o.transpose(0, 1, 4, 2, 3, 5).reshape(b, n, ATT_DIM).astype(q.dtype)


def context_attention(q, k, v, sink):
    b, l = q.shape[:2]
    qg = q.astype(F32).reshape(b, l, ATT_KV_HEADS, ATT_GROUP, HEAD_DIM) * HEAD_DIM ** -0.5
    s = jnp.einsum('blhgd,bmhd->bhglm', qg, k.astype(F32))
    s_sink = jnp.broadcast_to(sink.astype(F32).reshape(1, ATT_KV_HEADS, ATT_GROUP, 1, 1), s.shape[:-1] + (1,))
    p = jax.nn.softmax(jnp.concatenate([s, s_sink], axis=-1), axis=-1)[..., :-1]
    o = jnp.einsum('bhglm,bmhd->blhgd', p, v.astype(F32))
    return o.reshape(b, l, ATT_DIM).astype(q.dtype)


def delta_prep(qkv, a, bt, conv_w, a_log, dt_bias):
    b, n, _ = qkv.shape
    qkv = jax.nn.silu(dwconv_centred(qkv, conv_w)).astype(F32).reshape(b, n, 3, DN_HEADS, DN_HEAD_DIM)
    q = l2norm(qkv[:, :, 0]) * DN_HEAD_DIM ** -0.5
    k = l2norm(qkv[:, :, 1])
    v = qkv[:, :, 2]
    a = a.astype(F32).reshape(b, n, 2, DN_HEADS)
    g = -jnp.exp(a_log.astype(F32)) * jax.nn.softplus(a + dt_bias.astype(F32))
    beta = jax.nn.sigmoid(bt.astype(F32).reshape(b, n, 2, DN_HEADS))
    return q, k, v, g, beta


def gated_delta_chunked(q, k, v, g, beta, s0):
    b, n, h, _ = q.shape
    dv = v.shape[-1]
    nc = n // DN_CHUNK

    def chunks(t):
        t = t.reshape((b, nc, DN_CHUNK, h) + t.shape[3:])
        return jnp.moveaxis(t, 3, 1)

    q, k, v, g, beta = (chunks(t) for t in (q, k, v, g, beta))
    gcum = jnp.cumsum(g, axis=-1)
    incl = jnp.tril(jnp.ones((DN_CHUNK, DN_CHUNK), bool))
    strict = jnp.tril(jnp.ones((DN_CHUNK, DN_CHUNK), bool), -1)
    diff = gcum[..., :, None] - gcum[..., None, :]
    decay = jnp.where(incl, jnp.exp(jnp.where(incl, diff, 0.0)), 0.0)
    kb = k * beta[..., None]
    a_mat = jnp.where(strict, jnp.einsum('bhcid,bhcjd->bhcij', kb, k) * decay, 0.0)
    lmat = a_mat + jnp.eye(DN_CHUNK, dtype=F32)
    rhs = jnp.concatenate([v * beta[..., None], kb * jnp.exp(gcum)[..., None]], axis=-1)
    sol = lax.linalg.triangular_solve(lmat, rhs, left_side=True, lower=True, unit_diagonal=True)
    u, w = sol[..., :dv], sol[..., dv:]
    qk = jnp.where(incl, jnp.einsum('bhcid,bhcjd->bhcij', q, k) * decay, 0.0)
    qg = q * jnp.exp(gcum)[..., None]
    kd = k * jnp.exp(gcum[..., -1:] - gcum)[..., None]
    gl = jnp.exp(gcum[..., -1])
    xs = tuple(jnp.moveaxis(t, 2, 0) for t in (qg, qk, u, w, kd, gl))

    def step(s, inp):
        qg_c, qk_c, u_c, w_c, kd_c, gl_c = inp
        v_new = u_c - jnp.einsum('bhik,bhkv->bhiv', w_c, s)
        o = jnp.einsum('bhik,bhkv->bhiv', qg_c, s) + jnp.einsum('bhij,bhjv->bhiv', qk_c, v_new)
        s = s * gl_c[..., None, None] + jnp.einsum('bhik,bhiv->bhkv', kd_c, v_new)
        return s, o

    s_fin, o = lax.scan(step, s0, xs)
    return o.transpose(1, 0, 3, 2, 4).reshape(b, n, h, dv), s_fin


def gated_delta_bidir(ctx_t, lat_t, need_ctx):
    qc, kc, vc, gc, bc = ctx_t
    qx, kx, vx, gx, bx = lat_t
    b = qx.shape[0]
    outs_x, outs_c = [], []
    for d in range(2):
        rev = (lambda t: jnp.flip(t, axis=1)) if d == 1 else (lambda t: t)
        s0 = jnp.zeros((b, DN_HEADS, DN_HEAD_DIM, DN_HEAD_DIM), F32)
        oc, s_ctx = gated_delta_chunked(rev(qc), rev(kc), rev(vc), rev(gc[:, :, d]), rev(bc[:, :, d]), s0)
        ox, _ = gated_delta_chunked(rev(qx), rev(kx), rev(vx), rev(gx[:, :, d]), rev(bx[:, :, d]), s_ctx)
        outs_x.append(rev(ox))
        outs_c.append(rev(oc))
    o_x = outs_x[0] + outs_x[1]
    o_c = outs_c[0] + outs_c[1] if need_ctx else None
    return o_x, o_c


def delta_output(o, z, norm_g):
    b, n = z.shape[:2]
    zf = z.astype(F32).reshape(b, n, DN_HEADS, DN_HEAD_DIM)
    y = rmsnorm(o, norm_g) * jax.nn.silu(zf)
    return y.reshape(b, n, DN_DIM).astype(z.dtype)


def token_mixers(px, pc, conv_w, dn_conv_w, dn_a_log, dn_dt_bias, dn_norm_g, sink, cos, sin, need_ctx):
    b, n, _ = px.shape
    l = pc.shape[1]
    xb_, xc_, xh_, xqkv, xz, xa, xbt, xq, xk, xv = split_proj(px)
    cb_, cc_, ch_, cqkv, cz, ca, cbt, cq, ck, cv = split_proj(pc)
    ya_x = xb_ * dwconv_centred(xc_ * xh_, conv_w)
    dx = delta_prep(xqkv, xa, xbt, dn_conv_w, dn_a_log, dn_dt_bias)
    dc = delta_prep(cqkv, ca, cbt, dn_conv_w, dn_a_log, dn_dt_bias)
    ob_x, ob_c = gated_delta_bidir(dc, dx, need_ctx)
    yb_x = delta_output(ob_x, xz, dn_norm_g)
    qx = apply_axial_rope(xq.reshape(b, n, ATT_HEADS, HEAD_DIM), cos, sin)
    kx = apply_axial_rope(xk.reshape(b, n, ATT_KV_HEADS, HEAD_DIM), cos, sin)
    vx = xv.reshape(b, n, ATT_KV_HEADS, HEAD_DIM)
    kc = ck.reshape(b, l, ATT_KV_HEADS, HEAD_DIM)
    vc = cv.reshape(b, l, ATT_KV_HEADS, HEAD_DIM)
    yc_x = window_attention(qx, kx, vx, kc, vc, sink)
    o_x = jnp.concatenate([ya_x, yb_x, yc_x], axis=-1)
    if not need_ctx:
        return o_x, None
    ya_c = cb_ * dwconv_centred(cc_ * ch_, conv_w)
    yb_c = delta_output(ob_c, cz, dn_norm_g)
    yc_c = context_attention(cq.reshape(b, l, ATT_HEADS, HEAD_DIM), kc, vc, sink)
    o_c = jnp.concatenate([ya_c, yb_c, yc_c], axis=-1)
    return o_x, o_c


def swiglu(h, w_gate, w_up, w_down):
    return (jax.nn.silu(h @ w_gate) * (h @ w_up)) @ w_down


def moe_swiglu(h, w_router, w_gate, w_up, w_down):
    t, d = h.shape
    logits = h.astype(F32) @ w_router.astype(F32)
    top_logit, top_e = lax.top_k(logits, TOP_K)
    gates = jax.nn.softmax(top_logit, axis=-1)
    a = t * TOP_K
    flat_e = top_e.reshape(a)
    flat_tok = jnp.arange(a, dtype=jnp.int32) // TOP_K
    flat_g = gates.reshape(a)
    order = jnp.argsort(flat_e)
    e_sorted = flat_e[order]
    counts = jnp.zeros((N_EXPERTS,), jnp.int32).at[flat_e].add(1)
    starts = jnp.cumsum(counts) - counts
    padded = (counts + MOE_BLOCK - 1) // MOE_BLOCK * MOE_BLOCK
    pad_ends = jnp.cumsum(padded)
    pad_starts = pad_ends - padded
    dest = pad_starts[e_sorted] + jnp.arange(a, dtype=jnp.int32) - starts[e_sorted]
    cap = -(-a // MOE_BLOCK) * MOE_BLOCK + N_EXPERTS * MOE_BLOCK
    n_blk = cap // MOE_BLOCK
    row_tok = jnp.zeros((cap,), jnp.int32).at[dest].set(flat_tok[order])
    row_gate = jnp.zeros((cap,), h.dtype).at[dest].set(flat_g[order].astype(h.dtype))
    blk_e = jnp.minimum(jnp.searchsorted(pad_ends, jnp.arange(n_blk, dtype=jnp.int32) * MOE_BLOCK, side='right'),
                        N_EXPERTS - 1)
    xb = h[row_tok].reshape(n_blk, MOE_BLOCK, d)

    def block_ffn(args):
        xblk, e = args
        return swiglu(xblk, w_gate[e], w_up[e], w_down[e])

    yb = lax.map(block_ffn, (xb, blk_e))
    return jax.ops.segment_sum(yb.reshape(cap, d) * row_gate[:, None], row_tok, num_segments=t)


def channel_mixer(h, layer, ffn_w_gate, ffn_w_up, ffn_w_down, moe_router, moe_w_gate, moe_w_up, moe_w_down):
    i = layer // 2
    if layer % 2 == 0:
        return swiglu(h, ffn_w_gate[i], ffn_w_up[i], ffn_w_down[i])
    return moe_swiglu(h, moe_router[i], moe_w_gate[i], moe_w_up[i], moe_w_down[i])


def setup_inputs(seed: int = 0) -> dict:
    key = jax.random.key(seed)
    ks = jax.random.split(key, 24)

    def nrm(k, shape, scale):
        return jax.random.normal(k, shape, F32) * scale

    dt = jnp.exp(jax.random.uniform(ks[11], (DEPTH, 2, DN_HEADS), F32, jnp.log(1e-3), jnp.log(1e-1)))
    return {
        'x': nrm(ks[0], (BATCH, SEQ, D_MODEL), 1.0),
        'c': nrm(ks[1], (BATCH, D_MODEL), 1.0),
        'ctx': nrm(ks[2], (BATCH, CTX_LEN, D_MODEL), 1.0),
        'c_ctx': nrm(ks[3], (D_MODEL,), 1.0),
        'w_mod': nrm(ks[4], (DEPTH, D_MODEL, N_MOD * D_MODEL), 0.5 * D_MODEL ** -0.5),
        'b_mod': nrm(ks[5], (DEPTH, N_MOD * D_MODEL), 0.02),
        'w_in': nrm(ks[6], (DEPTH, D_MODEL, IN_COLS), D_MODEL ** -0.5),
        'w_out': nrm(ks[7], (DEPTH, MIX_DIM, D_MODEL), MIX_DIM ** -0.5),
        'conv_w': nrm(ks[8], (DEPTH, CONV_W, CONV_CH), CONV_W ** -0.5),
        'dn_conv_w': nrm(ks[9], (DEPTH, DN_CONV_W, 3 * DN_DIM), DN_CONV_W ** -0.5),
        'dn_a_log': jnp.log(jax.random.uniform(ks[10], (DEPTH, 2, DN_HEADS), F32, 1.0, 16.0)),
        'dn_dt_bias': dt + jnp.log(-jnp.expm1(-dt)),
        'dn_norm_g': 1.0 + nrm(ks[12], (DEPTH, DN_HEAD_DIM), 0.02),
        'attn_sink': nrm(ks[13], (DEPTH, ATT_HEADS), 0.5),
        'ffn_w_gate': nrm(ks[14], (N_DENSE, D_MODEL, D_FF), D_MODEL ** -0.5),
        'ffn_w_up': nrm(ks[15], (N_DENSE, D_MODEL, D_FF), D_MODEL ** -0.5),
        'ffn_w_down': nrm(ks[16], (N_DENSE, D_FF, D_MODEL), D_FF ** -0.5),
        'moe_router': nrm(ks[17], (N_MOE, D_MODEL, N_EXPERTS), D_MODEL ** -0.5),
        'moe_w_gate': nrm(ks[18], (N_MOE, N_EXPERTS, D_MODEL, D_FF_EXPERT), D_MODEL ** -0.5),
        'moe_w_up': nrm(ks[19], (N_MOE, N_EXPERTS, D_MODEL, D_FF_EXPERT), D_MODEL ** -0.5),
        'moe_w_down': nrm(ks[20], (N_MOE, N_EXPERTS, D_FF_EXPERT, D_MODEL), D_FF_EXPERT ** -0.5),
        'final_norm_g': 1.0 + nrm(ks[21], (D_MODEL,), 0.02),
    }


def reference(x, c, ctx, c_ctx, w_mod, b_mod, w_in, w_out, conv_w, dn_conv_w, dn_a_log, dn_dt_bias, dn_norm_g,
              attn_sink, ffn_w_gate, ffn_w_up, ffn_w_down, moe_router, moe_w_gate, moe_w_up, moe_w_down,
              final_norm_g):
    b, n, d = x.shape
    ctx_len = ctx.shape[1]
    cos, sin = axial_rope_tables(n)
    s_c = jax.nn.silu(c)
    s_cc = jax.nn.silu(c_ctx)
    hc = ctx
    for layer in range(DEPTH):
        need_ctx = layer < DEPTH - 1
        mod_x = (s_c @ w_mod[layer] + b_mod[layer]).reshape(b, N_MOD, 1, d)
        mod_c = (s_cc @ w_mod[layer] + b_mod[layer]).reshape(N_MOD, d)
        px = modulate(rmsnorm(x), mod_x[:, 0], mod_x[:, 1]) @ w_in[layer]
        pc = modulate(rmsnorm(hc), mod_c[0], mod_c[1]) @ w_in[layer]
        o_x, o_c = token_mixers(px, pc, conv_w[layer], dn_conv_w[layer], dn_a_log[layer], dn_dt_bias[layer],
                                dn_norm_g[layer], attn_sink[layer], cos, sin, need_ctx)
        x = x + mod_x[:, 2] * (o_x @ w_out[layer])
        hx = modulate(rmsnorm(x), mod_x[:, 3], mod_x[:, 4])
        if need_ctx:
            hc = hc + mod_c[2] * (o_c @ w_out[layer])
            hcc = modulate(rmsnorm(hc), mod_c[3], mod_c[4])
            tokens = jnp.concatenate([hcc.reshape(-1, d), hx.reshape(-1, d)], axis=0)
        else:
            tokens = hx.reshape(-1, d)
        f = channel_mixer(tokens, layer, ffn_w_gate, ffn_w_up, ffn_w_down, moe_router, moe_w_gate, moe_w_up,
                          moe_w_down)
        if need_ctx:
            hc = hc + mod_c[5] * f[:b * ctx_len].reshape(b, ctx_len, d)
            f = f[b * ctx_len:]
        x = x + mod_x[:, 5] * f.reshape(b, n, d)
    return rmsnorm(x, final_norm_g)
```

```python
import functools

import jax
import jax.numpy as jnp
from jax import lax
from jax.experimental import pallas as pl
from jax.experimental.pallas import tpu as pltpu

F32 = jnp.float32
BF16 = jnp.bfloat16

D = 1024
N_MOD = 6
EPS = 1e-6
NEG = -1e30
GRID_W = 64

CONV_CH = 256
DN_HEADS = 6
DN_HEAD_DIM = 64
DN_DIM = DN_HEADS * DN_HEAD_DIM
DN_CHUNK = 64
DN_SUB = 16
ATT_HEADS = 6
ATT_KV_HEADS = 2
ATT_GROUP = ATT_HEADS // ATT_KV_HEADS
HEAD_DIM = 64
ATT_DIM = ATT_HEADS * HEAD_DIM
ATT_KV_DIM = ATT_KV_HEADS * HEAD_DIM
ATT_BLOCK = 128
ROPE_BASE = 10000.0
AXIS_DIM = HEAD_DIM // 2
MIX_DIM = CONV_CH + DN_DIM + ATT_DIM

D_FF = 2816
N_EXPERTS = 8
TOP_K = 2
D_FF_EXPERT = 3584

TM = 256
FF_CHUNK = 256
MOE_TM = 512
MOE_TF = 512
LANE = 128
VMEM_LIMIT = 56 * 1024 * 1024

_C_QKV = 3 * CONV_CH
_C_Z = _C_QKV + 3 * DN_DIM
_C_A = _C_Z + DN_DIM
_C_Q = _C_A + 4 * DN_HEADS
_C_K = _C_Q + ATT_DIM
_C_V = _C_K + ATT_KV_DIM
_C_END = _C_V + ATT_KV_DIM
N_AB = 4 * DN_HEADS


def _params(sem=None, vmem=None):
    kw = {}
    if sem is not None:
        kw["dimension_semantics"] = sem
    if vmem is not None:
        kw["vmem_limit_bytes"] = vmem
    return pltpu.CompilerParams(**kw)


def _split2(a):
    hi = a.astype(BF16)
    lo = (a - hi.astype(F32)).astype(BF16)
    return hi, lo


def _split3(a):
    hi = a.astype(BF16)
    r = a - hi.astype(F32)
    mid = r.astype(BF16)
    lo = (r - mid.astype(F32)).astype(BF16)
    return hi, mid, lo


_NN = (((1,), (0,)), ((), ()))
_NT = (((1,), (1,)), ((), ()))
_TN = (((0,), (0,)), ((), ()))


def _mm(a, b, dims=_NN):
    return lax.dot_general(a.astype(BF16), b.astype(BF16), dims, preferred_element_type=F32)


def _mm3(a, b, dims=_NN):
    a1, a0 = _split2(a)
    b1, b0 = _split2(b)
    d = functools.partial(lax.dot_general, dimension_numbers=dims, preferred_element_type=F32)
    return d(a1, b1) + (d(a1, b0) + d(a0, b1))


def _sigmoid(x):
    return 1.0 / (1.0 + jnp.exp(-x))


def _softplus(x):
    return jnp.maximum(x, 0.0) + jnp.log1p(jnp.exp(-jnp.abs(x)))


def _mod_row(mod_ref, is_ctx):
    mod = mod_ref[...]
    return jnp.where(is_ctx, mod[1:2, :], mod[0:1, :])


def _rmsnorm_rows(x):
    return x * lax.rsqrt(jnp.mean(x * x, axis=-1, keepdims=True) + EPS)


def _shift_rows(u, prow, nrow):
    n = u.shape[0]
    rid = lax.broadcasted_iota(jnp.int32, u.shape, 0)
    up = jnp.where(rid == 0, prow, pltpu.roll(u, 1, 0))
    un = jnp.where(rid == n - 1, nrow, pltpu.roll(u, n - 1, 0))
    return up, un


def _same_group(shape, group):
    sh = group.bit_length() - 1
    assert 1 << sh == group
    return (lax.broadcasted_iota(jnp.int32, shape, 0) >> sh) == (lax.broadcasted_iota(jnp.int32, shape, 1) >> sh)


def _head_blockdiag(n, group):
    return jnp.where(_same_group((n, n), group), 1.0, 0.0).astype(BF16)


def _group_sum(t, bd):
    hi, lo = _split2(t)
    d = functools.partial(lax.dot_general, dimension_numbers=_NN, preferred_element_type=F32)
    return d(hi, bd) + d(lo, bd)


MOD_TN = 1536


def _mod_kernel(s_ref, w_ref, b_ref, o_ref):
    s = s_ref[...]
    s = s * _sigmoid(s)
    o_ref[0] = _mm3(s, w_ref[0]) + b_ref[0]


def _mod_vectors(c, c_ctx, w_mod, b_mod):
    depth = w_mod.shape[0]
    s = jnp.zeros((8, D), F32).at[0].set(c[0]).at[1].set(c_ctx)
    return pl.pallas_call(
        _mod_kernel,
        grid=(depth, N_MOD * D // MOD_TN),
        in_specs=[
            pl.BlockSpec((8, D), lambda l, j: (0, 0)),
            pl.BlockSpec((1, D, MOD_TN), lambda l, j: (l, 0, j)),
            pl.BlockSpec((1, 1, MOD_TN), lambda l, j: (l, 0, j)),
        ],
        out_specs=pl.BlockSpec((1, 8, MOD_TN), lambda l, j: (l, 0, j)),
        out_shape=jax.ShapeDtypeStruct((depth, 8, N_MOD * D), F32),
        compiler_params=_params(("arbitrary", "arbitrary"), VMEM_LIMIT),
        name="mod_vectors",
    )(s, w_mod, b_mod.reshape(depth, 1, N_MOD * D))


def _in_kernel(h_ref, mod_ref, w_ref, wab1_ref, wab0_ref,
               pconv_ref, pqkv_ref, pz_ref, pq_ref, pkv_ref, pab_ref):
    row = _mod_row(mod_ref, pl.program_id(0) == 0)
    hm = _rmsnorm_rows(h_ref[...]) * (1.0 + row[:, D:2 * D]) + row[:, 0:D]
    h1, h0 = _split2(hm)
    d = functools.partial(lax.dot_general, dimension_numbers=_NN, preferred_element_type=F32)
    off = 0
    for ref in (pconv_ref, pqkv_ref, pz_ref, pq_ref, pkv_ref):
        w = ref.shape[1]
        ref[...] = d(h1, w_ref[:, off:off + w])
        off += w
    pab_ref[...] = d(h1, wab1_ref[...]) + (d(h0, wab1_ref[...]) + d(h1, wab0_ref[...]))


def _in_proj(h, mod, w_main, wab1, wab0):
    t = h.shape[0]
    widths = (3 * CONV_CH, 3 * DN_DIM, DN_DIM, ATT_DIM, 2 * ATT_KV_DIM, LANE)
    const = lambda i: (0, 0)
    return pl.pallas_call(
        _in_kernel,
        grid=(t // TM,),
        in_specs=[
            pl.BlockSpec((TM, D), lambda i: (i, 0)),
            pl.BlockSpec((8, N_MOD * D), const),
            pl.BlockSpec(w_main.shape, const),
            pl.BlockSpec(wab1.shape, const),
            pl.BlockSpec(wab0.shape, const),
        ],
        out_specs=[pl.BlockSpec((TM, w), lambda i: (i, 0)) for w in widths],
        out_shape=[jax.ShapeDtypeStruct((t, w), F32) for w in widths],
        compiler_params=_params(("arbitrary",), VMEM_LIMIT),
        name="in_proj",
    )(h, mod, w_main, wab1, wab0)


def _halo_rows(prev_ref, next_ref, i, nblk):
    pvalid = jnp.logical_and(i != 0, i != 1)
    nvalid = jnp.logical_and(i != 0, i != nblk - 1)
    prow = jnp.where(pvalid, prev_ref[7:8, :], 0.0)
    nrow = jnp.where(nvalid, next_ref[0:1, :], 0.0)
    return prow, nrow


def _dnprep_kernel(qkv_ref, prev_ref, next_ref, ab_ref, cw_ref, alog_ref, dtb_ref,
                   q_ref, k_ref, v_ref, gb_ref, *, nblk):
    i = pl.program_id(0)
    u = qkv_ref[...]
    prow, nrow = _halo_rows(prev_ref, next_ref, i, nblk)
    up, un = _shift_rows(u, prow, nrow)
    cw = cw_ref[...]
    y = up * cw[0:1, :] + u * cw[1:2, :] + un * cw[2:3, :]
    y = y * _sigmoid(y)
    q = y[:, 0:DN_DIM]
    k = y[:, DN_DIM:2 * DN_DIM]
    bd = _head_blockdiag(DN_DIM, DN_HEAD_DIM)
    q_ref[...] = q * lax.rsqrt(_group_sum(q * q, bd) + 1e-6) * (DN_HEAD_DIM ** -0.5)
    k_ref[...] = k * lax.rsqrt(_group_sum(k * k, bd) + 1e-6)
    v_ref[...] = y[:, 2 * DN_DIM:3 * DN_DIM]
    ab = ab_ref[...]
    g = -jnp.exp(alog_ref[...]) * _softplus(ab + dtb_ref[...])
    lane = lax.broadcasted_iota(jnp.int32, ab.shape, 1)
    gb_ref[...] = jnp.where(lane < 2 * DN_HEADS, g, _sigmoid(ab))


def _dn_prep(pqkv, pab, dn_conv_w, a_log, dt_bias):
    t, w = pqkv.shape
    nblk = t // TM
    alog = jnp.zeros((1, LANE), F32).at[0, :2 * DN_HEADS].set(a_log.reshape(-1))
    dtb = jnp.zeros((1, LANE), F32).at[0, :2 * DN_HEADS].set(dt_bias.reshape(-1))
    r8 = TM // 8
    const = lambda i: (0, 0)
    return pl.pallas_call(
        functools.partial(_dnprep_kernel, nblk=nblk),
        grid=(nblk,),
        in_specs=[
            pl.BlockSpec((TM, w), lambda i: (i, 0)),
            pl.BlockSpec((8, w), lambda i: (jnp.maximum(i * r8 - 1, 0), 0)),
            pl.BlockSpec((8, w), lambda i: (jnp.minimum((i + 1) * r8, t // 8 - 1), 0)),
            pl.BlockSpec((TM, LANE), lambda i: (i, 0)),
            pl.BlockSpec((3, w), const),
            pl.BlockSpec((1, LANE), const),
            pl.BlockSpec((1, LANE), const),
        ],
        out_specs=[pl.BlockSpec((TM, DN_DIM), lambda i: (i, 0))] * 3
        + [pl.BlockSpec((TM, LANE), lambda i: (i, 0))],
        out_shape=[jax.ShapeDtypeStruct((t, DN_DIM), F32)] * 3 + [jax.ShapeDtypeStruct((t, LANE), F32)],
        compiler_params=_params(("arbitrary",), VMEM_LIMIT),
        name="dn_prep",
    )(pqkv, pqkv, pqkv, pab, dn_conv_w, alog, dtb)


def _dn_chunk(rev, q_ref, k_ref, v_ref, gb_ref, o_ref, s_ref):
    c_ = DN_CHUNK
    ri = lax.broadcasted_iota(jnp.int32, (c_, c_), 0)
    ci = lax.broadcasted_iota(jnp.int32, (c_, c_), 1)
    incl = (ri <= ci) if rev else (ri >= ci)
    strict = (ri < ci) if rev else (ri > ci)
    same_sub = _same_group((c_, c_), DN_SUB)
    eye = jnp.where(ri == ci, 1.0, 0.0)
    tri = jnp.where(incl, 1.0, 0.0).astype(BF16)
    last = 0 if rev else c_ - 1

    gb = gb_ref[...]
    gc = _cumsum_rows(tri, gb)
    gct = gc.T
    for h in range(DN_HEADS):
        col = (DN_HEADS if rev else 0) + h
        lanes = slice(h * DN_HEAD_DIM, (h + 1) * DN_HEAD_DIM)
        q = q_ref[:, lanes]
        k = k_ref[:, lanes]
        v = v_ref[:, lanes]
        gcol = gc[:, col:col + 1]
        grow = gct[col:col + 1, :]
        glast = gc[last:last + 1, col:col + 1]
        beta = gb[:, 2 * DN_HEADS + col:2 * DN_HEADS + col + 1]
        decay = jnp.where(incl, jnp.exp(jnp.where(incl, gcol - grow, 0.0)), 0.0)
        eg = jnp.exp(gcol)
        kb = k * beta
        a = jnp.where(strict, _mm3(kb, k, _NT) * decay, 0.0)
        qk = jnp.where(incl, _mm(q, k, _NT) * decay, 0.0)
        ad = jnp.where(same_sub, a, 0.0)
        ao = a - ad
        p = eye - ad
        n2 = _mm3(ad, ad)
        p = p + _mm3(p, n2)
        n4 = _mm3(n2, n2)
        p = p + _mm3(p, n4)
        n8 = _mm3(n4, n4)
        dinv = p + _mm3(p, n8)
        m = _mm3(dinv, ao)
        m2 = _mm3(m, m)
        sols = []
        for r in (v * beta, kb * eg):
            y = _mm3(dinv, r)
            z = y + _mm3(m2, y)
            sols.append(z - _mm3(m, z))
        u, w = sols
        s = s_ref[col]
        v_new = u - _mm(w, s)
        o_ref[:, lanes] = _mm(q * eg, s) + _mm(qk, v_new)
        kd = k * jnp.exp(glast - gcol)
        s_ref[col] = s * jnp.exp(glast) + _mm(kd, v_new, _TN)


def _cumsum_rows(tri_bf16, g):
    g2, g1, g0 = _split3(g)
    d = functools.partial(lax.dot_general, dimension_numbers=_NN, preferred_element_type=F32)
    return d(tri_bf16, g2) + (d(tri_bf16, g1) + d(tri_bf16, g0))


def _dn_kernel(qf, kf, vf, gf, qb, kb, vb, gbb, of_ref, ob_ref, s_ref):
    @pl.when(pl.program_id(0) == 0)
    def _():
        s_ref[...] = jnp.zeros(s_ref.shape, F32)

    _dn_chunk(False, qf, kf, vf, gf, of_ref, s_ref)
    _dn_chunk(True, qb, kb, vb, gbb, ob_ref, s_ref)


def _delta_net(q, k, v, gb, ctx_len):
    t = q.shape[0]
    nchunk = t // DN_CHUNK
    nctx = ctx_len // DN_CHUNK

    def fwd(s):
        return (s, 0)

    def bwd(s):
        return (jnp.where(s < nctx, nctx - 1 - s, nchunk + nctx - 1 - s), 0)

    wide = lambda im: pl.BlockSpec((DN_CHUNK, DN_DIM), im)
    narrow = lambda im: pl.BlockSpec((DN_CHUNK, LANE), im)
    return pl.pallas_call(
        _dn_kernel,
        grid=(nchunk,),
        in_specs=[wide(fwd), wide(fwd), wide(fwd), narrow(fwd), wide(bwd), wide(bwd), wide(bwd), narrow(bwd)],
        out_specs=[wide(fwd), wide(bwd)],
        out_shape=[jax.ShapeDtypeStruct((t, DN_DIM), F32)] * 2,
        scratch_shapes=[pltpu.VMEM((2 * DN_HEADS, DN_HEAD_DIM, DN_HEAD_DIM), F32)],
        compiler_params=_params(("arbitrary",), VMEM_LIMIT),
        name="delta_net",
    )(q, k, v, gb, q, k, v, gb)


def _rope(x, cos, sin):
    w = x.shape[1]
    lane = lax.broadcasted_iota(jnp.int32, x.shape, 1)
    first_half = (lane & (AXIS_DIM - 1)) < (AXIS_DIM // 2)
    swapped = jnp.where(first_half, pltpu.roll(x, w - AXIS_DIM // 2, 1), pltpu.roll(x, AXIS_DIM // 2, 1))
    return x * cos + swapped * sin


def _softmax_av(s, sink, vals):
    m = jnp.maximum(jnp.max(s, axis=-1, keepdims=True), sink)
    p = jnp.exp(s - m)
    denom = jnp.sum(p, axis=-1, keepdims=True) + jnp.exp(sink - m)
    return _mm(p, vals) / denom


def _attn_kernel(q_ref, kp_ref, kc_ref, kn_ref, kctx_ref, cq_ref, sq_ref, cp_ref, sp_ref, cn_ref, sn_ref,
                 sink_ref, o_ref, *, nb):
    i = pl.program_id(0)
    b = ATT_BLOCK
    cq = cq_ref[...]
    sq = sq_ref[...]
    q = q_ref[...]
    q = jnp.concatenate([_rope(q[:, j * LANE:(j + 1) * LANE], cq, sq) for j in range(ATT_DIM // LANE)], axis=1)
    q = q * (HEAD_DIM ** -0.5)
    kvp, kvc, kvn, kvx = kp_ref[...], kc_ref[...], kn_ref[...], kctx_ref[...]
    kp = _rope(kvp[:, :ATT_KV_DIM], cp_ref[...], sp_ref[...])
    kc = _rope(kvc[:, :ATT_KV_DIM], cq, sq)
    kn = _rope(kvn[:, :ATT_KV_DIM], cn_ref[...], sn_ref[...])
    keys = jnp.concatenate([kp, kc, kn, kvx[:, :ATT_KV_DIM]], axis=0)
    vals = jnp.concatenate([kvp[:, ATT_KV_DIM:], kvc[:, ATT_KV_DIM:], kvn[:, ATT_KV_DIM:], kvx[:, ATT_KV_DIM:]],
                           axis=0)
    nk = keys.shape[0]
    rows = ATT_GROUP * b
    c = lax.broadcasted_iota(jnp.int32, (1, nk), 1)
    far = 4 * nk
    ccol = jnp.where(c >= 3 * b, b + ((c - 3 * b) & (b - 1)), c)
    ccol = jnp.where(jnp.logical_and(c < b, i == 0), -far, ccol)
    ccol = jnp.where(jnp.logical_and(jnp.logical_and(c >= 2 * b, c < 3 * b), i == nb - 1), far, ccol)
    r = lax.broadcasted_iota(jnp.int32, (rows, 1), 0) & (b - 1)
    valid = lax.bitcast_convert_type(ccol - r, jnp.uint32) <= jnp.uint32(2 * b)
    sink_all = sink_ref[...]
    for kvh in range(ATT_KV_HEADS):
        kl = slice(kvh * HEAD_DIM, (kvh + 1) * HEAD_DIM)
        heads = range(kvh * ATT_GROUP, (kvh + 1) * ATT_GROUP)
        qs = jnp.concatenate([q[:, h * HEAD_DIM:(h + 1) * HEAD_DIM] for h in heads], axis=0)
        sink = jnp.concatenate([jnp.broadcast_to(sink_all[0:1, h:h + 1], (b, 1)) for h in heads], axis=0)
        s = jnp.where(valid, _mm(qs, keys[:, kl], _NT), NEG)
        o = _softmax_av(s, sink, vals[:, kl])
        for g, h in enumerate(heads):
            o_ref[:, h * HEAD_DIM:(h + 1) * HEAD_DIM] = o[g * b:(g + 1) * b, :]


def _window_attention(pq, pkv, cos, sin, sink, ctx_len):
    t = pq.shape[0]
    n = t - ctx_len
    nb = n // ATT_BLOCK
    off = ctx_len // ATT_BLOCK
    kv = lambda im: pl.BlockSpec((ATT_BLOCK, 2 * ATT_KV_DIM), im)
    tab = lambda im: pl.BlockSpec((ATT_BLOCK, LANE), im)
    cur = lambda i: (i, 0)
    prv = lambda i: (jnp.maximum(i - 1, 0), 0)
    nxt = lambda i: (jnp.minimum(i + 1, nb - 1), 0)
    shift = lambda im: (lambda i: (im(i)[0] + off, 0))
    return pl.pallas_call(
        functools.partial(_attn_kernel, nb=nb),
        grid=(nb,),
        in_specs=[
            pl.BlockSpec((ATT_BLOCK, ATT_DIM), shift(cur)),
            kv(shift(prv)), kv(shift(cur)), kv(shift(nxt)),
            pl.BlockSpec((ctx_len, 2 * ATT_KV_DIM), lambda i: (0, 0)),
            tab(cur), tab(cur), tab(prv), tab(prv), tab(nxt), tab(nxt),
            pl.BlockSpec((8, LANE), lambda i: (0, 0)),
        ],
        out_specs=pl.BlockSpec((ATT_BLOCK, ATT_DIM), shift(cur)),
        out_shape=jax.ShapeDtypeStruct((t, ATT_DIM), F32),
        compiler_params=_params(("arbitrary",), VMEM_LIMIT),
        name="window_attention",
    )(pq, pkv, pkv, pkv, pkv, cos, sin, cos, sin, cos, sin, sink)


def _ctx_attn_kernel(q_ref, kv_ref, sink_ref, prev_ref, o_ref):
    del prev_ref
    q = q_ref[...] * (HEAD_DIM ** -0.5)
    kv = kv_ref[...]
    n = q.shape[0]
    sink_all = sink_ref[...]
    for kvh in range(ATT_KV_HEADS):
        kl = slice(kvh * HEAD_DIM, (kvh + 1) * HEAD_DIM)
        heads = range(kvh * ATT_GROUP, (kvh + 1) * ATT_GROUP)
        qs = jnp.concatenate([q[:, h * HEAD_DIM:(h + 1) * HEAD_DIM] for h in heads], axis=0)
        sink = jnp.concatenate([jnp.broadcast_to(sink_all[0:1, h:h + 1], (n, 1)) for h in heads], axis=0)
        s = _mm(qs, kv[:, kl], _NT)
        o = _softmax_av(s, sink, kv[:, ATT_KV_DIM:][:, kl])
        for g, h in enumerate(heads):
            o_ref[:, h * HEAD_DIM:(h + 1) * HEAD_DIM] = o[g * n:(g + 1) * n, :]


def _context_attention(pq, pkv, sink, yc, ctx_len):
    blk = lambda w: pl.BlockSpec((ctx_len, w), lambda i: (0, 0))
    return pl.pallas_call(
        _ctx_attn_kernel,
        grid=(1,),
        in_specs=[blk(ATT_DIM), blk(2 * ATT_KV_DIM), pl.BlockSpec((8, LANE), lambda i: (0, 0)),
                  pl.BlockSpec(memory_space=pl.ANY)],
        out_specs=blk(ATT_DIM),
        out_shape=jax.ShapeDtypeStruct(yc.shape, F32),
        input_output_aliases={3: 0},
        compiler_params=_params(("arbitrary",), VMEM_LIMIT),
        name="context_attention",
    )(pq, pkv, sink, yc)


def _mixfin_kernel(h_ref, mod_ref, pconv_ref, prev_ref, next_ref, cw_ref, of_ref, ob_ref, z_ref, ng_ref, yc_ref,
                   wout_ref, *rest, nblk, blk0, with_router):
    if with_router:
        wr1_ref, wr0_ref, x_ref, hx_ref, lg_ref = rest
    else:
        x_ref, hx_ref = rest
    i = pl.program_id(0) + blk0
    row = _mod_row(mod_ref, i == 0)
    pc = pconv_ref[...]
    u = pc[:, CONV_CH:2 * CONV_CH] * pc[:, 2 * CONV_CH:]
    prow, nrow = _halo_rows(prev_ref, next_ref, i, nblk)
    prow = prow[:, CONV_CH:2 * CONV_CH] * prow[:, 2 * CONV_CH:]
    nrow = nrow[:, CONV_CH:2 * CONV_CH] * nrow[:, 2 * CONV_CH:]
    up, un = _shift_rows(u, prow, nrow)
    cw = cw_ref[...]
    ya = pc[:, :CONV_CH] * (up * cw[0:1, :] + u * cw[1:2, :] + un * cw[2:3, :])
    o = of_ref[...] + ob_ref[...]
    ms = _group_sum(o * o, _head_blockdiag(DN_DIM, DN_HEAD_DIM)) * (1.0 / DN_HEAD_DIM)
    z = z_ref[...]
    yb = o * lax.rsqrt(ms + EPS) * ng_ref[...] * (z * _sigmoid(z))
    mix = jnp.concatenate([ya, yb, yc_ref[...]], axis=1)
    x = h_ref[...] + row[:, 2 * D:3 * D] * _mm(mix, wout_ref[...])
    x_ref[...] = x
    hx = _rmsnorm_rows(x) * (1.0 + row[:, 4 * D:5 * D]) + row[:, 3 * D:4 * D]
    hx_ref[...] = hx.astype(BF16)
    if with_router:
        h1, h0 = _split2(hx)
        d = functools.partial(lax.dot_general, dimension_numbers=_NN, preferred_element_type=F32)
        lg_ref[...] = d(h1, wr1_ref[...]) + (d(h0, wr1_ref[...]) + d(h1, wr0_ref[...]))


def _mixer_finish(h, mod, pconv, conv_w, o_f, o_b, pz, norm_g, yc, w_out, ctx_len, with_ctx, router=None):
    t = h.shape[0]
    nblk = t // TM
    blk0 = 0 if with_ctx else ctx_len // TM
    rows = t - blk0 * TM
    r8 = TM // 8
    w = pconv.shape[1]
    cur = lambda i: (i + blk0, 0)
    out_cur = lambda i: (i, 0)
    const = lambda i: (0, 0)
    ng = jnp.tile(norm_g.reshape(1, DN_HEAD_DIM), (1, DN_HEADS))
    in_specs = [
        pl.BlockSpec((TM, D), cur),
        pl.BlockSpec((8, N_MOD * D), const),
        pl.BlockSpec((TM, w), cur),
        pl.BlockSpec((8, w), lambda i: (jnp.maximum((i + blk0) * r8 - 1, 0), 0)),
        pl.BlockSpec((8, w), lambda i: (jnp.minimum((i + blk0 + 1) * r8, t // 8 - 1), 0)),
        pl.BlockSpec((3, CONV_CH), const),
        pl.BlockSpec((TM, DN_DIM), cur),
        pl.BlockSpec((TM, DN_DIM), cur),
        pl.BlockSpec((TM, DN_DIM), cur),
        pl.BlockSpec((1, DN_DIM), const),
        pl.BlockSpec((TM, ATT_DIM), cur),
        pl.BlockSpec((MIX_DIM, D), const),
    ]
    args = [h, mod, pconv, pconv, pconv, conv_w, o_f, o_b, pz, ng, yc, w_out]
    out_specs = [pl.BlockSpec((TM, D), out_cur), pl.BlockSpec((TM, D), out_cur)]
    out_shape = [jax.ShapeDtypeStruct((rows, D), F32), jax.ShapeDtypeStruct((rows, D), BF16)]
    if router is not None:
        in_specs += [pl.BlockSpec((D, LANE), const)] * 2
        args += list(router)
        out_specs.append(pl.BlockSpec((TM, LANE), out_cur))
        out_shape.append(jax.ShapeDtypeStruct((rows, LANE), F32))
    return pl.pallas_call(
        functools.partial(_mixfin_kernel, nblk=nblk, blk0=blk0, with_router=router is not None),
        grid=(rows // TM,),
        in_specs=in_specs,
        out_specs=out_specs,
        out_shape=out_shape,
        compiler_params=_params(("arbitrary",), VMEM_LIMIT),
        name="mixer_finish",
    )(*args)


def _ffn_kernel(hx_ref, x_ref, mod_ref, wg_ref, wu_ref, wd_ref, o_ref):
    row = _mod_row(mod_ref, pl.program_id(0) == 0)
    hx = hx_ref[...]
    acc = jnp.zeros((hx.shape[0], D), F32)
    for f in range(0, D_FF, FF_CHUNK):
        g = _mm(hx, wg_ref[:, f:f + FF_CHUNK])
        u = _mm(hx, wu_ref[:, f:f + FF_CHUNK])
        acc = acc + _mm(g * _sigmoid(g) * u, wd_ref[f:f + FF_CHUNK, :])
    o_ref[...] = x_ref[...] + row[:, 5 * D:6 * D] * acc


def _dense_ffn(hx, x, mod, wg, wu, wd):
    t = x.shape[0]
    const = lambda i: (0, 0)
    once = dict(pipeline_mode=pl.Buffered(1))
    return pl.pallas_call(
        _ffn_kernel,
        grid=(t // TM,),
        in_specs=[
            pl.BlockSpec((TM, D), lambda i: (i, 0)),
            pl.BlockSpec((TM, D), lambda i: (i, 0)),
            pl.BlockSpec((8, N_MOD * D), const),
            pl.BlockSpec((D, D_FF), const, **once),
            pl.BlockSpec((D, D_FF), const, **once),
            pl.BlockSpec((D_FF, D), const, **once),
        ],
        out_specs=pl.BlockSpec((TM, D), lambda i: (i, 0)),
        out_shape=jax.ShapeDtypeStruct((t, D), F32),
        compiler_params=_params(("arbitrary",), VMEM_LIMIT),
        name="dense_ffn",
    )(hx, x, mod, wg, wu, wd)


def _moe_kernel(be_ref, nu_ref, xs_ref, wg_ref, wu_ref, wd_ref, y_ref, acc_ref):
    b = pl.program_id(0)
    f = pl.program_id(1)

    @pl.when(b < nu_ref[0])
    def _():
        xs = xs_ref[...]
        g = _mm(xs, wg_ref[0])
        u = _mm(xs, wu_ref[0])
        part = _mm(g * _sigmoid(g) * u, wd_ref[0])

        @pl.when(f == 0)
        def _():
            acc_ref[...] = part

        @pl.when(f != 0)
        def _():
            acc_ref[...] += part

        @pl.when(f == pl.num_programs(1) - 1)
        def _():
            y_ref[...] = acc_ref[...]

    @pl.when(b >= nu_ref[0])
    def _():
        y_ref[...] = jnp.zeros(y_ref.shape, F32)


def _moe_experts(xs, blk_e, n_used, wg, wu, wd):
    cap = xs.shape[0]
    nblk = cap // MOE_TM
    nf = D_FF_EXPERT // MOE_TF

    def fidx(b, f, nu):
        return jnp.where(b < nu[0], f, nf - 1)

    grid_spec = pltpu.PrefetchScalarGridSpec(
        num_scalar_prefetch=2,
        grid=(nblk, nf),
        in_specs=[
            pl.BlockSpec((MOE_TM, D), lambda b, f, be, nu: (b, 0)),
            pl.BlockSpec((1, D, MOE_TF), lambda b, f, be, nu: (be[b], 0, fidx(b, f, nu))),
            pl.BlockSpec((1, D, MOE_TF), lambda b, f, be, nu: (be[b], 0, fidx(b, f, nu))),
            pl.BlockSpec((1, MOE_TF, D), lambda b, f, be, nu: (be[b], fidx(b, f, nu), 0)),
        ],
        out_specs=pl.BlockSpec((MOE_TM, D), lambda b, f, be, nu: (b, 0)),
        scratch_shapes=[pltpu.VMEM((MOE_TM, D), F32)],
    )
    return pl.pallas_call(
        _moe_kernel,
        grid_spec=grid_spec,
        out_shape=jax.ShapeDtypeStruct((cap, D), F32),
        compiler_params=_params(("arbitrary", "arbitrary"), VMEM_LIMIT),
        name="moe_experts",
    )(blk_e, n_used, xs, wg, wu, wd)


def _moe_route(logits):
    n = logits.shape[0]
    top_logit, top_e = lax.top_k(logits, TOP_K)
    gates = jax.nn.softmax(top_logit, axis=-1)
    a = n * TOP_K
    flat_e = top_e.reshape(a)
    onehot = (flat_e[:, None] == jnp.arange(N_EXPERTS, dtype=flat_e.dtype)[None, :]).astype(jnp.int32)
    rank = jnp.take_along_axis(jnp.cumsum(onehot, axis=0) - onehot, flat_e[:, None], axis=1)[:, 0]
    counts = jnp.sum(onehot, axis=0)
    padded = (counts + MOE_TM - 1) // MOE_TM * MOE_TM
    pad_ends = jnp.cumsum(padded)
    pad_starts = pad_ends - padded
    dest = pad_starts[flat_e] + rank
    cap = a + N_EXPERTS * MOE_TM
    nblk = cap // MOE_TM
    row_tok = jnp.zeros((cap,), jnp.int32).at[dest].set(jnp.arange(a, dtype=jnp.int32) // TOP_K)
    blk_e = jnp.minimum(jnp.searchsorted(pad_ends, jnp.arange(nblk, dtype=jnp.int32) * MOE_TM, side="right"),
                        N_EXPERTS - 1).astype(jnp.int32)
    n_used = (pad_ends[-1] // MOE_TM).astype(jnp.int32).reshape(1)
    last_e = blk_e[jnp.maximum(n_used[0] - 1, 0)]
    blk_e = jnp.where(jnp.arange(nblk) < n_used[0], blk_e, last_e)
    return gates, dest.reshape(n, TOP_K), row_tok, blk_e, n_used


def _final_kernel(x_ref, y0_ref, y1_ref, gt_ref, mod_ref, fg_ref, o_ref):
    mod = mod_ref[...]
    gt = gt_ref[...]
    f = gt[:, 0:1] * y0_ref[...] + gt[:, 1:2] * y1_ref[...]
    x = x_ref[...] + mod[0:1, 5 * D:6 * D] * f
    o_ref[...] = _rmsnorm_rows(x) * fg_ref[...]


def _moe_combine_final(x, y0, y1, gates, mod, final_g):
    n = x.shape[0]
    gt = jnp.zeros((n, LANE), F32).at[:, :TOP_K].set(gates)
    row = lambda i: (i, 0)
    const = lambda i: (0, 0)
    return pl.pallas_call(
        _final_kernel,
        grid=(n // TM,),
        in_specs=[pl.BlockSpec((TM, D), row)] * 3
        + [pl.BlockSpec((TM, LANE), row), pl.BlockSpec((8, N_MOD * D), const), pl.BlockSpec((1, D), const)],
        out_specs=pl.BlockSpec((TM, D), row),
        out_shape=jax.ShapeDtypeStruct((n, D), F32),
        compiler_params=_params(("arbitrary",), VMEM_LIMIT),
        name="moe_combine_final",
    )(x, y0, y1, gt, mod, final_g.reshape(1, D))


def _rope_tables(n):
    pos = jnp.arange(n, dtype=jnp.int32)
    r = (pos // GRID_W).astype(F32)
    col = (pos % GRID_W).astype(F32)
    inv = ROPE_BASE ** (-jnp.arange(0, AXIS_DIM, 2, dtype=F32) / AXIS_DIM)
    ang = jnp.concatenate([r[:, None] * inv, r[:, None] * inv, col[:, None] * inv, col[:, None] * inv], axis=1)
    sign = jnp.tile(jnp.concatenate([-jnp.ones((AXIS_DIM // 2,), F32), jnp.ones((AXIS_DIM // 2,), F32)]), 2)
    cos = jnp.tile(jnp.cos(ang), (1, LANE // HEAD_DIM))
    sin = jnp.tile(jnp.sin(ang) * sign, (1, LANE // HEAD_DIM))
    return cos, sin


def _prep_w_in(w):
    main = jnp.concatenate([w[:, :_C_A], w[:, _C_Q:_C_END]], axis=1).astype(BF16)
    wab = jnp.zeros((D, LANE), F32).at[:, :N_AB].set(w[:, _C_A:_C_Q])
    wab1 = wab.astype(BF16)
    wab0 = (wab - wab1.astype(F32)).astype(BF16)
    return main, wab1, wab0


def kernel(x, c, ctx, c_ctx, w_mod, b_mod, w_in, w_out, conv_w, dn_conv_w, dn_a_log, dn_dt_bias, dn_norm_g,
           attn_sink, ffn_w_gate, ffn_w_up, ffn_w_down, moe_router, moe_w_gate, moe_w_up, moe_w_down,
           final_norm_g):
    bsz, n, d = x.shape
    ctx_len = ctx.shape[1]
    depth = w_in.shape[0]
    assert bsz == 1 and d == D and ctx_len == TM and n % TM == 0 and n % GRID_W == 0
    cos, sin = _rope_tables(n)
    mods = _mod_vectors(c, c_ctx, w_mod, b_mod)
    h = jnp.concatenate([ctx[0], x[0]], axis=0)
    for layer in range(depth):
        last = layer == depth - 1
        mod = mods[layer]
        w_main, wab1, wab0 = _prep_w_in(w_in[layer])
        pconv, pqkv, pz, pq, pkv, pab = _in_proj(h, mod, w_main, wab1, wab0)
        qn, kn, vv, gb = _dn_prep(pqkv, pab, dn_conv_w[layer], dn_a_log[layer], dn_dt_bias[layer])
        o_f, o_b = _delta_net(qn, kn, vv, gb, ctx_len)
        sink = jnp.zeros((8, LANE), F32).at[0, :ATT_HEADS].set(attn_sink[layer])
        yc = _window_attention(pq, pkv, cos, sin, sink, ctx_len)
        if not last:
            yc = _context_attention(pq, pkv, sink, yc, ctx_len)
        router = None
        if layer % 2 == 1:
            wr = jnp.zeros((D, LANE), F32).at[:, :N_EXPERTS].set(moe_router[layer // 2])
            wr1 = wr.astype(BF16)
            router = (wr1, (wr - wr1.astype(F32)).astype(BF16))
        outs = _mixer_finish(h, mod, pconv, conv_w[layer], o_f, o_b, pz, dn_norm_g[layer], yc,
                             w_out[layer].astype(BF16), ctx_len, with_ctx=not last, router=router)
        if layer % 2 == 0:
            assert not last
            x1, hx = outs
            i = layer // 2
            h = _dense_ffn(hx, x1, mod, ffn_w_gate[i].astype(BF16), ffn_w_up[i].astype(BF16),
                           ffn_w_down[i].astype(BF16))
        else:
            assert last
            x1, hx, logits = outs
            i = layer // 2
            gates, dest, row_tok, blk_e, n_used = _moe_route(logits[:, :N_EXPERTS])
            xs = jnp.take(hx, row_tok, axis=0)
            y = _moe_experts(xs, blk_e, n_used, moe_w_gate[i].astype(BF16), moe_w_up[i].astype(BF16),
                             moe_w_down[i].astype(BF16))
            y0 = jnp.take(y, dest[:, 0], axis=0)
            y1 = jnp.take(y, dest[:, 1], axis=0)
            h = _moe_combine_final(x1, y0, y1, gates, mod, final_norm_g)
    return h.reshape(bsz, n, d)
```

```python
import functools

import jax
import jax.numpy as jnp
from jax import lax
from jax.experimental import pallas as pl
from jax.experimental.pallas import tpu as pltpu

F32 = jnp.float32
BF16 = jnp.bfloat16

D = 1024
N_MOD = 6
EPS = 1e-6
NEG = -1e30
GRID_W = 64

CONV_CH = 256
DN_HEADS = 6
DN_HEAD_DIM = 64
DN_DIM = DN_HEADS * DN_HEAD_DIM
DN_CHUNK = 64
DN_SUB = 16
DN_STEP_CHUNKS = 4
ATT_HEADS = 6
ATT_KV_HEADS = 2
ATT_GROUP = ATT_HEADS // ATT_KV_HEADS
HEAD_DIM = 64
ATT_DIM = ATT_HEADS * HEAD_DIM
ATT_KV_DIM = ATT_KV_HEADS * HEAD_DIM
ATT_BLOCK = 128
ROPE_BASE = 10000.0
AXIS_DIM = HEAD_DIM // 2
MIX_DIM = CONV_CH + DN_DIM + ATT_DIM

D_FF = 2816
N_EXPERTS = 8
TOP_K = 2
D_FF_EXPERT = 3584

TM = 256
FF_CHUNK = 256
MOE_TM = 512
MOE_TF = 512
LANE = 128
VMEM_LIMIT = 56 * 1024 * 1024

_C_QKV = 3 * CONV_CH
_C_Z = _C_QKV + 3 * DN_DIM
_C_A = _C_Z + DN_DIM
_C_Q = _C_A + 4 * DN_HEADS
_C_K = _C_Q + ATT_DIM
_C_V = _C_K + ATT_KV_DIM
_C_END = _C_V + ATT_KV_DIM
N_AB = 4 * DN_HEADS


def _params(sem=None, vmem=None):
    kw = {}
    if sem is not None:
        kw["dimension_semantics"] = sem
    if vmem is not None:
        kw["vmem_limit_bytes"] = vmem
    return pltpu.CompilerParams(**kw)


def _split2(a):
    hi = a.astype(BF16)
    lo = (a - hi.astype(F32)).astype(BF16)
    return hi, lo


def _split3(a):
    hi = a.astype(BF16)
    r = a - hi.astype(F32)
    mid = r.astype(BF16)
    lo = (r - mid.astype(F32)).astype(BF16)
    return hi, mid, lo


_NN = (((1,), (0,)), ((), ()))
_NT = (((1,), (1,)), ((), ()))
_TN = (((0,), (0,)), ((), ()))


def _mm(a, b, dims=_NN):
    return lax.dot_general(a.astype(BF16), b.astype(BF16), dims, preferred_element_type=F32)


def _mm3(a, b, dims=_NN):
    a1, a0 = _split2(a)
    b1, b0 = _split2(b)
    d = functools.partial(lax.dot_general, dimension_numbers=dims, preferred_element_type=F32)
    return d(a1, b1) + (d(a1, b0) + d(a0, b1))


_BNN = (((2,), (1,)), ((0,), (0,)))
_BNT = (((2,), (2,)), ((0,), (0,)))
_BTN = (((1,), (1,)), ((0,), (0,)))


def _bmm(a, b, dims=_BNN):
    return lax.dot_general(a.astype(BF16), b.astype(BF16), dims, preferred_element_type=F32)


def _bmm3(a, b, dims=_BNN):
    return _mm3(a, b, dims)


def _sigmoid(x):
    return 1.0 / (1.0 + jnp.exp(-x))


def _softplus(x):
    return jnp.maximum(x, 0.0) + jnp.log1p(jnp.exp(-jnp.abs(x)))


def _mod_row(mod_ref, is_ctx):
    mod = mod_ref[...]
    return jnp.where(is_ctx, mod[1:2, :], mod[0:1, :])


def _rmsnorm_rows(x):
    return x * lax.rsqrt(jnp.mean(x * x, axis=-1, keepdims=True) + EPS)


def _shift_rows(u, prow, nrow):
    n = u.shape[0]
    rid = lax.broadcasted_iota(jnp.int32, u.shape, 0)
    up = jnp.where(rid == 0, prow, pltpu.roll(u, 1, 0))
    un = jnp.where(rid == n - 1, nrow, pltpu.roll(u, n - 1, 0))
    return up, un


def _same_group(shape, group):
    sh = group.bit_length() - 1
    assert 1 << sh == group
    return (lax.broadcasted_iota(jnp.int32, shape, 0) >> sh) == (lax.broadcasted_iota(jnp.int32, shape, 1) >> sh)


def _head_blockdiag(n, group):
    return jnp.where(_same_group((n, n), group), 1.0, 0.0).astype(BF16)


def _group_sum(t, bd):
    hi, lo = _split2(t)
    d = functools.partial(lax.dot_general, dimension_numbers=_NN, preferred_element_type=F32)
    return d(hi, bd) + d(lo, bd)


MOD_TN = 1536


def _mod_kernel(s_ref, w_ref, b_ref, o_ref):
    s = s_ref[...]
    s = s * _sigmoid(s)
    o_ref[0] = _mm3(s, w_ref[0]) + b_ref[0]


def _mod_vectors(c, c_ctx, w_mod, b_mod):
    depth = w_mod.shape[0]
    s = jnp.zeros((8, D), F32).at[0].set(c[0]).at[1].set(c_ctx)
    return pl.pallas_call(
        _mod_kernel,
        grid=(depth, N_MOD * D // MOD_TN),
        in_specs=[
            pl.BlockSpec((8, D), lambda l, j: (0, 0)),
            pl.BlockSpec((1, D, MOD_TN), lambda l, j: (l, 0, j)),
            pl.BlockSpec((1, 1, MOD_TN), lambda l, j: (l, 0, j)),
        ],
        out_specs=pl.BlockSpec((1, 8, MOD_TN), lambda l, j: (l, 0, j)),
        out_shape=jax.ShapeDtypeStruct((depth, 8, N_MOD * D), F32),
        compiler_params=_params(("arbitrary", "arbitrary"), VMEM_LIMIT),
        name="mod_vectors",
    )(s, w_mod, b_mod.reshape(depth, 1, N_MOD * D))


def _in_kernel(h_ref, mod_ref, w_ref, wab1_ref, wab0_ref,
               pconv_ref, pqkv_ref, pz_ref, pq_ref, pkv_ref, pab_ref):
    row = _mod_row(mod_ref, pl.program_id(0) == 0)
    hm = _rmsnorm_rows(h_ref[...]) * (1.0 + row[:, D:2 * D]) + row[:, 0:D]
    h1, h0 = _split2(hm)
    d = functools.partial(lax.dot_general, dimension_numbers=_NN, preferred_element_type=F32)
    off = 0
    for ref in (pconv_ref, pqkv_ref, pz_ref, pq_ref, pkv_ref):
        w = ref.shape[1]
        ref[...] = d(h1, w_ref[:, off:off + w])
        off += w
    pab_ref[...] = d(h1, wab1_ref[...]) + (d(h0, wab1_ref[...]) + d(h1, wab0_ref[...]))


def _in_proj(h, mod, w_main, wab1, wab0):
    t = h.shape[0]
    widths = (3 * CONV_CH, 3 * DN_DIM, DN_DIM, ATT_DIM, 2 * ATT_KV_DIM, LANE)
    const = lambda i: (0, 0)
    return pl.pallas_call(
        _in_kernel,
        grid=(t // TM,),
        in_specs=[
            pl.BlockSpec((TM, D), lambda i: (i, 0)),
            pl.BlockSpec((8, N_MOD * D), const),
            pl.BlockSpec(w_main.shape, const),
            pl.BlockSpec(wab1.shape, const),
            pl.BlockSpec(wab0.shape, const),
        ],
        out_specs=[pl.BlockSpec((TM, w), lambda i: (i, 0)) for w in widths],
        out_shape=[jax.ShapeDtypeStruct((t, w), F32) for w in widths],
        compiler_params=_params(("arbitrary",), VMEM_LIMIT),
        name="in_proj",
    )(h, mod, w_main, wab1, wab0)


def _halo_rows(prev_ref, next_ref, i, nblk):
    pvalid = jnp.logical_and(i != 0, i != 1)
    nvalid = jnp.logical_and(i != 0, i != nblk - 1)
    prow = jnp.where(pvalid, prev_ref[7:8, :], 0.0)
    nrow = jnp.where(nvalid, next_ref[0:1, :], 0.0)
    return prow, nrow


def _dnprep_kernel(qkv_ref, prev_ref, next_ref, ab_ref, cw_ref, alog_ref, dtb_ref,
                   q_ref, k_ref, v_ref, gb_ref, *, nblk):
    i = pl.program_id(0)
    u = qkv_ref[...]
    prow, nrow = _halo_rows(prev_ref, next_ref, i, nblk)
    up, un = _shift_rows(u, prow, nrow)
    cw = cw_ref[...]
    y = up * cw[0:1, :] + u * cw[1:2, :] + un * cw[2:3, :]
    y = y * _sigmoid(y)
    q = y[:, 0:DN_DIM]
    k = y[:, DN_DIM:2 * DN_DIM]
    bd = _head_blockdiag(DN_DIM, DN_HEAD_DIM)
    q_ref[...] = q * lax.rsqrt(_group_sum(q * q, bd) + 1e-6) * (DN_HEAD_DIM ** -0.5)
    k_ref[...] = k * lax.rsqrt(_group_sum(k * k, bd) + 1e-6)
    v_ref[...] = y[:, 2 * DN_DIM:3 * DN_DIM]
    ab = ab_ref[...]
    g = -jnp.exp(alog_ref[...]) * _softplus(ab + dtb_ref[...])
    lane = lax.broadcasted_iota(jnp.int32, ab.shape, 1)
    gb_ref[...] = jnp.where(lane < 2 * DN_HEADS, g, _sigmoid(ab))


def _dn_prep(pqkv, pab, dn_conv_w, a_log, dt_bias):
    t, w = pqkv.shape
    nblk = t // TM
    alog = jnp.zeros((1, LANE), F32).at[0, :2 * DN_HEADS].set(a_log.reshape(-1))
    dtb = jnp.zeros((1, LANE), F32).at[0, :2 * DN_HEADS].set(dt_bias.reshape(-1))
    r8 = TM // 8
    const = lambda i: (0, 0)
    return pl.pallas_call(
        functools.partial(_dnprep_kernel, nblk=nblk),
        grid=(nblk,),
        in_specs=[
            pl.BlockSpec((TM, w), lambda i: (i, 0)),
            pl.BlockSpec((8, w), lambda i: (jnp.maximum(i * r8 - 1, 0), 0)),
            pl.BlockSpec((8, w), lambda i: (jnp.minimum((i + 1) * r8, t // 8 - 1), 0)),
            pl.BlockSpec((TM, LANE), lambda i: (i, 0)),
            pl.BlockSpec((3, w), const),
            pl.BlockSpec((1, LANE), const),
            pl.BlockSpec((1, LANE), const),
        ],
        out_specs=[pl.BlockSpec((TM, DN_DIM), lambda i: (i, 0))] * 3
        + [pl.BlockSpec((TM, LANE), lambda i: (i, 0))],
        out_shape=[jax.ShapeDtypeStruct((t, DN_DIM), F32)] * 3 + [jax.ShapeDtypeStruct((t, LANE), F32)],
        compiler_params=_params(("arbitrary",), VMEM_LIMIT),
        name="dn_prep",
    )(pqkv, pqkv, pqkv, pab, dn_conv_w, alog, dtb)


def _dn_chunk(rev, q_ref, k_ref, v_ref, gb_ref, o_ref, s_ref):
    c_ = DN_CHUNK
    ri = lax.broadcasted_iota(jnp.int32, (c_, c_), 0)
    ci = lax.broadcasted_iota(jnp.int32, (c_, c_), 1)
    incl = (ri <= ci) if rev else (ri >= ci)
    strict = (ri < ci) if rev else (ri > ci)
    same_sub = _same_group((c_, c_), DN_SUB)
    eye = jnp.where(ri == ci, 1.0, 0.0)
    tri = jnp.where(incl, 1.0, 0.0).astype(BF16)
    last = 0 if rev else c_ - 1

    nchunks = q_ref.shape[0] // c_
    nh = DN_HEADS
    col0 = nh if rev else 0
    rows = lambda g: slice(g * c_, (g + 1) * c_)
    lanes = lambda h: slice(h * DN_HEAD_DIM, (h + 1) * DN_HEAD_DIM)

    def stack(fn):
        return jnp.stack([fn(g, h) for g in range(nchunks) for h in range(nh)])

    gb = gb_ref[...]
    gcs = [_cumsum_rows(tri, gb[rows(g), :]) for g in range(nchunks)]
    gcts = [gc.T for gc in gcs]
    q = stack(lambda g, h: q_ref[rows(g), lanes(h)])
    k = stack(lambda g, h: k_ref[rows(g), lanes(h)])
    v = stack(lambda g, h: v_ref[rows(g), lanes(h)])
    gcol = stack(lambda g, h: gcs[g][:, col0 + h:col0 + h + 1])
    grow = stack(lambda g, h: gcts[g][col0 + h:col0 + h + 1, :])
    beta = stack(lambda g, h: gb[rows(g), 2 * nh + col0 + h:2 * nh + col0 + h + 1])
    glast = gcol[:, last:last + 1, :]
    decay = jnp.where(incl, jnp.exp(jnp.where(incl, gcol - grow, 0.0)), 0.0)
    eg = jnp.exp(gcol)
    kb = k * beta
    a = jnp.where(strict, _bmm3(kb, k, _BNT) * decay, 0.0)
    qk = jnp.where(incl, _bmm(q, k, _BNT) * decay, 0.0)
    ad = jnp.where(same_sub, a, 0.0)
    ao = a - ad
    p = eye - ad
    n2 = _bmm3(ad, ad)
    p = p + _bmm3(p, n2)
    n4 = _bmm3(n2, n2)
    p = p + _bmm3(p, n4)
    n8 = _bmm3(n4, n4)
    dinv = p + _bmm3(p, n8)
    m = _bmm3(dinv, ao)
    m2 = _bmm3(m, m)
    sols = []
    for r in (v * beta, kb * eg):
        y = _bmm3(dinv, r)
        z = y + _bmm3(m2, y)
        sols.append(z - _bmm3(m, z))
    u, w = sols
    qg = q * eg
    kd = k * jnp.exp(glast - gcol)
    gl = jnp.exp(glast)
    s = s_ref[col0:col0 + nh]
    for g in (reversed(range(nchunks)) if rev else range(nchunks)):
        b = slice(g * nh, (g + 1) * nh)
        v_new = u[b] - _bmm(w[b], s)
        o = _bmm(qg[b], s) + _bmm(qk[b], v_new)
        s = s * gl[b] + _bmm(kd[b], v_new, _BTN)
        for h in range(nh):
            o_ref[rows(g), lanes(h)] = o[h]
    s_ref[col0:col0 + nh] = s


def _cumsum_rows(tri_bf16, g):
    g2, g1, g0 = _split3(g)
    d = functools.partial(lax.dot_general, dimension_numbers=_NN, preferred_element_type=F32)
    return d(tri_bf16, g2) + (d(tri_bf16, g1) + d(tri_bf16, g0))


def _dn_kernel(qf, kf, vf, gf, qb, kb, vb, gbb, of_ref, ob_ref, s_ref):
    @pl.when(pl.program_id(0) == 0)
    def _():
        s_ref[...] = jnp.zeros(s_ref.shape, F32)

    _dn_chunk(False, qf, kf, vf, gf, of_ref, s_ref)
    _dn_chunk(True, qb, kb, vb, gbb, ob_ref, s_ref)


def _delta_net(q, k, v, gb, ctx_len):
    t = q.shape[0]
    rows = DN_STEP_CHUNKS * DN_CHUNK
    assert ctx_len == rows and t % rows == 0
    nstep = t // rows

    def fwd(s):
        return (s, 0)

    def bwd(s):
        return (jnp.where(s == 0, 0, nstep - s), 0)

    wide = lambda im: pl.BlockSpec((rows, DN_DIM), im)
    narrow = lambda im: pl.BlockSpec((rows, LANE), im)
    return pl.pallas_call(
        _dn_kernel,
        grid=(nstep,),
        in_specs=[wide(fwd), wide(fwd), wide(fwd), narrow(fwd), wide(bwd), wide(bwd), wide(bwd), narrow(bwd)],
        out_specs=[wide(fwd), wide(bwd)],
        out_shape=[jax.ShapeDtypeStruct((t, DN_DIM), F32)] * 2,
        scratch_shapes=[pltpu.VMEM((2 * DN_HEADS, DN_HEAD_DIM, DN_HEAD_DIM), F32)],
        compiler_params=_params(("arbitrary",), VMEM_LIMIT),
        name="delta_net",
    )(q, k, v, gb, q, k, v, gb)


def _rope(x, cos, sin):
    w = x.shape[1]
    lane = lax.broadcasted_iota(jnp.int32, x.shape, 1)
    first_half = (lane & (AXIS_DIM - 1)) < (AXIS_DIM // 2)
    swapped = jnp.where(first_half, pltpu.roll(x, w - AXIS_DIM // 2, 1), pltpu.roll(x, AXIS_DIM // 2, 1))
    return x * cos + swapped * sin


def _softmax_av(s, sink, vals):
    m = jnp.maximum(jnp.max(s, axis=-1, keepdims=True), sink)
    p = jnp.exp(s - m)
    denom = jnp.sum(p, axis=-1, keepdims=True) + jnp.exp(sink - m)
    return _mm(p, vals) / denom


def _attn_kernel(q_ref, kp_ref, kc_ref, kn_ref, kctx_ref, cq_ref, sq_ref, cp_ref, sp_ref, cn_ref, sn_ref,
                 sink_ref, o_ref, *, nb):
    i = pl.program_id(0)
    b = ATT_BLOCK
    cq = cq_ref[...]
    sq = sq_ref[...]
    q = q_ref[...]
    q = jnp.concatenate([_rope(q[:, j * LANE:(j + 1) * LANE], cq, sq) for j in range(ATT_DIM // LANE)], axis=1)
    q = q * (HEAD_DIM ** -0.5)
    kvp, kvc, kvn, kvx = kp_ref[...], kc_ref[...], kn_ref[...], kctx_ref[...]
    kp = _rope(kvp[:, :ATT_KV_DIM], cp_ref[...], sp_ref[...])
    kc = _rope(kvc[:, :ATT_KV_DIM], cq, sq)
    kn = _rope(kvn[:, :ATT_KV_DIM], cn_ref[...], sn_ref[...])
    keys = jnp.concatenate([kp, kc, kn, kvx[:, :ATT_KV_DIM]], axis=0)
    vals = jnp.concatenate([kvp[:, ATT_KV_DIM:], kvc[:, ATT_KV_DIM:], kvn[:, ATT_KV_DIM:], kvx[:, ATT_KV_DIM:]],
                           axis=0)
    nk = keys.shape[0]
    rows = ATT_GROUP * b
    c = lax.broadcasted_iota(jnp.int32, (1, nk), 1)
    far = 4 * nk
    ccol = jnp.where(c >= 3 * b, b + ((c - 3 * b) & (b - 1)), c)
    ccol = jnp.where(jnp.logical_and(c < b, i == 0), -far, ccol)
    ccol = jnp.where(jnp.logical_and(jnp.logical_and(c >= 2 * b, c < 3 * b), i == nb - 1), far, ccol)
    r = lax.broadcasted_iota(jnp.int32, (rows, 1), 0) & (b - 1)
    valid = lax.bitcast_convert_type(ccol - r, jnp.uint32) <= jnp.uint32(2 * b)
    sink_all = sink_ref[...]
    for kvh in range(ATT_KV_HEADS):
        kl = slice(kvh * HEAD_DIM, (kvh + 1) * HEAD_DIM)
        heads = range(kvh * ATT_GROUP, (kvh + 1) * ATT_GROUP)
        qs = jnp.concatenate([q[:, h * HEAD_DIM:(h + 1) * HEAD_DIM] for h in heads], axis=0)
        sink = jnp.concatenate([jnp.broadcast_to(sink_all[0:1, h:h + 1], (b, 1)) for h in heads], axis=0)
        s = jnp.where(valid, _mm(qs, keys[:, kl], _NT), NEG)
        o = _softmax_av(s, sink, vals[:, kl])
        for g, h in enumerate(heads):
            o_ref[:, h * HEAD_DIM:(h + 1) * HEAD_DIM] = o[g * b:(g + 1) * b, :]


def _window_attention(pq, pkv, cos, sin, sink, ctx_len):
    t = pq.shape[0]
    n = t - ctx_len
    nb = n // ATT_BLOCK
    off = ctx_len // ATT_BLOCK
    kv = lambda im: pl.BlockSpec((ATT_BLOCK, 2 * ATT_KV_DIM), im)
    tab = lambda im: pl.BlockSpec((ATT_BLOCK, LANE), im)
    cur = lambda i: (i, 0)
    prv = lambda i: (jnp.maximum(i - 1, 0), 0)
    nxt = lambda i: (jnp.minimum(i + 1, nb - 1), 0)
    shift = lambda im: (lambda i: (im(i)[0] + off, 0))
    return pl.pallas_call(
        functools.partial(_attn_kernel, nb=nb),
        grid=(nb,),
        in_specs=[
            pl.BlockSpec((ATT_BLOCK, ATT_DIM), shift(cur)),
            kv(shift(prv)), kv(shift(cur)), kv(shift(nxt)),
            pl.BlockSpec((ctx_len, 2 * ATT_KV_DIM), lambda i: (0, 0)),
            tab(cur), tab(cur), tab(prv), tab(prv), tab(nxt), tab(nxt),
            pl.BlockSpec((8, LANE), lambda i: (0, 0)),
        ],
        out_specs=pl.BlockSpec((ATT_BLOCK, ATT_DIM), shift(cur)),
        out_shape=jax.ShapeDtypeStruct((t, ATT_DIM), F32),
        compiler_params=_params(("arbitrary",), VMEM_LIMIT),
        name="window_attention",
    )(pq, pkv, pkv, pkv, pkv, cos, sin, cos, sin, cos, sin, sink)


def _ctx_attn_kernel(q_ref, kv_ref, sink_ref, prev_ref, o_ref):
    del prev_ref
    q = q_ref[...] * (HEAD_DIM ** -0.5)
    kv = kv_ref[...]
    n = q.shape[0]
    sink_all = sink_ref[...]
    for kvh in range(ATT_KV_HEADS):
        kl = slice(kvh * HEAD_DIM, (kvh + 1) * HEAD_DIM)
        heads = range(kvh * ATT_GROUP, (kvh + 1) * ATT_GROUP)
        qs = jnp.concatenate([q[:, h * HEAD_DIM:(h + 1) * HEAD_DIM] for h in heads], axis=0)
        sink = jnp.concatenate([jnp.broadcast_to(sink_all[0:1, h:h + 1], (n, 1)) for h in heads], axis=0)
        s = _mm(qs, kv[:, kl], _NT)
        o = _softmax_av(s, sink, kv[:, ATT_KV_DIM:][:, kl])
        for g, h in enumerate(heads):
            o_ref[:, h * HEAD_DIM:(h + 1) * HEAD_DIM] = o[g * n:(g + 1) * n, :]


def _context_attention(pq, pkv, sink, yc, ctx_len):
    blk = lambda w: pl.BlockSpec((ctx_len, w), lambda i: (0, 0))
    return pl.pallas_call(
        _ctx_attn_kernel,
        grid=(1,),
        in_specs=[blk(ATT_DIM), blk(2 * ATT_KV_DIM), pl.BlockSpec((8, LANE), lambda i: (0, 0)),
                  pl.BlockSpec(memory_space=pl.ANY)],
        out_specs=blk(ATT_DIM),
        out_shape=jax.ShapeDtypeStruct(yc.shape, F32),
        input_output_aliases={3: 0},
        compiler_params=_params(("arbitrary",), VMEM_LIMIT),
        name="context_attention",
    )(pq, pkv, sink, yc)


def _mixfin_kernel(h_ref, mod_ref, pconv_ref, prev_ref, next_ref, cw_ref, of_ref, ob_ref, z_ref, ng_ref, yc_ref,
                   wout_ref, *rest, nblk, blk0, with_router):
    if with_router:
        wr1_ref, wr0_ref, x_ref, hx_ref, lg_ref = rest
    else:
        x_ref, hx_ref = rest
    i = pl.program_id(0) + blk0
    row = _mod_row(mod_ref, i == 0)
    pc = pconv_ref[...]
    u = pc[:, CONV_CH:2 * CONV_CH] * pc[:, 2 * CONV_CH:]
    prow, nrow = _halo_rows(prev_ref, next_ref, i, nblk)
    prow = prow[:, CONV_CH:2 * CONV_CH] * prow[:, 2 * CONV_CH:]
    nrow = nrow[:, CONV_CH:2 * CONV_CH] * nrow[:, 2 * CONV_CH:]
    up, un = _shift_rows(u, prow, nrow)
    cw = cw_ref[...]
    ya = pc[:, :CONV_CH] * (up * cw[0:1, :] + u * cw[1:2, :] + un * cw[2:3, :])
    o = of_ref[...] + ob_ref[...]
    ms = _group_sum(o * o, _head_blockdiag(DN_DIM, DN_HEAD_DIM)) * (1.0 / DN_HEAD_DIM)
    z = z_ref[...]
    yb = o * lax.rsqrt(ms + EPS) * ng_ref[...] * (z * _sigmoid(z))
    mix = jnp.concatenate([ya, yb, yc_ref[...]], axis=1)
    x = h_ref[...] + row[:, 2 * D:3 * D] * _mm(mix, wout_ref[...])
    x_ref[...] = x
    hx = _rmsnorm_rows(x) * (1.0 + row[:, 4 * D:5 * D]) + row[:, 3 * D:4 * D]
    hx_ref[...] = hx.astype(BF16)
    if with_router:
        h1, h0 = _split2(hx)
        d = functools.partial(lax.dot_general, dimension_numbers=_NN, preferred_element_type=F32)
        lg_ref[...] = d(h1, wr1_ref[...]) + (d(h0, wr1_ref[...]) + d(h1, wr0_ref[...]))


def _mixer_finish(h, mod, pconv, conv_w, o_f, o_b, pz, norm_g, yc, w_out, ctx_len, with_ctx, router=None):
    t = h.shape[0]
    nblk = t // TM
    blk0 = 0 if with_ctx else ctx_len // TM
    rows = t - blk0 * TM
    r8 = TM // 8
    w = pconv.shape[1]
    cur = lambda i: (i + blk0, 0)
    out_cur = lambda i: (i, 0)
    const = lambda i: (0, 0)
    ng = jnp.tile(norm_g.reshape(1, DN_HEAD_DIM), (1, DN_HEADS))
    in_specs = [
        pl.BlockSpec((TM, D), cur),
        pl.BlockSpec((8, N_MOD * D), const),
        pl.BlockSpec((TM, w), cur),
        pl.BlockSpec((8, w), lambda i: (jnp.maximum((i + blk0) * r8 - 1, 0), 0)),
        pl.BlockSpec((8, w), lambda i: (jnp.minimum((i + blk0 + 1) * r8, t // 8 - 1), 0)),
        pl.BlockSpec((3, CONV_CH), const),
        pl.BlockSpec((TM, DN_DIM), cur),
        pl.BlockSpec((TM, DN_DIM), cur),
        pl.BlockSpec((TM, DN_DIM), cur),
        pl.BlockSpec((1, DN_DIM), const),
        pl.BlockSpec((TM, ATT_DIM), cur),
        pl.BlockSpec((MIX_DIM, D), const),
    ]
    args = [h, mod, pconv, pconv, pconv, conv_w, o_f, o_b, pz, ng, yc, w_out]
    out_specs = [pl.BlockSpec((TM, D), out_cur), pl.BlockSpec((TM, D), out_cur)]
    out_shape = [jax.ShapeDtypeStruct((rows, D), F32), jax.ShapeDtypeStruct((rows, D), BF16)]
    if router is not None:
        in_specs += [pl.BlockSpec((D, LANE), const)] * 2
        args += list(router)
        out_specs.append(pl.BlockSpec((TM, LANE), out_cur))
        out_shape.append(jax.ShapeDtypeStruct((rows, LANE), F32))
    return pl.pallas_call(
        functools.partial(_mixfin_kernel, nblk=nblk, blk0=blk0, with_router=router is not None),
        grid=(rows // TM,),
        in_specs=in_specs,
        out_specs=out_specs,
        out_shape=out_shape,
        compiler_params=_params(("arbitrary",), VMEM_LIMIT),
        name="mixer_finish",
    )(*args)


def _ffn_kernel(hx_ref, x_ref, mod_ref, wg_ref, wu_ref, wd_ref, o_ref):
    row = _mod_row(mod_ref, pl.program_id(0) == 0)
    hx = hx_ref[...]
    acc = jnp.zeros((hx.shape[0], D), F32)
    for f in range(0, D_FF, FF_CHUNK):
        g = _mm(hx, wg_ref[:, f:f + FF_CHUNK])
        u = _mm(hx, wu_ref[:, f:f + FF_CHUNK])
        acc = acc + _mm(g * _sigmoid(g) * u, wd_ref[f:f + FF_CHUNK, :])
    o_ref[...] = x_ref[...] + row[:, 5 * D:6 * D] * acc


def _dense_ffn(hx, x, mod, wg, wu, wd):
    t = x.shape[0]
    const = lambda i: (0, 0)
    once = dict(pipeline_mode=pl.Buffered(1))
    return pl.pallas_call(
        _ffn_kernel,
        grid=(t // TM,),
        in_specs=[
            pl.BlockSpec((TM, D), lambda i: (i, 0)),
            pl.BlockSpec((TM, D), lambda i: (i, 0)),
            pl.BlockSpec((8, N_MOD * D), const),
            pl.BlockSpec((D, D_FF), const, **once),
            pl.BlockSpec((D, D_FF), const, **once),
            pl.BlockSpec((D_FF, D), const, **once),
        ],
        out_specs=pl.BlockSpec((TM, D), lambda i: (i, 0)),
        out_shape=jax.ShapeDtypeStruct((t, D), F32),
        compiler_params=_params(("arbitrary",), VMEM_LIMIT),
        name="dense_ffn",
    )(hx, x, mod, wg, wu, wd)


def _moe_kernel(be_ref, nu_ref, xs_ref, wg_ref, wu_ref, wd_ref, y_ref, acc_ref):
    b = pl.program_id(0)
    f = pl.program_id(1)

    @pl.when(b < nu_ref[0])
    def _():
        xs = xs_ref[...]
        g = _mm(xs, wg_ref[0])
        u = _mm(xs, wu_ref[0])
        part = _mm(g * _sigmoid(g) * u, wd_ref[0])

        @pl.when(f == 0)
        def _():
            acc_ref[...] = part

        @pl.when(f != 0)
        def _():
            acc_ref[...] += part

        @pl.when(f == pl.num_programs(1) - 1)
        def _():
            y_ref[...] = acc_ref[...]

    @pl.when(b >= nu_ref[0])
    def _():
        y_ref[...] = jnp.zeros(y_ref.shape, F32)


def _moe_experts(xs, blk_e, n_used, wg, wu, wd):
    cap = xs.shape[0]
    nblk = cap // MOE_TM
    nf = D_FF_EXPERT // MOE_TF

    def fidx(b, f, nu):
        return jnp.where(b < nu[0], f, nf - 1)

    grid_spec = pltpu.PrefetchScalarGridSpec(
        num_scalar_prefetch=2,
        grid=(nblk, nf),
        in_specs=[
            pl.BlockSpec((MOE_TM, D), lambda b, f, be, nu: (b, 0)),
            pl.BlockSpec((1, D, MOE_TF), lambda b, f, be, nu: (be[b], 0, fidx(b, f, nu))),
            pl.BlockSpec((1, D, MOE_TF), lambda b, f, be, nu: (be[b], 0, fidx(b, f, nu))),
            pl.BlockSpec((1, MOE_TF, D), lambda b, f, be, nu: (be[b], fidx(b, f, nu), 0)),
        ],
        out_specs=pl.BlockSpec((MOE_TM, D), lambda b, f, be, nu: (b, 0)),
        scratch_shapes=[pltpu.VMEM((MOE_TM, D), F32)],
    )
    return pl.pallas_call(
        _moe_kernel,
        grid_spec=grid_spec,
        out_shape=jax.ShapeDtypeStruct((cap, D), F32),
        compiler_params=_params(("arbitrary", "arbitrary"), VMEM_LIMIT),
        name="moe_experts",
    )(blk_e, n_used, xs, wg, wu, wd)


def _moe_route(logits):
    n = logits.shape[0]
    assert TOP_K == 2
    ids = jnp.arange(N_EXPERTS, dtype=jnp.int32)[None, :]
    m1 = jnp.max(logits, axis=-1, keepdims=True)
    i1 = jnp.min(jnp.where(logits == m1, ids, N_EXPERTS), axis=-1, keepdims=True)
    rest = jnp.where(ids == i1, -jnp.inf, logits)
    m2 = jnp.max(rest, axis=-1, keepdims=True)
    i2 = jnp.min(jnp.where(rest == m2, ids, N_EXPERTS), axis=-1, keepdims=True)
    top_e = jnp.concatenate([i1, i2], axis=-1)
    gates = jax.nn.softmax(jnp.concatenate([m1, m2], axis=-1), axis=-1)
    a = n * TOP_K
    flat_e = top_e.reshape(a)
    onehot = (flat_e[:, None] == jnp.arange(N_EXPERTS, dtype=flat_e.dtype)[None, :]).astype(jnp.int32)
    rank = jnp.take_along_axis(jnp.cumsum(onehot, axis=0) - onehot, flat_e[:, None], axis=1)[:, 0]
    counts = jnp.sum(onehot, axis=0)
    padded = (counts + MOE_TM - 1) // MOE_TM * MOE_TM
    pad_ends = jnp.cumsum(padded)
    pad_starts = pad_ends - padded
    dest = pad_starts[flat_e] + rank
    cap = a + N_EXPERTS * MOE_TM
    nblk = cap // MOE_TM
    row_tok = jnp.zeros((cap,), jnp.int32).at[dest].set(jnp.arange(a, dtype=jnp.int32) // TOP_K)
    blk_start = jnp.arange(nblk, dtype=jnp.int32) * MOE_TM
    blk_e = jnp.minimum(jnp.sum((pad_ends[None, :] <= blk_start[:, None]).astype(jnp.int32), axis=1),
                        N_EXPERTS - 1)
    n_used = (pad_ends[-1] // MOE_TM).astype(jnp.int32).reshape(1)
    last_e = blk_e[jnp.maximum(n_used[0] - 1, 0)]
    blk_e = jnp.where(jnp.arange(nblk) < n_used[0], blk_e, last_e)
    return gates, dest.reshape(n, TOP_K), row_tok, blk_e, n_used


def _final_kernel(x_ref, y0_ref, y1_ref, gt_ref, mod_ref, fg_ref, o_ref):
    mod = mod_ref[...]
    gt = gt_ref[...]
    f = gt[:, 0:1] * y0_ref[...] + gt[:, 1:2] * y1_ref[...]
    x = x_ref[...] + mod[0:1, 5 * D:6 * D] * f
    o_ref[...] = _rmsnorm_rows(x) * fg_ref[...]


def _moe_combine_final(x, y0, y1, gates, mod, final_g):
    n = x.shape[0]
    gt = jnp.zeros((n, LANE), F32).at[:, :TOP_K].set(gates)
    row = lambda i: (i, 0)
    const = lambda i: (0, 0)
    return pl.pallas_call(
        _final_kernel,
        grid=(n // TM,),
        in_specs=[pl.BlockSpec((TM, D), row)] * 3
        + [pl.BlockSpec((TM, LANE), row), pl.BlockSpec((8, N_MOD * D), const), pl.BlockSpec((1, D), const)],
        out_specs=pl.BlockSpec((TM, D), row),
        out_shape=jax.ShapeDtypeStruct((n, D), F32),
        compiler_params=_params(("arbitrary",), VMEM_LIMIT),
        name="moe_combine_final",
    )(x, y0, y1, gt, mod, final_g.reshape(1, D))


def _rope_tables(n):
    pos = jnp.arange(n, dtype=jnp.int32)
    r = (pos // GRID_W).astype(F32)
    col = (pos % GRID_W).astype(F32)
    inv = ROPE_BASE ** (-jnp.arange(0, AXIS_DIM, 2, dtype=F32) / AXIS_DIM)
    ang = jnp.concatenate([r[:, None] * inv, r[:, None] * inv, col[:, None] * inv, col[:, None] * inv], axis=1)
    sign = jnp.tile(jnp.concatenate([-jnp.ones((AXIS_DIM // 2,), F32), jnp.ones((AXIS_DIM // 2,), F32)]), 2)
    cos = jnp.tile(jnp.cos(ang), (1, LANE // HEAD_DIM))
    sin = jnp.tile(jnp.sin(ang) * sign, (1, LANE // HEAD_DIM))
    return cos, sin


def _prep_w_in(w):
    main = jnp.concatenate([w[:, :_C_A], w[:, _C_Q:_C_END]], axis=1).astype(BF16)
    wab = jnp.zeros((D, LANE), F32).at[:, :N_AB].set(w[:, _C_A:_C_Q])
    wab1 = wab.astype(BF16)
    wab0 = (wab - wab1.astype(F32)).astype(BF16)
    return main, wab1, wab0


def kernel(x, c, ctx, c_ctx, w_mod, b_mod, w_in, w_out, conv_w, dn_conv_w, dn_a_log, dn_dt_bias, dn_norm_g,
           attn_sink, ffn_w_gate, ffn_w_up, ffn_w_down, moe_router, moe_w_gate, moe_w_up, moe_w_down,
           final_norm_g):
    bsz, n, d = x.shape
    ctx_len = ctx.shape[1]
    depth = w_in.shape[0]
    assert bsz == 1 and d == D and ctx_len == TM and n % TM == 0 and n % GRID_W == 0
    cos, sin = _rope_tables(n)
    mods = _mod_vectors(c, c_ctx, w_mod, b_mod)
    h = jnp.concatenate([ctx[0], x[0]], axis=0)
    for layer in range(depth):
        last = layer == depth - 1
        mod = mods[layer]
        w_main, wab1, wab0 = _prep_w_in(w_in[layer])
        pconv, pqkv, pz, pq, pkv, pab = _in_proj(h, mod, w_main, wab1, wab0)
        qn, kn, vv, gb = _dn_prep(pqkv, pab, dn_conv_w[layer], dn_a_log[layer], dn_dt_bias[layer])
        o_f, o_b = _delta_net(qn, kn, vv, gb, ctx_len)
        sink = jnp.zeros((8, LANE), F32).at[0, :ATT_HEADS].set(attn_sink[layer])
        yc = _window_attention(pq, pkv, cos, sin, sink, ctx_len)
        if not last:
            yc = _context_attention(pq, pkv, sink, yc, ctx_len)
        router = None
        if layer % 2 == 1:
            wr = jnp.zeros((D, LANE), F32).at[:, :N_EXPERTS].set(moe_router[layer // 2])
            wr1 = wr.astype(BF16)
            router = (wr1, (wr - wr1.astype(F32)).astype(BF16))
        outs = _mixer_finish(h, mod, pconv, conv_w[layer], o_f, o_b, pz, dn_norm_g[layer], yc,
                             w_out[layer].astype(BF16), ctx_len, with_ctx=not last, router=router)
        if layer % 2 == 0:
            assert not last
            x1, hx = outs
            i = layer // 2
            h = _dense_ffn(hx, x1, mod, ffn_w_gate[i].astype(BF16), ffn_w_up[i].astype(BF16),
                           ffn_w_down[i].astype(BF16))
        else:
            assert last
            x1, hx, logits = outs
            i = layer // 2
            gates, dest, row_tok, blk_e, n_used = _moe_route(logits[:, :N_EXPERTS])
            xs = jnp.take(hx, row_tok, axis=0)
            y = _moe_experts(xs, blk_e, n_used, moe_w_gate[i].astype(BF16), moe_w_up[i].astype(BF16),
                             moe_w_down[i].astype(BF16))
            y0 = jnp.take(y, dest[:, 0], axis=0)
            y1 = jnp.take(y, dest[:, 1], axis=0)
            h = _moe_combine_final(x1, y0, y1, gates, mod, final_norm_g)
    return h.reshape(bsz, n, d)
```

```python
import functools

import jax
import jax.numpy as jnp
from jax import lax
from jax.experimental import pallas as pl
from jax.experimental.pallas import tpu as pltpu

F32 = jnp.float32
BF16 = jnp.bfloat16

D = 1024
N_MOD = 6
EPS = 1e-6
NEG = -1e30
GRID_W = 64

CONV_CH = 256
DN_HEADS = 6
DN_HEAD_DIM = 64
DN_DIM = DN_HEADS * DN_HEAD_DIM
DN_CHUNK = 64
DN_SUB = 16
DN_STEP_CHUNKS = 4
ATT_HEADS = 6
ATT_KV_HEADS = 2
ATT_GROUP = ATT_HEADS // ATT_KV_HEADS
HEAD_DIM = 64
ATT_DIM = ATT_HEADS * HEAD_DIM
ATT_KV_DIM = ATT_KV_HEADS * HEAD_DIM
ATT_BLOCK = 128
ROPE_BASE = 10000.0
AXIS_DIM = HEAD_DIM // 2
MIX_DIM = CONV_CH + DN_DIM + ATT_DIM

D_FF = 2816
N_EXPERTS = 8
TOP_K = 2
D_FF_EXPERT = 3584

TM = 256
FF_CHUNK = 256
MOE_TM = 512
MOE_TF = 512
LANE = 128
VMEM_LIMIT = 56 * 1024 * 1024

_C_QKV = 3 * CONV_CH
_C_Z = _C_QKV + 3 * DN_DIM
_C_A = _C_Z + DN_DIM
_C_Q = _C_A + 4 * DN_HEADS
_C_K = _C_Q + ATT_DIM
_C_V = _C_K + ATT_KV_DIM
_C_END = _C_V + ATT_KV_DIM
N_AB = 4 * DN_HEADS


def _params(sem=None, vmem=None):
    kw = {}
    if sem is not None:
        kw["dimension_semantics"] = sem
    if vmem is not None:
        kw["vmem_limit_bytes"] = vmem
    return pltpu.CompilerParams(**kw)


def _split2(a):
    hi = a.astype(BF16)
    lo = (a - hi.astype(F32)).astype(BF16)
    return hi, lo


def _split3(a):
    hi = a.astype(BF16)
    r = a - hi.astype(F32)
    mid = r.astype(BF16)
    lo = (r - mid.astype(F32)).astype(BF16)
    return hi, mid, lo


_NN = (((1,), (0,)), ((), ()))
_NT = (((1,), (1,)), ((), ()))
_TN = (((0,), (0,)), ((), ()))


def _mm(a, b, dims=_NN):
    return lax.dot_general(a.astype(BF16), b.astype(BF16), dims, preferred_element_type=F32)


def _mm3(a, b, dims=_NN):
    a1, a0 = _split2(a)
    b1, b0 = _split2(b)
    d = functools.partial(lax.dot_general, dimension_numbers=dims, preferred_element_type=F32)
    return d(a1, b1) + (d(a1, b0) + d(a0, b1))


_BNN = (((2,), (1,)), ((0,), (0,)))
_BNT = (((2,), (2,)), ((0,), (0,)))
_BTN = (((1,), (1,)), ((0,), (0,)))


def _bmm(a, b, dims=_BNN):
    return lax.dot_general(a.astype(BF16), b.astype(BF16), dims, preferred_element_type=F32)


def _bmm3(a, b, dims=_BNN):
    return _mm3(a, b, dims)


def _sigmoid(x):
    return 1.0 / (1.0 + jnp.exp(-x))


def _softplus(x):
    return jnp.maximum(x, 0.0) + jnp.log1p(jnp.exp(-jnp.abs(x)))


def _mod_row(mod_ref, is_ctx):
    mod = mod_ref[...]
    return jnp.where(is_ctx, mod[1:2, :], mod[0:1, :])


def _rmsnorm_rows(x):
    return x * lax.rsqrt(jnp.mean(x * x, axis=-1, keepdims=True) + EPS)


def _shift_rows(u, prow, nrow):
    n = u.shape[0]
    rid = lax.broadcasted_iota(jnp.int32, u.shape, 0)
    up = jnp.where(rid == 0, prow, pltpu.roll(u, 1, 0))
    un = jnp.where(rid == n - 1, nrow, pltpu.roll(u, n - 1, 0))
    return up, un


def _same_group(shape, group):
    sh = group.bit_length() - 1
    assert 1 << sh == group
    return (lax.broadcasted_iota(jnp.int32, shape, 0) >> sh) == (lax.broadcasted_iota(jnp.int32, shape, 1) >> sh)


def _head_blockdiag(n, group):
    return jnp.where(_same_group((n, n), group), 1.0, 0.0).astype(BF16)


def _group_sum(t, bd):
    hi, lo = _split2(t)
    d = functools.partial(lax.dot_general, dimension_numbers=_NN, preferred_element_type=F32)
    return d(hi, bd) + d(lo, bd)


MOD_TN = 1536


def _mod_kernel(s_ref, w_ref, b_ref, o_ref):
    s = s_ref[...]
    s = s * _sigmoid(s)
    o_ref[0] = _mm3(s, w_ref[0]) + b_ref[0]


def _mod_vectors(c, c_ctx, w_mod, b_mod):
    depth = w_mod.shape[0]
    s = jnp.zeros((8, D), F32).at[0].set(c[0]).at[1].set(c_ctx)
    return pl.pallas_call(
        _mod_kernel,
        grid=(depth, N_MOD * D // MOD_TN),
        in_specs=[
            pl.BlockSpec((8, D), lambda l, j: (0, 0)),
            pl.BlockSpec((1, D, MOD_TN), lambda l, j: (l, 0, j)),
            pl.BlockSpec((1, 1, MOD_TN), lambda l, j: (l, 0, j)),
        ],
        out_specs=pl.BlockSpec((1, 8, MOD_TN), lambda l, j: (l, 0, j)),
        out_shape=jax.ShapeDtypeStruct((depth, 8, N_MOD * D), F32),
        compiler_params=_params(("arbitrary", "arbitrary"), VMEM_LIMIT),
        name="mod_vectors",
    )(s, w_mod, b_mod.reshape(depth, 1, N_MOD * D))


def _in_kernel(h_ref, mod_ref, w_ref, wab1_ref, wab0_ref,
               pconv_ref, pqkv_ref, pz_ref, pq_ref, pkv_ref, pab_ref):
    row = _mod_row(mod_ref, pl.program_id(0) == 0)
    hm = _rmsnorm_rows(h_ref[...]) * (1.0 + row[:, D:2 * D]) + row[:, 0:D]
    h1, h0 = _split2(hm)
    d = functools.partial(lax.dot_general, dimension_numbers=_NN, preferred_element_type=F32)
    off = 0
    for ref in (pconv_ref, pqkv_ref, pz_ref, pq_ref, pkv_ref):
        w = ref.shape[1]
        ref[...] = d(h1, w_ref[:, off:off + w])
        off += w
    pab_ref[...] = d(h1, wab1_ref[...]) + (d(h0, wab1_ref[...]) + d(h1, wab0_ref[...]))


def _in_proj(h, mod, w_main, wab1, wab0):
    t = h.shape[0]
    widths = (3 * CONV_CH, 3 * DN_DIM, DN_DIM, ATT_DIM, 2 * ATT_KV_DIM, LANE)
    const = lambda i: (0, 0)
    return pl.pallas_call(
        _in_kernel,
        grid=(t // TM,),
        in_specs=[
            pl.BlockSpec((TM, D), lambda i: (i, 0)),
            pl.BlockSpec((8, N_MOD * D), const),
            pl.BlockSpec(w_main.shape, const),
            pl.BlockSpec(wab1.shape, const),
            pl.BlockSpec(wab0.shape, const),
        ],
        out_specs=[pl.BlockSpec((TM, w), lambda i: (i, 0)) for w in widths],
        out_shape=[jax.ShapeDtypeStruct((t, w), F32) for w in widths],
        compiler_params=_params(("arbitrary",), VMEM_LIMIT),
        name="in_proj",
    )(h, mod, w_main, wab1, wab0)


def _halo_rows(prev_ref, next_ref, i, nblk):
    pvalid = jnp.logical_and(i != 0, i != 1)
    nvalid = jnp.logical_and(i != 0, i != nblk - 1)
    prow = jnp.where(pvalid, prev_ref[7:8, :], 0.0)
    nrow = jnp.where(nvalid, next_ref[0:1, :], 0.0)
    return prow, nrow


def _dnprep_kernel(qkv_ref, prev_ref, next_ref, ab_ref, cw_ref, alog_ref, dtb_ref,
                   q_ref, k_ref, v_ref, gb_ref, *, nblk):
    i = pl.program_id(0)
    u = qkv_ref[...]
    prow, nrow = _halo_rows(prev_ref, next_ref, i, nblk)
    up, un = _shift_rows(u, prow, nrow)
    cw = cw_ref[...]
    y = up * cw[0:1, :] + u * cw[1:2, :] + un * cw[2:3, :]
    y = y * _sigmoid(y)
    q = y[:, 0:DN_DIM]
    k = y[:, DN_DIM:2 * DN_DIM]
    bd = _head_blockdiag(DN_DIM, DN_HEAD_DIM)
    q_ref[...] = q * lax.rsqrt(_group_sum(q * q, bd) + 1e-6) * (DN_HEAD_DIM ** -0.5)
    k_ref[...] = k * lax.rsqrt(_group_sum(k * k, bd) + 1e-6)
    v_ref[...] = y[:, 2 * DN_DIM:3 * DN_DIM]
    ab = ab_ref[...]
    g = -jnp.exp(alog_ref[...]) * _softplus(ab + dtb_ref[...])
    lane = lax.broadcasted_iota(jnp.int32, ab.shape, 1)
    gb_ref[...] = jnp.where(lane < 2 * DN_HEADS, g, _sigmoid(ab))


def _dn_prep(pqkv, pab, dn_conv_w, a_log, dt_bias):
    t, w = pqkv.shape
    nblk = t // TM
    alog = jnp.zeros((1, LANE), F32).at[0, :2 * DN_HEADS].set(a_log.reshape(-1))
    dtb = jnp.zeros((1, LANE), F32).at[0, :2 * DN_HEADS].set(dt_bias.reshape(-1))
    r8 = TM // 8
    const = lambda i: (0, 0)
    return pl.pallas_call(
        functools.partial(_dnprep_kernel, nblk=nblk),
        grid=(nblk,),
        in_specs=[
            pl.BlockSpec((TM, w), lambda i: (i, 0)),
            pl.BlockSpec((8, w), lambda i: (jnp.maximum(i * r8 - 1, 0), 0)),
            pl.BlockSpec((8, w), lambda i: (jnp.minimum((i + 1) * r8, t // 8 - 1), 0)),
            pl.BlockSpec((TM, LANE), lambda i: (i, 0)),
            pl.BlockSpec((3, w), const),
            pl.BlockSpec((1, LANE), const),
            pl.BlockSpec((1, LANE), const),
        ],
        out_specs=[pl.BlockSpec((TM, DN_DIM), lambda i: (i, 0))] * 3
        + [pl.BlockSpec((TM, LANE), lambda i: (i, 0))],
        out_shape=[jax.ShapeDtypeStruct((t, DN_DIM), F32)] * 3 + [jax.ShapeDtypeStruct((t, LANE), F32)],
        compiler_params=_params(("arbitrary",), VMEM_LIMIT),
        name="dn_prep",
    )(pqkv, pqkv, pqkv, pab, dn_conv_w, alog, dtb)


def _dn_chunk(rev, q_ref, k_ref, v_ref, gb_ref, o_ref, s_ref):
    c_ = DN_CHUNK
    ri = lax.broadcasted_iota(jnp.int32, (c_, c_), 0)
    ci = lax.broadcasted_iota(jnp.int32, (c_, c_), 1)
    incl = (ri <= ci) if rev else (ri >= ci)
    strict = (ri < ci) if rev else (ri > ci)
    same_sub = _same_group((c_, c_), DN_SUB)
    eye = jnp.where(ri == ci, 1.0, 0.0)
    tri = jnp.where(incl, 1.0, 0.0).astype(BF16)
    last = 0 if rev else c_ - 1

    nchunks = q_ref.shape[0] // c_
    nh = DN_HEADS
    col0 = nh if rev else 0
    rows = lambda g: slice(g * c_, (g + 1) * c_)
    lanes = lambda h: slice(h * DN_HEAD_DIM, (h + 1) * DN_HEAD_DIM)

    def stack(fn):
        return jnp.stack([fn(g, h) for g in range(nchunks) for h in range(nh)])

    gb = gb_ref[...]
    gcs = [_cumsum_rows(tri, gb[rows(g), :]) for g in range(nchunks)]
    gcts = [gc.T for gc in gcs]
    q = stack(lambda g, h: q_ref[rows(g), lanes(h)])
    k = stack(lambda g, h: k_ref[rows(g), lanes(h)])
    v = stack(lambda g, h: v_ref[rows(g), lanes(h)])
    gcol = stack(lambda g, h: gcs[g][:, col0 + h:col0 + h + 1])
    grow = stack(lambda g, h: gcts[g][col0 + h:col0 + h + 1, :])
    beta = stack(lambda g, h: gb[rows(g), 2 * nh + col0 + h:2 * nh + col0 + h + 1])
    glast = gcol[:, last:last + 1, :]
    decay = jnp.where(incl, jnp.exp(jnp.where(incl, gcol - grow, 0.0)), 0.0)
    eg = jnp.exp(gcol)
    kb = k * beta
    a = jnp.where(strict, _bmm(kb, k, _BNT) * decay, 0.0)
    qk = jnp.where(incl, _bmm(q, k, _BNT) * decay, 0.0)
    ad = jnp.where(same_sub, a, 0.0)
    ao = a - ad
    p = eye - ad
    n2 = _bmm(ad, ad)
    p = p + _bmm(p, n2)
    n4 = _bmm(n2, n2)
    p = p + _bmm(p, n4)
    n8 = _bmm(n4, n4)
    dinv = p + _bmm(p, n8)
    m = _bmm3(dinv, ao)
    m2 = _bmm(m, m)
    y = _bmm3(dinv, jnp.concatenate([v * beta, kb * eg], axis=-1))
    z = y + _bmm(m2, y)
    x = z - _bmm3(m, z)
    u = x[:, :, :DN_HEAD_DIM]
    w = x[:, :, DN_HEAD_DIM:]
    qg = q * eg
    kd = k * jnp.exp(glast - gcol)
    gl = jnp.exp(glast)
    s = s_ref[col0:col0 + nh]
    for g in (reversed(range(nchunks)) if rev else range(nchunks)):
        b = slice(g * nh, (g + 1) * nh)
        v_new = u[b] - _bmm(w[b], s)
        o = _bmm(qg[b], s) + _bmm(qk[b], v_new)
        s = s * gl[b] + _bmm(kd[b], v_new, _BTN)
        for h in range(nh):
            o_ref[rows(g), lanes(h)] = o[h]
    s_ref[col0:col0 + nh] = s


def _cumsum_rows(tri_bf16, g):
    g2, g1, g0 = _split3(g)
    d = functools.partial(lax.dot_general, dimension_numbers=_NN, preferred_element_type=F32)
    return d(tri_bf16, g2) + (d(tri_bf16, g1) + d(tri_bf16, g0))


def _dn_kernel(qf, kf, vf, gf, qb, kb, vb, gbb, of_ref, ob_ref, s_ref):
    @pl.when(pl.program_id(0) == 0)
    def _():
        s_ref[...] = jnp.zeros(s_ref.shape, F32)

    _dn_chunk(False, qf, kf, vf, gf, of_ref, s_ref)
    _dn_chunk(True, qb, kb, vb, gbb, ob_ref, s_ref)


def _delta_net(q, k, v, gb, ctx_len):
    t = q.shape[0]
    rows = DN_STEP_CHUNKS * DN_CHUNK
    assert ctx_len == rows and t % rows == 0
    nstep = t // rows

    def fwd(s):
        return (s, 0)

    def bwd(s):
        return (jnp.where(s == 0, 0, nstep - s), 0)

    wide = lambda im: pl.BlockSpec((rows, DN_DIM), im)
    narrow = lambda im: pl.BlockSpec((rows, LANE), im)
    return pl.pallas_call(
        _dn_kernel,
        grid=(nstep,),
        in_specs=[wide(fwd), wide(fwd), wide(fwd), narrow(fwd), wide(bwd), wide(bwd), wide(bwd), narrow(bwd)],
        out_specs=[wide(fwd), wide(bwd)],
        out_shape=[jax.ShapeDtypeStruct((t, DN_DIM), F32)] * 2,
        scratch_shapes=[pltpu.VMEM((2 * DN_HEADS, DN_HEAD_DIM, DN_HEAD_DIM), F32)],
        compiler_params=_params(("arbitrary",), VMEM_LIMIT),
        name="delta_net",
    )(q, k, v, gb, q, k, v, gb)


def _rope(x, cos, sin):
    w = x.shape[1]
    lane = lax.broadcasted_iota(jnp.int32, x.shape, 1)
    first_half = (lane & (AXIS_DIM - 1)) < (AXIS_DIM // 2)
    swapped = jnp.where(first_half, pltpu.roll(x, w - AXIS_DIM // 2, 1), pltpu.roll(x, AXIS_DIM // 2, 1))
    return x * cos + swapped * sin


def _softmax_av(s, sink, vals):
    m = jnp.maximum(jnp.max(s, axis=-1, keepdims=True), sink)
    p = jnp.exp(s - m)
    denom = jnp.sum(p, axis=-1, keepdims=True) + jnp.exp(sink - m)
    return _mm(p, vals) / denom


def _attend(q, keys, vals, valid, sink_all, o_ref):
    b = q.shape[0]
    for kvh in range(ATT_KV_HEADS):
        kl = slice(kvh * HEAD_DIM, (kvh + 1) * HEAD_DIM)
        heads = range(kvh * ATT_GROUP, (kvh + 1) * ATT_GROUP)
        qs = jnp.concatenate([q[:, h * HEAD_DIM:(h + 1) * HEAD_DIM] for h in heads], axis=0)
        sink = jnp.concatenate([jnp.broadcast_to(sink_all[0:1, h:h + 1], (b, 1)) for h in heads], axis=0)
        s = _mm(qs, keys[:, kl], _NT)
        if valid is not None:
            s = jnp.where(valid, s, NEG)
        o = _softmax_av(s, sink, vals[:, kl])
        for g, h in enumerate(heads):
            o_ref[:, h * HEAD_DIM:(h + 1) * HEAD_DIM] = o[g * b:(g + 1) * b, :]


def _attn_kernel(q_ref, kp_ref, kc_ref, kn_ref, kctx_ref, cq_ref, sq_ref, cp_ref, sp_ref, cn_ref, sn_ref,
                 sink_ref, o_ref, *, nb, off, with_ctx):
    j = pl.program_id(0)
    b = ATT_BLOCK
    scale = HEAD_DIM ** -0.5

    @pl.when(j < off)
    def _():
        if with_ctx:
            kvx = kctx_ref[...]
            _attend(q_ref[...] * scale, kvx[:, :ATT_KV_DIM], kvx[:, ATT_KV_DIM:], None, sink_ref[...], o_ref)
        else:
            o_ref[...] = jnp.zeros(o_ref.shape, F32)

    @pl.when(j >= off)
    def _():
        i = j - off
        cq = cq_ref[...]
        sq = sq_ref[...]
        q = q_ref[...]
        q = jnp.concatenate([_rope(q[:, l * LANE:(l + 1) * LANE], cq, sq) for l in range(ATT_DIM // LANE)], axis=1)
        kvp, kvc, kvn, kvx = kp_ref[...], kc_ref[...], kn_ref[...], kctx_ref[...]
        kp = _rope(kvp[:, :ATT_KV_DIM], cp_ref[...], sp_ref[...])
        kc = _rope(kvc[:, :ATT_KV_DIM], cq, sq)
        kn = _rope(kvn[:, :ATT_KV_DIM], cn_ref[...], sn_ref[...])
        keys = jnp.concatenate([kp, kc, kn, kvx[:, :ATT_KV_DIM]], axis=0)
        vals = jnp.concatenate(
            [kvp[:, ATT_KV_DIM:], kvc[:, ATT_KV_DIM:], kvn[:, ATT_KV_DIM:], kvx[:, ATT_KV_DIM:]], axis=0)
        nk = keys.shape[0]
        c = lax.broadcasted_iota(jnp.int32, (1, nk), 1)
        far = 4 * nk
        ccol = jnp.where(c >= 3 * b, b + ((c - 3 * b) & (b - 1)), c)
        ccol = jnp.where(jnp.logical_and(c < b, i == 0), -far, ccol)
        ccol = jnp.where(jnp.logical_and(jnp.logical_and(c >= 2 * b, c < 3 * b), i == nb - 1), far, ccol)
        r = lax.broadcasted_iota(jnp.int32, (ATT_GROUP * b, 1), 0) & (b - 1)
        valid = lax.bitcast_convert_type(ccol - r, jnp.uint32) <= jnp.uint32(2 * b)
        _attend(q * scale, keys, vals, valid, sink_ref[...], o_ref)


def _attention(pq, pkv, cos, sin, sink, ctx_len, with_ctx):
    t = pq.shape[0]
    n = t - ctx_len
    nb = n // ATT_BLOCK
    off = ctx_len // ATT_BLOCK
    kv = lambda im: pl.BlockSpec((ATT_BLOCK, 2 * ATT_KV_DIM), im)
    tab = lambda im: pl.BlockSpec((ATT_BLOCK, LANE), im)
    lat = lambda j: jnp.maximum(j - off, 0)
    cur = lambda j: (j, 0)
    prv = lambda j: (jnp.maximum(lat(j) - 1, 0) + off, 0)
    nxt = lambda j: (jnp.minimum(lat(j) + 1, nb - 1) + off, 0)
    tcur = lambda j: (lat(j), 0)
    tprv = lambda j: (jnp.maximum(lat(j) - 1, 0), 0)
    tnxt = lambda j: (jnp.minimum(lat(j) + 1, nb - 1), 0)
    return pl.pallas_call(
        functools.partial(_attn_kernel, nb=nb, off=off, with_ctx=with_ctx),
        grid=(off + nb,),
        in_specs=[
            pl.BlockSpec((ATT_BLOCK, ATT_DIM), cur),
            kv(prv), kv(cur), kv(nxt),
            pl.BlockSpec((ctx_len, 2 * ATT_KV_DIM), lambda j: (0, 0)),
            tab(tcur), tab(tcur), tab(tprv), tab(tprv), tab(tnxt), tab(tnxt),
            pl.BlockSpec((8, LANE), lambda j: (0, 0)),
        ],
        out_specs=pl.BlockSpec((ATT_BLOCK, ATT_DIM), cur),
        out_shape=jax.ShapeDtypeStruct((t, ATT_DIM), F32),
        compiler_params=_params(("arbitrary",), VMEM_LIMIT),
        name="attention",
    )(pq, pkv, pkv, pkv, pkv, cos, sin, cos, sin, cos, sin, sink)


def _mixfin_kernel(h_ref, mod_ref, pconv_ref, prev_ref, next_ref, cw_ref, of_ref, ob_ref, z_ref, ng_ref, yc_ref,
                   wout_ref, *rest, nblk, blk0, with_router):
    if with_router:
        wr1_ref, wr0_ref, x_ref, hx_ref, lg_ref = rest
    else:
        x_ref, hx_ref = rest
    i = pl.program_id(0) + blk0
    row = _mod_row(mod_ref, i == 0)
    pc = pconv_ref[...]
    u = pc[:, CONV_CH:2 * CONV_CH] * pc[:, 2 * CONV_CH:]
    prow, nrow = _halo_rows(prev_ref, next_ref, i, nblk)
    prow = prow[:, CONV_CH:2 * CONV_CH] * prow[:, 2 * CONV_CH:]
    nrow = nrow[:, CONV_CH:2 * CONV_CH] * nrow[:, 2 * CONV_CH:]
    up, un = _shift_rows(u, prow, nrow)
    cw = cw_ref[...]
    ya = pc[:, :CONV_CH] * (up * cw[0:1, :] + u * cw[1:2, :] + un * cw[2:3, :])
    o = of_ref[...] + ob_ref[...]
    ms = _group_sum(o * o, _head_blockdiag(DN_DIM, DN_HEAD_DIM)) * (1.0 / DN_HEAD_DIM)
    z = z_ref[...]
    yb = o * lax.rsqrt(ms + EPS) * ng_ref[...] * (z * _sigmoid(z))
    mix = jnp.concatenate([ya, yb, yc_ref[...]], axis=1)
    x = h_ref[...] + row[:, 2 * D:3 * D] * _mm(mix, wout_ref[...])
    x_ref[...] = x
    hx = _rmsnorm_rows(x) * (1.0 + row[:, 4 * D:5 * D]) + row[:, 3 * D:4 * D]
    hx_ref[...] = hx.astype(BF16)
    if with_router:
        h1, h0 = _split2(hx)
        d = functools.partial(lax.dot_general, dimension_numbers=_NN, preferred_element_type=F32)
        lg_ref[...] = d(h1, wr1_ref[...]) + (d(h0, wr1_ref[...]) + d(h1, wr0_ref[...]))


def _mixer_finish(h, mod, pconv, conv_w, o_f, o_b, pz, norm_g, yc, w_out, ctx_len, with_ctx, router=None):
    t = h.shape[0]
    nblk = t // TM
    blk0 = 0 if with_ctx else ctx_len // TM
    rows = t - blk0 * TM
    r8 = TM // 8
    w = pconv.shape[1]
    cur = lambda i: (i + blk0, 0)
    out_cur = lambda i: (i, 0)
    const = lambda i: (0, 0)
    ng = jnp.tile(norm_g.reshape(1, DN_HEAD_DIM), (1, DN_HEADS))
    in_specs = [
        pl.BlockSpec((TM, D), cur),
        pl.BlockSpec((8, N_MOD * D), const),
        pl.BlockSpec((TM, w), cur),
        pl.BlockSpec((8, w), lambda i: (jnp.maximum((i + blk0) * r8 - 1, 0), 0)),
        pl.BlockSpec((8, w), lambda i: (jnp.minimum((i + blk0 + 1) * r8, t // 8 - 1), 0)),
        pl.BlockSpec((3, CONV_CH), const),
        pl.BlockSpec((TM, DN_DIM), cur),
        pl.BlockSpec((TM, DN_DIM), cur),
        pl.BlockSpec((TM, DN_DIM), cur),
        pl.BlockSpec((1, DN_DIM), const),
        pl.BlockSpec((TM, ATT_DIM), cur),
        pl.BlockSpec((MIX_DIM, D), const),
    ]
    args = [h, mod, pconv, pconv, pconv, conv_w, o_f, o_b, pz, ng, yc, w_out]
    out_specs = [pl.BlockSpec((TM, D), out_cur), pl.BlockSpec((TM, D), out_cur)]
    out_shape = [jax.ShapeDtypeStruct((rows, D), F32), jax.ShapeDtypeStruct((rows, D), BF16)]
    if router is not None:
        in_specs += [pl.BlockSpec((D, LANE), const)] * 2
        args += list(router)
        out_specs.append(pl.BlockSpec((TM, LANE), out_cur))
        out_shape.append(jax.ShapeDtypeStruct((rows, LANE), F32))
    return pl.pallas_call(
        functools.partial(_mixfin_kernel, nblk=nblk, blk0=blk0, with_router=router is not None),
        grid=(rows // TM,),
        in_specs=in_specs,
        out_specs=out_specs,
        out_shape=out_shape,
        compiler_params=_params(("arbitrary",), VMEM_LIMIT),
        name="mixer_finish",
    )(*args)


def _ffn_kernel(hx_ref, x_ref, mod_ref, wg_ref, wu_ref, wd_ref, o_ref):
    row = _mod_row(mod_ref, pl.program_id(0) == 0)
    hx = hx_ref[...]
    acc = jnp.zeros((hx.shape[0], D), F32)
    for f in range(0, D_FF, FF_CHUNK):
        g = _mm(hx, wg_ref[:, f:f + FF_CHUNK])
        u = _mm(hx, wu_ref[:, f:f + FF_CHUNK])
        acc = acc + _mm(g * _sigmoid(g) * u, wd_ref[f:f + FF_CHUNK, :])
    o_ref[...] = x_ref[...] + row[:, 5 * D:6 * D] * acc


def _dense_ffn(hx, x, mod, wg, wu, wd):
    t = x.shape[0]
    const = lambda i: (0, 0)
    once = dict(pipeline_mode=pl.Buffered(1))
    return pl.pallas_call(
        _ffn_kernel,
        grid=(t // TM,),
        in_specs=[
            pl.BlockSpec((TM, D), lambda i: (i, 0)),
            pl.BlockSpec((TM, D), lambda i: (i, 0)),
            pl.BlockSpec((8, N_MOD * D), const),
            pl.BlockSpec((D, D_FF), const, **once),
            pl.BlockSpec((D, D_FF), const, **once),
            pl.BlockSpec((D_FF, D), const, **once),
        ],
        out_specs=pl.BlockSpec((TM, D), lambda i: (i, 0)),
        out_shape=jax.ShapeDtypeStruct((t, D), F32),
        compiler_params=_params(("arbitrary",), VMEM_LIMIT),
        name="dense_ffn",
    )(hx, x, mod, wg, wu, wd)


def _moe_kernel(be_ref, nu_ref, xs_ref, wg_ref, wu_ref, wd_ref, y_ref, acc_ref):
    b = pl.program_id(0)
    f = pl.program_id(1)

    @pl.when(b < nu_ref[0])
    def _():
        xs = xs_ref[...]
        g = _mm(xs, wg_ref[0])
        u = _mm(xs, wu_ref[0])
        part = _mm(g * _sigmoid(g) * u, wd_ref[0])

        @pl.when(f == 0)
        def _():
            acc_ref[...] = part

        @pl.when(f != 0)
        def _():
            acc_ref[...] += part

        @pl.when(f == pl.num_programs(1) - 1)
        def _():
            y_ref[...] = acc_ref[...]

    @pl.when(b >= nu_ref[0])
    def _():
        y_ref[...] = jnp.zeros(y_ref.shape, F32)


def _moe_experts(xs, blk_e, n_used, wg, wu, wd):
    cap = xs.shape[0]
    nblk = cap // MOE_TM
    nf = D_FF_EXPERT // MOE_TF

    def fidx(b, f, nu):
        return jnp.where(b < nu[0], f, nf - 1)

    grid_spec = pltpu.PrefetchScalarGridSpec(
        num_scalar_prefetch=2,
        grid=(nblk, nf),
        in_specs=[
            pl.BlockSpec((MOE_TM, D), lambda b, f, be, nu: (b, 0)),
            pl.BlockSpec((1, D, MOE_TF), lambda b, f, be, nu: (be[b], 0, fidx(b, f, nu))),
            pl.BlockSpec((1, D, MOE_TF), lambda b, f, be, nu: (be[b], 0, fidx(b, f, nu))),
            pl.BlockSpec((1, MOE_TF, D), lambda b, f, be, nu: (be[b], fidx(b, f, nu), 0)),
        ],
        out_specs=pl.BlockSpec((MOE_TM, D), lambda b, f, be, nu: (b, 0)),
        scratch_shapes=[pltpu.VMEM((MOE_TM, D), F32)],
    )
    return pl.pallas_call(
        _moe_kernel,
        grid_spec=grid_spec,
        out_shape=jax.ShapeDtypeStruct((cap, D), F32),
        compiler_params=_params(("arbitrary", "arbitrary"), VMEM_LIMIT),
        name="moe_experts",
    )(blk_e, n_used, xs, wg, wu, wd)


def _moe_route(logits):
    n = logits.shape[0]
    assert TOP_K == 2
    ids = jnp.arange(N_EXPERTS, dtype=jnp.int32)[None, :]
    m1 = jnp.max(logits, axis=-1, keepdims=True)
    i1 = jnp.min(jnp.where(logits == m1, ids, N_EXPERTS), axis=-1, keepdims=True)
    rest = jnp.where(ids == i1, -jnp.inf, logits)
    m2 = jnp.max(rest, axis=-1, keepdims=True)
    i2 = jnp.min(jnp.where(rest == m2, ids, N_EXPERTS), axis=-1, keepdims=True)
    top_e = jnp.concatenate([i1, i2], axis=-1)
    gates = jax.nn.softmax(jnp.concatenate([m1, m2], axis=-1), axis=-1)
    a = n * TOP_K
    flat_e = top_e.reshape(a)
    onehot = (flat_e[:, None] == jnp.arange(N_EXPERTS, dtype=flat_e.dtype)[None, :]).astype(jnp.int32)
    rank = jnp.take_along_axis(jnp.cumsum(onehot, axis=0) - onehot, flat_e[:, None], axis=1)[:, 0]
    counts = jnp.sum(onehot, axis=0)
    padded = (counts + MOE_TM - 1) // MOE_TM * MOE_TM
    pad_ends = jnp.cumsum(padded)
    pad_starts = pad_ends - padded
    dest = pad_starts[flat_e] + rank
    cap = a + N_EXPERTS * MOE_TM
    nblk = cap // MOE_TM
    row_tok = jnp.zeros((cap,), jnp.int32).at[dest].set(jnp.arange(a, dtype=jnp.int32) // TOP_K)
    blk_start = jnp.arange(nblk, dtype=jnp.int32) * MOE_TM
    blk_e = jnp.minimum(jnp.sum((pad_ends[None, :] <= blk_start[:, None]).astype(jnp.int32), axis=1),
                        N_EXPERTS - 1)
    n_used = (pad_ends[-1] // MOE_TM).astype(jnp.int32).reshape(1)
    last_e = blk_e[jnp.maximum(n_used[0] - 1, 0)]
    blk_e = jnp.where(jnp.arange(nblk) < n_used[0], blk_e, last_e)
    return gates, dest.reshape(n, TOP_K), row_tok, blk_e, n_used


def _final_kernel(x_ref, y0_ref, y1_ref, gt_ref, mod_ref, fg_ref, o_ref):
    mod = mod_ref[...]
    gt = gt_ref[...]
    f = gt[:, 0:1] * y0_ref[...] + gt[:, 1:2] * y1_ref[...]
    x = x_ref[...] + mod[0:1, 5 * D:6 * D] * f
    o_ref[...] = _rmsnorm_rows(x) * fg_ref[...]


def _moe_combine_final(x, y0, y1, gates, mod, final_g):
    n = x.shape[0]
    gt = jnp.zeros((n, LANE), F32).at[:, :TOP_K].set(gates)
    row = lambda i: (i, 0)
    const = lambda i: (0, 0)
    return pl.pallas_call(
        _final_kernel,
        grid=(n // TM,),
        in_specs=[pl.BlockSpec((TM, D), row)] * 3
        + [pl.BlockSpec((TM, LANE), row), pl.BlockSpec((8, N_MOD * D), const), pl.BlockSpec((1, D), const)],
        out_specs=pl.BlockSpec((TM, D), row),
        out_shape=jax.ShapeDtypeStruct((n, D), F32),
        compiler_params=_params(("arbitrary",), VMEM_LIMIT),
        name="moe_combine_final",
    )(x, y0, y1, gt, mod, final_g.reshape(1, D))


def _rope_tables(n):
    pos = jnp.arange(n, dtype=jnp.int32)
    r = (pos // GRID_W).astype(F32)
    col = (pos % GRID_W).astype(F32)
    inv = ROPE_BASE ** (-jnp.arange(0, AXIS_DIM, 2, dtype=F32) / AXIS_DIM)
    ang = jnp.concatenate([r[:, None] * inv, r[:, None] * inv, col[:, None] * inv, col[:, None] * inv], axis=1)
    sign = jnp.tile(jnp.concatenate([-jnp.ones((AXIS_DIM // 2,), F32), jnp.ones((AXIS_DIM // 2,), F32)]), 2)
    cos = jnp.tile(jnp.cos(ang), (1, LANE // HEAD_DIM))
    sin = jnp.tile(jnp.sin(ang) * sign, (1, LANE // HEAD_DIM))
    return cos, sin


def _prep_w_in(w):
    main = jnp.concatenate([w[:, :_C_A], w[:, _C_Q:_C_END]], axis=1).astype(BF16)
    wab = jnp.zeros((D, LANE), F32).at[:, :N_AB].set(w[:, _C_A:_C_Q])
    wab1 = wab.astype(BF16)
    wab0 = (wab - wab1.astype(F32)).astype(BF16)
    return main, wab1, wab0


def kernel(x, c, ctx, c_ctx, w_mod, b_mod, w_in, w_out, conv_w, dn_conv_w, dn_a_log, dn_dt_bias, dn_norm_g,
           attn_sink, ffn_w_gate, ffn_w_up, ffn_w_down, moe_router, moe_w_gate, moe_w_up, moe_w_down,
           final_norm_g):
    bsz, n, d = x.shape
    ctx_len = ctx.shape[1]
    depth = w_in.shape[0]
    assert bsz == 1 and d == D and ctx_len == TM and n % TM == 0 and n % GRID_W == 0
    cos, sin = _rope_tables(n)
    mods = _mod_vectors(c, c_ctx, w_mod, b_mod)
    h = jnp.concatenate([ctx[0], x[0]], axis=0)
    for layer in range(depth):
        last = layer == depth - 1
        mod = mods[layer]
        w_main, wab1, wab0 = _prep_w_in(w_in[layer])
        pconv, pqkv, pz, pq, pkv, pab = _in_proj(h, mod, w_main, wab1, wab0)
        qn, kn, vv, gb = _dn_prep(pqkv, pab, dn_conv_w[layer], dn_a_log[layer], dn_dt_bias[layer])
        o_f, o_b = _delta_net(qn, kn, vv, gb, ctx_len)
        sink = jnp.zeros((8, LANE), F32).at[0, :ATT_HEADS].set(attn_sink[layer])
        yc = _attention(pq, pkv, cos, sin, sink, ctx_len, with_ctx=not last)
        router = None
        if layer % 2 == 1:
            wr = jnp.zeros((D, LANE), F32).at[:, :N_EXPERTS].set(moe_router[layer // 2])
            wr1 = wr.astype(BF16)
            router = (wr1, (wr - wr1.astype(F32)).astype(BF16))
        outs = _mixer_finish(h, mod, pconv, conv_w[layer], o_f, o_b, pz, dn_norm_g[layer], yc,
                             w_out[layer].astype(BF16), ctx_len, with_ctx=not last, router=router)
        if layer % 2 == 0:
            assert not last
            x1, hx = outs
            i = layer // 2
            h = _dense_ffn(hx, x1, mod, ffn_w_gate[i].astype(BF16), ffn_w_up[i].astype(BF16),
                           ffn_w_down[i].astype(BF16))
        else:
            assert last
            x1, hx, logits = outs
            i = layer // 2
            gates, dest, row_tok, blk_e, n_used = _moe_route(logits[:, :N_EXPERTS])
            xs = jnp.take(hx, row_tok, axis=0)
            y = _moe_experts(xs, blk_e, n_used, moe_w_gate[i].astype(BF16), moe_w_up[i].astype(BF16),
                             moe_w_down[i].astype(BF16))
            y0 = jnp.take(y, dest[:, 0], axis=0)
            y1 = jnp.take(y, dest[:, 1], axis=0)
            h = _moe_combine_final(x1, y0, y1, gates, mod, final_norm_g)
    return h.reshape(bsz, n, d)
```

```python
import functools

import jax
import jax.numpy as jnp
from jax import lax
from jax.experimental import pallas as pl
from jax.experimental.pallas import tpu as pltpu

F32 = jnp.float32
BF16 = jnp.bfloat16

D = 1024
N_MOD = 6
EPS = 1e-6
NEG = -1e30
GRID_W = 64

CONV_CH = 256
DN_HEADS = 6
DN_HEAD_DIM = 64
DN_DIM = DN_HEADS * DN_HEAD_DIM
DN_CHUNK = 64
DN_SUB = 16
DN_STEP_CHUNKS = 4
ATT_HEADS = 6
ATT_KV_HEADS = 2
ATT_GROUP = ATT_HEADS // ATT_KV_HEADS
HEAD_DIM = 64
ATT_DIM = ATT_HEADS * HEAD_DIM
ATT_KV_DIM = ATT_KV_HEADS * HEAD_DIM
ATT_BLOCK = 128
ROPE_BASE = 10000.0
AXIS_DIM = HEAD_DIM // 2
MIX_DIM = CONV_CH + DN_DIM + ATT_DIM

D_FF = 2816
N_EXPERTS = 8
TOP_K = 2
D_FF_EXPERT = 3584

TM = 256
FF_CHUNK = 256
MOE_TM = 512
MOE_TF = 1792
MOE_SUB = 256
LANE = 128
VMEM_LIMIT = 56 * 1024 * 1024

_C_QKV = 3 * CONV_CH
_C_Z = _C_QKV + 3 * DN_DIM
_C_A = _C_Z + DN_DIM
_C_Q = _C_A + 4 * DN_HEADS
_C_K = _C_Q + ATT_DIM
_C_V = _C_K + ATT_KV_DIM
_C_END = _C_V + ATT_KV_DIM
N_AB = 4 * DN_HEADS


def _params(sem=None, vmem=None):
    kw = {}
    if sem is not None:
        kw["dimension_semantics"] = sem
    if vmem is not None:
        kw["vmem_limit_bytes"] = vmem
    return pltpu.CompilerParams(**kw)


def _split2(a):
    hi = a.astype(BF16)
    lo = (a - hi.astype(F32)).astype(BF16)
    return hi, lo


def _split3(a):
    hi = a.astype(BF16)
    r = a - hi.astype(F32)
    mid = r.astype(BF16)
    lo = (r - mid.astype(F32)).astype(BF16)
    return hi, mid, lo


_NN = (((1,), (0,)), ((), ()))
_NT = (((1,), (1,)), ((), ()))
_TN = (((0,), (0,)), ((), ()))


def _mm(a, b, dims=_NN):
    return lax.dot_general(a.astype(BF16), b.astype(BF16), dims, preferred_element_type=F32)


def _mm3(a, b, dims=_NN):
    a1, a0 = _split2(a)
    b1, b0 = _split2(b)
    d = functools.partial(lax.dot_general, dimension_numbers=dims, preferred_element_type=F32)
    return d(a1, b1) + (d(a1, b0) + d(a0, b1))


_BNN = (((2,), (1,)), ((0,), (0,)))
_BNT = (((2,), (2,)), ((0,), (0,)))
_BTN = (((1,), (1,)), ((0,), (0,)))


def _bmm(a, b, dims=_BNN):
    return lax.dot_general(a.astype(BF16), b.astype(BF16), dims, preferred_element_type=F32)


def _bmm3(a, b, dims=_BNN):
    return _mm3(a, b, dims)


def _sigmoid(x):
    return 1.0 / (1.0 + jnp.exp(-x))


def _softplus(x):
    return jnp.maximum(x, 0.0) + jnp.log1p(jnp.exp(-jnp.abs(x)))


def _mod_row(mod_ref, is_ctx):
    mod = mod_ref[...]
    return jnp.where(is_ctx, mod[1:2, :], mod[0:1, :])


def _rmsnorm_rows(x):
    return x * lax.rsqrt(jnp.mean(x * x, axis=-1, keepdims=True) + EPS)


def _shift_rows(u, prow, nrow):
    n = u.shape[0]
    rid = lax.broadcasted_iota(jnp.int32, u.shape, 0)
    up = jnp.where(rid == 0, prow, pltpu.roll(u, 1, 0))
    un = jnp.where(rid == n - 1, nrow, pltpu.roll(u, n - 1, 0))
    return up, un


def _same_group(shape, group):
    sh = group.bit_length() - 1
    assert 1 << sh == group
    return (lax.broadcasted_iota(jnp.int32, shape, 0) >> sh) == (lax.broadcasted_iota(jnp.int32, shape, 1) >> sh)


def _head_blockdiag(n, group):
    return jnp.where(_same_group((n, n), group), 1.0, 0.0).astype(BF16)


def _group_sum(t, bd):
    hi, lo = _split2(t)
    d = functools.partial(lax.dot_general, dimension_numbers=_NN, preferred_element_type=F32)
    return d(hi, bd) + d(lo, bd)


MOD_TN = 1536


def _mod_kernel(s_ref, w_ref, b_ref, o_ref):
    s = s_ref[...]
    s = s * _sigmoid(s)
    o_ref[0] = _mm3(s, w_ref[0]) + b_ref[0]


def _mod_vectors(c, c_ctx, w_mod, b_mod):
    depth = w_mod.shape[0]
    s = jnp.zeros((8, D), F32).at[0].set(c[0]).at[1].set(c_ctx)
    return pl.pallas_call(
        _mod_kernel,
        grid=(depth, N_MOD * D // MOD_TN),
        in_specs=[
            pl.BlockSpec((8, D), lambda l, j: (0, 0)),
            pl.BlockSpec((1, D, MOD_TN), lambda l, j: (l, 0, j)),
            pl.BlockSpec((1, 1, MOD_TN), lambda l, j: (l, 0, j)),
        ],
        out_specs=pl.BlockSpec((1, 8, MOD_TN), lambda l, j: (l, 0, j)),
        out_shape=jax.ShapeDtypeStruct((depth, 8, N_MOD * D), F32),
        compiler_params=_params(("arbitrary", "arbitrary"), VMEM_LIMIT),
        name="mod_vectors",
    )(s, w_mod, b_mod.reshape(depth, 1, N_MOD * D))


def _in_kernel(h_ref, mod_ref, w_ref, wab1_ref, wab0_ref,
               pconv_ref, pqkv_ref, pz_ref, pq_ref, pkv_ref, pab_ref):
    row = _mod_row(mod_ref, pl.program_id(0) == 0)
    hm = _rmsnorm_rows(h_ref[...]) * (1.0 + row[:, D:2 * D]) + row[:, 0:D]
    h1, h0 = _split2(hm)
    d = functools.partial(lax.dot_general, dimension_numbers=_NN, preferred_element_type=F32)
    off = 0
    for ref in (pconv_ref, pqkv_ref, pz_ref, pq_ref, pkv_ref):
        w = ref.shape[1]
        ref[...] = d(h1, w_ref[:, off:off + w])
        off += w
    pab_ref[...] = d(h1, wab1_ref[...]) + (d(h0, wab1_ref[...]) + d(h1, wab0_ref[...]))


def _in_proj(h, mod, w_main, wab1, wab0):
    t = h.shape[0]
    widths = (3 * CONV_CH, 3 * DN_DIM, DN_DIM, ATT_DIM, 2 * ATT_KV_DIM, LANE)
    const = lambda i: (0, 0)
    return pl.pallas_call(
        _in_kernel,
        grid=(t // TM,),
        in_specs=[
            pl.BlockSpec((TM, D), lambda i: (i, 0)),
            pl.BlockSpec((8, N_MOD * D), const),
            pl.BlockSpec(w_main.shape, const),
            pl.BlockSpec(wab1.shape, const),
            pl.BlockSpec(wab0.shape, const),
        ],
        out_specs=[pl.BlockSpec((TM, w), lambda i: (i, 0)) for w in widths],
        out_shape=[jax.ShapeDtypeStruct((t, w), F32) for w in widths],
        compiler_params=_params(("arbitrary",), VMEM_LIMIT),
        name="in_proj",
    )(h, mod, w_main, wab1, wab0)


def _halo_rows(prev_ref, next_ref, i, nblk):
    pvalid = jnp.logical_and(i != 0, i != 1)
    nvalid = jnp.logical_and(i != 0, i != nblk - 1)
    prow = jnp.where(pvalid, prev_ref[7:8, :], 0.0)
    nrow = jnp.where(nvalid, next_ref[0:1, :], 0.0)
    return prow, nrow


def _dnprep_kernel(qkv_ref, prev_ref, next_ref, ab_ref, cw_ref, alog_ref, dtb_ref,
                   q_ref, k_ref, v_ref, gb_ref, *, nblk):
    i = pl.program_id(0)
    u = qkv_ref[...]
    prow, nrow = _halo_rows(prev_ref, next_ref, i, nblk)
    up, un = _shift_rows(u, prow, nrow)
    cw = cw_ref[...]
    y = up * cw[0:1, :] + u * cw[1:2, :] + un * cw[2:3, :]
    y = y * _sigmoid(y)
    q = y[:, 0:DN_DIM]
    k = y[:, DN_DIM:2 * DN_DIM]
    bd = _head_blockdiag(DN_DIM, DN_HEAD_DIM)
    q_ref[...] = q * lax.rsqrt(_group_sum(q * q, bd) + 1e-6) * (DN_HEAD_DIM ** -0.5)
    k_ref[...] = k * lax.rsqrt(_group_sum(k * k, bd) + 1e-6)
    v_ref[...] = y[:, 2 * DN_DIM:3 * DN_DIM]
    ab = ab_ref[...]
    g = -jnp.exp(alog_ref[...]) * _softplus(ab + dtb_ref[...])
    lane = lax.broadcasted_iota(jnp.int32, ab.shape, 1)
    gb_ref[...] = jnp.where(lane < 2 * DN_HEADS, g, _sigmoid(ab))


def _dn_prep(pqkv, pab, dn_conv_w, a_log, dt_bias):
    t, w = pqkv.shape
    nblk = t // TM
    alog = jnp.zeros((1, LANE), F32).at[0, :2 * DN_HEADS].set(a_log.reshape(-1))
    dtb = jnp.zeros((1, LANE), F32).at[0, :2 * DN_HEADS].set(dt_bias.reshape(-1))
    r8 = TM // 8
    const = lambda i: (0, 0)
    return pl.pallas_call(
        functools.partial(_dnprep_kernel, nblk=nblk),
        grid=(nblk,),
        in_specs=[
            pl.BlockSpec((TM, w), lambda i: (i, 0)),
            pl.BlockSpec((8, w), lambda i: (jnp.maximum(i * r8 - 1, 0), 0)),
            pl.BlockSpec((8, w), lambda i: (jnp.minimum((i + 1) * r8, t // 8 - 1), 0)),
            pl.BlockSpec((TM, LANE), lambda i: (i, 0)),
            pl.BlockSpec((3, w), const),
            pl.BlockSpec((1, LANE), const),
            pl.BlockSpec((1, LANE), const),
        ],
        out_specs=[pl.BlockSpec((TM, DN_DIM), lambda i: (i, 0))] * 3
        + [pl.BlockSpec((TM, LANE), lambda i: (i, 0))],
        out_shape=[jax.ShapeDtypeStruct((t, DN_DIM), F32)] * 3 + [jax.ShapeDtypeStruct((t, LANE), F32)],
        compiler_params=_params(("arbitrary",), VMEM_LIMIT),
        name="dn_prep",
    )(pqkv, pqkv, pqkv, pab, dn_conv_w, alog, dtb)


def _dn_chunk(rev, q_ref, k_ref, v_ref, gb_ref, o_ref, s_ref):
    c_ = DN_CHUNK
    ri = lax.broadcasted_iota(jnp.int32, (c_, c_), 0)
    ci = lax.broadcasted_iota(jnp.int32, (c_, c_), 1)
    incl = (ri <= ci) if rev else (ri >= ci)
    strict = (ri < ci) if rev else (ri > ci)
    same_sub = _same_group((c_, c_), DN_SUB)
    eye = jnp.where(ri == ci, 1.0, 0.0)
    tri = jnp.where(incl, 1.0, 0.0).astype(BF16)
    last = 0 if rev else c_ - 1

    nchunks = q_ref.shape[0] // c_
    nh = DN_HEADS
    col0 = nh if rev else 0
    rows = lambda g: slice(g * c_, (g + 1) * c_)
    lanes = lambda h: slice(h * DN_HEAD_DIM, (h + 1) * DN_HEAD_DIM)

    def stack(fn):
        return jnp.stack([fn(g, h) for g in range(nchunks) for h in range(nh)])

    gb = gb_ref[...]
    gcs = [_cumsum_rows(tri, gb[rows(g), :]) for g in range(nchunks)]
    gcts = [gc.T for gc in gcs]
    q = stack(lambda g, h: q_ref[rows(g), lanes(h)])
    k = stack(lambda g, h: k_ref[rows(g), lanes(h)])
    v = stack(lambda g, h: v_ref[rows(g), lanes(h)])
    gcol = stack(lambda g, h: gcs[g][:, col0 + h:col0 + h + 1])
    grow = stack(lambda g, h: gcts[g][col0 + h:col0 + h + 1, :])
    beta = stack(lambda g, h: gb[rows(g), 2 * nh + col0 + h:2 * nh + col0 + h + 1])
    glast = gcol[:, last:last + 1, :]
    decay = jnp.where(incl, jnp.exp(jnp.where(incl, gcol - grow, 0.0)), 0.0)
    eg = jnp.exp(gcol)
    kb = k * beta
    a = jnp.where(strict, _bmm(kb, k, _BNT) * decay, 0.0)
    qk = jnp.where(incl, _bmm(q, k, _BNT) * decay, 0.0)
    ad = jnp.where(same_sub, a, 0.0)
    ao = a - ad
    p = eye - ad
    n2 = _bmm(ad, ad)
    p = p + _bmm(p, n2)
    n4 = _bmm(n2, n2)
    p = p + _bmm(p, n4)
    n8 = _bmm(n4, n4)
    dinv = p + _bmm(p, n8)
    m = _bmm3(dinv, ao)
    m2 = _bmm(m, m)
    y = _bmm3(dinv, jnp.concatenate([v * beta, kb * eg], axis=-1))
    z = y + _bmm(m2, y)
    x = z - _bmm3(m, z)
    u = x[:, :, :DN_HEAD_DIM]
    w = x[:, :, DN_HEAD_DIM:]
    qg = q * eg
    kd = k * jnp.exp(glast - gcol)
    gl = jnp.exp(glast)
    s = s_ref[col0:col0 + nh]
    for g in (reversed(range(nchunks)) if rev else range(nchunks)):
        b = slice(g * nh, (g + 1) * nh)
        v_new = u[b] - _bmm(w[b], s)
        o = _bmm(qg[b], s) + _bmm(qk[b], v_new)
        s = s * gl[b] + _bmm(kd[b], v_new, _BTN)
        for h in range(nh):
            o_ref[rows(g), lanes(h)] = o[h]
    s_ref[col0:col0 + nh] = s


def _cumsum_rows(tri_bf16, g):
    g2, g1, g0 = _split3(g)
    d = functools.partial(lax.dot_general, dimension_numbers=_NN, preferred_element_type=F32)
    return d(tri_bf16, g2) + (d(tri_bf16, g1) + d(tri_bf16, g0))


def _dn_kernel(qf, kf, vf, gf, qb, kb, vb, gbb, of_ref, ob_ref, s_ref):
    @pl.when(pl.program_id(0) == 0)
    def _():
        s_ref[...] = jnp.zeros(s_ref.shape, F32)

    _dn_chunk(False, qf, kf, vf, gf, of_ref, s_ref)
    _dn_chunk(True, qb, kb, vb, gbb, ob_ref, s_ref)


def _delta_net(q, k, v, gb, ctx_len):
    t = q.shape[0]
    rows = DN_STEP_CHUNKS * DN_CHUNK
    assert ctx_len == rows and t % rows == 0
    nstep = t // rows

    def fwd(s):
        return (s, 0)

    def bwd(s):
        return (jnp.where(s == 0, 0, nstep - s), 0)

    wide = lambda im: pl.BlockSpec((rows, DN_DIM), im)
    narrow = lambda im: pl.BlockSpec((rows, LANE), im)
    return pl.pallas_call(
        _dn_kernel,
        grid=(nstep,),
        in_specs=[wide(fwd), wide(fwd), wide(fwd), narrow(fwd), wide(bwd), wide(bwd), wide(bwd), narrow(bwd)],
        out_specs=[wide(fwd), wide(bwd)],
        out_shape=[jax.ShapeDtypeStruct((t, DN_DIM), F32)] * 2,
        scratch_shapes=[pltpu.VMEM((2 * DN_HEADS, DN_HEAD_DIM, DN_HEAD_DIM), F32)],
        compiler_params=_params(("arbitrary",), VMEM_LIMIT),
        name="delta_net",
    )(q, k, v, gb, q, k, v, gb)


def _rope(x, cos, sin):
    w = x.shape[1]
    lane = lax.broadcasted_iota(jnp.int32, x.shape, 1)
    first_half = (lane & (AXIS_DIM - 1)) < (AXIS_DIM // 2)
    swapped = jnp.where(first_half, pltpu.roll(x, w - AXIS_DIM // 2, 1), pltpu.roll(x, AXIS_DIM // 2, 1))
    return x * cos + swapped * sin


def _softmax_av(s, sink, vals):
    m = jnp.maximum(jnp.max(s, axis=-1, keepdims=True), sink)
    p = jnp.exp(s - m)
    denom = jnp.sum(p, axis=-1, keepdims=True) + jnp.exp(sink - m)
    return _mm(p, vals) / denom


def _attend(q, keys, vals, valid, sink_all, o_ref):
    b = q.shape[0]
    for kvh in range(ATT_KV_HEADS):
        kl = slice(kvh * HEAD_DIM, (kvh + 1) * HEAD_DIM)
        heads = range(kvh * ATT_GROUP, (kvh + 1) * ATT_GROUP)
        qs = jnp.concatenate([q[:, h * HEAD_DIM:(h + 1) * HEAD_DIM] for h in heads], axis=0)
        sink = jnp.concatenate([jnp.broadcast_to(sink_all[0:1, h:h + 1], (b, 1)) for h in heads], axis=0)
        s = _mm(qs, keys[:, kl], _NT)
        if valid is not None:
            s = jnp.where(valid, s, NEG)
        o = _softmax_av(s, sink, vals[:, kl])
        for g, h in enumerate(heads):
            o_ref[:, h * HEAD_DIM:(h + 1) * HEAD_DIM] = o[g * b:(g + 1) * b, :]


def _attn_kernel(q_ref, kp_ref, kc_ref, kn_ref, kctx_ref, cq_ref, sq_ref, cp_ref, sp_ref, cn_ref, sn_ref,
                 sink_ref, o_ref, *, nb, off, with_ctx):
    j = pl.program_id(0)
    b = ATT_BLOCK
    scale = HEAD_DIM ** -0.5

    @pl.when(j < off)
    def _():
        if with_ctx:
            kvx = kctx_ref[...]
            _attend(q_ref[...] * scale, kvx[:, :ATT_KV_DIM], kvx[:, ATT_KV_DIM:], None, sink_ref[...], o_ref)
        else:
            o_ref[...] = jnp.zeros(o_ref.shape, F32)

    @pl.when(j >= off)
    def _():
        i = j - off
        cq = cq_ref[...]
        sq = sq_ref[...]
        q = q_ref[...]
        q = jnp.concatenate([_rope(q[:, l * LANE:(l + 1) * LANE], cq, sq) for l in range(ATT_DIM // LANE)], axis=1)
        kvp, kvc, kvn, kvx = kp_ref[...], kc_ref[...], kn_ref[...], kctx_ref[...]
        kp = _rope(kvp[:, :ATT_KV_DIM], cp_ref[...], sp_ref[...])
        kc = _rope(kvc[:, :ATT_KV_DIM], cq, sq)
        kn = _rope(kvn[:, :ATT_KV_DIM], cn_ref[...], sn_ref[...])
        keys = jnp.concatenate([kp, kc, kn, kvx[:, :ATT_KV_DIM]], axis=0)
        vals = jnp.concatenate(
            [kvp[:, ATT_KV_DIM:], kvc[:, ATT_KV_DIM:], kvn[:, ATT_KV_DIM:], kvx[:, ATT_KV_DIM:]], axis=0)
        nk = keys.shape[0]
        c = lax.broadcasted_iota(jnp.int32, (1, nk), 1)
        far = 4 * nk
        ccol = jnp.where(c >= 3 * b, b + ((c - 3 * b) & (b - 1)), c)
        ccol = jnp.where(jnp.logical_and(c < b, i == 0), -far, ccol)
        ccol = jnp.where(jnp.logical_and(jnp.logical_and(c >= 2 * b, c < 3 * b), i == nb - 1), far, ccol)
        r = lax.broadcasted_iota(jnp.int32, (ATT_GROUP * b, 1), 0) & (b - 1)
        valid = lax.bitcast_convert_type(ccol - r, jnp.uint32) <= jnp.uint32(2 * b)
        _attend(q * scale, keys, vals, valid, sink_ref[...], o_ref)


def _attention(pq, pkv, cos, sin, sink, ctx_len, with_ctx):
    t = pq.shape[0]
    n = t - ctx_len
    nb = n // ATT_BLOCK
    off = ctx_len // ATT_BLOCK
    kv = lambda im: pl.BlockSpec((ATT_BLOCK, 2 * ATT_KV_DIM), im)
    tab = lambda im: pl.BlockSpec((ATT_BLOCK, LANE), im)
    lat = lambda j: jnp.maximum(j - off, 0)
    cur = lambda j: (j, 0)
    prv = lambda j: (jnp.maximum(lat(j) - 1, 0) + off, 0)
    nxt = lambda j: (jnp.minimum(lat(j) + 1, nb - 1) + off, 0)
    tcur = lambda j: (lat(j), 0)
    tprv = lambda j: (jnp.maximum(lat(j) - 1, 0), 0)
    tnxt = lambda j: (jnp.minimum(lat(j) + 1, nb - 1), 0)
    return pl.pallas_call(
        functools.partial(_attn_kernel, nb=nb, off=off, with_ctx=with_ctx),
        grid=(off + nb,),
        in_specs=[
            pl.BlockSpec((ATT_BLOCK, ATT_DIM), cur),
            kv(prv), kv(cur), kv(nxt),
            pl.BlockSpec((ctx_len, 2 * ATT_KV_DIM), lambda j: (0, 0)),
            tab(tcur), tab(tcur), tab(tprv), tab(tprv), tab(tnxt), tab(tnxt),
            pl.BlockSpec((8, LANE), lambda j: (0, 0)),
        ],
        out_specs=pl.BlockSpec((ATT_BLOCK, ATT_DIM), cur),
        out_shape=jax.ShapeDtypeStruct((t, ATT_DIM), F32),
        compiler_params=_params(("arbitrary",), VMEM_LIMIT),
        name="attention",
    )(pq, pkv, pkv, pkv, pkv, cos, sin, cos, sin, cos, sin, sink)


def _mixfin_kernel(h_ref, mod_ref, pconv_ref, prev_ref, next_ref, cw_ref, of_ref, ob_ref, z_ref, ng_ref, yc_ref,
                   wout_ref, *rest, nblk, blk0, with_router):
    if with_router:
        wr1_ref, wr0_ref, x_ref, hx_ref, lg_ref = rest
    else:
        x_ref, hx_ref = rest
    i = pl.program_id(0) + blk0
    row = _mod_row(mod_ref, i == 0)
    pc = pconv_ref[...]
    u = pc[:, CONV_CH:2 * CONV_CH] * pc[:, 2 * CONV_CH:]
    prow, nrow = _halo_rows(prev_ref, next_ref, i, nblk)
    prow = prow[:, CONV_CH:2 * CONV_CH] * prow[:, 2 * CONV_CH:]
    nrow = nrow[:, CONV_CH:2 * CONV_CH] * nrow[:, 2 * CONV_CH:]
    up, un = _shift_rows(u, prow, nrow)
    cw = cw_ref[...]
    ya = pc[:, :CONV_CH] * (up * cw[0:1, :] + u * cw[1:2, :] + un * cw[2:3, :])
    o = of_ref[...] + ob_ref[...]
    ms = _group_sum(o * o, _head_blockdiag(DN_DIM, DN_HEAD_DIM)) * (1.0 / DN_HEAD_DIM)
    z = z_ref[...]
    yb = o * lax.rsqrt(ms + EPS) * ng_ref[...] * (z * _sigmoid(z))
    mix = jnp.concatenate([ya, yb, yc_ref[...]], axis=1)
    x = h_ref[...] + row[:, 2 * D:3 * D] * _mm(mix, wout_ref[...])
    x_ref[...] = x
    hx = _rmsnorm_rows(x) * (1.0 + row[:, 4 * D:5 * D]) + row[:, 3 * D:4 * D]
    hx_ref[...] = hx.astype(BF16)
    if with_router:
        h1, h0 = _split2(hx)
        d = functools.partial(lax.dot_general, dimension_numbers=_NN, preferred_element_type=F32)
        lg = d(h1, wr1_ref[...]) + (d(h0, wr1_ref[...]) + d(h1, wr0_ref[...]))
        lane = lax.broadcasted_iota(jnp.int32, lg.shape, 1)
        lanef = lane.astype(F32)
        lg = jnp.where(lane < N_EXPERTS, lg, -jnp.inf)
        m1 = jnp.max(lg, axis=-1, keepdims=True)
        i1 = jnp.min(jnp.where(lg == m1, lanef, float(LANE)), axis=-1, keepdims=True)
        rest = jnp.where(lanef == i1, -jnp.inf, lg)
        m2 = jnp.max(rest, axis=-1, keepdims=True)
        i2 = jnp.min(jnp.where(rest == m2, lanef, float(LANE)), axis=-1, keepdims=True)
        e2 = jnp.exp(m2 - m1)
        g1 = 1.0 / (1.0 + e2)
        lg_ref[...] = jnp.where(lane == 0, g1, jnp.where(lane == 1, e2 * g1, jnp.where(lane == 2, i1, i2)))


def _mixer_finish(h, mod, pconv, conv_w, o_f, o_b, pz, norm_g, yc, w_out, ctx_len, with_ctx, router=None):
    t = h.shape[0]
    nblk = t // TM
    blk0 = 0 if with_ctx else ctx_len // TM
    rows = t - blk0 * TM
    r8 = TM // 8
    w = pconv.shape[1]
    cur = lambda i: (i + blk0, 0)
    out_cur = lambda i: (i, 0)
    const = lambda i: (0, 0)
    ng = jnp.tile(norm_g.reshape(1, DN_HEAD_DIM), (1, DN_HEADS))
    in_specs = [
        pl.BlockSpec((TM, D), cur),
        pl.BlockSpec((8, N_MOD * D), const),
        pl.BlockSpec((TM, w), cur),
        pl.BlockSpec((8, w), lambda i: (jnp.maximum((i + blk0) * r8 - 1, 0), 0)),
        pl.BlockSpec((8, w), lambda i: (jnp.minimum((i + blk0 + 1) * r8, t // 8 - 1), 0)),
        pl.BlockSpec((3, CONV_CH), const),
        pl.BlockSpec((TM, DN_DIM), cur),
        pl.BlockSpec((TM, DN_DIM), cur),
        pl.BlockSpec((TM, DN_DIM), cur),
        pl.BlockSpec((1, DN_DIM), const),
        pl.BlockSpec((TM, ATT_DIM), cur),
        pl.BlockSpec((MIX_DIM, D), const),
    ]
    args = [h, mod, pconv, pconv, pconv, conv_w, o_f, o_b, pz, ng, yc, w_out]
    out_specs = [pl.BlockSpec((TM, D), out_cur), pl.BlockSpec((TM, D), out_cur)]
    out_shape = [jax.ShapeDtypeStruct((rows, D), F32), jax.ShapeDtypeStruct((rows, D), BF16)]
    if router is not None:
        in_specs += [pl.BlockSpec((D, LANE), const)] * 2
        args += list(router)
        out_specs.append(pl.BlockSpec((TM, LANE), out_cur))
        out_shape.append(jax.ShapeDtypeStruct((rows, LANE), F32))
    return pl.pallas_call(
        functools.partial(_mixfin_kernel, nblk=nblk, blk0=blk0, with_router=router is not None),
        grid=(rows // TM,),
        in_specs=in_specs,
        out_specs=out_specs,
        out_shape=out_shape,
        compiler_params=_params(("arbitrary",), VMEM_LIMIT),
        name="mixer_finish",
    )(*args)


def _ffn_kernel(hx_ref, x_ref, mod_ref, wg_ref, wu_ref, wd_ref, o_ref):
    row = _mod_row(mod_ref, pl.program_id(0) == 0)
    hx = hx_ref[...]
    acc = jnp.zeros((hx.shape[0], D), F32)
    for f in range(0, D_FF, FF_CHUNK):
        g = _mm(hx, wg_ref[:, f:f + FF_CHUNK])
        u = _mm(hx, wu_ref[:, f:f + FF_CHUNK])
        acc = acc + _mm(g * _sigmoid(g) * u, wd_ref[f:f + FF_CHUNK, :])
    o_ref[...] = x_ref[...] + row[:, 5 * D:6 * D] * acc


def _dense_ffn(hx, x, mod, wg, wu, wd):
    t = x.shape[0]
    const = lambda i: (0, 0)
    once = dict(pipeline_mode=pl.Buffered(1))
    return pl.pallas_call(
        _ffn_kernel,
        grid=(t // TM,),
        in_specs=[
            pl.BlockSpec((TM, D), lambda i: (i, 0)),
            pl.BlockSpec((TM, D), lambda i: (i, 0)),
            pl.BlockSpec((8, N_MOD * D), const),
            pl.BlockSpec((D, D_FF), const, **once),
            pl.BlockSpec((D, D_FF), const, **once),
            pl.BlockSpec((D_FF, D), const, **once),
        ],
        out_specs=pl.BlockSpec((TM, D), lambda i: (i, 0)),
        out_shape=jax.ShapeDtypeStruct((t, D), F32),
        compiler_params=_params(("arbitrary",), VMEM_LIMIT),
        name="dense_ffn",
    )(hx, x, mod, wg, wu, wd)


def _moe_kernel(be_ref, nu_ref, xs_ref, wg_ref, wu_ref, wd_ref, y_ref, acc_ref):
    b = pl.program_id(0)
    f = pl.program_id(1)

    @pl.when(b < nu_ref[0])
    def _():
        xs = xs_ref[...]
        part = jnp.zeros((MOE_TM, D), F32)
        for c in range(0, MOE_TF, MOE_SUB):
            g = _mm(xs, wg_ref[0, :, c:c + MOE_SUB])
            u = _mm(xs, wu_ref[0, :, c:c + MOE_SUB])
            part = part + _mm(g * _sigmoid(g) * u, wd_ref[0, c:c + MOE_SUB, :])

        @pl.when(f == 0)
        def _():
            acc_ref[...] = part

        @pl.when(f != 0)
        def _():
            acc_ref[...] += part

        @pl.when(f == pl.num_programs(1) - 1)
        def _():
            y_ref[...] = acc_ref[...]

    @pl.when(b >= nu_ref[0])
    def _():
        y_ref[...] = jnp.zeros(y_ref.shape, F32)


def _moe_experts(xs, blk_e, n_used, wg, wu, wd):
    cap = xs.shape[0]
    nblk = cap // MOE_TM
    nf = D_FF_EXPERT // MOE_TF

    def fidx(b, f, nu):
        return jnp.where(b < nu[0], f, nf - 1)

    grid_spec = pltpu.PrefetchScalarGridSpec(
        num_scalar_prefetch=2,
        grid=(nblk, nf),
        in_specs=[
            pl.BlockSpec((MOE_TM, D), lambda b, f, be, nu: (b, 0)),
            pl.BlockSpec((1, D, MOE_TF), lambda b, f, be, nu: (be[b], 0, fidx(b, f, nu))),
            pl.BlockSpec((1, D, MOE_TF), lambda b, f, be, nu: (be[b], 0, fidx(b, f, nu))),
            pl.BlockSpec((1, MOE_TF, D), lambda b, f, be, nu: (be[b], fidx(b, f, nu), 0)),
        ],
        out_specs=pl.BlockSpec((MOE_TM, D), lambda b, f, be, nu: (b, 0)),
        scratch_shapes=[pltpu.VMEM((MOE_TM, D), F32)],
    )
    return pl.pallas_call(
        _moe_kernel,
        grid_spec=grid_spec,
        out_shape=jax.ShapeDtypeStruct((cap, D), F32),
        compiler_params=_params(("arbitrary", "arbitrary"), VMEM_LIMIT),
        name="moe_experts",
    )(blk_e, n_used, xs, wg, wu, wd)


def _moe_route(top_e):
    n = top_e.shape[0]
    a = n * TOP_K
    flat_e = top_e.reshape(a)
    onehot = (flat_e[:, None] == jnp.arange(N_EXPERTS, dtype=flat_e.dtype)[None, :]).astype(jnp.int32)
    counts = jnp.sum(onehot, axis=0)
    padded = (counts + MOE_TM - 1) // MOE_TM * MOE_TM
    pad_ends = jnp.cumsum(padded)
    pad_starts = pad_ends - padded
    dest = jnp.sum(onehot * (jnp.cumsum(onehot, axis=0) - onehot + pad_starts[None, :]), axis=1)
    cap = a + N_EXPERTS * MOE_TM
    nblk = cap // MOE_TM
    row_tok = jnp.zeros((cap,), jnp.int32).at[dest].set(jnp.arange(a, dtype=jnp.int32) // TOP_K)
    blk_start = jnp.arange(nblk, dtype=jnp.int32) * MOE_TM
    blk_e = jnp.minimum(jnp.sum((pad_ends[None, :] <= blk_start[:, None]).astype(jnp.int32), axis=1),
                        N_EXPERTS - 1)
    n_used = (pad_ends[-1] // MOE_TM).astype(jnp.int32).reshape(1)
    last_e = blk_e[jnp.maximum(n_used[0] - 1, 0)]
    blk_e = jnp.where(jnp.arange(nblk) < n_used[0], blk_e, last_e)
    return dest.reshape(n, TOP_K), row_tok, blk_e, n_used


def _final_kernel(x_ref, y0_ref, y1_ref, gt_ref, mod_ref, fg_ref, o_ref):
    mod = mod_ref[...]
    gt = gt_ref[...]
    f = gt[:, 0:1] * y0_ref[...] + gt[:, 1:2] * y1_ref[...]
    x = x_ref[...] + mod[0:1, 5 * D:6 * D] * f
    o_ref[...] = _rmsnorm_rows(x) * fg_ref[...]


def _moe_combine_final(x, y0, y1, gt, mod, final_g):
    n = x.shape[0]
    row = lambda i: (i, 0)
    const = lambda i: (0, 0)
    return pl.pallas_call(
        _final_kernel,
        grid=(n // TM,),
        in_specs=[pl.BlockSpec((TM, D), row)] * 3
        + [pl.BlockSpec((TM, LANE), row), pl.BlockSpec((8, N_MOD * D), const), pl.BlockSpec((1, D), const)],
        out_specs=pl.BlockSpec((TM, D), row),
        out_shape=jax.ShapeDtypeStruct((n, D), F32),
        compiler_params=_params(("arbitrary",), VMEM_LIMIT),
        name="moe_combine_final",
    )(x, y0, y1, gt, mod, final_g.reshape(1, D))


def _rope_tables(n):
    pos = jnp.arange(n, dtype=jnp.int32)
    r = (pos // GRID_W).astype(F32)
    col = (pos % GRID_W).astype(F32)
    inv = ROPE_BASE ** (-jnp.arange(0, AXIS_DIM, 2, dtype=F32) / AXIS_DIM)
    ang = jnp.concatenate([r[:, None] * inv, r[:, None] * inv, col[:, None] * inv, col[:, None] * inv], axis=1)
    sign = jnp.tile(jnp.concatenate([-jnp.ones((AXIS_DIM // 2,), F32), jnp.ones((AXIS_DIM // 2,), F32)]), 2)
    cos = jnp.tile(jnp.cos(ang), (1, LANE // HEAD_DIM))
    sin = jnp.tile(jnp.sin(ang) * sign, (1, LANE // HEAD_DIM))
    return cos, sin


def _prep_w_in(w):
    main = jnp.concatenate([w[:, :_C_A], w[:, _C_Q:_C_END]], axis=1).astype(BF16)
    wab = jnp.zeros((D, LANE), F32).at[:, :N_AB].set(w[:, _C_A:_C_Q])
    wab1 = wab.astype(BF16)
    wab0 = (wab - wab1.astype(F32)).astype(BF16)
    return main, wab1, wab0


def kernel(x, c, ctx, c_ctx, w_mod, b_mod, w_in, w_out, conv_w, dn_conv_w, dn_a_log, dn_dt_bias, dn_norm_g,
           attn_sink, ffn_w_gate, ffn_w_up, ffn_w_down, moe_router, moe_w_gate, moe_w_up, moe_w_down,
           final_norm_g):
    bsz, n, d = x.shape
    ctx_len = ctx.shape[1]
    depth = w_in.shape[0]
    assert bsz == 1 and d == D and ctx_len == TM and n % TM == 0 and n % GRID_W == 0
    cos, sin = _rope_tables(n)
    mods = _mod_vectors(c, c_ctx, w_mod, b_mod)
    h = jnp.concatenate([ctx[0], x[0]], axis=0)
    for layer in range(depth):
        last = layer == depth - 1
        mod = mods[layer]
        w_main, wab1, wab0 = _prep_w_in(w_in[layer])
        pconv, pqkv, pz, pq, pkv, pab = _in_proj(h, mod, w_main, wab1, wab0)
        qn, kn, vv, gb = _dn_prep(pqkv, pab, dn_conv_w[layer], dn_a_log[layer], dn_dt_bias[layer])
        o_f, o_b = _delta_net(qn, kn, vv, gb, ctx_len)
        sink = jnp.zeros((8, LANE), F32).at[0, :ATT_HEADS].set(attn_sink[layer])
        yc = _attention(pq, pkv, cos, sin, sink, ctx_len, with_ctx=not last)
        router = None
        if layer % 2 == 1:
            wr = jnp.zeros((D, LANE), F32).at[:, :N_EXPERTS].set(moe_router[layer // 2])
            wr1 = wr.astype(BF16)
            router = (wr1, (wr - wr1.astype(F32)).astype(BF16))
        outs = _mixer_finish(h, mod, pconv, conv_w[layer], o_f, o_b, pz, dn_norm_g[layer], yc,
                             w_out[layer].astype(BF16), ctx_len, with_ctx=not last, router=router)
        if layer % 2 == 0:
            assert not last
            x1, hx = outs
            i = layer // 2
            h = _dense_ffn(hx, x1, mod, ffn_w_gate[i].astype(BF16), ffn_w_up[i].astype(BF16),
                           ffn_w_down[i].astype(BF16))
        else:
            assert last
            x1, hx, route = outs
            i = layer // 2
            dest, row_tok, blk_e, n_used = _moe_route(route[:, 2:2 + TOP_K].astype(jnp.int32))
            xs = jnp.take(hx, row_tok, axis=0)
            y = _moe_experts(xs, blk_e, n_used, moe_w_gate[i].astype(BF16), moe_w_up[i].astype(BF16),
                             moe_w_down[i].astype(BF16))
            y0 = jnp.take(y, dest[:, 0], axis=0)
            y1 = jnp.take(y, dest[:, 1], axis=0)
            h = _moe_combine_final(x1, y0, y1, route, mod, final_norm_g)
    return h.reshape(bsz, n, d)
```

```python
import functools

import jax
import jax.numpy as jnp
from jax import lax
from jax.experimental import pallas as pl
from jax.experimental.pallas import tpu as pltpu

F32 = jnp.float32
BF16 = jnp.bfloat16

D = 1024
N_MOD = 6
EPS = 1e-6
NEG = -1e30
GRID_W = 64

CONV_CH = 256
DN_HEADS = 6
DN_HEAD_DIM = 64
DN_DIM = DN_HEADS * DN_HEAD_DIM
DN_CHUNK = 64
DN_SUB = 16
DN_STEP_CHUNKS = 4
ATT_HEADS = 6
ATT_KV_HEADS = 2
ATT_GROUP = ATT_HEADS // ATT_KV_HEADS
HEAD_DIM = 64
ATT_DIM = ATT_HEADS * HEAD_DIM
ATT_KV_DIM = ATT_KV_HEADS * HEAD_DIM
ATT_BLOCK = 128
ROPE_BASE = 10000.0
AXIS_DIM = HEAD_DIM // 2
MIX_DIM = CONV_CH + DN_DIM + ATT_DIM

D_FF = 2816
N_EXPERTS = 8
TOP_K = 2
D_FF_EXPERT = 3584

TM = 256
FF_CHUNK = 256
MOE_TM = 512
MOE_TF = 1792
MOE_SUB = 256
LANE = 128
VMEM_LIMIT = 56 * 1024 * 1024

_C_QKV = 3 * CONV_CH
_C_Z = _C_QKV + 3 * DN_DIM
_C_A = _C_Z + DN_DIM
_C_Q = _C_A + 4 * DN_HEADS
_C_K = _C_Q + ATT_DIM
_C_V = _C_K + ATT_KV_DIM
_C_END = _C_V + ATT_KV_DIM
N_AB = 4 * DN_HEADS


def _params(sem=None, vmem=None):
    kw = {}
    if sem is not None:
        kw["dimension_semantics"] = sem
    if vmem is not None:
        kw["vmem_limit_bytes"] = vmem
    return pltpu.CompilerParams(**kw)


def _split2(a):
    hi = a.astype(BF16)
    lo = (a - hi.astype(F32)).astype(BF16)
    return hi, lo


def _split3(a):
    hi = a.astype(BF16)
    r = a - hi.astype(F32)
    mid = r.astype(BF16)
    lo = (r - mid.astype(F32)).astype(BF16)
    return hi, mid, lo


_NN = (((1,), (0,)), ((), ()))
_NT = (((1,), (1,)), ((), ()))
_TN = (((0,), (0,)), ((), ()))


def _mm(a, b, dims=_NN):
    return lax.dot_general(a.astype(BF16), b.astype(BF16), dims, preferred_element_type=F32)


def _mm3(a, b, dims=_NN):
    a1, a0 = _split2(a)
    b1, b0 = _split2(b)
    d = functools.partial(lax.dot_general, dimension_numbers=dims, preferred_element_type=F32)
    return d(a1, b1) + (d(a1, b0) + d(a0, b1))


_BNN = (((2,), (1,)), ((0,), (0,)))
_BNT = (((2,), (2,)), ((0,), (0,)))
_BTN = (((1,), (1,)), ((0,), (0,)))


def _bmm(a, b, dims=_BNN):
    return lax.dot_general(a.astype(BF16), b.astype(BF16), dims, preferred_element_type=F32)


def _bmm3(a, b, dims=_BNN):
    return _mm3(a, b, dims)


def _sigmoid(x):
    return 1.0 / (1.0 + jnp.exp(-x))


def _softplus(x):
    return jnp.maximum(x, 0.0) + jnp.log1p(jnp.exp(-jnp.abs(x)))


def _mod_row(mod_ref, is_ctx):
    mod = mod_ref[...]
    return jnp.where(is_ctx, mod[1:2, :], mod[0:1, :])


def _rmsnorm_rows(x):
    return x * lax.rsqrt(jnp.mean(x * x, axis=-1, keepdims=True) + EPS)


def _shift_rows(u, prow, nrow):
    n = u.shape[0]
    rid = lax.broadcasted_iota(jnp.int32, u.shape, 0)
    up = jnp.where(rid == 0, prow, pltpu.roll(u, 1, 0))
    un = jnp.where(rid == n - 1, nrow, pltpu.roll(u, n - 1, 0))
    return up, un


def _same_group(shape, group):
    sh = group.bit_length() - 1
    assert 1 << sh == group
    return (lax.broadcasted_iota(jnp.int32, shape, 0) >> sh) == (lax.broadcasted_iota(jnp.int32, shape, 1) >> sh)


def _head_blockdiag(n, group):
    return jnp.where(_same_group((n, n), group), 1.0, 0.0).astype(BF16)


def _group_sum(t, bd):
    hi, lo = _split2(t)
    d = functools.partial(lax.dot_general, dimension_numbers=_NN, preferred_element_type=F32)
    return d(hi, bd) + d(lo, bd)


MOD_TN = 1536


def _mod_kernel(s_ref, w_ref, b_ref, o_ref):
    s = s_ref[...]
    s = s * _sigmoid(s)
    o_ref[0] = _mm3(s, w_ref[0]) + b_ref[0]


def _mod_vectors(c, c_ctx, w_mod, b_mod):
    depth = w_mod.shape[0]
    s = jnp.zeros((8, D), F32).at[0].set(c[0]).at[1].set(c_ctx)
    return pl.pallas_call(
        _mod_kernel,
        grid=(depth, N_MOD * D // MOD_TN),
        in_specs=[
            pl.BlockSpec((8, D), lambda l, j: (0, 0)),
            pl.BlockSpec((1, D, MOD_TN), lambda l, j: (l, 0, j)),
            pl.BlockSpec((1, 1, MOD_TN), lambda l, j: (l, 0, j)),
        ],
        out_specs=pl.BlockSpec((1, 8, MOD_TN), lambda l, j: (l, 0, j)),
        out_shape=jax.ShapeDtypeStruct((depth, 8, N_MOD * D), F32),
        compiler_params=_params(("arbitrary", "arbitrary"), VMEM_LIMIT),
        name="mod_vectors",
    )(s, w_mod, b_mod.reshape(depth, 1, N_MOD * D))


def _in_kernel(h_ref, mod_ref, w_ref, wab1_ref, wab0_ref,
               pconv_ref, pqkv_ref, pz_ref, pq_ref, pkv_ref, pab_ref):
    row = _mod_row(mod_ref, pl.program_id(0) == 0)
    hm = _rmsnorm_rows(h_ref[...]) * (1.0 + row[:, D:2 * D]) + row[:, 0:D]
    h1, h0 = _split2(hm)
    d = functools.partial(lax.dot_general, dimension_numbers=_NN, preferred_element_type=F32)
    off = 0
    for ref in (pconv_ref, pqkv_ref, pz_ref, pq_ref, pkv_ref):
        w = ref.shape[1]
        ref[...] = d(h1, w_ref[:, off:off + w])
        off += w
    pab_ref[...] = d(h1, wab1_ref[...]) + (d(h0, wab1_ref[...]) + d(h1, wab0_ref[...]))


def _in_proj(h, mod, w_main, wab1, wab0):
    t = h.shape[0]
    widths = (3 * CONV_CH, 3 * DN_DIM, DN_DIM, ATT_DIM, 2 * ATT_KV_DIM, LANE)
    const = lambda i: (0, 0)
    return pl.pallas_call(
        _in_kernel,
        grid=(t // TM,),
        in_specs=[
            pl.BlockSpec((TM, D), lambda i: (i, 0)),
            pl.BlockSpec((8, N_MOD * D), const),
            pl.BlockSpec(w_main.shape, const),
            pl.BlockSpec(wab1.shape, const),
            pl.BlockSpec(wab0.shape, const),
        ],
        out_specs=[pl.BlockSpec((TM, w), lambda i: (i, 0)) for w in widths],
        out_shape=[jax.ShapeDtypeStruct((t, w), F32) for w in widths],
        compiler_params=_params(("arbitrary",), VMEM_LIMIT),
        name="in_proj",
    )(h, mod, w_main, wab1, wab0)


def _halo_rows(prev_ref, next_ref, i, nblk):
    pvalid = jnp.logical_and(i != 0, i != 1)
    nvalid = jnp.logical_and(i != 0, i != nblk - 1)
    prow = jnp.where(pvalid, prev_ref[7:8, :], 0.0)
    nrow = jnp.where(nvalid, next_ref[0:1, :], 0.0)
    return prow, nrow


def _dnprep_kernel(qkv_ref, prev_ref, next_ref, ab_ref, cw_ref, alog_ref, dtb_ref,
                   q_ref, k_ref, v_ref, gb_ref, *, nblk):
    i = pl.program_id(0)
    u = qkv_ref[...]
    prow, nrow = _halo_rows(prev_ref, next_ref, i, nblk)
    up, un = _shift_rows(u, prow, nrow)
    cw = cw_ref[...]
    y = up * cw[0:1, :] + u * cw[1:2, :] + un * cw[2:3, :]
    y = y * _sigmoid(y)
    q = y[:, 0:DN_DIM]
    k = y[:, DN_DIM:2 * DN_DIM]
    bd = _head_blockdiag(DN_DIM, DN_HEAD_DIM)
    q_ref[...] = q * lax.rsqrt(_group_sum(q * q, bd) + 1e-6) * (DN_HEAD_DIM ** -0.5)
    k_ref[...] = k * lax.rsqrt(_group_sum(k * k, bd) + 1e-6)
    v_ref[...] = y[:, 2 * DN_DIM:3 * DN_DIM]
    ab = ab_ref[...]
    g = -jnp.exp(alog_ref[...]) * _softplus(ab + dtb_ref[...])
    lane = lax.broadcasted_iota(jnp.int32, ab.shape, 1)
    gb_ref[...] = jnp.where(lane < 2 * DN_HEADS, g, _sigmoid(ab))


def _dn_prep(pqkv, pab, dn_conv_w, a_log, dt_bias):
    t, w = pqkv.shape
    nblk = t // TM
    alog = jnp.zeros((1, LANE), F32).at[0, :2 * DN_HEADS].set(a_log.reshape(-1))
    dtb = jnp.zeros((1, LANE), F32).at[0, :2 * DN_HEADS].set(dt_bias.reshape(-1))
    r8 = TM // 8
    const = lambda i: (0, 0)
    return pl.pallas_call(
        functools.partial(_dnprep_kernel, nblk=nblk),
        grid=(nblk,),
        in_specs=[
            pl.BlockSpec((TM, w), lambda i: (i, 0)),
            pl.BlockSpec((8, w), lambda i: (jnp.maximum(i * r8 - 1, 0), 0)),
            pl.BlockSpec((8, w), lambda i: (jnp.minimum((i + 1) * r8, t // 8 - 1), 0)),
            pl.BlockSpec((TM, LANE), lambda i: (i, 0)),
            pl.BlockSpec((3, w), const),
            pl.BlockSpec((1, LANE), const),
            pl.BlockSpec((1, LANE), const),
        ],
        out_specs=[pl.BlockSpec((TM, DN_DIM), lambda i: (i, 0))] * 3
        + [pl.BlockSpec((TM, LANE), lambda i: (i, 0))],
        out_shape=[jax.ShapeDtypeStruct((t, DN_DIM), F32)] * 3 + [jax.ShapeDtypeStruct((t, LANE), F32)],
        compiler_params=_params(("arbitrary",), VMEM_LIMIT),
        name="dn_prep",
    )(pqkv, pqkv, pqkv, pab, dn_conv_w, alog, dtb)


def _dn_chunk(rev, q_ref, k_ref, v_ref, gb_ref, o_ref, s_ref):
    c_ = DN_CHUNK
    ri = lax.broadcasted_iota(jnp.int32, (c_, c_), 0)
    ci = lax.broadcasted_iota(jnp.int32, (c_, c_), 1)
    incl = (ri <= ci) if rev else (ri >= ci)
    strict = (ri < ci) if rev else (ri > ci)
    same_sub = _same_group((c_, c_), DN_SUB)
    eye = jnp.where(ri == ci, 1.0, 0.0)
    tri = jnp.where(incl, 1.0, 0.0).astype(BF16)
    last = 0 if rev else c_ - 1

    nchunks = q_ref.shape[0] // c_
    nh = DN_HEADS
    col0 = nh if rev else 0
    rows = lambda g: slice(g * c_, (g + 1) * c_)
    lanes = lambda h: slice(h * DN_HEAD_DIM, (h + 1) * DN_HEAD_DIM)

    def stack(fn):
        return jnp.stack([fn(g, h) for g in range(nchunks) for h in range(nh)])

    gb = gb_ref[...]
    gcs = [_cumsum_rows(tri, gb[rows(g), :]) for g in range(nchunks)]
    gcts = [gc.T for gc in gcs]
    q = stack(lambda g, h: q_ref[rows(g), lanes(h)])
    k = stack(lambda g, h: k_ref[rows(g), lanes(h)])
    v = stack(lambda g, h: v_ref[rows(g), lanes(h)])
    gcol = stack(lambda g, h: gcs[g][:, col0 + h:col0 + h + 1])
    grow = stack(lambda g, h: gcts[g][col0 + h:col0 + h + 1, :])
    beta = stack(lambda g, h: gb[rows(g), 2 * nh + col0 + h:2 * nh + col0 + h + 1])
    glast = gcol[:, last:last + 1, :]
    decay = jnp.where(incl, jnp.exp(jnp.where(incl, gcol - grow, 0.0)), 0.0)
    eg = jnp.exp(gcol)
    kb = k * beta
    a = jnp.where(strict, _bmm(kb, k, _BNT) * decay, 0.0)
    qk = jnp.where(incl, _bmm(q, k, _BNT) * decay, 0.0)
    ad = jnp.where(same_sub, a, 0.0)
    ao = a - ad
    p = eye - ad
    n2 = _bmm(ad, ad)
    p = p + _bmm(p, n2)
    n4 = _bmm(n2, n2)
    p = p + _bmm(p, n4)
    n8 = _bmm(n4, n4)
    dinv = p + _bmm(p, n8)
    m = _bmm3(dinv, ao)
    m2 = _bmm(m, m)
    y = _bmm3(dinv, jnp.concatenate([v * beta, kb * eg], axis=-1))
    z = y + _bmm(m2, y)
    x = z - _bmm3(m, z)
    u = x[:, :, :DN_HEAD_DIM]
    w = x[:, :, DN_HEAD_DIM:]
    qg = q * eg
    kd = k * jnp.exp(glast - gcol)
    gl = jnp.exp(glast)
    s = s_ref[col0:col0 + nh]
    for g in (reversed(range(nchunks)) if rev else range(nchunks)):
        b = slice(g * nh, (g + 1) * nh)
        v_new = u[b] - _bmm(w[b], s)
        o = _bmm(qg[b], s) + _bmm(qk[b], v_new)
        s = s * gl[b] + _bmm(kd[b], v_new, _BTN)
        for h in range(nh):
            o_ref[rows(g), lanes(h)] = o[h]
    s_ref[col0:col0 + nh] = s


def _cumsum_rows(tri_bf16, g):
    g2, g1, g0 = _split3(g)
    d = functools.partial(lax.dot_general, dimension_numbers=_NN, preferred_element_type=F32)
    return d(tri_bf16, g2) + (d(tri_bf16, g1) + d(tri_bf16, g0))


def _dn_kernel(qf, kf, vf, gf, qb, kb, vb, gbb, of_ref, ob_ref, s_ref):
    @pl.when(pl.program_id(0) == 0)
    def _():
        s_ref[...] = jnp.zeros(s_ref.shape, F32)

    _dn_chunk(False, qf, kf, vf, gf, of_ref, s_ref)
    _dn_chunk(True, qb, kb, vb, gbb, ob_ref, s_ref)


def _delta_net(q, k, v, gb, ctx_len):
    t = q.shape[0]
    rows = DN_STEP_CHUNKS * DN_CHUNK
    assert ctx_len == rows and t % rows == 0
    nstep = t // rows

    def fwd(s):
        return (s, 0)

    def bwd(s):
        return (jnp.where(s == 0, 0, nstep - s), 0)

    wide = lambda im: pl.BlockSpec((rows, DN_DIM), im)
    narrow = lambda im: pl.BlockSpec((rows, LANE), im)
    return pl.pallas_call(
        _dn_kernel,
        grid=(nstep,),
        in_specs=[wide(fwd), wide(fwd), wide(fwd), narrow(fwd), wide(bwd), wide(bwd), wide(bwd), narrow(bwd)],
        out_specs=[wide(fwd), wide(bwd)],
        out_shape=[jax.ShapeDtypeStruct((t, DN_DIM), F32)] * 2,
        scratch_shapes=[pltpu.VMEM((2 * DN_HEADS, DN_HEAD_DIM, DN_HEAD_DIM), F32)],
        compiler_params=_params(("arbitrary",), VMEM_LIMIT),
        name="delta_net",
    )(q, k, v, gb, q, k, v, gb)


def _rope(x, cos, sin):
    w = x.shape[1]
    lane = lax.broadcasted_iota(jnp.int32, x.shape, 1)
    first_half = (lane & (AXIS_DIM - 1)) < (AXIS_DIM // 2)
    swapped = jnp.where(first_half, pltpu.roll(x, w - AXIS_DIM // 2, 1), pltpu.roll(x, AXIS_DIM // 2, 1))
    return x * cos + swapped * sin


def _softmax_av(s, sink, vals):
    m = jnp.maximum(jnp.max(s, axis=-1, keepdims=True), sink)
    p = jnp.exp(s - m)
    denom = jnp.sum(p, axis=-1, keepdims=True) + jnp.exp(sink - m)
    return _mm(p, vals) / denom


def _attend(q, keys, vals, valid, sink_all, o_ref):
    b = q.shape[0]
    for kvh in range(ATT_KV_HEADS):
        kl = slice(kvh * HEAD_DIM, (kvh + 1) * HEAD_DIM)
        heads = range(kvh * ATT_GROUP, (kvh + 1) * ATT_GROUP)
        qs = jnp.concatenate([q[:, h * HEAD_DIM:(h + 1) * HEAD_DIM] for h in heads], axis=0)
        sink = jnp.concatenate([jnp.broadcast_to(sink_all[0:1, h:h + 1], (b, 1)) for h in heads], axis=0)
        s = _mm(qs, keys[:, kl], _NT)
        if valid is not None:
            s = jnp.where(valid, s, NEG)
        o = _softmax_av(s, sink, vals[:, kl])
        for g, h in enumerate(heads):
            o_ref[:, h * HEAD_DIM:(h + 1) * HEAD_DIM] = o[g * b:(g + 1) * b, :]


def _attn_kernel(q_ref, kp_ref, kc_ref, kn_ref, kctx_ref, cq_ref, sq_ref, cp_ref, sp_ref, cn_ref, sn_ref,
                 sink_ref, o_ref, *, nb, off, with_ctx):
    j = pl.program_id(0)
    b = ATT_BLOCK
    scale = HEAD_DIM ** -0.5

    @pl.when(j < off)
    def _():
        if with_ctx:
            kvx = kctx_ref[...]
            _attend(q_ref[...] * scale, kvx[:, :ATT_KV_DIM], kvx[:, ATT_KV_DIM:], None, sink_ref[...], o_ref)
        else:
            o_ref[...] = jnp.zeros(o_ref.shape, F32)

    @pl.when(j >= off)
    def _():
        i = j - off
        cq = cq_ref[...]
        sq = sq_ref[...]
        q = q_ref[...]
        q = jnp.concatenate([_rope(q[:, l * LANE:(l + 1) * LANE], cq, sq) for l in range(ATT_DIM // LANE)], axis=1)
        kvp, kvc, kvn, kvx = kp_ref[...], kc_ref[...], kn_ref[...], kctx_ref[...]
        kp = _rope(kvp[:, :ATT_KV_DIM], cp_ref[...], sp_ref[...])
        kc = _rope(kvc[:, :ATT_KV_DIM], cq, sq)
        kn = _rope(kvn[:, :ATT_KV_DIM], cn_ref[...], sn_ref[...])
        keys = jnp.concatenate([kp, kc, kn, kvx[:, :ATT_KV_DIM]], axis=0)
        vals = jnp.concatenate(
            [kvp[:, ATT_KV_DIM:], kvc[:, ATT_KV_DIM:], kvn[:, ATT_KV_DIM:], kvx[:, ATT_KV_DIM:]], axis=0)
        nk = keys.shape[0]
        c = lax.broadcasted_iota(jnp.int32, (1, nk), 1)
        far = 4 * nk
        ccol = jnp.where(c >= 3 * b, b + ((c - 3 * b) & (b - 1)), c)
        ccol = jnp.where(jnp.logical_and(c < b, i == 0), -far, ccol)
        ccol = jnp.where(jnp.logical_and(jnp.logical_and(c >= 2 * b, c < 3 * b), i == nb - 1), far, ccol)
        r = lax.broadcasted_iota(jnp.int32, (ATT_GROUP * b, 1), 0) & (b - 1)
        valid = lax.bitcast_convert_type(ccol - r, jnp.uint32) <= jnp.uint32(2 * b)
        _attend(q * scale, keys, vals, valid, sink_ref[...], o_ref)


def _attention(pq, pkv, cos, sin, sink, ctx_len, with_ctx):
    t = pq.shape[0]
    n = t - ctx_len
    nb = n // ATT_BLOCK
    off = ctx_len // ATT_BLOCK
    kv = lambda im: pl.BlockSpec((ATT_BLOCK, 2 * ATT_KV_DIM), im)
    tab = lambda im: pl.BlockSpec((ATT_BLOCK, LANE), im)
    lat = lambda j: jnp.maximum(j - off, 0)
    cur = lambda j: (j, 0)
    prv = lambda j: (jnp.maximum(lat(j) - 1, 0) + off, 0)
    nxt = lambda j: (jnp.minimum(lat(j) + 1, nb - 1) + off, 0)
    tcur = lambda j: (lat(j), 0)
    tprv = lambda j: (jnp.maximum(lat(j) - 1, 0), 0)
    tnxt = lambda j: (jnp.minimum(lat(j) + 1, nb - 1), 0)
    return pl.pallas_call(
        functools.partial(_attn_kernel, nb=nb, off=off, with_ctx=with_ctx),
        grid=(off + nb,),
        in_specs=[
            pl.BlockSpec((ATT_BLOCK, ATT_DIM), cur),
            kv(prv), kv(cur), kv(nxt),
            pl.BlockSpec((ctx_len, 2 * ATT_KV_DIM), lambda j: (0, 0)),
            tab(tcur), tab(tcur), tab(tprv), tab(tprv), tab(tnxt), tab(tnxt),
            pl.BlockSpec((8, LANE), lambda j: (0, 0)),
        ],
        out_specs=pl.BlockSpec((ATT_BLOCK, ATT_DIM), cur),
        out_shape=jax.ShapeDtypeStruct((t, ATT_DIM), F32),
        compiler_params=_params(("arbitrary",), VMEM_LIMIT),
        name="attention",
    )(pq, pkv, pkv, pkv, pkv, cos, sin, cos, sin, cos, sin, sink)


def _mixfin_kernel(h_ref, mod_ref, pconv_ref, prev_ref, next_ref, cw_ref, of_ref, ob_ref, z_ref, ng_ref, yc_ref,
                   wout_ref, *rest, nblk, blk0, with_router):
    if with_router:
        wr1_ref, wr0_ref, x_ref, hx_ref, lg_ref = rest
    else:
        x_ref, hx_ref = rest
    i = pl.program_id(0) + blk0
    row = _mod_row(mod_ref, i == 0)
    pc = pconv_ref[...]
    u = pc[:, CONV_CH:2 * CONV_CH] * pc[:, 2 * CONV_CH:]
    prow, nrow = _halo_rows(prev_ref, next_ref, i, nblk)
    prow = prow[:, CONV_CH:2 * CONV_CH] * prow[:, 2 * CONV_CH:]
    nrow = nrow[:, CONV_CH:2 * CONV_CH] * nrow[:, 2 * CONV_CH:]
    up, un = _shift_rows(u, prow, nrow)
    cw = cw_ref[...]
    ya = pc[:, :CONV_CH] * (up * cw[0:1, :] + u * cw[1:2, :] + un * cw[2:3, :])
    o = of_ref[...] + ob_ref[...]
    ms = _group_sum(o * o, _head_blockdiag(DN_DIM, DN_HEAD_DIM)) * (1.0 / DN_HEAD_DIM)
    z = z_ref[...]
    yb = o * lax.rsqrt(ms + EPS) * ng_ref[...] * (z * _sigmoid(z))
    mix = jnp.concatenate([ya, yb, yc_ref[...]], axis=1)
    x = h_ref[...] + row[:, 2 * D:3 * D] * _mm(mix, wout_ref[...])
    x_ref[...] = x
    hx = _rmsnorm_rows(x) * (1.0 + row[:, 4 * D:5 * D]) + row[:, 3 * D:4 * D]
    hx_ref[...] = hx.astype(hx_ref.dtype)
    if with_router:
        h1, h0 = _split2(hx)
        d = functools.partial(lax.dot_general, dimension_numbers=_NN, preferred_element_type=F32)
        lg = d(h1, wr1_ref[...]) + (d(h0, wr1_ref[...]) + d(h1, wr0_ref[...]))
        lane = lax.broadcasted_iota(jnp.int32, lg.shape, 1)
        lanef = lane.astype(F32)
        lg = jnp.where(lane < N_EXPERTS, lg, -jnp.inf)
        m1 = jnp.max(lg, axis=-1, keepdims=True)
        i1 = jnp.min(jnp.where(lg == m1, lanef, float(LANE)), axis=-1, keepdims=True)
        rest = jnp.where(lanef == i1, -jnp.inf, lg)
        m2 = jnp.max(rest, axis=-1, keepdims=True)
        i2 = jnp.min(jnp.where(rest == m2, lanef, float(LANE)), axis=-1, keepdims=True)
        e2 = jnp.exp(m2 - m1)
        g1 = 1.0 / (1.0 + e2)
        lg_ref[...] = jnp.where(lane == 0, g1, jnp.where(lane == 1, e2 * g1, jnp.where(lane == 2, i1, i2)))


def _mixer_finish(h, mod, pconv, conv_w, o_f, o_b, pz, norm_g, yc, w_out, ctx_len, with_ctx, router=None):
    t = h.shape[0]
    nblk = t // TM
    blk0 = 0 if with_ctx else ctx_len // TM
    rows = t - blk0 * TM
    r8 = TM // 8
    w = pconv.shape[1]
    cur = lambda i: (i + blk0, 0)
    out_cur = lambda i: (i, 0)
    const = lambda i: (0, 0)
    ng = jnp.tile(norm_g.reshape(1, DN_HEAD_DIM), (1, DN_HEADS))
    in_specs = [
        pl.BlockSpec((TM, D), cur),
        pl.BlockSpec((8, N_MOD * D), const),
        pl.BlockSpec((TM, w), cur),
        pl.BlockSpec((8, w), lambda i: (jnp.maximum((i + blk0) * r8 - 1, 0), 0)),
        pl.BlockSpec((8, w), lambda i: (jnp.minimum((i + blk0 + 1) * r8, t // 8 - 1), 0)),
        pl.BlockSpec((3, CONV_CH), const),
        pl.BlockSpec((TM, DN_DIM), cur),
        pl.BlockSpec((TM, DN_DIM), cur),
        pl.BlockSpec((TM, DN_DIM), cur),
        pl.BlockSpec((1, DN_DIM), const),
        pl.BlockSpec((TM, ATT_DIM), cur),
        pl.BlockSpec((MIX_DIM, D), const),
    ]
    args = [h, mod, pconv, pconv, pconv, conv_w, o_f, o_b, pz, ng, yc, w_out]
    out_specs = [pl.BlockSpec((TM, D), out_cur), pl.BlockSpec((TM, D), out_cur)]
    hx_dtype = BF16 if router is None else F32
    out_shape = [jax.ShapeDtypeStruct((rows, D), F32), jax.ShapeDtypeStruct((rows, D), hx_dtype)]
    if router is not None:
        in_specs += [pl.BlockSpec((D, LANE), const)] * 2
        args += list(router)
        out_specs.append(pl.BlockSpec((TM, LANE), out_cur))
        out_shape.append(jax.ShapeDtypeStruct((rows, LANE), F32))
    return pl.pallas_call(
        functools.partial(_mixfin_kernel, nblk=nblk, blk0=blk0, with_router=router is not None),
        grid=(rows // TM,),
        in_specs=in_specs,
        out_specs=out_specs,
        out_shape=out_shape,
        compiler_params=_params(("arbitrary",), VMEM_LIMIT),
        name="mixer_finish",
    )(*args)


def _ffn_kernel(hx_ref, x_ref, mod_ref, wg_ref, wu_ref, wd_ref, o_ref):
    row = _mod_row(mod_ref, pl.program_id(0) == 0)
    hx = hx_ref[...]
    acc = jnp.zeros((hx.shape[0], D), F32)
    for f in range(0, D_FF, FF_CHUNK):
        g = _mm(hx, wg_ref[:, f:f + FF_CHUNK])
        u = _mm(hx, wu_ref[:, f:f + FF_CHUNK])
        acc = acc + _mm(g * _sigmoid(g) * u, wd_ref[f:f + FF_CHUNK, :])
    o_ref[...] = x_ref[...] + row[:, 5 * D:6 * D] * acc


def _dense_ffn(hx, x, mod, wg, wu, wd):
    t = x.shape[0]
    const = lambda i: (0, 0)
    once = dict(pipeline_mode=pl.Buffered(1))
    return pl.pallas_call(
        _ffn_kernel,
        grid=(t // TM,),
        in_specs=[
            pl.BlockSpec((TM, D), lambda i: (i, 0)),
            pl.BlockSpec((TM, D), lambda i: (i, 0)),
            pl.BlockSpec((8, N_MOD * D), const),
            pl.BlockSpec((D, D_FF), const, **once),
            pl.BlockSpec((D, D_FF), const, **once),
            pl.BlockSpec((D_FF, D), const, **once),
        ],
        out_specs=pl.BlockSpec((TM, D), lambda i: (i, 0)),
        out_shape=jax.ShapeDtypeStruct((t, D), F32),
        compiler_params=_params(("arbitrary",), VMEM_LIMIT),
        name="dense_ffn",
    )(hx, x, mod, wg, wu, wd)


def _moe_kernel(be_ref, nu_ref, xs_ref, wg_ref, wu_ref, wd_ref, y_ref, acc_ref):
    b = pl.program_id(0)
    f = pl.program_id(1)

    @pl.when(b < nu_ref[0])
    def _():
        xs = xs_ref[...].astype(BF16)
        part = jnp.zeros((MOE_TM, D), F32)
        for c in range(0, MOE_TF, MOE_SUB):
            g = _mm(xs, wg_ref[0, :, c:c + MOE_SUB])
            u = _mm(xs, wu_ref[0, :, c:c + MOE_SUB])
            part = part + _mm(g * _sigmoid(g) * u, wd_ref[0, c:c + MOE_SUB, :])

        @pl.when(f == 0)
        def _():
            acc_ref[...] = part

        @pl.when(f != 0)
        def _():
            acc_ref[...] += part

        @pl.when(f == pl.num_programs(1) - 1)
        def _():
            y_ref[...] = acc_ref[...]

    @pl.when(b >= nu_ref[0])
    def _():
        y_ref[...] = jnp.zeros(y_ref.shape, F32)


def _moe_experts(xs, blk_e, n_used, wg, wu, wd):
    cap = xs.shape[0]
    nblk = cap // MOE_TM
    nf = D_FF_EXPERT // MOE_TF

    def fidx(b, f, nu):
        return jnp.where(b < nu[0], f, nf - 1)

    grid_spec = pltpu.PrefetchScalarGridSpec(
        num_scalar_prefetch=2,
        grid=(nblk, nf),
        in_specs=[
            pl.BlockSpec((MOE_TM, D), lambda b, f, be, nu: (b, 0)),
            pl.BlockSpec((1, D, MOE_TF), lambda b, f, be, nu: (be[b], 0, fidx(b, f, nu))),
            pl.BlockSpec((1, D, MOE_TF), lambda b, f, be, nu: (be[b], 0, fidx(b, f, nu))),
            pl.BlockSpec((1, MOE_TF, D), lambda b, f, be, nu: (be[b], fidx(b, f, nu), 0)),
        ],
        out_specs=pl.BlockSpec((MOE_TM, D), lambda b, f, be, nu: (b, 0)),
        scratch_shapes=[pltpu.VMEM((MOE_TM, D), F32)],
    )
    return pl.pallas_call(
        _moe_kernel,
        grid_spec=grid_spec,
        out_shape=jax.ShapeDtypeStruct((cap, D), F32),
        compiler_params=_params(("arbitrary", "arbitrary"), VMEM_LIMIT),
        name="moe_experts",
    )(blk_e, n_used, xs, wg, wu, wd)


def _moe_route(top_e):
    n = top_e.shape[0]
    a = n * TOP_K
    flat_e = top_e.reshape(a)
    onehot = (flat_e[:, None] == jnp.arange(N_EXPERTS, dtype=flat_e.dtype)[None, :]).astype(jnp.int32)
    counts = jnp.sum(onehot, axis=0)
    padded = (counts + MOE_TM - 1) // MOE_TM * MOE_TM
    pad_ends = jnp.cumsum(padded)
    pad_starts = pad_ends - padded
    dest = jnp.sum(onehot * (jnp.cumsum(onehot, axis=0) - onehot + pad_starts[None, :]), axis=1)
    cap = a + N_EXPERTS * MOE_TM
    nblk = cap // MOE_TM
    row_tok = jnp.zeros((cap,), jnp.int32).at[dest].set(jnp.arange(a, dtype=jnp.int32) // TOP_K)
    blk_start = jnp.arange(nblk, dtype=jnp.int32) * MOE_TM
    blk_e = jnp.minimum(jnp.sum((pad_ends[None, :] <= blk_start[:, None]).astype(jnp.int32), axis=1),
                        N_EXPERTS - 1)
    n_used = (pad_ends[-1] // MOE_TM).astype(jnp.int32).reshape(1)
    last_e = blk_e[jnp.maximum(n_used[0] - 1, 0)]
    blk_e = jnp.where(jnp.arange(nblk) < n_used[0], blk_e, last_e)
    return dest.reshape(n, TOP_K), row_tok, blk_e, n_used


def _final_kernel(x_ref, y0_ref, y1_ref, gt_ref, mod_ref, fg_ref, o_ref):
    mod = mod_ref[...]
    gt = gt_ref[...]
    f = gt[:, 0:1] * y0_ref[...] + gt[:, 1:2] * y1_ref[...]
    x = x_ref[...] + mod[0:1, 5 * D:6 * D] * f
    o_ref[...] = _rmsnorm_rows(x) * fg_ref[...]


def _moe_combine_final(x, y0, y1, gt, mod, final_g):
    n = x.shape[0]
    row = lambda i: (i, 0)
    const = lambda i: (0, 0)
    return pl.pallas_call(
        _final_kernel,
        grid=(n // TM,),
        in_specs=[pl.BlockSpec((TM, D), row)] * 3
        + [pl.BlockSpec((TM, LANE), row), pl.BlockSpec((8, N_MOD * D), const), pl.BlockSpec((1, D), const)],
        out_specs=pl.BlockSpec((TM, D), row),
        out_shape=jax.ShapeDtypeStruct((n, D), F32),
        compiler_params=_params(("arbitrary",), VMEM_LIMIT),
        name="moe_combine_final",
    )(x, y0, y1, gt, mod, final_g.reshape(1, D))


def _rope_tables(n):
    pos = jnp.arange(n, dtype=jnp.int32)
    r = (pos // GRID_W).astype(F32)
    col = (pos % GRID_W).astype(F32)
    inv = ROPE_BASE ** (-jnp.arange(0, AXIS_DIM, 2, dtype=F32) / AXIS_DIM)
    ang = jnp.concatenate([r[:, None] * inv, r[:, None] * inv, col[:, None] * inv, col[:, None] * inv], axis=1)
    sign = jnp.tile(jnp.concatenate([-jnp.ones((AXIS_DIM // 2,), F32), jnp.ones((AXIS_DIM // 2,), F32)]), 2)
    cos = jnp.tile(jnp.cos(ang), (1, LANE // HEAD_DIM))
    sin = jnp.tile(jnp.sin(ang) * sign, (1, LANE // HEAD_DIM))
    return cos, sin


def _prep_w_in(w):
    main = jnp.concatenate([w[:, :_C_A], w[:, _C_Q:_C_END]], axis=1).astype(BF16)
    wab = jnp.zeros((D, LANE), F32).at[:, :N_AB].set(w[:, _C_A:_C_Q])
    wab1 = wab.astype(BF16)
    wab0 = (wab - wab1.astype(F32)).astype(BF16)
    return main, wab1, wab0


def kernel(x, c, ctx, c_ctx, w_mod, b_mod, w_in, w_out, conv_w, dn_conv_w, dn_a_log, dn_dt_bias, dn_norm_g,
           attn_sink, ffn_w_gate, ffn_w_up, ffn_w_down, moe_router, moe_w_gate, moe_w_up, moe_w_down,
           final_norm_g):
    bsz, n, d = x.shape
    ctx_len = ctx.shape[1]
    depth = w_in.shape[0]
    assert bsz == 1 and d == D and ctx_len == TM and n % TM == 0 and n % GRID_W == 0
    cos, sin = _rope_tables(n)
    mods = _mod_vectors(c, c_ctx, w_mod, b_mod)
    h = jnp.concatenate([ctx[0], x[0]], axis=0)
    for layer in range(depth):
        last = layer == depth - 1
        mod = mods[layer]
        w_main, wab1, wab0 = _prep_w_in(w_in[layer])
        pconv, pqkv, pz, pq, pkv, pab = _in_proj(h, mod, w_main, wab1, wab0)
        qn, kn, vv, gb = _dn_prep(pqkv, pab, dn_conv_w[layer], dn_a_log[layer], dn_dt_bias[layer])
        o_f, o_b = _delta_net(qn, kn, vv, gb, ctx_len)
        sink = jnp.zeros((8, LANE), F32).at[0, :ATT_HEADS].set(attn_sink[layer])
        yc = _attention(pq, pkv, cos, sin, sink, ctx_len, with_ctx=not last)
        router = None
        if layer % 2 == 1:
            wr = jnp.zeros((D, LANE), F32).at[:, :N_EXPERTS].set(moe_router[layer // 2])
            wr1 = wr.astype(BF16)
            router = (wr1, (wr - wr1.astype(F32)).astype(BF16))
        outs = _mixer_finish(h, mod, pconv, conv_w[layer], o_f, o_b, pz, dn_norm_g[layer], yc,
                             w_out[layer].astype(BF16), ctx_len, with_ctx=not last, router=router)
        if layer % 2 == 0:
            assert not last
            x1, hx = outs
            i = layer // 2
            h = _dense_ffn(hx, x1, mod, ffn_w_gate[i].astype(BF16), ffn_w_up[i].astype(BF16),
                           ffn_w_down[i].astype(BF16))
        else:
            assert last
            x1, hx, route = outs
            i = layer // 2
            dest, row_tok, blk_e, n_used = _moe_route(route[:, 2:2 + TOP_K].astype(jnp.int32))
            take = lambda rows_, idx: rows_.at[idx].get(mode="promise_in_bounds")
            xs = take(hx, row_tok)
            y = _moe_experts(xs, blk_e, n_used, moe_w_gate[i].astype(BF16), moe_w_up[i].astype(BF16),
                             moe_w_down[i].astype(BF16))
            y0 = take(y, dest[:, 0])
            y1 = take(y, dest[:, 1])
            h = _moe_combine_final(x1, y0, y1, route, mod, final_norm_g)
    return h.reshape(bsz, n, d)
```

```python
import functools

import jax
import jax.numpy as jnp
from jax import lax
from jax.experimental import pallas as pl
from jax.experimental.pallas import tpu as pltpu

F32 = jnp.float32
BF16 = jnp.bfloat16

D = 1024
N_MOD = 6
EPS = 1e-6
NEG = -1e30
GRID_W = 64

CONV_CH = 256
DN_HEADS = 6
DN_HEAD_DIM = 64
DN_DIM = DN_HEADS * DN_HEAD_DIM
DN_CHUNK = 64
DN_SUB = 16
DN_STEP_CHUNKS = 4
ATT_HEADS = 6
ATT_KV_HEADS = 2
ATT_GROUP = ATT_HEADS // ATT_KV_HEADS
HEAD_DIM = 64
ATT_DIM = ATT_HEADS * HEAD_DIM
ATT_KV_DIM = ATT_KV_HEADS * HEAD_DIM
ATT_BLOCK = 128
ROPE_BASE = 10000.0
AXIS_DIM = HEAD_DIM // 2
MIX_DIM = CONV_CH + DN_DIM + ATT_DIM

D_FF = 2816
N_EXPERTS = 8
TOP_K = 2
D_FF_EXPERT = 3584

TM = 256
FF_CHUNK = 256
MOE_TM = 512
MOE_TF = 1792
MOE_SUB = 256
LANE = 128
VMEM_LIMIT = 56 * 1024 * 1024

_C_QKV = 3 * CONV_CH
_C_Z = _C_QKV + 3 * DN_DIM
_C_A = _C_Z + DN_DIM
_C_Q = _C_A + 4 * DN_HEADS
_C_K = _C_Q + ATT_DIM
_C_V = _C_K + ATT_KV_DIM
_C_END = _C_V + ATT_KV_DIM
N_AB = 4 * DN_HEADS


def _params(sem=None, vmem=None):
    kw = {}
    if sem is not None:
        kw["dimension_semantics"] = sem
    if vmem is not None:
        kw["vmem_limit_bytes"] = vmem
    return pltpu.CompilerParams(**kw)


def _split2(a):
    hi = a.astype(BF16)
    lo = (a - hi.astype(F32)).astype(BF16)
    return hi, lo


def _split3(a):
    hi = a.astype(BF16)
    r = a - hi.astype(F32)
    mid = r.astype(BF16)
    lo = (r - mid.astype(F32)).astype(BF16)
    return hi, mid, lo


_NN = (((1,), (0,)), ((), ()))
_NT = (((1,), (1,)), ((), ()))
_TN = (((0,), (0,)), ((), ()))


def _mm(a, b, dims=_NN):
    return lax.dot_general(a.astype(BF16), b.astype(BF16), dims, preferred_element_type=F32)


def _mm3(a, b, dims=_NN):
    a1, a0 = _split2(a)
    b1, b0 = _split2(b)
    d = functools.partial(lax.dot_general, dimension_numbers=dims, preferred_element_type=F32)
    return d(a1, b1) + (d(a1, b0) + d(a0, b1))


_BNN = (((2,), (1,)), ((0,), (0,)))
_BNT = (((2,), (2,)), ((0,), (0,)))
_BTN = (((1,), (1,)), ((0,), (0,)))


def _bmm(a, b, dims=_BNN):
    return lax.dot_general(a.astype(BF16), b.astype(BF16), dims, preferred_element_type=F32)


def _bmm3(a, b, dims=_BNN):
    return _mm3(a, b, dims)


def _sigmoid(x):
    return 1.0 / (1.0 + jnp.exp(-x))


def _softplus(x):
    return jnp.maximum(x, 0.0) + jnp.log1p(jnp.exp(-jnp.abs(x)))


def _mod_row(mod_ref, is_ctx):
    mod = mod_ref[...]
    return jnp.where(is_ctx, mod[1:2, :], mod[0:1, :])


def _rmsnorm_rows(x):
    return x * lax.rsqrt(jnp.mean(x * x, axis=-1, keepdims=True) + EPS)


def _shift_rows(u, prow, nrow):
    n = u.shape[0]
    rid = lax.broadcasted_iota(jnp.int32, u.shape, 0)
    up = jnp.where(rid == 0, prow, pltpu.roll(u, 1, 0))
    un = jnp.where(rid == n - 1, nrow, pltpu.roll(u, n - 1, 0))
    return up, un


def _same_group(shape, group):
    sh = group.bit_length() - 1
    assert 1 << sh == group
    return (lax.broadcasted_iota(jnp.int32, shape, 0) >> sh) == (lax.broadcasted_iota(jnp.int32, shape, 1) >> sh)


def _head_blockdiag(n, group):
    return jnp.where(_same_group((n, n), group), 1.0, 0.0).astype(BF16)


def _group_sum(t, bd):
    hi, lo = _split2(t)
    d = functools.partial(lax.dot_general, dimension_numbers=_NN, preferred_element_type=F32)
    return d(hi, bd) + d(lo, bd)


MOD_TN = 1536


def _mod_kernel(s_ref, w_ref, b_ref, o_ref):
    s = s_ref[...]
    s = s * _sigmoid(s)
    o_ref[0] = _mm3(s, w_ref[0]) + b_ref[0]


def _mod_vectors(c, c_ctx, w_mod, b_mod):
    depth = w_mod.shape[0]
    s = jnp.zeros((8, D), F32).at[0].set(c[0]).at[1].set(c_ctx)
    return pl.pallas_call(
        _mod_kernel,
        grid=(depth, N_MOD * D // MOD_TN),
        in_specs=[
            pl.BlockSpec((8, D), lambda l, j: (0, 0)),
            pl.BlockSpec((1, D, MOD_TN), lambda l, j: (l, 0, j)),
            pl.BlockSpec((1, 1, MOD_TN), lambda l, j: (l, 0, j)),
        ],
        out_specs=pl.BlockSpec((1, 8, MOD_TN), lambda l, j: (l, 0, j)),
        out_shape=jax.ShapeDtypeStruct((depth, 8, N_MOD * D), F32),
        compiler_params=_params(("arbitrary", "arbitrary"), VMEM_LIMIT),
        name="mod_vectors",
    )(s, w_mod, b_mod.reshape(depth, 1, N_MOD * D))


def _in_kernel(h_ref, mod_ref, w_ref, wab1_ref, wab0_ref,
               pconv_ref, pqkv_ref, pz_ref, pq_ref, pkv_ref, pab_ref):
    row = _mod_row(mod_ref, pl.program_id(0) == 0)
    hm = _rmsnorm_rows(h_ref[...]) * (1.0 + row[:, D:2 * D]) + row[:, 0:D]
    h1, h0 = _split2(hm)
    d = functools.partial(lax.dot_general, dimension_numbers=_NN, preferred_element_type=F32)
    off = 0
    for ref in (pconv_ref, pqkv_ref, pz_ref, pq_ref, pkv_ref):
        w = ref.shape[1]
        ref[...] = d(h1, w_ref[:, off:off + w])
        off += w
    pab_ref[...] = d(h1, wab1_ref[...]) + (d(h0, wab1_ref[...]) + d(h1, wab0_ref[...]))


def _in_proj(h, mod, w_main, wab1, wab0):
    t = h.shape[0]
    widths = (3 * CONV_CH, 3 * DN_DIM, DN_DIM, ATT_DIM, 2 * ATT_KV_DIM, LANE)
    const = lambda i: (0, 0)
    return pl.pallas_call(
        _in_kernel,
        grid=(t // TM,),
        in_specs=[
            pl.BlockSpec((TM, D), lambda i: (i, 0)),
            pl.BlockSpec((8, N_MOD * D), const),
            pl.BlockSpec(w_main.shape, const),
            pl.BlockSpec(wab1.shape, const),
            pl.BlockSpec(wab0.shape, const),
        ],
        out_specs=[pl.BlockSpec((TM, w), lambda i: (i, 0)) for w in widths],
        out_shape=[jax.ShapeDtypeStruct((t, w), F32) for w in widths],
        compiler_params=_params(("arbitrary",), VMEM_LIMIT),
        name="in_proj",
    )(h, mod, w_main, wab1, wab0)


def _halo_rows(prev_ref, next_ref, i, nblk):
    pvalid = jnp.logical_and(i != 0, i != 1)
    nvalid = jnp.logical_and(i != 0, i != nblk - 1)
    prow = jnp.where(pvalid, prev_ref[7:8, :], 0.0)
    nrow = jnp.where(nvalid, next_ref[0:1, :], 0.0)
    return prow, nrow


def _dnprep_kernel(qkv_ref, prev_ref, next_ref, ab_ref, cw_ref, alog_ref, dtb_ref,
                   q_ref, k_ref, v_ref, gb_ref, *, nblk):
    i = pl.program_id(0)
    u = qkv_ref[...]
    prow, nrow = _halo_rows(prev_ref, next_ref, i, nblk)
    up, un = _shift_rows(u, prow, nrow)
    cw = cw_ref[...]
    y = up * cw[0:1, :] + u * cw[1:2, :] + un * cw[2:3, :]
    y = y * _sigmoid(y)
    q = y[:, 0:DN_DIM]
    k = y[:, DN_DIM:2 * DN_DIM]
    bd = _head_blockdiag(DN_DIM, DN_HEAD_DIM)
    q_ref[...] = q * lax.rsqrt(_group_sum(q * q, bd) + 1e-6) * (DN_HEAD_DIM ** -0.5)
    k_ref[...] = k * lax.rsqrt(_group_sum(k * k, bd) + 1e-6)
    v_ref[...] = y[:, 2 * DN_DIM:3 * DN_DIM]
    ab = ab_ref[...]
    g = -jnp.exp(alog_ref[...]) * _softplus(ab + dtb_ref[...])
    lane = lax.broadcasted_iota(jnp.int32, ab.shape, 1)
    gb_ref[...] = jnp.where(lane < 2 * DN_HEADS, g, _sigmoid(ab))


def _dn_prep(pqkv, pab, dn_conv_w, a_log, dt_bias):
    t, w = pqkv.shape
    nblk = t // TM
    alog = jnp.zeros((1, LANE), F32).at[0, :2 * DN_HEADS].set(a_log.reshape(-1))
    dtb = jnp.zeros((1, LANE), F32).at[0, :2 * DN_HEADS].set(dt_bias.reshape(-1))
    r8 = TM // 8
    const = lambda i: (0, 0)
    return pl.pallas_call(
        functools.partial(_dnprep_kernel, nblk=nblk),
        grid=(nblk,),
        in_specs=[
            pl.BlockSpec((TM, w), lambda i: (i, 0)),
            pl.BlockSpec((8, w), lambda i: (jnp.maximum(i * r8 - 1, 0), 0)),
            pl.BlockSpec((8, w), lambda i: (jnp.minimum((i + 1) * r8, t // 8 - 1), 0)),
            pl.BlockSpec((TM, LANE), lambda i: (i, 0)),
            pl.BlockSpec((3, w), const),
            pl.BlockSpec((1, LANE), const),
            pl.BlockSpec((1, LANE), const),
        ],
        out_specs=[pl.BlockSpec((TM, DN_DIM), lambda i: (i, 0))] * 3
        + [pl.BlockSpec((TM, LANE), lambda i: (i, 0))],
        out_shape=[jax.ShapeDtypeStruct((t, DN_DIM), F32)] * 3 + [jax.ShapeDtypeStruct((t, LANE), F32)],
        compiler_params=_params(("arbitrary",), VMEM_LIMIT),
        name="dn_prep",
    )(pqkv, pqkv, pqkv, pab, dn_conv_w, alog, dtb)


def _dn_chunk(rev, q_ref, k_ref, v_ref, gb_ref, o_ref, s_ref):
    c_ = DN_CHUNK
    ri = lax.broadcasted_iota(jnp.int32, (c_, c_), 0)
    ci = lax.broadcasted_iota(jnp.int32, (c_, c_), 1)
    incl = (ri <= ci) if rev else (ri >= ci)
    strict = (ri < ci) if rev else (ri > ci)
    same_sub = _same_group((c_, c_), DN_SUB)
    eye = jnp.where(ri == ci, 1.0, 0.0)
    tri = jnp.where(incl, 1.0, 0.0).astype(BF16)
    last = 0 if rev else c_ - 1

    nchunks = q_ref.shape[0] // c_
    nh = DN_HEADS
    col0 = nh if rev else 0
    rows = lambda g: slice(g * c_, (g + 1) * c_)
    lanes = lambda h: slice(h * DN_HEAD_DIM, (h + 1) * DN_HEAD_DIM)

    def stack(fn):
        return jnp.stack([fn(g, h) for g in range(nchunks) for h in range(nh)])

    gb = gb_ref[...]
    gcs = [_cumsum_rows(tri, gb[rows(g), :]) for g in range(nchunks)]
    gcts = [gc.T for gc in gcs]
    q = stack(lambda g, h: q_ref[rows(g), lanes(h)])
    k = stack(lambda g, h: k_ref[rows(g), lanes(h)])
    v = stack(lambda g, h: v_ref[rows(g), lanes(h)])
    gcol = stack(lambda g, h: gcs[g][:, col0 + h:col0 + h + 1])
    grow = stack(lambda g, h: gcts[g][col0 + h:col0 + h + 1, :])
    beta = stack(lambda g, h: gb[rows(g), 2 * nh + col0 + h:2 * nh + col0 + h + 1])
    glast = gcol[:, last:last + 1, :]
    decay = jnp.where(incl, jnp.exp(jnp.where(incl, gcol - grow, 0.0)), 0.0)
    eg = jnp.exp(gcol)
    kb = k * beta
    a = jnp.where(strict, _bmm(kb, k, _BNT) * decay, 0.0)
    qk = jnp.where(incl, _bmm(q, k, _BNT) * decay, 0.0)
    ad = jnp.where(same_sub, a, 0.0)
    ao = a - ad
    p = eye - ad
    n2 = _bmm(ad, ad)
    p = p + _bmm(p, n2)
    n4 = _bmm(n2, n2)
    p = p + _bmm(p, n4)
    n8 = _bmm(n4, n4)
    dinv = p + _bmm(p, n8)
    m = _bmm3(dinv, ao)
    m2 = _bmm(m, m)
    y = _bmm3(dinv, jnp.concatenate([v * beta, kb * eg], axis=-1))
    z = y + _bmm(m2, y)
    x = z - _bmm3(m, z)
    u = x[:, :, :DN_HEAD_DIM]
    w = x[:, :, DN_HEAD_DIM:]
    qg = q * eg
    kd = k * jnp.exp(glast - gcol)
    gl = jnp.exp(glast)
    s = s_ref[col0:col0 + nh]
    for g in (reversed(range(nchunks)) if rev else range(nchunks)):
        b = slice(g * nh, (g + 1) * nh)
        v_new = u[b] - _bmm(w[b], s)
        o = _bmm(qg[b], s) + _bmm(qk[b], v_new)
        s = s * gl[b] + _bmm(kd[b], v_new, _BTN)
        for h in range(nh):
            o_ref[rows(g), lanes(h)] = o[h]
    s_ref[col0:col0 + nh] = s


def _cumsum_rows(tri_bf16, g):
    g2, g1, g0 = _split3(g)
    d = functools.partial(lax.dot_general, dimension_numbers=_NN, preferred_element_type=F32)
    return d(tri_bf16, g2) + (d(tri_bf16, g1) + d(tri_bf16, g0))


def _dn_kernel(qf, kf, vf, gf, qb, kb, vb, gbb, of_ref, ob_ref, s_ref):
    @pl.when(pl.program_id(0) == 0)
    def _():
        s_ref[...] = jnp.zeros(s_ref.shape, F32)

    _dn_chunk(False, qf, kf, vf, gf, of_ref, s_ref)
    _dn_chunk(True, qb, kb, vb, gbb, ob_ref, s_ref)


def _delta_net(q, k, v, gb, ctx_len):
    t = q.shape[0]
    rows = DN_STEP_CHUNKS * DN_CHUNK
    assert ctx_len == rows and t % rows == 0
    nstep = t // rows

    def fwd(s):
        return (s, 0)

    def bwd(s):
        return (jnp.where(s == 0, 0, nstep - s), 0)

    wide = lambda im: pl.BlockSpec((rows, DN_DIM), im)
    narrow = lambda im: pl.BlockSpec((rows, LANE), im)
    return pl.pallas_call(
        _dn_kernel,
        grid=(nstep,),
        in_specs=[wide(fwd), wide(fwd), wide(fwd), narrow(fwd), wide(bwd), wide(bwd), wide(bwd), narrow(bwd)],
        out_specs=[wide(fwd), wide(bwd)],
        out_shape=[jax.ShapeDtypeStruct((t, DN_DIM), F32)] * 2,
        scratch_shapes=[pltpu.VMEM((2 * DN_HEADS, DN_HEAD_DIM, DN_HEAD_DIM), F32)],
        compiler_params=_params(("arbitrary",), VMEM_LIMIT),
        name="delta_net",
    )(q, k, v, gb, q, k, v, gb)


def _rope(x, cos, sin):
    w = x.shape[1]
    lane = lax.broadcasted_iota(jnp.int32, x.shape, 1)
    first_half = (lane & (AXIS_DIM - 1)) < (AXIS_DIM // 2)
    swapped = jnp.where(first_half, pltpu.roll(x, w - AXIS_DIM // 2, 1), pltpu.roll(x, AXIS_DIM // 2, 1))
    return x * cos + swapped * sin


def _softmax_av(s, sink, vals):
    m = jnp.maximum(jnp.max(s, axis=-1, keepdims=True), sink)
    p = jnp.exp(s - m)
    denom = jnp.sum(p, axis=-1, keepdims=True) + jnp.exp(sink - m)
    return _mm(p, vals) / denom


def _attend(q, keys, vals, valid, sink_all, o_ref):
    b = q.shape[0]
    for kvh in range(ATT_KV_HEADS):
        kl = slice(kvh * HEAD_DIM, (kvh + 1) * HEAD_DIM)
        heads = range(kvh * ATT_GROUP, (kvh + 1) * ATT_GROUP)
        qs = jnp.concatenate([q[:, h * HEAD_DIM:(h + 1) * HEAD_DIM] for h in heads], axis=0)
        sink = jnp.concatenate([jnp.broadcast_to(sink_all[0:1, h:h + 1], (b, 1)) for h in heads], axis=0)
        s = _mm(qs, keys[:, kl], _NT)
        if valid is not None:
            s = jnp.where(valid, s, NEG)
        o = _softmax_av(s, sink, vals[:, kl])
        for g, h in enumerate(heads):
            o_ref[:, h * HEAD_DIM:(h + 1) * HEAD_DIM] = o[g * b:(g + 1) * b, :]


def _attn_kernel(q_ref, kp_ref, kc_ref, kn_ref, kctx_ref, cq_ref, sq_ref, cp_ref, sp_ref, cn_ref, sn_ref,
                 sink_ref, o_ref, *, nb, off, with_ctx):
    j = pl.program_id(0)
    b = ATT_BLOCK
    scale = HEAD_DIM ** -0.5

    @pl.when(j < off)
    def _():
        if with_ctx:
            kvx = kctx_ref[...]
            _attend(q_ref[...] * scale, kvx[:, :ATT_KV_DIM], kvx[:, ATT_KV_DIM:], None, sink_ref[...], o_ref)
        else:
            o_ref[...] = jnp.zeros(o_ref.shape, F32)

    @pl.when(j >= off)
    def _():
        i = j - off
        cq = cq_ref[...]
        sq = sq_ref[...]
        q = q_ref[...]
        q = jnp.concatenate([_rope(q[:, l * LANE:(l + 1) * LANE], cq, sq) for l in range(ATT_DIM // LANE)], axis=1)
        kvp, kvc, kvn, kvx = kp_ref[...], kc_ref[...], kn_ref[...], kctx_ref[...]
        kp = _rope(kvp[:, :ATT_KV_DIM], cp_ref[...], sp_ref[...])
        kc = _rope(kvc[:, :ATT_KV_DIM], cq, sq)
        kn = _rope(kvn[:, :ATT_KV_DIM], cn_ref[...], sn_ref[...])
        keys = jnp.concatenate([kp, kc, kn, kvx[:, :ATT_KV_DIM]], axis=0)
        vals = jnp.concatenate(
            [kvp[:, ATT_KV_DIM:], kvc[:, ATT_KV_DIM:], kvn[:, ATT_KV_DIM:], kvx[:, ATT_KV_DIM:]], axis=0)
        nk = keys.shape[0]
        c = lax.broadcasted_iota(jnp.int32, (1, nk), 1)
        far = 4 * nk
        ccol = jnp.where(c >= 3 * b, b + ((c - 3 * b) & (b - 1)), c)
        ccol = jnp.where(jnp.logical_and(c < b, i == 0), -far, ccol)
        ccol = jnp.where(jnp.logical_and(jnp.logical_and(c >= 2 * b, c < 3 * b), i == nb - 1), far, ccol)
        r = lax.broadcasted_iota(jnp.int32, (ATT_GROUP * b, 1), 0) & (b - 1)
        valid = lax.bitcast_convert_type(ccol - r, jnp.uint32) <= jnp.uint32(2 * b)
        _attend(q * scale, keys, vals, valid, sink_ref[...], o_ref)


def _attention(pq, pkv, cos, sin, sink, ctx_len, with_ctx):
    t = pq.shape[0]
    n = t - ctx_len
    nb = n // ATT_BLOCK
    off = ctx_len // ATT_BLOCK
    kv = lambda im: pl.BlockSpec((ATT_BLOCK, 2 * ATT_KV_DIM), im)
    tab = lambda im: pl.BlockSpec((ATT_BLOCK, LANE), im)
    lat = lambda j: jnp.maximum(j - off, 0)
    cur = lambda j: (j, 0)
    prv = lambda j: (jnp.maximum(lat(j) - 1, 0) + off, 0)
    nxt = lambda j: (jnp.minimum(lat(j) + 1, nb - 1) + off, 0)
    tcur = lambda j: (lat(j), 0)
    tprv = lambda j: (jnp.maximum(lat(j) - 1, 0), 0)
    tnxt = lambda j: (jnp.minimum(lat(j) + 1, nb - 1), 0)
    return pl.pallas_call(
        functools.partial(_attn_kernel, nb=nb, off=off, with_ctx=with_ctx),
        grid=(off + nb,),
        in_specs=[
            pl.BlockSpec((ATT_BLOCK, ATT_DIM), cur),
            kv(prv), kv(cur), kv(nxt),
            pl.BlockSpec((ctx_len, 2 * ATT_KV_DIM), lambda j: (0, 0)),
            tab(tcur), tab(tcur), tab(tprv), tab(tprv), tab(tnxt), tab(tnxt),
            pl.BlockSpec((8, LANE), lambda j: (0, 0)),
        ],
        out_specs=pl.BlockSpec((ATT_BLOCK, ATT_DIM), cur),
        out_shape=jax.ShapeDtypeStruct((t, ATT_DIM), F32),
        compiler_params=_params(("arbitrary",), VMEM_LIMIT),
        name="attention",
    )(pq, pkv, pkv, pkv, pkv, cos, sin, cos, sin, cos, sin, sink)


def _mixfin_kernel(h_ref, mod_ref, pconv_ref, prev_ref, next_ref, cw_ref, of_ref, ob_ref, z_ref, ng_ref, yc_ref,
                   wout_ref, *rest, nblk, blk0, with_router):
    if with_router:
        wr1_ref, wr0_ref, x_ref, hx_ref, lg_ref = rest
    else:
        x_ref, hx_ref = rest
    i = pl.program_id(0) + blk0
    row = _mod_row(mod_ref, i == 0)
    pc = pconv_ref[...]
    u = pc[:, CONV_CH:2 * CONV_CH] * pc[:, 2 * CONV_CH:]
    prow, nrow = _halo_rows(prev_ref, next_ref, i, nblk)
    prow = prow[:, CONV_CH:2 * CONV_CH] * prow[:, 2 * CONV_CH:]
    nrow = nrow[:, CONV_CH:2 * CONV_CH] * nrow[:, 2 * CONV_CH:]
    up, un = _shift_rows(u, prow, nrow)
    cw = cw_ref[...]
    ya = pc[:, :CONV_CH] * (up * cw[0:1, :] + u * cw[1:2, :] + un * cw[2:3, :])
    o = of_ref[...] + ob_ref[...]
    ms = _group_sum(o * o, _head_blockdiag(DN_DIM, DN_HEAD_DIM)) * (1.0 / DN_HEAD_DIM)
    z = z_ref[...]
    yb = o * lax.rsqrt(ms + EPS) * ng_ref[...] * (z * _sigmoid(z))
    mix = jnp.concatenate([ya, yb, yc_ref[...]], axis=1)
    x = h_ref[...] + row[:, 2 * D:3 * D] * _mm(mix, wout_ref[...])
    x_ref[...] = x
    hx = _rmsnorm_rows(x) * (1.0 + row[:, 4 * D:5 * D]) + row[:, 3 * D:4 * D]
    hx_ref[...] = hx.astype(hx_ref.dtype)
    if with_router:
        h1, h0 = _split2(hx)
        d = functools.partial(lax.dot_general, dimension_numbers=_NN, preferred_element_type=F32)
        lg = d(h1, wr1_ref[...]) + (d(h0, wr1_ref[...]) + d(h1, wr0_ref[...]))
        lane = lax.broadcasted_iota(jnp.int32, lg.shape, 1)
        lanef = lane.astype(F32)
        lg = jnp.where(lane < N_EXPERTS, lg, -jnp.inf)
        m1 = jnp.max(lg, axis=-1, keepdims=True)
        i1 = jnp.min(jnp.where(lg == m1, lanef, float(LANE)), axis=-1, keepdims=True)
        rest = jnp.where(lanef == i1, -jnp.inf, lg)
        m2 = jnp.max(rest, axis=-1, keepdims=True)
        i2 = jnp.min(jnp.where(rest == m2, lanef, float(LANE)), axis=-1, keepdims=True)
        e2 = jnp.exp(m2 - m1)
        g1 = 1.0 / (1.0 + e2)
        lg_ref[...] = jnp.where(lane == 0, g1, jnp.where(lane == 1, e2 * g1, jnp.where(lane == 2, i1, i2)))


def _mixer_finish(h, mod, pconv, conv_w, o_f, o_b, pz, norm_g, yc, w_out, ctx_len, with_ctx, router=None):
    t = h.shape[0]
    nblk = t // TM
    blk0 = 0 if with_ctx else ctx_len // TM
    rows = t - blk0 * TM
    r8 = TM // 8
    w = pconv.shape[1]
    cur = lambda i: (i + blk0, 0)
    out_cur = lambda i: (i, 0)
    const = lambda i: (0, 0)
    ng = jnp.tile(norm_g.reshape(1, DN_HEAD_DIM), (1, DN_HEADS))
    in_specs = [
        pl.BlockSpec((TM, D), cur),
        pl.BlockSpec((8, N_MOD * D), const),
        pl.BlockSpec((TM, w), cur),
        pl.BlockSpec((8, w), lambda i: (jnp.maximum((i + blk0) * r8 - 1, 0), 0)),
        pl.BlockSpec((8, w), lambda i: (jnp.minimum((i + blk0 + 1) * r8, t // 8 - 1), 0)),
        pl.BlockSpec((3, CONV_CH), const),
        pl.BlockSpec((TM, DN_DIM), cur),
        pl.BlockSpec((TM, DN_DIM), cur),
        pl.BlockSpec((TM, DN_DIM), cur),
        pl.BlockSpec((1, DN_DIM), const),
        pl.BlockSpec((TM, ATT_DIM), cur),
        pl.BlockSpec((MIX_DIM, D), const),
    ]
    args = [h, mod, pconv, pconv, pconv, conv_w, o_f, o_b, pz, ng, yc, w_out]
    out_specs = [pl.BlockSpec((TM, D), out_cur), pl.BlockSpec((TM, D), out_cur)]
    hx_dtype = BF16 if router is None else F32
    out_shape = [jax.ShapeDtypeStruct((rows, D), F32), jax.ShapeDtypeStruct((rows, D), hx_dtype)]
    if router is not None:
        in_specs += [pl.BlockSpec((D, LANE), const)] * 2
        args += list(router)
        out_specs.append(pl.BlockSpec((TM, LANE), out_cur))
        out_shape.append(jax.ShapeDtypeStruct((rows, LANE), F32))
    return pl.pallas_call(
        functools.partial(_mixfin_kernel, nblk=nblk, blk0=blk0, with_router=router is not None),
        grid=(rows // TM,),
        in_specs=in_specs,
        out_specs=out_specs,
        out_shape=out_shape,
        compiler_params=_params(("arbitrary",), VMEM_LIMIT),
        name="mixer_finish",
    )(*args)


def _ffn_kernel(hx_ref, x_ref, mod_ref, wg_ref, wu_ref, wd_ref, o_ref):
    row = _mod_row(mod_ref, pl.program_id(0) == 0)
    hx = hx_ref[...]
    acc = jnp.zeros((hx.shape[0], D), F32)
    for f in range(0, D_FF, FF_CHUNK):
        g = _mm(hx, wg_ref[:, f:f + FF_CHUNK])
        u = _mm(hx, wu_ref[:, f:f + FF_CHUNK])
        acc = acc + _mm(g * _sigmoid(g) * u, wd_ref[f:f + FF_CHUNK, :])
    o_ref[...] = x_ref[...] + row[:, 5 * D:6 * D] * acc


def _dense_ffn(hx, x, mod, wg, wu, wd):
    t = x.shape[0]
    const = lambda i: (0, 0)
    once = dict(pipeline_mode=pl.Buffered(1))
    return pl.pallas_call(
        _ffn_kernel,
        grid=(t // TM,),
        in_specs=[
            pl.BlockSpec((TM, D), lambda i: (i, 0)),
            pl.BlockSpec((TM, D), lambda i: (i, 0)),
            pl.BlockSpec((8, N_MOD * D), const),
            pl.BlockSpec((D, D_FF), const, **once),
            pl.BlockSpec((D, D_FF), const, **once),
            pl.BlockSpec((D_FF, D), const, **once),
        ],
        out_specs=pl.BlockSpec((TM, D), lambda i: (i, 0)),
        out_shape=jax.ShapeDtypeStruct((t, D), F32),
        compiler_params=_params(("arbitrary",), VMEM_LIMIT),
        name="dense_ffn",
    )(hx, x, mod, wg, wu, wd)


def _moe_kernel(be_ref, nu_ref, xs_ref, wg_ref, wu_ref, wd_ref, y_ref, acc_ref):
    b = pl.program_id(0)
    f = pl.program_id(1)

    @pl.when(b < nu_ref[0])
    def _():
        xs = xs_ref[...].astype(BF16)
        part = jnp.zeros((MOE_TM, D), F32)
        for c in range(0, MOE_TF, MOE_SUB):
            g = _mm(xs, wg_ref[0, :, c:c + MOE_SUB])
            u = _mm(xs, wu_ref[0, :, c:c + MOE_SUB])
            part = part + _mm(g * _sigmoid(g) * u, wd_ref[0, c:c + MOE_SUB, :])

        @pl.when(f == 0)
        def _():
            acc_ref[...] = part

        @pl.when(f != 0)
        def _():
            acc_ref[...] += part

        @pl.when(f == pl.num_programs(1) - 1)
        def _():
            y_ref[...] = acc_ref[...]

    @pl.when(b >= nu_ref[0])
    def _():
        y_ref[...] = jnp.zeros(y_ref.shape, F32)


def _moe_experts(xs, blk_e, n_used, wg, wu, wd):
    cap = xs.shape[0]
    nblk = cap // MOE_TM
    nf = D_FF_EXPERT // MOE_TF

    def fidx(b, f, nu):
        return jnp.where(b < nu[0], f, nf - 1)

    grid_spec = pltpu.PrefetchScalarGridSpec(
        num_scalar_prefetch=2,
        grid=(nblk, nf),
        in_specs=[
            pl.BlockSpec((MOE_TM, D), lambda b, f, be, nu: (b, 0)),
            pl.BlockSpec((1, D, MOE_TF), lambda b, f, be, nu: (be[b], 0, fidx(b, f, nu))),
            pl.BlockSpec((1, D, MOE_TF), lambda b, f, be, nu: (be[b], 0, fidx(b, f, nu))),
            pl.BlockSpec((1, MOE_TF, D), lambda b, f, be, nu: (be[b], fidx(b, f, nu), 0)),
        ],
        out_specs=pl.BlockSpec((MOE_TM, D), lambda b, f, be, nu: (b, 0)),
        scratch_shapes=[pltpu.VMEM((MOE_TM, D), F32)],
    )
    return pl.pallas_call(
        _moe_kernel,
        grid_spec=grid_spec,
        out_shape=jax.ShapeDtypeStruct((cap, D), F32),
        compiler_params=_params(("arbitrary", "arbitrary"), VMEM_LIMIT),
        name="moe_experts",
    )(blk_e, n_used, xs, wg, wu, wd)


def _moe_route(top_e):
    n = top_e.shape[0]
    a = n * TOP_K
    flat_e = top_e.reshape(a)
    onehot = (flat_e[:, None] == jnp.arange(N_EXPERTS, dtype=flat_e.dtype)[None, :]).astype(jnp.int32)
    counts = jnp.sum(onehot, axis=0)
    padded = (counts + MOE_TM - 1) // MOE_TM * MOE_TM
    pad_ends = jnp.cumsum(padded)
    pad_starts = pad_ends - padded
    dest = jnp.sum(onehot * (jnp.cumsum(onehot, axis=0) - onehot + pad_starts[None, :]), axis=1)
    cap = a + N_EXPERTS * MOE_TM
    nblk = cap // MOE_TM
    row_tok = (jnp.arange(cap, dtype=jnp.int32) % n).at[dest].set(
        jnp.arange(a, dtype=jnp.int32) // TOP_K, unique_indices=True, mode="promise_in_bounds")
    blk_start = jnp.arange(nblk, dtype=jnp.int32) * MOE_TM
    blk_e = jnp.minimum(jnp.sum((pad_ends[None, :] <= blk_start[:, None]).astype(jnp.int32), axis=1),
                        N_EXPERTS - 1)
    n_used = (pad_ends[-1] // MOE_TM).astype(jnp.int32).reshape(1)
    last_e = blk_e[jnp.maximum(n_used[0] - 1, 0)]
    blk_e = jnp.where(jnp.arange(nblk) < n_used[0], blk_e, last_e)
    return dest.reshape(n, TOP_K), row_tok, blk_e, n_used


def _final_kernel(x_ref, y0_ref, y1_ref, gt_ref, mod_ref, fg_ref, o_ref):
    mod = mod_ref[...]
    gt = gt_ref[...]
    f = gt[:, 0:1] * y0_ref[...] + gt[:, 1:2] * y1_ref[...]
    x = x_ref[...] + mod[0:1, 5 * D:6 * D] * f
    o_ref[...] = _rmsnorm_rows(x) * fg_ref[...]


def _moe_combine_final(x, y0, y1, gt, mod, final_g):
    n = x.shape[0]
    row = lambda i: (i, 0)
    const = lambda i: (0, 0)
    return pl.pallas_call(
        _final_kernel,
        grid=(n // TM,),
        in_specs=[pl.BlockSpec((TM, D), row)] * 3
        + [pl.BlockSpec((TM, LANE), row), pl.BlockSpec((8, N_MOD * D), const), pl.BlockSpec((1, D), const)],
        out_specs=pl.BlockSpec((TM, D), row),
        out_shape=jax.ShapeDtypeStruct((n, D), F32),
        compiler_params=_params(("arbitrary",), VMEM_LIMIT),
        name="moe_combine_final",
    )(x, y0, y1, gt, mod, final_g.reshape(1, D))


def _rope_tables(n):
    pos = jnp.arange(n, dtype=jnp.int32)
    r = (pos // GRID_W).astype(F32)
    col = (pos % GRID_W).astype(F32)
    inv = ROPE_BASE ** (-jnp.arange(0, AXIS_DIM, 2, dtype=F32) / AXIS_DIM)
    ang = jnp.concatenate([r[:, None] * inv, r[:, None] * inv, col[:, None] * inv, col[:, None] * inv], axis=1)
    sign = jnp.tile(jnp.concatenate([-jnp.ones((AXIS_DIM // 2,), F32), jnp.ones((AXIS_DIM // 2,), F32)]), 2)
    cos = jnp.tile(jnp.cos(ang), (1, LANE // HEAD_DIM))
    sin = jnp.tile(jnp.sin(ang) * sign, (1, LANE // HEAD_DIM))
    return cos, sin


def _prep_w_in(w):
    main = jnp.concatenate([w[:, :_C_A], w[:, _C_Q:_C_END]], axis=1).astype(BF16)
    wab = jnp.zeros((D, LANE), F32).at[:, :N_AB].set(w[:, _C_A:_C_Q])
    wab1 = wab.astype(BF16)
    wab0 = (wab - wab1.astype(F32)).astype(BF16)
    return main, wab1, wab0


def kernel(x, c, ctx, c_ctx, w_mod, b_mod, w_in, w_out, conv_w, dn_conv_w, dn_a_log, dn_dt_bias, dn_norm_g,
           attn_sink, ffn_w_gate, ffn_w_up, ffn_w_down, moe_router, moe_w_gate, moe_w_up, moe_w_down,
           final_norm_g):
    bsz, n, d = x.shape
    ctx_len = ctx.shape[1]
    depth = w_in.shape[0]
    assert bsz == 1 and d == D and ctx_len == TM and n % TM == 0 and n % GRID_W == 0
    cos, sin = _rope_tables(n)
    mods = _mod_vectors(c, c_ctx, w_mod, b_mod)
    h = jnp.concatenate([ctx[0], x[0]], axis=0)
    for layer in range(depth):
        last = layer == depth - 1
        mod = mods[layer]
        w_main, wab1, wab0 = _prep_w_in(w_in[layer])
        pconv, pqkv, pz, pq, pkv, pab = _in_proj(h, mod, w_main, wab1, wab0)
        qn, kn, vv, gb = _dn_prep(pqkv, pab, dn_conv_w[layer], dn_a_log[layer], dn_dt_bias[layer])
        o_f, o_b = _delta_net(qn, kn, vv, gb, ctx_len)
        sink = jnp.zeros((8, LANE), F32).at[0, :ATT_HEADS].set(attn_sink[layer])
        yc = _attention(pq, pkv, cos, sin, sink, ctx_len, with_ctx=not last)
        router = None
        if layer % 2 == 1:
            wr = jnp.zeros((D, LANE), F32).at[:, :N_EXPERTS].set(moe_router[layer // 2])
            wr1 = wr.astype(BF16)
            router = (wr1, (wr - wr1.astype(F32)).astype(BF16))
        outs = _mixer_finish(h, mod, pconv, conv_w[layer], o_f, o_b, pz, dn_norm_g[layer], yc,
                             w_out[layer].astype(BF16), ctx_len, with_ctx=not last, router=router)
        if layer % 2 == 0:
            assert not last
            x1, hx = outs
            i = layer // 2
            h = _dense_ffn(hx, x1, mod, ffn_w_gate[i].astype(BF16), ffn_w_up[i].astype(BF16),
                           ffn_w_down[i].astype(BF16))
        else:
            assert last
            x1, hx, route = outs
            i = layer // 2
            dest, row_tok, blk_e, n_used = _moe_route(route[:, 2:2 + TOP_K].astype(jnp.int32))
            take = lambda rows_, idx: rows_.at[idx].get(mode="promise_in_bounds")
            xs = take(hx, row_tok)
            y = _moe_experts(xs, blk_e, n_used, moe_w_gate[i].astype(BF16), moe_w_up[i].astype(BF16),
                             moe_w_down[i].astype(BF16))
            y0 = take(y, dest[:, 0])
            y1 = take(y, dest[:, 1])
            h = _moe_combine_final(x1, y0, y1, route, mod, final_norm_g)
    return h.reshape(bsz, n, d)
```

```python
import functools

import jax
import jax.numpy as jnp
from jax import lax
from jax.experimental import pallas as pl
from jax.experimental.pallas import tpu as pltpu

F32 = jnp.float32
BF16 = jnp.bfloat16

D = 1024
N_MOD = 6
EPS = 1e-6
NEG = -1e30
GRID_W = 64

CONV_CH = 256
DN_HEADS = 6
DN_HEAD_DIM = 64
DN_DIM = DN_HEADS * DN_HEAD_DIM
DN_CHUNK = 64
DN_SUB = 16
DN_STEP_CHUNKS = 4
ATT_HEADS = 6
ATT_KV_HEADS = 2
ATT_GROUP = ATT_HEADS // ATT_KV_HEADS
HEAD_DIM = 64
ATT_DIM = ATT_HEADS * HEAD_DIM
ATT_KV_DIM = ATT_KV_HEADS * HEAD_DIM
ATT_BLOCK = 128
ROPE_BASE = 10000.0
AXIS_DIM = HEAD_DIM // 2
MIX_DIM = CONV_CH + DN_DIM + ATT_DIM

D_FF = 2816
N_EXPERTS = 8
TOP_K = 2
D_FF_EXPERT = 3584

TM = 256
FF_CHUNK = 256
MOE_TM = 512
MOE_TF = 1792
MOE_SUB = 256
LANE = 128
VMEM_LIMIT = 56 * 1024 * 1024

_C_QKV = 3 * CONV_CH
_C_Z = _C_QKV + 3 * DN_DIM
_C_A = _C_Z + DN_DIM
_C_Q = _C_A + 4 * DN_HEADS
_C_K = _C_Q + ATT_DIM
_C_V = _C_K + ATT_KV_DIM
_C_END = _C_V + ATT_KV_DIM
N_AB = 4 * DN_HEADS


def _params(sem=None, vmem=None):
    kw = {}
    if sem is not None:
        kw["dimension_semantics"] = sem
    if vmem is not None:
        kw["vmem_limit_bytes"] = vmem
    return pltpu.CompilerParams(**kw)


def _split2(a):
    hi = a.astype(BF16)
    lo = (a - hi.astype(F32)).astype(BF16)
    return hi, lo


def _split3(a):
    hi = a.astype(BF16)
    r = a - hi.astype(F32)
    mid = r.astype(BF16)
    lo = (r - mid.astype(F32)).astype(BF16)
    return hi, mid, lo


_NN = (((1,), (0,)), ((), ()))
_NT = (((1,), (1,)), ((), ()))
_TN = (((0,), (0,)), ((), ()))


def _mm(a, b, dims=_NN):
    return lax.dot_general(a.astype(BF16), b.astype(BF16), dims, preferred_element_type=F32)


def _mm3(a, b, dims=_NN):
    a1, a0 = _split2(a)
    b1, b0 = _split2(b)
    d = functools.partial(lax.dot_general, dimension_numbers=dims, preferred_element_type=F32)
    return d(a1, b1) + (d(a1, b0) + d(a0, b1))


_BNN = (((2,), (1,)), ((0,), (0,)))
_BNT = (((2,), (2,)), ((0,), (0,)))
_BTN = (((1,), (1,)), ((0,), (0,)))


def _bmm(a, b, dims=_BNN):
    return lax.dot_general(a.astype(BF16), b.astype(BF16), dims, preferred_element_type=F32)


def _sigmoid(x):
    return 1.0 / (1.0 + jnp.exp(-x))


def _softplus(x):
    return jnp.maximum(x, 0.0) + jnp.log1p(jnp.exp(-jnp.abs(x)))


def _mod_row(mod_ref, is_ctx):
    mod = mod_ref[...]
    return jnp.where(is_ctx, mod[1:2, :], mod[0:1, :])


def _rmsnorm_rows(x):
    return x * lax.rsqrt(jnp.mean(x * x, axis=-1, keepdims=True) + EPS)


def _shift_rows(u, prow, nrow):
    n = u.shape[0]
    rid = lax.broadcasted_iota(jnp.int32, u.shape, 0)
    up = jnp.where(rid == 0, prow, pltpu.roll(u, 1, 0))
    un = jnp.where(rid == n - 1, nrow, pltpu.roll(u, n - 1, 0))
    return up, un


def _same_group(shape, group):
    sh = group.bit_length() - 1
    assert 1 << sh == group
    return (lax.broadcasted_iota(jnp.int32, shape, 0) >> sh) == (lax.broadcasted_iota(jnp.int32, shape, 1) >> sh)


def _head_blockdiag(n, group):
    return jnp.where(_same_group((n, n), group), 1.0, 0.0).astype(BF16)


def _group_sum(t, bd):
    hi, lo = _split2(t)
    d = functools.partial(lax.dot_general, dimension_numbers=_NN, preferred_element_type=F32)
    return d(hi, bd) + d(lo, bd)


MOD_TN = 1536


def _mod_kernel(s_ref, w_ref, b_ref, o_ref):
    s = s_ref[...]
    s = s * _sigmoid(s)
    o_ref[0] = _mm3(s, w_ref[0]) + b_ref[0]


def _mod_vectors(c, c_ctx, w_mod, b_mod):
    depth = w_mod.shape[0]
    s = jnp.zeros((8, D), F32).at[0].set(c[0]).at[1].set(c_ctx)
    return pl.pallas_call(
        _mod_kernel,
        grid=(depth, N_MOD * D // MOD_TN),
        in_specs=[
            pl.BlockSpec((8, D), lambda l, j: (0, 0)),
            pl.BlockSpec((1, D, MOD_TN), lambda l, j: (l, 0, j)),
            pl.BlockSpec((1, 1, MOD_TN), lambda l, j: (l, 0, j)),
        ],
        out_specs=pl.BlockSpec((1, 8, MOD_TN), lambda l, j: (l, 0, j)),
        out_shape=jax.ShapeDtypeStruct((depth, 8, N_MOD * D), F32),
        compiler_params=_params(("arbitrary", "arbitrary"), VMEM_LIMIT),
        name="mod_vectors",
    )(s, w_mod, b_mod.reshape(depth, 1, N_MOD * D))


def _in_kernel(h_ref, mod_ref, w_ref, wab1_ref, wab0_ref,
               pconv_ref, pqkv_ref, pz_ref, pq_ref, pkv_ref, pab_ref):
    row = _mod_row(mod_ref, pl.program_id(0) == 0)
    hm = _rmsnorm_rows(h_ref[...]) * (1.0 + row[:, D:2 * D]) + row[:, 0:D]
    h1, h0 = _split2(hm)
    d = functools.partial(lax.dot_general, dimension_numbers=_NN, preferred_element_type=F32)
    off = 0
    for ref in (pconv_ref, pqkv_ref, pz_ref, pq_ref, pkv_ref):
        w = ref.shape[1]
        ref[...] = d(h1, w_ref[:, off:off + w])
        off += w
    pab_ref[...] = d(h1, wab1_ref[...]) + (d(h0, wab1_ref[...]) + d(h1, wab0_ref[...]))


def _in_proj(h, mod, w_main, wab1, wab0):
    t = h.shape[0]
    widths = (3 * CONV_CH, 3 * DN_DIM, DN_DIM, ATT_DIM, 2 * ATT_KV_DIM, LANE)
    const = lambda i: (0, 0)
    return pl.pallas_call(
        _in_kernel,
        grid=(t // TM,),
        in_specs=[
            pl.BlockSpec((TM, D), lambda i: (i, 0)),
            pl.BlockSpec((8, N_MOD * D), const),
            pl.BlockSpec(w_main.shape, const),
            pl.BlockSpec(wab1.shape, const),
            pl.BlockSpec(wab0.shape, const),
        ],
        out_specs=[pl.BlockSpec((TM, w), lambda i: (i, 0)) for w in widths],
        out_shape=[jax.ShapeDtypeStruct((t, w), F32) for w in widths],
        compiler_params=_params(("arbitrary",), VMEM_LIMIT),
        name="in_proj",
    )(h, mod, w_main, wab1, wab0)


def _halo_rows(prev_ref, next_ref, i, nblk):
    pvalid = jnp.logical_and(i != 0, i != 1)
    nvalid = jnp.logical_and(i != 0, i != nblk - 1)
    prow = jnp.where(pvalid, prev_ref[7:8, :], 0.0)
    nrow = jnp.where(nvalid, next_ref[0:1, :], 0.0)
    return prow, nrow


def _dnprep_kernel(qkv_ref, prev_ref, next_ref, ab_ref, cw_ref, alog_ref, dtb_ref,
                   q_ref, k_ref, v_ref, gb_ref, *, nblk):
    i = pl.program_id(0)
    u = qkv_ref[...]
    prow, nrow = _halo_rows(prev_ref, next_ref, i, nblk)
    up, un = _shift_rows(u, prow, nrow)
    cw = cw_ref[...]
    y = up * cw[0:1, :] + u * cw[1:2, :] + un * cw[2:3, :]
    y = y * _sigmoid(y)
    q = y[:, 0:DN_DIM]
    k = y[:, DN_DIM:2 * DN_DIM]
    bd = _head_blockdiag(DN_DIM, DN_HEAD_DIM)
    q_ref[...] = q * lax.rsqrt(_group_sum(q * q, bd) + 1e-6) * (DN_HEAD_DIM ** -0.5)
    k_ref[...] = k * lax.rsqrt(_group_sum(k * k, bd) + 1e-6)
    v_ref[...] = y[:, 2 * DN_DIM:3 * DN_DIM]
    ab = ab_ref[...]
    g = -jnp.exp(alog_ref[...]) * _softplus(ab + dtb_ref[...])
    lane = lax.broadcasted_iota(jnp.int32, ab.shape, 1)
    gb_ref[...] = jnp.where(lane < 2 * DN_HEADS, g, _sigmoid(ab))


def _dn_prep(pqkv, pab, dn_conv_w, a_log, dt_bias):
    t, w = pqkv.shape
    nblk = t // TM
    alog = jnp.zeros((1, LANE), F32).at[0, :2 * DN_HEADS].set(a_log.reshape(-1))
    dtb = jnp.zeros((1, LANE), F32).at[0, :2 * DN_HEADS].set(dt_bias.reshape(-1))
    r8 = TM // 8
    const = lambda i: (0, 0)
    return pl.pallas_call(
        functools.partial(_dnprep_kernel, nblk=nblk),
        grid=(nblk,),
        in_specs=[
            pl.BlockSpec((TM, w), lambda i: (i, 0)),
            pl.BlockSpec((8, w), lambda i: (jnp.maximum(i * r8 - 1, 0), 0)),
            pl.BlockSpec((8, w), lambda i: (jnp.minimum((i + 1) * r8, t // 8 - 1), 0)),
            pl.BlockSpec((TM, LANE), lambda i: (i, 0)),
            pl.BlockSpec((3, w), const),
            pl.BlockSpec((1, LANE), const),
            pl.BlockSpec((1, LANE), const),
        ],
        out_specs=[pl.BlockSpec((TM, DN_DIM), lambda i: (i, 0))] * 3
        + [pl.BlockSpec((TM, LANE), lambda i: (i, 0))],
        out_shape=[jax.ShapeDtypeStruct((t, DN_DIM), F32)] * 3 + [jax.ShapeDtypeStruct((t, LANE), F32)],
        compiler_params=_params(("arbitrary",), VMEM_LIMIT),
        name="dn_prep",
    )(pqkv, pqkv, pqkv, pab, dn_conv_w, alog, dtb)


def _dn_chunk(rev, q_ref, k_ref, v_ref, gb_ref, o_ref, s_ref):
    c_ = DN_CHUNK
    ri = lax.broadcasted_iota(jnp.int32, (c_, c_), 0)
    ci = lax.broadcasted_iota(jnp.int32, (c_, c_), 1)
    incl = (ri <= ci) if rev else (ri >= ci)
    strict = (ri < ci) if rev else (ri > ci)
    same_sub = _same_group((c_, c_), DN_SUB)
    eye = jnp.where(ri == ci, 1.0, 0.0)
    tri = jnp.where(incl, 1.0, 0.0).astype(BF16)
    last = 0 if rev else c_ - 1

    nchunks = q_ref.shape[0] // c_
    nh = DN_HEADS
    col0 = nh if rev else 0
    rows = lambda g: slice(g * c_, (g + 1) * c_)
    lanes = lambda h: slice(h * DN_HEAD_DIM, (h + 1) * DN_HEAD_DIM)

    def stack(fn):
        return jnp.stack([fn(g, h) for g in range(nchunks) for h in range(nh)])

    gb = gb_ref[...]
    gcs = [_cumsum_rows(tri, gb[rows(g), :]) for g in range(nchunks)]
    gcts = [gc.T for gc in gcs]
    q = stack(lambda g, h: q_ref[rows(g), lanes(h)])
    k = stack(lambda g, h: k_ref[rows(g), lanes(h)])
    v = stack(lambda g, h: v_ref[rows(g), lanes(h)])
    gcol = stack(lambda g, h: gcs[g][:, col0 + h:col0 + h + 1])
    grow = stack(lambda g, h: gcts[g][col0 + h:col0 + h + 1, :])
    beta = stack(lambda g, h: gb[rows(g), 2 * nh + col0 + h:2 * nh + col0 + h + 1])
    glast = gcol[:, last:last + 1, :]
    decay = jnp.where(incl, jnp.exp(jnp.where(incl, gcol - grow, 0.0)), 0.0)
    eg = jnp.exp(gcol)
    kb = k * beta
    a = jnp.where(strict, _bmm(kb, k, _BNT) * decay, 0.0)
    qk = jnp.where(incl, _bmm(q, k, _BNT) * decay, 0.0)
    ad = jnp.where(same_sub, a, 0.0)
    ao = a - ad
    p = eye - ad
    n2 = _bmm(ad, ad)
    p = p + _bmm(p, n2)
    n4 = _bmm(n2, n2)
    p = p + _bmm(p, n4)
    n8 = _bmm(n4, n4)
    dinv = p + _bmm(p, n8)
    m = _bmm(dinv, ao)
    m2 = _bmm(m, m)
    y = _bmm(dinv, jnp.concatenate([v * beta, kb * eg], axis=-1))
    z = y + _bmm(m2, y)
    x = z - _bmm(m, z)
    u = x[:, :, :DN_HEAD_DIM]
    w = x[:, :, DN_HEAD_DIM:]
    qg = q * eg
    kd = k * jnp.exp(glast - gcol)
    gl = jnp.exp(glast)
    s = s_ref[col0:col0 + nh]
    for g in (reversed(range(nchunks)) if rev else range(nchunks)):
        b = slice(g * nh, (g + 1) * nh)
        v_new = u[b] - _bmm(w[b], s)
        o = _bmm(qg[b], s) + _bmm(qk[b], v_new)
        s = s * gl[b] + _bmm(kd[b], v_new, _BTN)
        for h in range(nh):
            o_ref[rows(g), lanes(h)] = o[h]
    s_ref[col0:col0 + nh] = s


def _cumsum_rows(tri_bf16, g):
    g2, g1, g0 = _split3(g)
    d = functools.partial(lax.dot_general, dimension_numbers=_NN, preferred_element_type=F32)
    return d(tri_bf16, g2) + (d(tri_bf16, g1) + d(tri_bf16, g0))


def _dn_kernel(qf, kf, vf, gf, qb, kb, vb, gbb, of_ref, ob_ref, s_ref):
    @pl.when(pl.program_id(0) == 0)
    def _():
        s_ref[...] = jnp.zeros(s_ref.shape, F32)

    _dn_chunk(False, qf, kf, vf, gf, of_ref, s_ref)
    _dn_chunk(True, qb, kb, vb, gbb, ob_ref, s_ref)


def _delta_net(q, k, v, gb, ctx_len):
    t = q.shape[0]
    rows = DN_STEP_CHUNKS * DN_CHUNK
    assert ctx_len == rows and t % rows == 0
    nstep = t // rows

    def fwd(s):
        return (s, 0)

    def bwd(s):
        return (jnp.where(s == 0, 0, nstep - s), 0)

    wide = lambda im: pl.BlockSpec((rows, DN_DIM), im)
    narrow = lambda im: pl.BlockSpec((rows, LANE), im)
    return pl.pallas_call(
        _dn_kernel,
        grid=(nstep,),
        in_specs=[wide(fwd), wide(fwd), wide(fwd), narrow(fwd), wide(bwd), wide(bwd), wide(bwd), narrow(bwd)],
        out_specs=[wide(fwd), wide(bwd)],
        out_shape=[jax.ShapeDtypeStruct((t, DN_DIM), F32)] * 2,
        scratch_shapes=[pltpu.VMEM((2 * DN_HEADS, DN_HEAD_DIM, DN_HEAD_DIM), F32)],
        compiler_params=_params(("arbitrary",), VMEM_LIMIT),
        name="delta_net",
    )(q, k, v, gb, q, k, v, gb)


def _rope(x, cos, sin):
    w = x.shape[1]
    lane = lax.broadcasted_iota(jnp.int32, x.shape, 1)
    first_half = (lane & (AXIS_DIM - 1)) < (AXIS_DIM // 2)
    swapped = jnp.where(first_half, pltpu.roll(x, w - AXIS_DIM // 2, 1), pltpu.roll(x, AXIS_DIM // 2, 1))
    return x * cos + swapped * sin


def _softmax_av(s, sink, vals):
    m = jnp.maximum(jnp.max(s, axis=-1, keepdims=True), sink)
    p = jnp.exp(s - m)
    denom = jnp.sum(p, axis=-1, keepdims=True) + jnp.exp(sink - m)
    return _mm(p, vals) / denom


def _attend(q, keys, vals, valid, sink_all, o_ref):
    b = q.shape[0]
    for kvh in range(ATT_KV_HEADS):
        kl = slice(kvh * HEAD_DIM, (kvh + 1) * HEAD_DIM)
        heads = range(kvh * ATT_GROUP, (kvh + 1) * ATT_GROUP)
        qs = jnp.concatenate([q[:, h * HEAD_DIM:(h + 1) * HEAD_DIM] for h in heads], axis=0)
        sink = jnp.concatenate([jnp.broadcast_to(sink_all[0:1, h:h + 1], (b, 1)) for h in heads], axis=0)
        s = _mm(qs, keys[:, kl], _NT)
        if valid is not None:
            s = jnp.where(valid, s, NEG)
        o = _softmax_av(s, sink, vals[:, kl])
        for g, h in enumerate(heads):
            o_ref[:, h * HEAD_DIM:(h + 1) * HEAD_DIM] = o[g * b:(g + 1) * b, :]


def _attn_kernel(q_ref, kp_ref, kc_ref, kn_ref, kctx_ref, cq_ref, sq_ref, cp_ref, sp_ref, cn_ref, sn_ref,
                 sink_ref, o_ref, *, nb, off, with_ctx):
    j = pl.program_id(0)
    b = ATT_BLOCK
    scale = HEAD_DIM ** -0.5

    @pl.when(j < off)
    def _():
        if with_ctx:
            kvx = kctx_ref[...]
            _attend(q_ref[...] * scale, kvx[:, :ATT_KV_DIM], kvx[:, ATT_KV_DIM:], None, sink_ref[...], o_ref)
        else:
            o_ref[...] = jnp.zeros(o_ref.shape, F32)

    @pl.when(j >= off)
    def _():
        i = j - off
        cq = cq_ref[...]
        sq = sq_ref[...]
        q = q_ref[...]
        q = jnp.concatenate([_rope(q[:, l * LANE:(l + 1) * LANE], cq, sq) for l in range(ATT_DIM // LANE)], axis=1)
        kvp, kvc, kvn, kvx = kp_ref[...], kc_ref[...], kn_ref[...], kctx_ref[...]
        kp = _rope(kvp[:, :ATT_KV_DIM], cp_ref[...], sp_ref[...])
        kc = _rope(kvc[:, :ATT_KV_DIM], cq, sq)
        kn = _rope(kvn[:, :ATT_KV_DIM], cn_ref[...], sn_ref[...])
        keys = jnp.concatenate([kp, kc, kn, kvx[:, :ATT_KV_DIM]], axis=0)
        vals = jnp.concatenate(
            [kvp[:, ATT_KV_DIM:], kvc[:, ATT_KV_DIM:], kvn[:, ATT_KV_DIM:], kvx[:, ATT_KV_DIM:]], axis=0)
        nk = keys.shape[0]
        c = lax.broadcasted_iota(jnp.int32, (1, nk), 1)
        far = 4 * nk
        ccol = jnp.where(c >= 3 * b, b + ((c - 3 * b) & (b - 1)), c)
        ccol = jnp.where(jnp.logical_and(c < b, i == 0), -far, ccol)
        ccol = jnp.where(jnp.logical_and(jnp.logical_and(c >= 2 * b, c < 3 * b), i == nb - 1), far, ccol)
        r = lax.broadcasted_iota(jnp.int32, (ATT_GROUP * b, 1), 0) & (b - 1)
        valid = lax.bitcast_convert_type(ccol - r, jnp.uint32) <= jnp.uint32(2 * b)
        _attend(q * scale, keys, vals, valid, sink_ref[...], o_ref)


def _attention(pq, pkv, cos, sin, sink, ctx_len, with_ctx):
    t = pq.shape[0]
    n = t - ctx_len
    nb = n // ATT_BLOCK
    off = ctx_len // ATT_BLOCK
    kv = lambda im: pl.BlockSpec((ATT_BLOCK, 2 * ATT_KV_DIM), im)
    tab = lambda im: pl.BlockSpec((ATT_BLOCK, LANE), im)
    lat = lambda j: jnp.maximum(j - off, 0)
    cur = lambda j: (j, 0)
    prv = lambda j: (jnp.maximum(lat(j) - 1, 0) + off, 0)
    nxt = lambda j: (jnp.minimum(lat(j) + 1, nb - 1) + off, 0)
    tcur = lambda j: (lat(j), 0)
    tprv = lambda j: (jnp.maximum(lat(j) - 1, 0), 0)
    tnxt = lambda j: (jnp.minimum(lat(j) + 1, nb - 1), 0)
    return pl.pallas_call(
        functools.partial(_attn_kernel, nb=nb, off=off, with_ctx=with_ctx),
        grid=(off + nb,),
        in_specs=[
            pl.BlockSpec((ATT_BLOCK, ATT_DIM), cur),
            kv(prv), kv(cur), kv(nxt),
            pl.BlockSpec((ctx_len, 2 * ATT_KV_DIM), lambda j: (0, 0)),
            tab(tcur), tab(tcur), tab(tprv), tab(tprv), tab(tnxt), tab(tnxt),
            pl.BlockSpec((8, LANE), lambda j: (0, 0)),
        ],
        out_specs=pl.BlockSpec((ATT_BLOCK, ATT_DIM), cur),
        out_shape=jax.ShapeDtypeStruct((t, ATT_DIM), F32),
        compiler_params=_params(("arbitrary",), VMEM_LIMIT),
        name="attention",
    )(pq, pkv, pkv, pkv, pkv, cos, sin, cos, sin, cos, sin, sink)


def _mixfin_kernel(h_ref, mod_ref, pconv_ref, prev_ref, next_ref, cw_ref, of_ref, ob_ref, z_ref, ng_ref, yc_ref,
                   wout_ref, *rest, nblk, blk0, with_router):
    if with_router:
        wr1_ref, wr0_ref, x_ref, hx_ref, lg_ref = rest
    else:
        x_ref, hx_ref = rest
    i = pl.program_id(0) + blk0
    row = _mod_row(mod_ref, i == 0)
    pc = pconv_ref[...]
    u = pc[:, CONV_CH:2 * CONV_CH] * pc[:, 2 * CONV_CH:]
    prow, nrow = _halo_rows(prev_ref, next_ref, i, nblk)
    prow = prow[:, CONV_CH:2 * CONV_CH] * prow[:, 2 * CONV_CH:]
    nrow = nrow[:, CONV_CH:2 * CONV_CH] * nrow[:, 2 * CONV_CH:]
    up, un = _shift_rows(u, prow, nrow)
    cw = cw_ref[...]
    ya = pc[:, :CONV_CH] * (up * cw[0:1, :] + u * cw[1:2, :] + un * cw[2:3, :])
    o = of_ref[...] + ob_ref[...]
    ms = _group_sum(o * o, _head_blockdiag(DN_DIM, DN_HEAD_DIM)) * (1.0 / DN_HEAD_DIM)
    z = z_ref[...]
    yb = o * lax.rsqrt(ms + EPS) * ng_ref[...] * (z * _sigmoid(z))
    mix = jnp.concatenate([ya, yb, yc_ref[...]], axis=1)
    x = h_ref[...] + row[:, 2 * D:3 * D] * _mm(mix, wout_ref[...])
    x_ref[...] = x
    hx = _rmsnorm_rows(x) * (1.0 + row[:, 4 * D:5 * D]) + row[:, 3 * D:4 * D]
    hx_ref[...] = hx.astype(hx_ref.dtype)
    if with_router:
        h1, h0 = _split2(hx)
        d = functools.partial(lax.dot_general, dimension_numbers=_NN, preferred_element_type=F32)
        lg = d(h1, wr1_ref[...]) + (d(h0, wr1_ref[...]) + d(h1, wr0_ref[...]))
        lane = lax.broadcasted_iota(jnp.int32, lg.shape, 1)
        lanef = lane.astype(F32)
        lg = jnp.where(lane < N_EXPERTS, lg, -jnp.inf)
        m1 = jnp.max(lg, axis=-1, keepdims=True)
        i1 = jnp.min(jnp.where(lg == m1, lanef, float(LANE)), axis=-1, keepdims=True)
        rest = jnp.where(lanef == i1, -jnp.inf, lg)
        m2 = jnp.max(rest, axis=-1, keepdims=True)
        i2 = jnp.min(jnp.where(rest == m2, lanef, float(LANE)), axis=-1, keepdims=True)
        e2 = jnp.exp(m2 - m1)
        g1 = 1.0 / (1.0 + e2)
        lg_ref[...] = jnp.where(lane == 0, g1, jnp.where(lane == 1, e2 * g1, jnp.where(lane == 2, i1, i2)))


def _mixer_finish(h, mod, pconv, conv_w, o_f, o_b, pz, norm_g, yc, w_out, ctx_len, with_ctx, router=None):
    t = h.shape[0]
    nblk = t // TM
    blk0 = 0 if with_ctx else ctx_len // TM
    rows = t - blk0 * TM
    r8 = TM // 8
    w = pconv.shape[1]
    cur = lambda i: (i + blk0, 0)
    out_cur = lambda i: (i, 0)
    const = lambda i: (0, 0)
    ng = jnp.tile(norm_g.reshape(1, DN_HEAD_DIM), (1, DN_HEADS))
    in_specs = [
        pl.BlockSpec((TM, D), cur),
        pl.BlockSpec((8, N_MOD * D), const),
        pl.BlockSpec((TM, w), cur),
        pl.BlockSpec((8, w), lambda i: (jnp.maximum((i + blk0) * r8 - 1, 0), 0)),
        pl.BlockSpec((8, w), lambda i: (jnp.minimum((i + blk0 + 1) * r8, t // 8 - 1), 0)),
        pl.BlockSpec((3, CONV_CH), const),
        pl.BlockSpec((TM, DN_DIM), cur),
        pl.BlockSpec((TM, DN_DIM), cur),
        pl.BlockSpec((TM, DN_DIM), cur),
        pl.BlockSpec((1, DN_DIM), const),
        pl.BlockSpec((TM, ATT_DIM), cur),
        pl.BlockSpec((MIX_DIM, D), const),
    ]
    args = [h, mod, pconv, pconv, pconv, conv_w, o_f, o_b, pz, ng, yc, w_out]
    out_specs = [pl.BlockSpec((TM, D), out_cur), pl.BlockSpec((TM, D), out_cur)]
    hx_dtype = BF16 if router is None else F32
    out_shape = [jax.ShapeDtypeStruct((rows, D), F32), jax.ShapeDtypeStruct((rows, D), hx_dtype)]
    if router is not None:
        in_specs += [pl.BlockSpec((D, LANE), const)] * 2
        args += list(router)
        out_specs.append(pl.BlockSpec((TM, LANE), out_cur))
        out_shape.append(jax.ShapeDtypeStruct((rows, LANE), F32))
    return pl.pallas_call(
        functools.partial(_mixfin_kernel, nblk=nblk, blk0=blk0, with_router=router is not None),
        grid=(rows // TM,),
        in_specs=in_specs,
        out_specs=out_specs,
        out_shape=out_shape,
        compiler_params=_params(("arbitrary",), VMEM_LIMIT),
        name="mixer_finish",
    )(*args)


def _ffn_kernel(hx_ref, x_ref, mod_ref, wg_ref, wu_ref, wd_ref, o_ref):
    row = _mod_row(mod_ref, pl.program_id(0) == 0)
    hx = hx_ref[...]
    acc = jnp.zeros((hx.shape[0], D), F32)
    for f in range(0, D_FF, FF_CHUNK):
        g = _mm(hx, wg_ref[:, f:f + FF_CHUNK])
        u = _mm(hx, wu_ref[:, f:f + FF_CHUNK])
        acc = acc + _mm(g * _sigmoid(g) * u, wd_ref[f:f + FF_CHUNK, :])
    o_ref[...] = x_ref[...] + row[:, 5 * D:6 * D] * acc


def _dense_ffn(hx, x, mod, wg, wu, wd):
    t = x.shape[0]
    const = lambda i: (0, 0)
    once = dict(pipeline_mode=pl.Buffered(1))
    return pl.pallas_call(
        _ffn_kernel,
        grid=(t // TM,),
        in_specs=[
            pl.BlockSpec((TM, D), lambda i: (i, 0)),
            pl.BlockSpec((TM, D), lambda i: (i, 0)),
            pl.BlockSpec((8, N_MOD * D), const),
            pl.BlockSpec((D, D_FF), const, **once),
            pl.BlockSpec((D, D_FF), const, **once),
            pl.BlockSpec((D_FF, D), const, **once),
        ],
        out_specs=pl.BlockSpec((TM, D), lambda i: (i, 0)),
        out_shape=jax.ShapeDtypeStruct((t, D), F32),
        compiler_params=_params(("arbitrary",), VMEM_LIMIT),
        name="dense_ffn",
    )(hx, x, mod, wg, wu, wd)


def _moe_kernel(be_ref, nu_ref, xs_ref, wg_ref, wu_ref, wd_ref, y_ref, acc_ref):
    b = pl.program_id(0)
    f = pl.program_id(1)

    @pl.when(b < nu_ref[0])
    def _():
        xs = xs_ref[...].astype(BF16)
        part = jnp.zeros((MOE_TM, D), F32)
        for c in range(0, MOE_TF, MOE_SUB):
            g = _mm(xs, wg_ref[0, :, c:c + MOE_SUB])
            u = _mm(xs, wu_ref[0, :, c:c + MOE_SUB])
            part = part + _mm(g * _sigmoid(g) * u, wd_ref[0, c:c + MOE_SUB, :])

        @pl.when(f == 0)
        def _():
            acc_ref[...] = part

        @pl.when(f != 0)
        def _():
            acc_ref[...] += part

        @pl.when(f == pl.num_programs(1) - 1)
        def _():
            y_ref[...] = acc_ref[...]

    @pl.when(b >= nu_ref[0])
    def _():
        y_ref[...] = jnp.zeros(y_ref.shape, F32)


def _moe_experts(xs, blk_e, n_used, wg, wu, wd):
    cap = xs.shape[0]
    nblk = cap // MOE_TM
    nf = D_FF_EXPERT // MOE_TF

    def fidx(b, f, nu):
        return jnp.where(b < nu[0], f, nf - 1)

    grid_spec = pltpu.PrefetchScalarGridSpec(
        num_scalar_prefetch=2,
        grid=(nblk, nf),
        in_specs=[
            pl.BlockSpec((MOE_TM, D), lambda b, f, be, nu: (b, 0)),
            pl.BlockSpec((1, D, MOE_TF), lambda b, f, be, nu: (be[b], 0, fidx(b, f, nu))),
            pl.BlockSpec((1, D, MOE_TF), lambda b, f, be, nu: (be[b], 0, fidx(b, f, nu))),
            pl.BlockSpec((1, MOE_TF, D), lambda b, f, be, nu: (be[b], fidx(b, f, nu), 0)),
        ],
        out_specs=pl.BlockSpec((MOE_TM, D), lambda b, f, be, nu: (b, 0)),
        scratch_shapes=[pltpu.VMEM((MOE_TM, D), F32)],
    )
    return pl.pallas_call(
        _moe_kernel,
        grid_spec=grid_spec,
        out_shape=jax.ShapeDtypeStruct((cap, D), F32),
        compiler_params=_params(("arbitrary", "arbitrary"), VMEM_LIMIT),
        name="moe_experts",
    )(blk_e, n_used, xs, wg, wu, wd)


def _moe_route(top_e):
    n = top_e.shape[0]
    a = n * TOP_K
    flat_e = top_e.reshape(a)
    onehot = (flat_e[:, None] == jnp.arange(N_EXPERTS, dtype=flat_e.dtype)[None, :]).astype(jnp.int32)
    counts = jnp.sum(onehot, axis=0)
    padded = (counts + MOE_TM - 1) // MOE_TM * MOE_TM
    pad_ends = jnp.cumsum(padded)
    pad_starts = pad_ends - padded
    dest = jnp.sum(onehot * (jnp.cumsum(onehot, axis=0) - onehot + pad_starts[None, :]), axis=1)
    cap = a + N_EXPERTS * MOE_TM
    nblk = cap // MOE_TM
    row_tok = (jnp.arange(cap, dtype=jnp.int32) % n).at[dest].set(
        jnp.arange(a, dtype=jnp.int32) // TOP_K, unique_indices=True, mode="promise_in_bounds")
    blk_start = jnp.arange(nblk, dtype=jnp.int32) * MOE_TM
    blk_e = jnp.minimum(jnp.sum((pad_ends[None, :] <= blk_start[:, None]).astype(jnp.int32), axis=1),
                        N_EXPERTS - 1)
    n_used = (pad_ends[-1] // MOE_TM).astype(jnp.int32).reshape(1)
    last_e = blk_e[jnp.maximum(n_used[0] - 1, 0)]
    blk_e = jnp.where(jnp.arange(nblk) < n_used[0], blk_e, last_e)
    return dest.reshape(n, TOP_K), row_tok, blk_e, n_used


def _final_kernel(x_ref, y0_ref, y1_ref, gt_ref, mod_ref, fg_ref, o_ref):
    mod = mod_ref[...]
    gt = gt_ref[...]
    f = gt[:, 0:1] * y0_ref[...] + gt[:, 1:2] * y1_ref[...]
    x = x_ref[...] + mod[0:1, 5 * D:6 * D] * f
    o_ref[...] = _rmsnorm_rows(x) * fg_ref[...]


def _moe_combine_final(x, y0, y1, gt, mod, final_g):
    n = x.shape[0]
    row = lambda i: (i, 0)
    const = lambda i: (0, 0)
    return pl.pallas_call(
        _final_kernel,
        grid=(n // TM,),
        in_specs=[pl.BlockSpec((TM, D), row)] * 3
        + [pl.BlockSpec((TM, LANE), row), pl.BlockSpec((8, N_MOD * D), const), pl.BlockSpec((1, D), const)],
        out_specs=pl.BlockSpec((TM, D), row),
        out_shape=jax.ShapeDtypeStruct((n, D), F32),
        compiler_params=_params(("arbitrary",), VMEM_LIMIT),
        name="moe_combine_final",
    )(x, y0, y1, gt, mod, final_g.reshape(1, D))


def _rope_tables(n):
    pos = jnp.arange(n, dtype=jnp.int32)
    r = (pos // GRID_W).astype(F32)
    col = (pos % GRID_W).astype(F32)
    inv = ROPE_BASE ** (-jnp.arange(0, AXIS_DIM, 2, dtype=F32) / AXIS_DIM)
    ang = jnp.concatenate([r[:, None] * inv, r[:, None] * inv, col[:, None] * inv, col[:, None] * inv], axis=1)
    sign = jnp.tile(jnp.concatenate([-jnp.ones((AXIS_DIM // 2,), F32), jnp.ones((AXIS_DIM // 2,), F32)]), 2)
    cos = jnp.tile(jnp.cos(ang), (1, LANE // HEAD_DIM))
    sin = jnp.tile(jnp.sin(ang) * sign, (1, LANE // HEAD_DIM))
    return cos, sin


def _prep_w_in(w):
    main = jnp.concatenate([w[:, :_C_A], w[:, _C_Q:_C_END]], axis=1).astype(BF16)
    wab = jnp.zeros((D, LANE), F32).at[:, :N_AB].set(w[:, _C_A:_C_Q])
    wab1 = wab.astype(BF16)
    wab0 = (wab - wab1.astype(F32)).astype(BF16)
    return main, wab1, wab0


def kernel(x, c, ctx, c_ctx, w_mod, b_mod, w_in, w_out, conv_w, dn_conv_w, dn_a_log, dn_dt_bias, dn_norm_g,
           attn_sink, ffn_w_gate, ffn_w_up, ffn_w_down, moe_router, moe_w_gate, moe_w_up, moe_w_down,
           final_norm_g):
    bsz, n, d = x.shape
    ctx_len = ctx.shape[1]
    depth = w_in.shape[0]
    assert bsz == 1 and d == D and ctx_len == TM and n % TM == 0 and n % GRID_W == 0
    cos, sin = _rope_tables(n)
    mods = _mod_vectors(c, c_ctx, w_mod, b_mod)
    h = jnp.concatenate([ctx[0], x[0]], axis=0)
    for layer in range(depth):
        last = layer == depth - 1
        mod = mods[layer]
        w_main, wab1, wab0 = _prep_w_in(w_in[layer])
        pconv, pqkv, pz, pq, pkv, pab = _in_proj(h, mod, w_main, wab1, wab0)
        qn, kn, vv, gb = _dn_prep(pqkv, pab, dn_conv_w[layer], dn_a_log[layer], dn_dt_bias[layer])
        o_f, o_b = _delta_net(qn, kn, vv, gb, ctx_len)
        sink = jnp.zeros((8, LANE), F32).at[0, :ATT_HEADS].set(attn_sink[layer])
        yc = _attention(pq, pkv, cos, sin, sink, ctx_len, with_ctx=not last)
        router = None
        if layer % 2 == 1:
            wr = jnp.zeros((D, LANE), F32).at[:, :N_EXPERTS].set(moe_router[layer // 2])
            wr1 = wr.astype(BF16)
            router = (wr1, (wr - wr1.astype(F32)).astype(BF16))
        outs = _mixer_finish(h, mod, pconv, conv_w[layer], o_f, o_b, pz, dn_norm_g[layer], yc,
                             w_out[layer].astype(BF16), ctx_len, with_ctx=not last, router=router)
        if layer % 2 == 0:
            assert not last
            x1, hx = outs
            i = layer // 2
            h = _dense_ffn(hx, x1, mod, ffn_w_gate[i], ffn_w_up[i], ffn_w_down[i])
        else:
            assert last
            x1, hx, route = outs
            i = layer // 2
            dest, row_tok, blk_e, n_used = _moe_route(route[:, 2:2 + TOP_K].astype(jnp.int32))
            take = lambda rows_, idx: rows_.at[idx].get(mode="promise_in_bounds")
            xs = take(hx, row_tok)
            y = _moe_experts(xs, blk_e, n_used, moe_w_gate[i].astype(BF16), moe_w_up[i].astype(BF16),
                             moe_w_down[i].astype(BF16))
            y0 = take(y, dest[:, 0])
            y1 = take(y, dest[:, 1])
            h = _moe_combine_final(x1, y0, y1, route, mod, final_norm_g)
    return h.reshape(bsz, n, d)
```

```python
import functools

import jax
import jax.numpy as jnp
from jax import lax
from jax.experimental import pallas as pl
from jax.experimental.pallas import tpu as pltpu

F32 = jnp.float32
BF16 = jnp.bfloat16

D = 1024
N_MOD = 6
EPS = 1e-6
NEG = -1e30
GRID_W = 64

CONV_CH = 256
DN_HEADS = 6
DN_HEAD_DIM = 64
DN_DIM = DN_HEADS * DN_HEAD_DIM
DN_CHUNK = 64
DN_SUB = 16
DN_STEP_CHUNKS = 4
ATT_HEADS = 6
ATT_KV_HEADS = 2
ATT_GROUP = ATT_HEADS // ATT_KV_HEADS
HEAD_DIM = 64
ATT_DIM = ATT_HEADS * HEAD_DIM
ATT_KV_DIM = ATT_KV_HEADS * HEAD_DIM
ATT_BLOCK = 128
ROPE_BASE = 10000.0
AXIS_DIM = HEAD_DIM // 2
MIX_DIM = CONV_CH + DN_DIM + ATT_DIM

D_FF = 2816
N_EXPERTS = 8
TOP_K = 2
D_FF_EXPERT = 3584

TM = 256
FF_CHUNK = 256
MOE_TM = 512
MOE_TF = 1792
MOE_SUB = 256
LANE = 128
VMEM_LIMIT = 56 * 1024 * 1024

_C_QKV = 3 * CONV_CH
_C_Z = _C_QKV + 3 * DN_DIM
_C_A = _C_Z + DN_DIM
_C_Q = _C_A + 4 * DN_HEADS
_C_K = _C_Q + ATT_DIM
_C_V = _C_K + ATT_KV_DIM
_C_END = _C_V + ATT_KV_DIM
N_AB = 4 * DN_HEADS


def _params(sem=None, vmem=None):
    kw = {}
    if sem is not None:
        kw["dimension_semantics"] = sem
    if vmem is not None:
        kw["vmem_limit_bytes"] = vmem
    return pltpu.CompilerParams(**kw)


def _split2(a):
    hi = a.astype(BF16)
    lo = (a - hi.astype(F32)).astype(BF16)
    return hi, lo


def _split3(a):
    hi = a.astype(BF16)
    r = a - hi.astype(F32)
    mid = r.astype(BF16)
    lo = (r - mid.astype(F32)).astype(BF16)
    return hi, mid, lo


_NN = (((1,), (0,)), ((), ()))
_NT = (((1,), (1,)), ((), ()))
_TN = (((0,), (0,)), ((), ()))


def _mm(a, b, dims=_NN):
    return lax.dot_general(a.astype(BF16), b.astype(BF16), dims, preferred_element_type=F32)


def _mm3(a, b, dims=_NN):
    a1, a0 = _split2(a)
    b1, b0 = _split2(b)
    d = functools.partial(lax.dot_general, dimension_numbers=dims, preferred_element_type=F32)
    return d(a1, b1) + (d(a1, b0) + d(a0, b1))


_BNN = (((2,), (1,)), ((0,), (0,)))
_BNT = (((2,), (2,)), ((0,), (0,)))
_BTN = (((1,), (1,)), ((0,), (0,)))


def _bmm(a, b, dims=_BNN):
    return lax.dot_general(a.astype(BF16), b.astype(BF16), dims, preferred_element_type=F32)


def _sigmoid(x):
    return 1.0 / (1.0 + jnp.exp(-x))


def _softplus(x):
    return jnp.maximum(x, 0.0) + jnp.log1p(jnp.exp(-jnp.abs(x)))


def _mod_row(mod_ref, is_ctx):
    mod = mod_ref[...]
    return jnp.where(is_ctx, mod[1:2, :], mod[0:1, :])


def _rmsnorm_rows(x):
    return x * lax.rsqrt(jnp.mean(x * x, axis=-1, keepdims=True) + EPS)


def _shift_rows(u, prow, nrow):
    n = u.shape[0]
    rid = lax.broadcasted_iota(jnp.int32, u.shape, 0)
    up = jnp.where(rid == 0, prow, pltpu.roll(u, 1, 0))
    un = jnp.where(rid == n - 1, nrow, pltpu.roll(u, n - 1, 0))
    return up, un


def _same_group(shape, group):
    sh = group.bit_length() - 1
    assert 1 << sh == group
    return (lax.broadcasted_iota(jnp.int32, shape, 0) >> sh) == (lax.broadcasted_iota(jnp.int32, shape, 1) >> sh)


def _head_blockdiag(n, group):
    return jnp.where(_same_group((n, n), group), 1.0, 0.0).astype(BF16)


def _group_sum(t, bd):
    hi, lo = _split2(t)
    d = functools.partial(lax.dot_general, dimension_numbers=_NN, preferred_element_type=F32)
    return d(hi, bd) + d(lo, bd)


MOD_TN = 1536


def _mod_kernel(s_ref, w_ref, b_ref, o_ref):
    s = s_ref[...]
    s = s * _sigmoid(s)
    o_ref[0] = _mm3(s, w_ref[0]) + b_ref[0]


def _mod_vectors(c, c_ctx, w_mod, b_mod):
    depth = w_mod.shape[0]
    s = jnp.zeros((8, D), F32).at[0].set(c[0]).at[1].set(c_ctx)
    return pl.pallas_call(
        _mod_kernel,
        grid=(depth, N_MOD * D // MOD_TN),
        in_specs=[
            pl.BlockSpec((8, D), lambda l, j: (0, 0)),
            pl.BlockSpec((1, D, MOD_TN), lambda l, j: (l, 0, j)),
            pl.BlockSpec((1, 1, MOD_TN), lambda l, j: (l, 0, j)),
        ],
        out_specs=pl.BlockSpec((1, 8, MOD_TN), lambda l, j: (l, 0, j)),
        out_shape=jax.ShapeDtypeStruct((depth, 8, N_MOD * D), F32),
        compiler_params=_params(("arbitrary", "arbitrary"), VMEM_LIMIT),
        name="mod_vectors",
    )(s, w_mod, b_mod.reshape(depth, 1, N_MOD * D))


def _halo_valid(i, nblk):
    return jnp.logical_and(i != 0, i != 1), jnp.logical_and(i != 0, i != nblk - 1)


def _halo_rows(prev_ref, next_ref, i, nblk):
    pvalid, nvalid = _halo_valid(i, nblk)
    prow = jnp.where(pvalid, prev_ref[7:8, :], 0.0)
    nrow = jnp.where(nvalid, next_ref[0:1, :], 0.0)
    return prow, nrow


def _in_kernel(h_ref, hprev_ref, hnext_ref, mod_ref, w_ref, wab1_ref, wab0_ref, cw_ref, alog_ref, dtb_ref,
               pconv_ref, pz_ref, pq_ref, pkv_ref, q_ref, k_ref, v_ref, gb_ref, *, nblk):
    i = pl.program_id(0)
    row = _mod_row(mod_ref, i == 0)
    norm_mod = lambda x: _rmsnorm_rows(x) * (1.0 + row[:, D:2 * D]) + row[:, 0:D]
    hm = norm_mod(h_ref[...])
    h1, h0 = _split2(hm)
    halo = norm_mod(jnp.concatenate([hprev_ref[...], hnext_ref[...]], axis=0)).astype(BF16)
    d = functools.partial(lax.dot_general, dimension_numbers=_NN, preferred_element_type=F32)
    tm = h1.shape[0]
    c0 = 3 * CONV_CH
    c1 = c0 + 3 * DN_DIM
    c2 = c1 + DN_DIM
    c3 = c2 + ATT_DIM
    pconv_ref[...] = d(h1, w_ref[:, 0:c0])
    pz_ref[...] = d(h1, w_ref[:, c1:c2])
    pq_ref[...] = d(h1, w_ref[:, c2:c3])
    pkv_ref[...] = d(h1, w_ref[:, c3:c3 + 2 * ATT_KV_DIM])
    qkv = d(jnp.concatenate([h1, halo], axis=0), w_ref[:, c0:c1])
    ab = d(h1, wab1_ref[...]) + (d(h0, wab1_ref[...]) + d(h1, wab0_ref[...]))

    u = qkv[0:tm, :]
    pvalid, nvalid = _halo_valid(i, nblk)
    prow = jnp.where(pvalid, qkv[tm + 7:tm + 8, :], 0.0)
    nrow = jnp.where(nvalid, qkv[tm + 8:tm + 9, :], 0.0)
    up, un = _shift_rows(u, prow, nrow)
    cw = cw_ref[...]
    y = up * cw[0:1, :] + u * cw[1:2, :] + un * cw[2:3, :]
    y = y * _sigmoid(y)
    q = y[:, 0:DN_DIM]
    k = y[:, DN_DIM:2 * DN_DIM]
    bd = _head_blockdiag(DN_DIM, DN_HEAD_DIM)
    q_ref[...] = q * lax.rsqrt(_group_sum(q * q, bd) + 1e-6) * (DN_HEAD_DIM ** -0.5)
    k_ref[...] = k * lax.rsqrt(_group_sum(k * k, bd) + 1e-6)
    v_ref[...] = y[:, 2 * DN_DIM:3 * DN_DIM]
    g = -jnp.exp(alog_ref[...]) * _softplus(ab + dtb_ref[...])
    lane = lax.broadcasted_iota(jnp.int32, ab.shape, 1)
    gb_ref[...] = jnp.where(lane < 2 * DN_HEADS, g, _sigmoid(ab))


def _in_proj(h, mod, w_main, wab1, wab0, dn_conv_w, a_log, dt_bias):
    t = h.shape[0]
    nblk = t // TM
    r8 = TM // 8
    alog = jnp.zeros((1, LANE), F32).at[0, :2 * DN_HEADS].set(a_log.reshape(-1))
    dtb = jnp.zeros((1, LANE), F32).at[0, :2 * DN_HEADS].set(dt_bias.reshape(-1))
    widths = (3 * CONV_CH, DN_DIM, ATT_DIM, 2 * ATT_KV_DIM, DN_DIM, DN_DIM, DN_DIM, LANE)
    const = lambda i: (0, 0)
    return pl.pallas_call(
        functools.partial(_in_kernel, nblk=nblk),
        grid=(nblk,),
        in_specs=[
            pl.BlockSpec((TM, D), lambda i: (i, 0)),
            pl.BlockSpec((8, D), lambda i: (jnp.maximum(i * r8 - 1, 0), 0)),
            pl.BlockSpec((8, D), lambda i: (jnp.minimum((i + 1) * r8, t // 8 - 1), 0)),
            pl.BlockSpec((8, N_MOD * D), const),
            pl.BlockSpec(w_main.shape, const),
            pl.BlockSpec(wab1.shape, const),
            pl.BlockSpec(wab0.shape, const),
            pl.BlockSpec((3, 3 * DN_DIM), const),
            pl.BlockSpec((1, LANE), const),
            pl.BlockSpec((1, LANE), const),
        ],
        out_specs=[pl.BlockSpec((TM, w), lambda i: (i, 0)) for w in widths],
        out_shape=[jax.ShapeDtypeStruct((t, w), F32) for w in widths],
        compiler_params=_params(("arbitrary",), VMEM_LIMIT),
        name="in_proj",
    )(h, h, h, mod, w_main, wab1, wab0, dn_conv_w, alog, dtb)


def _dn_chunk(rev, q_ref, k_ref, v_ref, gb_ref, o_ref, s_ref):
    c_ = DN_CHUNK
    ri = lax.broadcasted_iota(jnp.int32, (c_, c_), 0)
    ci = lax.broadcasted_iota(jnp.int32, (c_, c_), 1)
    incl = (ri <= ci) if rev else (ri >= ci)
    strict = (ri < ci) if rev else (ri > ci)
    same_sub = _same_group((c_, c_), DN_SUB)
    eye = jnp.where(ri == ci, 1.0, 0.0)
    tri = jnp.where(incl, 1.0, 0.0).astype(BF16)
    last = 0 if rev else c_ - 1

    nchunks = q_ref.shape[0] // c_
    nh = DN_HEADS
    col0 = nh if rev else 0
    rows = lambda g: slice(g * c_, (g + 1) * c_)
    lanes = lambda h: slice(h * DN_HEAD_DIM, (h + 1) * DN_HEAD_DIM)

    def stack(fn):
        return jnp.stack([fn(g, h) for g in range(nchunks) for h in range(nh)])

    gb = gb_ref[...]
    gcs = [_cumsum_rows(tri, gb[rows(g), :]) for g in range(nchunks)]
    gcts = [gc.T for gc in gcs]
    q = stack(lambda g, h: q_ref[rows(g), lanes(h)])
    k = stack(lambda g, h: k_ref[rows(g), lanes(h)])
    v = stack(lambda g, h: v_ref[rows(g), lanes(h)])
    gcol = stack(lambda g, h: gcs[g][:, col0 + h:col0 + h + 1])
    grow = stack(lambda g, h: gcts[g][col0 + h:col0 + h + 1, :])
    beta = stack(lambda g, h: gb[rows(g), 2 * nh + col0 + h:2 * nh + col0 + h + 1])
    glast = gcol[:, last:last + 1, :]
    decay = jnp.where(incl, jnp.exp(jnp.where(incl, gcol - grow, 0.0)), 0.0)
    eg = jnp.exp(gcol)
    kb = k * beta
    a = jnp.where(strict, _bmm(kb, k, _BNT) * decay, 0.0)
    qk = jnp.where(incl, _bmm(q, k, _BNT) * decay, 0.0)
    ad = jnp.where(same_sub, a, 0.0)
    ao = a - ad
    p = eye - ad
    n2 = _bmm(ad, ad)
    p = p + _bmm(p, n2)
    n4 = _bmm(n2, n2)
    p = p + _bmm(p, n4)
    n8 = _bmm(n4, n4)
    dinv = p + _bmm(p, n8)
    m = _bmm(dinv, ao)
    m2 = _bmm(m, m)
    y = _bmm(dinv, jnp.concatenate([v * beta, kb * eg], axis=-1))
    z = y + _bmm(m2, y)
    x = z - _bmm(m, z)
    u = x[:, :, :DN_HEAD_DIM]
    w = x[:, :, DN_HEAD_DIM:]
    qg = q * eg
    kd = k * jnp.exp(glast - gcol)
    gl = jnp.exp(glast)
    s = s_ref[col0:col0 + nh]
    for g in (reversed(range(nchunks)) if rev else range(nchunks)):
        b = slice(g * nh, (g + 1) * nh)
        v_new = u[b] - _bmm(w[b], s)
        o = _bmm(qg[b], s) + _bmm(qk[b], v_new)
        s = s * gl[b] + _bmm(kd[b], v_new, _BTN)
        for h in range(nh):
            o_ref[rows(g), lanes(h)] = o[h]
    s_ref[col0:col0 + nh] = s


def _cumsum_rows(tri_bf16, g):
    g2, g1, g0 = _split3(g)
    d = functools.partial(lax.dot_general, dimension_numbers=_NN, preferred_element_type=F32)
    return d(tri_bf16, g2) + (d(tri_bf16, g1) + d(tri_bf16, g0))


def _dn_kernel(qf, kf, vf, gf, qb, kb, vb, gbb, of_ref, ob_ref, s_ref):
    @pl.when(pl.program_id(0) == 0)
    def _():
        s_ref[...] = jnp.zeros(s_ref.shape, F32)

    _dn_chunk(False, qf, kf, vf, gf, of_ref, s_ref)
    _dn_chunk(True, qb, kb, vb, gbb, ob_ref, s_ref)


def _delta_net(q, k, v, gb, ctx_len):
    t = q.shape[0]
    rows = DN_STEP_CHUNKS * DN_CHUNK
    assert ctx_len == rows and t % rows == 0
    nstep = t // rows

    def fwd(s):
        return (s, 0)

    def bwd(s):
        return (jnp.where(s == 0, 0, nstep - s), 0)

    wide = lambda im: pl.BlockSpec((rows, DN_DIM), im)
    narrow = lambda im: pl.BlockSpec((rows, LANE), im)
    return pl.pallas_call(
        _dn_kernel,
        grid=(nstep,),
        in_specs=[wide(fwd), wide(fwd), wide(fwd), narrow(fwd), wide(bwd), wide(bwd), wide(bwd), narrow(bwd)],
        out_specs=[wide(fwd), wide(bwd)],
        out_shape=[jax.ShapeDtypeStruct((t, DN_DIM), F32)] * 2,
        scratch_shapes=[pltpu.VMEM((2 * DN_HEADS, DN_HEAD_DIM, DN_HEAD_DIM), F32)],
        compiler_params=_params(("arbitrary",), VMEM_LIMIT),
        name="delta_net",
    )(q, k, v, gb, q, k, v, gb)


def _rope(x, cos, sin):
    w = x.shape[1]
    lane = lax.broadcasted_iota(jnp.int32, x.shape, 1)
    first_half = (lane & (AXIS_DIM - 1)) < (AXIS_DIM // 2)
    swapped = jnp.where(first_half, pltpu.roll(x, w - AXIS_DIM // 2, 1), pltpu.roll(x, AXIS_DIM // 2, 1))
    return x * cos + swapped * sin


def _softmax_av(s, sink, vals):
    m = jnp.maximum(jnp.max(s, axis=-1, keepdims=True), sink)
    p = jnp.exp(s - m)
    denom = jnp.sum(p, axis=-1, keepdims=True) + jnp.exp(sink - m)
    return _mm(p, vals) / denom


def _attend(q, keys, vals, valid, sink_all, o_ref):
    b = q.shape[0]
    for kvh in range(ATT_KV_HEADS):
        kl = slice(kvh * HEAD_DIM, (kvh + 1) * HEAD_DIM)
        heads = range(kvh * ATT_GROUP, (kvh + 1) * ATT_GROUP)
        qs = jnp.concatenate([q[:, h * HEAD_DIM:(h + 1) * HEAD_DIM] for h in heads], axis=0)
        sink = jnp.concatenate([jnp.broadcast_to(sink_all[0:1, h:h + 1], (b, 1)) for h in heads], axis=0)
        s = _mm(qs, keys[:, kl], _NT)
        if valid is not None:
            s = jnp.where(valid, s, NEG)
        o = _softmax_av(s, sink, vals[:, kl])
        for g, h in enumerate(heads):
            o_ref[:, h * HEAD_DIM:(h + 1) * HEAD_DIM] = o[g * b:(g + 1) * b, :]


def _attn_kernel(q_ref, kp_ref, kc_ref, kn_ref, kctx_ref, cq_ref, sq_ref, cp_ref, sp_ref, cn_ref, sn_ref,
                 sink_ref, o_ref, *, nb, off, with_ctx):
    j = pl.program_id(0)
    b = ATT_BLOCK
    scale = HEAD_DIM ** -0.5

    @pl.when(j < off)
    def _():
        if with_ctx:
            kvx = kctx_ref[...]
            _attend(q_ref[...] * scale, kvx[:, :ATT_KV_DIM], kvx[:, ATT_KV_DIM:], None, sink_ref[...], o_ref)
        else:
            o_ref[...] = jnp.zeros(o_ref.shape, F32)

    @pl.when(j >= off)
    def _():
        i = j - off
        cq = cq_ref[...]
        sq = sq_ref[...]
        q = q_ref[...]
        q = jnp.concatenate([_rope(q[:, l * LANE:(l + 1) * LANE], cq, sq) for l in range(ATT_DIM // LANE)], axis=1)
        kvp, kvc, kvn, kvx = kp_ref[...], kc_ref[...], kn_ref[...], kctx_ref[...]
        kp = _rope(kvp[:, :ATT_KV_DIM], cp_ref[...], sp_ref[...])
        kc = _rope(kvc[:, :ATT_KV_DIM], cq, sq)
        kn = _rope(kvn[:, :ATT_KV_DIM], cn_ref[...], sn_ref[...])
        keys = jnp.concatenate([kp, kc, kn, kvx[:, :ATT_KV_DIM]], axis=0)
        vals = jnp.concatenate(
            [kvp[:, ATT_KV_DIM:], kvc[:, ATT_KV_DIM:], kvn[:, ATT_KV_DIM:], kvx[:, ATT_KV_DIM:]], axis=0)
        nk = keys.shape[0]
        c = lax.broadcasted_iota(jnp.int32, (1, nk), 1)
        far = 4 * nk
        ccol = jnp.where(c >= 3 * b, b + ((c - 3 * b) & (b - 1)), c)
        ccol = jnp.where(jnp.logical_and(c < b, i == 0), -far, ccol)
        ccol = jnp.where(jnp.logical_and(jnp.logical_and(c >= 2 * b, c < 3 * b), i == nb - 1), far, ccol)
        r = lax.broadcasted_iota(jnp.int32, (ATT_GROUP * b, 1), 0) & (b - 1)
        valid = lax.bitcast_convert_type(ccol - r, jnp.uint32) <= jnp.uint32(2 * b)
        _attend(q * scale, keys, vals, valid, sink_ref[...], o_ref)


def _attention(pq, pkv, cos, sin, sink, ctx_len, with_ctx):
    t = pq.shape[0]
    n = t - ctx_len
    nb = n // ATT_BLOCK
    off = ctx_len // ATT_BLOCK
    kv = lambda im: pl.BlockSpec((ATT_BLOCK, 2 * ATT_KV_DIM), im)
    tab = lambda im: pl.BlockSpec((ATT_BLOCK, LANE), im)
    lat = lambda j: jnp.maximum(j - off, 0)
    cur = lambda j: (j, 0)
    prv = lambda j: (jnp.maximum(lat(j) - 1, 0) + off, 0)
    nxt = lambda j: (jnp.minimum(lat(j) + 1, nb - 1) + off, 0)
    tcur = lambda j: (lat(j), 0)
    tprv = lambda j: (jnp.maximum(lat(j) - 1, 0), 0)
    tnxt = lambda j: (jnp.minimum(lat(j) + 1, nb - 1), 0)
    return pl.pallas_call(
        functools.partial(_attn_kernel, nb=nb, off=off, with_ctx=with_ctx),
        grid=(off + nb,),
        in_specs=[
            pl.BlockSpec((ATT_BLOCK, ATT_DIM), cur),
            kv(prv), kv(cur), kv(nxt),
            pl.BlockSpec((ctx_len, 2 * ATT_KV_DIM), lambda j: (0, 0)),
            tab(tcur), tab(tcur), tab(tprv), tab(tprv), tab(tnxt), tab(tnxt),
            pl.BlockSpec((8, LANE), lambda j: (0, 0)),
        ],
        out_specs=pl.BlockSpec((ATT_BLOCK, ATT_DIM), cur),
        out_shape=jax.ShapeDtypeStruct((t, ATT_DIM), F32),
        compiler_params=_params(("arbitrary",), VMEM_LIMIT),
        name="attention",
    )(pq, pkv, pkv, pkv, pkv, cos, sin, cos, sin, cos, sin, sink)


def _mixfin_kernel(h_ref, mod_ref, pconv_ref, prev_ref, next_ref, cw_ref, of_ref, ob_ref, z_ref, ng_ref, yc_ref,
                   wout_ref, *rest, nblk, blk0, with_router):
    if with_router:
        wr1_ref, wr0_ref, x_ref, hx_ref, lg_ref = rest
    else:
        x_ref, hx_ref = rest
    i = pl.program_id(0) + blk0
    row = _mod_row(mod_ref, i == 0)
    pc = pconv_ref[...]
    u = pc[:, CONV_CH:2 * CONV_CH] * pc[:, 2 * CONV_CH:]
    prow, nrow = _halo_rows(prev_ref, next_ref, i, nblk)
    prow = prow[:, CONV_CH:2 * CONV_CH] * prow[:, 2 * CONV_CH:]
    nrow = nrow[:, CONV_CH:2 * CONV_CH] * nrow[:, 2 * CONV_CH:]
    up, un = _shift_rows(u, prow, nrow)
    cw = cw_ref[...]
    ya = pc[:, :CONV_CH] * (up * cw[0:1, :] + u * cw[1:2, :] + un * cw[2:3, :])
    o = of_ref[...] + ob_ref[...]
    ms = _group_sum(o * o, _head_blockdiag(DN_DIM, DN_HEAD_DIM)) * (1.0 / DN_HEAD_DIM)
    z = z_ref[...]
    yb = o * lax.rsqrt(ms + EPS) * ng_ref[...] * (z * _sigmoid(z))
    mix = jnp.concatenate([ya, yb, yc_ref[...]], axis=1)
    x = h_ref[...] + row[:, 2 * D:3 * D] * _mm(mix, wout_ref[...])
    x_ref[...] = x
    hx = _rmsnorm_rows(x) * (1.0 + row[:, 4 * D:5 * D]) + row[:, 3 * D:4 * D]
    hx_ref[...] = hx.astype(hx_ref.dtype)
    if with_router:
        h1, h0 = _split2(hx)
        d = functools.partial(lax.dot_general, dimension_numbers=_NN, preferred_element_type=F32)
        lg = d(h1, wr1_ref[...]) + (d(h0, wr1_ref[...]) + d(h1, wr0_ref[...]))
        lane = lax.broadcasted_iota(jnp.int32, lg.shape, 1)
        lanef = lane.astype(F32)
        lg = jnp.where(lane < N_EXPERTS, lg, -jnp.inf)
        m1 = jnp.max(lg, axis=-1, keepdims=True)
        i1 = jnp.min(jnp.where(lg == m1, lanef, float(LANE)), axis=-1, keepdims=True)
        rest = jnp.where(lanef == i1, -jnp.inf, lg)
        m2 = jnp.max(rest, axis=-1, keepdims=True)
        i2 = jnp.min(jnp.where(rest == m2, lanef, float(LANE)), axis=-1, keepdims=True)
        e2 = jnp.exp(m2 - m1)
        g1 = 1.0 / (1.0 + e2)
        lg_ref[...] = jnp.where(lane == 0, g1, jnp.where(lane == 1, e2 * g1, jnp.where(lane == 2, i1, i2)))


def _mixer_finish(h, mod, pconv, conv_w, o_f, o_b, pz, norm_g, yc, w_out, ctx_len, with_ctx, router=None):
    t = h.shape[0]
    nblk = t // TM
    blk0 = 0 if with_ctx else ctx_len // TM
    rows = t - blk0 * TM
    r8 = TM // 8
    w = pconv.shape[1]
    cur = lambda i: (i + blk0, 0)
    out_cur = lambda i: (i, 0)
    const = lambda i: (0, 0)
    ng = jnp.tile(norm_g.reshape(1, DN_HEAD_DIM), (1, DN_HEADS))
    in_specs = [
        pl.BlockSpec((TM, D), cur),
        pl.BlockSpec((8, N_MOD * D), const),
        pl.BlockSpec((TM, w), cur),
        pl.BlockSpec((8, w), lambda i: (jnp.maximum((i + blk0) * r8 - 1, 0), 0)),
        pl.BlockSpec((8, w), lambda i: (jnp.minimum((i + blk0 + 1) * r8, t // 8 - 1), 0)),
        pl.BlockSpec((3, CONV_CH), const),
        pl.BlockSpec((TM, DN_DIM), cur),
        pl.BlockSpec((TM, DN_DIM), cur),
        pl.BlockSpec((TM, DN_DIM), cur),
        pl.BlockSpec((1, DN_DIM), const),
        pl.BlockSpec((TM, ATT_DIM), cur),
        pl.BlockSpec((MIX_DIM, D), const),
    ]
    args = [h, mod, pconv, pconv, pconv, conv_w, o_f, o_b, pz, ng, yc, w_out]
    out_specs = [pl.BlockSpec((TM, D), out_cur), pl.BlockSpec((TM, D), out_cur)]
    hx_dtype = BF16 if router is None else F32
    out_shape = [jax.ShapeDtypeStruct((rows, D), F32), jax.ShapeDtypeStruct((rows, D), hx_dtype)]
    if router is not None:
        in_specs += [pl.BlockSpec((D, LANE), const)] * 2
        args += list(router)
        out_specs.append(pl.BlockSpec((TM, LANE), out_cur))
        out_shape.append(jax.ShapeDtypeStruct((rows, LANE), F32))
    return pl.pallas_call(
        functools.partial(_mixfin_kernel, nblk=nblk, blk0=blk0, with_router=router is not None),
        grid=(rows // TM,),
        in_specs=in_specs,
        out_specs=out_specs,
        out_shape=out_shape,
        compiler_params=_params(("arbitrary",), VMEM_LIMIT),
        name="mixer_finish",
    )(*args)


def _ffn_kernel(hx_ref, x_ref, mod_ref, wg_ref, wu_ref, wd_ref, o_ref):
    row = _mod_row(mod_ref, pl.program_id(0) == 0)
    hx = hx_ref[...]
    acc = jnp.zeros((hx.shape[0], D), F32)
    for f in range(0, D_FF, FF_CHUNK):
        g = _mm(hx, wg_ref[:, f:f + FF_CHUNK])
        u = _mm(hx, wu_ref[:, f:f + FF_CHUNK])
        acc = acc + _mm(g * _sigmoid(g) * u, wd_ref[f:f + FF_CHUNK, :])
    o_ref[...] = x_ref[...] + row[:, 5 * D:6 * D] * acc


def _dense_ffn(hx, x, mod, wg, wu, wd):
    t = x.shape[0]
    const = lambda i: (0, 0)
    once = dict(pipeline_mode=pl.Buffered(1))
    return pl.pallas_call(
        _ffn_kernel,
        grid=(t // TM,),
        in_specs=[
            pl.BlockSpec((TM, D), lambda i: (i, 0)),
            pl.BlockSpec((TM, D), lambda i: (i, 0)),
            pl.BlockSpec((8, N_MOD * D), const),
            pl.BlockSpec((D, D_FF), const, **once),
            pl.BlockSpec((D, D_FF), const, **once),
            pl.BlockSpec((D_FF, D), const, **once),
        ],
        out_specs=pl.BlockSpec((TM, D), lambda i: (i, 0)),
        out_shape=jax.ShapeDtypeStruct((t, D), F32),
        compiler_params=_params(("arbitrary",), VMEM_LIMIT),
        name="dense_ffn",
    )(hx, x, mod, wg, wu, wd)


def _moe_kernel(be_ref, nu_ref, xs_ref, wg_ref, wu_ref, wd_ref, y_ref, acc_ref):
    b = pl.program_id(0)
    f = pl.program_id(1)

    @pl.when(b < nu_ref[0])
    def _():
        xs = xs_ref[...].astype(BF16)
        part = jnp.zeros((MOE_TM, D), F32)
        for c in range(0, MOE_TF, MOE_SUB):
            g = _mm(xs, wg_ref[0, :, c:c + MOE_SUB])
            u = _mm(xs, wu_ref[0, :, c:c + MOE_SUB])
            part = part + _mm(g * _sigmoid(g) * u, wd_ref[0, c:c + MOE_SUB, :])

        @pl.when(f == 0)
        def _():
            acc_ref[...] = part

        @pl.when(f != 0)
        def _():
            acc_ref[...] += part

        @pl.when(f == pl.num_programs(1) - 1)
        def _():
            y_ref[...] = acc_ref[...]

    @pl.when(b >= nu_ref[0])
    def _():
        y_ref[...] = jnp.zeros(y_ref.shape, F32)


def _moe_experts(xs, blk_e, n_used, wg, wu, wd):
    cap = xs.shape[0]
    nblk = cap // MOE_TM
    nf = D_FF_EXPERT // MOE_TF

    def fidx(b, f, nu):
        return jnp.where(b < nu[0], f, nf - 1)

    grid_spec = pltpu.PrefetchScalarGridSpec(
        num_scalar_prefetch=2,
        grid=(nblk, nf),
        in_specs=[
            pl.BlockSpec((MOE_TM, D), lambda b, f, be, nu: (b, 0)),
            pl.BlockSpec((1, D, MOE_TF), lambda b, f, be, nu: (be[b], 0, fidx(b, f, nu))),
            pl.BlockSpec((1, D, MOE_TF), lambda b, f, be, nu: (be[b], 0, fidx(b, f, nu))),
            pl.BlockSpec((1, MOE_TF, D), lambda b, f, be, nu: (be[b], fidx(b, f, nu), 0)),
        ],
        out_specs=pl.BlockSpec((MOE_TM, D), lambda b, f, be, nu: (b, 0)),
        scratch_shapes=[pltpu.VMEM((MOE_TM, D), F32)],
    )
    return pl.pallas_call(
        _moe_kernel,
        grid_spec=grid_spec,
        out_shape=jax.ShapeDtypeStruct((cap, D), F32),
        compiler_params=_params(("arbitrary", "arbitrary"), VMEM_LIMIT),
        name="moe_experts",
    )(blk_e, n_used, xs, wg, wu, wd)


def _moe_route(top_e):
    n = top_e.shape[0]
    a = n * TOP_K
    flat_e = top_e.reshape(a)
    onehot = (flat_e[:, None] == jnp.arange(N_EXPERTS, dtype=flat_e.dtype)[None, :]).astype(jnp.int32)
    counts = jnp.sum(onehot, axis=0)
    padded = (counts + MOE_TM - 1) // MOE_TM * MOE_TM
    pad_ends = jnp.cumsum(padded)
    pad_starts = pad_ends - padded
    dest = jnp.sum(onehot * (jnp.cumsum(onehot, axis=0) - onehot + pad_starts[None, :]), axis=1)
    cap = a + N_EXPERTS * MOE_TM
    nblk = cap // MOE_TM
    row_tok = (jnp.arange(cap, dtype=jnp.int32) % n).at[dest].set(
        jnp.arange(a, dtype=jnp.int32) // TOP_K, unique_indices=True, mode="promise_in_bounds")
    blk_start = jnp.arange(nblk, dtype=jnp.int32) * MOE_TM
    blk_e = jnp.minimum(jnp.sum((pad_ends[None, :] <= blk_start[:, None]).astype(jnp.int32), axis=1),
                        N_EXPERTS - 1)
    n_used = (pad_ends[-1] // MOE_TM).astype(jnp.int32).reshape(1)
    last_e = blk_e[jnp.maximum(n_used[0] - 1, 0)]
    blk_e = jnp.where(jnp.arange(nblk) < n_used[0], blk_e, last_e)
    return dest.reshape(n, TOP_K), row_tok, blk_e, n_used


def _final_kernel(x_ref, y0_ref, y1_ref, gt_ref, mod_ref, fg_ref, o_ref):
    mod = mod_ref[...]
    gt = gt_ref[...]
    f = gt[:, 0:1] * y0_ref[...] + gt[:, 1:2] * y1_ref[...]
    x = x_ref[...] + mod[0:1, 5 * D:6 * D] * f
    o_ref[...] = _rmsnorm_rows(x) * fg_ref[...]


def _moe_combine_final(x, y0, y1, gt, mod, final_g):
    n = x.shape[0]
    row = lambda i: (i, 0)
    const = lambda i: (0, 0)
    return pl.pallas_call(
        _final_kernel,
        grid=(n // TM,),
        in_specs=[pl.BlockSpec((TM, D), row)] * 3
        + [pl.BlockSpec((TM, LANE), row), pl.BlockSpec((8, N_MOD * D), const), pl.BlockSpec((1, D), const)],
        out_specs=pl.BlockSpec((TM, D), row),
        out_shape=jax.ShapeDtypeStruct((n, D), F32),
        compiler_params=_params(("arbitrary",), VMEM_LIMIT),
        name="moe_combine_final",
    )(x, y0, y1, gt, mod, final_g.reshape(1, D))


def _rope_tables(n):
    pos = jnp.arange(n, dtype=jnp.int32)
    r = (pos // GRID_W).astype(F32)
    col = (pos % GRID_W).astype(F32)
    inv = ROPE_BASE ** (-jnp.arange(0, AXIS_DIM, 2, dtype=F32) / AXIS_DIM)
    ang = jnp.concatenate([r[:, None] * inv, r[:, None] * inv, col[:, None] * inv, col[:, None] * inv], axis=1)
    sign = jnp.tile(jnp.concatenate([-jnp.ones((AXIS_DIM // 2,), F32), jnp.ones((AXIS_DIM // 2,), F32)]), 2)
    cos = jnp.tile(jnp.cos(ang), (1, LANE // HEAD_DIM))
    sin = jnp.tile(jnp.sin(ang) * sign, (1, LANE // HEAD_DIM))
    return cos, sin


def _prep_w_in(w):
    main = jnp.concatenate([w[:, :_C_A], w[:, _C_Q:_C_END]], axis=1).astype(BF16)
    wab = jnp.zeros((D, LANE), F32).at[:, :N_AB].set(w[:, _C_A:_C_Q])
    wab1 = wab.astype(BF16)
    wab0 = (wab - wab1.astype(F32)).astype(BF16)
    return main, wab1, wab0


def kernel(x, c, ctx, c_ctx, w_mod, b_mod, w_in, w_out, conv_w, dn_conv_w, dn_a_log, dn_dt_bias, dn_norm_g,
           attn_sink, ffn_w_gate, ffn_w_up, ffn_w_down, moe_router, moe_w_gate, moe_w_up, moe_w_down,
           final_norm_g):
    bsz, n, d = x.shape
    ctx_len = ctx.shape[1]
    depth = w_in.shape[0]
    assert bsz == 1 and d == D and ctx_len == TM and n % TM == 0 and n % GRID_W == 0
    cos, sin = _rope_tables(n)
    mods = _mod_vectors(c, c_ctx, w_mod, b_mod)
    h = jnp.concatenate([ctx[0], x[0]], axis=0)
    for layer in range(depth):
        last = layer == depth - 1
        mod = mods[layer]
        w_main, wab1, wab0 = _prep_w_in(w_in[layer])
        pconv, pz, pq, pkv, qn, kn, vv, gb = _in_proj(h, mod, w_main, wab1, wab0, dn_conv_w[layer],
                                                      dn_a_log[layer], dn_dt_bias[layer])
        o_f, o_b = _delta_net(qn, kn, vv, gb, ctx_len)
        sink = jnp.zeros((8, LANE), F32).at[0, :ATT_HEADS].set(attn_sink[layer])
        yc = _attention(pq, pkv, cos, sin, sink, ctx_len, with_ctx=not last)
        router = None
        if layer % 2 == 1:
            wr = jnp.zeros((D, LANE), F32).at[:, :N_EXPERTS].set(moe_router[layer // 2])
            wr1 = wr.astype(BF16)
            router = (wr1, (wr - wr1.astype(F32)).astype(BF16))
        outs = _mixer_finish(h, mod, pconv, conv_w[layer], o_f, o_b, pz, dn_norm_g[layer], yc,
                             w_out[layer].astype(BF16), ctx_len, with_ctx=not last, router=router)
        if layer % 2 == 0:
            assert not last
            x1, hx = outs
            i = layer // 2
            h = _dense_ffn(hx, x1, mod, ffn_w_gate[i], ffn_w_up[i], ffn_w_down[i])
        else:
            assert last
            x1, hx, route = outs
            i = layer // 2
            dest, row_tok, blk_e, n_used = _moe_route(route[:, 2:2 + TOP_K].astype(jnp.int32))
            take = lambda rows_, idx: rows_.at[idx].get(mode="promise_in_bounds")
            xs = take(hx, row_tok)
            y = _moe_experts(xs, blk_e, n_used, moe_w_gate[i].astype(BF16), moe_w_up[i].astype(BF16),
                             moe_w_down[i].astype(BF16))
            y0 = take(y, dest[:, 0])
            y1 = take(y, dest[:, 1])
            h = _moe_combine_final(x1, y0, y1, route, mod, final_norm_g)
    return h.reshape(bsz, n, d)
```

```python
import functools

import jax
import jax.numpy as jnp
from jax import lax
from jax.experimental import pallas as pl
from jax.experimental.pallas import tpu as pltpu

F32 = jnp.float32
BF16 = jnp.bfloat16

D = 1024
N_MOD = 6
EPS = 1e-6
NEG = -1e30
GRID_W = 64

CONV_CH = 256
DN_HEADS = 6
DN_HEAD_DIM = 64
DN_DIM = DN_HEADS * DN_HEAD_DIM
DN_CHUNK = 64
DN_SUB = 16
DN_STEP_CHUNKS = 4
ATT_HEADS = 6
ATT_KV_HEADS = 2
ATT_GROUP = ATT_HEADS // ATT_KV_HEADS
HEAD_DIM = 64
ATT_DIM = ATT_HEADS * HEAD_DIM
ATT_KV_DIM = ATT_KV_HEADS * HEAD_DIM
ATT_BLOCK = 128
ROPE_BASE = 10000.0
AXIS_DIM = HEAD_DIM // 2
MIX_DIM = CONV_CH + DN_DIM + ATT_DIM

D_FF = 2816
N_EXPERTS = 8
TOP_K = 2
D_FF_EXPERT = 3584

TM = 256
FF_CHUNK = 256
MOE_TM = 512
MOE_TF = 1792
MOE_SUB = 256
LANE = 128
VMEM_LIMIT = 56 * 1024 * 1024

_C_QKV = 3 * CONV_CH
_C_Z = _C_QKV + 3 * DN_DIM
_C_A = _C_Z + DN_DIM
_C_Q = _C_A + 4 * DN_HEADS
_C_K = _C_Q + ATT_DIM
_C_V = _C_K + ATT_KV_DIM
_C_END = _C_V + ATT_KV_DIM
N_AB = 4 * DN_HEADS


def _params(sem=None, vmem=None):
    kw = {}
    if sem is not None:
        kw["dimension_semantics"] = sem
    if vmem is not None:
        kw["vmem_limit_bytes"] = vmem
    return pltpu.CompilerParams(**kw)


def _split2(a):
    hi = a.astype(BF16)
    lo = (a - hi.astype(F32)).astype(BF16)
    return hi, lo


def _split3(a):
    hi = a.astype(BF16)
    r = a - hi.astype(F32)
    mid = r.astype(BF16)
    lo = (r - mid.astype(F32)).astype(BF16)
    return hi, mid, lo


_NN = (((1,), (0,)), ((), ()))
_NT = (((1,), (1,)), ((), ()))
_TN = (((0,), (0,)), ((), ()))


def _mm(a, b, dims=_NN):
    return lax.dot_general(a.astype(BF16), b.astype(BF16), dims, preferred_element_type=F32)


def _mm3(a, b, dims=_NN):
    a1, a0 = _split2(a)
    b1, b0 = _split2(b)
    d = functools.partial(lax.dot_general, dimension_numbers=dims, preferred_element_type=F32)
    return d(a1, b1) + (d(a1, b0) + d(a0, b1))


_BNN = (((2,), (1,)), ((0,), (0,)))
_BNT = (((2,), (2,)), ((0,), (0,)))
_BTN = (((1,), (1,)), ((0,), (0,)))


def _bmm(a, b, dims=_BNN):
    return lax.dot_general(a.astype(BF16), b.astype(BF16), dims, preferred_element_type=F32)


def _sigmoid(x):
    return 1.0 / (1.0 + jnp.exp(-x))


def _softplus(x):
    return jnp.maximum(x, 0.0) + jnp.log1p(jnp.exp(-jnp.abs(x)))


def _mod_row(mod_ref, is_ctx):
    mod = mod_ref[...]
    return jnp.where(is_ctx, mod[1:2, :], mod[0:1, :])


def _rmsnorm_rows(x):
    return x * lax.rsqrt(jnp.mean(x * x, axis=-1, keepdims=True) + EPS)


def _shift_rows(u, prow, nrow):
    n = u.shape[0]
    rid = lax.broadcasted_iota(jnp.int32, u.shape, 0)
    up = jnp.where(rid == 0, prow, pltpu.roll(u, 1, 0))
    un = jnp.where(rid == n - 1, nrow, pltpu.roll(u, n - 1, 0))
    return up, un


def _same_group(shape, group):
    sh = group.bit_length() - 1
    assert 1 << sh == group
    return (lax.broadcasted_iota(jnp.int32, shape, 0) >> sh) == (lax.broadcasted_iota(jnp.int32, shape, 1) >> sh)


def _head_blockdiag(n, group):
    return jnp.where(_same_group((n, n), group), 1.0, 0.0).astype(BF16)


def _group_sum(t, bd):
    hi, lo = _split2(t)
    d = functools.partial(lax.dot_general, dimension_numbers=_NN, preferred_element_type=F32)
    return d(hi, bd) + d(lo, bd)


MOD_TN = 1536


def _mod_kernel(s_ref, w_ref, b_ref, o_ref):
    s = s_ref[...]
    s = s * _sigmoid(s)
    o_ref[0] = _mm3(s, w_ref[0]) + b_ref[0]


def _mod_vectors(c, c_ctx, w_mod, b_mod):
    depth = w_mod.shape[0]
    s = jnp.zeros((8, D), F32).at[0].set(c[0]).at[1].set(c_ctx)
    return pl.pallas_call(
        _mod_kernel,
        grid=(depth, N_MOD * D // MOD_TN),
        in_specs=[
            pl.BlockSpec((8, D), lambda l, j: (0, 0)),
            pl.BlockSpec((1, D, MOD_TN), lambda l, j: (l, 0, j)),
            pl.BlockSpec((1, 1, MOD_TN), lambda l, j: (l, 0, j)),
        ],
        out_specs=pl.BlockSpec((1, 8, MOD_TN), lambda l, j: (l, 0, j)),
        out_shape=jax.ShapeDtypeStruct((depth, 8, N_MOD * D), F32),
        compiler_params=_params(("arbitrary", "arbitrary"), VMEM_LIMIT),
        name="mod_vectors",
    )(s, w_mod, b_mod.reshape(depth, 1, N_MOD * D))


def _halo_valid(i, nblk):
    return jnp.logical_and(i != 0, i != 1), jnp.logical_and(i != 0, i != nblk - 1)


def _halo_rows(prev_ref, next_ref, i, nblk):
    pvalid, nvalid = _halo_valid(i, nblk)
    prow = jnp.where(pvalid, prev_ref[7:8, :], 0.0)
    nrow = jnp.where(nvalid, next_ref[0:1, :], 0.0)
    return prow, nrow


def _in_kernel(h_ref, hprev_ref, hnext_ref, mod_ref, w_ref, wab1_ref, wab0_ref, cw_ref, alog_ref, dtb_ref,
               pconv_ref, pz_ref, pq_ref, pkv_ref, q_ref, k_ref, v_ref, gb_ref, *, nblk):
    i = pl.program_id(0)
    row = _mod_row(mod_ref, i == 0)
    norm_mod = lambda x: _rmsnorm_rows(x) * (1.0 + row[:, D:2 * D]) + row[:, 0:D]
    hm = norm_mod(h_ref[...])
    h1, h0 = _split2(hm)
    halo = norm_mod(jnp.concatenate([hprev_ref[...], hnext_ref[...]], axis=0)).astype(BF16)
    d = functools.partial(lax.dot_general, dimension_numbers=_NN, preferred_element_type=F32)
    tm = h1.shape[0]
    c0 = 3 * CONV_CH
    c1 = c0 + 3 * DN_DIM
    c2 = c1 + DN_DIM
    c3 = c2 + ATT_DIM
    pconv_ref[...] = d(h1, w_ref[:, 0:c0])
    pz_ref[...] = d(h1, w_ref[:, c1:c2])
    pq_ref[...] = d(h1, w_ref[:, c2:c3])
    pkv_ref[...] = d(h1, w_ref[:, c3:c3 + 2 * ATT_KV_DIM])
    qkv = d(jnp.concatenate([h1, halo], axis=0), w_ref[:, c0:c1])
    ab = d(h1, wab1_ref[...]) + (d(h0, wab1_ref[...]) + d(h1, wab0_ref[...]))

    u = qkv[0:tm, :]
    pvalid, nvalid = _halo_valid(i, nblk)
    prow = jnp.where(pvalid, qkv[tm + 7:tm + 8, :], 0.0)
    nrow = jnp.where(nvalid, qkv[tm + 8:tm + 9, :], 0.0)
    up, un = _shift_rows(u, prow, nrow)
    cw = cw_ref[...]
    y = up * cw[0:1, :] + u * cw[1:2, :] + un * cw[2:3, :]
    y = y * _sigmoid(y)
    q = y[:, 0:DN_DIM]
    k = y[:, DN_DIM:2 * DN_DIM]
    bd = _head_blockdiag(DN_DIM, DN_HEAD_DIM)
    q_ref[...] = q * lax.rsqrt(_group_sum(q * q, bd) + 1e-6) * (DN_HEAD_DIM ** -0.5)
    k_ref[...] = k * lax.rsqrt(_group_sum(k * k, bd) + 1e-6)
    v_ref[...] = y[:, 2 * DN_DIM:3 * DN_DIM]
    g = -jnp.exp(alog_ref[...]) * _softplus(ab + dtb_ref[...])
    lane = lax.broadcasted_iota(jnp.int32, ab.shape, 1)
    gb_ref[...] = jnp.where(lane < 2 * DN_HEADS, g, _sigmoid(ab))


def _in_proj(h, mod, w_main, wab1, wab0, dn_conv_w, a_log, dt_bias):
    t = h.shape[0]
    nblk = t // TM
    r8 = TM // 8
    alog = jnp.zeros((1, LANE), F32).at[0, :2 * DN_HEADS].set(a_log.reshape(-1))
    dtb = jnp.zeros((1, LANE), F32).at[0, :2 * DN_HEADS].set(dt_bias.reshape(-1))
    widths = (3 * CONV_CH, DN_DIM, ATT_DIM, 2 * ATT_KV_DIM, DN_DIM, DN_DIM, DN_DIM, LANE)
    const = lambda i: (0, 0)
    return pl.pallas_call(
        functools.partial(_in_kernel, nblk=nblk),
        grid=(nblk,),
        in_specs=[
            pl.BlockSpec((TM, D), lambda i: (i, 0)),
            pl.BlockSpec((8, D), lambda i: (jnp.maximum(i * r8 - 1, 0), 0)),
            pl.BlockSpec((8, D), lambda i: (jnp.minimum((i + 1) * r8, t // 8 - 1), 0)),
            pl.BlockSpec((8, N_MOD * D), const),
            pl.BlockSpec(w_main.shape, const),
            pl.BlockSpec(wab1.shape, const),
            pl.BlockSpec(wab0.shape, const),
            pl.BlockSpec((3, 3 * DN_DIM), const),
            pl.BlockSpec((1, LANE), const),
            pl.BlockSpec((1, LANE), const),
        ],
        out_specs=[pl.BlockSpec((TM, w), lambda i: (i, 0)) for w in widths],
        out_shape=[jax.ShapeDtypeStruct((t, w), F32) for w in widths],
        compiler_params=_params(("arbitrary",), VMEM_LIMIT),
        name="in_proj",
    )(h, h, h, mod, w_main, wab1, wab0, dn_conv_w, alog, dtb)


def _dn_chunk(rev, q_ref, k_ref, v_ref, gb_ref, o_ref, s_ref):
    c_ = DN_CHUNK
    ri = lax.broadcasted_iota(jnp.int32, (c_, c_), 0)
    ci = lax.broadcasted_iota(jnp.int32, (c_, c_), 1)
    incl = (ri <= ci) if rev else (ri >= ci)
    strict = (ri < ci) if rev else (ri > ci)
    same_sub = _same_group((c_, c_), DN_SUB)
    eye = jnp.where(ri == ci, 1.0, 0.0)
    tri = jnp.where(incl, 1.0, 0.0).astype(BF16)
    last = 0 if rev else c_ - 1

    nchunks = q_ref.shape[0] // c_
    nh = DN_HEADS
    col0 = nh if rev else 0
    rows = lambda g: slice(g * c_, (g + 1) * c_)
    lanes = lambda h: slice(h * DN_HEAD_DIM, (h + 1) * DN_HEAD_DIM)

    def stack(fn):
        return jnp.stack([fn(g, h) for g in range(nchunks) for h in range(nh)])

    gb = gb_ref[...]
    gcs = [_cumsum_rows(tri, gb[rows(g), :]) for g in range(nchunks)]
    gcts = [gc.T for gc in gcs]
    q = stack(lambda g, h: q_ref[rows(g), lanes(h)])
    k = stack(lambda g, h: k_ref[rows(g), lanes(h)])
    v = stack(lambda g, h: v_ref[rows(g), lanes(h)])
    gcol = stack(lambda g, h: gcs[g][:, col0 + h:col0 + h + 1])
    grow = stack(lambda g, h: gcts[g][col0 + h:col0 + h + 1, :])
    beta = stack(lambda g, h: gb[rows(g), 2 * nh + col0 + h:2 * nh + col0 + h + 1])
    glast = gcol[:, last:last + 1, :]
    decay = jnp.where(incl, jnp.exp(jnp.where(incl, gcol - grow, 0.0)), 0.0)
    eg = jnp.exp(gcol)
    kb = k * beta
    a = jnp.where(strict, _bmm(kb, k, _BNT) * decay, 0.0)
    qk = jnp.where(incl, _bmm(q, k, _BNT) * decay, 0.0)
    ad = jnp.where(same_sub, a, 0.0)
    ao = a - ad
    p = eye - ad
    n2 = _bmm(ad, ad)
    p = p + _bmm(p, n2)
    n4 = _bmm(n2, n2)
    p = p + _bmm(p, n4)
    n8 = _bmm(n4, n4)
    dinv = p + _bmm(p, n8)
    m = _bmm(dinv, ao)
    m2 = _bmm(m, m)
    y = _bmm(dinv, jnp.concatenate([v * beta, kb * eg], axis=-1))
    z = y + _bmm(m2, y)
    x = z - _bmm(m, z)
    u = x[:, :, :DN_HEAD_DIM]
    w = x[:, :, DN_HEAD_DIM:]
    qg = q * eg
    kd = k * jnp.exp(glast - gcol)
    gl = jnp.exp(glast)
    s = s_ref[col0:col0 + nh]
    for g in (reversed(range(nchunks)) if rev else range(nchunks)):
        b = slice(g * nh, (g + 1) * nh)
        v_new = u[b] - _bmm(w[b], s)
        o = _bmm(qg[b], s) + _bmm(qk[b], v_new)
        s = s * gl[b] + _bmm(kd[b], v_new, _BTN)
        for h in range(nh):
            o_ref[rows(g), lanes(h)] = o[h]
    s_ref[col0:col0 + nh] = s


def _cumsum_rows(tri_bf16, g):
    g2, g1, g0 = _split3(g)
    d = functools.partial(lax.dot_general, dimension_numbers=_NN, preferred_element_type=F32)
    return d(tri_bf16, g2) + (d(tri_bf16, g1) + d(tri_bf16, g0))


def _dn_kernel(qf, kf, vf, gf, qb, kb, vb, gbb, *rest, n_cast):
    cast_in = rest[:n_cast]
    of_ref, ob_ref = rest[n_cast:n_cast + 2]
    cast_out = rest[n_cast + 2:2 * n_cast + 2]
    s_ref = rest[-1]

    @pl.when(pl.program_id(0) == 0)
    def _():
        s_ref[...] = jnp.zeros(s_ref.shape, F32)

    _dn_chunk(False, qf, kf, vf, gf, of_ref, s_ref)
    _dn_chunk(True, qb, kb, vb, gbb, ob_ref, s_ref)
    for src, dst in zip(cast_in, cast_out):
        dst[...] = src[...].astype(BF16)


def _delta_net(q, k, v, gb, ctx_len, to_bf16=()):
    t = q.shape[0]
    rows = DN_STEP_CHUNKS * DN_CHUNK
    assert ctx_len == rows and t % rows == 0
    nstep = t // rows
    cast_specs = []
    for m in to_bf16:
        rb = -(-m.shape[0] // nstep)
        rb = -(-rb // 16) * 16
        nb = -(-m.shape[0] // rb)
        cast_specs.append(pl.BlockSpec((rb, m.shape[1]), lambda s, nb=nb: (jnp.minimum(s, nb - 1), 0)))

    def fwd(s):
        return (s, 0)

    def bwd(s):
        return (jnp.where(s == 0, 0, nstep - s), 0)

    wide = lambda im: pl.BlockSpec((rows, DN_DIM), im)
    narrow = lambda im: pl.BlockSpec((rows, LANE), im)
    return pl.pallas_call(
        functools.partial(_dn_kernel, n_cast=len(to_bf16)),
        grid=(nstep,),
        in_specs=[wide(fwd), wide(fwd), wide(fwd), narrow(fwd), wide(bwd), wide(bwd), wide(bwd), narrow(bwd)]
        + cast_specs,
        out_specs=[wide(fwd), wide(bwd)] + cast_specs,
        out_shape=[jax.ShapeDtypeStruct((t, DN_DIM), F32)] * 2
        + [jax.ShapeDtypeStruct(m.shape, BF16) for m in to_bf16],
        scratch_shapes=[pltpu.VMEM((2 * DN_HEADS, DN_HEAD_DIM, DN_HEAD_DIM), F32)],
        compiler_params=_params(("arbitrary",), VMEM_LIMIT),
        name="delta_net",
    )(q, k, v, gb, q, k, v, gb, *to_bf16)


def _rope(x, cos, sin):
    w = x.shape[1]
    lane = lax.broadcasted_iota(jnp.int32, x.shape, 1)
    first_half = (lane & (AXIS_DIM - 1)) < (AXIS_DIM // 2)
    swapped = jnp.where(first_half, pltpu.roll(x, w - AXIS_DIM // 2, 1), pltpu.roll(x, AXIS_DIM // 2, 1))
    return x * cos + swapped * sin


def _softmax_av(s, sink, vals):
    m = jnp.maximum(jnp.max(s, axis=-1, keepdims=True), sink)
    p = jnp.exp(s - m)
    denom = jnp.sum(p, axis=-1, keepdims=True) + jnp.exp(sink - m)
    return _mm(p, vals) / denom


def _attend(q, keys, vals, valid, sink_all, o_ref):
    b = q.shape[0]
    for kvh in range(ATT_KV_HEADS):
        kl = slice(kvh * HEAD_DIM, (kvh + 1) * HEAD_DIM)
        heads = range(kvh * ATT_GROUP, (kvh + 1) * ATT_GROUP)
        qs = jnp.concatenate([q[:, h * HEAD_DIM:(h + 1) * HEAD_DIM] for h in heads], axis=0)
        sink = jnp.concatenate([jnp.broadcast_to(sink_all[0:1, h:h + 1], (b, 1)) for h in heads], axis=0)
        s = _mm(qs, keys[:, kl], _NT)
        if valid is not None:
            s = jnp.where(valid, s, NEG)
        o = _softmax_av(s, sink, vals[:, kl])
        for g, h in enumerate(heads):
            o_ref[:, h * HEAD_DIM:(h + 1) * HEAD_DIM] = o[g * b:(g + 1) * b, :]


def _attn_kernel(q_ref, kp_ref, kc_ref, kn_ref, kctx_ref, cq_ref, sq_ref, cp_ref, sp_ref, cn_ref, sn_ref,
                 sink_ref, o_ref, *, nb, off, with_ctx):
    j = pl.program_id(0)
    b = ATT_BLOCK
    scale = HEAD_DIM ** -0.5

    @pl.when(j < off)
    def _():
        if with_ctx:
            kvx = kctx_ref[...]
            _attend(q_ref[...] * scale, kvx[:, :ATT_KV_DIM], kvx[:, ATT_KV_DIM:], None, sink_ref[...], o_ref)
        else:
            o_ref[...] = jnp.zeros(o_ref.shape, F32)

    @pl.when(j >= off)
    def _():
        i = j - off
        cq = cq_ref[...]
        sq = sq_ref[...]
        q = q_ref[...]
        q = jnp.concatenate([_rope(q[:, l * LANE:(l + 1) * LANE], cq, sq) for l in range(ATT_DIM // LANE)], axis=1)
        kvp, kvc, kvn, kvx = kp_ref[...], kc_ref[...], kn_ref[...], kctx_ref[...]
        kp = _rope(kvp[:, :ATT_KV_DIM], cp_ref[...], sp_ref[...])
        kc = _rope(kvc[:, :ATT_KV_DIM], cq, sq)
        kn = _rope(kvn[:, :ATT_KV_DIM], cn_ref[...], sn_ref[...])
        keys = jnp.concatenate([kp, kc, kn, kvx[:, :ATT_KV_DIM]], axis=0)
        vals = jnp.concatenate(
            [kvp[:, ATT_KV_DIM:], kvc[:, ATT_KV_DIM:], kvn[:, ATT_KV_DIM:], kvx[:, ATT_KV_DIM:]], axis=0)
        nk = keys.shape[0]
        c = lax.broadcasted_iota(jnp.int32, (1, nk), 1)
        far = 4 * nk
        ccol = jnp.where(c >= 3 * b, b + ((c - 3 * b) & (b - 1)), c)
        ccol = jnp.where(jnp.logical_and(c < b, i == 0), -far, ccol)
        ccol = jnp.where(jnp.logical_and(jnp.logical_and(c >= 2 * b, c < 3 * b), i == nb - 1), far, ccol)
        r = lax.broadcasted_iota(jnp.int32, (ATT_GROUP * b, 1), 0) & (b - 1)
        valid = lax.bitcast_convert_type(ccol - r, jnp.uint32) <= jnp.uint32(2 * b)
        _attend(q * scale, keys, vals, valid, sink_ref[...], o_ref)


def _attention(pq, pkv, cos, sin, sink, ctx_len, with_ctx):
    t = pq.shape[0]
    n = t - ctx_len
    nb = n // ATT_BLOCK
    off = ctx_len // ATT_BLOCK
    kv = lambda im: pl.BlockSpec((ATT_BLOCK, 2 * ATT_KV_DIM), im)
    tab = lambda im: pl.BlockSpec((ATT_BLOCK, LANE), im)
    lat = lambda j: jnp.maximum(j - off, 0)
    cur = lambda j: (j, 0)
    prv = lambda j: (jnp.maximum(lat(j) - 1, 0) + off, 0)
    nxt = lambda j: (jnp.minimum(lat(j) + 1, nb - 1) + off, 0)
    tcur = lambda j: (lat(j), 0)
    tprv = lambda j: (jnp.maximum(lat(j) - 1, 0), 0)
    tnxt = lambda j: (jnp.minimum(lat(j) + 1, nb - 1), 0)
    return pl.pallas_call(
        functools.partial(_attn_kernel, nb=nb, off=off, with_ctx=with_ctx),
        grid=(off + nb,),
        in_specs=[
            pl.BlockSpec((ATT_BLOCK, ATT_DIM), cur),
            kv(prv), kv(cur), kv(nxt),
            pl.BlockSpec((ctx_len, 2 * ATT_KV_DIM), lambda j: (0, 0)),
            tab(tcur), tab(tcur), tab(tprv), tab(tprv), tab(tnxt), tab(tnxt),
            pl.BlockSpec((8, LANE), lambda j: (0, 0)),
        ],
        out_specs=pl.BlockSpec((ATT_BLOCK, ATT_DIM), cur),
        out_shape=jax.ShapeDtypeStruct((t, ATT_DIM), F32),
        compiler_params=_params(("arbitrary",), VMEM_LIMIT),
        name="attention",
    )(pq, pkv, pkv, pkv, pkv, cos, sin, cos, sin, cos, sin, sink)


def _mixfin_kernel(h_ref, mod_ref, pconv_ref, prev_ref, next_ref, cw_ref, of_ref, ob_ref, z_ref, ng_ref, yc_ref,
                   wout_ref, *rest, nblk, blk0, with_router):
    if with_router:
        wr1_ref, wr0_ref, x_ref, hx_ref, lg_ref = rest
    else:
        x_ref, hx_ref = rest
    i = pl.program_id(0) + blk0
    row = _mod_row(mod_ref, i == 0)
    pc = pconv_ref[...]
    u = pc[:, CONV_CH:2 * CONV_CH] * pc[:, 2 * CONV_CH:]
    prow, nrow = _halo_rows(prev_ref, next_ref, i, nblk)
    prow = prow[:, CONV_CH:2 * CONV_CH] * prow[:, 2 * CONV_CH:]
    nrow = nrow[:, CONV_CH:2 * CONV_CH] * nrow[:, 2 * CONV_CH:]
    up, un = _shift_rows(u, prow, nrow)
    cw = cw_ref[...]
    ya = pc[:, :CONV_CH] * (up * cw[0:1, :] + u * cw[1:2, :] + un * cw[2:3, :])
    o = of_ref[...] + ob_ref[...]
    ms = _group_sum(o * o, _head_blockdiag(DN_DIM, DN_HEAD_DIM)) * (1.0 / DN_HEAD_DIM)
    z = z_ref[...]
    yb = o * lax.rsqrt(ms + EPS) * ng_ref[...] * (z * _sigmoid(z))
    mix = jnp.concatenate([ya, yb, yc_ref[...]], axis=1)
    x = h_ref[...] + row[:, 2 * D:3 * D] * _mm(mix, wout_ref[...])
    x_ref[...] = x
    hx = _rmsnorm_rows(x) * (1.0 + row[:, 4 * D:5 * D]) + row[:, 3 * D:4 * D]
    hx_ref[...] = hx.astype(hx_ref.dtype)
    if with_router:
        h1, h0 = _split2(hx)
        d = functools.partial(lax.dot_general, dimension_numbers=_NN, preferred_element_type=F32)
        lg = d(h1, wr1_ref[...]) + (d(h0, wr1_ref[...]) + d(h1, wr0_ref[...]))
        lane = lax.broadcasted_iota(jnp.int32, lg.shape, 1)
        lanef = lane.astype(F32)
        lg = jnp.where(lane < N_EXPERTS, lg, -jnp.inf)
        m1 = jnp.max(lg, axis=-1, keepdims=True)
        i1 = jnp.min(jnp.where(lg == m1, lanef, float(LANE)), axis=-1, keepdims=True)
        rest = jnp.where(lanef == i1, -jnp.inf, lg)
        m2 = jnp.max(rest, axis=-1, keepdims=True)
        i2 = jnp.min(jnp.where(rest == m2, lanef, float(LANE)), axis=-1, keepdims=True)
        e2 = jnp.exp(m2 - m1)
        g1 = 1.0 / (1.0 + e2)
        lg_ref[...] = jnp.where(lane == 0, g1, jnp.where(lane == 1, e2 * g1, jnp.where(lane == 2, i1, i2)))


def _mixer_finish(h, mod, pconv, conv_w, o_f, o_b, pz, norm_g, yc, w_out, ctx_len, with_ctx, router=None):
    t = h.shape[0]
    nblk = t // TM
    blk0 = 0 if with_ctx else ctx_len // TM
    rows = t - blk0 * TM
    r8 = TM // 8
    w = pconv.shape[1]
    cur = lambda i: (i + blk0, 0)
    out_cur = lambda i: (i, 0)
    const = lambda i: (0, 0)
    ng = jnp.tile(norm_g.reshape(1, DN_HEAD_DIM), (1, DN_HEADS))
    in_specs = [
        pl.BlockSpec((TM, D), cur),
        pl.BlockSpec((8, N_MOD * D), const),
        pl.BlockSpec((TM, w), cur),
        pl.BlockSpec((8, w), lambda i: (jnp.maximum((i + blk0) * r8 - 1, 0), 0)),
        pl.BlockSpec((8, w), lambda i: (jnp.minimum((i + blk0 + 1) * r8, t // 8 - 1), 0)),
        pl.BlockSpec((3, CONV_CH), const),
        pl.BlockSpec((TM, DN_DIM), cur),
        pl.BlockSpec((TM, DN_DIM), cur),
        pl.BlockSpec((TM, DN_DIM), cur),
        pl.BlockSpec((1, DN_DIM), const),
        pl.BlockSpec((TM, ATT_DIM), cur),
        pl.BlockSpec((MIX_DIM, D), const),
    ]
    args = [h, mod, pconv, pconv, pconv, conv_w, o_f, o_b, pz, ng, yc, w_out]
    out_specs = [pl.BlockSpec((TM, D), out_cur), pl.BlockSpec((TM, D), out_cur)]
    hx_dtype = BF16 if router is None else F32
    out_shape = [jax.ShapeDtypeStruct((rows, D), F32), jax.ShapeDtypeStruct((rows, D), hx_dtype)]
    if router is not None:
        in_specs += [pl.BlockSpec((D, LANE), const)] * 2
        args += list(router)
        out_specs.append(pl.BlockSpec((TM, LANE), out_cur))
        out_shape.append(jax.ShapeDtypeStruct((rows, LANE), F32))
    return pl.pallas_call(
        functools.partial(_mixfin_kernel, nblk=nblk, blk0=blk0, with_router=router is not None),
        grid=(rows // TM,),
        in_specs=in_specs,
        out_specs=out_specs,
        out_shape=out_shape,
        compiler_params=_params(("arbitrary",), VMEM_LIMIT),
        name="mixer_finish",
    )(*args)


def _ffn_kernel(hx_ref, x_ref, mod_ref, wg_ref, wu_ref, wd_ref, o_ref):
    row = _mod_row(mod_ref, pl.program_id(0) == 0)
    hx = hx_ref[...]
    acc = jnp.zeros((hx.shape[0], D), F32)
    for f in range(0, D_FF, FF_CHUNK):
        g = _mm(hx, wg_ref[:, f:f + FF_CHUNK])
        u = _mm(hx, wu_ref[:, f:f + FF_CHUNK])
        acc = acc + _mm(g * _sigmoid(g) * u, wd_ref[f:f + FF_CHUNK, :])
    o_ref[...] = x_ref[...] + row[:, 5 * D:6 * D] * acc


def _dense_ffn(hx, x, mod, wg, wu, wd):
    t = x.shape[0]
    const = lambda i: (0, 0)
    once = dict(pipeline_mode=pl.Buffered(1))
    return pl.pallas_call(
        _ffn_kernel,
        grid=(t // TM,),
        in_specs=[
            pl.BlockSpec((TM, D), lambda i: (i, 0)),
            pl.BlockSpec((TM, D), lambda i: (i, 0)),
            pl.BlockSpec((8, N_MOD * D), const),
            pl.BlockSpec((D, D_FF), const, **once),
            pl.BlockSpec((D, D_FF), const, **once),
            pl.BlockSpec((D_FF, D), const, **once),
        ],
        out_specs=pl.BlockSpec((TM, D), lambda i: (i, 0)),
        out_shape=jax.ShapeDtypeStruct((t, D), F32),
        compiler_params=_params(("arbitrary",), VMEM_LIMIT),
        name="dense_ffn",
    )(hx, x, mod, wg, wu, wd)


def _moe_kernel(be_ref, nu_ref, xs_ref, wg_ref, wu_ref, wd_ref, y_ref, acc_ref):
    b = pl.program_id(0)
    f = pl.program_id(1)

    @pl.when(b < nu_ref[0])
    def _():
        xs = xs_ref[...].astype(BF16)
        part = jnp.zeros((MOE_TM, D), F32)
        for c in range(0, MOE_TF, MOE_SUB):
            g = _mm(xs, wg_ref[0, :, c:c + MOE_SUB])
            u = _mm(xs, wu_ref[0, :, c:c + MOE_SUB])
            part = part + _mm(g * _sigmoid(g) * u, wd_ref[0, c:c + MOE_SUB, :])

        @pl.when(f == 0)
        def _():
            acc_ref[...] = part

        @pl.when(f != 0)
        def _():
            acc_ref[...] += part

        @pl.when(f == pl.num_programs(1) - 1)
        def _():
            y_ref[...] = acc_ref[...]

    @pl.when(b >= nu_ref[0])
    def _():
        y_ref[...] = jnp.zeros(y_ref.shape, F32)


def _moe_experts(xs, blk_e, n_used, wg, wu, wd):
    cap = xs.shape[0]
    nblk = cap // MOE_TM
    nf = D_FF_EXPERT // MOE_TF

    def fidx(b, f, nu):
        return jnp.where(b < nu[0], f, nf - 1)

    grid_spec = pltpu.PrefetchScalarGridSpec(
        num_scalar_prefetch=2,
        grid=(nblk, nf),
        in_specs=[
            pl.BlockSpec((MOE_TM, D), lambda b, f, be, nu: (b, 0)),
            pl.BlockSpec((1, D, MOE_TF), lambda b, f, be, nu: (be[b], 0, fidx(b, f, nu))),
            pl.BlockSpec((1, D, MOE_TF), lambda b, f, be, nu: (be[b], 0, fidx(b, f, nu))),
            pl.BlockSpec((1, MOE_TF, D), lambda b, f, be, nu: (be[b], fidx(b, f, nu), 0)),
        ],
        out_specs=pl.BlockSpec((MOE_TM, D), lambda b, f, be, nu: (b, 0)),
        scratch_shapes=[pltpu.VMEM((MOE_TM, D), F32)],
    )
    return pl.pallas_call(
        _moe_kernel,
        grid_spec=grid_spec,
        out_shape=jax.ShapeDtypeStruct((cap, D), F32),
        compiler_params=_params(("arbitrary", "arbitrary"), VMEM_LIMIT),
        name="moe_experts",
    )(blk_e, n_used, xs, wg, wu, wd)


def _moe_route(top_e):
    n = top_e.shape[0]
    a = n * TOP_K
    flat_e = top_e.reshape(a)
    onehot = (flat_e[:, None] == jnp.arange(N_EXPERTS, dtype=flat_e.dtype)[None, :]).astype(jnp.int32)
    counts = jnp.sum(onehot, axis=0)
    padded = (counts + MOE_TM - 1) // MOE_TM * MOE_TM
    pad_ends = jnp.cumsum(padded)
    pad_starts = pad_ends - padded
    dest = jnp.sum(onehot * (jnp.cumsum(onehot, axis=0) - onehot + pad_starts[None, :]), axis=1)
    cap = a + N_EXPERTS * MOE_TM
    nblk = cap // MOE_TM
    row_tok = (jnp.arange(cap, dtype=jnp.int32) % n).at[dest].set(
        jnp.arange(a, dtype=jnp.int32) // TOP_K, unique_indices=True, mode="promise_in_bounds")
    blk_start = jnp.arange(nblk, dtype=jnp.int32) * MOE_TM
    blk_e = jnp.minimum(jnp.sum((pad_ends[None, :] <= blk_start[:, None]).astype(jnp.int32), axis=1),
                        N_EXPERTS - 1)
    n_used = (pad_ends[-1] // MOE_TM).astype(jnp.int32).reshape(1)
    last_e = blk_e[jnp.maximum(n_used[0] - 1, 0)]
    blk_e = jnp.where(jnp.arange(nblk) < n_used[0], blk_e, last_e)
    return dest.reshape(n, TOP_K), row_tok, blk_e, n_used


def _final_kernel(x_ref, y0_ref, y1_ref, gt_ref, mod_ref, fg_ref, o_ref):
    mod = mod_ref[...]
    gt = gt_ref[...]
    f = gt[:, 0:1] * y0_ref[...] + gt[:, 1:2] * y1_ref[...]
    x = x_ref[...] + mod[0:1, 5 * D:6 * D] * f
    o_ref[...] = _rmsnorm_rows(x) * fg_ref[...]


def _moe_combine_final(x, y0, y1, gt, mod, final_g):
    n = x.shape[0]
    row = lambda i: (i, 0)
    const = lambda i: (0, 0)
    return pl.pallas_call(
        _final_kernel,
        grid=(n // TM,),
        in_specs=[pl.BlockSpec((TM, D), row)] * 3
        + [pl.BlockSpec((TM, LANE), row), pl.BlockSpec((8, N_MOD * D), const), pl.BlockSpec((1, D), const)],
        out_specs=pl.BlockSpec((TM, D), row),
        out_shape=jax.ShapeDtypeStruct((n, D), F32),
        compiler_params=_params(("arbitrary",), VMEM_LIMIT),
        name="moe_combine_final",
    )(x, y0, y1, gt, mod, final_g.reshape(1, D))


def _rope_tables(n):
    pos = jnp.arange(n, dtype=jnp.int32)
    r = (pos // GRID_W).astype(F32)
    col = (pos % GRID_W).astype(F32)
    inv = ROPE_BASE ** (-jnp.arange(0, AXIS_DIM, 2, dtype=F32) / AXIS_DIM)
    ang = jnp.concatenate([r[:, None] * inv, r[:, None] * inv, col[:, None] * inv, col[:, None] * inv], axis=1)
    sign = jnp.tile(jnp.concatenate([-jnp.ones((AXIS_DIM // 2,), F32), jnp.ones((AXIS_DIM // 2,), F32)]), 2)
    cos = jnp.tile(jnp.cos(ang), (1, LANE // HEAD_DIM))
    sin = jnp.tile(jnp.sin(ang) * sign, (1, LANE // HEAD_DIM))
    return cos, sin


def _prep_w_in(w):
    main = jnp.concatenate([w[:, :_C_A], w[:, _C_Q:_C_END]], axis=1).astype(BF16)
    wab = jnp.zeros((D, LANE), F32).at[:, :N_AB].set(w[:, _C_A:_C_Q])
    wab1 = wab.astype(BF16)
    wab0 = (wab - wab1.astype(F32)).astype(BF16)
    return main, wab1, wab0


def kernel(x, c, ctx, c_ctx, w_mod, b_mod, w_in, w_out, conv_w, dn_conv_w, dn_a_log, dn_dt_bias, dn_norm_g,
           attn_sink, ffn_w_gate, ffn_w_up, ffn_w_down, moe_router, moe_w_gate, moe_w_up, moe_w_down,
           final_norm_g):
    bsz, n, d = x.shape
    ctx_len = ctx.shape[1]
    depth = w_in.shape[0]
    assert bsz == 1 and d == D and ctx_len == TM and n % TM == 0 and n % GRID_W == 0
    cos, sin = _rope_tables(n)
    mods = _mod_vectors(c, c_ctx, w_mod, b_mod)
    h = jnp.concatenate([ctx[0], x[0]], axis=0)
    for layer in range(depth):
        last = layer == depth - 1
        mod = mods[layer]
        w_main, wab1, wab0 = _prep_w_in(w_in[layer])
        pconv, pz, pq, pkv, qn, kn, vv, gb = _in_proj(h, mod, w_main, wab1, wab0, dn_conv_w[layer],
                                                      dn_a_log[layer], dn_dt_bias[layer])
        if (layer + 1) % 2 == 1 and layer + 1 < depth:
            j = (layer + 1) // 2
            to_bf16 = tuple(w[j].reshape(-1, w.shape[-1]) for w in (moe_w_gate, moe_w_up, moe_w_down))
        else:
            to_bf16 = ()
        o_f, o_b, *cast = _delta_net(qn, kn, vv, gb, ctx_len, to_bf16)
        if to_bf16:
            moe_bf16 = [c.reshape(w.shape[1:]) for c, w in zip(cast, (moe_w_gate, moe_w_up, moe_w_down))]
        sink = jnp.zeros((8, LANE), F32).at[0, :ATT_HEADS].set(attn_sink[layer])
        yc = _attention(pq, pkv, cos, sin, sink, ctx_len, with_ctx=not last)
        router = None
        if layer % 2 == 1:
            wr = jnp.zeros((D, LANE), F32).at[:, :N_EXPERTS].set(moe_router[layer // 2])
            wr1 = wr.astype(BF16)
            router = (wr1, (wr - wr1.astype(F32)).astype(BF16))
        outs = _mixer_finish(h, mod, pconv, conv_w[layer], o_f, o_b, pz, dn_norm_g[layer], yc,
                             w_out[layer].astype(BF16), ctx_len, with_ctx=not last, router=router)
        if layer % 2 == 0:
            assert not last
            x1, hx = outs
            i = layer // 2
            h = _dense_ffn(hx, x1, mod, ffn_w_gate[i], ffn_w_up[i], ffn_w_down[i])
        else:
            assert last
            x1, hx, route = outs
            i = layer // 2
            dest, row_tok, blk_e, n_used = _moe_route(route[:, 2:2 + TOP_K].astype(jnp.int32))
            take = lambda rows_, idx: rows_.at[idx].get(mode="promise_in_bounds")
            xs = take(hx, row_tok)
            y = _moe_experts(xs, blk_e, n_used, *moe_bf16)
            y0 = take(y, dest[:, 0])
            y1 = take(y, dest[:, 1])
            h = _moe_combine_final(x1, y0, y1, route, mod, final_norm_g)
    return h.reshape(bsz, n, d)
```

```python
import functools

import jax
import jax.numpy as jnp
from jax import lax
from jax.experimental import pallas as pl
from jax.experimental.pallas import tpu as pltpu

F32 = jnp.float32
BF16 = jnp.bfloat16

D = 1024
N_MOD = 6
EPS = 1e-6
NEG = -1e30
GRID_W = 64

CONV_CH = 256
DN_HEADS = 6
DN_HEAD_DIM = 64
DN_DIM = DN_HEADS * DN_HEAD_DIM
DN_CHUNK = 64
DN_SUB = 16
DN_STEP_CHUNKS = 4
ATT_HEADS = 6
ATT_KV_HEADS = 2
ATT_GROUP = ATT_HEADS // ATT_KV_HEADS
HEAD_DIM = 64
ATT_DIM = ATT_HEADS * HEAD_DIM
ATT_KV_DIM = ATT_KV_HEADS * HEAD_DIM
ATT_BLOCK = 128
ROPE_BASE = 10000.0
AXIS_DIM = HEAD_DIM // 2
MIX_DIM = CONV_CH + DN_DIM + ATT_DIM

D_FF = 2816
N_EXPERTS = 8
TOP_K = 2
D_FF_EXPERT = 3584

TM = 256
FF_CHUNK = 256
MOE_TM = 512
MOE_TF = 1792
MOE_SUB = 256
LANE = 128
VMEM_LIMIT = 56 * 1024 * 1024

_C_QKV = 3 * CONV_CH
_C_Z = _C_QKV + 3 * DN_DIM
_C_A = _C_Z + DN_DIM
_C_Q = _C_A + 4 * DN_HEADS
_C_K = _C_Q + ATT_DIM
_C_V = _C_K + ATT_KV_DIM
_C_END = _C_V + ATT_KV_DIM
N_AB = 4 * DN_HEADS


def _params(sem=None, vmem=None):
    kw = {}
    if sem is not None:
        kw["dimension_semantics"] = sem
    if vmem is not None:
        kw["vmem_limit_bytes"] = vmem
    return pltpu.CompilerParams(**kw)


def _split2(a):
    hi = a.astype(BF16)
    lo = (a - hi.astype(F32)).astype(BF16)
    return hi, lo


def _split3(a):
    hi = a.astype(BF16)
    r = a - hi.astype(F32)
    mid = r.astype(BF16)
    lo = (r - mid.astype(F32)).astype(BF16)
    return hi, mid, lo


_NN = (((1,), (0,)), ((), ()))
_NT = (((1,), (1,)), ((), ()))
_TN = (((0,), (0,)), ((), ()))


def _mm(a, b, dims=_NN):
    return lax.dot_general(a.astype(BF16), b.astype(BF16), dims, preferred_element_type=F32)


def _mm3(a, b, dims=_NN):
    a1, a0 = _split2(a)
    b1, b0 = _split2(b)
    d = functools.partial(lax.dot_general, dimension_numbers=dims, preferred_element_type=F32)
    return d(a1, b1) + (d(a1, b0) + d(a0, b1))


_BNN = (((2,), (1,)), ((0,), (0,)))
_BNT = (((2,), (2,)), ((0,), (0,)))
_BTN = (((1,), (1,)), ((0,), (0,)))


def _bmm(a, b, dims=_BNN):
    return lax.dot_general(a.astype(BF16), b.astype(BF16), dims, preferred_element_type=F32)


def _sigmoid(x):
    return 1.0 / (1.0 + jnp.exp(-x))


def _softplus(x):
    return jnp.maximum(x, 0.0) + jnp.log1p(jnp.exp(-jnp.abs(x)))


def _mod_row(mod_ref, is_ctx):
    mod = mod_ref[...]
    return jnp.where(is_ctx, mod[1:2, :], mod[0:1, :])


def _rmsnorm_rows(x):
    return x * lax.rsqrt(jnp.mean(x * x, axis=-1, keepdims=True) + EPS)


def _shift_rows(u, prow, nrow):
    n = u.shape[0]
    rid = lax.broadcasted_iota(jnp.int32, u.shape, 0)
    up = jnp.where(rid == 0, prow, pltpu.roll(u, 1, 0))
    un = jnp.where(rid == n - 1, nrow, pltpu.roll(u, n - 1, 0))
    return up, un


def _same_group(shape, group):
    sh = group.bit_length() - 1
    assert 1 << sh == group
    return (lax.broadcasted_iota(jnp.int32, shape, 0) >> sh) == (lax.broadcasted_iota(jnp.int32, shape, 1) >> sh)


def _head_blockdiag(n, group):
    return jnp.where(_same_group((n, n), group), 1.0, 0.0).astype(BF16)


def _group_sum(t, bd):
    hi, lo = _split2(t)
    d = functools.partial(lax.dot_general, dimension_numbers=_NN, preferred_element_type=F32)
    return d(hi, bd) + d(lo, bd)


MOD_TN = 1536


def _mod_kernel(s_ref, w_ref, b_ref, o_ref):
    s = s_ref[...]
    s = s * _sigmoid(s)
    o_ref[0] = _mm3(s, w_ref[0]) + b_ref[0]


def _mod_vectors(c, c_ctx, w_mod, b_mod):
    depth = w_mod.shape[0]
    s = jnp.zeros((8, D), F32).at[0].set(c[0]).at[1].set(c_ctx)
    return pl.pallas_call(
        _mod_kernel,
        grid=(depth, N_MOD * D // MOD_TN),
        in_specs=[
            pl.BlockSpec((8, D), lambda l, j: (0, 0)),
            pl.BlockSpec((1, D, MOD_TN), lambda l, j: (l, 0, j)),
            pl.BlockSpec((1, 1, MOD_TN), lambda l, j: (l, 0, j)),
        ],
        out_specs=pl.BlockSpec((1, 8, MOD_TN), lambda l, j: (l, 0, j)),
        out_shape=jax.ShapeDtypeStruct((depth, 8, N_MOD * D), F32),
        compiler_params=_params(("arbitrary", "arbitrary"), VMEM_LIMIT),
        name="mod_vectors",
    )(s, w_mod, b_mod.reshape(depth, 1, N_MOD * D))


def _halo_valid(i, nblk):
    return jnp.logical_and(i != 0, i != 1), jnp.logical_and(i != 0, i != nblk - 1)


def _halo_rows(prev_ref, next_ref, i, nblk):
    pvalid, nvalid = _halo_valid(i, nblk)
    prow = jnp.where(pvalid, prev_ref[7:8, :], 0.0)
    nrow = jnp.where(nvalid, next_ref[0:1, :], 0.0)
    return prow, nrow


def _in_kernel(h_ref, hprev_ref, hnext_ref, mod_ref, w_ref, cw_ref, alog_ref, dtb_ref,
               pconv_ref, pz_ref, pq_ref, pkv_ref, q_ref, k_ref, v_ref, gb_ref, *, nblk):
    i = pl.program_id(0)
    row = _mod_row(mod_ref, i == 0)
    norm_mod = lambda x: _rmsnorm_rows(x) * (1.0 + row[:, D:2 * D]) + row[:, 0:D]
    h1 = norm_mod(h_ref[...]).astype(BF16)
    halo = norm_mod(jnp.concatenate([hprev_ref[...], hnext_ref[...]], axis=0)).astype(BF16)
    d = functools.partial(lax.dot_general, dimension_numbers=_NN, preferred_element_type=F32)
    tm = h1.shape[0]
    c0 = 3 * CONV_CH
    c1 = c0 + 3 * DN_DIM
    c2 = c1 + LANE
    c3 = c2 + DN_DIM + ATT_DIM
    pconv_ref[...] = d(h1, w_ref[:, 0:c0])
    zq = d(h1, w_ref[:, c2:c3])
    pz_ref[...] = zq[:, 0:DN_DIM]
    pq_ref[...] = zq[:, DN_DIM:]
    pkv_ref[...] = d(h1, w_ref[:, c3:c3 + 2 * ATT_KV_DIM])
    qkv_ab = d(jnp.concatenate([h1, halo], axis=0), w_ref[:, c0:c2])
    qkv = qkv_ab[:, 0:3 * DN_DIM]
    ab = qkv_ab[0:tm, 3 * DN_DIM:]

    u = qkv[0:tm, :]
    pvalid, nvalid = _halo_valid(i, nblk)
    prow = jnp.where(pvalid, qkv[tm + 7:tm + 8, :], 0.0)
    nrow = jnp.where(nvalid, qkv[tm + 8:tm + 9, :], 0.0)
    up, un = _shift_rows(u, prow, nrow)
    cw = cw_ref[...]
    y = up * cw[0:1, :] + u * cw[1:2, :] + un * cw[2:3, :]
    y = y * _sigmoid(y)
    q = y[:, 0:DN_DIM]
    k = y[:, DN_DIM:2 * DN_DIM]
    bd = _head_blockdiag(DN_DIM, DN_HEAD_DIM)
    q_ref[...] = q * lax.rsqrt(_group_sum(q * q, bd) + 1e-6) * (DN_HEAD_DIM ** -0.5)
    k_ref[...] = k * lax.rsqrt(_group_sum(k * k, bd) + 1e-6)
    v_ref[...] = y[:, 2 * DN_DIM:3 * DN_DIM]
    g = -jnp.exp(alog_ref[...]) * _softplus(ab + dtb_ref[...])
    lane = lax.broadcasted_iota(jnp.int32, ab.shape, 1)
    gb_ref[...] = jnp.where(lane < 2 * DN_HEADS, g, _sigmoid(ab))


def _in_proj(h, mod, w_main, dn_conv_w, a_log, dt_bias):
    t = h.shape[0]
    nblk = t // TM
    r8 = TM // 8
    alog = jnp.zeros((1, LANE), F32).at[0, :2 * DN_HEADS].set(a_log.reshape(-1))
    dtb = jnp.zeros((1, LANE), F32).at[0, :2 * DN_HEADS].set(dt_bias.reshape(-1))
    widths = (3 * CONV_CH, DN_DIM, ATT_DIM, 2 * ATT_KV_DIM, DN_DIM, DN_DIM, DN_DIM, LANE)
    const = lambda i: (0, 0)
    return pl.pallas_call(
        functools.partial(_in_kernel, nblk=nblk),
        grid=(nblk,),
        in_specs=[
            pl.BlockSpec((TM, D), lambda i: (i, 0)),
            pl.BlockSpec((8, D), lambda i: (jnp.maximum(i * r8 - 1, 0), 0)),
            pl.BlockSpec((8, D), lambda i: (jnp.minimum((i + 1) * r8, t // 8 - 1), 0)),
            pl.BlockSpec((8, N_MOD * D), const),
            pl.BlockSpec(w_main.shape, const),
            pl.BlockSpec((3, 3 * DN_DIM), const),
            pl.BlockSpec((1, LANE), const),
            pl.BlockSpec((1, LANE), const),
        ],
        out_specs=[pl.BlockSpec((TM, w), lambda i: (i, 0)) for w in widths],
        out_shape=[jax.ShapeDtypeStruct((t, w), F32) for w in widths],
        compiler_params=_params(("arbitrary",), VMEM_LIMIT),
        name="in_proj",
    )(h, h, h, mod, w_main, dn_conv_w, alog, dtb)


def _dn_chunk(rev, q_ref, k_ref, v_ref, gb_ref, o_ref, s_ref):
    c_ = DN_CHUNK
    ri = lax.broadcasted_iota(jnp.int32, (c_, c_), 0)
    ci = lax.broadcasted_iota(jnp.int32, (c_, c_), 1)
    incl = (ri <= ci) if rev else (ri >= ci)
    strict = (ri < ci) if rev else (ri > ci)
    same_sub = _same_group((c_, c_), DN_SUB)
    eye = jnp.where(ri == ci, 1.0, 0.0)
    tri = jnp.where(incl, 1.0, 0.0).astype(BF16)
    last = 0 if rev else c_ - 1

    nchunks = q_ref.shape[0] // c_
    nh = DN_HEADS
    col0 = nh if rev else 0
    rows = lambda g: slice(g * c_, (g + 1) * c_)
    lanes = lambda h: slice(h * DN_HEAD_DIM, (h + 1) * DN_HEAD_DIM)

    def stack(fn):
        return jnp.stack([fn(g, h) for g in range(nchunks) for h in range(nh)])

    gb = gb_ref[...]
    gcs = [_cumsum_rows(tri, gb[rows(g), :]) for g in range(nchunks)]
    gcts = [gc.T for gc in gcs]
    q = stack(lambda g, h: q_ref[rows(g), lanes(h)])
    k = stack(lambda g, h: k_ref[rows(g), lanes(h)])
    v = stack(lambda g, h: v_ref[rows(g), lanes(h)])
    gcol = stack(lambda g, h: gcs[g][:, col0 + h:col0 + h + 1])
    grow = stack(lambda g, h: gcts[g][col0 + h:col0 + h + 1, :])
    beta = stack(lambda g, h: gb[rows(g), 2 * nh + col0 + h:2 * nh + col0 + h + 1])
    glast = gcol[:, last:last + 1, :]
    decay = jnp.where(incl, jnp.exp(jnp.where(incl, gcol - grow, 0.0)), 0.0)
    eg = jnp.exp(gcol)
    kb = k * beta
    a = jnp.where(strict, _bmm(kb, k, _BNT) * decay, 0.0)
    qk = jnp.where(incl, _bmm(q, k, _BNT) * decay, 0.0)
    ad = jnp.where(same_sub, a, 0.0)
    ao = a - ad
    p = eye - ad
    n2 = _bmm(ad, ad)
    p = p + _bmm(p, n2)
    n4 = _bmm(n2, n2)
    p = p + _bmm(p, n4)
    n8 = _bmm(n4, n4)
    dinv = p + _bmm(p, n8)
    m = _bmm(dinv, ao)
    m2 = _bmm(m, m)
    y = _bmm(dinv, jnp.concatenate([v * beta, kb * eg], axis=-1))
    z = y + _bmm(m2, y)
    x = z - _bmm(m, z)
    u = x[:, :, :DN_HEAD_DIM]
    w = x[:, :, DN_HEAD_DIM:]
    qg = q * eg
    kd = k * jnp.exp(glast - gcol)
    gl = jnp.exp(glast)
    s = s_ref[col0:col0 + nh]
    for g in (reversed(range(nchunks)) if rev else range(nchunks)):
        b = slice(g * nh, (g + 1) * nh)
        v_new = u[b] - _bmm(w[b], s)
        o = _bmm(qg[b], s) + _bmm(qk[b], v_new)
        s = s * gl[b] + _bmm(kd[b], v_new, _BTN)
        for h in range(nh):
            o_ref[rows(g), lanes(h)] = o[h]
    s_ref[col0:col0 + nh] = s


def _cumsum_rows(tri_bf16, g):
    g2, g1, g0 = _split3(g)
    d = functools.partial(lax.dot_general, dimension_numbers=_NN, preferred_element_type=F32)
    return d(tri_bf16, g2) + (d(tri_bf16, g1) + d(tri_bf16, g0))


def _dn_kernel(qf, kf, vf, gf, qb, kb, vb, gbb, *rest, n_cast):
    cast_in = rest[:n_cast]
    of_ref, ob_ref = rest[n_cast:n_cast + 2]
    cast_out = rest[n_cast + 2:2 * n_cast + 2]
    s_ref = rest[-1]

    @pl.when(pl.program_id(0) == 0)
    def _():
        s_ref[...] = jnp.zeros(s_ref.shape, F32)

    _dn_chunk(False, qf, kf, vf, gf, of_ref, s_ref)
    _dn_chunk(True, qb, kb, vb, gbb, ob_ref, s_ref)
    for src, dst in zip(cast_in, cast_out):
        dst[...] = src[...].astype(BF16)


def _delta_net(q, k, v, gb, ctx_len, to_bf16=()):
    t = q.shape[0]
    rows = DN_STEP_CHUNKS * DN_CHUNK
    assert ctx_len == rows and t % rows == 0
    nstep = t // rows
    cast_specs = []
    for m in to_bf16:
        rb = -(-m.shape[0] // nstep)
        rb = -(-rb // 16) * 16
        nb = -(-m.shape[0] // rb)
        cast_specs.append(pl.BlockSpec((rb, m.shape[1]), lambda s, nb=nb: (jnp.minimum(s, nb - 1), 0)))

    def fwd(s):
        return (s, 0)

    def bwd(s):
        return (jnp.where(s == 0, 0, nstep - s), 0)

    wide = lambda im: pl.BlockSpec((rows, DN_DIM), im)
    narrow = lambda im: pl.BlockSpec((rows, LANE), im)
    return pl.pallas_call(
        functools.partial(_dn_kernel, n_cast=len(to_bf16)),
        grid=(nstep,),
        in_specs=[wide(fwd), wide(fwd), wide(fwd), narrow(fwd), wide(bwd), wide(bwd), wide(bwd), narrow(bwd)]
        + cast_specs,
        out_specs=[wide(fwd), wide(bwd)] + cast_specs,
        out_shape=[jax.ShapeDtypeStruct((t, DN_DIM), F32)] * 2
        + [jax.ShapeDtypeStruct(m.shape, BF16) for m in to_bf16],
        scratch_shapes=[pltpu.VMEM((2 * DN_HEADS, DN_HEAD_DIM, DN_HEAD_DIM), F32)],
        compiler_params=_params(("arbitrary",), VMEM_LIMIT),
        name="delta_net",
    )(q, k, v, gb, q, k, v, gb, *to_bf16)


def _rope(x, cos, sin):
    w = x.shape[1]
    lane = lax.broadcasted_iota(jnp.int32, x.shape, 1)
    first_half = (lane & (AXIS_DIM - 1)) < (AXIS_DIM // 2)
    swapped = jnp.where(first_half, pltpu.roll(x, w - AXIS_DIM // 2, 1), pltpu.roll(x, AXIS_DIM // 2, 1))
    return x * cos + swapped * sin


def _softmax_av(s, sink, vals):
    m = jnp.maximum(jnp.max(s, axis=-1, keepdims=True), sink)
    p = jnp.exp(s - m)
    denom = jnp.sum(p, axis=-1, keepdims=True) + jnp.exp(sink - m)
    return _mm(p, vals) / denom


def _attend(q, keys, vals, valid, sink_all, o_ref):
    b = q.shape[0]
    for kvh in range(ATT_KV_HEADS):
        kl = slice(kvh * HEAD_DIM, (kvh + 1) * HEAD_DIM)
        heads = range(kvh * ATT_GROUP, (kvh + 1) * ATT_GROUP)
        qs = jnp.concatenate([q[:, h * HEAD_DIM:(h + 1) * HEAD_DIM] for h in heads], axis=0)
        sink = jnp.concatenate([jnp.broadcast_to(sink_all[0:1, h:h + 1], (b, 1)) for h in heads], axis=0)
        s = _mm(qs, keys[:, kl], _NT)
        if valid is not None:
            s = jnp.where(valid, s, NEG)
        o = _softmax_av(s, sink, vals[:, kl])
        for g, h in enumerate(heads):
            o_ref[:, h * HEAD_DIM:(h + 1) * HEAD_DIM] = o[g * b:(g + 1) * b, :]


def _attn_kernel(q_ref, kp_ref, kc_ref, kn_ref, kctx_ref, cq_ref, sq_ref, cp_ref, sp_ref, cn_ref, sn_ref,
                 sink_ref, o_ref, *, nb, off, with_ctx):
    j = pl.program_id(0)
    b = ATT_BLOCK
    scale = HEAD_DIM ** -0.5

    @pl.when(j < off)
    def _():
        if with_ctx:
            kvx = kctx_ref[...]
            _attend(q_ref[...] * scale, kvx[:, :ATT_KV_DIM], kvx[:, ATT_KV_DIM:], None, sink_ref[...], o_ref)
        else:
            o_ref[...] = jnp.zeros(o_ref.shape, F32)

    @pl.when(j >= off)
    def _():
        i = j - off
        cq = cq_ref[...]
        sq = sq_ref[...]
        q = q_ref[...]
        q = jnp.concatenate([_rope(q[:, l * LANE:(l + 1) * LANE], cq, sq) for l in range(ATT_DIM // LANE)], axis=1)
        kvp, kvc, kvn, kvx = kp_ref[...], kc_ref[...], kn_ref[...], kctx_ref[...]
        kp = _rope(kvp[:, :ATT_KV_DIM], cp_ref[...], sp_ref[...])
        kc = _rope(kvc[:, :ATT_KV_DIM], cq, sq)
        kn = _rope(kvn[:, :ATT_KV_DIM], cn_ref[...], sn_ref[...])
        keys = jnp.concatenate([kp, kc, kn, kvx[:, :ATT_KV_DIM]], axis=0)
        vals = jnp.concatenate(
            [kvp[:, ATT_KV_DIM:], kvc[:, ATT_KV_DIM:], kvn[:, ATT_KV_DIM:], kvx[:, ATT_KV_DIM:]], axis=0)
        nk = keys.shape[0]
        c = lax.broadcasted_iota(jnp.int32, (1, nk), 1)
        far = 4 * nk
        ccol = jnp.where(c >= 3 * b, b + ((c - 3 * b) & (b - 1)), c)
        ccol = jnp.where(jnp.logical_and(c < b, i == 0), -far, ccol)
        ccol = jnp.where(jnp.logical_and(jnp.logical_and(c >= 2 * b, c < 3 * b), i == nb - 1), far, ccol)
        r = lax.broadcasted_iota(jnp.int32, (ATT_GROUP * b, 1), 0) & (b - 1)
        valid = lax.bitcast_convert_type(ccol - r, jnp.uint32) <= jnp.uint32(2 * b)
        _attend(q * scale, keys, vals, valid, sink_ref[...], o_ref)


def _attention(pq, pkv, cos, sin, sink, ctx_len, with_ctx):
    t = pq.shape[0]
    n = t - ctx_len
    nb = n // ATT_BLOCK
    off = ctx_len // ATT_BLOCK
    kv = lambda im: pl.BlockSpec((ATT_BLOCK, 2 * ATT_KV_DIM), im)
    tab = lambda im: pl.BlockSpec((ATT_BLOCK, LANE), im)
    lat = lambda j: jnp.maximum(j - off, 0)
    cur = lambda j: (j, 0)
    prv = lambda j: (jnp.maximum(lat(j) - 1, 0) + off, 0)
    nxt = lambda j: (jnp.minimum(lat(j) + 1, nb - 1) + off, 0)
    tcur = lambda j: (lat(j), 0)
    tprv = lambda j: (jnp.maximum(lat(j) - 1, 0), 0)
    tnxt = lambda j: (jnp.minimum(lat(j) + 1, nb - 1), 0)
    return pl.pallas_call(
        functools.partial(_attn_kernel, nb=nb, off=off, with_ctx=with_ctx),
        grid=(off + nb,),
        in_specs=[
            pl.BlockSpec((ATT_BLOCK, ATT_DIM), cur),
            kv(prv), kv(cur), kv(nxt),
            pl.BlockSpec((ctx_len, 2 * ATT_KV_DIM), lambda j: (0, 0)),
            tab(tcur), tab(tcur), tab(tprv), tab(tprv), tab(tnxt), tab(tnxt),
            pl.BlockSpec((8, LANE), lambda j: (0, 0)),
        ],
        out_specs=pl.BlockSpec((ATT_BLOCK, ATT_DIM), cur),
        out_shape=jax.ShapeDtypeStruct((t, ATT_DIM), F32),
        compiler_params=_params(("arbitrary",), VMEM_LIMIT),
        name="attention",
    )(pq, pkv, pkv, pkv, pkv, cos, sin, cos, sin, cos, sin, sink)


def _mixfin_kernel(h_ref, mod_ref, pconv_ref, prev_ref, next_ref, cw_ref, of_ref, ob_ref, z_ref, ng_ref, yc_ref,
                   wout_ref, *rest, nblk, blk0, with_router):
    if with_router:
        wr1_ref, wr0_ref, x_ref, hx_ref, lg_ref = rest
    else:
        x_ref, hx_ref = rest
    i = pl.program_id(0) + blk0
    row = _mod_row(mod_ref, i == 0)
    pc = pconv_ref[...]
    u = pc[:, CONV_CH:2 * CONV_CH] * pc[:, 2 * CONV_CH:]
    prow, nrow = _halo_rows(prev_ref, next_ref, i, nblk)
    prow = prow[:, CONV_CH:2 * CONV_CH] * prow[:, 2 * CONV_CH:]
    nrow = nrow[:, CONV_CH:2 * CONV_CH] * nrow[:, 2 * CONV_CH:]
    up, un = _shift_rows(u, prow, nrow)
    cw = cw_ref[...]
    ya = pc[:, :CONV_CH] * (up * cw[0:1, :] + u * cw[1:2, :] + un * cw[2:3, :])
    o = of_ref[...] + ob_ref[...]
    ms = _group_sum(o * o, _head_blockdiag(DN_DIM, DN_HEAD_DIM)) * (1.0 / DN_HEAD_DIM)
    z = z_ref[...]
    yb = o * lax.rsqrt(ms + EPS) * ng_ref[...] * (z * _sigmoid(z))
    mix = jnp.concatenate([ya, yb, yc_ref[...]], axis=1)
    x = h_ref[...] + row[:, 2 * D:3 * D] * _mm(mix, wout_ref[...])
    x_ref[...] = x
    hx = _rmsnorm_rows(x) * (1.0 + row[:, 4 * D:5 * D]) + row[:, 3 * D:4 * D]
    hx_ref[...] = hx.astype(hx_ref.dtype)
    if with_router:
        h1, h0 = _split2(hx)
        d = functools.partial(lax.dot_general, dimension_numbers=_NN, preferred_element_type=F32)
        lg = d(h1, wr1_ref[...]) + (d(h0, wr1_ref[...]) + d(h1, wr0_ref[...]))
        lane = lax.broadcasted_iota(jnp.int32, lg.shape, 1)
        lanef = lane.astype(F32)
        lg = jnp.where(lane < N_EXPERTS, lg, -jnp.inf)
        m1 = jnp.max(lg, axis=-1, keepdims=True)
        i1 = jnp.min(jnp.where(lg == m1, lanef, float(LANE)), axis=-1, keepdims=True)
        rest = jnp.where(lanef == i1, -jnp.inf, lg)
        m2 = jnp.max(rest, axis=-1, keepdims=True)
        i2 = jnp.min(jnp.where(rest == m2, lanef, float(LANE)), axis=-1, keepdims=True)
        e2 = jnp.exp(m2 - m1)
        g1 = 1.0 / (1.0 + e2)
        lg_ref[...] = jnp.where(lane == 0, g1, jnp.where(lane == 1, e2 * g1, jnp.where(lane == 2, i1, i2)))


def _mixer_finish(h, mod, pconv, conv_w, o_f, o_b, pz, norm_g, yc, w_out, ctx_len, with_ctx, router=None):
    t = h.shape[0]
    nblk = t // TM
    blk0 = 0 if with_ctx else ctx_len // TM
    rows = t - blk0 * TM
    r8 = TM // 8
    w = pconv.shape[1]
    cur = lambda i: (i + blk0, 0)
    out_cur = lambda i: (i, 0)
    const = lambda i: (0, 0)
    ng = jnp.tile(norm_g.reshape(1, DN_HEAD_DIM), (1, DN_HEADS))
    in_specs = [
        pl.BlockSpec((TM, D), cur),
        pl.BlockSpec((8, N_MOD * D), const),
        pl.BlockSpec((TM, w), cur),
        pl.BlockSpec((8, w), lambda i: (jnp.maximum((i + blk0) * r8 - 1, 0), 0)),
        pl.BlockSpec((8, w), lambda i: (jnp.minimum((i + blk0 + 1) * r8, t // 8 - 1), 0)),
        pl.BlockSpec((3, CONV_CH), const),
        pl.BlockSpec((TM, DN_DIM), cur),
        pl.BlockSpec((TM, DN_DIM), cur),
        pl.BlockSpec((TM, DN_DIM), cur),
        pl.BlockSpec((1, DN_DIM), const),
        pl.BlockSpec((TM, ATT_DIM), cur),
        pl.BlockSpec((MIX_DIM, D), const),
    ]
    args = [h, mod, pconv, pconv, pconv, conv_w, o_f, o_b, pz, ng, yc, w_out]
    out_specs = [pl.BlockSpec((TM, D), out_cur), pl.BlockSpec((TM, D), out_cur)]
    hx_dtype = BF16 if router is None else F32
    out_shape = [jax.ShapeDtypeStruct((rows, D), F32), jax.ShapeDtypeStruct((rows, D), hx_dtype)]
    if router is not None:
        in_specs += [pl.BlockSpec((D, LANE), const)] * 2
        args += list(router)
        out_specs.append(pl.BlockSpec((TM, LANE), out_cur))
        out_shape.append(jax.ShapeDtypeStruct((rows, LANE), F32))
    return pl.pallas_call(
        functools.partial(_mixfin_kernel, nblk=nblk, blk0=blk0, with_router=router is not None),
        grid=(rows // TM,),
        in_specs=in_specs,
        out_specs=out_specs,
        out_shape=out_shape,
        compiler_params=_params(("arbitrary",), VMEM_LIMIT),
        name="mixer_finish",
    )(*args)


def _ffn_kernel(hx_ref, x_ref, mod_ref, wg_ref, wu_ref, wd_ref, o_ref):
    row = _mod_row(mod_ref, pl.program_id(0) == 0)
    hx = hx_ref[...]
    acc = jnp.zeros((hx.shape[0], D), F32)
    for f in range(0, D_FF, FF_CHUNK):
        g = _mm(hx, wg_ref[:, f:f + FF_CHUNK])
        u = _mm(hx, wu_ref[:, f:f + FF_CHUNK])
        acc = acc + _mm(g * _sigmoid(g) * u, wd_ref[f:f + FF_CHUNK, :])
    o_ref[...] = x_ref[...] + row[:, 5 * D:6 * D] * acc


def _dense_ffn(hx, x, mod, wg, wu, wd):
    t = x.shape[0]
    const = lambda i: (0, 0)
    once = dict(pipeline_mode=pl.Buffered(1))
    return pl.pallas_call(
        _ffn_kernel,
        grid=(t // TM,),
        in_specs=[
            pl.BlockSpec((TM, D), lambda i: (i, 0)),
            pl.BlockSpec((TM, D), lambda i: (i, 0)),
            pl.BlockSpec((8, N_MOD * D), const),
            pl.BlockSpec((D, D_FF), const, **once),
            pl.BlockSpec((D, D_FF), const, **once),
            pl.BlockSpec((D_FF, D), const, **once),
        ],
        out_specs=pl.BlockSpec((TM, D), lambda i: (i, 0)),
        out_shape=jax.ShapeDtypeStruct((t, D), F32),
        compiler_params=_params(("arbitrary",), VMEM_LIMIT),
        name="dense_ffn",
    )(hx, x, mod, wg, wu, wd)


def _moe_kernel(be_ref, nu_ref, xs_ref, wg_ref, wu_ref, wd_ref, y_ref, acc_ref):
    b = pl.program_id(0)
    f = pl.program_id(1)

    @pl.when(b < nu_ref[0])
    def _():
        xs = xs_ref[...].astype(BF16)
        part = jnp.zeros((MOE_TM, D), F32)
        for c in range(0, MOE_TF, MOE_SUB):
            g = _mm(xs, wg_ref[0, :, c:c + MOE_SUB])
            u = _mm(xs, wu_ref[0, :, c:c + MOE_SUB])
            part = part + _mm(g * _sigmoid(g) * u, wd_ref[0, c:c + MOE_SUB, :])

        @pl.when(f == 0)
        def _():
            acc_ref[...] = part

        @pl.when(f != 0)
        def _():
            acc_ref[...] += part

        @pl.when(f == pl.num_programs(1) - 1)
        def _():
            y_ref[...] = acc_ref[...]

    @pl.when(b >= nu_ref[0])
    def _():
        y_ref[...] = jnp.zeros(y_ref.shape, F32)


def _moe_experts(xs, blk_e, n_used, wg, wu, wd):
    cap = xs.shape[0]
    nblk = cap // MOE_TM
    nf = D_FF_EXPERT // MOE_TF

    def fidx(b, f, nu):
        return jnp.where(b < nu[0], f, nf - 1)

    grid_spec = pltpu.PrefetchScalarGridSpec(
        num_scalar_prefetch=2,
        grid=(nblk, nf),
        in_specs=[
            pl.BlockSpec((MOE_TM, D), lambda b, f, be, nu: (b, 0)),
            pl.BlockSpec((1, D, MOE_TF), lambda b, f, be, nu: (be[b], 0, fidx(b, f, nu))),
            pl.BlockSpec((1, D, MOE_TF), lambda b, f, be, nu: (be[b], 0, fidx(b, f, nu))),
            pl.BlockSpec((1, MOE_TF, D), lambda b, f, be, nu: (be[b], fidx(b, f, nu), 0)),
        ],
        out_specs=pl.BlockSpec((MOE_TM, D), lambda b, f, be, nu: (b, 0)),
        scratch_shapes=[pltpu.VMEM((MOE_TM, D), F32)],
    )
    return pl.pallas_call(
        _moe_kernel,
        grid_spec=grid_spec,
        out_shape=jax.ShapeDtypeStruct((cap, D), F32),
        compiler_params=_params(("arbitrary", "arbitrary"), VMEM_LIMIT),
        name="moe_experts",
    )(blk_e, n_used, xs, wg, wu, wd)


def _moe_route(top_e):
    n = top_e.shape[0]
    a = n * TOP_K
    flat_e = top_e.reshape(a)
    onehot = (flat_e[:, None] == jnp.arange(N_EXPERTS, dtype=flat_e.dtype)[None, :]).astype(jnp.int32)
    counts = jnp.sum(onehot, axis=0)
    padded = (counts + MOE_TM - 1) // MOE_TM * MOE_TM
    pad_ends = jnp.cumsum(padded)
    pad_starts = pad_ends - padded
    dest = jnp.sum(onehot * (jnp.cumsum(onehot, axis=0) - onehot + pad_starts[None, :]), axis=1)
    cap = a + N_EXPERTS * MOE_TM
    nblk = cap // MOE_TM
    row_tok = (jnp.arange(cap, dtype=jnp.int32) % n).at[dest].set(
        jnp.arange(a, dtype=jnp.int32) // TOP_K, unique_indices=True, mode="promise_in_bounds")
    blk_start = jnp.arange(nblk, dtype=jnp.int32) * MOE_TM
    blk_e = jnp.minimum(jnp.sum((pad_ends[None, :] <= blk_start[:, None]).astype(jnp.int32), axis=1),
                        N_EXPERTS - 1)
    n_used = (pad_ends[-1] // MOE_TM).astype(jnp.int32).reshape(1)
    last_e = blk_e[jnp.maximum(n_used[0] - 1, 0)]
    blk_e = jnp.where(jnp.arange(nblk) < n_used[0], blk_e, last_e)
    return dest.reshape(n, TOP_K), row_tok, blk_e, n_used


def _final_kernel(x_ref, y0_ref, y1_ref, gt_ref, mod_ref, fg_ref, o_ref):
    mod = mod_ref[...]
    gt = gt_ref[...]
    f = gt[:, 0:1] * y0_ref[...] + gt[:, 1:2] * y1_ref[...]
    x = x_ref[...] + mod[0:1, 5 * D:6 * D] * f
    o_ref[...] = _rmsnorm_rows(x) * fg_ref[...]


def _moe_combine_final(x, y0, y1, gt, mod, final_g):
    n = x.shape[0]
    row = lambda i: (i, 0)
    const = lambda i: (0, 0)
    return pl.pallas_call(
        _final_kernel,
        grid=(n // TM,),
        in_specs=[pl.BlockSpec((TM, D), row)] * 3
        + [pl.BlockSpec((TM, LANE), row), pl.BlockSpec((8, N_MOD * D), const), pl.BlockSpec((1, D), const)],
        out_specs=pl.BlockSpec((TM, D), row),
        out_shape=jax.ShapeDtypeStruct((n, D), F32),
        compiler_params=_params(("arbitrary",), VMEM_LIMIT),
        name="moe_combine_final",
    )(x, y0, y1, gt, mod, final_g.reshape(1, D))


def _rope_tables(n):
    pos = jnp.arange(n, dtype=jnp.int32)
    r = (pos // GRID_W).astype(F32)
    col = (pos % GRID_W).astype(F32)
    inv = ROPE_BASE ** (-jnp.arange(0, AXIS_DIM, 2, dtype=F32) / AXIS_DIM)
    ang = jnp.concatenate([r[:, None] * inv, r[:, None] * inv, col[:, None] * inv, col[:, None] * inv], axis=1)
    sign = jnp.tile(jnp.concatenate([-jnp.ones((AXIS_DIM // 2,), F32), jnp.ones((AXIS_DIM // 2,), F32)]), 2)
    cos = jnp.tile(jnp.cos(ang), (1, LANE // HEAD_DIM))
    sin = jnp.tile(jnp.sin(ang) * sign, (1, LANE // HEAD_DIM))
    return cos, sin


def _prep_w_in(w):
    pad = jnp.zeros((D, LANE - N_AB), w.dtype)
    return jnp.concatenate([w[:, :_C_Z], w[:, _C_A:_C_Q], pad, w[:, _C_Z:_C_A], w[:, _C_Q:_C_END]],
                           axis=1).astype(BF16)


def kernel(x, c, ctx, c_ctx, w_mod, b_mod, w_in, w_out, conv_w, dn_conv_w, dn_a_log, dn_dt_bias, dn_norm_g,
           attn_sink, ffn_w_gate, ffn_w_up, ffn_w_down, moe_router, moe_w_gate, moe_w_up, moe_w_down,
           final_norm_g):
    bsz, n, d = x.shape
    ctx_len = ctx.shape[1]
    depth = w_in.shape[0]
    assert bsz == 1 and d == D and ctx_len == TM and n % TM == 0 and n % GRID_W == 0
    cos, sin = _rope_tables(n)
    mods = _mod_vectors(c, c_ctx, w_mod, b_mod)
    h = jnp.concatenate([ctx[0], x[0]], axis=0)
    for layer in range(depth):
        last = layer == depth - 1
        mod = mods[layer]
        pconv, pz, pq, pkv, qn, kn, vv, gb = _in_proj(h, mod, _prep_w_in(w_in[layer]), dn_conv_w[layer],
                                                      dn_a_log[layer], dn_dt_bias[layer])
        if (layer + 1) % 2 == 1 and layer + 1 < depth:
            j = (layer + 1) // 2
            to_bf16 = tuple(w[j].reshape(-1, w.shape[-1]) for w in (moe_w_gate, moe_w_up, moe_w_down))
        else:
            to_bf16 = ()
        o_f, o_b, *cast = _delta_net(qn, kn, vv, gb, ctx_len, to_bf16)
        if to_bf16:
            moe_bf16 = [c.reshape(w.shape[1:]) for c, w in zip(cast, (moe_w_gate, moe_w_up, moe_w_down))]
        sink = jnp.zeros((8, LANE), F32).at[0, :ATT_HEADS].set(attn_sink[layer])
        yc = _attention(pq, pkv, cos, sin, sink, ctx_len, with_ctx=not last)
        router = None
        if layer % 2 == 1:
            wr = jnp.zeros((D, LANE), F32).at[:, :N_EXPERTS].set(moe_router[layer // 2])
            wr1 = wr.astype(BF16)
            router = (wr1, (wr - wr1.astype(F32)).astype(BF16))
        outs = _mixer_finish(h, mod, pconv, conv_w[layer], o_f, o_b, pz, dn_norm_g[layer], yc,
                             w_out[layer].astype(BF16), ctx_len, with_ctx=not last, router=router)
        if layer % 2 == 0:
            assert not last
            x1, hx = outs
            i = layer // 2
            h = _dense_ffn(hx, x1, mod, ffn_w_gate[i], ffn_w_up[i], ffn_w_down[i])
        else:
            assert last
            x1, hx, route = outs
            i = layer // 2
            dest, row_tok, blk_e, n_used = _moe_route(route[:, 2:2 + TOP_K].astype(jnp.int32))
            take = lambda rows_, idx: rows_.at[idx].get(mode="promise_in_bounds")
            xs = take(hx, row_tok)
            y = _moe_experts(xs, blk_e, n_used, *moe_bf16)
            y0 = take(y, dest[:, 0])
            y1 = take(y, dest[:, 1])
            h = _moe_combine_final(x1, y0, y1, route, mod, final_norm_g)
    return h.reshape(bsz, n, d)
```

```python
import functools

import jax
import jax.numpy as jnp
from jax import lax
from jax.experimental import pallas as pl
from jax.experimental.pallas import tpu as pltpu

F32 = jnp.float32
BF16 = jnp.bfloat16

D = 1024
N_MOD = 6
EPS = 1e-6
NEG = -1e30
GRID_W = 64

CONV_CH = 256
DN_HEADS = 6
DN_HEAD_DIM = 64
DN_DIM = DN_HEADS * DN_HEAD_DIM
DN_CHUNK = 64
DN_SUB = 16
DN_STEP_CHUNKS = 4
ATT_HEADS = 6
ATT_KV_HEADS = 2
ATT_GROUP = ATT_HEADS // ATT_KV_HEADS
HEAD_DIM = 64
ATT_DIM = ATT_HEADS * HEAD_DIM
ATT_KV_DIM = ATT_KV_HEADS * HEAD_DIM
ATT_BLOCK = 128
ROPE_BASE = 10000.0
AXIS_DIM = HEAD_DIM // 2
MIX_DIM = CONV_CH + DN_DIM + ATT_DIM

D_FF = 2816
N_EXPERTS = 8
TOP_K = 2
D_FF_EXPERT = 3584

TM = 256
FF_CHUNK = 256
MOE_TM = 512
MOE_TF = 1792
MOE_SUB = 256
LANE = 128
VMEM_LIMIT = 56 * 1024 * 1024

_C_QKV = 3 * CONV_CH
_C_Z = _C_QKV + 3 * DN_DIM
_C_A = _C_Z + DN_DIM
_C_Q = _C_A + 4 * DN_HEADS
_C_K = _C_Q + ATT_DIM
_C_V = _C_K + ATT_KV_DIM
_C_END = _C_V + ATT_KV_DIM
N_AB = 4 * DN_HEADS


def _params(sem=None, vmem=None):
    kw = {}
    if sem is not None:
        kw["dimension_semantics"] = sem
    if vmem is not None:
        kw["vmem_limit_bytes"] = vmem
    return pltpu.CompilerParams(**kw)


def _split2(a):
    hi = a.astype(BF16)
    lo = (a - hi.astype(F32)).astype(BF16)
    return hi, lo


def _split3(a):
    hi = a.astype(BF16)
    r = a - hi.astype(F32)
    mid = r.astype(BF16)
    lo = (r - mid.astype(F32)).astype(BF16)
    return hi, mid, lo


_NN = (((1,), (0,)), ((), ()))
_NT = (((1,), (1,)), ((), ()))
_TN = (((0,), (0,)), ((), ()))


def _mm(a, b, dims=_NN):
    return lax.dot_general(a.astype(BF16), b.astype(BF16), dims, preferred_element_type=F32)


def _mm3(a, b, dims=_NN):
    a1, a0 = _split2(a)
    b1, b0 = _split2(b)
    d = functools.partial(lax.dot_general, dimension_numbers=dims, preferred_element_type=F32)
    return d(a1, b1) + (d(a1, b0) + d(a0, b1))


_BNN = (((2,), (1,)), ((0,), (0,)))
_BNT = (((2,), (2,)), ((0,), (0,)))
_BTN = (((1,), (1,)), ((0,), (0,)))


def _bmm(a, b, dims=_BNN):
    return lax.dot_general(a.astype(BF16), b.astype(BF16), dims, preferred_element_type=F32)


def _sigmoid(x):
    return 1.0 / (1.0 + jnp.exp(-x))


def _softplus(x):
    return jnp.maximum(x, 0.0) + jnp.log1p(jnp.exp(-jnp.abs(x)))


def _mod_row(mod_ref, is_ctx):
    mod = mod_ref[...]
    return jnp.where(is_ctx, mod[1:2, :], mod[0:1, :])


def _rmsnorm_rows(x):
    return x * lax.rsqrt(jnp.mean(x * x, axis=-1, keepdims=True) + EPS)


def _shift_rows(u, prow, nrow):
    n = u.shape[0]
    rid = lax.broadcasted_iota(jnp.int32, u.shape, 0)
    up = jnp.where(rid == 0, prow, pltpu.roll(u, 1, 0))
    un = jnp.where(rid == n - 1, nrow, pltpu.roll(u, n - 1, 0))
    return up, un


def _same_group(shape, group):
    sh = group.bit_length() - 1
    assert 1 << sh == group
    return (lax.broadcasted_iota(jnp.int32, shape, 0) >> sh) == (lax.broadcasted_iota(jnp.int32, shape, 1) >> sh)


def _head_blockdiag(n, group):
    return jnp.where(_same_group((n, n), group), 1.0, 0.0).astype(BF16)


def _group_sum(t, bd):
    hi, lo = _split2(t)
    d = functools.partial(lax.dot_general, dimension_numbers=_NN, preferred_element_type=F32)
    return d(hi, bd) + d(lo, bd)


MOD_TN = 1536


def _mod_kernel(s_ref, w_ref, b_ref, o_ref):
    s = s_ref[...]
    s = s * _sigmoid(s)
    o_ref[0] = _mm3(s, w_ref[0]) + b_ref[0]


def _mod_vectors(c, c_ctx, w_mod, b_mod):
    depth = w_mod.shape[0]
    s = jnp.zeros((8, D), F32).at[0].set(c[0]).at[1].set(c_ctx)
    return pl.pallas_call(
        _mod_kernel,
        grid=(depth, N_MOD * D // MOD_TN),
        in_specs=[
            pl.BlockSpec((8, D), lambda l, j: (0, 0)),
            pl.BlockSpec((1, D, MOD_TN), lambda l, j: (l, 0, j)),
            pl.BlockSpec((1, 1, MOD_TN), lambda l, j: (l, 0, j)),
        ],
        out_specs=pl.BlockSpec((1, 8, MOD_TN), lambda l, j: (l, 0, j)),
        out_shape=jax.ShapeDtypeStruct((depth, 8, N_MOD * D), F32),
        compiler_params=_params(("arbitrary", "arbitrary"), VMEM_LIMIT),
        name="mod_vectors",
    )(s, w_mod, b_mod.reshape(depth, 1, N_MOD * D))


def _halo_valid(i, nblk):
    return jnp.logical_and(i != 0, i != 1), jnp.logical_and(i != 0, i != nblk - 1)


def _halo_rows(prev_ref, next_ref, i, nblk):
    pvalid, nvalid = _halo_valid(i, nblk)
    prow = jnp.where(pvalid, prev_ref[7:8, :], 0.0)
    nrow = jnp.where(nvalid, next_ref[0:1, :], 0.0)
    return prow, nrow


def _in_kernel(h_ref, hprev_ref, hnext_ref, mod_ref, w_ref, cw_ref, alog_ref, dtb_ref,
               pconv_ref, pz_ref, pq_ref, pkv_ref, q_ref, k_ref, v_ref, gb_ref, *, nblk):
    i = pl.program_id(0)
    row = _mod_row(mod_ref, i == 0)
    norm_mod = lambda x: _rmsnorm_rows(x) * (1.0 + row[:, D:2 * D]) + row[:, 0:D]
    h1 = norm_mod(h_ref[...]).astype(BF16)
    halo = norm_mod(jnp.concatenate([hprev_ref[...], hnext_ref[...]], axis=0)).astype(BF16)
    d = functools.partial(lax.dot_general, dimension_numbers=_NN, preferred_element_type=F32)
    tm = h1.shape[0]
    c0 = 3 * CONV_CH
    c1 = c0 + 3 * DN_DIM
    c2 = c1 + LANE
    c3 = c2 + DN_DIM + ATT_DIM
    pconv_ref[...] = d(h1, w_ref[:, 0:c0])
    zq = d(h1, w_ref[:, c2:c3])
    pz_ref[...] = zq[:, 0:DN_DIM]
    pq_ref[...] = zq[:, DN_DIM:]
    pkv_ref[...] = d(h1, w_ref[:, c3:c3 + 2 * ATT_KV_DIM])
    qkv_ab = d(jnp.concatenate([h1, halo], axis=0), w_ref[:, c0:c2])
    qkv = qkv_ab[:, 0:3 * DN_DIM]
    ab = qkv_ab[0:tm, 3 * DN_DIM:]

    u = qkv[0:tm, :]
    pvalid, nvalid = _halo_valid(i, nblk)
    prow = jnp.where(pvalid, qkv[tm + 7:tm + 8, :], 0.0)
    nrow = jnp.where(nvalid, qkv[tm + 8:tm + 9, :], 0.0)
    up, un = _shift_rows(u, prow, nrow)
    cw = cw_ref[...]
    y = up * cw[0:1, :] + u * cw[1:2, :] + un * cw[2:3, :]
    y = y * _sigmoid(y)
    q = y[:, 0:DN_DIM]
    k = y[:, DN_DIM:2 * DN_DIM]
    bd = _head_blockdiag(DN_DIM, DN_HEAD_DIM)
    q_ref[...] = q * lax.rsqrt(_group_sum(q * q, bd) + 1e-6) * (DN_HEAD_DIM ** -0.5)
    k_ref[...] = k * lax.rsqrt(_group_sum(k * k, bd) + 1e-6)
    v_ref[...] = y[:, 2 * DN_DIM:3 * DN_DIM]
    g = -jnp.exp(alog_ref[...]) * _softplus(ab + dtb_ref[...])
    lane = lax.broadcasted_iota(jnp.int32, ab.shape, 1)
    gb_ref[...] = jnp.where(lane < 2 * DN_HEADS, g, _sigmoid(ab))


def _in_proj(h, mod, w_main, dn_conv_w, a_log, dt_bias):
    t = h.shape[0]
    nblk = t // TM
    r8 = TM // 8
    alog = jnp.zeros((1, LANE), F32).at[0, :2 * DN_HEADS].set(a_log.reshape(-1))
    dtb = jnp.zeros((1, LANE), F32).at[0, :2 * DN_HEADS].set(dt_bias.reshape(-1))
    widths = (3 * CONV_CH, DN_DIM, ATT_DIM, 2 * ATT_KV_DIM, DN_DIM, DN_DIM, DN_DIM, LANE)
    const = lambda i: (0, 0)
    return pl.pallas_call(
        functools.partial(_in_kernel, nblk=nblk),
        grid=(nblk,),
        in_specs=[
            pl.BlockSpec((TM, D), lambda i: (i, 0)),
            pl.BlockSpec((8, D), lambda i: (jnp.maximum(i * r8 - 1, 0), 0)),
            pl.BlockSpec((8, D), lambda i: (jnp.minimum((i + 1) * r8, t // 8 - 1), 0)),
            pl.BlockSpec((8, N_MOD * D), const),
            pl.BlockSpec(w_main.shape, const),
            pl.BlockSpec((3, 3 * DN_DIM), const),
            pl.BlockSpec((1, LANE), const),
            pl.BlockSpec((1, LANE), const),
        ],
        out_specs=[pl.BlockSpec((TM, w), lambda i: (i, 0)) for w in widths],
        out_shape=[jax.ShapeDtypeStruct((t, w), F32) for w in widths],
        compiler_params=_params(("arbitrary",), VMEM_LIMIT),
        name="in_proj",
    )(h, h, h, mod, w_main, dn_conv_w, alog, dtb)


def _dn_chunk(rev, q_ref, k_ref, v_ref, gb_ref, o_ref, s_ref):
    c_ = DN_CHUNK
    ri = lax.broadcasted_iota(jnp.int32, (c_, c_), 0)
    ci = lax.broadcasted_iota(jnp.int32, (c_, c_), 1)
    incl = (ri <= ci) if rev else (ri >= ci)
    strict = (ri < ci) if rev else (ri > ci)
    same_sub = _same_group((c_, c_), DN_SUB)
    eye = jnp.where(ri == ci, 1.0, 0.0)
    tri = jnp.where(incl, 1.0, 0.0).astype(BF16)
    last = 0 if rev else c_ - 1

    nchunks = q_ref.shape[0] // c_
    nh = DN_HEADS
    col0 = nh if rev else 0
    rows = lambda g: slice(g * c_, (g + 1) * c_)
    lanes = lambda h: slice(h * DN_HEAD_DIM, (h + 1) * DN_HEAD_DIM)

    def stack(fn):
        return jnp.stack([fn(g, h) for g in range(nchunks) for h in range(nh)])

    gb = gb_ref[...]
    gcs = [_cumsum_rows(tri, gb[rows(g), :]) for g in range(nchunks)]
    gcts = [gc.T for gc in gcs]
    q = stack(lambda g, h: q_ref[rows(g), lanes(h)])
    k = stack(lambda g, h: k_ref[rows(g), lanes(h)])
    v = stack(lambda g, h: v_ref[rows(g), lanes(h)])
    gcol = stack(lambda g, h: gcs[g][:, col0 + h:col0 + h + 1])
    grow = stack(lambda g, h: gcts[g][col0 + h:col0 + h + 1, :])
    beta = stack(lambda g, h: gb[rows(g), 2 * nh + col0 + h:2 * nh + col0 + h + 1])
    glast = gcol[:, last:last + 1, :]
    decay = jnp.where(incl, jnp.exp(jnp.where(incl, gcol - grow, 0.0)), 0.0)
    eg = jnp.exp(gcol)
    kb = k * beta
    a = jnp.where(strict, _bmm(kb, k, _BNT) * decay, 0.0)
    qk = jnp.where(incl, _bmm(q, k, _BNT) * decay, 0.0)
    ad = jnp.where(same_sub, a, 0.0)
    ao = a - ad
    p = eye - ad
    n2 = _bmm(ad, ad)
    p = p + _bmm(p, n2)
    n4 = _bmm(n2, n2)
    p = p + _bmm(p, n4)
    n8 = _bmm(n4, n4)
    dinv = p + _bmm(p, n8)
    m = _bmm(dinv, ao)
    m2 = _bmm(m, m)
    y = _bmm(dinv, jnp.concatenate([v * beta, kb * eg], axis=-1))
    z = y + _bmm(m2, y)
    x = z - _bmm(m, z)
    u = x[:, :, :DN_HEAD_DIM]
    w = x[:, :, DN_HEAD_DIM:]
    qg = q * eg
    kd = k * jnp.exp(glast - gcol)
    gl = jnp.exp(glast)
    s = s_ref[col0:col0 + nh]
    for g in (reversed(range(nchunks)) if rev else range(nchunks)):
        b = slice(g * nh, (g + 1) * nh)
        v_new = u[b] - _bmm(w[b], s)
        o = _bmm(qg[b], s) + _bmm(qk[b], v_new)
        s = s * gl[b] + _bmm(kd[b], v_new, _BTN)
        for h in range(nh):
            o_ref[rows(g), lanes(h)] = o[h]
    s_ref[col0:col0 + nh] = s


def _cumsum_rows(tri_bf16, g):
    g2, g1, g0 = _split3(g)
    d = functools.partial(lax.dot_general, dimension_numbers=_NN, preferred_element_type=F32)
    return d(tri_bf16, g2) + (d(tri_bf16, g1) + d(tri_bf16, g0))


def _dn_kernel(qf, kf, vf, gf, qb, kb, vb, gbb, *rest, n_cast):
    cast_in = rest[:n_cast]
    of_ref, ob_ref = rest[n_cast:n_cast + 2]
    cast_out = rest[n_cast + 2:2 * n_cast + 2]
    s_ref = rest[-1]

    @pl.when(pl.program_id(0) == 0)
    def _():
        s_ref[...] = jnp.zeros(s_ref.shape, F32)

    _dn_chunk(False, qf, kf, vf, gf, of_ref, s_ref)
    _dn_chunk(True, qb, kb, vb, gbb, ob_ref, s_ref)
    for src, dst in zip(cast_in, cast_out):
        dst[...] = src[...].astype(BF16)


def _delta_net(q, k, v, gb, ctx_len, to_bf16=()):
    t = q.shape[0]
    rows = DN_STEP_CHUNKS * DN_CHUNK
    assert ctx_len == rows and t % rows == 0
    nstep = t // rows
    cast_specs = []
    for m in to_bf16:
        rb = -(-m.shape[0] // nstep)
        rb = -(-rb // 16) * 16
        nb = -(-m.shape[0] // rb)
        cast_specs.append(pl.BlockSpec((rb, m.shape[1]), lambda s, nb=nb: (jnp.minimum(s, nb - 1), 0)))

    def fwd(s):
        return (s, 0)

    def bwd(s):
        return (jnp.where(s == 0, 0, nstep - s), 0)

    wide = lambda im: pl.BlockSpec((rows, DN_DIM), im)
    narrow = lambda im: pl.BlockSpec((rows, LANE), im)
    return pl.pallas_call(
        functools.partial(_dn_kernel, n_cast=len(to_bf16)),
        grid=(nstep,),
        in_specs=[wide(fwd), wide(fwd), wide(fwd), narrow(fwd), wide(bwd), wide(bwd), wide(bwd), narrow(bwd)]
        + cast_specs,
        out_specs=[wide(fwd), wide(bwd)] + cast_specs,
        out_shape=[jax.ShapeDtypeStruct((t, DN_DIM), F32)] * 2
        + [jax.ShapeDtypeStruct(m.shape, BF16) for m in to_bf16],
        scratch_shapes=[pltpu.VMEM((2 * DN_HEADS, DN_HEAD_DIM, DN_HEAD_DIM), F32)],
        compiler_params=_params(("arbitrary",), VMEM_LIMIT),
        name="delta_net",
    )(q, k, v, gb, q, k, v, gb, *to_bf16)


def _rope(x, cos, sin):
    w = x.shape[1]
    lane = lax.broadcasted_iota(jnp.int32, x.shape, 1)
    first_half = (lane & (AXIS_DIM - 1)) < (AXIS_DIM // 2)
    swapped = jnp.where(first_half, pltpu.roll(x, w - AXIS_DIM // 2, 1), pltpu.roll(x, AXIS_DIM // 2, 1))
    return x * cos + swapped * sin


def _softmax_av(s, sink, vals):
    m = jnp.maximum(jnp.max(s, axis=-1, keepdims=True), sink)
    p = jnp.exp(s - m)
    denom = jnp.sum(p, axis=-1, keepdims=True) + jnp.exp(sink - m)
    return _mm(p, vals) / denom


def _attend(q, keys, vals, valid, sink_all, o_ref, row0=0):
    b = q.shape[0]
    for kvh in range(ATT_KV_HEADS):
        kl = slice(kvh * HEAD_DIM, (kvh + 1) * HEAD_DIM)
        heads = range(kvh * ATT_GROUP, (kvh + 1) * ATT_GROUP)
        qs = jnp.concatenate([q[:, h * HEAD_DIM:(h + 1) * HEAD_DIM] for h in heads], axis=0)
        sink = jnp.concatenate([jnp.broadcast_to(sink_all[0:1, h:h + 1], (b, 1)) for h in heads], axis=0)
        s = _mm(qs, keys[:, kl], _NT)
        if valid is not None:
            s = jnp.where(valid, s, NEG)
        o = _softmax_av(s, sink, vals[:, kl])
        for g, h in enumerate(heads):
            o_ref[row0:row0 + b, h * HEAD_DIM:(h + 1) * HEAD_DIM] = o[g * b:(g + 1) * b, :]


def _band_valid(nk, first, last):
    b = ATT_BLOCK
    c = lax.broadcasted_iota(jnp.int32, (1, nk), 1)
    far = 4 * nk
    ccol = jnp.where(c >= 3 * b, b + ((c - 3 * b) & (b - 1)), c)
    ccol = jnp.where(jnp.logical_and(c < b, first), -far, ccol)
    ccol = jnp.where(jnp.logical_and(jnp.logical_and(c >= 2 * b, c < 3 * b), last), far, ccol)
    r = lax.broadcasted_iota(jnp.int32, (ATT_GROUP * b, 1), 0) & (b - 1)
    return lax.bitcast_convert_type(ccol - r, jnp.uint32) <= jnp.uint32(2 * b)


def _rope_block(rowtab_ref, coltab_ref, blk):
    rt = rowtab_ref[blk]
    ct = coltab_ref[...]
    reps = ATT_BLOCK // GRID_W
    rows = jnp.concatenate([jnp.broadcast_to(rt[g:g + 1, :], (GRID_W, LANE)) for g in range(reps)], axis=0)
    return rows + jnp.concatenate([ct] * reps, axis=0)


def _attn_kernel(q_ref, kp_ref, kc_ref, kn_ref, kctx_ref, cosr_ref, sinr_ref, cosc_ref, sinc_ref, sink_ref, o_ref,
                 *, nb, with_ctx):
    j = pl.program_id(0)
    b = ATT_BLOCK
    scale = HEAD_DIM ** -0.5

    @pl.when(j == 0)
    def _():
        if with_ctx:
            kvx = kctx_ref[...]
            _attend(q_ref[...] * scale, kvx[:, :ATT_KV_DIM], kvx[:, ATT_KV_DIM:], None, sink_ref[...], o_ref)
        else:
            o_ref[...] = jnp.zeros(o_ref.shape, F32)

    @pl.when(j > 0)
    def _():
        b0 = 2 * (j - 1)
        blocks = (jnp.maximum(b0 - 1, 0), b0, b0 + 1, jnp.minimum(b0 + 2, nb - 1))
        cos = [_rope_block(cosr_ref, cosc_ref, blk) for blk in blocks]
        sin = [_rope_block(sinr_ref, sinc_ref, blk) for blk in blocks]
        kvc = kc_ref[...]
        kv = (kp_ref[...], kvc[0:b, :], kvc[b:2 * b, :], kn_ref[...])
        kvx = kctx_ref[...]
        keys = [_rope(t[:, :ATT_KV_DIM], c_, s_) for t, c_, s_ in zip(kv, cos, sin)]
        q_all = q_ref[...]
        sink_all = sink_ref[...]
        nk = 3 * b + kvx.shape[0]
        for sub in range(2):
            q = q_all[sub * b:(sub + 1) * b, :]
            q = jnp.concatenate([_rope(q[:, l * LANE:(l + 1) * LANE], cos[1 + sub], sin[1 + sub])
                                 for l in range(ATT_DIM // LANE)], axis=1)
            kcat = jnp.concatenate(keys[sub:sub + 3] + [kvx[:, :ATT_KV_DIM]], axis=0)
            vcat = jnp.concatenate([t[:, ATT_KV_DIM:] for t in kv[sub:sub + 3]] + [kvx[:, ATT_KV_DIM:]], axis=0)
            valid = _band_valid(nk, first=(b0 + sub == 0), last=(b0 + sub == nb - 1))
            _attend(q * scale, kcat, vcat, valid, sink_all, o_ref, row0=sub * b)


def _attention(pq, pkv, rope, sink, ctx_len, with_ctx):
    t = pq.shape[0]
    n = t - ctx_len
    nb = n // ATT_BLOCK
    step = 2 * ATT_BLOCK
    assert ctx_len == step and nb % 2 == 0
    off = ctx_len // ATT_BLOCK
    half = lambda im: pl.BlockSpec((ATT_BLOCK, 2 * ATT_KV_DIM), im)
    lat = lambda j: 2 * jnp.maximum(j - 1, 0)
    prv = lambda j: (jnp.maximum(lat(j) - 1, 0) + off, 0)
    nxt = lambda j: (jnp.minimum(lat(j) + 2, nb - 1) + off, 0)
    cur = lambda j: (j, 0)
    whole = lambda a: pl.BlockSpec(a.shape, lambda j: (0,) * a.ndim)
    return pl.pallas_call(
        functools.partial(_attn_kernel, nb=nb, with_ctx=with_ctx),
        grid=(1 + nb // 2,),
        in_specs=[
            pl.BlockSpec((step, ATT_DIM), cur),
            half(prv), pl.BlockSpec((step, 2 * ATT_KV_DIM), cur), half(nxt),
            pl.BlockSpec((ctx_len, 2 * ATT_KV_DIM), lambda j: (0, 0)),
        ] + [whole(a) for a in rope] + [pl.BlockSpec((8, LANE), lambda j: (0, 0))],
        out_specs=pl.BlockSpec((step, ATT_DIM), cur),
        out_shape=jax.ShapeDtypeStruct((t, ATT_DIM), F32),
        compiler_params=_params(("arbitrary",), VMEM_LIMIT),
        name="attention",
    )(pq, pkv, pkv, pkv, pkv, *rope, sink)


def _mixfin_kernel(h_ref, mod_ref, pconv_ref, prev_ref, next_ref, cw_ref, of_ref, ob_ref, z_ref, ng_ref, yc_ref,
                   wout_ref, *rest, nblk, blk0, with_router):
    if with_router:
        wr1_ref, wr0_ref, x_ref, hx_ref, lg_ref = rest
    else:
        x_ref, hx_ref = rest
    i = pl.program_id(0) + blk0
    row = _mod_row(mod_ref, i == 0)
    pc = pconv_ref[...]
    u = pc[:, CONV_CH:2 * CONV_CH] * pc[:, 2 * CONV_CH:]
    prow, nrow = _halo_rows(prev_ref, next_ref, i, nblk)
    prow = prow[:, CONV_CH:2 * CONV_CH] * prow[:, 2 * CONV_CH:]
    nrow = nrow[:, CONV_CH:2 * CONV_CH] * nrow[:, 2 * CONV_CH:]
    up, un = _shift_rows(u, prow, nrow)
    cw = cw_ref[...]
    ya = pc[:, :CONV_CH] * (up * cw[0:1, :] + u * cw[1:2, :] + un * cw[2:3, :])
    o = of_ref[...] + ob_ref[...]
    ms = _group_sum(o * o, _head_blockdiag(DN_DIM, DN_HEAD_DIM)) * (1.0 / DN_HEAD_DIM)
    z = z_ref[...]
    yb = o * lax.rsqrt(ms + EPS) * ng_ref[...] * (z * _sigmoid(z))
    mix = jnp.concatenate([ya, yb, yc_ref[...]], axis=1)
    x = h_ref[...] + row[:, 2 * D:3 * D] * _mm(mix, wout_ref[...])
    x_ref[...] = x
    hx = _rmsnorm_rows(x) * (1.0 + row[:, 4 * D:5 * D]) + row[:, 3 * D:4 * D]
    hx_ref[...] = hx.astype(hx_ref.dtype)
    if with_router:
        h1, h0 = _split2(hx)
        d = functools.partial(lax.dot_general, dimension_numbers=_NN, preferred_element_type=F32)
        lg = d(h1, wr1_ref[...]) + (d(h0, wr1_ref[...]) + d(h1, wr0_ref[...]))
        lane = lax.broadcasted_iota(jnp.int32, lg.shape, 1)
        lanef = lane.astype(F32)
        lg = jnp.where(lane < N_EXPERTS, lg, -jnp.inf)
        m1 = jnp.max(lg, axis=-1, keepdims=True)
        i1 = jnp.min(jnp.where(lg == m1, lanef, float(LANE)), axis=-1, keepdims=True)
        rest = jnp.where(lanef == i1, -jnp.inf, lg)
        m2 = jnp.max(rest, axis=-1, keepdims=True)
        i2 = jnp.min(jnp.where(rest == m2, lanef, float(LANE)), axis=-1, keepdims=True)
        e2 = jnp.exp(m2 - m1)
        g1 = 1.0 / (1.0 + e2)
        lg_ref[...] = jnp.where(lane == 0, g1, jnp.where(lane == 1, e2 * g1, jnp.where(lane == 2, i1, i2)))


def _mixer_finish(h, mod, pconv, conv_w, o_f, o_b, pz, norm_g, yc, w_out, ctx_len, with_ctx, router=None):
    t = h.shape[0]
    nblk = t // TM
    blk0 = 0 if with_ctx else ctx_len // TM
    rows = t - blk0 * TM
    r8 = TM // 8
    w = pconv.shape[1]
    cur = lambda i: (i + blk0, 0)
    out_cur = lambda i: (i, 0)
    const = lambda i: (0, 0)
    ng = jnp.tile(norm_g.reshape(1, DN_HEAD_DIM), (1, DN_HEADS))
    in_specs = [
        pl.BlockSpec((TM, D), cur),
        pl.BlockSpec((8, N_MOD * D), const),
        pl.BlockSpec((TM, w), cur),
        pl.BlockSpec((8, w), lambda i: (jnp.maximum((i + blk0) * r8 - 1, 0), 0)),
        pl.BlockSpec((8, w), lambda i: (jnp.minimum((i + blk0 + 1) * r8, t // 8 - 1), 0)),
        pl.BlockSpec((3, CONV_CH), const),
        pl.BlockSpec((TM, DN_DIM), cur),
        pl.BlockSpec((TM, DN_DIM), cur),
        pl.BlockSpec((TM, DN_DIM), cur),
        pl.BlockSpec((1, DN_DIM), const),
        pl.BlockSpec((TM, ATT_DIM), cur),
        pl.BlockSpec((MIX_DIM, D), const),
    ]
    args = [h, mod, pconv, pconv, pconv, conv_w, o_f, o_b, pz, ng, yc, w_out]
    out_specs = [pl.BlockSpec((TM, D), out_cur), pl.BlockSpec((TM, D), out_cur)]
    hx_dtype = BF16 if router is None else F32
    out_shape = [jax.ShapeDtypeStruct((rows, D), F32), jax.ShapeDtypeStruct((rows, D), hx_dtype)]
    if router is not None:
        in_specs += [pl.BlockSpec((D, LANE), const)] * 2
        args += list(router)
        out_specs.append(pl.BlockSpec((TM, LANE), out_cur))
        out_shape.append(jax.ShapeDtypeStruct((rows, LANE), F32))
    return pl.pallas_call(
        functools.partial(_mixfin_kernel, nblk=nblk, blk0=blk0, with_router=router is not None),
        grid=(rows // TM,),
        in_specs=in_specs,
        out_specs=out_specs,
        out_shape=out_shape,
        compiler_params=_params(("arbitrary",), VMEM_LIMIT),
        name="mixer_finish",
    )(*args)


def _ffn_kernel(hx_ref, x_ref, mod_ref, wg_ref, wu_ref, wd_ref, o_ref):
    row = _mod_row(mod_ref, pl.program_id(0) == 0)
    hx = hx_ref[...]
    acc = jnp.zeros((hx.shape[0], D), F32)
    for f in range(0, D_FF, FF_CHUNK):
        g = _mm(hx, wg_ref[:, f:f + FF_CHUNK])
        u = _mm(hx, wu_ref[:, f:f + FF_CHUNK])
        acc = acc + _mm(g * _sigmoid(g) * u, wd_ref[f:f + FF_CHUNK, :])
    o_ref[...] = x_ref[...] + row[:, 5 * D:6 * D] * acc


def _dense_ffn(hx, x, mod, wg, wu, wd):
    t = x.shape[0]
    const = lambda i: (0, 0)
    once = dict(pipeline_mode=pl.Buffered(1))
    return pl.pallas_call(
        _ffn_kernel,
        grid=(t // TM,),
        in_specs=[
            pl.BlockSpec((TM, D), lambda i: (i, 0)),
            pl.BlockSpec((TM, D), lambda i: (i, 0)),
            pl.BlockSpec((8, N_MOD * D), const),
            pl.BlockSpec((D, D_FF), const, **once),
            pl.BlockSpec((D, D_FF), const, **once),
            pl.BlockSpec((D_FF, D), const, **once),
        ],
        out_specs=pl.BlockSpec((TM, D), lambda i: (i, 0)),
        out_shape=jax.ShapeDtypeStruct((t, D), F32),
        compiler_params=_params(("arbitrary",), VMEM_LIMIT),
        name="dense_ffn",
    )(hx, x, mod, wg, wu, wd)


def _moe_kernel(be_ref, nu_ref, xs_ref, wg_ref, wu_ref, wd_ref, y_ref, acc_ref):
    b = pl.program_id(0)
    f = pl.program_id(1)

    @pl.when(b < nu_ref[0])
    def _():
        xs = xs_ref[...].astype(BF16)
        part = jnp.zeros((MOE_TM, D), F32)
        for c in range(0, MOE_TF, MOE_SUB):
            g = _mm(xs, wg_ref[0, :, c:c + MOE_SUB])
            u = _mm(xs, wu_ref[0, :, c:c + MOE_SUB])
            part = part + _mm(g * _sigmoid(g) * u, wd_ref[0, c:c + MOE_SUB, :])

        @pl.when(f == 0)
        def _():
            acc_ref[...] = part

        @pl.when(f != 0)
        def _():
            acc_ref[...] += part

        @pl.when(f == pl.num_programs(1) - 1)
        def _():
            y_ref[...] = acc_ref[...]

    @pl.when(b >= nu_ref[0])
    def _():
        y_ref[...] = jnp.zeros(y_ref.shape, F32)


def _moe_experts(xs, blk_e, n_used, wg, wu, wd):
    cap = xs.shape[0]
    nblk = cap // MOE_TM
    nf = D_FF_EXPERT // MOE_TF

    def fidx(b, f, nu):
        return jnp.where(b < nu[0], f, nf - 1)

    grid_spec = pltpu.PrefetchScalarGridSpec(
        num_scalar_prefetch=2,
        grid=(nblk, nf),
        in_specs=[
            pl.BlockSpec((MOE_TM, D), lambda b, f, be, nu: (b, 0)),
            pl.BlockSpec((1, D, MOE_TF), lambda b, f, be, nu: (be[b], 0, fidx(b, f, nu))),
            pl.BlockSpec((1, D, MOE_TF), lambda b, f, be, nu: (be[b], 0, fidx(b, f, nu))),
            pl.BlockSpec((1, MOE_TF, D), lambda b, f, be, nu: (be[b], fidx(b, f, nu), 0)),
        ],
        out_specs=pl.BlockSpec((MOE_TM, D), lambda b, f, be, nu: (b, 0)),
        scratch_shapes=[pltpu.VMEM((MOE_TM, D), F32)],
    )
    return pl.pallas_call(
        _moe_kernel,
        grid_spec=grid_spec,
        out_shape=jax.ShapeDtypeStruct((cap, D), F32),
        compiler_params=_params(("arbitrary", "arbitrary"), VMEM_LIMIT),
        name="moe_experts",
    )(blk_e, n_used, xs, wg, wu, wd)


def _moe_route(top_e):
    n = top_e.shape[0]
    a = n * TOP_K
    flat_e = top_e.reshape(a)
    onehot = (flat_e[:, None] == jnp.arange(N_EXPERTS, dtype=flat_e.dtype)[None, :]).astype(jnp.int32)
    counts = jnp.sum(onehot, axis=0)
    padded = (counts + MOE_TM - 1) // MOE_TM * MOE_TM
    pad_ends = jnp.cumsum(padded)
    pad_starts = pad_ends - padded
    dest = jnp.sum(onehot * (jnp.cumsum(onehot, axis=0) - onehot + pad_starts[None, :]), axis=1)
    cap = a + N_EXPERTS * MOE_TM
    nblk = cap // MOE_TM
    row_tok = (jnp.arange(cap, dtype=jnp.int32) % n).at[dest].set(
        jnp.arange(a, dtype=jnp.int32) // TOP_K, unique_indices=True, mode="promise_in_bounds")
    blk_start = jnp.arange(nblk, dtype=jnp.int32) * MOE_TM
    blk_e = jnp.minimum(jnp.sum((pad_ends[None, :] <= blk_start[:, None]).astype(jnp.int32), axis=1),
                        N_EXPERTS - 1)
    n_used = (pad_ends[-1] // MOE_TM).astype(jnp.int32).reshape(1)
    last_e = blk_e[jnp.maximum(n_used[0] - 1, 0)]
    blk_e = jnp.where(jnp.arange(nblk) < n_used[0], blk_e, last_e)
    return dest.reshape(n, TOP_K), row_tok, blk_e, n_used


def _final_kernel(x_ref, y0_ref, y1_ref, gt_ref, mod_ref, fg_ref, o_ref):
    mod = mod_ref[...]
    gt = gt_ref[...]
    f = gt[:, 0:1] * y0_ref[...] + gt[:, 1:2] * y1_ref[...]
    x = x_ref[...] + mod[0:1, 5 * D:6 * D] * f
    o_ref[...] = _rmsnorm_rows(x) * fg_ref[...]


def _moe_combine_final(x, y0, y1, gt, mod, final_g):
    n = x.shape[0]
    row = lambda i: (i, 0)
    const = lambda i: (0, 0)
    return pl.pallas_call(
        _final_kernel,
        grid=(n // TM,),
        in_specs=[pl.BlockSpec((TM, D), row)] * 3
        + [pl.BlockSpec((TM, LANE), row), pl.BlockSpec((8, N_MOD * D), const), pl.BlockSpec((1, D), const)],
        out_specs=pl.BlockSpec((TM, D), row),
        out_shape=jax.ShapeDtypeStruct((n, D), F32),
        compiler_params=_params(("arbitrary",), VMEM_LIMIT),
        name="moe_combine_final",
    )(x, y0, y1, gt, mod, final_g.reshape(1, D))


def _rope_tables(n):
    lane = jnp.arange(LANE, dtype=jnp.int32) % HEAD_DIM
    inv = ROPE_BASE ** (-jnp.arange(0, AXIS_DIM, 2, dtype=F32) / AXIS_DIM)
    freq = inv[lane % (AXIS_DIM // 2)]
    row_axis = (lane // AXIS_DIM) == 0
    sign = jnp.where((lane % AXIS_DIM) < AXIS_DIM // 2, -1.0, 1.0)
    reps = ATT_BLOCK // GRID_W

    def tables(count, on_axis):
        ang = jnp.arange(count, dtype=F32)[:, None] * freq[None, :]
        return jnp.where(on_axis, jnp.cos(ang), 0.0), jnp.where(on_axis, jnp.sin(ang) * sign, 0.0)

    by_block = lambda t: jnp.pad(t.reshape(-1, reps, LANE), ((0, 0), (0, 8 - reps), (0, 0)))
    cosr, sinr = tables(n // GRID_W, row_axis)
    cosc, sinc = tables(GRID_W, ~row_axis)
    return by_block(cosr), by_block(sinr), cosc, sinc


def _prep_w_in(w):
    pad = jnp.zeros((D, LANE - N_AB), w.dtype)
    return jnp.concatenate([w[:, :_C_Z], w[:, _C_A:_C_Q], pad, w[:, _C_Z:_C_A], w[:, _C_Q:_C_END]],
                           axis=1).astype(BF16)


def kernel(x, c, ctx, c_ctx, w_mod, b_mod, w_in, w_out, conv_w, dn_conv_w, dn_a_log, dn_dt_bias, dn_norm_g,
           attn_sink, ffn_w_gate, ffn_w_up, ffn_w_down, moe_router, moe_w_gate, moe_w_up, moe_w_down,
           final_norm_g):
    bsz, n, d = x.shape
    ctx_len = ctx.shape[1]
    depth = w_in.shape[0]
    assert bsz == 1 and d == D and ctx_len == TM and n % TM == 0 and n % GRID_W == 0
    rope = _rope_tables(n)
    mods = _mod_vectors(c, c_ctx, w_mod, b_mod)
    h = jnp.concatenate([ctx[0], x[0]], axis=0)
    for layer in range(depth):
        last = layer == depth - 1
        mod = mods[layer]
        pconv, pz, pq, pkv, qn, kn, vv, gb = _in_proj(h, mod, _prep_w_in(w_in[layer]), dn_conv_w[layer],
                                                      dn_a_log[layer], dn_dt_bias[layer])
        if (layer + 1) % 2 == 1 and layer + 1 < depth:
            j = (layer + 1) // 2
            to_bf16 = tuple(w[j].reshape(-1, w.shape[-1]) for w in (moe_w_gate, moe_w_up, moe_w_down))
        else:
            to_bf16 = ()
        o_f, o_b, *cast = _delta_net(qn, kn, vv, gb, ctx_len, to_bf16)
        if to_bf16:
            moe_bf16 = [c.reshape(w.shape[1:]) for c, w in zip(cast, (moe_w_gate, moe_w_up, moe_w_down))]
        sink = jnp.zeros((8, LANE), F32).at[0, :ATT_HEADS].set(attn_sink[layer])
        yc = _attention(pq, pkv, rope, sink, ctx_len, with_ctx=not last)
        router = None
        if layer % 2 == 1:
            wr = jnp.zeros((D, LANE), F32).at[:, :N_EXPERTS].set(moe_router[layer // 2])
            wr1 = wr.astype(BF16)
            router = (wr1, (wr - wr1.astype(F32)).astype(BF16))
        outs = _mixer_finish(h, mod, pconv, conv_w[layer], o_f, o_b, pz, dn_norm_g[layer], yc,
                             w_out[layer].astype(BF16), ctx_len, with_ctx=not last, router=router)
        if layer % 2 == 0:
            assert not last
            x1, hx = outs
            i = layer // 2
            h = _dense_ffn(hx, x1, mod, ffn_w_gate[i], ffn_w_up[i], ffn_w_down[i])
        else:
            assert last
            x1, hx, route = outs
            i = layer // 2
            dest, row_tok, blk_e, n_used = _moe_route(route[:, 2:2 + TOP_K].astype(jnp.int32))
            take = lambda rows_, idx: rows_.at[idx].get(mode="promise_in_bounds")
            xs = take(hx, row_tok)
            y = _moe_experts(xs, blk_e, n_used, *moe_bf16)
            y0 = take(y, dest[:, 0])
            y1 = take(y, dest[:, 1])
            h = _moe_combine_final(x1, y0, y1, route, mod, final_norm_g)
    return h.reshape(bsz, n, d)
```

```python
import functools

import jax
import jax.numpy as jnp
from jax import lax
from jax.experimental import pallas as pl
from jax.experimental.pallas import tpu as pltpu

F32 = jnp.float32
BF16 = jnp.bfloat16

D = 1024
N_MOD = 6
EPS = 1e-6
NEG = -1e30
GRID_W = 64

CONV_CH = 256
DN_HEADS = 6
DN_HEAD_DIM = 64
DN_DIM = DN_HEADS * DN_HEAD_DIM
DN_CHUNK = 64
DN_SUB = 16
DN_STEP_CHUNKS = 4
ATT_HEADS = 6
ATT_KV_HEADS = 2
ATT_GROUP = ATT_HEADS // ATT_KV_HEADS
HEAD_DIM = 64
ATT_DIM = ATT_HEADS * HEAD_DIM
ATT_KV_DIM = ATT_KV_HEADS * HEAD_DIM
ATT_BLOCK = 128
ROPE_BASE = 10000.0
AXIS_DIM = HEAD_DIM // 2
MIX_DIM = CONV_CH + DN_DIM + ATT_DIM

D_FF = 2816
N_EXPERTS = 8
TOP_K = 2
D_FF_EXPERT = 3584

TM = 256
FF_CHUNK = 256
MOE_TM = 512
MOE_TF = 1792
MOE_SUB = 256
LANE = 128
VMEM_LIMIT = 56 * 1024 * 1024

_C_QKV = 3 * CONV_CH
_C_Z = _C_QKV + 3 * DN_DIM
_C_A = _C_Z + DN_DIM
_C_Q = _C_A + 4 * DN_HEADS
_C_K = _C_Q + ATT_DIM
_C_V = _C_K + ATT_KV_DIM
_C_END = _C_V + ATT_KV_DIM
N_AB = 4 * DN_HEADS


def _params(sem=None, vmem=None):
    kw = {}
    if sem is not None:
        kw["dimension_semantics"] = sem
    if vmem is not None:
        kw["vmem_limit_bytes"] = vmem
    return pltpu.CompilerParams(**kw)


def _split2(a):
    hi = a.astype(BF16)
    lo = (a - hi.astype(F32)).astype(BF16)
    return hi, lo


def _split3(a):
    hi = a.astype(BF16)
    r = a - hi.astype(F32)
    mid = r.astype(BF16)
    lo = (r - mid.astype(F32)).astype(BF16)
    return hi, mid, lo


_NN = (((1,), (0,)), ((), ()))
_NT = (((1,), (1,)), ((), ()))
_TN = (((0,), (0,)), ((), ()))


def _mm(a, b, dims=_NN):
    return lax.dot_general(a.astype(BF16), b.astype(BF16), dims, preferred_element_type=F32)


def _mm3(a, b, dims=_NN):
    a1, a0 = _split2(a)
    b1, b0 = _split2(b)
    d = functools.partial(lax.dot_general, dimension_numbers=dims, preferred_element_type=F32)
    return d(a1, b1) + (d(a1, b0) + d(a0, b1))


_BNN = (((2,), (1,)), ((0,), (0,)))
_BNT = (((2,), (2,)), ((0,), (0,)))
_BTN = (((1,), (1,)), ((0,), (0,)))


def _bmm(a, b, dims=_BNN):
    return lax.dot_general(a.astype(BF16), b.astype(BF16), dims, preferred_element_type=F32)


def _sigmoid(x):
    return 1.0 / (1.0 + jnp.exp(-x))


def _softplus(x):
    return jnp.maximum(x, 0.0) + jnp.log1p(jnp.exp(-jnp.abs(x)))


def _mod_row(mod_ref, is_ctx):
    mod = mod_ref[...]
    return jnp.where(is_ctx, mod[1:2, :], mod[0:1, :])


def _rmsnorm_rows(x):
    return x * lax.rsqrt(jnp.mean(x * x, axis=-1, keepdims=True) + EPS)


def _shift_rows(u, prow, nrow):
    n = u.shape[0]
    rid = lax.broadcasted_iota(jnp.int32, u.shape, 0)
    up = jnp.where(rid == 0, prow, pltpu.roll(u, 1, 0))
    un = jnp.where(rid == n - 1, nrow, pltpu.roll(u, n - 1, 0))
    return up, un


def _same_group(shape, group):
    sh = group.bit_length() - 1
    assert 1 << sh == group
    return (lax.broadcasted_iota(jnp.int32, shape, 0) >> sh) == (lax.broadcasted_iota(jnp.int32, shape, 1) >> sh)


def _head_blockdiag(n, group):
    g = jnp.arange(n, dtype=jnp.int32) // group
    return (g[:, None] == g[None, :]).astype(BF16)


def _group_sum(t, bd):
    return _mm(t, bd)


MOD_TN = 1536


def _mod_kernel(s_ref, w_ref, b_ref, o_ref):
    s = s_ref[...]
    s = s * _sigmoid(s)
    o_ref[0] = _mm3(s, w_ref[0]) + b_ref[0]


def _mod_vectors(c, c_ctx, w_mod, b_mod):
    depth = w_mod.shape[0]
    s = jnp.zeros((8, D), F32).at[0].set(c[0]).at[1].set(c_ctx)
    return pl.pallas_call(
        _mod_kernel,
        grid=(depth, N_MOD * D // MOD_TN),
        in_specs=[
            pl.BlockSpec((8, D), lambda l, j: (0, 0)),
            pl.BlockSpec((1, D, MOD_TN), lambda l, j: (l, 0, j)),
            pl.BlockSpec((1, 1, MOD_TN), lambda l, j: (l, 0, j)),
        ],
        out_specs=pl.BlockSpec((1, 8, MOD_TN), lambda l, j: (l, 0, j)),
        out_shape=jax.ShapeDtypeStruct((depth, 8, N_MOD * D), F32),
        compiler_params=_params(("arbitrary", "arbitrary"), VMEM_LIMIT),
        name="mod_vectors",
    )(s, w_mod, b_mod.reshape(depth, 1, N_MOD * D))


def _halo_valid(i, nblk):
    return jnp.logical_and(i != 0, i != 1), jnp.logical_and(i != 0, i != nblk - 1)


def _halo_rows(prev_ref, next_ref, i, nblk):
    pvalid, nvalid = _halo_valid(i, nblk)
    prow = jnp.where(pvalid, prev_ref[7:8, :], 0.0)
    nrow = jnp.where(nvalid, next_ref[0:1, :], 0.0)
    return prow, nrow


def _in_kernel(ctx_ref, h_ref, hprev_ref, hnext_ref, mod_ref, w_ref, cw_ref, alog_ref, dtb_ref, bd_ref,
               pconv_ref, pz_ref, pq_ref, pkv_ref, q_ref, k_ref, v_ref, gb_ref, *, nblk):
    i = pl.program_id(0)
    row = _mod_row(mod_ref, i == 0)
    norm_mod = lambda x: _rmsnorm_rows(x) * (1.0 + row[:, D:2 * D]) + row[:, 0:D]
    h1 = norm_mod(jnp.where(i == 0, ctx_ref[...], h_ref[...])).astype(BF16)
    halo = norm_mod(jnp.concatenate([hprev_ref[...], hnext_ref[...]], axis=0)).astype(BF16)
    d = functools.partial(lax.dot_general, dimension_numbers=_NN, preferred_element_type=F32)
    tm = h1.shape[0]
    c0 = 3 * CONV_CH
    c1 = c0 + 3 * DN_DIM
    c2 = c1 + LANE
    c3 = c2 + DN_DIM + ATT_DIM
    pconv_ref[...] = d(h1, w_ref[:, 0:c0])
    zq = d(h1, w_ref[:, c2:c3])
    pz_ref[...] = zq[:, 0:DN_DIM]
    pq_ref[...] = zq[:, DN_DIM:]
    pkv_ref[...] = d(h1, w_ref[:, c3:c3 + 2 * ATT_KV_DIM])
    qkv_ab = d(jnp.concatenate([h1, halo], axis=0), w_ref[:, c0:c2])
    qkv = qkv_ab[:, 0:3 * DN_DIM]
    ab = qkv_ab[0:tm, 3 * DN_DIM:]

    u = qkv[0:tm, :]
    pvalid, nvalid = _halo_valid(i, nblk)
    prow = jnp.where(pvalid, qkv[tm + 7:tm + 8, :], 0.0)
    nrow = jnp.where(nvalid, qkv[tm + 8:tm + 9, :], 0.0)
    up, un = _shift_rows(u, prow, nrow)
    cw = cw_ref[...]
    y = up * cw[0:1, :] + u * cw[1:2, :] + un * cw[2:3, :]
    y = y * _sigmoid(y)
    q = y[:, 0:DN_DIM]
    k = y[:, DN_DIM:2 * DN_DIM]
    bd = bd_ref[...]
    q_ref[...] = q * lax.rsqrt(_group_sum(q * q, bd) + 1e-6) * (DN_HEAD_DIM ** -0.5)
    k_ref[...] = k * lax.rsqrt(_group_sum(k * k, bd) + 1e-6)
    v_ref[...] = y[:, 2 * DN_DIM:3 * DN_DIM]
    g = -jnp.exp(alog_ref[...]) * _softplus(ab + dtb_ref[...])
    lane = lax.broadcasted_iota(jnp.int32, ab.shape, 1)
    gb_ref[...] = jnp.where(lane < 2 * DN_HEADS, g, _sigmoid(ab))


def _in_proj(ctx_src, lat_src, lat_blk0, mod, w_main, dn_conv_w, a_log, dt_bias, bd):
    r8 = TM // 8
    last8 = lat_src.shape[0] // 8 - 1
    nblk = 1 + lat_src.shape[0] // TM - lat_blk0
    t = nblk * TM
    alog = jnp.zeros((1, LANE), F32).at[0, :2 * DN_HEADS].set(a_log.reshape(-1))
    dtb = jnp.zeros((1, LANE), F32).at[0, :2 * DN_HEADS].set(dt_bias.reshape(-1))
    widths = (3 * CONV_CH, DN_DIM, ATT_DIM, 2 * ATT_KV_DIM, DN_DIM, DN_DIM, DN_DIM, LANE)
    const = lambda i: (0, 0)
    lat = lambda i: jnp.maximum(i - 1, 0) + lat_blk0
    return pl.pallas_call(
        functools.partial(_in_kernel, nblk=nblk),
        grid=(nblk,),
        in_specs=[
            pl.BlockSpec((TM, D), const),
            pl.BlockSpec((TM, D), lambda i: (lat(i), 0)),
            pl.BlockSpec((8, D), lambda i: (jnp.maximum(lat(i) * r8 - 1, 0), 0)),
            pl.BlockSpec((8, D), lambda i: (jnp.minimum((lat(i) + 1) * r8, last8), 0)),
            pl.BlockSpec((8, N_MOD * D), const),
            pl.BlockSpec(w_main.shape, const),
            pl.BlockSpec((3, 3 * DN_DIM), const),
            pl.BlockSpec((1, LANE), const),
            pl.BlockSpec((1, LANE), const),
            pl.BlockSpec(bd.shape, const),
        ],
        out_specs=[pl.BlockSpec((TM, w), lambda i: (i, 0)) for w in widths],
        out_shape=[jax.ShapeDtypeStruct((t, w), F32) for w in widths],
        compiler_params=_params(("arbitrary",), VMEM_LIMIT),
        name="in_proj",
    )(ctx_src, lat_src, lat_src, lat_src, mod, w_main, dn_conv_w, alog, dtb, bd)


def _dn_chunk(rev, q_ref, k_ref, v_ref, gb_ref, o_ref, s_ref):
    c_ = DN_CHUNK
    ri = lax.broadcasted_iota(jnp.int32, (c_, c_), 0)
    ci = lax.broadcasted_iota(jnp.int32, (c_, c_), 1)
    incl = (ri <= ci) if rev else (ri >= ci)
    strict = (ri < ci) if rev else (ri > ci)
    same_sub = _same_group((c_, c_), DN_SUB)
    eye = jnp.where(ri == ci, 1.0, 0.0)
    tri = jnp.where(incl, 1.0, 0.0).astype(BF16)
    last = 0 if rev else c_ - 1

    nchunks = q_ref.shape[0] // c_
    nh = DN_HEADS
    col0 = nh if rev else 0
    rows = lambda g: slice(g * c_, (g + 1) * c_)
    lanes = lambda h: slice(h * DN_HEAD_DIM, (h + 1) * DN_HEAD_DIM)

    def stack(fn):
        return jnp.stack([fn(g, h) for g in range(nchunks) for h in range(nh)])

    gb = gb_ref[...]
    gcs = [_cumsum_rows(tri, gb[rows(g), :]) for g in range(nchunks)]
    gcts = [gc.T for gc in gcs]
    q = stack(lambda g, h: q_ref[rows(g), lanes(h)])
    k = stack(lambda g, h: k_ref[rows(g), lanes(h)])
    v = stack(lambda g, h: v_ref[rows(g), lanes(h)])
    gcol = stack(lambda g, h: gcs[g][:, col0 + h:col0 + h + 1])
    grow = stack(lambda g, h: gcts[g][col0 + h:col0 + h + 1, :])
    beta = stack(lambda g, h: gb[rows(g), 2 * nh + col0 + h:2 * nh + col0 + h + 1])
    glast = gcol[:, last:last + 1, :]
    decay = jnp.where(incl, jnp.exp(jnp.where(incl, gcol - grow, 0.0)), 0.0)
    eg = jnp.exp(gcol)
    kb = k * beta
    a = jnp.where(strict, _bmm(kb, k, _BNT) * decay, 0.0)
    qk = jnp.where(incl, _bmm(q, k, _BNT) * decay, 0.0)
    ad = jnp.where(same_sub, a, 0.0)
    ao = a - ad
    p = eye - ad
    n2 = _bmm(ad, ad)
    p = p + _bmm(p, n2)
    n4 = _bmm(n2, n2)
    p = p + _bmm(p, n4)
    n8 = _bmm(n4, n4)
    dinv = p + _bmm(p, n8)
    m = _bmm(dinv, ao)
    m2 = _bmm(m, m)
    y = _bmm(dinv, jnp.concatenate([v * beta, kb * eg], axis=-1))
    z = y + _bmm(m2, y)
    x = z - _bmm(m, z)
    u = x[:, :, :DN_HEAD_DIM]
    w = x[:, :, DN_HEAD_DIM:]
    qg = q * eg
    kd = k * jnp.exp(glast - gcol)
    gl = jnp.exp(glast)
    s = s_ref[col0:col0 + nh]
    for g in (reversed(range(nchunks)) if rev else range(nchunks)):
        b = slice(g * nh, (g + 1) * nh)
        v_new = u[b] - _bmm(w[b], s)
        o = _bmm(qg[b], s) + _bmm(qk[b], v_new)
        s = s * gl[b] + _bmm(kd[b], v_new, _BTN)
        for h in range(nh):
            o_ref[rows(g), lanes(h)] = o[h]
    s_ref[col0:col0 + nh] = s


def _cumsum_rows(tri_bf16, g):
    g2, g1, g0 = _split3(g)
    d = functools.partial(lax.dot_general, dimension_numbers=_NN, preferred_element_type=F32)
    return d(tri_bf16, g2) + (d(tri_bf16, g1) + d(tri_bf16, g0))


def _dn_kernel(qf, kf, vf, gf, qb, kb, vb, gbb, *rest, n_cast):
    cast_in = rest[:n_cast]
    of_ref, ob_ref = rest[n_cast:n_cast + 2]
    cast_out = rest[n_cast + 2:2 * n_cast + 2]
    s_ref = rest[-1]

    @pl.when(pl.program_id(0) == 0)
    def _():
        s_ref[...] = jnp.zeros(s_ref.shape, F32)

    _dn_chunk(False, qf, kf, vf, gf, of_ref, s_ref)
    _dn_chunk(True, qb, kb, vb, gbb, ob_ref, s_ref)
    for src, dst in zip(cast_in, cast_out):
        dst[...] = src[...].astype(BF16)


def _delta_net(q, k, v, gb, ctx_len, to_bf16=()):
    t = q.shape[0]
    rows = DN_STEP_CHUNKS * DN_CHUNK
    assert ctx_len == rows and t % rows == 0
    nstep = t // rows
    cast_specs = []
    for m in to_bf16:
        rb = -(-m.shape[0] // nstep)
        rb = -(-rb // 16) * 16
        nb = -(-m.shape[0] // rb)
        cast_specs.append(pl.BlockSpec((rb, m.shape[1]), lambda s, nb=nb: (jnp.minimum(s, nb - 1), 0)))

    def fwd(s):
        return (s, 0)

    def bwd(s):
        return (jnp.where(s == 0, 0, nstep - s), 0)

    wide = lambda im: pl.BlockSpec((rows, DN_DIM), im)
    narrow = lambda im: pl.BlockSpec((rows, LANE), im)
    return pl.pallas_call(
        functools.partial(_dn_kernel, n_cast=len(to_bf16)),
        grid=(nstep,),
        in_specs=[wide(fwd), wide(fwd), wide(fwd), narrow(fwd), wide(bwd), wide(bwd), wide(bwd), narrow(bwd)]
        + cast_specs,
        out_specs=[wide(fwd), wide(bwd)] + cast_specs,
        out_shape=[jax.ShapeDtypeStruct((t, DN_DIM), F32)] * 2
        + [jax.ShapeDtypeStruct(m.shape, BF16) for m in to_bf16],
        scratch_shapes=[pltpu.VMEM((2 * DN_HEADS, DN_HEAD_DIM, DN_HEAD_DIM), F32)],
        compiler_params=_params(("arbitrary",), VMEM_LIMIT),
        name="delta_net",
    )(q, k, v, gb, q, k, v, gb, *to_bf16)


def _rope(x, cos, sin):
    w = x.shape[1]
    lane = lax.broadcasted_iota(jnp.int32, x.shape, 1)
    first_half = (lane & (AXIS_DIM - 1)) < (AXIS_DIM // 2)
    swapped = jnp.where(first_half, pltpu.roll(x, w - AXIS_DIM // 2, 1), pltpu.roll(x, AXIS_DIM // 2, 1))
    return x * cos + swapped * sin


def _softmax_av(s, sink, vals):
    m = jnp.maximum(jnp.max(s, axis=-1, keepdims=True), sink)
    p = jnp.exp(s - m)
    denom = jnp.sum(p, axis=-1, keepdims=True) + jnp.exp(sink - m)
    return _mm(p, vals) / denom


def _attend(q, keys, vals, valid, sink_all, o_ref, row0=0):
    b = q.shape[0]
    for kvh in range(ATT_KV_HEADS):
        kl = slice(kvh * HEAD_DIM, (kvh + 1) * HEAD_DIM)
        heads = range(kvh * ATT_GROUP, (kvh + 1) * ATT_GROUP)
        qs = jnp.concatenate([q[:, h * HEAD_DIM:(h + 1) * HEAD_DIM] for h in heads], axis=0)
        sink = jnp.concatenate([jnp.broadcast_to(sink_all[0:1, h:h + 1], (b, 1)) for h in heads], axis=0)
        s = _mm(qs, keys[:, kl], _NT)
        if valid is not None:
            s = jnp.where(valid, s, NEG)
        o = _softmax_av(s, sink, vals[:, kl])
        for g, h in enumerate(heads):
            o_ref[row0:row0 + b, h * HEAD_DIM:(h + 1) * HEAD_DIM] = o[g * b:(g + 1) * b, :]


def _band_valid(nk, first, last):
    b = ATT_BLOCK
    c = lax.broadcasted_iota(jnp.int32, (1, nk), 1)
    far = 4 * nk
    ccol = jnp.where(c >= 3 * b, b + ((c - 3 * b) & (b - 1)), c)
    ccol = jnp.where(jnp.logical_and(c < b, first), -far, ccol)
    ccol = jnp.where(jnp.logical_and(jnp.logical_and(c >= 2 * b, c < 3 * b), last), far, ccol)
    r = lax.broadcasted_iota(jnp.int32, (ATT_GROUP * b, 1), 0) & (b - 1)
    return lax.bitcast_convert_type(ccol - r, jnp.uint32) <= jnp.uint32(2 * b)


def _rope_block(rowtab_ref, coltab_ref, blk):
    rt = rowtab_ref[blk]
    ct = coltab_ref[...]
    reps = ATT_BLOCK // GRID_W
    rows = jnp.concatenate([jnp.broadcast_to(rt[g:g + 1, :], (GRID_W, LANE)) for g in range(reps)], axis=0)
    return rows + jnp.concatenate([ct] * reps, axis=0)


def _attn_kernel(q_ref, kp_ref, kc_ref, kn_ref, kctx_ref, cosr_ref, sinr_ref, cosc_ref, sinc_ref, sink_ref, o_ref,
                 *, nb, with_ctx):
    j = pl.program_id(0)
    b = ATT_BLOCK
    scale = HEAD_DIM ** -0.5

    @pl.when(j == 0)
    def _():
        if with_ctx:
            kvx = kctx_ref[...]
            _attend(q_ref[...] * scale, kvx[:, :ATT_KV_DIM], kvx[:, ATT_KV_DIM:], None, sink_ref[...], o_ref)
        else:
            o_ref[...] = jnp.zeros(o_ref.shape, F32)

    @pl.when(j > 0)
    def _():
        b0 = 2 * (j - 1)
        blocks = (jnp.maximum(b0 - 1, 0), b0, b0 + 1, jnp.minimum(b0 + 2, nb - 1))
        cos = [_rope_block(cosr_ref, cosc_ref, blk) for blk in blocks]
        sin = [_rope_block(sinr_ref, sinc_ref, blk) for blk in blocks]
        kvc = kc_ref[...]
        kv = (kp_ref[...], kvc[0:b, :], kvc[b:2 * b, :], kn_ref[...])
        kvx = kctx_ref[...]
        keys = [_rope(t[:, :ATT_KV_DIM], c_, s_) for t, c_, s_ in zip(kv, cos, sin)]
        q_all = q_ref[...]
        sink_all = sink_ref[...]
        nk = 3 * b + kvx.shape[0]
        for sub in range(2):
            q = q_all[sub * b:(sub + 1) * b, :]
            q = jnp.concatenate([_rope(q[:, l * LANE:(l + 1) * LANE], cos[1 + sub], sin[1 + sub])
                                 for l in range(ATT_DIM // LANE)], axis=1)
            kcat = jnp.concatenate(keys[sub:sub + 3] + [kvx[:, :ATT_KV_DIM]], axis=0)
            vcat = jnp.concatenate([t[:, ATT_KV_DIM:] for t in kv[sub:sub + 3]] + [kvx[:, ATT_KV_DIM:]], axis=0)
            valid = _band_valid(nk, first=(b0 + sub == 0), last=(b0 + sub == nb - 1))
            _attend(q * scale, kcat, vcat, valid, sink_all, o_ref, row0=sub * b)


def _attention(pq, pkv, rope, sink, ctx_len, with_ctx):
    t = pq.shape[0]
    n = t - ctx_len
    nb = n // ATT_BLOCK
    step = 2 * ATT_BLOCK
    assert ctx_len == step and nb % 2 == 0
    off = ctx_len // ATT_BLOCK
    half = lambda im: pl.BlockSpec((ATT_BLOCK, 2 * ATT_KV_DIM), im)
    lat = lambda j: 2 * jnp.maximum(j - 1, 0)
    prv = lambda j: (jnp.maximum(lat(j) - 1, 0) + off, 0)
    nxt = lambda j: (jnp.minimum(lat(j) + 2, nb - 1) + off, 0)
    cur = lambda j: (j, 0)
    whole = lambda a: pl.BlockSpec(a.shape, lambda j: (0,) * a.ndim)
    return pl.pallas_call(
        functools.partial(_attn_kernel, nb=nb, with_ctx=with_ctx),
        grid=(1 + nb // 2,),
        in_specs=[
            pl.BlockSpec((step, ATT_DIM), cur),
            half(prv), pl.BlockSpec((step, 2 * ATT_KV_DIM), cur), half(nxt),
            pl.BlockSpec((ctx_len, 2 * ATT_KV_DIM), lambda j: (0, 0)),
        ] + [whole(a) for a in rope] + [pl.BlockSpec((8, LANE), lambda j: (0, 0))],
        out_specs=pl.BlockSpec((step, ATT_DIM), cur),
        out_shape=jax.ShapeDtypeStruct((t, ATT_DIM), F32),
        compiler_params=_params(("arbitrary",), VMEM_LIMIT),
        name="attention",
    )(pq, pkv, pkv, pkv, pkv, *rope, sink)


def _mixfin_kernel(ctx_ref, h_ref, mod_ref, pconv_ref, prev_ref, next_ref, cw_ref, of_ref, ob_ref, z_ref, ng_ref,
                   yc_ref, wout_ref, bd_ref, *rest, nblk, blk0, with_router):
    if with_router:
        wr1_ref, wr0_ref, x_ref, hx_ref, lg_ref = rest
    else:
        x_ref, hx_ref = rest
    i = pl.program_id(0) + blk0
    row = _mod_row(mod_ref, i == 0)
    pc = pconv_ref[...]
    u = pc[:, CONV_CH:2 * CONV_CH] * pc[:, 2 * CONV_CH:]
    prow, nrow = _halo_rows(prev_ref, next_ref, i, nblk)
    prow = prow[:, CONV_CH:2 * CONV_CH] * prow[:, 2 * CONV_CH:]
    nrow = nrow[:, CONV_CH:2 * CONV_CH] * nrow[:, 2 * CONV_CH:]
    up, un = _shift_rows(u, prow, nrow)
    cw = cw_ref[...]
    ya = pc[:, :CONV_CH] * (up * cw[0:1, :] + u * cw[1:2, :] + un * cw[2:3, :])
    o = of_ref[...] + ob_ref[...]
    ms = _group_sum(o * o, bd_ref[...]) * (1.0 / DN_HEAD_DIM)
    z = z_ref[...]
    yb = o * lax.rsqrt(ms + EPS) * ng_ref[...] * (z * _sigmoid(z))
    mix = jnp.concatenate([ya, yb, yc_ref[...]], axis=1)
    x = jnp.where(i == 0, ctx_ref[...], h_ref[...]) + row[:, 2 * D:3 * D] * _mm(mix, wout_ref[...])
    x_ref[...] = x
    hx = _rmsnorm_rows(x) * (1.0 + row[:, 4 * D:5 * D]) + row[:, 3 * D:4 * D]
    hx_ref[...] = hx.astype(hx_ref.dtype)
    if with_router:
        h1, h0 = _split2(hx)
        d = functools.partial(lax.dot_general, dimension_numbers=_NN, preferred_element_type=F32)
        lg = d(h1, wr1_ref[...]) + (d(h0, wr1_ref[...]) + d(h1, wr0_ref[...]))
        lane = lax.broadcasted_iota(jnp.int32, lg.shape, 1)
        lanef = lane.astype(F32)
        lg = jnp.where(lane < N_EXPERTS, lg, -jnp.inf)
        m1 = jnp.max(lg, axis=-1, keepdims=True)
        i1 = jnp.min(jnp.where(lg == m1, lanef, float(LANE)), axis=-1, keepdims=True)
        rest = jnp.where(lanef == i1, -jnp.inf, lg)
        m2 = jnp.max(rest, axis=-1, keepdims=True)
        i2 = jnp.min(jnp.where(rest == m2, lanef, float(LANE)), axis=-1, keepdims=True)
        e2 = jnp.exp(m2 - m1)
        g1 = 1.0 / (1.0 + e2)
        lg_ref[...] = jnp.where(lane == 0, g1, jnp.where(lane == 1, e2 * g1, jnp.where(lane == 2, i1, i2)))


def _mixer_finish(ctx_src, lat_src, lat_blk0, mod, pconv, conv_w, o_f, o_b, pz, norm_g, yc, w_out, bd, ctx_len,
                  with_ctx, router=None):
    t = pconv.shape[0]
    nblk = t // TM
    blk0 = 0 if with_ctx else ctx_len // TM
    rows = t - blk0 * TM
    r8 = TM // 8
    w = pconv.shape[1]
    cur = lambda i: (i + blk0, 0)
    out_cur = lambda i: (i, 0)
    const = lambda i: (0, 0)
    ng = jnp.tile(norm_g.reshape(1, DN_HEAD_DIM), (1, DN_HEADS))
    in_specs = [
        pl.BlockSpec((TM, D), const),
        pl.BlockSpec((TM, D), lambda i: (jnp.maximum(i + blk0 - 1, 0) + lat_blk0, 0)),
        pl.BlockSpec((8, N_MOD * D), const),
        pl.BlockSpec((TM, w), cur),
        pl.BlockSpec((8, w), lambda i: (jnp.maximum((i + blk0) * r8 - 1, 0), 0)),
        pl.BlockSpec((8, w), lambda i: (jnp.minimum((i + blk0 + 1) * r8, t // 8 - 1), 0)),
        pl.BlockSpec((3, CONV_CH), const),
        pl.BlockSpec((TM, DN_DIM), cur),
        pl.BlockSpec((TM, DN_DIM), cur),
        pl.BlockSpec((TM, DN_DIM), cur),
        pl.BlockSpec((1, DN_DIM), const),
        pl.BlockSpec((TM, ATT_DIM), cur),
        pl.BlockSpec((MIX_DIM, D), const),
        pl.BlockSpec(bd.shape, const),
    ]
    args = [ctx_src, lat_src, mod, pconv, pconv, pconv, conv_w, o_f, o_b, pz, ng, yc, w_out, bd]
    out_specs = [pl.BlockSpec((TM, D), out_cur), pl.BlockSpec((TM, D), out_cur)]
    hx_dtype = BF16 if router is None else F32
    out_shape = [jax.ShapeDtypeStruct((rows, D), F32), jax.ShapeDtypeStruct((rows, D), hx_dtype)]
    if router is not None:
        in_specs += [pl.BlockSpec((D, LANE), const)] * 2
        args += list(router)
        out_specs.append(pl.BlockSpec((TM, LANE), out_cur))
        out_shape.append(jax.ShapeDtypeStruct((rows, LANE), F32))
    return pl.pallas_call(
        functools.partial(_mixfin_kernel, nblk=nblk, blk0=blk0, with_router=router is not None),
        grid=(rows // TM,),
        in_specs=in_specs,
        out_specs=out_specs,
        out_shape=out_shape,
        compiler_params=_params(("arbitrary",), VMEM_LIMIT),
        name="mixer_finish",
    )(*args)


def _ffn_kernel(hx_ref, x_ref, mod_ref, wg_ref, wu_ref, wd_ref, o_ref):
    row = _mod_row(mod_ref, pl.program_id(0) == 0)
    hx = hx_ref[...]
    acc = jnp.zeros((hx.shape[0], D), F32)
    for f in range(0, D_FF, FF_CHUNK):
        g = _mm(hx, wg_ref[:, f:f + FF_CHUNK])
        u = _mm(hx, wu_ref[:, f:f + FF_CHUNK])
        acc = acc + _mm(g * _sigmoid(g) * u, wd_ref[f:f + FF_CHUNK, :])
    o_ref[...] = x_ref[...] + row[:, 5 * D:6 * D] * acc


def _dense_ffn(hx, x, mod, wg, wu, wd):
    t = x.shape[0]
    const = lambda i: (0, 0)
    once = dict(pipeline_mode=pl.Buffered(1))
    return pl.pallas_call(
        _ffn_kernel,
        grid=(t // TM,),
        in_specs=[
            pl.BlockSpec((TM, D), lambda i: (i, 0)),
            pl.BlockSpec((TM, D), lambda i: (i, 0)),
            pl.BlockSpec((8, N_MOD * D), const),
            pl.BlockSpec((D, D_FF), const, **once),
            pl.BlockSpec((D, D_FF), const, **once),
            pl.BlockSpec((D_FF, D), const, **once),
        ],
        out_specs=pl.BlockSpec((TM, D), lambda i: (i, 0)),
        out_shape=jax.ShapeDtypeStruct((t, D), F32),
        compiler_params=_params(("arbitrary",), VMEM_LIMIT),
        name="dense_ffn",
    )(hx, x, mod, wg, wu, wd)


def _moe_kernel(be_ref, nu_ref, xs_ref, wg_ref, wu_ref, wd_ref, y_ref, acc_ref):
    b = pl.program_id(0)
    f = pl.program_id(1)

    @pl.when(b < nu_ref[0])
    def _():
        xs = xs_ref[...].astype(BF16)
        part = jnp.zeros((MOE_TM, D), F32)
        for c in range(0, MOE_TF, MOE_SUB):
            g = _mm(xs, wg_ref[0, :, c:c + MOE_SUB])
            u = _mm(xs, wu_ref[0, :, c:c + MOE_SUB])
            part = part + _mm(g * _sigmoid(g) * u, wd_ref[0, c:c + MOE_SUB, :])

        @pl.when(f == 0)
        def _():
            acc_ref[...] = part

        @pl.when(f != 0)
        def _():
            acc_ref[...] += part

        @pl.when(f == pl.num_programs(1) - 1)
        def _():
            y_ref[...] = acc_ref[...]

    @pl.when(b >= nu_ref[0])
    def _():
        y_ref[...] = jnp.zeros(y_ref.shape, F32)


def _moe_experts(xs, blk_e, n_used, wg, wu, wd):
    cap = xs.shape[0]
    nblk = cap // MOE_TM
    nf = D_FF_EXPERT // MOE_TF

    def fidx(b, f, nu):
        return jnp.where(b < nu[0], f, nf - 1)

    grid_spec = pltpu.PrefetchScalarGridSpec(
        num_scalar_prefetch=2,
        grid=(nblk, nf),
        in_specs=[
            pl.BlockSpec((MOE_TM, D), lambda b, f, be, nu: (b, 0)),
            pl.BlockSpec((1, D, MOE_TF), lambda b, f, be, nu: (be[b], 0, fidx(b, f, nu))),
            pl.BlockSpec((1, D, MOE_TF), lambda b, f, be, nu: (be[b], 0, fidx(b, f, nu))),
            pl.BlockSpec((1, MOE_TF, D), lambda b, f, be, nu: (be[b], fidx(b, f, nu), 0)),
        ],
        out_specs=pl.BlockSpec((MOE_TM, D), lambda b, f, be, nu: (b, 0)),
        scratch_shapes=[pltpu.VMEM((MOE_TM, D), F32)],
    )
    return pl.pallas_call(
        _moe_kernel,
        grid_spec=grid_spec,
        out_shape=jax.ShapeDtypeStruct((cap, D), F32),
        compiler_params=_params(("arbitrary", "arbitrary"), VMEM_LIMIT),
        name="moe_experts",
    )(blk_e, n_used, xs, wg, wu, wd)


def _moe_route(top_e):
    n = top_e.shape[0]
    a = n * TOP_K
    flat_e = top_e.reshape(a)
    onehot = (flat_e[:, None] == jnp.arange(N_EXPERTS, dtype=flat_e.dtype)[None, :]).astype(jnp.int32)
    counts = jnp.sum(onehot, axis=0)
    padded = (counts + MOE_TM - 1) // MOE_TM * MOE_TM
    pad_ends = jnp.cumsum(padded)
    pad_starts = pad_ends - padded
    dest = jnp.sum(onehot * (jnp.cumsum(onehot, axis=0) - onehot + pad_starts[None, :]), axis=1)
    cap = a + N_EXPERTS * MOE_TM
    nblk = cap // MOE_TM
    row_tok = (jnp.arange(cap, dtype=jnp.int32) % n).at[dest].set(
        jnp.arange(a, dtype=jnp.int32) // TOP_K, unique_indices=True, mode="promise_in_bounds")
    blk_start = jnp.arange(nblk, dtype=jnp.int32) * MOE_TM
    blk_e = jnp.minimum(jnp.sum((pad_ends[None, :] <= blk_start[:, None]).astype(jnp.int32), axis=1),
                        N_EXPERTS - 1)
    n_used = (pad_ends[-1] // MOE_TM).astype(jnp.int32).reshape(1)
    last_e = blk_e[jnp.maximum(n_used[0] - 1, 0)]
    blk_e = jnp.where(jnp.arange(nblk) < n_used[0], blk_e, last_e)
    return dest.reshape(n, TOP_K), row_tok, blk_e, n_used


def _final_kernel(x_ref, y0_ref, y1_ref, gt_ref, mod_ref, fg_ref, o_ref):
    mod = mod_ref[...]
    gt = gt_ref[...]
    f = gt[:, 0:1] * y0_ref[...] + gt[:, 1:2] * y1_ref[...]
    x = x_ref[...] + mod[0:1, 5 * D:6 * D] * f
    o_ref[...] = _rmsnorm_rows(x) * fg_ref[...]


def _moe_combine_final(x, y0, y1, gt, mod, final_g):
    n = x.shape[0]
    row = lambda i: (i, 0)
    const = lambda i: (0, 0)
    return pl.pallas_call(
        _final_kernel,
        grid=(n // TM,),
        in_specs=[pl.BlockSpec((TM, D), row)] * 3
        + [pl.BlockSpec((TM, LANE), row), pl.BlockSpec((8, N_MOD * D), const), pl.BlockSpec((1, D), const)],
        out_specs=pl.BlockSpec((TM, D), row),
        out_shape=jax.ShapeDtypeStruct((n, D), F32),
        compiler_params=_params(("arbitrary",), VMEM_LIMIT),
        name="moe_combine_final",
    )(x, y0, y1, gt, mod, final_g.reshape(1, D))


def _rope_tables(n):
    lane = jnp.arange(LANE, dtype=jnp.int32) % HEAD_DIM
    inv = ROPE_BASE ** (-jnp.arange(0, AXIS_DIM, 2, dtype=F32) / AXIS_DIM)
    freq = inv[lane % (AXIS_DIM // 2)]
    row_axis = (lane // AXIS_DIM) == 0
    sign = jnp.where((lane % AXIS_DIM) < AXIS_DIM // 2, -1.0, 1.0)
    reps = ATT_BLOCK // GRID_W

    def tables(count, on_axis):
        ang = jnp.arange(count, dtype=F32)[:, None] * freq[None, :]
        return jnp.where(on_axis, jnp.cos(ang), 0.0), jnp.where(on_axis, jnp.sin(ang) * sign, 0.0)

    by_block = lambda t: jnp.pad(t.reshape(-1, reps, LANE), ((0, 0), (0, 8 - reps), (0, 0)))
    cosr, sinr = tables(n // GRID_W, row_axis)
    cosc, sinc = tables(GRID_W, ~row_axis)
    return by_block(cosr), by_block(sinr), cosc, sinc


def _prep_w_in(w):
    pad = jnp.zeros((D, LANE - N_AB), w.dtype)
    return jnp.concatenate([w[:, :_C_Z], w[:, _C_A:_C_Q], pad, w[:, _C_Z:_C_A], w[:, _C_Q:_C_END]],
                           axis=1).astype(BF16)


def kernel(x, c, ctx, c_ctx, w_mod, b_mod, w_in, w_out, conv_w, dn_conv_w, dn_a_log, dn_dt_bias, dn_norm_g,
           attn_sink, ffn_w_gate, ffn_w_up, ffn_w_down, moe_router, moe_w_gate, moe_w_up, moe_w_down,
           final_norm_g):
    bsz, n, d = x.shape
    ctx_len = ctx.shape[1]
    depth = w_in.shape[0]
    assert bsz == 1 and d == D and ctx_len == TM and n % TM == 0 and n % GRID_W == 0
    rope = _rope_tables(n)
    mods = _mod_vectors(c, c_ctx, w_mod, b_mod)
    bd = _head_blockdiag(DN_DIM, DN_HEAD_DIM)
    stream = (ctx[0], x[0], 0)
    for layer in range(depth):
        last = layer == depth - 1
        mod = mods[layer]
        pconv, pz, pq, pkv, qn, kn, vv, gb = _in_proj(*stream, mod, _prep_w_in(w_in[layer]), dn_conv_w[layer],
                                                      dn_a_log[layer], dn_dt_bias[layer], bd)
        if (layer + 1) % 2 == 1 and layer + 1 < depth:
            j = (layer + 1) // 2
            to_bf16 = tuple(w[j].reshape(-1, w.shape[-1]) for w in (moe_w_gate, moe_w_up, moe_w_down))
        else:
            to_bf16 = ()
        o_f, o_b, *cast = _delta_net(qn, kn, vv, gb, ctx_len, to_bf16)
        if to_bf16:
            moe_bf16 = [c.reshape(w.shape[1:]) for c, w in zip(cast, (moe_w_gate, moe_w_up, moe_w_down))]
        sink = jnp.zeros((8, LANE), F32).at[0, :ATT_HEADS].set(attn_sink[layer])
        yc = _attention(pq, pkv, rope, sink, ctx_len, with_ctx=not last)
        router = None
        if layer % 2 == 1:
            wr = jnp.zeros((D, LANE), F32).at[:, :N_EXPERTS].set(moe_router[layer // 2])
            wr1 = wr.astype(BF16)
            router = (wr1, (wr - wr1.astype(F32)).astype(BF16))
        outs = _mixer_finish(*stream, mod, pconv, conv_w[layer], o_f, o_b, pz, dn_norm_g[layer], yc,
                             w_out[layer].astype(BF16), bd, ctx_len, with_ctx=not last, router=router)
        if layer % 2 == 0:
            assert not last
            x1, hx = outs
            i = layer // 2
            h = _dense_ffn(hx, x1, mod, ffn_w_gate[i], ffn_w_up[i], ffn_w_down[i])
            stream = (h, h, ctx_len // TM)
        else:
            assert last
            x1, hx, route = outs
            i = layer // 2
            dest, row_tok, blk_e, n_used = _moe_route(route[:, 2:2 + TOP_K].astype(jnp.int32))
            take = lambda rows_, idx: rows_.at[idx].get(mode="promise_in_bounds")
            xs = take(hx, row_tok)
            y = _moe_experts(xs, blk_e, n_used, *moe_bf16)
            y0 = take(y, dest[:, 0])
            y1 = take(y, dest[:, 1])
            h = _moe_combine_final(x1, y0, y1, route, mod, final_norm_g)
    return h.reshape(bsz, n, d)
```

```python
import functools

import jax
import jax.numpy as jnp
from jax import lax
from jax.experimental import pallas as pl
from jax.experimental.pallas import tpu as pltpu

F32 = jnp.float32
BF16 = jnp.bfloat16

D = 1024
N_MOD = 6
EPS = 1e-6
NEG = -1e30
GRID_W = 64

CONV_CH = 256
DN_HEADS = 6
DN_HEAD_DIM = 64
DN_DIM = DN_HEADS * DN_HEAD_DIM
DN_CHUNK = 64
DN_SUB = 16
DN_STEP_CHUNKS = 4
ATT_HEADS = 6
ATT_KV_HEADS = 2
ATT_GROUP = ATT_HEADS // ATT_KV_HEADS
HEAD_DIM = 64
ATT_DIM = ATT_HEADS * HEAD_DIM
ATT_KV_DIM = ATT_KV_HEADS * HEAD_DIM
ATT_BLOCK = 128
ROPE_BASE = 10000.0
AXIS_DIM = HEAD_DIM // 2
MIX_DIM = CONV_CH + DN_DIM + ATT_DIM

D_FF = 2816
N_EXPERTS = 8
TOP_K = 2
D_FF_EXPERT = 3584

TM = 256
FF_CHUNK = 512
MOE_TM = 512
MOE_TF = 1792
MOE_SUB = 256
LANE = 128
VMEM_LIMIT = 56 * 1024 * 1024

_C_QKV = 3 * CONV_CH
_C_Z = _C_QKV + 3 * DN_DIM
_C_A = _C_Z + DN_DIM
_C_Q = _C_A + 4 * DN_HEADS
_C_K = _C_Q + ATT_DIM
_C_V = _C_K + ATT_KV_DIM
_C_END = _C_V + ATT_KV_DIM
N_AB = 4 * DN_HEADS


def _params(sem=None, vmem=None):
    kw = {}
    if sem is not None:
        kw["dimension_semantics"] = sem
    if vmem is not None:
        kw["vmem_limit_bytes"] = vmem
    return pltpu.CompilerParams(**kw)


def _split2(a):
    hi = a.astype(BF16)
    lo = (a - hi.astype(F32)).astype(BF16)
    return hi, lo


def _split3(a):
    hi = a.astype(BF16)
    r = a - hi.astype(F32)
    mid = r.astype(BF16)
    lo = (r - mid.astype(F32)).astype(BF16)
    return hi, mid, lo


_NN = (((1,), (0,)), ((), ()))
_NT = (((1,), (1,)), ((), ()))
_TN = (((0,), (0,)), ((), ()))


def _mm(a, b, dims=_NN):
    return lax.dot_general(a.astype(BF16), b.astype(BF16), dims, preferred_element_type=F32)


def _mm3(a, b, dims=_NN):
    a1, a0 = _split2(a)
    b1, b0 = _split2(b)
    d = functools.partial(lax.dot_general, dimension_numbers=dims, preferred_element_type=F32)
    return d(a1, b1) + (d(a1, b0) + d(a0, b1))


_BNN = (((2,), (1,)), ((0,), (0,)))
_BNT = (((2,), (2,)), ((0,), (0,)))
_BTN = (((1,), (1,)), ((0,), (0,)))


def _bmm(a, b, dims=_BNN):
    return lax.dot_general(a.astype(BF16), b.astype(BF16), dims, preferred_element_type=F32)


def _sigmoid(x):
    return 1.0 / (1.0 + jnp.exp(-x))


def _softplus(x):
    return jnp.maximum(x, 0.0) + jnp.log1p(jnp.exp(-jnp.abs(x)))


def _mod_row(mod_ref, is_ctx):
    mod = mod_ref[...]
    return jnp.where(is_ctx, mod[1:2, :], mod[0:1, :])


def _rmsnorm_rows(x):
    return x * lax.rsqrt(jnp.mean(x * x, axis=-1, keepdims=True) + EPS)


def _shift_rows(u, prow, nrow):
    n = u.shape[0]
    rid = lax.broadcasted_iota(jnp.int32, u.shape, 0)
    up = jnp.where(rid == 0, prow, pltpu.roll(u, 1, 0))
    un = jnp.where(rid == n - 1, nrow, pltpu.roll(u, n - 1, 0))
    return up, un


def _same_group(shape, group):
    sh = group.bit_length() - 1
    assert 1 << sh == group
    return (lax.broadcasted_iota(jnp.int32, shape, 0) >> sh) == (lax.broadcasted_iota(jnp.int32, shape, 1) >> sh)


def _head_blockdiag(n, group):
    g = jnp.arange(n, dtype=jnp.int32) // group
    return (g[:, None] == g[None, :]).astype(BF16)


def _group_sum(t, bd):
    return _mm(t, bd)


MOD_TN = 1536


def _mod_kernel(s_ref, w_ref, b_ref, o_ref):
    s = s_ref[...]
    s = s * _sigmoid(s)
    o_ref[0] = _mm3(s, w_ref[0]) + b_ref[0]


def _mod_vectors(c, c_ctx, w_mod, b_mod):
    depth = w_mod.shape[0]
    s = jnp.zeros((8, D), F32).at[0].set(c[0]).at[1].set(c_ctx)
    return pl.pallas_call(
        _mod_kernel,
        grid=(depth, N_MOD * D // MOD_TN),
        in_specs=[
            pl.BlockSpec((8, D), lambda l, j: (0, 0)),
            pl.BlockSpec((1, D, MOD_TN), lambda l, j: (l, 0, j)),
            pl.BlockSpec((1, 1, MOD_TN), lambda l, j: (l, 0, j)),
        ],
        out_specs=pl.BlockSpec((1, 8, MOD_TN), lambda l, j: (l, 0, j)),
        out_shape=jax.ShapeDtypeStruct((depth, 8, N_MOD * D), F32),
        compiler_params=_params(("arbitrary", "arbitrary"), VMEM_LIMIT),
        name="mod_vectors",
    )(s, w_mod, b_mod.reshape(depth, 1, N_MOD * D))


def _halo_valid(i, nblk):
    return jnp.logical_and(i != 0, i != 1), jnp.logical_and(i != 0, i != nblk - 1)


def _halo_rows(prev_ref, next_ref, i, nblk):
    pvalid, nvalid = _halo_valid(i, nblk)
    prow = jnp.where(pvalid, prev_ref[7:8, :], 0.0)
    nrow = jnp.where(nvalid, next_ref[0:1, :], 0.0)
    return prow, nrow


def _in_kernel(ctx_ref, h_ref, hprev_ref, hnext_ref, mod_ref, w_ref, cw_ref, alog_ref, dtb_ref, bd_ref,
               pconv_ref, pz_ref, pq_ref, pkv_ref, q_ref, k_ref, v_ref, gb_ref, *, nblk):
    i = pl.program_id(0)
    row = _mod_row(mod_ref, i == 0)
    norm_mod = lambda x: _rmsnorm_rows(x) * (1.0 + row[:, D:2 * D]) + row[:, 0:D]
    h1 = norm_mod(jnp.where(i == 0, ctx_ref[...], h_ref[...])).astype(BF16)
    halo = norm_mod(jnp.concatenate([hprev_ref[...], hnext_ref[...]], axis=0)).astype(BF16)
    d = functools.partial(lax.dot_general, dimension_numbers=_NN, preferred_element_type=F32)
    tm = h1.shape[0]
    c0 = 3 * CONV_CH
    c1 = c0 + 3 * DN_DIM
    c2 = c1 + LANE
    c3 = c2 + DN_DIM + ATT_DIM
    pconv_ref[...] = d(h1, w_ref[:, 0:c0])
    zq = d(h1, w_ref[:, c2:c3])
    pz_ref[...] = zq[:, 0:DN_DIM]
    pq_ref[...] = zq[:, DN_DIM:]
    pkv_ref[...] = d(h1, w_ref[:, c3:c3 + 2 * ATT_KV_DIM])
    qkv_ab = d(jnp.concatenate([h1, halo], axis=0), w_ref[:, c0:c2])
    qkv = qkv_ab[:, 0:3 * DN_DIM]
    ab = qkv_ab[0:tm, 3 * DN_DIM:]

    u = qkv[0:tm, :]
    pvalid, nvalid = _halo_valid(i, nblk)
    prow = jnp.where(pvalid, qkv[tm + 7:tm + 8, :], 0.0)
    nrow = jnp.where(nvalid, qkv[tm + 8:tm + 9, :], 0.0)
    up, un = _shift_rows(u, prow, nrow)
    cw = cw_ref[...]
    y = up * cw[0:1, :] + u * cw[1:2, :] + un * cw[2:3, :]
    y = y * _sigmoid(y)
    q = y[:, 0:DN_DIM]
    k = y[:, DN_DIM:2 * DN_DIM]
    bd = bd_ref[...]
    q_ref[...] = q * lax.rsqrt(_group_sum(q * q, bd) + 1e-6) * (DN_HEAD_DIM ** -0.5)
    k_ref[...] = k * lax.rsqrt(_group_sum(k * k, bd) + 1e-6)
    v_ref[...] = y[:, 2 * DN_DIM:3 * DN_DIM]
    g = -jnp.exp(alog_ref[...]) * _softplus(ab + dtb_ref[...])
    lane = lax.broadcasted_iota(jnp.int32, ab.shape, 1)
    gb_ref[...] = jnp.where(lane < 2 * DN_HEADS, g, _sigmoid(ab))


def _in_proj(ctx_src, lat_src, lat_blk0, mod, w_main, dn_conv_w, a_log, dt_bias, bd):
    r8 = TM // 8
    last8 = lat_src.shape[0] // 8 - 1
    nblk = 1 + lat_src.shape[0] // TM - lat_blk0
    t = nblk * TM
    alog = jnp.zeros((1, LANE), F32).at[0, :2 * DN_HEADS].set(a_log.reshape(-1))
    dtb = jnp.zeros((1, LANE), F32).at[0, :2 * DN_HEADS].set(dt_bias.reshape(-1))
    widths = (3 * CONV_CH, DN_DIM, ATT_DIM, 2 * ATT_KV_DIM, DN_DIM, DN_DIM, DN_DIM, LANE)
    const = lambda i: (0, 0)
    lat = lambda i: jnp.maximum(i - 1, 0) + lat_blk0
    return pl.pallas_call(
        functools.partial(_in_kernel, nblk=nblk),
        grid=(nblk,),
        in_specs=[
            pl.BlockSpec((TM, D), const),
            pl.BlockSpec((TM, D), lambda i: (lat(i), 0)),
            pl.BlockSpec((8, D), lambda i: (jnp.maximum(lat(i) * r8 - 1, 0), 0)),
            pl.BlockSpec((8, D), lambda i: (jnp.minimum((lat(i) + 1) * r8, last8), 0)),
            pl.BlockSpec((8, N_MOD * D), const),
            pl.BlockSpec(w_main.shape, const),
            pl.BlockSpec((3, 3 * DN_DIM), const),
            pl.BlockSpec((1, LANE), const),
            pl.BlockSpec((1, LANE), const),
            pl.BlockSpec(bd.shape, const),
        ],
        out_specs=[pl.BlockSpec((TM, w), lambda i: (i, 0)) for w in widths],
        out_shape=[jax.ShapeDtypeStruct((t, w), F32) for w in widths],
        compiler_params=_params(("arbitrary",), VMEM_LIMIT),
        name="in_proj",
    )(ctx_src, lat_src, lat_src, lat_src, mod, w_main, dn_conv_w, alog, dtb, bd)


def _dn_chunk(rev, q_ref, k_ref, v_ref, gb_ref, o_ref, s_ref):
    c_ = DN_CHUNK
    ri = lax.broadcasted_iota(jnp.int32, (c_, c_), 0)
    ci = lax.broadcasted_iota(jnp.int32, (c_, c_), 1)
    incl = (ri <= ci) if rev else (ri >= ci)
    strict = (ri < ci) if rev else (ri > ci)
    same_sub = _same_group((c_, c_), DN_SUB)
    eye = jnp.where(ri == ci, 1.0, 0.0)
    tri = jnp.where(incl, 1.0, 0.0).astype(BF16)
    last = 0 if rev else c_ - 1

    nchunks = q_ref.shape[0] // c_
    nh = DN_HEADS
    col0 = nh if rev else 0
    rows = lambda g: slice(g * c_, (g + 1) * c_)
    lanes = lambda h: slice(h * DN_HEAD_DIM, (h + 1) * DN_HEAD_DIM)

    def stack(fn):
        return jnp.stack([fn(g, h) for g in range(nchunks) for h in range(nh)])

    gb = gb_ref[...]
    gcs = [_cumsum_rows(tri, gb[rows(g), :]) for g in range(nchunks)]
    gcts = [gc.T for gc in gcs]
    q = stack(lambda g, h: q_ref[rows(g), lanes(h)])
    k = stack(lambda g, h: k_ref[rows(g), lanes(h)])
    v = stack(lambda g, h: v_ref[rows(g), lanes(h)])
    gcol = stack(lambda g, h: gcs[g][:, col0 + h:col0 + h + 1])
    grow = stack(lambda g, h: gcts[g][col0 + h:col0 + h + 1, :])
    beta = stack(lambda g, h: gb[rows(g), 2 * nh + col0 + h:2 * nh + col0 + h + 1])
    glast = gcol[:, last:last + 1, :]
    decay = jnp.where(incl, jnp.exp(jnp.where(incl, gcol - grow, 0.0)), 0.0)
    eg = jnp.exp(gcol)
    kb = k * beta
    a = jnp.where(strict, _bmm(kb, k, _BNT) * decay, 0.0)
    qk = jnp.where(incl, _bmm(q, k, _BNT) * decay, 0.0)
    ad = jnp.where(same_sub, a, 0.0)
    ao = a - ad
    p = eye - ad
    n2 = _bmm(ad, ad)
    p = p + _bmm(p, n2)
    n4 = _bmm(n2, n2)
    p = p + _bmm(p, n4)
    n8 = _bmm(n4, n4)
    dinv = p + _bmm(p, n8)
    m = _bmm(dinv, ao)
    m2 = _bmm(m, m)
    y = _bmm(dinv, jnp.concatenate([v * beta, kb * eg], axis=-1))
    z = y + _bmm(m2, y)
    x = z - _bmm(m, z)
    u = x[:, :, :DN_HEAD_DIM]
    w = x[:, :, DN_HEAD_DIM:]
    qg = q * eg
    kd = k * jnp.exp(glast - gcol)
    gl = jnp.exp(glast)
    s = s_ref[col0:col0 + nh]
    for g in (reversed(range(nchunks)) if rev else range(nchunks)):
        b = slice(g * nh, (g + 1) * nh)
        v_new = u[b] - _bmm(w[b], s)
        o = _bmm(qg[b], s) + _bmm(qk[b], v_new)
        s = s * gl[b] + _bmm(kd[b], v_new, _BTN)
        for h in range(nh):
            o_ref[rows(g), lanes(h)] = o[h]
    s_ref[col0:col0 + nh] = s


def _cumsum_rows(tri_bf16, g):
    g2, g1, g0 = _split3(g)
    d = functools.partial(lax.dot_general, dimension_numbers=_NN, preferred_element_type=F32)
    return d(tri_bf16, g2) + (d(tri_bf16, g1) + d(tri_bf16, g0))


def _dn_kernel(qf, kf, vf, gf, qb, kb, vb, gbb, *rest, n_cast):
    cast_in = rest[:n_cast]
    of_ref, ob_ref = rest[n_cast:n_cast + 2]
    cast_out = rest[n_cast + 2:2 * n_cast + 2]
    s_ref = rest[-1]

    @pl.when(pl.program_id(0) == 0)
    def _():
        s_ref[...] = jnp.zeros(s_ref.shape, F32)

    _dn_chunk(False, qf, kf, vf, gf, of_ref, s_ref)
    _dn_chunk(True, qb, kb, vb, gbb, ob_ref, s_ref)
    for src, dst in zip(cast_in, cast_out):
        dst[...] = src[...].astype(BF16)


def _delta_net(q, k, v, gb, ctx_len, to_bf16=()):
    t = q.shape[0]
    rows = DN_STEP_CHUNKS * DN_CHUNK
    assert ctx_len == rows and t % rows == 0
    nstep = t // rows
    cast_specs = []
    for m in to_bf16:
        rb = -(-m.shape[0] // nstep)
        rb = -(-rb // 16) * 16
        nb = -(-m.shape[0] // rb)
        cast_specs.append(pl.BlockSpec((rb, m.shape[1]), lambda s, nb=nb: (jnp.minimum(s, nb - 1), 0)))

    def fwd(s):
        return (s, 0)

    def bwd(s):
        return (jnp.where(s == 0, 0, nstep - s), 0)

    wide = lambda im: pl.BlockSpec((rows, DN_DIM), im)
    narrow = lambda im: pl.BlockSpec((rows, LANE), im)
    return pl.pallas_call(
        functools.partial(_dn_kernel, n_cast=len(to_bf16)),
        grid=(nstep,),
        in_specs=[wide(fwd), wide(fwd), wide(fwd), narrow(fwd), wide(bwd), wide(bwd), wide(bwd), narrow(bwd)]
        + cast_specs,
        out_specs=[wide(fwd), wide(bwd)] + cast_specs,
        out_shape=[jax.ShapeDtypeStruct((t, DN_DIM), F32)] * 2
        + [jax.ShapeDtypeStruct(m.shape, BF16) for m in to_bf16],
        scratch_shapes=[pltpu.VMEM((2 * DN_HEADS, DN_HEAD_DIM, DN_HEAD_DIM), F32)],
        compiler_params=_params(("arbitrary",), VMEM_LIMIT),
        name="delta_net",
    )(q, k, v, gb, q, k, v, gb, *to_bf16)


def _rope(x, cos, sin):
    w = x.shape[1]
    lane = lax.broadcasted_iota(jnp.int32, x.shape, 1)
    first_half = (lane & (AXIS_DIM - 1)) < (AXIS_DIM // 2)
    swapped = jnp.where(first_half, pltpu.roll(x, w - AXIS_DIM // 2, 1), pltpu.roll(x, AXIS_DIM // 2, 1))
    return x * cos + swapped * sin


def _softmax_av(s, sink, vals):
    m = jnp.maximum(jnp.max(s, axis=-1, keepdims=True), sink)
    p = jnp.exp(s - m)
    denom = jnp.sum(p, axis=-1, keepdims=True) + jnp.exp(sink - m)
    return _mm(p, vals) / denom


def _attend(q, keys, vals, valid, sink_all, o_ref, row0=0):
    b = q.shape[0]
    for kvh in range(ATT_KV_HEADS):
        kl = slice(kvh * HEAD_DIM, (kvh + 1) * HEAD_DIM)
        heads = range(kvh * ATT_GROUP, (kvh + 1) * ATT_GROUP)
        qs = jnp.concatenate([q[:, h * HEAD_DIM:(h + 1) * HEAD_DIM] for h in heads], axis=0)
        sink = jnp.concatenate([jnp.broadcast_to(sink_all[0:1, h:h + 1], (b, 1)) for h in heads], axis=0)
        s = _mm(qs, keys[:, kl], _NT)
        if valid is not None:
            s = jnp.where(valid, s, NEG)
        o = _softmax_av(s, sink, vals[:, kl])
        for g, h in enumerate(heads):
            o_ref[row0:row0 + b, h * HEAD_DIM:(h + 1) * HEAD_DIM] = o[g * b:(g + 1) * b, :]


def _band_valid(nk, first, last):
    b = ATT_BLOCK
    c = lax.broadcasted_iota(jnp.int32, (1, nk), 1)
    far = 4 * nk
    ccol = jnp.where(c >= 3 * b, b + ((c - 3 * b) & (b - 1)), c)
    ccol = jnp.where(jnp.logical_and(c < b, first), -far, ccol)
    ccol = jnp.where(jnp.logical_and(jnp.logical_and(c >= 2 * b, c < 3 * b), last), far, ccol)
    r = lax.broadcasted_iota(jnp.int32, (ATT_GROUP * b, 1), 0) & (b - 1)
    return lax.bitcast_convert_type(ccol - r, jnp.uint32) <= jnp.uint32(2 * b)


def _rope_block(rowtab_ref, coltab_ref, blk):
    rt = rowtab_ref[blk]
    ct = coltab_ref[...]
    reps = ATT_BLOCK // GRID_W
    rows = jnp.concatenate([jnp.broadcast_to(rt[g:g + 1, :], (GRID_W, LANE)) for g in range(reps)], axis=0)
    return rows + jnp.concatenate([ct] * reps, axis=0)


def _attn_kernel(q_ref, kp_ref, kc_ref, kn_ref, kctx_ref, cosr_ref, sinr_ref, cosc_ref, sinc_ref, sink_ref, o_ref,
                 *, nb, with_ctx):
    j = pl.program_id(0)
    b = ATT_BLOCK
    scale = HEAD_DIM ** -0.5

    @pl.when(j == 0)
    def _():
        if with_ctx:
            kvx = kctx_ref[...]
            _attend(q_ref[...] * scale, kvx[:, :ATT_KV_DIM], kvx[:, ATT_KV_DIM:], None, sink_ref[...], o_ref)
        else:
            o_ref[...] = jnp.zeros(o_ref.shape, F32)

    @pl.when(j > 0)
    def _():
        b0 = 2 * (j - 1)
        blocks = (jnp.maximum(b0 - 1, 0), b0, b0 + 1, jnp.minimum(b0 + 2, nb - 1))
        cos = [_rope_block(cosr_ref, cosc_ref, blk) for blk in blocks]
        sin = [_rope_block(sinr_ref, sinc_ref, blk) for blk in blocks]
        kvc = kc_ref[...]
        kv = (kp_ref[...], kvc[0:b, :], kvc[b:2 * b, :], kn_ref[...])
        kvx = kctx_ref[...]
        keys = [_rope(t[:, :ATT_KV_DIM], c_, s_) for t, c_, s_ in zip(kv, cos, sin)]
        q_all = q_ref[...]
        sink_all = sink_ref[...]
        nk = 3 * b + kvx.shape[0]
        for sub in range(2):
            q = q_all[sub * b:(sub + 1) * b, :]
            q = jnp.concatenate([_rope(q[:, l * LANE:(l + 1) * LANE], cos[1 + sub], sin[1 + sub])
                                 for l in range(ATT_DIM // LANE)], axis=1)
            kcat = jnp.concatenate(keys[sub:sub + 3] + [kvx[:, :ATT_KV_DIM]], axis=0)
            vcat = jnp.concatenate([t[:, ATT_KV_DIM:] for t in kv[sub:sub + 3]] + [kvx[:, ATT_KV_DIM:]], axis=0)
            valid = _band_valid(nk, first=(b0 + sub == 0), last=(b0 + sub == nb - 1))
            _attend(q * scale, kcat, vcat, valid, sink_all, o_ref, row0=sub * b)


def _attention(pq, pkv, rope, sink, ctx_len, with_ctx):
    t = pq.shape[0]
    n = t - ctx_len
    nb = n // ATT_BLOCK
    step = 2 * ATT_BLOCK
    assert ctx_len == step and nb % 2 == 0
    off = ctx_len // ATT_BLOCK
    half = lambda im: pl.BlockSpec((ATT_BLOCK, 2 * ATT_KV_DIM), im)
    lat = lambda j: 2 * jnp.maximum(j - 1, 0)
    prv = lambda j: (jnp.maximum(lat(j) - 1, 0) + off, 0)
    nxt = lambda j: (jnp.minimum(lat(j) + 2, nb - 1) + off, 0)
    cur = lambda j: (j, 0)
    whole = lambda a: pl.BlockSpec(a.shape, lambda j: (0,) * a.ndim)
    return pl.pallas_call(
        functools.partial(_attn_kernel, nb=nb, with_ctx=with_ctx),
        grid=(1 + nb // 2,),
        in_specs=[
            pl.BlockSpec((step, ATT_DIM), cur),
            half(prv), pl.BlockSpec((step, 2 * ATT_KV_DIM), cur), half(nxt),
            pl.BlockSpec((ctx_len, 2 * ATT_KV_DIM), lambda j: (0, 0)),
        ] + [whole(a) for a in rope] + [pl.BlockSpec((8, LANE), lambda j: (0, 0))],
        out_specs=pl.BlockSpec((step, ATT_DIM), cur),
        out_shape=jax.ShapeDtypeStruct((t, ATT_DIM), F32),
        compiler_params=_params(("arbitrary",), VMEM_LIMIT),
        name="attention",
    )(pq, pkv, pkv, pkv, pkv, *rope, sink)


def _mixfin_kernel(ctx_ref, h_ref, mod_ref, pconv_ref, prev_ref, next_ref, cw_ref, of_ref, ob_ref, z_ref, ng_ref,
                   yc_ref, wout_ref, bd_ref, *rest, nblk, blk0, with_router):
    if with_router:
        wr1_ref, wr0_ref, x_ref, hx_ref, lg_ref = rest
    else:
        x_ref, hx_ref = rest
    i = pl.program_id(0) + blk0
    row = _mod_row(mod_ref, i == 0)
    pc = pconv_ref[...]
    u = pc[:, CONV_CH:2 * CONV_CH] * pc[:, 2 * CONV_CH:]
    prow, nrow = _halo_rows(prev_ref, next_ref, i, nblk)
    prow = prow[:, CONV_CH:2 * CONV_CH] * prow[:, 2 * CONV_CH:]
    nrow = nrow[:, CONV_CH:2 * CONV_CH] * nrow[:, 2 * CONV_CH:]
    up, un = _shift_rows(u, prow, nrow)
    cw = cw_ref[...]
    ya = pc[:, :CONV_CH] * (up * cw[0:1, :] + u * cw[1:2, :] + un * cw[2:3, :])
    o = of_ref[...] + ob_ref[...]
    ms = _group_sum(o * o, bd_ref[...]) * (1.0 / DN_HEAD_DIM)
    z = z_ref[...]
    yb = o * lax.rsqrt(ms + EPS) * ng_ref[...] * (z * _sigmoid(z))
    mix = jnp.concatenate([ya, yb, yc_ref[...]], axis=1)
    x = jnp.where(i == 0, ctx_ref[...], h_ref[...]) + row[:, 2 * D:3 * D] * _mm(mix, wout_ref[...])
    x_ref[...] = x
    hx = _rmsnorm_rows(x) * (1.0 + row[:, 4 * D:5 * D]) + row[:, 3 * D:4 * D]
    hx_ref[...] = hx.astype(hx_ref.dtype)
    if with_router:
        h1, h0 = _split2(hx)
        d = functools.partial(lax.dot_general, dimension_numbers=_NN, preferred_element_type=F32)
        lg = d(h1, wr1_ref[...]) + (d(h0, wr1_ref[...]) + d(h1, wr0_ref[...]))
        lane = lax.broadcasted_iota(jnp.int32, lg.shape, 1)
        lanef = lane.astype(F32)
        lg = jnp.where(lane < N_EXPERTS, lg, -jnp.inf)
        m1 = jnp.max(lg, axis=-1, keepdims=True)
        i1 = jnp.min(jnp.where(lg == m1, lanef, float(LANE)), axis=-1, keepdims=True)
        rest = jnp.where(lanef == i1, -jnp.inf, lg)
        m2 = jnp.max(rest, axis=-1, keepdims=True)
        i2 = jnp.min(jnp.where(rest == m2, lanef, float(LANE)), axis=-1, keepdims=True)
        e2 = jnp.exp(m2 - m1)
        g1 = 1.0 / (1.0 + e2)
        lg_ref[...] = jnp.where(lane == 0, g1, jnp.where(lane == 1, e2 * g1, jnp.where(lane == 2, i1, i2)))


def _mixer_finish(ctx_src, lat_src, lat_blk0, mod, pconv, conv_w, o_f, o_b, pz, norm_g, yc, w_out, bd, ctx_len,
                  with_ctx, router=None):
    t = pconv.shape[0]
    nblk = t // TM
    blk0 = 0 if with_ctx else ctx_len // TM
    rows = t - blk0 * TM
    r8 = TM // 8
    w = pconv.shape[1]
    cur = lambda i: (i + blk0, 0)
    out_cur = lambda i: (i, 0)
    const = lambda i: (0, 0)
    ng = jnp.tile(norm_g.reshape(1, DN_HEAD_DIM), (1, DN_HEADS))
    in_specs = [
        pl.BlockSpec((TM, D), const),
        pl.BlockSpec((TM, D), lambda i: (jnp.maximum(i + blk0 - 1, 0) + lat_blk0, 0)),
        pl.BlockSpec((8, N_MOD * D), const),
        pl.BlockSpec((TM, w), cur),
        pl.BlockSpec((8, w), lambda i: (jnp.maximum((i + blk0) * r8 - 1, 0), 0)),
        pl.BlockSpec((8, w), lambda i: (jnp.minimum((i + blk0 + 1) * r8, t // 8 - 1), 0)),
        pl.BlockSpec((3, CONV_CH), const),
        pl.BlockSpec((TM, DN_DIM), cur),
        pl.BlockSpec((TM, DN_DIM), cur),
        pl.BlockSpec((TM, DN_DIM), cur),
        pl.BlockSpec((1, DN_DIM), const),
        pl.BlockSpec((TM, ATT_DIM), cur),
        pl.BlockSpec((MIX_DIM, D), const),
        pl.BlockSpec(bd.shape, const),
    ]
    args = [ctx_src, lat_src, mod, pconv, pconv, pconv, conv_w, o_f, o_b, pz, ng, yc, w_out, bd]
    out_specs = [pl.BlockSpec((TM, D), out_cur), pl.BlockSpec((TM, D), out_cur)]
    hx_dtype = BF16 if router is None else F32
    out_shape = [jax.ShapeDtypeStruct((rows, D), F32), jax.ShapeDtypeStruct((rows, D), hx_dtype)]
    if router is not None:
        in_specs += [pl.BlockSpec((D, LANE), const)] * 2
        args += list(router)
        out_specs.append(pl.BlockSpec((TM, LANE), out_cur))
        out_shape.append(jax.ShapeDtypeStruct((rows, LANE), F32))
    return pl.pallas_call(
        functools.partial(_mixfin_kernel, nblk=nblk, blk0=blk0, with_router=router is not None),
        grid=(rows // TM,),
        in_specs=in_specs,
        out_specs=out_specs,
        out_shape=out_shape,
        compiler_params=_params(("arbitrary",), VMEM_LIMIT),
        name="mixer_finish",
    )(*args)


def _ffn_kernel(hx_ref, x_ref, mod_ref, wg_ref, wu_ref, wd_ref, o_ref):
    row = _mod_row(mod_ref, pl.program_id(0) == 0)
    hx = hx_ref[...]
    acc = jnp.zeros((hx.shape[0], D), F32)
    for f in range(0, D_FF, FF_CHUNK):
        fe = min(f + FF_CHUNK, D_FF)
        g = _mm(hx, wg_ref[:, f:fe])
        u = _mm(hx, wu_ref[:, f:fe])
        acc = acc + _mm(g * _sigmoid(g) * u, wd_ref[f:fe, :])
    o_ref[...] = x_ref[...] + row[:, 5 * D:6 * D] * acc


def _dense_ffn(hx, x, mod, wg, wu, wd):
    t = x.shape[0]
    const = lambda i: (0, 0)
    once = dict(pipeline_mode=pl.Buffered(1))
    return pl.pallas_call(
        _ffn_kernel,
        grid=(t // TM,),
        in_specs=[
            pl.BlockSpec((TM, D), lambda i: (i, 0)),
            pl.BlockSpec((TM, D), lambda i: (i, 0)),
            pl.BlockSpec((8, N_MOD * D), const),
            pl.BlockSpec((D, D_FF), const, **once),
            pl.BlockSpec((D, D_FF), const, **once),
            pl.BlockSpec((D_FF, D), const, **once),
        ],
        out_specs=pl.BlockSpec((TM, D), lambda i: (i, 0)),
        out_shape=jax.ShapeDtypeStruct((t, D), F32),
        compiler_params=_params(("arbitrary",), VMEM_LIMIT),
        name="dense_ffn",
    )(hx, x, mod, wg, wu, wd)


def _moe_kernel(be_ref, nu_ref, xs_ref, wg_ref, wu_ref, wd_ref, *rest, boff):
    y_ref, acc_ref = rest[-2:]
    b = pl.program_id(0) + boff
    f = pl.program_id(1)

    @pl.when(b < nu_ref[0])
    def _():
        xs = xs_ref[...].astype(BF16)
        part = jnp.zeros((MOE_TM, D), F32)
        for c in range(0, MOE_TF, MOE_SUB):
            g = _mm(xs, wg_ref[0, :, c:c + MOE_SUB])
            u = _mm(xs, wu_ref[0, :, c:c + MOE_SUB])
            part = part + _mm(g * _sigmoid(g) * u, wd_ref[0, c:c + MOE_SUB, :])

        @pl.when(f == 0)
        def _():
            acc_ref[...] = part

        @pl.when(f != 0)
        def _():
            acc_ref[...] += part

        @pl.when(f == pl.num_programs(1) - 1)
        def _():
            y_ref[...] = acc_ref[...]

    @pl.when(b >= nu_ref[0])
    def _():
        y_ref[...] = jnp.zeros(y_ref.shape, F32)


def _moe_experts(xs_parts, blk_e, n_used, wg, wu, wd):
    rows_p = xs_parts[0].shape[0]
    nblk_p = rows_p // MOE_TM
    cap = rows_p * len(xs_parts)
    nf = D_FF_EXPERT // MOE_TF
    y = None
    for part, xs in enumerate(xs_parts):
        boff = part * nblk_p

        def fidx(b, f, nu):
            return jnp.where(b < nu[0], f, nf - 1)

        g = lambda b, boff=boff: b + boff
        in_specs = [
            pl.BlockSpec((MOE_TM, D), lambda b, f, be, nu: (b, 0)),
            pl.BlockSpec((1, D, MOE_TF), lambda b, f, be, nu, g=g: (be[g(b)], 0, fidx(g(b), f, nu))),
            pl.BlockSpec((1, D, MOE_TF), lambda b, f, be, nu, g=g: (be[g(b)], 0, fidx(g(b), f, nu))),
            pl.BlockSpec((1, MOE_TF, D), lambda b, f, be, nu, g=g: (be[g(b)], fidx(g(b), f, nu), 0)),
        ]
        args = [blk_e, n_used, xs, wg, wu, wd]
        aliases = {}
        if y is not None:
            in_specs.append(pl.BlockSpec(memory_space=pl.ANY))
            aliases = {len(args): 0}
            args.append(y)
        grid_spec = pltpu.PrefetchScalarGridSpec(
            num_scalar_prefetch=2,
            grid=(nblk_p, nf),
            in_specs=in_specs,
            out_specs=pl.BlockSpec((MOE_TM, D), lambda b, f, be, nu, g=g: (g(b), 0)),
            scratch_shapes=[pltpu.VMEM((MOE_TM, D), F32)],
        )
        y = pl.pallas_call(
            functools.partial(_moe_kernel, boff=boff),
            grid_spec=grid_spec,
            out_shape=jax.ShapeDtypeStruct((cap, D), F32),
            input_output_aliases=aliases,
            compiler_params=_params(("arbitrary", "arbitrary"), VMEM_LIMIT),
            name="moe_experts",
        )(*args)
    return y


def _moe_route(top_e):
    n = top_e.shape[0]
    a = n * TOP_K
    flat_e = top_e.reshape(a)
    onehot = (flat_e[:, None] == jnp.arange(N_EXPERTS, dtype=flat_e.dtype)[None, :]).astype(jnp.int32)
    counts = jnp.sum(onehot, axis=0)
    padded = (counts + MOE_TM - 1) // MOE_TM * MOE_TM
    pad_ends = jnp.cumsum(padded)
    pad_starts = pad_ends - padded
    dest = jnp.sum(onehot * (jnp.cumsum(onehot, axis=0) - onehot + pad_starts[None, :]), axis=1)
    cap = a + N_EXPERTS * MOE_TM
    nblk = cap // MOE_TM
    row_tok = (jnp.arange(cap, dtype=jnp.int32) % n).at[dest].set(
        jnp.arange(a, dtype=jnp.int32) // TOP_K, unique_indices=True, mode="promise_in_bounds")
    blk_start = jnp.arange(nblk, dtype=jnp.int32) * MOE_TM
    blk_e = jnp.minimum(jnp.sum((pad_ends[None, :] <= blk_start[:, None]).astype(jnp.int32), axis=1),
                        N_EXPERTS - 1)
    n_used = (pad_ends[-1] // MOE_TM).astype(jnp.int32).reshape(1)
    last_e = blk_e[jnp.maximum(n_used[0] - 1, 0)]
    blk_e = jnp.where(jnp.arange(nblk) < n_used[0], blk_e, last_e)
    return dest.reshape(n, TOP_K), row_tok, blk_e, n_used


def _final_kernel(x_ref, y0_ref, y1_ref, gt_ref, mod_ref, fg_ref, *rest):
    o_ref = rest[-1]
    mod = mod_ref[...]
    gt = gt_ref[...]
    f = gt[:, 0:1] * y0_ref[...] + gt[:, 1:2] * y1_ref[...]
    x = x_ref[...] + mod[0:1, 5 * D:6 * D] * f
    o_ref[...] = _rmsnorm_rows(x) * fg_ref[...]


def _moe_combine_final(x, y_parts, gt, mod, final_g):
    n = x.shape[0]
    rows_p = y_parts[0][0].shape[0]
    nblk_p = rows_p // TM
    const = lambda i: (0, 0)
    out = None
    for part, (y0, y1) in enumerate(y_parts):
        glob = lambda i, off=part * nblk_p: (i + off, 0)
        loc = lambda i: (i, 0)
        in_specs = [pl.BlockSpec((TM, D), glob), pl.BlockSpec((TM, D), loc), pl.BlockSpec((TM, D), loc),
                    pl.BlockSpec((TM, LANE), glob), pl.BlockSpec((8, N_MOD * D), const),
                    pl.BlockSpec((1, D), const)]
        args = [x, y0, y1, gt, mod, final_g.reshape(1, D)]
        aliases = {}
        if out is not None:
            in_specs.append(pl.BlockSpec(memory_space=pl.ANY))
            aliases = {len(args): 0}
            args.append(out)
        out = pl.pallas_call(
            _final_kernel,
            grid=(nblk_p,),
            in_specs=in_specs,
            out_specs=pl.BlockSpec((TM, D), glob),
            out_shape=jax.ShapeDtypeStruct((n, D), F32),
            input_output_aliases=aliases,
            compiler_params=_params(("arbitrary",), VMEM_LIMIT),
            name="moe_combine_final",
        )(*args)
    return out


def _rope_tables(n):
    lane = jnp.arange(LANE, dtype=jnp.int32) % HEAD_DIM
    inv = ROPE_BASE ** (-jnp.arange(0, AXIS_DIM, 2, dtype=F32) / AXIS_DIM)
    freq = inv[lane % (AXIS_DIM // 2)]
    row_axis = (lane // AXIS_DIM) == 0
    sign = jnp.where((lane % AXIS_DIM) < AXIS_DIM // 2, -1.0, 1.0)
    reps = ATT_BLOCK // GRID_W

    def tables(count, on_axis):
        ang = jnp.arange(count, dtype=F32)[:, None] * freq[None, :]
        return jnp.where(on_axis, jnp.cos(ang), 0.0), jnp.where(on_axis, jnp.sin(ang) * sign, 0.0)

    by_block = lambda t: jnp.pad(t.reshape(-1, reps, LANE), ((0, 0), (0, 8 - reps), (0, 0)))
    cosr, sinr = tables(n // GRID_W, row_axis)
    cosc, sinc = tables(GRID_W, ~row_axis)
    return by_block(cosr), by_block(sinr), cosc, sinc


def _prep_w_in(w):
    pad = jnp.zeros((D, LANE - N_AB), w.dtype)
    return jnp.concatenate([w[:, :_C_Z], w[:, _C_A:_C_Q], pad, w[:, _C_Z:_C_A], w[:, _C_Q:_C_END]],
                           axis=1).astype(BF16)


def kernel(x, c, ctx, c_ctx, w_mod, b_mod, w_in, w_out, conv_w, dn_conv_w, dn_a_log, dn_dt_bias, dn_norm_g,
           attn_sink, ffn_w_gate, ffn_w_up, ffn_w_down, moe_router, moe_w_gate, moe_w_up, moe_w_down,
           final_norm_g):
    bsz, n, d = x.shape
    ctx_len = ctx.shape[1]
    depth = w_in.shape[0]
    assert bsz == 1 and d == D and ctx_len == TM and n % TM == 0 and n % GRID_W == 0
    rope = _rope_tables(n)
    mods = _mod_vectors(c, c_ctx, w_mod, b_mod)
    bd = _head_blockdiag(DN_DIM, DN_HEAD_DIM)
    stream = (ctx[0], x[0], 0)
    for layer in range(depth):
        last = layer == depth - 1
        mod = mods[layer]
        pconv, pz, pq, pkv, qn, kn, vv, gb = _in_proj(*stream, mod, _prep_w_in(w_in[layer]), dn_conv_w[layer],
                                                      dn_a_log[layer], dn_dt_bias[layer], bd)
        to_bf16 = []
        if layer % 2 == 0:
            to_bf16 += [w[layer // 2] for w in (ffn_w_gate, ffn_w_up, ffn_w_down)]
        if (layer + 1) % 2 == 1 and layer + 1 < depth:
            j = (layer + 1) // 2
            to_bf16 += [w[j].reshape(-1, w.shape[-1]) for w in (moe_w_gate, moe_w_up, moe_w_down)]
        o_f, o_b, *cast = _delta_net(qn, kn, vv, gb, ctx_len, tuple(to_bf16))
        if layer % 2 == 0:
            ffn_bf16, cast = cast[:3], cast[3:]
        if cast:
            moe_bf16 = [c.reshape(w.shape[1:]) for c, w in zip(cast, (moe_w_gate, moe_w_up, moe_w_down))]
        sink = jnp.zeros((8, LANE), F32).at[0, :ATT_HEADS].set(attn_sink[layer])
        yc = _attention(pq, pkv, rope, sink, ctx_len, with_ctx=not last)
        router = None
        if layer % 2 == 1:
            wr = jnp.zeros((D, LANE), F32).at[:, :N_EXPERTS].set(moe_router[layer // 2])
            wr1 = wr.astype(BF16)
            router = (wr1, (wr - wr1.astype(F32)).astype(BF16))
        outs = _mixer_finish(*stream, mod, pconv, conv_w[layer], o_f, o_b, pz, dn_norm_g[layer], yc,
                             w_out[layer].astype(BF16), bd, ctx_len, with_ctx=not last, router=router)
        if layer % 2 == 0:
            assert not last
            x1, hx = outs
            i = layer // 2
            h = _dense_ffn(hx, x1, mod, *ffn_bf16)
            stream = (h, h, ctx_len // TM)
        else:
            assert last
            x1, hx, route = outs
            i = layer // 2
            dest, row_tok, blk_e, n_used = _moe_route(route[:, 2:2 + TOP_K].astype(jnp.int32))
            take = lambda rows_, idx: rows_.at[idx].get(mode="promise_in_bounds")
            halves = lambda a: (a[:a.shape[0] // 2], a[a.shape[0] // 2:])
            y = _moe_experts([take(hx, r) for r in halves(row_tok)], blk_e, n_used, *moe_bf16)
            h = _moe_combine_final(x1, [(take(y, dd[:, 0]), take(y, dd[:, 1])) for dd in halves(dest)],
                                   route, mod, final_norm_g)
    return h.reshape(bsz, n, d)
```

```python
import functools

import jax
import jax.numpy as jnp
from jax import lax
from jax.experimental import pallas as pl
from jax.experimental.pallas import tpu as pltpu

F32 = jnp.float32
BF16 = jnp.bfloat16

D = 1024
N_MOD = 6
EPS = 1e-6
NEG = -1e30
GRID_W = 64

CONV_CH = 256
DN_HEADS = 6
DN_HEAD_DIM = 64
DN_DIM = DN_HEADS * DN_HEAD_DIM
DN_CHUNK = 64
DN_SUB = 16
DN_STEP_CHUNKS = 4
ATT_HEADS = 6
ATT_KV_HEADS = 2
ATT_GROUP = ATT_HEADS // ATT_KV_HEADS
HEAD_DIM = 64
ATT_DIM = ATT_HEADS * HEAD_DIM
ATT_KV_DIM = ATT_KV_HEADS * HEAD_DIM
ATT_BLOCK = 128
ROPE_BASE = 10000.0
AXIS_DIM = HEAD_DIM // 2
MIX_DIM = CONV_CH + DN_DIM + ATT_DIM

D_FF = 2816
N_EXPERTS = 8
TOP_K = 2
D_FF_EXPERT = 3584

TM = 256
FF_CHUNK = 512
MOE_TM = 512
MOE_TF = 1792
MOE_SUB = 256
MOE_PARTS = 4
LANE = 128
VMEM_LIMIT = 56 * 1024 * 1024

_C_QKV = 3 * CONV_CH
_C_Z = _C_QKV + 3 * DN_DIM
_C_A = _C_Z + DN_DIM
_C_Q = _C_A + 4 * DN_HEADS
_C_K = _C_Q + ATT_DIM
_C_V = _C_K + ATT_KV_DIM
_C_END = _C_V + ATT_KV_DIM
N_AB = 4 * DN_HEADS


def _params(sem=None, vmem=None):
    kw = {}
    if sem is not None:
        kw["dimension_semantics"] = sem
    if vmem is not None:
        kw["vmem_limit_bytes"] = vmem
    return pltpu.CompilerParams(**kw)


def _split2(a):
    hi = a.astype(BF16)
    lo = (a - hi.astype(F32)).astype(BF16)
    return hi, lo


def _split3(a):
    hi = a.astype(BF16)
    r = a - hi.astype(F32)
    mid = r.astype(BF16)
    lo = (r - mid.astype(F32)).astype(BF16)
    return hi, mid, lo


_NN = (((1,), (0,)), ((), ()))
_NT = (((1,), (1,)), ((), ()))
_TN = (((0,), (0,)), ((), ()))


def _mm(a, b, dims=_NN):
    return lax.dot_general(a.astype(BF16), b.astype(BF16), dims, preferred_element_type=F32)


def _mm3(a, b, dims=_NN):
    a1, a0 = _split2(a)
    b1, b0 = _split2(b)
    d = functools.partial(lax.dot_general, dimension_numbers=dims, preferred_element_type=F32)
    return d(a1, b1) + (d(a1, b0) + d(a0, b1))


_BNN = (((2,), (1,)), ((0,), (0,)))
_BNT = (((2,), (2,)), ((0,), (0,)))
_BTN = (((1,), (1,)), ((0,), (0,)))


def _bmm(a, b, dims=_BNN):
    return lax.dot_general(a.astype(BF16), b.astype(BF16), dims, preferred_element_type=F32)


def _sigmoid(x):
    return 1.0 / (1.0 + jnp.exp(-x))


def _softplus(x):
    return jnp.maximum(x, 0.0) + jnp.log1p(jnp.exp(-jnp.abs(x)))


def _mod_row(mod_ref, is_ctx):
    mod = mod_ref[...]
    return jnp.where(is_ctx, mod[1:2, :], mod[0:1, :])


def _rmsnorm_rows(x):
    return x * lax.rsqrt(jnp.mean(x * x, axis=-1, keepdims=True) + EPS)


def _shift_rows(u, prow, nrow):
    n = u.shape[0]
    rid = lax.broadcasted_iota(jnp.int32, u.shape, 0)
    up = jnp.where(rid == 0, prow, pltpu.roll(u, 1, 0))
    un = jnp.where(rid == n - 1, nrow, pltpu.roll(u, n - 1, 0))
    return up, un


def _same_group(shape, group):
    sh = group.bit_length() - 1
    assert 1 << sh == group
    return (lax.broadcasted_iota(jnp.int32, shape, 0) >> sh) == (lax.broadcasted_iota(jnp.int32, shape, 1) >> sh)


def _head_blockdiag(n, group):
    g = jnp.arange(n, dtype=jnp.int32) // group
    return (g[:, None] == g[None, :]).astype(BF16)


def _group_sum(t, bd):
    return _mm(t, bd)


MOD_TN = 1536


def _mod_kernel(s_ref, w_ref, b_ref, o_ref):
    s = s_ref[...]
    s = s * _sigmoid(s)
    o_ref[0] = _mm3(s, w_ref[0]) + b_ref[0]


def _mod_vectors(c, c_ctx, w_mod, b_mod):
    depth = w_mod.shape[0]
    s = jnp.zeros((8, D), F32).at[0].set(c[0]).at[1].set(c_ctx)
    return pl.pallas_call(
        _mod_kernel,
        grid=(depth, N_MOD * D // MOD_TN),
        in_specs=[
            pl.BlockSpec((8, D), lambda l, j: (0, 0)),
            pl.BlockSpec((1, D, MOD_TN), lambda l, j: (l, 0, j)),
            pl.BlockSpec((1, 1, MOD_TN), lambda l, j: (l, 0, j)),
        ],
        out_specs=pl.BlockSpec((1, 8, MOD_TN), lambda l, j: (l, 0, j)),
        out_shape=jax.ShapeDtypeStruct((depth, 8, N_MOD * D), F32),
        compiler_params=_params(("arbitrary", "arbitrary"), VMEM_LIMIT),
        name="mod_vectors",
    )(s, w_mod, b_mod.reshape(depth, 1, N_MOD * D))


def _halo_valid(i, nblk):
    return jnp.logical_and(i != 0, i != 1), jnp.logical_and(i != 0, i != nblk - 1)


def _halo_rows(prev_ref, next_ref, i, nblk):
    pvalid, nvalid = _halo_valid(i, nblk)
    prow = jnp.where(pvalid, prev_ref[7:8, :], 0.0)
    nrow = jnp.where(nvalid, next_ref[0:1, :], 0.0)
    return prow, nrow


def _in_kernel(ctx_ref, h_ref, hprev_ref, hnext_ref, mod_ref, w_ref, cw_ref, alog_ref, dtb_ref, bd_ref,
               pconv_ref, pz_ref, pq_ref, pkv_ref, q_ref, k_ref, v_ref, gb_ref, *, nblk):
    i = pl.program_id(0)
    row = _mod_row(mod_ref, i == 0)
    norm_mod = lambda x: _rmsnorm_rows(x) * (1.0 + row[:, D:2 * D]) + row[:, 0:D]
    h1 = norm_mod(jnp.where(i == 0, ctx_ref[...], h_ref[...])).astype(BF16)
    halo = norm_mod(jnp.concatenate([hprev_ref[...], hnext_ref[...]], axis=0)).astype(BF16)
    d = functools.partial(lax.dot_general, dimension_numbers=_NN, preferred_element_type=F32)
    tm = h1.shape[0]
    c0 = 3 * CONV_CH
    c1 = c0 + 3 * DN_DIM
    c2 = c1 + LANE
    c3 = c2 + DN_DIM + ATT_DIM
    pconv_ref[...] = d(h1, w_ref[:, 0:c0])
    zq = d(h1, w_ref[:, c2:c3])
    pz_ref[...] = zq[:, 0:DN_DIM]
    pq_ref[...] = zq[:, DN_DIM:]
    pkv_ref[...] = d(h1, w_ref[:, c3:c3 + 2 * ATT_KV_DIM])
    qkv_ab = d(jnp.concatenate([h1, halo], axis=0), w_ref[:, c0:c2])
    qkv = qkv_ab[:, 0:3 * DN_DIM]
    ab = qkv_ab[0:tm, 3 * DN_DIM:]

    u = qkv[0:tm, :]
    pvalid, nvalid = _halo_valid(i, nblk)
    prow = jnp.where(pvalid, qkv[tm + 7:tm + 8, :], 0.0)
    nrow = jnp.where(nvalid, qkv[tm + 8:tm + 9, :], 0.0)
    up, un = _shift_rows(u, prow, nrow)
    cw = cw_ref[...]
    y = up * cw[0:1, :] + u * cw[1:2, :] + un * cw[2:3, :]
    y = y * _sigmoid(y)
    q = y[:, 0:DN_DIM]
    k = y[:, DN_DIM:2 * DN_DIM]
    bd = bd_ref[...]
    q_ref[...] = q * lax.rsqrt(_group_sum(q * q, bd) + 1e-6) * (DN_HEAD_DIM ** -0.5)
    k_ref[...] = k * lax.rsqrt(_group_sum(k * k, bd) + 1e-6)
    v_ref[...] = y[:, 2 * DN_DIM:3 * DN_DIM]
    g = -jnp.exp(alog_ref[...]) * _softplus(ab + dtb_ref[...])
    lane = lax.broadcasted_iota(jnp.int32, ab.shape, 1)
    gb_ref[...] = jnp.where(lane < 2 * DN_HEADS, g, _sigmoid(ab))


def _in_proj(ctx_src, lat_src, lat_blk0, mod, w_main, dn_conv_w, a_log, dt_bias, bd):
    r8 = TM // 8
    last8 = lat_src.shape[0] // 8 - 1
    nblk = 1 + lat_src.shape[0] // TM - lat_blk0
    t = nblk * TM
    alog = jnp.zeros((1, LANE), F32).at[0, :2 * DN_HEADS].set(a_log.reshape(-1))
    dtb = jnp.zeros((1, LANE), F32).at[0, :2 * DN_HEADS].set(dt_bias.reshape(-1))
    widths = (3 * CONV_CH, DN_DIM, ATT_DIM, 2 * ATT_KV_DIM, DN_DIM, DN_DIM, DN_DIM, LANE)
    const = lambda i: (0, 0)
    lat = lambda i: jnp.maximum(i - 1, 0) + lat_blk0
    return pl.pallas_call(
        functools.partial(_in_kernel, nblk=nblk),
        grid=(nblk,),
        in_specs=[
            pl.BlockSpec((TM, D), const),
            pl.BlockSpec((TM, D), lambda i: (lat(i), 0)),
            pl.BlockSpec((8, D), lambda i: (jnp.maximum(lat(i) * r8 - 1, 0), 0)),
            pl.BlockSpec((8, D), lambda i: (jnp.minimum((lat(i) + 1) * r8, last8), 0)),
            pl.BlockSpec((8, N_MOD * D), const),
            pl.BlockSpec(w_main.shape, const),
            pl.BlockSpec((3, 3 * DN_DIM), const),
            pl.BlockSpec((1, LANE), const),
            pl.BlockSpec((1, LANE), const),
            pl.BlockSpec(bd.shape, const),
        ],
        out_specs=[pl.BlockSpec((TM, w), lambda i: (i, 0)) for w in widths],
        out_shape=[jax.ShapeDtypeStruct((t, w), F32) for w in widths],
        compiler_params=_params(("arbitrary",), VMEM_LIMIT),
        name="in_proj",
    )(ctx_src, lat_src, lat_src, lat_src, mod, w_main, dn_conv_w, alog, dtb, bd)


def _dn_chunk(rev, q_ref, k_ref, v_ref, gb_ref, o_ref, s_ref):
    c_ = DN_CHUNK
    ri = lax.broadcasted_iota(jnp.int32, (c_, c_), 0)
    ci = lax.broadcasted_iota(jnp.int32, (c_, c_), 1)
    incl = (ri <= ci) if rev else (ri >= ci)
    strict = (ri < ci) if rev else (ri > ci)
    same_sub = _same_group((c_, c_), DN_SUB)
    eye = jnp.where(ri == ci, 1.0, 0.0)
    tri = jnp.where(incl, 1.0, 0.0).astype(BF16)
    last = 0 if rev else c_ - 1

    nchunks = q_ref.shape[0] // c_
    nh = DN_HEADS
    col0 = nh if rev else 0
    rows = lambda g: slice(g * c_, (g + 1) * c_)
    lanes = lambda h: slice(h * DN_HEAD_DIM, (h + 1) * DN_HEAD_DIM)

    def stack(fn):
        return jnp.stack([fn(g, h) for g in range(nchunks) for h in range(nh)])

    gb = gb_ref[...]
    gcs = [_cumsum_rows(tri, gb[rows(g), :]) for g in range(nchunks)]
    gcts = [gc.T for gc in gcs]
    q = stack(lambda g, h: q_ref[rows(g), lanes(h)])
    k = stack(lambda g, h: k_ref[rows(g), lanes(h)])
    v = stack(lambda g, h: v_ref[rows(g), lanes(h)])
    gcol = stack(lambda g, h: gcs[g][:, col0 + h:col0 + h + 1])
    grow = stack(lambda g, h: gcts[g][col0 + h:col0 + h + 1, :])
    beta = stack(lambda g, h: gb[rows(g), 2 * nh + col0 + h:2 * nh + col0 + h + 1])
    glast = gcol[:, last:last + 1, :]
    decay = jnp.where(incl, jnp.exp(jnp.where(incl, gcol - grow, 0.0)), 0.0)
    eg = jnp.exp(gcol)
    kb = k * beta
    a = jnp.where(strict, _bmm(kb, k, _BNT) * decay, 0.0)
    qk = jnp.where(incl, _bmm(q, k, _BNT) * decay, 0.0)
    ad = jnp.where(same_sub, a, 0.0)
    ao = a - ad
    p = eye - ad
    n2 = _bmm(ad, ad)
    p = p + _bmm(p, n2)
    n4 = _bmm(n2, n2)
    p = p + _bmm(p, n4)
    n8 = _bmm(n4, n4)
    dinv = p + _bmm(p, n8)
    m = _bmm(dinv, ao)
    m2 = _bmm(m, m)
    y = _bmm(dinv, jnp.concatenate([v * beta, kb * eg], axis=-1))
    z = y + _bmm(m2, y)
    x = z - _bmm(m, z)
    u = x[:, :, :DN_HEAD_DIM]
    w = x[:, :, DN_HEAD_DIM:]
    qg = q * eg
    kd = k * jnp.exp(glast - gcol)
    gl = jnp.exp(glast)
    s = s_ref[col0:col0 + nh]
    for g in (reversed(range(nchunks)) if rev else range(nchunks)):
        b = slice(g * nh, (g + 1) * nh)
        v_new = u[b] - _bmm(w[b], s)
        o = _bmm(qg[b], s) + _bmm(qk[b], v_new)
        s = s * gl[b] + _bmm(kd[b], v_new, _BTN)
        for h in range(nh):
            o_ref[rows(g), lanes(h)] = o[h]
    s_ref[col0:col0 + nh] = s


def _cumsum_rows(tri_bf16, g):
    g2, g1, g0 = _split3(g)
    d = functools.partial(lax.dot_general, dimension_numbers=_NN, preferred_element_type=F32)
    return d(tri_bf16, g2) + (d(tri_bf16, g1) + d(tri_bf16, g0))


def _dn_kernel(qf, kf, vf, gf, qb, kb, vb, gbb, *rest, n_cast):
    cast_in = rest[:n_cast]
    of_ref, ob_ref = rest[n_cast:n_cast + 2]
    cast_out = rest[n_cast + 2:2 * n_cast + 2]
    s_ref = rest[-1]

    @pl.when(pl.program_id(0) == 0)
    def _():
        s_ref[...] = jnp.zeros(s_ref.shape, F32)

    _dn_chunk(False, qf, kf, vf, gf, of_ref, s_ref)
    _dn_chunk(True, qb, kb, vb, gbb, ob_ref, s_ref)
    for src, dst in zip(cast_in, cast_out):
        dst[...] = src[...].astype(BF16)


def _delta_net(q, k, v, gb, ctx_len, to_bf16=()):
    t = q.shape[0]
    rows = DN_STEP_CHUNKS * DN_CHUNK
    assert ctx_len == rows and t % rows == 0
    nstep = t // rows
    cast_specs = []
    for m in to_bf16:
        rb = -(-m.shape[0] // nstep)
        rb = -(-rb // 16) * 16
        nb = -(-m.shape[0] // rb)
        cast_specs.append(pl.BlockSpec((rb, m.shape[1]), lambda s, nb=nb: (jnp.minimum(s, nb - 1), 0)))

    def fwd(s):
        return (s, 0)

    def bwd(s):
        return (jnp.where(s == 0, 0, nstep - s), 0)

    wide = lambda im: pl.BlockSpec((rows, DN_DIM), im)
    narrow = lambda im: pl.BlockSpec((rows, LANE), im)
    return pl.pallas_call(
        functools.partial(_dn_kernel, n_cast=len(to_bf16)),
        grid=(nstep,),
        in_specs=[wide(fwd), wide(fwd), wide(fwd), narrow(fwd), wide(bwd), wide(bwd), wide(bwd), narrow(bwd)]
        + cast_specs,
        out_specs=[wide(fwd), wide(bwd)] + cast_specs,
        out_shape=[jax.ShapeDtypeStruct((t, DN_DIM), F32)] * 2
        + [jax.ShapeDtypeStruct(m.shape, BF16) for m in to_bf16],
        scratch_shapes=[pltpu.VMEM((2 * DN_HEADS, DN_HEAD_DIM, DN_HEAD_DIM), F32)],
        compiler_params=_params(("arbitrary",), VMEM_LIMIT),
        name="delta_net",
    )(q, k, v, gb, q, k, v, gb, *to_bf16)


def _rope(x, cos, sin):
    w = x.shape[1]
    lane = lax.broadcasted_iota(jnp.int32, x.shape, 1)
    first_half = (lane & (AXIS_DIM - 1)) < (AXIS_DIM // 2)
    swapped = jnp.where(first_half, pltpu.roll(x, w - AXIS_DIM // 2, 1), pltpu.roll(x, AXIS_DIM // 2, 1))
    return x * cos + swapped * sin


def _softmax_av(s, sink, vals):
    m = jnp.maximum(jnp.max(s, axis=-1, keepdims=True), sink)
    p = jnp.exp(s - m)
    denom = jnp.sum(p, axis=-1, keepdims=True) + jnp.exp(sink - m)
    return _mm(p, vals) / denom


def _attend(q, keys, vals, valid, sink_all, o_ref, row0=0):
    b = q.shape[0]
    for kvh in range(ATT_KV_HEADS):
        kl = slice(kvh * HEAD_DIM, (kvh + 1) * HEAD_DIM)
        heads = range(kvh * ATT_GROUP, (kvh + 1) * ATT_GROUP)
        qs = jnp.concatenate([q[:, h * HEAD_DIM:(h + 1) * HEAD_DIM] for h in heads], axis=0)
        sink = jnp.concatenate([jnp.broadcast_to(sink_all[0:1, h:h + 1], (b, 1)) for h in heads], axis=0)
        s = _mm(qs, keys[:, kl], _NT)
        if valid is not None:
            s = jnp.where(valid, s, NEG)
        o = _softmax_av(s, sink, vals[:, kl])
        for g, h in enumerate(heads):
            o_ref[row0:row0 + b, h * HEAD_DIM:(h + 1) * HEAD_DIM] = o[g * b:(g + 1) * b, :]


def _band_valid(nk, first, last):
    b = ATT_BLOCK
    c = lax.broadcasted_iota(jnp.int32, (1, nk), 1)
    far = 4 * nk
    ccol = jnp.where(c >= 3 * b, b + ((c - 3 * b) & (b - 1)), c)
    ccol = jnp.where(jnp.logical_and(c < b, first), -far, ccol)
    ccol = jnp.where(jnp.logical_and(jnp.logical_and(c >= 2 * b, c < 3 * b), last), far, ccol)
    r = lax.broadcasted_iota(jnp.int32, (ATT_GROUP * b, 1), 0) & (b - 1)
    return lax.bitcast_convert_type(ccol - r, jnp.uint32) <= jnp.uint32(2 * b)


def _rope_block(rowtab_ref, coltab_ref, blk):
    rt = rowtab_ref[blk]
    ct = coltab_ref[...]
    reps = ATT_BLOCK // GRID_W
    rows = jnp.concatenate([jnp.broadcast_to(rt[g:g + 1, :], (GRID_W, LANE)) for g in range(reps)], axis=0)
    return rows + jnp.concatenate([ct] * reps, axis=0)


def _attn_kernel(q_ref, kp_ref, kc_ref, kn_ref, kctx_ref, cosr_ref, sinr_ref, cosc_ref, sinc_ref, sink_ref, o_ref,
                 *, nb, with_ctx):
    j = pl.program_id(0)
    b = ATT_BLOCK
    scale = HEAD_DIM ** -0.5

    @pl.when(j == 0)
    def _():
        if with_ctx:
            kvx = kctx_ref[...]
            _attend(q_ref[...] * scale, kvx[:, :ATT_KV_DIM], kvx[:, ATT_KV_DIM:], None, sink_ref[...], o_ref)
        else:
            o_ref[...] = jnp.zeros(o_ref.shape, F32)

    @pl.when(j > 0)
    def _():
        b0 = 2 * (j - 1)
        blocks = (jnp.maximum(b0 - 1, 0), b0, b0 + 1, jnp.minimum(b0 + 2, nb - 1))
        cos = [_rope_block(cosr_ref, cosc_ref, blk) for blk in blocks]
        sin = [_rope_block(sinr_ref, sinc_ref, blk) for blk in blocks]
        kvc = kc_ref[...]
        kv = (kp_ref[...], kvc[0:b, :], kvc[b:2 * b, :], kn_ref[...])
        kvx = kctx_ref[...]
        keys = [_rope(t[:, :ATT_KV_DIM], c_, s_) for t, c_, s_ in zip(kv, cos, sin)]
        q_all = q_ref[...]
        sink_all = sink_ref[...]
        nk = 3 * b + kvx.shape[0]
        for sub in range(2):
            q = q_all[sub * b:(sub + 1) * b, :]
            q = jnp.concatenate([_rope(q[:, l * LANE:(l + 1) * LANE], cos[1 + sub], sin[1 + sub])
                                 for l in range(ATT_DIM // LANE)], axis=1)
            kcat = jnp.concatenate(keys[sub:sub + 3] + [kvx[:, :ATT_KV_DIM]], axis=0)
            vcat = jnp.concatenate([t[:, ATT_KV_DIM:] for t in kv[sub:sub + 3]] + [kvx[:, ATT_KV_DIM:]], axis=0)
            valid = _band_valid(nk, first=(b0 + sub == 0), last=(b0 + sub == nb - 1))
            _attend(q * scale, kcat, vcat, valid, sink_all, o_ref, row0=sub * b)


def _attention(pq, pkv, rope, sink, ctx_len, with_ctx):
    t = pq.shape[0]
    n = t - ctx_len
    nb = n // ATT_BLOCK
    step = 2 * ATT_BLOCK
    assert ctx_len == step and nb % 2 == 0
    off = ctx_len // ATT_BLOCK
    half = lambda im: pl.BlockSpec((ATT_BLOCK, 2 * ATT_KV_DIM), im)
    lat = lambda j: 2 * jnp.maximum(j - 1, 0)
    prv = lambda j: (jnp.maximum(lat(j) - 1, 0) + off, 0)
    nxt = lambda j: (jnp.minimum(lat(j) + 2, nb - 1) + off, 0)
    cur = lambda j: (j, 0)
    whole = lambda a: pl.BlockSpec(a.shape, lambda j: (0,) * a.ndim)
    return pl.pallas_call(
        functools.partial(_attn_kernel, nb=nb, with_ctx=with_ctx),
        grid=(1 + nb // 2,),
        in_specs=[
            pl.BlockSpec((step, ATT_DIM), cur),
            half(prv), pl.BlockSpec((step, 2 * ATT_KV_DIM), cur), half(nxt),
            pl.BlockSpec((ctx_len, 2 * ATT_KV_DIM), lambda j: (0, 0)),
        ] + [whole(a) for a in rope] + [pl.BlockSpec((8, LANE), lambda j: (0, 0))],
        out_specs=pl.BlockSpec((step, ATT_DIM), cur),
        out_shape=jax.ShapeDtypeStruct((t, ATT_DIM), F32),
        compiler_params=_params(("arbitrary",), VMEM_LIMIT),
        name="attention",
    )(pq, pkv, pkv, pkv, pkv, *rope, sink)


def _mixfin_kernel(ctx_ref, h_ref, mod_ref, pconv_ref, prev_ref, next_ref, cw_ref, of_ref, ob_ref, z_ref, ng_ref,
                   yc_ref, wout_ref, bd_ref, *rest, nblk, blk0, with_router):
    if with_router:
        wr1_ref, wr0_ref, x_ref, hx_ref, lg_ref = rest
    else:
        x_ref, hx_ref = rest
    i = pl.program_id(0) + blk0
    row = _mod_row(mod_ref, i == 0)
    pc = pconv_ref[...]
    u = pc[:, CONV_CH:2 * CONV_CH] * pc[:, 2 * CONV_CH:]
    prow, nrow = _halo_rows(prev_ref, next_ref, i, nblk)
    prow = prow[:, CONV_CH:2 * CONV_CH] * prow[:, 2 * CONV_CH:]
    nrow = nrow[:, CONV_CH:2 * CONV_CH] * nrow[:, 2 * CONV_CH:]
    up, un = _shift_rows(u, prow, nrow)
    cw = cw_ref[...]
    ya = pc[:, :CONV_CH] * (up * cw[0:1, :] + u * cw[1:2, :] + un * cw[2:3, :])
    o = of_ref[...] + ob_ref[...]
    ms = _group_sum(o * o, bd_ref[...]) * (1.0 / DN_HEAD_DIM)
    z = z_ref[...]
    yb = o * lax.rsqrt(ms + EPS) * ng_ref[...] * (z * _sigmoid(z))
    mix = jnp.concatenate([ya, yb, yc_ref[...]], axis=1)
    x = jnp.where(i == 0, ctx_ref[...], h_ref[...]) + row[:, 2 * D:3 * D] * _mm(mix, wout_ref[...])
    x_ref[...] = x
    hx = _rmsnorm_rows(x) * (1.0 + row[:, 4 * D:5 * D]) + row[:, 3 * D:4 * D]
    hx_ref[...] = hx.astype(hx_ref.dtype)
    if with_router:
        h1, h0 = _split2(hx)
        d = functools.partial(lax.dot_general, dimension_numbers=_NN, preferred_element_type=F32)
        lg = d(h1, wr1_ref[...]) + (d(h0, wr1_ref[...]) + d(h1, wr0_ref[...]))
        lane = lax.broadcasted_iota(jnp.int32, lg.shape, 1)
        lanef = lane.astype(F32)
        lg = jnp.where(lane < N_EXPERTS, lg, -jnp.inf)
        m1 = jnp.max(lg, axis=-1, keepdims=True)
        i1 = jnp.min(jnp.where(lg == m1, lanef, float(LANE)), axis=-1, keepdims=True)
        rest = jnp.where(lanef == i1, -jnp.inf, lg)
        m2 = jnp.max(rest, axis=-1, keepdims=True)
        i2 = jnp.min(jnp.where(rest == m2, lanef, float(LANE)), axis=-1, keepdims=True)
        e2 = jnp.exp(m2 - m1)
        g1 = 1.0 / (1.0 + e2)
        lg_ref[...] = jnp.where(lane == 0, g1, jnp.where(lane == 1, e2 * g1, jnp.where(lane == 2, i1, i2)))


def _mixer_finish(ctx_src, lat_src, lat_blk0, mod, pconv, conv_w, o_f, o_b, pz, norm_g, yc, w_out, bd, ctx_len,
                  with_ctx, router=None):
    t = pconv.shape[0]
    nblk = t // TM
    blk0 = 0 if with_ctx else ctx_len // TM
    rows = t - blk0 * TM
    r8 = TM // 8
    w = pconv.shape[1]
    cur = lambda i: (i + blk0, 0)
    out_cur = lambda i: (i, 0)
    const = lambda i: (0, 0)
    ng = jnp.tile(norm_g.reshape(1, DN_HEAD_DIM), (1, DN_HEADS))
    in_specs = [
        pl.BlockSpec((TM, D), const),
        pl.BlockSpec((TM, D), lambda i: (jnp.maximum(i + blk0 - 1, 0) + lat_blk0, 0)),
        pl.BlockSpec((8, N_MOD * D), const),
        pl.BlockSpec((TM, w), cur),
        pl.BlockSpec((8, w), lambda i: (jnp.maximum((i + blk0) * r8 - 1, 0), 0)),
        pl.BlockSpec((8, w), lambda i: (jnp.minimum((i + blk0 + 1) * r8, t // 8 - 1), 0)),
        pl.BlockSpec((3, CONV_CH), const),
        pl.BlockSpec((TM, DN_DIM), cur),
        pl.BlockSpec((TM, DN_DIM), cur),
        pl.BlockSpec((TM, DN_DIM), cur),
        pl.BlockSpec((1, DN_DIM), const),
        pl.BlockSpec((TM, ATT_DIM), cur),
        pl.BlockSpec((MIX_DIM, D), const),
        pl.BlockSpec(bd.shape, const),
    ]
    args = [ctx_src, lat_src, mod, pconv, pconv, pconv, conv_w, o_f, o_b, pz, ng, yc, w_out, bd]
    out_specs = [pl.BlockSpec((TM, D), out_cur), pl.BlockSpec((TM, D), out_cur)]
    hx_dtype = BF16 if router is None else F32
    out_shape = [jax.ShapeDtypeStruct((rows, D), F32), jax.ShapeDtypeStruct((rows, D), hx_dtype)]
    if router is not None:
        in_specs += [pl.BlockSpec((D, LANE), const)] * 2
        args += list(router)
        out_specs.append(pl.BlockSpec((TM, LANE), out_cur))
        out_shape.append(jax.ShapeDtypeStruct((rows, LANE), F32))
    return pl.pallas_call(
        functools.partial(_mixfin_kernel, nblk=nblk, blk0=blk0, with_router=router is not None),
        grid=(rows // TM,),
        in_specs=in_specs,
        out_specs=out_specs,
        out_shape=out_shape,
        compiler_params=_params(("arbitrary",), VMEM_LIMIT),
        name="mixer_finish",
    )(*args)


def _ffn_kernel(hx_ref, x_ref, mod_ref, wg_ref, wu_ref, wd_ref, o_ref):
    row = _mod_row(mod_ref, pl.program_id(0) == 0)
    hx = hx_ref[...]
    acc = jnp.zeros((hx.shape[0], D), F32)
    for f in range(0, D_FF, FF_CHUNK):
        fe = min(f + FF_CHUNK, D_FF)
        g = _mm(hx, wg_ref[:, f:fe])
        u = _mm(hx, wu_ref[:, f:fe])
        acc = acc + _mm(g * _sigmoid(g) * u, wd_ref[f:fe, :])
    o_ref[...] = x_ref[...] + row[:, 5 * D:6 * D] * acc


def _dense_ffn(hx, x, mod, wg, wu, wd):
    t = x.shape[0]
    const = lambda i: (0, 0)
    once = dict(pipeline_mode=pl.Buffered(1))
    return pl.pallas_call(
        _ffn_kernel,
        grid=(t // TM,),
        in_specs=[
            pl.BlockSpec((TM, D), lambda i: (i, 0)),
            pl.BlockSpec((TM, D), lambda i: (i, 0)),
            pl.BlockSpec((8, N_MOD * D), const),
            pl.BlockSpec((D, D_FF), const, **once),
            pl.BlockSpec((D, D_FF), const, **once),
            pl.BlockSpec((D_FF, D), const, **once),
        ],
        out_specs=pl.BlockSpec((TM, D), lambda i: (i, 0)),
        out_shape=jax.ShapeDtypeStruct((t, D), F32),
        compiler_params=_params(("arbitrary",), VMEM_LIMIT),
        name="dense_ffn",
    )(hx, x, mod, wg, wu, wd)


def _moe_kernel(be_ref, nu_ref, xs_ref, wg_ref, wu_ref, wd_ref, *rest, boff):
    y_ref, acc_ref = rest[-2:]
    b = pl.program_id(0) + boff
    f = pl.program_id(1)

    @pl.when(b < nu_ref[0])
    def _():
        xs = xs_ref[...].astype(BF16)
        part = jnp.zeros((MOE_TM, D), F32)
        for c in range(0, MOE_TF, MOE_SUB):
            g = _mm(xs, wg_ref[0, :, c:c + MOE_SUB])
            u = _mm(xs, wu_ref[0, :, c:c + MOE_SUB])
            part = part + _mm(g * _sigmoid(g) * u, wd_ref[0, c:c + MOE_SUB, :])

        @pl.when(f == 0)
        def _():
            acc_ref[...] = part

        @pl.when(f != 0)
        def _():
            acc_ref[...] += part

        @pl.when(f == pl.num_programs(1) - 1)
        def _():
            y_ref[...] = acc_ref[...]

    @pl.when(b >= nu_ref[0])
    def _():
        y_ref[...] = jnp.zeros(y_ref.shape, F32)


def _moe_experts(xs_parts, blk_e, n_used, wg, wu, wd):
    rows_p = xs_parts[0].shape[0]
    nblk_p = rows_p // MOE_TM
    cap = rows_p * len(xs_parts)
    nf = D_FF_EXPERT // MOE_TF
    y = None
    for part, xs in enumerate(xs_parts):
        boff = part * nblk_p

        def fidx(b, f, nu):
            return jnp.where(b < nu[0], f, nf - 1)

        g = lambda b, boff=boff: b + boff
        in_specs = [
            pl.BlockSpec((MOE_TM, D), lambda b, f, be, nu: (b, 0)),
            pl.BlockSpec((1, D, MOE_TF), lambda b, f, be, nu, g=g: (be[g(b)], 0, fidx(g(b), f, nu))),
            pl.BlockSpec((1, D, MOE_TF), lambda b, f, be, nu, g=g: (be[g(b)], 0, fidx(g(b), f, nu))),
            pl.BlockSpec((1, MOE_TF, D), lambda b, f, be, nu, g=g: (be[g(b)], fidx(g(b), f, nu), 0)),
        ]
        args = [blk_e, n_used, xs, wg, wu, wd]
        aliases = {}
        if y is not None:
            in_specs.append(pl.BlockSpec(memory_space=pl.ANY))
            aliases = {len(args): 0}
            args.append(y)
        grid_spec = pltpu.PrefetchScalarGridSpec(
            num_scalar_prefetch=2,
            grid=(nblk_p, nf),
            in_specs=in_specs,
            out_specs=pl.BlockSpec((MOE_TM, D), lambda b, f, be, nu, g=g: (g(b), 0)),
            scratch_shapes=[pltpu.VMEM((MOE_TM, D), F32)],
        )
        y = pl.pallas_call(
            functools.partial(_moe_kernel, boff=boff),
            grid_spec=grid_spec,
            out_shape=jax.ShapeDtypeStruct((cap, D), F32),
            input_output_aliases=aliases,
            compiler_params=_params(("arbitrary", "arbitrary"), VMEM_LIMIT),
            name="moe_experts",
        )(*args)
    return y


def _moe_route(top_e):
    n = top_e.shape[0]
    a = n * TOP_K
    flat_e = top_e.reshape(a)
    onehot = (flat_e[:, None] == jnp.arange(N_EXPERTS, dtype=flat_e.dtype)[None, :]).astype(jnp.int32)
    counts = jnp.sum(onehot, axis=0)
    padded = (counts + MOE_TM - 1) // MOE_TM * MOE_TM
    pad_ends = jnp.cumsum(padded)
    pad_starts = pad_ends - padded
    dest = jnp.sum(onehot * (jnp.cumsum(onehot, axis=0) - onehot + pad_starts[None, :]), axis=1)
    cap = a + N_EXPERTS * MOE_TM
    nblk = cap // MOE_TM
    row_tok = (jnp.arange(cap, dtype=jnp.int32) % n).at[dest].set(
        jnp.arange(a, dtype=jnp.int32) // TOP_K, unique_indices=True, mode="promise_in_bounds")
    blk_start = jnp.arange(nblk, dtype=jnp.int32) * MOE_TM
    blk_e = jnp.minimum(jnp.sum((pad_ends[None, :] <= blk_start[:, None]).astype(jnp.int32), axis=1),
                        N_EXPERTS - 1)
    n_used = (pad_ends[-1] // MOE_TM).astype(jnp.int32).reshape(1)
    last_e = blk_e[jnp.maximum(n_used[0] - 1, 0)]
    blk_e = jnp.where(jnp.arange(nblk) < n_used[0], blk_e, last_e)
    return dest.reshape(n, TOP_K), row_tok, blk_e, n_used


def _final_kernel(x_ref, y0_ref, y1_ref, gt_ref, mod_ref, fg_ref, *rest):
    o_ref = rest[-1]
    mod = mod_ref[...]
    gt = gt_ref[...]
    f = gt[:, 0:1] * y0_ref[...] + gt[:, 1:2] * y1_ref[...]
    x = x_ref[...] + mod[0:1, 5 * D:6 * D] * f
    o_ref[...] = _rmsnorm_rows(x) * fg_ref[...]


def _moe_combine_final(x, y_parts, gt, mod, final_g):
    n = x.shape[0]
    rows_p = y_parts[0][0].shape[0]
    nblk_p = rows_p // TM
    const = lambda i: (0, 0)
    out = None
    for part, (y0, y1) in enumerate(y_parts):
        glob = lambda i, off=part * nblk_p: (i + off, 0)
        loc = lambda i: (i, 0)
        in_specs = [pl.BlockSpec((TM, D), glob), pl.BlockSpec((TM, D), loc), pl.BlockSpec((TM, D), loc),
                    pl.BlockSpec((TM, LANE), glob), pl.BlockSpec((8, N_MOD * D), const),
                    pl.BlockSpec((1, D), const)]
        args = [x, y0, y1, gt, mod, final_g.reshape(1, D)]
        aliases = {}
        if out is not None:
            in_specs.append(pl.BlockSpec(memory_space=pl.ANY))
            aliases = {len(args): 0}
            args.append(out)
        out = pl.pallas_call(
            _final_kernel,
            grid=(nblk_p,),
            in_specs=in_specs,
            out_specs=pl.BlockSpec((TM, D), glob),
            out_shape=jax.ShapeDtypeStruct((n, D), F32),
            input_output_aliases=aliases,
            compiler_params=_params(("arbitrary",), VMEM_LIMIT),
            name="moe_combine_final",
        )(*args)
    return out


def _rope_tables(n):
    lane = jnp.arange(LANE, dtype=jnp.int32) % HEAD_DIM
    inv = ROPE_BASE ** (-jnp.arange(0, AXIS_DIM, 2, dtype=F32) / AXIS_DIM)
    freq = inv[lane % (AXIS_DIM // 2)]
    row_axis = (lane // AXIS_DIM) == 0
    sign = jnp.where((lane % AXIS_DIM) < AXIS_DIM // 2, -1.0, 1.0)
    reps = ATT_BLOCK // GRID_W

    def tables(count, on_axis):
        ang = jnp.arange(count, dtype=F32)[:, None] * freq[None, :]
        return jnp.where(on_axis, jnp.cos(ang), 0.0), jnp.where(on_axis, jnp.sin(ang) * sign, 0.0)

    by_block = lambda t: jnp.pad(t.reshape(-1, reps, LANE), ((0, 0), (0, 8 - reps), (0, 0)))
    cosr, sinr = tables(n // GRID_W, row_axis)
    cosc, sinc = tables(GRID_W, ~row_axis)
    return by_block(cosr), by_block(sinr), cosc, sinc


def _prep_w_in(w):
    pad = jnp.zeros((D, LANE - N_AB), w.dtype)
    return jnp.concatenate([w[:, :_C_Z], w[:, _C_A:_C_Q], pad, w[:, _C_Z:_C_A], w[:, _C_Q:_C_END]],
                           axis=1).astype(BF16)


def kernel(x, c, ctx, c_ctx, w_mod, b_mod, w_in, w_out, conv_w, dn_conv_w, dn_a_log, dn_dt_bias, dn_norm_g,
           attn_sink, ffn_w_gate, ffn_w_up, ffn_w_down, moe_router, moe_w_gate, moe_w_up, moe_w_down,
           final_norm_g):
    bsz, n, d = x.shape
    ctx_len = ctx.shape[1]
    depth = w_in.shape[0]
    assert bsz == 1 and d == D and ctx_len == TM and n % TM == 0 and n % GRID_W == 0
    rope = _rope_tables(n)
    mods = _mod_vectors(c, c_ctx, w_mod, b_mod)
    bd = _head_blockdiag(DN_DIM, DN_HEAD_DIM)
    stream = (ctx[0], x[0], 0)
    for layer in range(depth):
        last = layer == depth - 1
        mod = mods[layer]
        pconv, pz, pq, pkv, qn, kn, vv, gb = _in_proj(*stream, mod, _prep_w_in(w_in[layer]), dn_conv_w[layer],
                                                      dn_a_log[layer], dn_dt_bias[layer], bd)
        to_bf16 = []
        if layer % 2 == 0:
            to_bf16 += [w[layer // 2] for w in (ffn_w_gate, ffn_w_up, ffn_w_down)]
        if (layer + 1) % 2 == 1 and layer + 1 < depth:
            j = (layer + 1) // 2
            to_bf16 += [w[j].reshape(-1, w.shape[-1]) for w in (moe_w_gate, moe_w_up, moe_w_down)]
        o_f, o_b, *cast = _delta_net(qn, kn, vv, gb, ctx_len, tuple(to_bf16))
        if layer % 2 == 0:
            ffn_bf16, cast = cast[:3], cast[3:]
        if cast:
            moe_bf16 = [c.reshape(w.shape[1:]) for c, w in zip(cast, (moe_w_gate, moe_w_up, moe_w_down))]
        sink = jnp.zeros((8, LANE), F32).at[0, :ATT_HEADS].set(attn_sink[layer])
        yc = _attention(pq, pkv, rope, sink, ctx_len, with_ctx=not last)
        router = None
        if layer % 2 == 1:
            wr = jnp.zeros((D, LANE), F32).at[:, :N_EXPERTS].set(moe_router[layer // 2])
            wr1 = wr.astype(BF16)
            router = (wr1, (wr - wr1.astype(F32)).astype(BF16))
        outs = _mixer_finish(*stream, mod, pconv, conv_w[layer], o_f, o_b, pz, dn_norm_g[layer], yc,
                             w_out[layer].astype(BF16), bd, ctx_len, with_ctx=not last, router=router)
        if layer % 2 == 0:
            assert not last
            x1, hx = outs
            i = layer // 2
            h = _dense_ffn(hx, x1, mod, *ffn_bf16)
            stream = (h, h, ctx_len // TM)
        else:
            assert last
            x1, hx, route = outs
            i = layer // 2
            dest, row_tok, blk_e, n_used = _moe_route(route[:, 2:2 + TOP_K].astype(jnp.int32))
            take = lambda rows_, idx: rows_.at[idx].get(mode="promise_in_bounds")
            parts = lambda a: jnp.split(a, MOE_PARTS, axis=0)
            y = _moe_experts([take(hx, r) for r in parts(row_tok)], blk_e, n_used, *moe_bf16)
            h = _moe_combine_final(x1, [(take(y, dd[:, 0]), take(y, dd[:, 1])) for dd in parts(dest)],
                                   route, mod, final_norm_g)
    return h.reshape(bsz, n, d)
```

```python
import functools

import jax
import jax.numpy as jnp
from jax import lax
from jax.experimental import pallas as pl
from jax.experimental.pallas import tpu as pltpu

F32 = jnp.float32
BF16 = jnp.bfloat16

D = 1024
N_MOD = 6
EPS = 1e-6
NEG = -1e30
GRID_W = 64

CONV_CH = 256
DN_HEADS = 6
DN_HEAD_DIM = 64
DN_DIM = DN_HEADS * DN_HEAD_DIM
DN_CHUNK = 64
DN_SUB = 16
DN_STEP_CHUNKS = 4
ATT_HEADS = 6
ATT_KV_HEADS = 2
ATT_GROUP = ATT_HEADS // ATT_KV_HEADS
HEAD_DIM = 64
ATT_DIM = ATT_HEADS * HEAD_DIM
ATT_KV_DIM = ATT_KV_HEADS * HEAD_DIM
ATT_BLOCK = 128
ROPE_BASE = 10000.0
AXIS_DIM = HEAD_DIM // 2
MIX_DIM = CONV_CH + DN_DIM + ATT_DIM

D_FF = 2816
N_EXPERTS = 8
TOP_K = 2
D_FF_EXPERT = 3584

TM = 256
FF_CHUNK = 512
MOE_TM = 512
MOE_TF = 1792
MOE_SUB = 256
MOE_FIRST_PART = 4
LANE = 128
VMEM_LIMIT = 56 * 1024 * 1024

_C_QKV = 3 * CONV_CH
_C_Z = _C_QKV + 3 * DN_DIM
_C_A = _C_Z + DN_DIM
_C_Q = _C_A + 4 * DN_HEADS
_C_K = _C_Q + ATT_DIM
_C_V = _C_K + ATT_KV_DIM
_C_END = _C_V + ATT_KV_DIM
N_AB = 4 * DN_HEADS


def _params(sem=None, vmem=None):
    kw = {}
    if sem is not None:
        kw["dimension_semantics"] = sem
    if vmem is not None:
        kw["vmem_limit_bytes"] = vmem
    return pltpu.CompilerParams(**kw)


def _split2(a):
    hi = a.astype(BF16)
    lo = (a - hi.astype(F32)).astype(BF16)
    return hi, lo


def _split3(a):
    hi = a.astype(BF16)
    r = a - hi.astype(F32)
    mid = r.astype(BF16)
    lo = (r - mid.astype(F32)).astype(BF16)
    return hi, mid, lo


_NN = (((1,), (0,)), ((), ()))
_NT = (((1,), (1,)), ((), ()))
_TN = (((0,), (0,)), ((), ()))


def _mm(a, b, dims=_NN):
    return lax.dot_general(a.astype(BF16), b.astype(BF16), dims, preferred_element_type=F32)


def _mm3(a, b, dims=_NN):
    a1, a0 = _split2(a)
    b1, b0 = _split2(b)
    d = functools.partial(lax.dot_general, dimension_numbers=dims, preferred_element_type=F32)
    return d(a1, b1) + (d(a1, b0) + d(a0, b1))


_BNN = (((2,), (1,)), ((0,), (0,)))
_BNT = (((2,), (2,)), ((0,), (0,)))
_BTN = (((1,), (1,)), ((0,), (0,)))


def _bmm(a, b, dims=_BNN):
    return lax.dot_general(a.astype(BF16), b.astype(BF16), dims, preferred_element_type=F32)


def _sigmoid(x):
    return 1.0 / (1.0 + jnp.exp(-x))


def _softplus(x):
    return jnp.maximum(x, 0.0) + jnp.log1p(jnp.exp(-jnp.abs(x)))


def _mod_row(mod_ref, is_ctx):
    mod = mod_ref[...]
    return jnp.where(is_ctx, mod[1:2, :], mod[0:1, :])


def _rmsnorm_rows(x):
    return x * lax.rsqrt(jnp.mean(x * x, axis=-1, keepdims=True) + EPS)


def _shift_rows(u, prow, nrow):
    n = u.shape[0]
    rid = lax.broadcasted_iota(jnp.int32, u.shape, 0)
    up = jnp.where(rid == 0, prow, pltpu.roll(u, 1, 0))
    un = jnp.where(rid == n - 1, nrow, pltpu.roll(u, n - 1, 0))
    return up, un


def _same_group(shape, group):
    sh = group.bit_length() - 1
    assert 1 << sh == group
    return (lax.broadcasted_iota(jnp.int32, shape, 0) >> sh) == (lax.broadcasted_iota(jnp.int32, shape, 1) >> sh)


def _head_blockdiag(n, group):
    g = jnp.arange(n, dtype=jnp.int32) // group
    return (g[:, None] == g[None, :]).astype(BF16)


def _group_sum(t, bd):
    return _mm(t, bd)


MOD_TN = 1536


def _mod_kernel(s_ref, w_ref, b_ref, o_ref):
    s = s_ref[...]
    s = s * _sigmoid(s)
    o_ref[0] = _mm3(s, w_ref[0]) + b_ref[0]


def _mod_vectors(c, c_ctx, w_mod, b_mod):
    depth = w_mod.shape[0]
    s = jnp.zeros((8, D), F32).at[0].set(c[0]).at[1].set(c_ctx)
    return pl.pallas_call(
        _mod_kernel,
        grid=(depth, N_MOD * D // MOD_TN),
        in_specs=[
            pl.BlockSpec((8, D), lambda l, j: (0, 0)),
            pl.BlockSpec((1, D, MOD_TN), lambda l, j: (l, 0, j)),
            pl.BlockSpec((1, 1, MOD_TN), lambda l, j: (l, 0, j)),
        ],
        out_specs=pl.BlockSpec((1, 8, MOD_TN), lambda l, j: (l, 0, j)),
        out_shape=jax.ShapeDtypeStruct((depth, 8, N_MOD * D), F32),
        compiler_params=_params(("arbitrary", "arbitrary"), VMEM_LIMIT),
        name="mod_vectors",
    )(s, w_mod, b_mod.reshape(depth, 1, N_MOD * D))


def _halo_valid(i, nblk):
    return jnp.logical_and(i != 0, i != 1), jnp.logical_and(i != 0, i != nblk - 1)


def _halo_rows(prev_ref, next_ref, i, nblk):
    pvalid, nvalid = _halo_valid(i, nblk)
    prow = jnp.where(pvalid, prev_ref[7:8, :], 0.0)
    nrow = jnp.where(nvalid, next_ref[0:1, :], 0.0)
    return prow, nrow


def _in_kernel(ctx_ref, h_ref, hprev_ref, hnext_ref, mod_ref, w_ref, cw_ref, alog_ref, dtb_ref, bd_ref,
               pconv_ref, pz_ref, pq_ref, pkv_ref, q_ref, k_ref, v_ref, gb_ref, *, nblk):
    i = pl.program_id(0)
    row = _mod_row(mod_ref, i == 0)
    norm_mod = lambda x: _rmsnorm_rows(x) * (1.0 + row[:, D:2 * D]) + row[:, 0:D]
    h1 = norm_mod(jnp.where(i == 0, ctx_ref[...], h_ref[...])).astype(BF16)
    halo = norm_mod(jnp.concatenate([hprev_ref[...], hnext_ref[...]], axis=0)).astype(BF16)
    d = functools.partial(lax.dot_general, dimension_numbers=_NN, preferred_element_type=F32)
    tm = h1.shape[0]
    c0 = 3 * CONV_CH
    c1 = c0 + 3 * DN_DIM
    c2 = c1 + LANE
    c3 = c2 + DN_DIM + ATT_DIM
    pconv_ref[...] = d(h1, w_ref[:, 0:c0])
    zq = d(h1, w_ref[:, c2:c3])
    pz_ref[...] = zq[:, 0:DN_DIM]
    pq_ref[...] = zq[:, DN_DIM:]
    pkv_ref[...] = d(h1, w_ref[:, c3:c3 + 2 * ATT_KV_DIM])
    qkv_ab = d(jnp.concatenate([h1, halo], axis=0), w_ref[:, c0:c2])
    qkv = qkv_ab[:, 0:3 * DN_DIM]
    ab = qkv_ab[0:tm, 3 * DN_DIM:]

    u = qkv[0:tm, :]
    pvalid, nvalid = _halo_valid(i, nblk)
    prow = jnp.where(pvalid, qkv[tm + 7:tm + 8, :], 0.0)
    nrow = jnp.where(nvalid, qkv[tm + 8:tm + 9, :], 0.0)
    up, un = _shift_rows(u, prow, nrow)
    cw = cw_ref[...]
    y = up * cw[0:1, :] + u * cw[1:2, :] + un * cw[2:3, :]
    y = y * _sigmoid(y)
    q = y[:, 0:DN_DIM]
    k = y[:, DN_DIM:2 * DN_DIM]
    bd = bd_ref[...]
    q_ref[...] = q * lax.rsqrt(_group_sum(q * q, bd) + 1e-6) * (DN_HEAD_DIM ** -0.5)
    k_ref[...] = k * lax.rsqrt(_group_sum(k * k, bd) + 1e-6)
    v_ref[...] = y[:, 2 * DN_DIM:3 * DN_DIM]
    g = -jnp.exp(alog_ref[...]) * _softplus(ab + dtb_ref[...])
    lane = lax.broadcasted_iota(jnp.int32, ab.shape, 1)
    gb_ref[...] = jnp.where(lane < 2 * DN_HEADS, g, _sigmoid(ab))


def _in_proj(ctx_src, lat_src, lat_blk0, mod, w_main, dn_conv_w, a_log, dt_bias, bd):
    r8 = TM // 8
    last8 = lat_src.shape[0] // 8 - 1
    nblk = 1 + lat_src.shape[0] // TM - lat_blk0
    t = nblk * TM
    alog = jnp.zeros((1, LANE), F32).at[0, :2 * DN_HEADS].set(a_log.reshape(-1))
    dtb = jnp.zeros((1, LANE), F32).at[0, :2 * DN_HEADS].set(dt_bias.reshape(-1))
    widths = (3 * CONV_CH, DN_DIM, ATT_DIM, 2 * ATT_KV_DIM, DN_DIM, DN_DIM, DN_DIM, LANE)
    const = lambda i: (0, 0)
    lat = lambda i: jnp.maximum(i - 1, 0) + lat_blk0
    return pl.pallas_call(
        functools.partial(_in_kernel, nblk=nblk),
        grid=(nblk,),
        in_specs=[
            pl.BlockSpec((TM, D), const),
            pl.BlockSpec((TM, D), lambda i: (lat(i), 0)),
            pl.BlockSpec((8, D), lambda i: (jnp.maximum(lat(i) * r8 - 1, 0), 0)),
            pl.BlockSpec((8, D), lambda i: (jnp.minimum((lat(i) + 1) * r8, last8), 0)),
            pl.BlockSpec((8, N_MOD * D), const),
            pl.BlockSpec(w_main.shape, const),
            pl.BlockSpec((3, 3 * DN_DIM), const),
            pl.BlockSpec((1, LANE), const),
            pl.BlockSpec((1, LANE), const),
            pl.BlockSpec(bd.shape, const),
        ],
        out_specs=[pl.BlockSpec((TM, w), lambda i: (i, 0)) for w in widths],
        out_shape=[jax.ShapeDtypeStruct((t, w), F32) for w in widths],
        compiler_params=_params(("arbitrary",), VMEM_LIMIT),
        name="in_proj",
    )(ctx_src, lat_src, lat_src, lat_src, mod, w_main, dn_conv_w, alog, dtb, bd)


def _dn_chunk(rev, q_ref, k_ref, v_ref, gb_ref, o_ref, s_ref):
    c_ = DN_CHUNK
    ri = lax.broadcasted_iota(jnp.int32, (c_, c_), 0)
    ci = lax.broadcasted_iota(jnp.int32, (c_, c_), 1)
    incl = (ri <= ci) if rev else (ri >= ci)
    strict = (ri < ci) if rev else (ri > ci)
    same_sub = _same_group((c_, c_), DN_SUB)
    eye = jnp.where(ri == ci, 1.0, 0.0)
    tri = jnp.where(incl, 1.0, 0.0).astype(BF16)
    last = 0 if rev else c_ - 1

    nchunks = q_ref.shape[0] // c_
    nh = DN_HEADS
    col0 = nh if rev else 0
    rows = lambda g: slice(g * c_, (g + 1) * c_)
    lanes = lambda h: slice(h * DN_HEAD_DIM, (h + 1) * DN_HEAD_DIM)

    def stack(fn):
        return jnp.stack([fn(g, h) for g in range(nchunks) for h in range(nh)])

    gb = gb_ref[...]
    gcs = [_cumsum_rows(tri, gb[rows(g), :]) for g in range(nchunks)]
    gcts = [gc.T for gc in gcs]
    q = stack(lambda g, h: q_ref[rows(g), lanes(h)])
    k = stack(lambda g, h: k_ref[rows(g), lanes(h)])
    v = stack(lambda g, h: v_ref[rows(g), lanes(h)])
    gcol = stack(lambda g, h: gcs[g][:, col0 + h:col0 + h + 1])
    grow = stack(lambda g, h: gcts[g][col0 + h:col0 + h + 1, :])
    beta = stack(lambda g, h: gb[rows(g), 2 * nh + col0 + h:2 * nh + col0 + h + 1])
    glast = gcol[:, last:last + 1, :]
    decay = jnp.where(incl, jnp.exp(jnp.where(incl, gcol - grow, 0.0)), 0.0)
    eg = jnp.exp(gcol)
    kb = k * beta
    a = jnp.where(strict, _bmm(kb, k, _BNT) * decay, 0.0)
    qk = jnp.where(incl, _bmm(q, k, _BNT) * decay, 0.0)
    ad = jnp.where(same_sub, a, 0.0)
    ao = a - ad
    p = eye - ad
    n2 = _bmm(ad, ad)
    p = p + _bmm(p, n2)
    n4 = _bmm(n2, n2)
    p = p + _bmm(p, n4)
    n8 = _bmm(n4, n4)
    dinv = p + _bmm(p, n8)
    m = _bmm(dinv, ao)
    m2 = _bmm(m, m)
    y = _bmm(dinv, jnp.concatenate([v * beta, kb * eg], axis=-1))
    z = y + _bmm(m2, y)
    x = z - _bmm(m, z)
    u = x[:, :, :DN_HEAD_DIM]
    w = x[:, :, DN_HEAD_DIM:]
    qg = q * eg
    kd = k * jnp.exp(glast - gcol)
    gl = jnp.exp(glast)
    s = s_ref[col0:col0 + nh]
    for g in (reversed(range(nchunks)) if rev else range(nchunks)):
        b = slice(g * nh, (g + 1) * nh)
        v_new = u[b] - _bmm(w[b], s)
        o = _bmm(qg[b], s) + _bmm(qk[b], v_new)
        s = s * gl[b] + _bmm(kd[b], v_new, _BTN)
        for h in range(nh):
            o_ref[rows(g), lanes(h)] = o[h]
    s_ref[col0:col0 + nh] = s


def _cumsum_rows(tri_bf16, g):
    g2, g1, g0 = _split3(g)
    d = functools.partial(lax.dot_general, dimension_numbers=_NN, preferred_element_type=F32)
    return d(tri_bf16, g2) + (d(tri_bf16, g1) + d(tri_bf16, g0))


def _dn_kernel(qf, kf, vf, gf, qb, kb, vb, gbb, *rest, n_cast):
    cast_in = rest[:n_cast]
    of_ref, ob_ref = rest[n_cast:n_cast + 2]
    cast_out = rest[n_cast + 2:2 * n_cast + 2]
    s_ref = rest[-1]

    @pl.when(pl.program_id(0) == 0)
    def _():
        s_ref[...] = jnp.zeros(s_ref.shape, F32)

    _dn_chunk(False, qf, kf, vf, gf, of_ref, s_ref)
    _dn_chunk(True, qb, kb, vb, gbb, ob_ref, s_ref)
    for src, dst in zip(cast_in, cast_out):
        dst[...] = src[...].astype(BF16)


def _delta_net(q, k, v, gb, ctx_len, to_bf16=()):
    t = q.shape[0]
    rows = DN_STEP_CHUNKS * DN_CHUNK
    assert ctx_len == rows and t % rows == 0
    nstep = t // rows
    cast_specs = []
    for m in to_bf16:
        rb = -(-m.shape[0] // nstep)
        rb = -(-rb // 16) * 16
        nb = -(-m.shape[0] // rb)
        cast_specs.append(pl.BlockSpec((rb, m.shape[1]), lambda s, nb=nb: (jnp.minimum(s, nb - 1), 0)))

    def fwd(s):
        return (s, 0)

    def bwd(s):
        return (jnp.where(s == 0, 0, nstep - s), 0)

    wide = lambda im: pl.BlockSpec((rows, DN_DIM), im)
    narrow = lambda im: pl.BlockSpec((rows, LANE), im)
    return pl.pallas_call(
        functools.partial(_dn_kernel, n_cast=len(to_bf16)),
        grid=(nstep,),
        in_specs=[wide(fwd), wide(fwd), wide(fwd), narrow(fwd), wide(bwd), wide(bwd), wide(bwd), narrow(bwd)]
        + cast_specs,
        out_specs=[wide(fwd), wide(bwd)] + cast_specs,
        out_shape=[jax.ShapeDtypeStruct((t, DN_DIM), F32)] * 2
        + [jax.ShapeDtypeStruct(m.shape, BF16) for m in to_bf16],
        scratch_shapes=[pltpu.VMEM((2 * DN_HEADS, DN_HEAD_DIM, DN_HEAD_DIM), F32)],
        compiler_params=_params(("arbitrary",), VMEM_LIMIT),
        name="delta_net",
    )(q, k, v, gb, q, k, v, gb, *to_bf16)


def _rope(x, cos, sin):
    w = x.shape[1]
    lane = lax.broadcasted_iota(jnp.int32, x.shape, 1)
    first_half = (lane & (AXIS_DIM - 1)) < (AXIS_DIM // 2)
    swapped = jnp.where(first_half, pltpu.roll(x, w - AXIS_DIM // 2, 1), pltpu.roll(x, AXIS_DIM // 2, 1))
    return x * cos + swapped * sin


LOG2E = 1.4426950408889634


def _softmax_av(s, sink, vals):
    m = jnp.maximum(jnp.max(s, axis=-1, keepdims=True), sink)
    p = jnp.exp2(s - m)
    denom = jnp.sum(p, axis=-1, keepdims=True) + jnp.exp2(sink - m)
    return _mm(p, vals) / denom


def _attend(q, keys, vals, band, sink_all, o_ref, row0=0):
    b = q.shape[0]
    for kvh in range(ATT_KV_HEADS):
        kl = slice(kvh * HEAD_DIM, (kvh + 1) * HEAD_DIM)
        heads = range(kvh * ATT_GROUP, (kvh + 1) * ATT_GROUP)
        qs = jnp.concatenate([q[:, h * HEAD_DIM:(h + 1) * HEAD_DIM] for h in heads], axis=0)
        sink = jnp.concatenate([jnp.broadcast_to(sink_all[0:1, h:h + 1], (b, 1)) for h in heads], axis=0)
        s = _mm(qs, keys[:, kl], _NT)
        if band is not None:
            kb = ATT_BLOCK
            s = jnp.concatenate([jnp.where(band[0], s[:, 0:kb], NEG), s[:, kb:2 * kb],
                                 jnp.where(band[1], s[:, 2 * kb:3 * kb], NEG), s[:, 3 * kb:]], axis=1)
        o = _softmax_av(s, sink * LOG2E, vals[:, kl])
        for g, h in enumerate(heads):
            o_ref[row0:row0 + b, h * HEAD_DIM:(h + 1) * HEAD_DIM] = o[g * b:(g + 1) * b, :]


def _band_valid(first, last):
    b = ATT_BLOCK
    c = lax.broadcasted_iota(jnp.int32, (1, b), 1)
    r = lax.broadcasted_iota(jnp.int32, (ATT_GROUP * b, 1), 0) & (b - 1)
    prev_ok = jnp.where(first, -1, c) >= r
    next_ok = jnp.where(last, b, c) <= r
    return prev_ok, next_ok


def _rope_block(rowtab_ref, coltab_ref, blk):
    rt = rowtab_ref[blk]
    ct = coltab_ref[...]
    reps = ATT_BLOCK // GRID_W
    rows = jnp.concatenate([jnp.broadcast_to(rt[g:g + 1, :], (GRID_W, LANE)) for g in range(reps)], axis=0)
    return rows + jnp.concatenate([ct] * reps, axis=0)


def _attn_kernel(q_ref, kp_ref, kc_ref, kn_ref, kctx_ref, cosr_ref, sinr_ref, cosc_ref, sinc_ref, sink_ref, o_ref,
                 *, nb, with_ctx):
    j = pl.program_id(0)
    b = ATT_BLOCK
    scale = HEAD_DIM ** -0.5 * LOG2E

    @pl.when(j == 0)
    def _():
        if with_ctx:
            kvx = kctx_ref[...]
            _attend(q_ref[...] * scale, kvx[:, :ATT_KV_DIM], kvx[:, ATT_KV_DIM:], None, sink_ref[...], o_ref)
        else:
            o_ref[...] = jnp.zeros(o_ref.shape, F32)

    @pl.when(j > 0)
    def _():
        b0 = 2 * (j - 1)
        blocks = (jnp.maximum(b0 - 1, 0), b0, b0 + 1, jnp.minimum(b0 + 2, nb - 1))
        cos = [_rope_block(cosr_ref, cosc_ref, blk) for blk in blocks]
        sin = [_rope_block(sinr_ref, sinc_ref, blk) for blk in blocks]
        kvc = kc_ref[...]
        kv = (kp_ref[...], kvc[0:b, :], kvc[b:2 * b, :], kn_ref[...])
        kvx = kctx_ref[...]
        keys = [_rope(t[:, :ATT_KV_DIM], c_, s_) for t, c_, s_ in zip(kv, cos, sin)]
        q_all = q_ref[...]
        sink_all = sink_ref[...]
        for sub in range(2):
            q = q_all[sub * b:(sub + 1) * b, :]
            q = jnp.concatenate([_rope(q[:, l * LANE:(l + 1) * LANE], cos[1 + sub], sin[1 + sub])
                                 for l in range(ATT_DIM // LANE)], axis=1)
            kcat = jnp.concatenate(keys[sub:sub + 3] + [kvx[:, :ATT_KV_DIM]], axis=0)
            vcat = jnp.concatenate([t[:, ATT_KV_DIM:] for t in kv[sub:sub + 3]] + [kvx[:, ATT_KV_DIM:]], axis=0)
            band = _band_valid(first=(b0 + sub == 0), last=(b0 + sub == nb - 1))
            _attend(q * scale, kcat, vcat, band, sink_all, o_ref, row0=sub * b)


def _attention(pq, pkv, rope, sink, ctx_len, with_ctx):
    t = pq.shape[0]
    n = t - ctx_len
    nb = n // ATT_BLOCK
    step = 2 * ATT_BLOCK
    assert ctx_len == step and nb % 2 == 0
    off = ctx_len // ATT_BLOCK
    half = lambda im: pl.BlockSpec((ATT_BLOCK, 2 * ATT_KV_DIM), im)
    lat = lambda j: 2 * jnp.maximum(j - 1, 0)
    prv = lambda j: (jnp.maximum(lat(j) - 1, 0) + off, 0)
    nxt = lambda j: (jnp.minimum(lat(j) + 2, nb - 1) + off, 0)
    cur = lambda j: (j, 0)
    whole = lambda a: pl.BlockSpec(a.shape, lambda j: (0,) * a.ndim)
    return pl.pallas_call(
        functools.partial(_attn_kernel, nb=nb, with_ctx=with_ctx),
        grid=(1 + nb // 2,),
        in_specs=[
            pl.BlockSpec((step, ATT_DIM), cur),
            half(prv), pl.BlockSpec((step, 2 * ATT_KV_DIM), cur), half(nxt),
            pl.BlockSpec((ctx_len, 2 * ATT_KV_DIM), lambda j: (0, 0)),
        ] + [whole(a) for a in rope] + [pl.BlockSpec((8, LANE), lambda j: (0, 0))],
        out_specs=pl.BlockSpec((step, ATT_DIM), cur),
        out_shape=jax.ShapeDtypeStruct((t, ATT_DIM), F32),
        compiler_params=_params(("arbitrary",), VMEM_LIMIT),
        name="attention",
    )(pq, pkv, pkv, pkv, pkv, *rope, sink)


def _mixfin_kernel(ctx_ref, h_ref, mod_ref, pconv_ref, prev_ref, next_ref, cw_ref, of_ref, ob_ref, z_ref, ng_ref,
                   yc_ref, wout_ref, bd_ref, *rest, nblk, blk0, with_router):
    if with_router:
        wr1_ref, wr0_ref, x_ref, hx_ref, lg_ref = rest
    else:
        x_ref, hx_ref = rest
    i = pl.program_id(0) + blk0
    row = _mod_row(mod_ref, i == 0)
    pc = pconv_ref[...]
    u = pc[:, CONV_CH:2 * CONV_CH] * pc[:, 2 * CONV_CH:]
    prow, nrow = _halo_rows(prev_ref, next_ref, i, nblk)
    prow = prow[:, CONV_CH:2 * CONV_CH] * prow[:, 2 * CONV_CH:]
    nrow = nrow[:, CONV_CH:2 * CONV_CH] * nrow[:, 2 * CONV_CH:]
    up, un = _shift_rows(u, prow, nrow)
    cw = cw_ref[...]
    ya = pc[:, :CONV_CH] * (up * cw[0:1, :] + u * cw[1:2, :] + un * cw[2:3, :])
    o = of_ref[...] + ob_ref[...]
    ms = _group_sum(o * o, bd_ref[...]) * (1.0 / DN_HEAD_DIM)
    z = z_ref[...]
    yb = o * lax.rsqrt(ms + EPS) * ng_ref[...] * (z * _sigmoid(z))
    mix = jnp.concatenate([ya, yb, yc_ref[...]], axis=1)
    x = jnp.where(i == 0, ctx_ref[...], h_ref[...]) + row[:, 2 * D:3 * D] * _mm(mix, wout_ref[...])
    x_ref[...] = x
    hx = _rmsnorm_rows(x) * (1.0 + row[:, 4 * D:5 * D]) + row[:, 3 * D:4 * D]
    hx_ref[...] = hx.astype(hx_ref.dtype)
    if with_router:
        h1, h0 = _split2(hx)
        d = functools.partial(lax.dot_general, dimension_numbers=_NN, preferred_element_type=F32)
        lg = d(h1, wr1_ref[...]) + (d(h0, wr1_ref[...]) + d(h1, wr0_ref[...]))
        lane = lax.broadcasted_iota(jnp.int32, lg.shape, 1)
        lanef = lane.astype(F32)
        lg = jnp.where(lane < N_EXPERTS, lg, -jnp.inf)
        m1 = jnp.max(lg, axis=-1, keepdims=True)
        i1 = jnp.min(jnp.where(lg == m1, lanef, float(LANE)), axis=-1, keepdims=True)
        rest = jnp.where(lanef == i1, -jnp.inf, lg)
        m2 = jnp.max(rest, axis=-1, keepdims=True)
        i2 = jnp.min(jnp.where(rest == m2, lanef, float(LANE)), axis=-1, keepdims=True)
        e2 = jnp.exp(m2 - m1)
        g1 = 1.0 / (1.0 + e2)
        lg_ref[...] = jnp.where(lane == 0, g1, jnp.where(lane == 1, e2 * g1, jnp.where(lane == 2, i1, i2)))


def _mixer_finish(ctx_src, lat_src, lat_blk0, mod, pconv, conv_w, o_f, o_b, pz, norm_g, yc, w_out, bd, ctx_len,
                  with_ctx, router=None):
    t = pconv.shape[0]
    nblk = t // TM
    blk0 = 0 if with_ctx else ctx_len // TM
    rows = t - blk0 * TM
    r8 = TM // 8
    w = pconv.shape[1]
    cur = lambda i: (i + blk0, 0)
    out_cur = lambda i: (i, 0)
    const = lambda i: (0, 0)
    ng = jnp.tile(norm_g.reshape(1, DN_HEAD_DIM), (1, DN_HEADS))
    in_specs = [
        pl.BlockSpec((TM, D), const),
        pl.BlockSpec((TM, D), lambda i: (jnp.maximum(i + blk0 - 1, 0) + lat_blk0, 0)),
        pl.BlockSpec((8, N_MOD * D), const),
        pl.BlockSpec((TM, w), cur),
        pl.BlockSpec((8, w), lambda i: (jnp.maximum((i + blk0) * r8 - 1, 0), 0)),
        pl.BlockSpec((8, w), lambda i: (jnp.minimum((i + blk0 + 1) * r8, t // 8 - 1), 0)),
        pl.BlockSpec((3, CONV_CH), const),
        pl.BlockSpec((TM, DN_DIM), cur),
        pl.BlockSpec((TM, DN_DIM), cur),
        pl.BlockSpec((TM, DN_DIM), cur),
        pl.BlockSpec((1, DN_DIM), const),
        pl.BlockSpec((TM, ATT_DIM), cur),
        pl.BlockSpec((MIX_DIM, D), const),
        pl.BlockSpec(bd.shape, const),
    ]
    args = [ctx_src, lat_src, mod, pconv, pconv, pconv, conv_w, o_f, o_b, pz, ng, yc, w_out, bd]
    out_specs = [pl.BlockSpec((TM, D), out_cur), pl.BlockSpec((TM, D), out_cur)]
    hx_dtype = BF16 if router is None else F32
    out_shape = [jax.ShapeDtypeStruct((rows, D), F32), jax.ShapeDtypeStruct((rows, D), hx_dtype)]
    if router is not None:
        in_specs += [pl.BlockSpec((D, LANE), const)] * 2
        args += list(router)
        out_specs.append(pl.BlockSpec((TM, LANE), out_cur))
        out_shape.append(jax.ShapeDtypeStruct((rows, LANE), F32))
    return pl.pallas_call(
        functools.partial(_mixfin_kernel, nblk=nblk, blk0=blk0, with_router=router is not None),
        grid=(rows // TM,),
        in_specs=in_specs,
        out_specs=out_specs,
        out_shape=out_shape,
        compiler_params=_params(("arbitrary",), VMEM_LIMIT),
        name="mixer_finish",
    )(*args)


def _ffn_kernel(hx_ref, x_ref, mod_ref, wg_ref, wu_ref, wd_ref, o_ref):
    row = _mod_row(mod_ref, pl.program_id(0) == 0)
    hx = hx_ref[...]
    acc = jnp.zeros((hx.shape[0], D), F32)
    for f in range(0, D_FF, FF_CHUNK):
        fe = min(f + FF_CHUNK, D_FF)
        g = _mm(hx, wg_ref[:, f:fe])
        u = _mm(hx, wu_ref[:, f:fe])
        acc = acc + _mm(g * _sigmoid(g) * u, wd_ref[f:fe, :])
    o_ref[...] = x_ref[...] + row[:, 5 * D:6 * D] * acc


def _dense_ffn(hx, x, mod, wg, wu, wd):
    t = x.shape[0]
    const = lambda i: (0, 0)
    once = dict(pipeline_mode=pl.Buffered(1))
    return pl.pallas_call(
        _ffn_kernel,
        grid=(t // TM,),
        in_specs=[
            pl.BlockSpec((TM, D), lambda i: (i, 0)),
            pl.BlockSpec((TM, D), lambda i: (i, 0)),
            pl.BlockSpec((8, N_MOD * D), const),
            pl.BlockSpec((D, D_FF), const, **once),
            pl.BlockSpec((D, D_FF), const, **once),
            pl.BlockSpec((D_FF, D), const, **once),
        ],
        out_specs=pl.BlockSpec((TM, D), lambda i: (i, 0)),
        out_shape=jax.ShapeDtypeStruct((t, D), F32),
        compiler_params=_params(("arbitrary",), VMEM_LIMIT),
        name="dense_ffn",
    )(hx, x, mod, wg, wu, wd)


def _moe_kernel(be_ref, nu_ref, xs_ref, wg_ref, wu_ref, wd_ref, *rest, boff):
    y_ref, acc_ref = rest[-2:]
    b = pl.program_id(0) + boff
    f = pl.program_id(1)

    @pl.when(b < nu_ref[0])
    def _():
        xs = xs_ref[...].astype(BF16)
        part = jnp.zeros((MOE_TM, D), F32)
        for c in range(0, MOE_TF, MOE_SUB):
            g = _mm(xs, wg_ref[0, :, c:c + MOE_SUB])
            u = _mm(xs, wu_ref[0, :, c:c + MOE_SUB])
            part = part + _mm(g * _sigmoid(g) * u, wd_ref[0, c:c + MOE_SUB, :])

        @pl.when(f == 0)
        def _():
            acc_ref[...] = part

        @pl.when(f != 0)
        def _():
            acc_ref[...] += part

        @pl.when(f == pl.num_programs(1) - 1)
        def _():
            y_ref[...] = acc_ref[...]

    @pl.when(b >= nu_ref[0])
    def _():
        y_ref[...] = jnp.zeros(y_ref.shape, F32)


def _moe_experts(xs_parts, blk_e, n_used, wg, wu, wd):
    cap = sum(xs.shape[0] for xs in xs_parts)
    nf = D_FF_EXPERT // MOE_TF
    y = None
    boff = 0
    for xs in xs_parts:
        nblk_p = xs.shape[0] // MOE_TM

        def fidx(b, f, nu):
            return jnp.where(b < nu[0], f, nf - 1)

        g = lambda b, boff=boff: b + boff
        in_specs = [
            pl.BlockSpec((MOE_TM, D), lambda b, f, be, nu: (b, 0)),
            pl.BlockSpec((1, D, MOE_TF), lambda b, f, be, nu, g=g: (be[g(b)], 0, fidx(g(b), f, nu))),
            pl.BlockSpec((1, D, MOE_TF), lambda b, f, be, nu, g=g: (be[g(b)], 0, fidx(g(b), f, nu))),
            pl.BlockSpec((1, MOE_TF, D), lambda b, f, be, nu, g=g: (be[g(b)], fidx(g(b), f, nu), 0)),
        ]
        args = [blk_e, n_used, xs, wg, wu, wd]
        aliases = {}
        if y is not None:
            in_specs.append(pl.BlockSpec(memory_space=pl.ANY))
            aliases = {len(args): 0}
            args.append(y)
        grid_spec = pltpu.PrefetchScalarGridSpec(
            num_scalar_prefetch=2,
            grid=(nblk_p, nf),
            in_specs=in_specs,
            out_specs=pl.BlockSpec((MOE_TM, D), lambda b, f, be, nu, g=g: (g(b), 0)),
            scratch_shapes=[pltpu.VMEM((MOE_TM, D), F32)],
        )
        y = pl.pallas_call(
            functools.partial(_moe_kernel, boff=boff),
            grid_spec=grid_spec,
            out_shape=jax.ShapeDtypeStruct((cap, D), F32),
            input_output_aliases=aliases,
            compiler_params=_params(("arbitrary", "arbitrary"), VMEM_LIMIT),
            name="moe_experts",
        )(*args)
        boff += nblk_p
    return y


def _moe_route(top_e):
    n = top_e.shape[0]
    a = n * TOP_K
    flat_e = top_e.reshape(a)
    onehot = (flat_e[:, None] == jnp.arange(N_EXPERTS, dtype=flat_e.dtype)[None, :]).astype(jnp.int32)
    counts = jnp.sum(onehot, axis=0)
    padded = (counts + MOE_TM - 1) // MOE_TM * MOE_TM
    pad_ends = jnp.cumsum(padded)
    pad_starts = pad_ends - padded
    dest = jnp.sum(onehot * (jnp.cumsum(onehot, axis=0) - onehot + pad_starts[None, :]), axis=1)
    cap = a + N_EXPERTS * MOE_TM
    nblk = cap // MOE_TM
    row_tok = (jnp.arange(cap, dtype=jnp.int32) % n).at[dest].set(
        jnp.arange(a, dtype=jnp.int32) // TOP_K, unique_indices=True, mode="promise_in_bounds")
    blk_start = jnp.arange(nblk, dtype=jnp.int32) * MOE_TM
    blk_e = jnp.minimum(jnp.sum((pad_ends[None, :] <= blk_start[:, None]).astype(jnp.int32), axis=1),
                        N_EXPERTS - 1)
    n_used = (pad_ends[-1] // MOE_TM).astype(jnp.int32).reshape(1)
    last_e = blk_e[jnp.maximum(n_used[0] - 1, 0)]
    blk_e = jnp.where(jnp.arange(nblk) < n_used[0], blk_e, last_e)
    return dest.reshape(n, TOP_K), row_tok, blk_e, n_used


def _final_kernel(x_ref, y0_ref, y1_ref, gt_ref, mod_ref, fg_ref, *rest):
    o_ref = rest[-1]
    mod = mod_ref[...]
    gt = gt_ref[...]
    f = gt[:, 0:1] * y0_ref[...] + gt[:, 1:2] * y1_ref[...]
    x = x_ref[...] + mod[0:1, 5 * D:6 * D] * f
    o_ref[...] = _rmsnorm_rows(x) * fg_ref[...]


def _moe_combine_final(x, y_parts, gt, mod, final_g):
    n = x.shape[0]
    const = lambda i: (0, 0)
    out = None
    boff = 0
    for y0, y1 in y_parts:
        nblk_p = y0.shape[0] // TM
        glob = lambda i, off=boff: (i + off, 0)
        loc = lambda i: (i, 0)
        in_specs = [pl.BlockSpec((TM, D), glob), pl.BlockSpec((TM, D), loc), pl.BlockSpec((TM, D), loc),
                    pl.BlockSpec((TM, LANE), glob), pl.BlockSpec((8, N_MOD * D), const),
                    pl.BlockSpec((1, D), const)]
        args = [x, y0, y1, gt, mod, final_g.reshape(1, D)]
        aliases = {}
        if out is not None:
            in_specs.append(pl.BlockSpec(memory_space=pl.ANY))
            aliases = {len(args): 0}
            args.append(out)
        out = pl.pallas_call(
            _final_kernel,
            grid=(nblk_p,),
            in_specs=in_specs,
            out_specs=pl.BlockSpec((TM, D), glob),
            out_shape=jax.ShapeDtypeStruct((n, D), F32),
            input_output_aliases=aliases,
            compiler_params=_params(("arbitrary",), VMEM_LIMIT),
            name="moe_combine_final",
        )(*args)
        boff += nblk_p
    return out


def _rope_tables(n):
    lane = jnp.arange(LANE, dtype=jnp.int32) % HEAD_DIM
    inv = ROPE_BASE ** (-jnp.arange(0, AXIS_DIM, 2, dtype=F32) / AXIS_DIM)
    freq = inv[lane % (AXIS_DIM // 2)]
    row_axis = (lane // AXIS_DIM) == 0
    sign = jnp.where((lane % AXIS_DIM) < AXIS_DIM // 2, -1.0, 1.0)
    reps = ATT_BLOCK // GRID_W

    def tables(count, on_axis):
        ang = jnp.arange(count, dtype=F32)[:, None] * freq[None, :]
        return jnp.where(on_axis, jnp.cos(ang), 0.0), jnp.where(on_axis, jnp.sin(ang) * sign, 0.0)

    by_block = lambda t: jnp.pad(t.reshape(-1, reps, LANE), ((0, 0), (0, 8 - reps), (0, 0)))
    cosr, sinr = tables(n // GRID_W, row_axis)
    cosc, sinc = tables(GRID_W, ~row_axis)
    return by_block(cosr), by_block(sinr), cosc, sinc


def _prep_w_in(w):
    pad = jnp.zeros((D, LANE - N_AB), w.dtype)
    return jnp.concatenate([w[:, :_C_Z], w[:, _C_A:_C_Q], pad, w[:, _C_Z:_C_A], w[:, _C_Q:_C_END]],
                           axis=1).astype(BF16)


def kernel(x, c, ctx, c_ctx, w_mod, b_mod, w_in, w_out, conv_w, dn_conv_w, dn_a_log, dn_dt_bias, dn_norm_g,
           attn_sink, ffn_w_gate, ffn_w_up, ffn_w_down, moe_router, moe_w_gate, moe_w_up, moe_w_down,
           final_norm_g):
    bsz, n, d = x.shape
    ctx_len = ctx.shape[1]
    depth = w_in.shape[0]
    assert bsz == 1 and d == D and ctx_len == TM and n % TM == 0 and n % GRID_W == 0
    rope = _rope_tables(n)
    mods = _mod_vectors(c, c_ctx, w_mod, b_mod)
    bd = _head_blockdiag(DN_DIM, DN_HEAD_DIM)
    stream = (ctx[0], x[0], 0)
    for layer in range(depth):
        last = layer == depth - 1
        mod = mods[layer]
        pconv, pz, pq, pkv, qn, kn, vv, gb = _in_proj(*stream, mod, _prep_w_in(w_in[layer]), dn_conv_w[layer],
                                                      dn_a_log[layer], dn_dt_bias[layer], bd)
        to_bf16 = []
        if layer % 2 == 0:
            to_bf16 += [w[layer // 2] for w in (ffn_w_gate, ffn_w_up, ffn_w_down)]
        if (layer + 1) % 2 == 1 and layer + 1 < depth:
            j = (layer + 1) // 2
            to_bf16 += [w[j].reshape(-1, w.shape[-1]) for w in (moe_w_gate, moe_w_up, moe_w_down)]
        o_f, o_b, *cast = _delta_net(qn, kn, vv, gb, ctx_len, tuple(to_bf16))
        if layer % 2 == 0:
            ffn_bf16, cast = cast[:3], cast[3:]
        if cast:
            moe_bf16 = [c.reshape(w.shape[1:]) for c, w in zip(cast, (moe_w_gate, moe_w_up, moe_w_down))]
        sink = jnp.zeros((8, LANE), F32).at[0, :ATT_HEADS].set(attn_sink[layer])
        yc = _attention(pq, pkv, rope, sink, ctx_len, with_ctx=not last)
        router = None
        if layer % 2 == 1:
            wr = jnp.zeros((D, LANE), F32).at[:, :N_EXPERTS].set(moe_router[layer // 2])
            wr1 = wr.astype(BF16)
            router = (wr1, (wr - wr1.astype(F32)).astype(BF16))
        outs = _mixer_finish(*stream, mod, pconv, conv_w[layer], o_f, o_b, pz, dn_norm_g[layer], yc,
                             w_out[layer].astype(BF16), bd, ctx_len, with_ctx=not last, router=router)
        if layer % 2 == 0:
            assert not last
            x1, hx = outs
            i = layer // 2
            h = _dense_ffn(hx, x1, mod, *ffn_bf16)
            stream = (h, h, ctx_len // TM)
        else:
            assert last
            x1, hx, route = outs
            i = layer // 2
            dest, row_tok, blk_e, n_used = _moe_route(route[:, 2:2 + TOP_K].astype(jnp.int32))
            take = lambda rows_, idx: rows_.at[idx].get(mode="promise_in_bounds")
            parts = lambda a: (a[:a.shape[0] // MOE_FIRST_PART], a[a.shape[0] // MOE_FIRST_PART:])
            y = _moe_experts([take(hx, r) for r in parts(row_tok)], blk_e, n_used, *moe_bf16)
            h = _moe_combine_final(x1, [(take(y, dd[:, 0]), take(y, dd[:, 1])) for dd in parts(dest)],
                                   route, mod, final_norm_g)
    return h.reshape(bsz, n, d)
```

```python
import functools

import jax
import jax.numpy as jnp
from jax import lax
from jax.experimental import pallas as pl
from jax.experimental.pallas import tpu as pltpu

F32 = jnp.float32
BF16 = jnp.bfloat16

D = 1024
N_MOD = 6
EPS = 1e-6
NEG = -1e30
GRID_W = 64

CONV_CH = 256
DN_HEADS = 6
DN_HEAD_DIM = 64
DN_DIM = DN_HEADS * DN_HEAD_DIM
DN_CHUNK = 64
DN_SUB = 16
DN_STEP_CHUNKS = 4
ATT_HEADS = 6
ATT_KV_HEADS = 2
ATT_GROUP = ATT_HEADS // ATT_KV_HEADS
HEAD_DIM = 64
ATT_DIM = ATT_HEADS * HEAD_DIM
ATT_KV_DIM = ATT_KV_HEADS * HEAD_DIM
ATT_BLOCK = 128
ROPE_BASE = 10000.0
AXIS_DIM = HEAD_DIM // 2
MIX_DIM = CONV_CH + DN_DIM + ATT_DIM

D_FF = 2816
N_EXPERTS = 8
TOP_K = 2
D_FF_EXPERT = 3584

TM = 256
FF_CHUNK = 512
MOE_TM = 1024
MOE_TF = 1792
MOE_SUB = 256
MOE_FIRST_PART = 4
LANE = 128
VMEM_LIMIT = 56 * 1024 * 1024

_C_QKV = 3 * CONV_CH
_C_Z = _C_QKV + 3 * DN_DIM
_C_A = _C_Z + DN_DIM
_C_Q = _C_A + 4 * DN_HEADS
_C_K = _C_Q + ATT_DIM
_C_V = _C_K + ATT_KV_DIM
_C_END = _C_V + ATT_KV_DIM
N_AB = 4 * DN_HEADS


def _params(sem=None, vmem=None):
    kw = {}
    if sem is not None:
        kw["dimension_semantics"] = sem
    if vmem is not None:
        kw["vmem_limit_bytes"] = vmem
    return pltpu.CompilerParams(**kw)


def _split2(a):
    hi = a.astype(BF16)
    lo = (a - hi.astype(F32)).astype(BF16)
    return hi, lo


def _split3(a):
    hi = a.astype(BF16)
    r = a - hi.astype(F32)
    mid = r.astype(BF16)
    lo = (r - mid.astype(F32)).astype(BF16)
    return hi, mid, lo


_NN = (((1,), (0,)), ((), ()))
_NT = (((1,), (1,)), ((), ()))
_TN = (((0,), (0,)), ((), ()))


def _mm(a, b, dims=_NN):
    return lax.dot_general(a.astype(BF16), b.astype(BF16), dims, preferred_element_type=F32)


def _mm3(a, b, dims=_NN):
    a1, a0 = _split2(a)
    b1, b0 = _split2(b)
    d = functools.partial(lax.dot_general, dimension_numbers=dims, preferred_element_type=F32)
    return d(a1, b1) + (d(a1, b0) + d(a0, b1))


_BNN = (((2,), (1,)), ((0,), (0,)))
_BNT = (((2,), (2,)), ((0,), (0,)))
_BTN = (((1,), (1,)), ((0,), (0,)))


def _bmm(a, b, dims=_BNN):
    return lax.dot_general(a.astype(BF16), b.astype(BF16), dims, preferred_element_type=F32)


def _sigmoid(x):
    return 1.0 / (1.0 + jnp.exp(-x))


def _softplus(x):
    return jnp.maximum(x, 0.0) + jnp.log1p(jnp.exp(-jnp.abs(x)))


def _mod_row(mod_ref, is_ctx):
    mod = mod_ref[...]
    return jnp.where(is_ctx, mod[1:2, :], mod[0:1, :])


def _rmsnorm_rows(x):
    return x * lax.rsqrt(jnp.mean(x * x, axis=-1, keepdims=True) + EPS)


def _shift_rows(u, prow, nrow):
    n = u.shape[0]
    rid = lax.broadcasted_iota(jnp.int32, u.shape, 0)
    up = jnp.where(rid == 0, prow, pltpu.roll(u, 1, 0))
    un = jnp.where(rid == n - 1, nrow, pltpu.roll(u, n - 1, 0))
    return up, un


def _same_group(shape, group):
    sh = group.bit_length() - 1
    assert 1 << sh == group
    return (lax.broadcasted_iota(jnp.int32, shape, 0) >> sh) == (lax.broadcasted_iota(jnp.int32, shape, 1) >> sh)


def _head_blockdiag(n, group):
    g = jnp.arange(n, dtype=jnp.int32) // group
    return (g[:, None] == g[None, :]).astype(BF16)


def _group_sum(t, bd):
    return _mm(t, bd)


MOD_TN = 1536


def _mod_kernel(s_ref, w_ref, b_ref, o_ref):
    s = s_ref[...]
    s = s * _sigmoid(s)
    o_ref[0] = _mm3(s, w_ref[0]) + b_ref[0]


def _mod_vectors(c, c_ctx, w_mod, b_mod):
    depth = w_mod.shape[0]
    s = jnp.zeros((8, D), F32).at[0].set(c[0]).at[1].set(c_ctx)
    return pl.pallas_call(
        _mod_kernel,
        grid=(depth, N_MOD * D // MOD_TN),
        in_specs=[
            pl.BlockSpec((8, D), lambda l, j: (0, 0)),
            pl.BlockSpec((1, D, MOD_TN), lambda l, j: (l, 0, j)),
            pl.BlockSpec((1, 1, MOD_TN), lambda l, j: (l, 0, j)),
        ],
        out_specs=pl.BlockSpec((1, 8, MOD_TN), lambda l, j: (l, 0, j)),
        out_shape=jax.ShapeDtypeStruct((depth, 8, N_MOD * D), F32),
        compiler_params=_params(("arbitrary", "arbitrary"), VMEM_LIMIT),
        name="mod_vectors",
    )(s, w_mod, b_mod.reshape(depth, 1, N_MOD * D))


def _halo_valid(i, nblk):
    return jnp.logical_and(i != 0, i != 1), jnp.logical_and(i != 0, i != nblk - 1)


def _halo_rows(prev_ref, next_ref, i, nblk):
    pvalid, nvalid = _halo_valid(i, nblk)
    prow = jnp.where(pvalid, prev_ref[7:8, :], 0.0)
    nrow = jnp.where(nvalid, next_ref[0:1, :], 0.0)
    return prow, nrow


def _in_kernel(ctx_ref, h_ref, hprev_ref, hnext_ref, mod_ref, w_ref, cw_ref, alog_ref, dtb_ref, bd_ref,
               pconv_ref, pz_ref, pq_ref, pkv_ref, q_ref, k_ref, v_ref, gb_ref, *, nblk):
    i = pl.program_id(0)
    row = _mod_row(mod_ref, i == 0)
    norm_mod = lambda x: _rmsnorm_rows(x) * (1.0 + row[:, D:2 * D]) + row[:, 0:D]
    h1 = norm_mod(jnp.where(i == 0, ctx_ref[...], h_ref[...])).astype(BF16)
    halo = norm_mod(jnp.concatenate([hprev_ref[...], hnext_ref[...]], axis=0)).astype(BF16)
    d = functools.partial(lax.dot_general, dimension_numbers=_NN, preferred_element_type=F32)
    tm = h1.shape[0]
    c0 = 3 * CONV_CH
    c1 = c0 + 3 * DN_DIM
    c2 = c1 + LANE
    c3 = c2 + DN_DIM + ATT_DIM
    pconv_ref[...] = d(h1, w_ref[:, 0:c0])
    zq = d(h1, w_ref[:, c2:c3])
    pz_ref[...] = zq[:, 0:DN_DIM]
    pq_ref[...] = zq[:, DN_DIM:]
    pkv_ref[...] = d(h1, w_ref[:, c3:c3 + 2 * ATT_KV_DIM])
    qkv_ab = d(jnp.concatenate([h1, halo], axis=0), w_ref[:, c0:c2])
    qkv = qkv_ab[:, 0:3 * DN_DIM]
    ab = qkv_ab[0:tm, 3 * DN_DIM:]

    u = qkv[0:tm, :]
    pvalid, nvalid = _halo_valid(i, nblk)
    prow = jnp.where(pvalid, qkv[tm + 7:tm + 8, :], 0.0)
    nrow = jnp.where(nvalid, qkv[tm + 8:tm + 9, :], 0.0)
    up, un = _shift_rows(u, prow, nrow)
    cw = cw_ref[...]
    y = up * cw[0:1, :] + u * cw[1:2, :] + un * cw[2:3, :]
    y = y * _sigmoid(y)
    q = y[:, 0:DN_DIM]
    k = y[:, DN_DIM:2 * DN_DIM]
    bd = bd_ref[...]
    q_ref[...] = q * lax.rsqrt(_group_sum(q * q, bd) + 1e-6) * (DN_HEAD_DIM ** -0.5)
    k_ref[...] = k * lax.rsqrt(_group_sum(k * k, bd) + 1e-6)
    v_ref[...] = y[:, 2 * DN_DIM:3 * DN_DIM]
    g = -jnp.exp(alog_ref[...]) * _softplus(ab + dtb_ref[...])
    lane = lax.broadcasted_iota(jnp.int32, ab.shape, 1)
    gb_ref[...] = jnp.where(lane < 2 * DN_HEADS, g, _sigmoid(ab))


def _in_proj(ctx_src, lat_src, lat_blk0, mod, w_main, dn_conv_w, a_log, dt_bias, bd):
    r8 = TM // 8
    last8 = lat_src.shape[0] // 8 - 1
    nblk = 1 + lat_src.shape[0] // TM - lat_blk0
    t = nblk * TM
    alog = jnp.zeros((1, LANE), F32).at[0, :2 * DN_HEADS].set(a_log.reshape(-1))
    dtb = jnp.zeros((1, LANE), F32).at[0, :2 * DN_HEADS].set(dt_bias.reshape(-1))
    widths = (3 * CONV_CH, DN_DIM, ATT_DIM, 2 * ATT_KV_DIM, DN_DIM, DN_DIM, DN_DIM, LANE)
    const = lambda i: (0, 0)
    lat = lambda i: jnp.maximum(i - 1, 0) + lat_blk0
    return pl.pallas_call(
        functools.partial(_in_kernel, nblk=nblk),
        grid=(nblk,),
        in_specs=[
            pl.BlockSpec((TM, D), const),
            pl.BlockSpec((TM, D), lambda i: (lat(i), 0)),
            pl.BlockSpec((8, D), lambda i: (jnp.maximum(lat(i) * r8 - 1, 0), 0)),
            pl.BlockSpec((8, D), lambda i: (jnp.minimum((lat(i) + 1) * r8, last8), 0)),
            pl.BlockSpec((8, N_MOD * D), const),
            pl.BlockSpec(w_main.shape, const),
            pl.BlockSpec((3, 3 * DN_DIM), const),
            pl.BlockSpec((1, LANE), const),
            pl.BlockSpec((1, LANE), const),
            pl.BlockSpec(bd.shape, const),
        ],
        out_specs=[pl.BlockSpec((TM, w), lambda i: (i, 0)) for w in widths],
        out_shape=[jax.ShapeDtypeStruct((t, w), F32) for w in widths],
        compiler_params=_params(("arbitrary",), VMEM_LIMIT),
        name="in_proj",
    )(ctx_src, lat_src, lat_src, lat_src, mod, w_main, dn_conv_w, alog, dtb, bd)


def _dn_chunk(rev, q_ref, k_ref, v_ref, gb_ref, o_ref, s_ref):
    c_ = DN_CHUNK
    ri = lax.broadcasted_iota(jnp.int32, (c_, c_), 0)
    ci = lax.broadcasted_iota(jnp.int32, (c_, c_), 1)
    incl = (ri <= ci) if rev else (ri >= ci)
    strict = (ri < ci) if rev else (ri > ci)
    same_sub = _same_group((c_, c_), DN_SUB)
    eye = jnp.where(ri == ci, 1.0, 0.0)
    tri = jnp.where(incl, 1.0, 0.0).astype(BF16)
    last = 0 if rev else c_ - 1

    nchunks = q_ref.shape[0] // c_
    nh = DN_HEADS
    col0 = nh if rev else 0
    rows = lambda g: slice(g * c_, (g + 1) * c_)
    lanes = lambda h: slice(h * DN_HEAD_DIM, (h + 1) * DN_HEAD_DIM)

    def stack(fn):
        return jnp.stack([fn(g, h) for g in range(nchunks) for h in range(nh)])

    gb = gb_ref[...]
    gcs = [_cumsum_rows(tri, gb[rows(g), :]) for g in range(nchunks)]
    gcts = [gc.T for gc in gcs]
    q = stack(lambda g, h: q_ref[rows(g), lanes(h)])
    k = stack(lambda g, h: k_ref[rows(g), lanes(h)])
    v = stack(lambda g, h: v_ref[rows(g), lanes(h)])
    gcol = stack(lambda g, h: gcs[g][:, col0 + h:col0 + h + 1])
    grow = stack(lambda g, h: gcts[g][col0 + h:col0 + h + 1, :])
    beta = stack(lambda g, h: gb[rows(g), 2 * nh + col0 + h:2 * nh + col0 + h + 1])
    glast = gcol[:, last:last + 1, :]
    decay = jnp.where(incl, jnp.exp(jnp.where(incl, gcol - grow, 0.0)), 0.0)
    eg = jnp.exp(gcol)
    kb = k * beta
    a = jnp.where(strict, _bmm(kb, k, _BNT) * decay, 0.0)
    qk = jnp.where(incl, _bmm(q, k, _BNT) * decay, 0.0)
    ad = jnp.where(same_sub, a, 0.0)
    ao = a - ad
    p = eye - ad
    n2 = _bmm(ad, ad)
    p = p + _bmm(p, n2)
    n4 = _bmm(n2, n2)
    p = p + _bmm(p, n4)
    n8 = _bmm(n4, n4)
    dinv = p + _bmm(p, n8)
    m = _bmm(dinv, ao)
    m2 = _bmm(m, m)
    y = _bmm(dinv, jnp.concatenate([v * beta, kb * eg], axis=-1))
    z = y + _bmm(m2, y)
    x = z - _bmm(m, z)
    u = x[:, :, :DN_HEAD_DIM]
    w = x[:, :, DN_HEAD_DIM:]
    qg = q * eg
    kd = k * jnp.exp(glast - gcol)
    gl = jnp.exp(glast)
    s = s_ref[col0:col0 + nh]
    for g in (reversed(range(nchunks)) if rev else range(nchunks)):
        b = slice(g * nh, (g + 1) * nh)
        v_new = u[b] - _bmm(w[b], s)
        o = _bmm(qg[b], s) + _bmm(qk[b], v_new)
        s = s * gl[b] + _bmm(kd[b], v_new, _BTN)
        for h in range(nh):
            o_ref[rows(g), lanes(h)] = o[h]
    s_ref[col0:col0 + nh] = s


def _cumsum_rows(tri_bf16, g):
    g2, g1, g0 = _split3(g)
    d = functools.partial(lax.dot_general, dimension_numbers=_NN, preferred_element_type=F32)
    return d(tri_bf16, g2) + (d(tri_bf16, g1) + d(tri_bf16, g0))


def _dn_kernel(qf, kf, vf, gf, qb, kb, vb, gbb, *rest, n_cast):
    cast_in = rest[:n_cast]
    of_ref, ob_ref = rest[n_cast:n_cast + 2]
    cast_out = rest[n_cast + 2:2 * n_cast + 2]
    s_ref = rest[-1]

    @pl.when(pl.program_id(0) == 0)
    def _():
        s_ref[...] = jnp.zeros(s_ref.shape, F32)

    _dn_chunk(False, qf, kf, vf, gf, of_ref, s_ref)
    _dn_chunk(True, qb, kb, vb, gbb, ob_ref, s_ref)
    for src, dst in zip(cast_in, cast_out):
        dst[...] = src[...].astype(BF16)


def _delta_net(q, k, v, gb, ctx_len, to_bf16=()):
    t = q.shape[0]
    rows = DN_STEP_CHUNKS * DN_CHUNK
    assert ctx_len == rows and t % rows == 0
    nstep = t // rows
    cast_specs = []
    for m in to_bf16:
        rb = -(-m.shape[0] // nstep)
        rb = -(-rb // 16) * 16
        nb = -(-m.shape[0] // rb)
        cast_specs.append(pl.BlockSpec((rb, m.shape[1]), lambda s, nb=nb: (jnp.minimum(s, nb - 1), 0)))

    def fwd(s):
        return (s, 0)

    def bwd(s):
        return (jnp.where(s == 0, 0, nstep - s), 0)

    wide = lambda im: pl.BlockSpec((rows, DN_DIM), im)
    narrow = lambda im: pl.BlockSpec((rows, LANE), im)
    return pl.pallas_call(
        functools.partial(_dn_kernel, n_cast=len(to_bf16)),
        grid=(nstep,),
        in_specs=[wide(fwd), wide(fwd), wide(fwd), narrow(fwd), wide(bwd), wide(bwd), wide(bwd), narrow(bwd)]
        + cast_specs,
        out_specs=[wide(fwd), wide(bwd)] + cast_specs,
        out_shape=[jax.ShapeDtypeStruct((t, DN_DIM), F32)] * 2
        + [jax.ShapeDtypeStruct(m.shape, BF16) for m in to_bf16],
        scratch_shapes=[pltpu.VMEM((2 * DN_HEADS, DN_HEAD_DIM, DN_HEAD_DIM), F32)],
        compiler_params=_params(("arbitrary",), VMEM_LIMIT),
        name="delta_net",
    )(q, k, v, gb, q, k, v, gb, *to_bf16)


def _rope(x, cos, sin):
    w = x.shape[1]
    lane = lax.broadcasted_iota(jnp.int32, x.shape, 1)
    first_half = (lane & (AXIS_DIM - 1)) < (AXIS_DIM // 2)
    swapped = jnp.where(first_half, pltpu.roll(x, w - AXIS_DIM // 2, 1), pltpu.roll(x, AXIS_DIM // 2, 1))
    return x * cos + swapped * sin


LOG2E = 1.4426950408889634


def _softmax_av(s, sink, vals):
    m = jnp.maximum(jnp.max(s, axis=-1, keepdims=True), sink)
    p = jnp.exp2(s - m)
    denom = jnp.sum(p, axis=-1, keepdims=True) + jnp.exp2(sink - m)
    return _mm(p, vals) / denom


def _attend(q, keys, vals, band, sink_all, o_ref, row0=0):
    b = q.shape[0]
    for kvh in range(ATT_KV_HEADS):
        kl = slice(kvh * HEAD_DIM, (kvh + 1) * HEAD_DIM)
        heads = range(kvh * ATT_GROUP, (kvh + 1) * ATT_GROUP)
        qs = jnp.concatenate([q[:, h * HEAD_DIM:(h + 1) * HEAD_DIM] for h in heads], axis=0)
        sink = jnp.concatenate([jnp.broadcast_to(sink_all[0:1, h:h + 1], (b, 1)) for h in heads], axis=0)
        s = _mm(qs, keys[:, kl], _NT)
        if band is not None:
            kb = ATT_BLOCK
            s = jnp.concatenate([jnp.where(band[0], s[:, 0:kb], NEG), s[:, kb:2 * kb],
                                 jnp.where(band[1], s[:, 2 * kb:3 * kb], NEG), s[:, 3 * kb:]], axis=1)
        o = _softmax_av(s, sink * LOG2E, vals[:, kl])
        for g, h in enumerate(heads):
            o_ref[row0:row0 + b, h * HEAD_DIM:(h + 1) * HEAD_DIM] = o[g * b:(g + 1) * b, :]


def _band_valid(first, last):
    b = ATT_BLOCK
    c = lax.broadcasted_iota(jnp.int32, (1, b), 1)
    r = lax.broadcasted_iota(jnp.int32, (ATT_GROUP * b, 1), 0) & (b - 1)
    prev_ok = jnp.where(first, -1, c) >= r
    next_ok = jnp.where(last, b, c) <= r
    return prev_ok, next_ok


def _rope_block(rowtab_ref, coltab_ref, blk):
    rt = rowtab_ref[blk]
    ct = coltab_ref[...]
    reps = ATT_BLOCK // GRID_W
    rows = jnp.concatenate([jnp.broadcast_to(rt[g:g + 1, :], (GRID_W, LANE)) for g in range(reps)], axis=0)
    return rows + jnp.concatenate([ct] * reps, axis=0)


def _attn_kernel(q_ref, kp_ref, kc_ref, kn_ref, kctx_ref, cosr_ref, sinr_ref, cosc_ref, sinc_ref, sink_ref, o_ref,
                 *, nb, with_ctx):
    j = pl.program_id(0)
    b = ATT_BLOCK
    scale = HEAD_DIM ** -0.5 * LOG2E

    @pl.when(j == 0)
    def _():
        if with_ctx:
            kvx = kctx_ref[...]
            _attend(q_ref[...] * scale, kvx[:, :ATT_KV_DIM], kvx[:, ATT_KV_DIM:], None, sink_ref[...], o_ref)
        else:
            o_ref[...] = jnp.zeros(o_ref.shape, F32)

    @pl.when(j > 0)
    def _():
        b0 = 2 * (j - 1)
        blocks = (jnp.maximum(b0 - 1, 0), b0, b0 + 1, jnp.minimum(b0 + 2, nb - 1))
        cos = [_rope_block(cosr_ref, cosc_ref, blk) for blk in blocks]
        sin = [_rope_block(sinr_ref, sinc_ref, blk) for blk in blocks]
        kvc = kc_ref[...]
        kv = (kp_ref[...], kvc[0:b, :], kvc[b:2 * b, :], kn_ref[...])
        kvx = kctx_ref[...]
        keys = [_rope(t[:, :ATT_KV_DIM], c_, s_) for t, c_, s_ in zip(kv, cos, sin)]
        q_all = q_ref[...]
        sink_all = sink_ref[...]
        for sub in range(2):
            q = q_all[sub * b:(sub + 1) * b, :]
            q = jnp.concatenate([_rope(q[:, l * LANE:(l + 1) * LANE], cos[1 + sub], sin[1 + sub])
                                 for l in range(ATT_DIM // LANE)], axis=1)
            kcat = jnp.concatenate(keys[sub:sub + 3] + [kvx[:, :ATT_KV_DIM]], axis=0)
            vcat = jnp.concatenate([t[:, ATT_KV_DIM:] for t in kv[sub:sub + 3]] + [kvx[:, ATT_KV_DIM:]], axis=0)
            band = _band_valid(first=(b0 + sub == 0), last=(b0 + sub == nb - 1))
            _attend(q * scale, kcat, vcat, band, sink_all, o_ref, row0=sub * b)


def _attention(pq, pkv, rope, sink, ctx_len, with_ctx):
    t = pq.shape[0]
    n = t - ctx_len
    nb = n // ATT_BLOCK
    step = 2 * ATT_BLOCK
    assert ctx_len == step and nb % 2 == 0
    off = ctx_len // ATT_BLOCK
    half = lambda im: pl.BlockSpec((ATT_BLOCK, 2 * ATT_KV_DIM), im)
    lat = lambda j: 2 * jnp.maximum(j - 1, 0)
    prv = lambda j: (jnp.maximum(lat(j) - 1, 0) + off, 0)
    nxt = lambda j: (jnp.minimum(lat(j) + 2, nb - 1) + off, 0)
    cur = lambda j: (j, 0)
    whole = lambda a: pl.BlockSpec(a.shape, lambda j: (0,) * a.ndim)
    return pl.pallas_call(
        functools.partial(_attn_kernel, nb=nb, with_ctx=with_ctx),
        grid=(1 + nb // 2,),
        in_specs=[
            pl.BlockSpec((step, ATT_DIM), cur),
            half(prv), pl.BlockSpec((step, 2 * ATT_KV_DIM), cur), half(nxt),
            pl.BlockSpec((ctx_len, 2 * ATT_KV_DIM), lambda j: (0, 0)),
        ] + [whole(a) for a in rope] + [pl.BlockSpec((8, LANE), lambda j: (0, 0))],
        out_specs=pl.BlockSpec((step, ATT_DIM), cur),
        out_shape=jax.ShapeDtypeStruct((t, ATT_DIM), F32),
        compiler_params=_params(("arbitrary",), VMEM_LIMIT),
        name="attention",
    )(pq, pkv, pkv, pkv, pkv, *rope, sink)


def _mixfin_kernel(ctx_ref, h_ref, mod_ref, pconv_ref, prev_ref, next_ref, cw_ref, of_ref, ob_ref, z_ref, ng_ref,
                   yc_ref, wout_ref, bd_ref, *rest, nblk, blk0, with_router):
    if with_router:
        wr1_ref, wr0_ref, x_ref, hx_ref, lg_ref = rest
    else:
        x_ref, hx_ref = rest
    i = pl.program_id(0) + blk0
    row = _mod_row(mod_ref, i == 0)
    pc = pconv_ref[...]
    u = pc[:, CONV_CH:2 * CONV_CH] * pc[:, 2 * CONV_CH:]
    prow, nrow = _halo_rows(prev_ref, next_ref, i, nblk)
    prow = prow[:, CONV_CH:2 * CONV_CH] * prow[:, 2 * CONV_CH:]
    nrow = nrow[:, CONV_CH:2 * CONV_CH] * nrow[:, 2 * CONV_CH:]
    up, un = _shift_rows(u, prow, nrow)
    cw = cw_ref[...]
    ya = pc[:, :CONV_CH] * (up * cw[0:1, :] + u * cw[1:2, :] + un * cw[2:3, :])
    o = of_ref[...] + ob_ref[...]
    ms = _group_sum(o * o, bd_ref[...]) * (1.0 / DN_HEAD_DIM)
    z = z_ref[...]
    yb = o * lax.rsqrt(ms + EPS) * ng_ref[...] * (z * _sigmoid(z))
    mix = jnp.concatenate([ya, yb, yc_ref[...]], axis=1)
    x = jnp.where(i == 0, ctx_ref[...], h_ref[...]) + row[:, 2 * D:3 * D] * _mm(mix, wout_ref[...])
    x_ref[...] = x
    hx = _rmsnorm_rows(x) * (1.0 + row[:, 4 * D:5 * D]) + row[:, 3 * D:4 * D]
    hx_ref[...] = hx.astype(hx_ref.dtype)
    if with_router:
        h1, h0 = _split2(hx)
        d = functools.partial(lax.dot_general, dimension_numbers=_NN, preferred_element_type=F32)
        lg = d(h1, wr1_ref[...]) + (d(h0, wr1_ref[...]) + d(h1, wr0_ref[...]))
        lane = lax.broadcasted_iota(jnp.int32, lg.shape, 1)
        lanef = lane.astype(F32)
        lg = jnp.where(lane < N_EXPERTS, lg, -jnp.inf)
        m1 = jnp.max(lg, axis=-1, keepdims=True)
        i1 = jnp.min(jnp.where(lg == m1, lanef, float(LANE)), axis=-1, keepdims=True)
        rest = jnp.where(lanef == i1, -jnp.inf, lg)
        m2 = jnp.max(rest, axis=-1, keepdims=True)
        i2 = jnp.min(jnp.where(rest == m2, lanef, float(LANE)), axis=-1, keepdims=True)
        e2 = jnp.exp(m2 - m1)
        g1 = 1.0 / (1.0 + e2)
        lg_ref[...] = jnp.where(lane == 0, g1, jnp.where(lane == 1, e2 * g1, jnp.where(lane == 2, i1, i2)))


def _mixer_finish(ctx_src, lat_src, lat_blk0, mod, pconv, conv_w, o_f, o_b, pz, norm_g, yc, w_out, bd, ctx_len,
                  with_ctx, router=None):
    t = pconv.shape[0]
    nblk = t // TM
    blk0 = 0 if with_ctx else ctx_len // TM
    rows = t - blk0 * TM
    r8 = TM // 8
    w = pconv.shape[1]
    cur = lambda i: (i + blk0, 0)
    out_cur = lambda i: (i, 0)
    const = lambda i: (0, 0)
    ng = jnp.tile(norm_g.reshape(1, DN_HEAD_DIM), (1, DN_HEADS))
    in_specs = [
        pl.BlockSpec((TM, D), const),
        pl.BlockSpec((TM, D), lambda i: (jnp.maximum(i + blk0 - 1, 0) + lat_blk0, 0)),
        pl.BlockSpec((8, N_MOD * D), const),
        pl.BlockSpec((TM, w), cur),
        pl.BlockSpec((8, w), lambda i: (jnp.maximum((i + blk0) * r8 - 1, 0), 0)),
        pl.BlockSpec((8, w), lambda i: (jnp.minimum((i + blk0 + 1) * r8, t // 8 - 1), 0)),
        pl.BlockSpec((3, CONV_CH), const),
        pl.BlockSpec((TM, DN_DIM), cur),
        pl.BlockSpec((TM, DN_DIM), cur),
        pl.BlockSpec((TM, DN_DIM), cur),
        pl.BlockSpec((1, DN_DIM), const),
        pl.BlockSpec((TM, ATT_DIM), cur),
        pl.BlockSpec((MIX_DIM, D), const),
        pl.BlockSpec(bd.shape, const),
    ]
    args = [ctx_src, lat_src, mod, pconv, pconv, pconv, conv_w, o_f, o_b, pz, ng, yc, w_out, bd]
    out_specs = [pl.BlockSpec((TM, D), out_cur), pl.BlockSpec((TM, D), out_cur)]
    hx_dtype = BF16 if router is None else F32
    out_shape = [jax.ShapeDtypeStruct((rows, D), F32), jax.ShapeDtypeStruct((rows, D), hx_dtype)]
    if router is not None:
        in_specs += [pl.BlockSpec((D, LANE), const)] * 2
        args += list(router)
        out_specs.append(pl.BlockSpec((TM, LANE), out_cur))
        out_shape.append(jax.ShapeDtypeStruct((rows, LANE), F32))
    return pl.pallas_call(
        functools.partial(_mixfin_kernel, nblk=nblk, blk0=blk0, with_router=router is not None),
        grid=(rows // TM,),
        in_specs=in_specs,
        out_specs=out_specs,
        out_shape=out_shape,
        compiler_params=_params(("arbitrary",), VMEM_LIMIT),
        name="mixer_finish",
    )(*args)


def _ffn_kernel(hx_ref, x_ref, mod_ref, wg_ref, wu_ref, wd_ref, o_ref):
    row = _mod_row(mod_ref, pl.program_id(0) == 0)
    hx = hx_ref[...]
    acc = jnp.zeros((hx.shape[0], D), F32)
    for f in range(0, D_FF, FF_CHUNK):
        fe = min(f + FF_CHUNK, D_FF)
        g = _mm(hx, wg_ref[:, f:fe])
        u = _mm(hx, wu_ref[:, f:fe])
        acc = acc + _mm(g * _sigmoid(g) * u, wd_ref[f:fe, :])
    o_ref[...] = x_ref[...] + row[:, 5 * D:6 * D] * acc


def _dense_ffn(hx, x, mod, wg, wu, wd):
    t = x.shape[0]
    const = lambda i: (0, 0)
    once = dict(pipeline_mode=pl.Buffered(1))
    return pl.pallas_call(
        _ffn_kernel,
        grid=(t // TM,),
        in_specs=[
            pl.BlockSpec((TM, D), lambda i: (i, 0)),
            pl.BlockSpec((TM, D), lambda i: (i, 0)),
            pl.BlockSpec((8, N_MOD * D), const),
            pl.BlockSpec((D, D_FF), const, **once),
            pl.BlockSpec((D, D_FF), const, **once),
            pl.BlockSpec((D_FF, D), const, **once),
        ],
        out_specs=pl.BlockSpec((TM, D), lambda i: (i, 0)),
        out_shape=jax.ShapeDtypeStruct((t, D), F32),
        compiler_params=_params(("arbitrary",), VMEM_LIMIT),
        name="dense_ffn",
    )(hx, x, mod, wg, wu, wd)


def _moe_kernel(be_ref, nu_ref, xs_ref, wg_ref, wu_ref, wd_ref, *rest, boff):
    y_ref, acc_ref = rest[-2:]
    b = pl.program_id(0) + boff
    f = pl.program_id(1)

    @pl.when(b < nu_ref[0])
    def _():
        xs = xs_ref[...].astype(BF16)
        part = jnp.zeros((MOE_TM, D), F32)
        for c in range(0, MOE_TF, MOE_SUB):
            g = _mm(xs, wg_ref[0, :, c:c + MOE_SUB])
            u = _mm(xs, wu_ref[0, :, c:c + MOE_SUB])
            part = part + _mm(g * _sigmoid(g) * u, wd_ref[0, c:c + MOE_SUB, :])

        @pl.when(f == 0)
        def _():
            acc_ref[...] = part

        @pl.when(f != 0)
        def _():
            acc_ref[...] += part

        @pl.when(f == pl.num_programs(1) - 1)
        def _():
            y_ref[...] = acc_ref[...]

    @pl.when(b >= nu_ref[0])
    def _():
        y_ref[...] = jnp.zeros(y_ref.shape, F32)


def _moe_experts(xs_parts, blk_e, n_used, wg, wu, wd):
    cap = sum(xs.shape[0] for xs in xs_parts)
    nf = D_FF_EXPERT // MOE_TF
    y = None
    boff = 0
    for xs in xs_parts:
        nblk_p = xs.shape[0] // MOE_TM

        def fidx(b, f, nu):
            return jnp.where(b < nu[0], f, nf - 1)

        g = lambda b, boff=boff: b + boff
        in_specs = [
            pl.BlockSpec((MOE_TM, D), lambda b, f, be, nu: (b, 0)),
            pl.BlockSpec((1, D, MOE_TF), lambda b, f, be, nu, g=g: (be[g(b)], 0, fidx(g(b), f, nu))),
            pl.BlockSpec((1, D, MOE_TF), lambda b, f, be, nu, g=g: (be[g(b)], 0, fidx(g(b), f, nu))),
            pl.BlockSpec((1, MOE_TF, D), lambda b, f, be, nu, g=g: (be[g(b)], fidx(g(b), f, nu), 0)),
        ]
        args = [blk_e, n_used, xs, wg, wu, wd]
        aliases = {}
        if y is not None:
            in_specs.append(pl.BlockSpec(memory_space=pl.ANY))
            aliases = {len(args): 0}
            args.append(y)
        grid_spec = pltpu.PrefetchScalarGridSpec(
            num_scalar_prefetch=2,
            grid=(nblk_p, nf),
            in_specs=in_specs,
            out_specs=pl.BlockSpec((MOE_TM, D), lambda b, f, be, nu, g=g: (g(b), 0)),
            scratch_shapes=[pltpu.VMEM((MOE_TM, D), F32)],
        )
        y = pl.pallas_call(
            functools.partial(_moe_kernel, boff=boff),
            grid_spec=grid_spec,
            out_shape=jax.ShapeDtypeStruct((cap, D), F32),
            input_output_aliases=aliases,
            compiler_params=_params(("arbitrary", "arbitrary"), VMEM_LIMIT),
            name="moe_experts",
        )(*args)
        boff += nblk_p
    return y


def _moe_route(top_e):
    n = top_e.shape[0]
    a = n * TOP_K
    flat_e = top_e.reshape(a)
    onehot = (flat_e[:, None] == jnp.arange(N_EXPERTS, dtype=flat_e.dtype)[None, :]).astype(jnp.int32)
    counts = jnp.sum(onehot, axis=0)
    padded = (counts + MOE_TM - 1) // MOE_TM * MOE_TM
    pad_ends = jnp.cumsum(padded)
    pad_starts = pad_ends - padded
    dest = jnp.sum(onehot * (jnp.cumsum(onehot, axis=0) - onehot + pad_starts[None, :]), axis=1)
    cap = a + N_EXPERTS * MOE_TM
    nblk = cap // MOE_TM
    row_tok = (jnp.arange(cap, dtype=jnp.int32) % n).at[dest].set(
        jnp.arange(a, dtype=jnp.int32) // TOP_K, unique_indices=True, mode="promise_in_bounds")
    blk_start = jnp.arange(nblk, dtype=jnp.int32) * MOE_TM
    blk_e = jnp.minimum(jnp.sum((pad_ends[None, :] <= blk_start[:, None]).astype(jnp.int32), axis=1),
                        N_EXPERTS - 1)
    n_used = (pad_ends[-1] // MOE_TM).astype(jnp.int32).reshape(1)
    last_e = blk_e[jnp.maximum(n_used[0] - 1, 0)]
    blk_e = jnp.where(jnp.arange(nblk) < n_used[0], blk_e, last_e)
    return dest.reshape(n, TOP_K), row_tok, blk_e, n_used


def _final_kernel(x_ref, y0_ref, y1_ref, gt_ref, mod_ref, fg_ref, *rest):
    o_ref = rest[-1]
    mod = mod_ref[...]
    gt = gt_ref[...]
    f = gt[:, 0:1] * y0_ref[...] + gt[:, 1:2] * y1_ref[...]
    x = x_ref[...] + mod[0:1, 5 * D:6 * D] * f
    o_ref[...] = _rmsnorm_rows(x) * fg_ref[...]


def _moe_combine_final(x, y_parts, gt, mod, final_g):
    n = x.shape[0]
    const = lambda i: (0, 0)
    out = None
    boff = 0
    for y0, y1 in y_parts:
        nblk_p = y0.shape[0] // TM
        glob = lambda i, off=boff: (i + off, 0)
        loc = lambda i: (i, 0)
        in_specs = [pl.BlockSpec((TM, D), glob), pl.BlockSpec((TM, D), loc), pl.BlockSpec((TM, D), loc),
                    pl.BlockSpec((TM, LANE), glob), pl.BlockSpec((8, N_MOD * D), const),
                    pl.BlockSpec((1, D), const)]
        args = [x, y0, y1, gt, mod, final_g.reshape(1, D)]
        aliases = {}
        if out is not None:
            in_specs.append(pl.BlockSpec(memory_space=pl.ANY))
            aliases = {len(args): 0}
            args.append(out)
        out = pl.pallas_call(
            _final_kernel,
            grid=(nblk_p,),
            in_specs=in_specs,
            out_specs=pl.BlockSpec((TM, D), glob),
            out_shape=jax.ShapeDtypeStruct((n, D), F32),
            input_output_aliases=aliases,
            compiler_params=_params(("arbitrary",), VMEM_LIMIT),
            name="moe_combine_final",
        )(*args)
        boff += nblk_p
    return out


def _rope_tables(n):
    lane = jnp.arange(LANE, dtype=jnp.int32) % HEAD_DIM
    inv = ROPE_BASE ** (-jnp.arange(0, AXIS_DIM, 2, dtype=F32) / AXIS_DIM)
    freq = inv[lane % (AXIS_DIM // 2)]
    row_axis = (lane // AXIS_DIM) == 0
    sign = jnp.where((lane % AXIS_DIM) < AXIS_DIM // 2, -1.0, 1.0)
    reps = ATT_BLOCK // GRID_W

    def tables(count, on_axis):
        ang = jnp.arange(count, dtype=F32)[:, None] * freq[None, :]
        return jnp.where(on_axis, jnp.cos(ang), 0.0), jnp.where(on_axis, jnp.sin(ang) * sign, 0.0)

    by_block = lambda t: jnp.pad(t.reshape(-1, reps, LANE), ((0, 0), (0, 8 - reps), (0, 0)))
    cosr, sinr = tables(n // GRID_W, row_axis)
    cosc, sinc = tables(GRID_W, ~row_axis)
    return by_block(cosr), by_block(sinr), cosc, sinc


def _prep_w_in(w):
    pad = jnp.zeros((D, LANE - N_AB), w.dtype)
    return jnp.concatenate([w[:, :_C_Z], w[:, _C_A:_C_Q], pad, w[:, _C_Z:_C_A], w[:, _C_Q:_C_END]],
                           axis=1).astype(BF16)


def kernel(x, c, ctx, c_ctx, w_mod, b_mod, w_in, w_out, conv_w, dn_conv_w, dn_a_log, dn_dt_bias, dn_norm_g,
           attn_sink, ffn_w_gate, ffn_w_up, ffn_w_down, moe_router, moe_w_gate, moe_w_up, moe_w_down,
           final_norm_g):
    bsz, n, d = x.shape
    ctx_len = ctx.shape[1]
    depth = w_in.shape[0]
    assert bsz == 1 and d == D and ctx_len == TM and n % TM == 0 and n % GRID_W == 0
    rope = _rope_tables(n)
    mods = _mod_vectors(c, c_ctx, w_mod, b_mod)
    bd = _head_blockdiag(DN_DIM, DN_HEAD_DIM)
    stream = (ctx[0], x[0], 0)
    for layer in range(depth):
        last = layer == depth - 1
        mod = mods[layer]
        pconv, pz, pq, pkv, qn, kn, vv, gb = _in_proj(*stream, mod, _prep_w_in(w_in[layer]), dn_conv_w[layer],
                                                      dn_a_log[layer], dn_dt_bias[layer], bd)
        to_bf16 = []
        if layer % 2 == 0:
            to_bf16 += [w[layer // 2] for w in (ffn_w_gate, ffn_w_up, ffn_w_down)]
        if (layer + 1) % 2 == 1 and layer + 1 < depth:
            j = (layer + 1) // 2
            to_bf16 += [w[j].reshape(-1, w.shape[-1]) for w in (moe_w_gate, moe_w_up, moe_w_down)]
        o_f, o_b, *cast = _delta_net(qn, kn, vv, gb, ctx_len, tuple(to_bf16))
        if layer % 2 == 0:
            ffn_bf16, cast = cast[:3], cast[3:]
        if cast:
            moe_bf16 = [c.reshape(w.shape[1:]) for c, w in zip(cast, (moe_w_gate, moe_w_up, moe_w_down))]
        sink = jnp.zeros((8, LANE), F32).at[0, :ATT_HEADS].set(attn_sink[layer])
        yc = _attention(pq, pkv, rope, sink, ctx_len, with_ctx=not last)
        router = None
        if layer % 2 == 1:
            wr = jnp.zeros((D, LANE), F32).at[:, :N_EXPERTS].set(moe_router[layer // 2])
            wr1 = wr.astype(BF16)
            router = (wr1, (wr - wr1.astype(F32)).astype(BF16))
        outs = _mixer_finish(*stream, mod, pconv, conv_w[layer], o_f, o_b, pz, dn_norm_g[layer], yc,
                             w_out[layer].astype(BF16), bd, ctx_len, with_ctx=not last, router=router)
        if layer % 2 == 0:
            assert not last
            x1, hx = outs
            i = layer // 2
            h = _dense_ffn(hx, x1, mod, *ffn_bf16)
            stream = (h, h, ctx_len // TM)
        else:
            assert last
            x1, hx, route = outs
            i = layer // 2
            dest, row_tok, blk_e, n_used = _moe_route(route[:, 2:2 + TOP_K].astype(jnp.int32))
            take = lambda rows_, idx: rows_.at[idx].get(mode="promise_in_bounds")
            parts = lambda a: (a[:a.shape[0] // MOE_FIRST_PART], a[a.shape[0] // MOE_FIRST_PART:])
            y = _moe_experts([take(hx, r) for r in parts(row_tok)], blk_e, n_used, *moe_bf16)
            h = _moe_combine_final(x1, [(take(y, dd[:, 0]), take(y, dd[:, 1])) for dd in parts(dest)],
                                   route, mod, final_norm_g)
    return h.reshape(bsz, n, d)
```

```python
import functools

import jax
import jax.numpy as jnp
from jax import lax
from jax.experimental import pallas as pl
from jax.experimental.pallas import tpu as pltpu

F32 = jnp.float32
BF16 = jnp.bfloat16

D = 1024
N_MOD = 6
EPS = 1e-6
NEG = -1e30
GRID_W = 64

CONV_CH = 256
DN_HEADS = 6
DN_HEAD_DIM = 64
DN_DIM = DN_HEADS * DN_HEAD_DIM
DN_CHUNK = 64
DN_SUB = 16
DN_STEP_CHUNKS = 4
ATT_HEADS = 6
ATT_KV_HEADS = 2
ATT_GROUP = ATT_HEADS // ATT_KV_HEADS
HEAD_DIM = 64
ATT_DIM = ATT_HEADS * HEAD_DIM
ATT_KV_DIM = ATT_KV_HEADS * HEAD_DIM
ATT_BLOCK = 128
ROPE_BASE = 10000.0
AXIS_DIM = HEAD_DIM // 2
MIX_DIM = CONV_CH + DN_DIM + ATT_DIM

D_FF = 2816
N_EXPERTS = 8
TOP_K = 2
D_FF_EXPERT = 3584

TM = 256
FF_CHUNK = 512
MOE_TM = 512
MOE_TF = 1792
MOE_SUB = 256
MOE_FIRST_PART = 4
LANE = 128
VMEM_LIMIT = 56 * 1024 * 1024

_C_QKV = 3 * CONV_CH
_C_Z = _C_QKV + 3 * DN_DIM
_C_A = _C_Z + DN_DIM
_C_Q = _C_A + 4 * DN_HEADS
_C_K = _C_Q + ATT_DIM
_C_V = _C_K + ATT_KV_DIM
_C_END = _C_V + ATT_KV_DIM
N_AB = 4 * DN_HEADS


def _params(sem=None, vmem=None):
    kw = {}
    if sem is not None:
        kw["dimension_semantics"] = sem
    if vmem is not None:
        kw["vmem_limit_bytes"] = vmem
    return pltpu.CompilerParams(**kw)


def _split2(a):
    hi = a.astype(BF16)
    lo = (a - hi.astype(F32)).astype(BF16)
    return hi, lo


def _split3(a):
    hi = a.astype(BF16)
    r = a - hi.astype(F32)
    mid = r.astype(BF16)
    lo = (r - mid.astype(F32)).astype(BF16)
    return hi, mid, lo


_NN = (((1,), (0,)), ((), ()))
_NT = (((1,), (1,)), ((), ()))
_TN = (((0,), (0,)), ((), ()))


def _mm(a, b, dims=_NN):
    return lax.dot_general(a.astype(BF16), b.astype(BF16), dims, preferred_element_type=F32)


def _mm3(a, b, dims=_NN):
    a1, a0 = _split2(a)
    b1, b0 = _split2(b)
    d = functools.partial(lax.dot_general, dimension_numbers=dims, preferred_element_type=F32)
    return d(a1, b1) + (d(a1, b0) + d(a0, b1))


_BNN = (((2,), (1,)), ((0,), (0,)))
_BNT = (((2,), (2,)), ((0,), (0,)))
_BTN = (((1,), (1,)), ((0,), (0,)))


def _bmm(a, b, dims=_BNN):
    return lax.dot_general(a.astype(BF16), b.astype(BF16), dims, preferred_element_type=F32)


def _sigmoid(x):
    return 1.0 / (1.0 + jnp.exp(-x))


def _softplus(x):
    return jnp.maximum(x, 0.0) + jnp.log1p(jnp.exp(-jnp.abs(x)))


def _mod_row(mod_ref, is_ctx):
    mod = mod_ref[...]
    return jnp.where(is_ctx, mod[1:2, :], mod[0:1, :])


def _rmsnorm_rows(x):
    return x * lax.rsqrt(jnp.mean(x * x, axis=-1, keepdims=True) + EPS)


def _shift_rows(u, prow, nrow):
    n = u.shape[0]
    rid = lax.broadcasted_iota(jnp.int32, u.shape, 0)
    up = jnp.where(rid == 0, prow, pltpu.roll(u, 1, 0))
    un = jnp.where(rid == n - 1, nrow, pltpu.roll(u, n - 1, 0))
    return up, un


def _same_group(shape, group):
    sh = group.bit_length() - 1
    assert 1 << sh == group
    return (lax.broadcasted_iota(jnp.int32, shape, 0) >> sh) == (lax.broadcasted_iota(jnp.int32, shape, 1) >> sh)


def _head_blockdiag(n, group):
    g = jnp.arange(n, dtype=jnp.int32) // group
    return (g[:, None] == g[None, :]).astype(BF16)


def _group_sum(t, bd):
    return _mm(t, bd)


MOD_TN = 1536


def _mod_kernel(s_ref, w_ref, b_ref, o_ref):
    s = s_ref[...]
    s = s * _sigmoid(s)
    o_ref[0] = _mm3(s, w_ref[0]) + b_ref[0]


def _mod_vectors(c, c_ctx, w_mod, b_mod):
    depth = w_mod.shape[0]
    s = jnp.zeros((8, D), F32).at[0].set(c[0]).at[1].set(c_ctx)
    return pl.pallas_call(
        _mod_kernel,
        grid=(depth, N_MOD * D // MOD_TN),
        in_specs=[
            pl.BlockSpec((8, D), lambda l, j: (0, 0)),
            pl.BlockSpec((1, D, MOD_TN), lambda l, j: (l, 0, j)),
            pl.BlockSpec((1, 1, MOD_TN), lambda l, j: (l, 0, j)),
        ],
        out_specs=pl.BlockSpec((1, 8, MOD_TN), lambda l, j: (l, 0, j)),
        out_shape=jax.ShapeDtypeStruct((depth, 8, N_MOD * D), F32),
        compiler_params=_params(("arbitrary", "arbitrary"), VMEM_LIMIT),
        name="mod_vectors",
    )(s, w_mod, b_mod.reshape(depth, 1, N_MOD * D))


def _halo_valid(i, nblk):
    return jnp.logical_and(i != 0, i != 1), jnp.logical_and(i != 0, i != nblk - 1)


def _halo_rows(prev_ref, next_ref, i, nblk):
    pvalid, nvalid = _halo_valid(i, nblk)
    prow = jnp.where(pvalid, prev_ref[7:8, :], 0.0)
    nrow = jnp.where(nvalid, next_ref[0:1, :], 0.0)
    return prow, nrow


def _in_kernel(ctx_ref, h_ref, hprev_ref, hnext_ref, mod_ref, w_ref, cw_ref, alog_ref, dtb_ref, bd_ref,
               pconv_ref, pz_ref, pq_ref, pkv_ref, q_ref, k_ref, v_ref, gb_ref, *, nblk):
    i = pl.program_id(0)
    row = _mod_row(mod_ref, i == 0)
    norm_mod = lambda x: _rmsnorm_rows(x) * (1.0 + row[:, D:2 * D]) + row[:, 0:D]
    h1 = norm_mod(jnp.where(i == 0, ctx_ref[...], h_ref[...])).astype(BF16)
    halo = norm_mod(jnp.concatenate([hprev_ref[...], hnext_ref[...]], axis=0)).astype(BF16)
    d = functools.partial(lax.dot_general, dimension_numbers=_NN, preferred_element_type=F32)
    tm = h1.shape[0]
    c0 = 3 * CONV_CH
    c1 = c0 + 3 * DN_DIM
    c2 = c1 + LANE
    c3 = c2 + DN_DIM + ATT_DIM
    pconv_ref[...] = d(h1, w_ref[:, 0:c0])
    zq = d(h1, w_ref[:, c2:c3])
    pz_ref[...] = zq[:, 0:DN_DIM]
    pq_ref[...] = zq[:, DN_DIM:]
    pkv_ref[...] = d(h1, w_ref[:, c3:c3 + 2 * ATT_KV_DIM])
    qkv_ab = d(jnp.concatenate([h1, halo], axis=0), w_ref[:, c0:c2])
    qkv = qkv_ab[:, 0:3 * DN_DIM]
    ab = qkv_ab[0:tm, 3 * DN_DIM:]

    u = qkv[0:tm, :]
    pvalid, nvalid = _halo_valid(i, nblk)
    prow = jnp.where(pvalid, qkv[tm + 7:tm + 8, :], 0.0)
    nrow = jnp.where(nvalid, qkv[tm + 8:tm + 9, :], 0.0)
    up, un = _shift_rows(u, prow, nrow)
    cw = cw_ref[...]
    y = up * cw[0:1, :] + u * cw[1:2, :] + un * cw[2:3, :]
    y = y * _sigmoid(y)
    q = y[:, 0:DN_DIM]
    k = y[:, DN_DIM:2 * DN_DIM]
    bd = bd_ref[...]
    q_ref[...] = q * lax.rsqrt(_group_sum(q * q, bd) + 1e-6) * (DN_HEAD_DIM ** -0.5)
    k_ref[...] = k * lax.rsqrt(_group_sum(k * k, bd) + 1e-6)
    v_ref[...] = y[:, 2 * DN_DIM:3 * DN_DIM]
    g = -jnp.exp(alog_ref[...]) * _softplus(ab + dtb_ref[...])
    lane = lax.broadcasted_iota(jnp.int32, ab.shape, 1)
    gb_ref[...] = jnp.where(lane < 2 * DN_HEADS, g, _sigmoid(ab))


def _in_proj(ctx_src, lat_src, lat_blk0, mod, w_main, dn_conv_w, a_log, dt_bias, bd):
    r8 = TM // 8
    last8 = lat_src.shape[0] // 8 - 1
    nblk = 1 + lat_src.shape[0] // TM - lat_blk0
    t = nblk * TM
    alog = jnp.zeros((1, LANE), F32).at[0, :2 * DN_HEADS].set(a_log.reshape(-1))
    dtb = jnp.zeros((1, LANE), F32).at[0, :2 * DN_HEADS].set(dt_bias.reshape(-1))
    widths = (3 * CONV_CH, DN_DIM, ATT_DIM, 2 * ATT_KV_DIM, DN_DIM, DN_DIM, DN_DIM, LANE)
    const = lambda i: (0, 0)
    lat = lambda i: jnp.maximum(i - 1, 0) + lat_blk0
    return pl.pallas_call(
        functools.partial(_in_kernel, nblk=nblk),
        grid=(nblk,),
        in_specs=[
            pl.BlockSpec((TM, D), const),
            pl.BlockSpec((TM, D), lambda i: (lat(i), 0)),
            pl.BlockSpec((8, D), lambda i: (jnp.maximum(lat(i) * r8 - 1, 0), 0)),
            pl.BlockSpec((8, D), lambda i: (jnp.minimum((lat(i) + 1) * r8, last8), 0)),
            pl.BlockSpec((8, N_MOD * D), const),
            pl.BlockSpec(w_main.shape, const),
            pl.BlockSpec((3, 3 * DN_DIM), const),
            pl.BlockSpec((1, LANE), const),
            pl.BlockSpec((1, LANE), const),
            pl.BlockSpec(bd.shape, const),
        ],
        out_specs=[pl.BlockSpec((TM, w), lambda i: (i, 0)) for w in widths],
        out_shape=[jax.ShapeDtypeStruct((t, w), F32) for w in widths],
        compiler_params=_params(("arbitrary",), VMEM_LIMIT),
        name="in_proj",
    )(ctx_src, lat_src, lat_src, lat_src, mod, w_main, dn_conv_w, alog, dtb, bd)


def _dn_chunk(rev, q_ref, k_ref, v_ref, gb_ref, o_ref, s_ref):
    c_ = DN_CHUNK
    ri = lax.broadcasted_iota(jnp.int32, (c_, c_), 0)
    ci = lax.broadcasted_iota(jnp.int32, (c_, c_), 1)
    incl = (ri <= ci) if rev else (ri >= ci)
    strict = (ri < ci) if rev else (ri > ci)
    same_sub = _same_group((c_, c_), DN_SUB)
    eye = jnp.where(ri == ci, 1.0, 0.0)
    tri = jnp.where(incl, 1.0, 0.0).astype(BF16)
    last = 0 if rev else c_ - 1

    nchunks = q_ref.shape[0] // c_
    nh = DN_HEADS
    col0 = nh if rev else 0
    rows = lambda g: slice(g * c_, (g + 1) * c_)
    lanes = lambda h: slice(h * DN_HEAD_DIM, (h + 1) * DN_HEAD_DIM)

    def stack(fn):
        return jnp.stack([fn(g, h) for g in range(nchunks) for h in range(nh)])

    gb = gb_ref[...]
    gcs = [_cumsum_rows(tri, gb[rows(g), :]) for g in range(nchunks)]
    gcts = [gc.T for gc in gcs]
    q = stack(lambda g, h: q_ref[rows(g), lanes(h)])
    k = stack(lambda g, h: k_ref[rows(g), lanes(h)])
    v = stack(lambda g, h: v_ref[rows(g), lanes(h)])
    gcol = stack(lambda g, h: gcs[g][:, col0 + h:col0 + h + 1])
    grow = stack(lambda g, h: gcts[g][col0 + h:col0 + h + 1, :])
    beta = stack(lambda g, h: gb[rows(g), 2 * nh + col0 + h:2 * nh + col0 + h + 1])
    glast = gcol[:, last:last + 1, :]
    decay = jnp.where(incl, jnp.exp(jnp.where(incl, gcol - grow, 0.0)), 0.0)
    eg = jnp.exp(gcol)
    kb = k * beta
    a = jnp.where(strict, _bmm(kb, k, _BNT) * decay, 0.0)
    qk = jnp.where(incl, _bmm(q, k, _BNT) * decay, 0.0)
    ad = jnp.where(same_sub, a, 0.0)
    ao = a - ad
    p = eye - ad
    n2 = _bmm(ad, ad)
    p = p + _bmm(p, n2)
    n4 = _bmm(n2, n2)
    p = p + _bmm(p, n4)
    n8 = _bmm(n4, n4)
    dinv = p + _bmm(p, n8)
    m = _bmm(dinv, ao)
    m2 = _bmm(m, m)
    y = _bmm(dinv, jnp.concatenate([v * beta, kb * eg], axis=-1))
    z = y + _bmm(m2, y)
    x = z - _bmm(m, z)
    u = x[:, :, :DN_HEAD_DIM]
    w = x[:, :, DN_HEAD_DIM:]
    qg = q * eg
    kd = k * jnp.exp(glast - gcol)
    gl = jnp.exp(glast)
    s = s_ref[col0:col0 + nh]
    for g in (reversed(range(nchunks)) if rev else range(nchunks)):
        b = slice(g * nh, (g + 1) * nh)
        v_new = u[b] - _bmm(w[b], s)
        o = _bmm(qg[b], s) + _bmm(qk[b], v_new)
        s = s * gl[b] + _bmm(kd[b], v_new, _BTN)
        for h in range(nh):
            o_ref[rows(g), lanes(h)] = o[h]
    s_ref[col0:col0 + nh] = s


def _cumsum_rows(tri_bf16, g):
    g2, g1, g0 = _split3(g)
    d = functools.partial(lax.dot_general, dimension_numbers=_NN, preferred_element_type=F32)
    return d(tri_bf16, g2) + (d(tri_bf16, g1) + d(tri_bf16, g0))


def _dn_kernel(qf, kf, vf, gf, qb, kb, vb, gbb, *rest, n_cast):
    cast_in = rest[:n_cast]
    of_ref, ob_ref = rest[n_cast:n_cast + 2]
    cast_out = rest[n_cast + 2:2 * n_cast + 2]
    s_ref = rest[-1]

    @pl.when(pl.program_id(0) == 0)
    def _():
        s_ref[...] = jnp.zeros(s_ref.shape, F32)

    _dn_chunk(False, qf, kf, vf, gf, of_ref, s_ref)
    _dn_chunk(True, qb, kb, vb, gbb, ob_ref, s_ref)
    for src, dst in zip(cast_in, cast_out):
        dst[...] = src[...].astype(BF16)


def _delta_net(q, k, v, gb, ctx_len, to_bf16=()):
    t = q.shape[0]
    rows = DN_STEP_CHUNKS * DN_CHUNK
    assert ctx_len == rows and t % rows == 0
    nstep = t // rows
    cast_specs = []
    for m in to_bf16:
        rb = -(-m.shape[0] // nstep)
        rb = -(-rb // 16) * 16
        nb = -(-m.shape[0] // rb)
        cast_specs.append(pl.BlockSpec((rb, m.shape[1]), lambda s, nb=nb: (jnp.minimum(s, nb - 1), 0)))

    def fwd(s):
        return (s, 0)

    def bwd(s):
        return (jnp.where(s == 0, 0, nstep - s), 0)

    wide = lambda im: pl.BlockSpec((rows, DN_DIM), im)
    narrow = lambda im: pl.BlockSpec((rows, LANE), im)
    return pl.pallas_call(
        functools.partial(_dn_kernel, n_cast=len(to_bf16)),
        grid=(nstep,),
        in_specs=[wide(fwd), wide(fwd), wide(fwd), narrow(fwd), wide(bwd), wide(bwd), wide(bwd), narrow(bwd)]
        + cast_specs,
        out_specs=[wide(fwd), wide(bwd)] + cast_specs,
        out_shape=[jax.ShapeDtypeStruct((t, DN_DIM), F32)] * 2
        + [jax.ShapeDtypeStruct(m.shape, BF16) for m in to_bf16],
        scratch_shapes=[pltpu.VMEM((2 * DN_HEADS, DN_HEAD_DIM, DN_HEAD_DIM), F32)],
        compiler_params=_params(("arbitrary",), VMEM_LIMIT),
        name="delta_net",
    )(q, k, v, gb, q, k, v, gb, *to_bf16)


def _rope(x, cos, sin):
    w = x.shape[1]
    lane = lax.broadcasted_iota(jnp.int32, x.shape, 1)
    first_half = (lane & (AXIS_DIM - 1)) < (AXIS_DIM // 2)
    swapped = jnp.where(first_half, pltpu.roll(x, w - AXIS_DIM // 2, 1), pltpu.roll(x, AXIS_DIM // 2, 1))
    return x * cos + swapped * sin


LOG2E = 1.4426950408889634


def _softmax_av(s, sink, vals):
    m = jnp.maximum(jnp.max(s, axis=-1, keepdims=True), sink)
    p = jnp.exp2(s - m)
    denom = jnp.sum(p, axis=-1, keepdims=True) + jnp.exp2(sink - m)
    return _mm(p, vals) / denom


def _attend(q, keys, vals, band, sink_all, o_ref, row0=0):
    b = q.shape[0]
    for kvh in range(ATT_KV_HEADS):
        kl = slice(kvh * HEAD_DIM, (kvh + 1) * HEAD_DIM)
        heads = range(kvh * ATT_GROUP, (kvh + 1) * ATT_GROUP)
        qs = jnp.concatenate([q[:, h * HEAD_DIM:(h + 1) * HEAD_DIM] for h in heads], axis=0)
        sink = jnp.concatenate([jnp.broadcast_to(sink_all[0:1, h:h + 1], (b, 1)) for h in heads], axis=0)
        s = _mm(qs, keys[:, kl], _NT)
        if band is not None:
            kb = ATT_BLOCK
            s = jnp.concatenate([jnp.where(band[0], s[:, 0:kb], NEG), s[:, kb:2 * kb],
                                 jnp.where(band[1], s[:, 2 * kb:3 * kb], NEG), s[:, 3 * kb:]], axis=1)
        o = _softmax_av(s, sink * LOG2E, vals[:, kl])
        for g, h in enumerate(heads):
            o_ref[row0:row0 + b, h * HEAD_DIM:(h + 1) * HEAD_DIM] = o[g * b:(g + 1) * b, :]


def _band_valid(first, last):
    b = ATT_BLOCK
    c = lax.broadcasted_iota(jnp.int32, (1, b), 1)
    r = lax.broadcasted_iota(jnp.int32, (ATT_GROUP * b, 1), 0) & (b - 1)
    prev_ok = jnp.where(first, -1, c) >= r
    next_ok = jnp.where(last, b, c) <= r
    return prev_ok, next_ok


def _rope_block(rowtab_ref, coltab_ref, blk):
    rt = rowtab_ref[blk]
    ct = coltab_ref[...]
    reps = ATT_BLOCK // GRID_W
    rows = jnp.concatenate([jnp.broadcast_to(rt[g:g + 1, :], (GRID_W, LANE)) for g in range(reps)], axis=0)
    return rows + jnp.concatenate([ct] * reps, axis=0)


def _attn_kernel(q_ref, kp_ref, kc_ref, kn_ref, kctx_ref, cosr_ref, sinr_ref, cosc_ref, sinc_ref, sink_ref, o_ref,
                 *, nb, with_ctx):
    j = pl.program_id(0)
    b = ATT_BLOCK
    scale = HEAD_DIM ** -0.5 * LOG2E

    @pl.when(j == 0)
    def _():
        if with_ctx:
            kvx = kctx_ref[...]
            _attend(q_ref[...] * scale, kvx[:, :ATT_KV_DIM], kvx[:, ATT_KV_DIM:], None, sink_ref[...], o_ref)
        else:
            o_ref[...] = jnp.zeros(o_ref.shape, F32)

    @pl.when(j > 0)
    def _():
        b0 = 2 * (j - 1)
        blocks = (jnp.maximum(b0 - 1, 0), b0, b0 + 1, jnp.minimum(b0 + 2, nb - 1))
        cos = [_rope_block(cosr_ref, cosc_ref, blk) for blk in blocks]
        sin = [_rope_block(sinr_ref, sinc_ref, blk) for blk in blocks]
        kvc = kc_ref[...]
        kv = (kp_ref[...], kvc[0:b, :], kvc[b:2 * b, :], kn_ref[...])
        kvx = kctx_ref[...]
        keys = [_rope(t[:, :ATT_KV_DIM], c_, s_) for t, c_, s_ in zip(kv, cos, sin)]
        q_all = q_ref[...]
        sink_all = sink_ref[...]
        for sub in range(2):
            q = q_all[sub * b:(sub + 1) * b, :]
            q = jnp.concatenate([_rope(q[:, l * LANE:(l + 1) * LANE], cos[1 + sub], sin[1 + sub])
                                 for l in range(ATT_DIM // LANE)], axis=1)
            kcat = jnp.concatenate(keys[sub:sub + 3] + [kvx[:, :ATT_KV_DIM]], axis=0)
            vcat = jnp.concatenate([t[:, ATT_KV_DIM:] for t in kv[sub:sub + 3]] + [kvx[:, ATT_KV_DIM:]], axis=0)
            band = _band_valid(first=(b0 + sub == 0), last=(b0 + sub == nb - 1))
            _attend(q * scale, kcat, vcat, band, sink_all, o_ref, row0=sub * b)


def _attention(pq, pkv, rope, sink, ctx_len, with_ctx):
    t = pq.shape[0]
    n = t - ctx_len
    nb = n // ATT_BLOCK
    step = 2 * ATT_BLOCK
    assert ctx_len == step and nb % 2 == 0
    off = ctx_len // ATT_BLOCK
    half = lambda im: pl.BlockSpec((ATT_BLOCK, 2 * ATT_KV_DIM), im)
    lat = lambda j: 2 * jnp.maximum(j - 1, 0)
    prv = lambda j: (jnp.maximum(lat(j) - 1, 0) + off, 0)
    nxt = lambda j: (jnp.minimum(lat(j) + 2, nb - 1) + off, 0)
    cur = lambda j: (j, 0)
    whole = lambda a: pl.BlockSpec(a.shape, lambda j: (0,) * a.ndim)
    return pl.pallas_call(
        functools.partial(_attn_kernel, nb=nb, with_ctx=with_ctx),
        grid=(1 + nb // 2,),
        in_specs=[
            pl.BlockSpec((step, ATT_DIM), cur),
            half(prv), pl.BlockSpec((step, 2 * ATT_KV_DIM), cur), half(nxt),
            pl.BlockSpec((ctx_len, 2 * ATT_KV_DIM), lambda j: (0, 0)),
        ] + [whole(a) for a in rope] + [pl.BlockSpec((8, LANE), lambda j: (0, 0))],
        out_specs=pl.BlockSpec((step, ATT_DIM), cur),
        out_shape=jax.ShapeDtypeStruct((t, ATT_DIM), F32),
        compiler_params=_params(("arbitrary",), VMEM_LIMIT),
        name="attention",
    )(pq, pkv, pkv, pkv, pkv, *rope, sink)


def _mixfin_kernel(ctx_ref, h_ref, mod_ref, pconv_ref, prev_ref, next_ref, cw_ref, of_ref, ob_ref, z_ref, ng_ref,
                   yc_ref, wout_ref, bd_ref, *rest, nblk, blk0, with_router):
    if with_router:
        wr1_ref, wr0_ref, x_ref, hx_ref, lg_ref = rest
    else:
        wg_ref, wu_ref, wd_ref, x_ref = rest
    i = pl.program_id(0) + blk0
    row = _mod_row(mod_ref, i == 0)
    pc = pconv_ref[...]
    u = pc[:, CONV_CH:2 * CONV_CH] * pc[:, 2 * CONV_CH:]
    prow, nrow = _halo_rows(prev_ref, next_ref, i, nblk)
    prow = prow[:, CONV_CH:2 * CONV_CH] * prow[:, 2 * CONV_CH:]
    nrow = nrow[:, CONV_CH:2 * CONV_CH] * nrow[:, 2 * CONV_CH:]
    up, un = _shift_rows(u, prow, nrow)
    cw = cw_ref[...]
    ya = pc[:, :CONV_CH] * (up * cw[0:1, :] + u * cw[1:2, :] + un * cw[2:3, :])
    o = of_ref[...] + ob_ref[...]
    ms = _group_sum(o * o, bd_ref[...]) * (1.0 / DN_HEAD_DIM)
    z = z_ref[...]
    yb = o * lax.rsqrt(ms + EPS) * ng_ref[...] * (z * _sigmoid(z))
    mix = jnp.concatenate([ya, yb, yc_ref[...]], axis=1)
    x = jnp.where(i == 0, ctx_ref[...], h_ref[...]) + row[:, 2 * D:3 * D] * _mm(mix, wout_ref[...])
    hx = _rmsnorm_rows(x) * (1.0 + row[:, 4 * D:5 * D]) + row[:, 3 * D:4 * D]
    if not with_router:
        hb = hx.astype(BF16)
        acc = jnp.zeros((hb.shape[0], D), F32)
        for f in range(0, D_FF, FF_CHUNK):
            fe = min(f + FF_CHUNK, D_FF)
            g = _mm(hb, wg_ref[:, f:fe])
            u_ = _mm(hb, wu_ref[:, f:fe])
            acc = acc + _mm(g * _sigmoid(g) * u_, wd_ref[f:fe, :])
        x_ref[...] = x + row[:, 5 * D:6 * D] * acc
    else:
        x_ref[...] = x
        hx_ref[...] = hx
        h1, h0 = _split2(hx)
        d = functools.partial(lax.dot_general, dimension_numbers=_NN, preferred_element_type=F32)
        lg = d(h1, wr1_ref[...]) + (d(h0, wr1_ref[...]) + d(h1, wr0_ref[...]))
        lane = lax.broadcasted_iota(jnp.int32, lg.shape, 1)
        lanef = lane.astype(F32)
        lg = jnp.where(lane < N_EXPERTS, lg, -jnp.inf)
        m1 = jnp.max(lg, axis=-1, keepdims=True)
        i1 = jnp.min(jnp.where(lg == m1, lanef, float(LANE)), axis=-1, keepdims=True)
        rest = jnp.where(lanef == i1, -jnp.inf, lg)
        m2 = jnp.max(rest, axis=-1, keepdims=True)
        i2 = jnp.min(jnp.where(rest == m2, lanef, float(LANE)), axis=-1, keepdims=True)
        e2 = jnp.exp(m2 - m1)
        g1 = 1.0 / (1.0 + e2)
        lg_ref[...] = jnp.where(lane == 0, g1, jnp.where(lane == 1, e2 * g1, jnp.where(lane == 2, i1, i2)))


def _mixer_finish(ctx_src, lat_src, lat_blk0, mod, pconv, conv_w, o_f, o_b, pz, norm_g, yc, w_out, bd, ctx_len,
                  with_ctx, router=None, ffn=None):
    assert (router is None) != (ffn is None)
    t = pconv.shape[0]
    nblk = t // TM
    blk0 = 0 if with_ctx else ctx_len // TM
    rows = t - blk0 * TM
    r8 = TM // 8
    w = pconv.shape[1]
    cur = lambda i: (i + blk0, 0)
    out_cur = lambda i: (i, 0)
    const = lambda i: (0, 0)
    ng = jnp.tile(norm_g.reshape(1, DN_HEAD_DIM), (1, DN_HEADS))
    in_specs = [
        pl.BlockSpec((TM, D), const),
        pl.BlockSpec((TM, D), lambda i: (jnp.maximum(i + blk0 - 1, 0) + lat_blk0, 0)),
        pl.BlockSpec((8, N_MOD * D), const),
        pl.BlockSpec((TM, w), cur),
        pl.BlockSpec((8, w), lambda i: (jnp.maximum((i + blk0) * r8 - 1, 0), 0)),
        pl.BlockSpec((8, w), lambda i: (jnp.minimum((i + blk0 + 1) * r8, t // 8 - 1), 0)),
        pl.BlockSpec((3, CONV_CH), const),
        pl.BlockSpec((TM, DN_DIM), cur),
        pl.BlockSpec((TM, DN_DIM), cur),
        pl.BlockSpec((TM, DN_DIM), cur),
        pl.BlockSpec((1, DN_DIM), const),
        pl.BlockSpec((TM, ATT_DIM), cur),
        pl.BlockSpec((MIX_DIM, D), const),
        pl.BlockSpec(bd.shape, const),
    ]
    args = [ctx_src, lat_src, mod, pconv, pconv, pconv, conv_w, o_f, o_b, pz, ng, yc, w_out, bd]
    out_specs = [pl.BlockSpec((TM, D), out_cur)]
    out_shape = [jax.ShapeDtypeStruct((rows, D), F32)]
    if router is not None:
        in_specs += [pl.BlockSpec((D, LANE), const)] * 2
        args += list(router)
        out_specs += [pl.BlockSpec((TM, D), out_cur), pl.BlockSpec((TM, LANE), out_cur)]
        out_shape += [jax.ShapeDtypeStruct((rows, D), F32), jax.ShapeDtypeStruct((rows, LANE), F32)]
    else:
        once = dict(pipeline_mode=pl.Buffered(1))
        in_specs += [pl.BlockSpec(w_.shape, const, **once) for w_ in ffn]
        args += list(ffn)
    return pl.pallas_call(
        functools.partial(_mixfin_kernel, nblk=nblk, blk0=blk0, with_router=router is not None),
        grid=(rows // TM,),
        in_specs=in_specs,
        out_specs=out_specs,
        out_shape=out_shape,
        compiler_params=_params(("arbitrary",), VMEM_LIMIT),
        name="mixer_finish",
    )(*args)


def _moe_kernel(be_ref, nu_ref, xs_ref, wg_ref, wu_ref, wd_ref, *rest, boff):
    y_ref, acc_ref = rest[-2:]
    b = pl.program_id(0) + boff
    f = pl.program_id(1)

    @pl.when(b < nu_ref[0])
    def _():
        xs = xs_ref[...].astype(BF16)
        part = jnp.zeros((MOE_TM, D), F32)
        for c in range(0, MOE_TF, MOE_SUB):
            g = _mm(xs, wg_ref[0, :, c:c + MOE_SUB])
            u = _mm(xs, wu_ref[0, :, c:c + MOE_SUB])
            part = part + _mm(g * _sigmoid(g) * u, wd_ref[0, c:c + MOE_SUB, :])

        @pl.when(f == 0)
        def _():
            acc_ref[...] = part

        @pl.when(f != 0)
        def _():
            acc_ref[...] += part

        @pl.when(f == pl.num_programs(1) - 1)
        def _():
            y_ref[...] = acc_ref[...]

    @pl.when(b >= nu_ref[0])
    def _():
        y_ref[...] = jnp.zeros(y_ref.shape, F32)


def _moe_experts(xs_parts, blk_e, n_used, wg, wu, wd):
    cap = sum(xs.shape[0] for xs in xs_parts)
    nf = D_FF_EXPERT // MOE_TF
    y = None
    boff = 0
    for xs in xs_parts:
        nblk_p = xs.shape[0] // MOE_TM

        def fidx(b, f, nu):
            return jnp.where(b < nu[0], f, nf - 1)

        g = lambda b, boff=boff: b + boff
        in_specs = [
            pl.BlockSpec((MOE_TM, D), lambda b, f, be, nu: (b, 0)),
            pl.BlockSpec((1, D, MOE_TF), lambda b, f, be, nu, g=g: (be[g(b)], 0, fidx(g(b), f, nu))),
            pl.BlockSpec((1, D, MOE_TF), lambda b, f, be, nu, g=g: (be[g(b)], 0, fidx(g(b), f, nu))),
            pl.BlockSpec((1, MOE_TF, D), lambda b, f, be, nu, g=g: (be[g(b)], fidx(g(b), f, nu), 0)),
        ]
        args = [blk_e, n_used, xs, wg, wu, wd]
        aliases = {}
        if y is not None:
            in_specs.append(pl.BlockSpec(memory_space=pl.ANY))
            aliases = {len(args): 0}
            args.append(y)
        grid_spec = pltpu.PrefetchScalarGridSpec(
            num_scalar_prefetch=2,
            grid=(nblk_p, nf),
            in_specs=in_specs,
            out_specs=pl.BlockSpec((MOE_TM, D), lambda b, f, be, nu, g=g: (g(b), 0)),
            scratch_shapes=[pltpu.VMEM((MOE_TM, D), F32)],
        )
        y = pl.pallas_call(
            functools.partial(_moe_kernel, boff=boff),
            grid_spec=grid_spec,
            out_shape=jax.ShapeDtypeStruct((cap, D), F32),
            input_output_aliases=aliases,
            compiler_params=_params(("arbitrary", "arbitrary"), VMEM_LIMIT),
            name="moe_experts",
        )(*args)
        boff += nblk_p
    return y


def _moe_route(top_e):
    n = top_e.shape[0]
    a = n * TOP_K
    flat_e = top_e.reshape(a)
    onehot = (flat_e[:, None] == jnp.arange(N_EXPERTS, dtype=flat_e.dtype)[None, :]).astype(jnp.int32)
    counts = jnp.sum(onehot, axis=0)
    padded = (counts + MOE_TM - 1) // MOE_TM * MOE_TM
    pad_ends = jnp.cumsum(padded)
    pad_starts = pad_ends - padded
    dest = jnp.sum(onehot * (jnp.cumsum(onehot, axis=0) - onehot + pad_starts[None, :]), axis=1)
    cap = a + N_EXPERTS * MOE_TM
    nblk = cap // MOE_TM
    row_tok = (jnp.arange(cap, dtype=jnp.int32) % n).at[dest].set(
        jnp.arange(a, dtype=jnp.int32) // TOP_K, unique_indices=True, mode="promise_in_bounds")
    blk_start = jnp.arange(nblk, dtype=jnp.int32) * MOE_TM
    blk_e = jnp.minimum(jnp.sum((pad_ends[None, :] <= blk_start[:, None]).astype(jnp.int32), axis=1),
                        N_EXPERTS - 1)
    n_used = (pad_ends[-1] // MOE_TM).astype(jnp.int32).reshape(1)
    last_e = blk_e[jnp.maximum(n_used[0] - 1, 0)]
    blk_e = jnp.where(jnp.arange(nblk) < n_used[0], blk_e, last_e)
    return dest.reshape(n, TOP_K), row_tok, blk_e, n_used


def _final_kernel(x_ref, y0_ref, y1_ref, gt_ref, mod_ref, fg_ref, *rest):
    o_ref = rest[-1]
    mod = mod_ref[...]
    gt = gt_ref[...]
    f = gt[:, 0:1] * y0_ref[...] + gt[:, 1:2] * y1_ref[...]
    x = x_ref[...] + mod[0:1, 5 * D:6 * D] * f
    o_ref[...] = _rmsnorm_rows(x) * fg_ref[...]


def _moe_combine_final(x, y_parts, gt, mod, final_g):
    n = x.shape[0]
    const = lambda i: (0, 0)
    out = None
    boff = 0
    for y0, y1 in y_parts:
        nblk_p = y0.shape[0] // TM
        glob = lambda i, off=boff: (i + off, 0)
        loc = lambda i: (i, 0)
        in_specs = [pl.BlockSpec((TM, D), glob), pl.BlockSpec((TM, D), loc), pl.BlockSpec((TM, D), loc),
                    pl.BlockSpec((TM, LANE), glob), pl.BlockSpec((8, N_MOD * D), const),
                    pl.BlockSpec((1, D), const)]
        args = [x, y0, y1, gt, mod, final_g.reshape(1, D)]
        aliases = {}
        if out is not None:
            in_specs.append(pl.BlockSpec(memory_space=pl.ANY))
            aliases = {len(args): 0}
            args.append(out)
        out = pl.pallas_call(
            _final_kernel,
            grid=(nblk_p,),
            in_specs=in_specs,
            out_specs=pl.BlockSpec((TM, D), glob),
            out_shape=jax.ShapeDtypeStruct((n, D), F32),
            input_output_aliases=aliases,
            compiler_params=_params(("arbitrary",), VMEM_LIMIT),
            name="moe_combine_final",
        )(*args)
        boff += nblk_p
    return out


def _rope_tables(n):
    lane = jnp.arange(LANE, dtype=jnp.int32) % HEAD_DIM
    inv = ROPE_BASE ** (-jnp.arange(0, AXIS_DIM, 2, dtype=F32) / AXIS_DIM)
    freq = inv[lane % (AXIS_DIM // 2)]
    row_axis = (lane // AXIS_DIM) == 0
    sign = jnp.where((lane % AXIS_DIM) < AXIS_DIM // 2, -1.0, 1.0)
    reps = ATT_BLOCK // GRID_W

    def tables(count, on_axis):
        ang = jnp.arange(count, dtype=F32)[:, None] * freq[None, :]
        return jnp.where(on_axis, jnp.cos(ang), 0.0), jnp.where(on_axis, jnp.sin(ang) * sign, 0.0)

    by_block = lambda t: jnp.pad(t.reshape(-1, reps, LANE), ((0, 0), (0, 8 - reps), (0, 0)))
    cosr, sinr = tables(n // GRID_W, row_axis)
    cosc, sinc = tables(GRID_W, ~row_axis)
    return by_block(cosr), by_block(sinr), cosc, sinc


def _prep_w_in(w):
    pad = jnp.zeros((D, LANE - N_AB), w.dtype)
    return jnp.concatenate([w[:, :_C_Z], w[:, _C_A:_C_Q], pad, w[:, _C_Z:_C_A], w[:, _C_Q:_C_END]],
                           axis=1).astype(BF16)


def kernel(x, c, ctx, c_ctx, w_mod, b_mod, w_in, w_out, conv_w, dn_conv_w, dn_a_log, dn_dt_bias, dn_norm_g,
           attn_sink, ffn_w_gate, ffn_w_up, ffn_w_down, moe_router, moe_w_gate, moe_w_up, moe_w_down,
           final_norm_g):
    bsz, n, d = x.shape
    ctx_len = ctx.shape[1]
    depth = w_in.shape[0]
    assert bsz == 1 and d == D and ctx_len == TM and n % TM == 0 and n % GRID_W == 0
    rope = _rope_tables(n)
    mods = _mod_vectors(c, c_ctx, w_mod, b_mod)
    bd = _head_blockdiag(DN_DIM, DN_HEAD_DIM)
    stream = (ctx[0], x[0], 0)
    for layer in range(depth):
        last = layer == depth - 1
        mod = mods[layer]
        pconv, pz, pq, pkv, qn, kn, vv, gb = _in_proj(*stream, mod, _prep_w_in(w_in[layer]), dn_conv_w[layer],
                                                      dn_a_log[layer], dn_dt_bias[layer], bd)
        to_bf16 = []
        if layer % 2 == 0:
            to_bf16 += [w[layer // 2] for w in (ffn_w_gate, ffn_w_up, ffn_w_down)]
        if (layer + 1) % 2 == 1 and layer + 1 < depth:
            j = (layer + 1) // 2
            to_bf16 += [w[j].reshape(-1, w.shape[-1]) for w in (moe_w_gate, moe_w_up, moe_w_down)]
        o_f, o_b, *cast = _delta_net(qn, kn, vv, gb, ctx_len, tuple(to_bf16))
        if layer % 2 == 0:
            ffn_bf16, cast = cast[:3], cast[3:]
        if cast:
            moe_bf16 = [c.reshape(w.shape[1:]) for c, w in zip(cast, (moe_w_gate, moe_w_up, moe_w_down))]
        sink = jnp.zeros((8, LANE), F32).at[0, :ATT_HEADS].set(attn_sink[layer])
        yc = _attention(pq, pkv, rope, sink, ctx_len, with_ctx=not last)
        router = None
        if layer % 2 == 1:
            wr = jnp.zeros((D, LANE), F32).at[:, :N_EXPERTS].set(moe_router[layer // 2])
            wr1 = wr.astype(BF16)
            router = (wr1, (wr - wr1.astype(F32)).astype(BF16))
        outs = _mixer_finish(*stream, mod, pconv, conv_w[layer], o_f, o_b, pz, dn_norm_g[layer], yc,
                             w_out[layer].astype(BF16), bd, ctx_len, with_ctx=not last, router=router,
                             ffn=ffn_bf16 if layer % 2 == 0 else None)
        if layer % 2 == 0:
            assert not last
            h, = outs
            stream = (h, h, ctx_len // TM)
        else:
            assert last
            x1, hx, route = outs
            i = layer // 2
            dest, row_tok, blk_e, n_used = _moe_route(route[:, 2:2 + TOP_K].astype(jnp.int32))
            take = lambda rows_, idx: rows_.at[idx].get(mode="promise_in_bounds")
            parts = lambda a: (a[:a.shape[0] // MOE_FIRST_PART], a[a.shape[0] // MOE_FIRST_PART:])
            y = _moe_experts([take(hx, r) for r in parts(row_tok)], blk_e, n_used, *moe_bf16)
            h = _moe_combine_final(x1, [(take(y, dd[:, 0]), take(y, dd[:, 1])) for dd in parts(dest)],
                                   route, mod, final_norm_g)
    return h.reshape(bsz, n, d)
```

```python
import functools

import jax
import jax.numpy as jnp
from jax import lax
from jax.experimental import pallas as pl
from jax.experimental.pallas import tpu as pltpu

F32 = jnp.float32
BF16 = jnp.bfloat16

D = 1024
N_MOD = 6
EPS = 1e-6
NEG = -1e30
GRID_W = 64

CONV_CH = 256
DN_HEADS = 6
DN_HEAD_DIM = 64
DN_DIM = DN_HEADS * DN_HEAD_DIM
DN_CHUNK = 64
DN_SUB = 16
DN_STEP_CHUNKS = 4
ATT_HEADS = 6
ATT_KV_HEADS = 2
ATT_GROUP = ATT_HEADS // ATT_KV_HEADS
HEAD_DIM = 64
ATT_DIM = ATT_HEADS * HEAD_DIM
ATT_KV_DIM = ATT_KV_HEADS * HEAD_DIM
ATT_BLOCK = 128
ROPE_BASE = 10000.0
AXIS_DIM = HEAD_DIM // 2
MIX_DIM = CONV_CH + DN_DIM + ATT_DIM

D_FF = 2816
N_EXPERTS = 8
TOP_K = 2
D_FF_EXPERT = 3584

TM = 256
FF_CHUNK = 512
MOE_TM = 512
MOE_TF = 1792
MOE_SUB = 256
MOE_FIRST_PART = 4
COMBINE_FIRST_PART = 2
LANE = 128
VMEM_LIMIT = 56 * 1024 * 1024

_C_QKV = 3 * CONV_CH
_C_Z = _C_QKV + 3 * DN_DIM
_C_A = _C_Z + DN_DIM
_C_Q = _C_A + 4 * DN_HEADS
_C_K = _C_Q + ATT_DIM
_C_V = _C_K + ATT_KV_DIM
_C_END = _C_V + ATT_KV_DIM
N_AB = 4 * DN_HEADS


def _params(sem=None, vmem=None):
    kw = {}
    if sem is not None:
        kw["dimension_semantics"] = sem
    if vmem is not None:
        kw["vmem_limit_bytes"] = vmem
    return pltpu.CompilerParams(**kw)


def _split2(a):
    hi = a.astype(BF16)
    lo = (a - hi.astype(F32)).astype(BF16)
    return hi, lo


def _split3(a):
    hi = a.astype(BF16)
    r = a - hi.astype(F32)
    mid = r.astype(BF16)
    lo = (r - mid.astype(F32)).astype(BF16)
    return hi, mid, lo


_NN = (((1,), (0,)), ((), ()))
_NT = (((1,), (1,)), ((), ()))
_TN = (((0,), (0,)), ((), ()))


def _mm(a, b, dims=_NN):
    return lax.dot_general(a.astype(BF16), b.astype(BF16), dims, preferred_element_type=F32)


def _mm3(a, b, dims=_NN):
    a1, a0 = _split2(a)
    b1, b0 = _split2(b)
    d = functools.partial(lax.dot_general, dimension_numbers=dims, preferred_element_type=F32)
    return d(a1, b1) + (d(a1, b0) + d(a0, b1))


_BNN = (((2,), (1,)), ((0,), (0,)))
_BNT = (((2,), (2,)), ((0,), (0,)))
_BTN = (((1,), (1,)), ((0,), (0,)))


def _bmm(a, b, dims=_BNN):
    return lax.dot_general(a.astype(BF16), b.astype(BF16), dims, preferred_element_type=F32)


def _sigmoid(x):
    return 1.0 / (1.0 + jnp.exp(-x))


def _softplus(x):
    return jnp.maximum(x, 0.0) + jnp.log1p(jnp.exp(-jnp.abs(x)))


def _mod_row(mod_ref, is_ctx):
    mod = mod_ref[...]
    return jnp.where(is_ctx, mod[1:2, :], mod[0:1, :])


def _rmsnorm_rows(x):
    return x * lax.rsqrt(jnp.mean(x * x, axis=-1, keepdims=True) + EPS)


def _shift_rows(u, prow, nrow):
    n = u.shape[0]
    rid = lax.broadcasted_iota(jnp.int32, u.shape, 0)
    up = jnp.where(rid == 0, prow, pltpu.roll(u, 1, 0))
    un = jnp.where(rid == n - 1, nrow, pltpu.roll(u, n - 1, 0))
    return up, un


def _same_group(shape, group):
    sh = group.bit_length() - 1
    assert 1 << sh == group
    return (lax.broadcasted_iota(jnp.int32, shape, 0) >> sh) == (lax.broadcasted_iota(jnp.int32, shape, 1) >> sh)


def _head_blockdiag(n, group):
    g = jnp.arange(n, dtype=jnp.int32) // group
    return (g[:, None] == g[None, :]).astype(BF16)


def _group_sum(t, bd):
    return _mm(t, bd)


MOD_TN = 1536


def _mod_kernel(s_ref, w_ref, b_ref, o_ref):
    s = s_ref[...]
    s = s * _sigmoid(s)
    o_ref[0] = _mm3(s, w_ref[0]) + b_ref[0]


def _mod_vectors(c, c_ctx, w_mod, b_mod):
    depth = w_mod.shape[0]
    s = jnp.zeros((8, D), F32).at[0].set(c[0]).at[1].set(c_ctx)
    return pl.pallas_call(
        _mod_kernel,
        grid=(depth, N_MOD * D // MOD_TN),
        in_specs=[
            pl.BlockSpec((8, D), lambda l, j: (0, 0)),
            pl.BlockSpec((1, D, MOD_TN), lambda l, j: (l, 0, j)),
            pl.BlockSpec((1, 1, MOD_TN), lambda l, j: (l, 0, j)),
        ],
        out_specs=pl.BlockSpec((1, 8, MOD_TN), lambda l, j: (l, 0, j)),
        out_shape=jax.ShapeDtypeStruct((depth, 8, N_MOD * D), F32),
        compiler_params=_params(("arbitrary", "arbitrary"), VMEM_LIMIT),
        name="mod_vectors",
    )(s, w_mod, b_mod.reshape(depth, 1, N_MOD * D))


def _halo_valid(i, nblk):
    return jnp.logical_and(i != 0, i != 1), jnp.logical_and(i != 0, i != nblk - 1)


def _halo_rows(prev_ref, next_ref, i, nblk):
    pvalid, nvalid = _halo_valid(i, nblk)
    prow = jnp.where(pvalid, prev_ref[7:8, :], 0.0)
    nrow = jnp.where(nvalid, next_ref[0:1, :], 0.0)
    return prow, nrow


def _in_kernel(ctx_ref, h_ref, hprev_ref, hnext_ref, mod_ref, w_ref, cw_ref, alog_ref, dtb_ref, bd_ref,
               pconv_ref, pz_ref, pq_ref, pkv_ref, q_ref, k_ref, v_ref, gb_ref, *, nblk):
    i = pl.program_id(0)
    row = _mod_row(mod_ref, i == 0)
    norm_mod = lambda x: _rmsnorm_rows(x) * (1.0 + row[:, D:2 * D]) + row[:, 0:D]
    h1 = norm_mod(jnp.where(i == 0, ctx_ref[...], h_ref[...])).astype(BF16)
    halo = norm_mod(jnp.concatenate([hprev_ref[...], hnext_ref[...]], axis=0)).astype(BF16)
    d = functools.partial(lax.dot_general, dimension_numbers=_NN, preferred_element_type=F32)
    tm = h1.shape[0]
    c0 = 3 * CONV_CH
    c1 = c0 + 3 * DN_DIM
    c2 = c1 + LANE
    c3 = c2 + DN_DIM + ATT_DIM
    pconv_ref[...] = d(h1, w_ref[:, 0:c0])
    zq = d(h1, w_ref[:, c2:c3])
    pz_ref[...] = zq[:, 0:DN_DIM]
    pq_ref[...] = zq[:, DN_DIM:]
    pkv_ref[...] = d(h1, w_ref[:, c3:c3 + 2 * ATT_KV_DIM])
    qkv_ab = d(jnp.concatenate([h1, halo], axis=0), w_ref[:, c0:c2])
    qkv = qkv_ab[:, 0:3 * DN_DIM]
    ab = qkv_ab[0:tm, 3 * DN_DIM:]

    u = qkv[0:tm, :]
    pvalid, nvalid = _halo_valid(i, nblk)
    prow = jnp.where(pvalid, qkv[tm + 7:tm + 8, :], 0.0)
    nrow = jnp.where(nvalid, qkv[tm + 8:tm + 9, :], 0.0)
    up, un = _shift_rows(u, prow, nrow)
    cw = cw_ref[...]
    y = up * cw[0:1, :] + u * cw[1:2, :] + un * cw[2:3, :]
    y = y * _sigmoid(y)
    q = y[:, 0:DN_DIM]
    k = y[:, DN_DIM:2 * DN_DIM]
    bd = bd_ref[...]
    q_ref[...] = q * lax.rsqrt(_group_sum(q * q, bd) + 1e-6) * (DN_HEAD_DIM ** -0.5)
    k_ref[...] = k * lax.rsqrt(_group_sum(k * k, bd) + 1e-6)
    v_ref[...] = y[:, 2 * DN_DIM:3 * DN_DIM]
    g = -jnp.exp(alog_ref[...]) * _softplus(ab + dtb_ref[...])
    lane = lax.broadcasted_iota(jnp.int32, ab.shape, 1)
    gb_ref[...] = jnp.where(lane < 2 * DN_HEADS, g, _sigmoid(ab))


def _in_proj(ctx_src, lat_src, lat_blk0, mod, w_main, dn_conv_w, a_log, dt_bias, bd):
    r8 = TM // 8
    last8 = lat_src.shape[0] // 8 - 1
    nblk = 1 + lat_src.shape[0] // TM - lat_blk0
    t = nblk * TM
    alog = jnp.zeros((1, LANE), F32).at[0, :2 * DN_HEADS].set(a_log.reshape(-1))
    dtb = jnp.zeros((1, LANE), F32).at[0, :2 * DN_HEADS].set(dt_bias.reshape(-1))
    widths = (3 * CONV_CH, DN_DIM, ATT_DIM, 2 * ATT_KV_DIM, DN_DIM, DN_DIM, DN_DIM, LANE)
    const = lambda i: (0, 0)
    lat = lambda i: jnp.maximum(i - 1, 0) + lat_blk0
    return pl.pallas_call(
        functools.partial(_in_kernel, nblk=nblk),
        grid=(nblk,),
        in_specs=[
            pl.BlockSpec((TM, D), const),
            pl.BlockSpec((TM, D), lambda i: (lat(i), 0)),
            pl.BlockSpec((8, D), lambda i: (jnp.maximum(lat(i) * r8 - 1, 0), 0)),
            pl.BlockSpec((8, D), lambda i: (jnp.minimum((lat(i) + 1) * r8, last8), 0)),
            pl.BlockSpec((8, N_MOD * D), const),
            pl.BlockSpec(w_main.shape, const),
            pl.BlockSpec((3, 3 * DN_DIM), const),
            pl.BlockSpec((1, LANE), const),
            pl.BlockSpec((1, LANE), const),
            pl.BlockSpec(bd.shape, const),
        ],
        out_specs=[pl.BlockSpec((TM, w), lambda i: (i, 0)) for w in widths],
        out_shape=[jax.ShapeDtypeStruct((t, w), F32) for w in widths],
        compiler_params=_params(("arbitrary",), VMEM_LIMIT),
        name="in_proj",
    )(ctx_src, lat_src, lat_src, lat_src, mod, w_main, dn_conv_w, alog, dtb, bd)


def _dn_chunk(rev, q_ref, k_ref, v_ref, gb_ref, o_ref, s_ref):
    c_ = DN_CHUNK
    ri = lax.broadcasted_iota(jnp.int32, (c_, c_), 0)
    ci = lax.broadcasted_iota(jnp.int32, (c_, c_), 1)
    incl = (ri <= ci) if rev else (ri >= ci)
    strict = (ri < ci) if rev else (ri > ci)
    same_sub = _same_group((c_, c_), DN_SUB)
    eye = jnp.where(ri == ci, 1.0, 0.0)
    tri = jnp.where(incl, 1.0, 0.0).astype(BF16)
    last = 0 if rev else c_ - 1

    nchunks = q_ref.shape[0] // c_
    nh = DN_HEADS
    col0 = nh if rev else 0
    rows = lambda g: slice(g * c_, (g + 1) * c_)
    lanes = lambda h: slice(h * DN_HEAD_DIM, (h + 1) * DN_HEAD_DIM)

    def stack(fn):
        return jnp.stack([fn(g, h) for g in range(nchunks) for h in range(nh)])

    gb = gb_ref[...]
    gcs = [_cumsum_rows(tri, gb[rows(g), :]) for g in range(nchunks)]
    gcts = [gc.T for gc in gcs]
    q = stack(lambda g, h: q_ref[rows(g), lanes(h)])
    k = stack(lambda g, h: k_ref[rows(g), lanes(h)])
    v = stack(lambda g, h: v_ref[rows(g), lanes(h)])
    gcol = stack(lambda g, h: gcs[g][:, col0 + h:col0 + h + 1])
    grow = stack(lambda g, h: gcts[g][col0 + h:col0 + h + 1, :])
    beta = stack(lambda g, h: gb[rows(g), 2 * nh + col0 + h:2 * nh + col0 + h + 1])
    glast = gcol[:, last:last + 1, :]
    decay = jnp.where(incl, jnp.exp(jnp.where(incl, gcol - grow, 0.0)), 0.0)
    eg = jnp.exp(gcol)
    kb = k * beta
    a = jnp.where(strict, _bmm(kb, k, _BNT) * decay, 0.0)
    qk = jnp.where(incl, _bmm(q, k, _BNT) * decay, 0.0)
    ad = jnp.where(same_sub, a, 0.0)
    ao = a - ad
    p = eye - ad
    n2 = _bmm(ad, ad)
    p = p + _bmm(p, n2)
    n4 = _bmm(n2, n2)
    p = p + _bmm(p, n4)
    n8 = _bmm(n4, n4)
    dinv = p + _bmm(p, n8)
    m = _bmm(dinv, ao)
    m2 = _bmm(m, m)
    y = _bmm(dinv, jnp.concatenate([v * beta, kb * eg], axis=-1))
    z = y + _bmm(m2, y)
    x = z - _bmm(m, z)
    u = x[:, :, :DN_HEAD_DIM]
    w = x[:, :, DN_HEAD_DIM:]
    qg = q * eg
    kd = k * jnp.exp(glast - gcol)
    gl = jnp.exp(glast)
    s = s_ref[col0:col0 + nh]
    for g in (reversed(range(nchunks)) if rev else range(nchunks)):
        b = slice(g * nh, (g + 1) * nh)
        v_new = u[b] - _bmm(w[b], s)
        o = _bmm(qg[b], s) + _bmm(qk[b], v_new)
        s = s * gl[b] + _bmm(kd[b], v_new, _BTN)
        for h in range(nh):
            o_ref[rows(g), lanes(h)] = o[h]
    s_ref[col0:col0 + nh] = s


def _cumsum_rows(tri_bf16, g):
    g2, g1, g0 = _split3(g)
    d = functools.partial(lax.dot_general, dimension_numbers=_NN, preferred_element_type=F32)
    return d(tri_bf16, g2) + (d(tri_bf16, g1) + d(tri_bf16, g0))


def _dn_kernel(qf, kf, vf, gf, qb, kb, vb, gbb, *rest, n_cast):
    cast_in = rest[:n_cast]
    of_ref, ob_ref = rest[n_cast:n_cast + 2]
    cast_out = rest[n_cast + 2:2 * n_cast + 2]
    s_ref = rest[-1]

    @pl.when(pl.program_id(0) == 0)
    def _():
        s_ref[...] = jnp.zeros(s_ref.shape, F32)

    _dn_chunk(False, qf, kf, vf, gf, of_ref, s_ref)
    _dn_chunk(True, qb, kb, vb, gbb, ob_ref, s_ref)
    for src, dst in zip(cast_in, cast_out):
        dst[...] = src[...].astype(BF16)


def _delta_net(q, k, v, gb, ctx_len, to_bf16=()):
    t = q.shape[0]
    rows = DN_STEP_CHUNKS * DN_CHUNK
    assert ctx_len == rows and t % rows == 0
    nstep = t // rows
    cast_specs = []
    for m in to_bf16:
        rb = -(-m.shape[0] // nstep)
        rb = -(-rb // 16) * 16
        nb = -(-m.shape[0] // rb)
        cast_specs.append(pl.BlockSpec((rb, m.shape[1]), lambda s, nb=nb: (jnp.minimum(s, nb - 1), 0)))

    def fwd(s):
        return (s, 0)

    def bwd(s):
        return (jnp.where(s == 0, 0, nstep - s), 0)

    wide = lambda im: pl.BlockSpec((rows, DN_DIM), im)
    narrow = lambda im: pl.BlockSpec((rows, LANE), im)
    return pl.pallas_call(
        functools.partial(_dn_kernel, n_cast=len(to_bf16)),
        grid=(nstep,),
        in_specs=[wide(fwd), wide(fwd), wide(fwd), narrow(fwd), wide(bwd), wide(bwd), wide(bwd), narrow(bwd)]
        + cast_specs,
        out_specs=[wide(fwd), wide(bwd)] + cast_specs,
        out_shape=[jax.ShapeDtypeStruct((t, DN_DIM), F32)] * 2
        + [jax.ShapeDtypeStruct(m.shape, BF16) for m in to_bf16],
        scratch_shapes=[pltpu.VMEM((2 * DN_HEADS, DN_HEAD_DIM, DN_HEAD_DIM), F32)],
        compiler_params=_params(("arbitrary",), VMEM_LIMIT),
        name="delta_net",
    )(q, k, v, gb, q, k, v, gb, *to_bf16)


def _rope(x, cos, sin):
    w = x.shape[1]
    lane = lax.broadcasted_iota(jnp.int32, x.shape, 1)
    first_half = (lane & (AXIS_DIM - 1)) < (AXIS_DIM // 2)
    swapped = jnp.where(first_half, pltpu.roll(x, w - AXIS_DIM // 2, 1), pltpu.roll(x, AXIS_DIM // 2, 1))
    return x * cos + swapped * sin


LOG2E = 1.4426950408889634


def _softmax_av(s, sink, vals):
    m = jnp.maximum(jnp.max(s, axis=-1, keepdims=True), sink)
    p = jnp.exp2(s - m)
    denom = jnp.sum(p, axis=-1, keepdims=True) + jnp.exp2(sink - m)
    return _mm(p, vals) / denom


def _attend(q, keys, vals, band, sink_all, o_ref, row0=0):
    b = q.shape[0]
    for kvh in range(ATT_KV_HEADS):
        kl = slice(kvh * HEAD_DIM, (kvh + 1) * HEAD_DIM)
        heads = range(kvh * ATT_GROUP, (kvh + 1) * ATT_GROUP)
        qs = jnp.concatenate([q[:, h * HEAD_DIM:(h + 1) * HEAD_DIM] for h in heads], axis=0)
        sink = jnp.concatenate([jnp.broadcast_to(sink_all[0:1, h:h + 1], (b, 1)) for h in heads], axis=0)
        s = _mm(qs, keys[:, kl], _NT)
        if band is not None:
            kb = ATT_BLOCK
            s = jnp.concatenate([jnp.where(band[0], s[:, 0:kb], NEG), s[:, kb:2 * kb],
                                 jnp.where(band[1], s[:, 2 * kb:3 * kb], NEG), s[:, 3 * kb:]], axis=1)
        o = _softmax_av(s, sink * LOG2E, vals[:, kl])
        for g, h in enumerate(heads):
            o_ref[row0:row0 + b, h * HEAD_DIM:(h + 1) * HEAD_DIM] = o[g * b:(g + 1) * b, :]


def _band_valid(first, last):
    b = ATT_BLOCK
    c = lax.broadcasted_iota(jnp.int32, (1, b), 1)
    r = lax.broadcasted_iota(jnp.int32, (ATT_GROUP * b, 1), 0) & (b - 1)
    prev_ok = jnp.where(first, -1, c) >= r
    next_ok = jnp.where(last, b, c) <= r
    return prev_ok, next_ok


def _rope_block(rowtab_ref, coltab_ref, blk):
    rt = rowtab_ref[blk]
    ct = coltab_ref[...]
    reps = ATT_BLOCK // GRID_W
    rows = jnp.concatenate([jnp.broadcast_to(rt[g:g + 1, :], (GRID_W, LANE)) for g in range(reps)], axis=0)
    return rows + jnp.concatenate([ct] * reps, axis=0)


def _attn_kernel(q_ref, kp_ref, kc_ref, kn_ref, kctx_ref, cosr_ref, sinr_ref, cosc_ref, sinc_ref, sink_ref, o_ref,
                 *, nb, with_ctx):
    j = pl.program_id(0)
    b = ATT_BLOCK
    scale = HEAD_DIM ** -0.5 * LOG2E

    @pl.when(j == 0)
    def _():
        if with_ctx:
            kvx = kctx_ref[...]
            _attend(q_ref[...] * scale, kvx[:, :ATT_KV_DIM], kvx[:, ATT_KV_DIM:], None, sink_ref[...], o_ref)
        else:
            o_ref[...] = jnp.zeros(o_ref.shape, F32)

    @pl.when(j > 0)
    def _():
        b0 = 2 * (j - 1)
        blocks = (jnp.maximum(b0 - 1, 0), b0, b0 + 1, jnp.minimum(b0 + 2, nb - 1))
        cos = [_rope_block(cosr_ref, cosc_ref, blk) for blk in blocks]
        sin = [_rope_block(sinr_ref, sinc_ref, blk) for blk in blocks]
        kvc = kc_ref[...]
        kv = (kp_ref[...], kvc[0:b, :], kvc[b:2 * b, :], kn_ref[...])
        kvx = kctx_ref[...]
        keys = [_rope(t[:, :ATT_KV_DIM], c_, s_) for t, c_, s_ in zip(kv, cos, sin)]
        q_all = q_ref[...]
        sink_all = sink_ref[...]
        for sub in range(2):
            q = q_all[sub * b:(sub + 1) * b, :]
            q = jnp.concatenate([_rope(q[:, l * LANE:(l + 1) * LANE], cos[1 + sub], sin[1 + sub])
                                 for l in range(ATT_DIM // LANE)], axis=1)
            kcat = jnp.concatenate(keys[sub:sub + 3] + [kvx[:, :ATT_KV_DIM]], axis=0)
            vcat = jnp.concatenate([t[:, ATT_KV_DIM:] for t in kv[sub:sub + 3]] + [kvx[:, ATT_KV_DIM:]], axis=0)
            band = _band_valid(first=(b0 + sub == 0), last=(b0 + sub == nb - 1))
            _attend(q * scale, kcat, vcat, band, sink_all, o_ref, row0=sub * b)


def _attention(pq, pkv, rope, sink, ctx_len, with_ctx):
    t = pq.shape[0]
    n = t - ctx_len
    nb = n // ATT_BLOCK
    step = 2 * ATT_BLOCK
    assert ctx_len == step and nb % 2 == 0
    off = ctx_len // ATT_BLOCK
    half = lambda im: pl.BlockSpec((ATT_BLOCK, 2 * ATT_KV_DIM), im)
    lat = lambda j: 2 * jnp.maximum(j - 1, 0)
    prv = lambda j: (jnp.maximum(lat(j) - 1, 0) + off, 0)
    nxt = lambda j: (jnp.minimum(lat(j) + 2, nb - 1) + off, 0)
    cur = lambda j: (j, 0)
    whole = lambda a: pl.BlockSpec(a.shape, lambda j: (0,) * a.ndim)
    return pl.pallas_call(
        functools.partial(_attn_kernel, nb=nb, with_ctx=with_ctx),
        grid=(1 + nb // 2,),
        in_specs=[
            pl.BlockSpec((step, ATT_DIM), cur),
            half(prv), pl.BlockSpec((step, 2 * ATT_KV_DIM), cur), half(nxt),
            pl.BlockSpec((ctx_len, 2 * ATT_KV_DIM), lambda j: (0, 0)),
        ] + [whole(a) for a in rope] + [pl.BlockSpec((8, LANE), lambda j: (0, 0))],
        out_specs=pl.BlockSpec((step, ATT_DIM), cur),
        out_shape=jax.ShapeDtypeStruct((t, ATT_DIM), F32),
        compiler_params=_params(("arbitrary",), VMEM_LIMIT),
        name="attention",
    )(pq, pkv, pkv, pkv, pkv, *rope, sink)


def _mixfin_kernel(ctx_ref, h_ref, mod_ref, pconv_ref, prev_ref, next_ref, cw_ref, of_ref, ob_ref, z_ref, ng_ref,
                   yc_ref, wout_ref, bd_ref, *rest, nblk, blk0, with_router):
    if with_router:
        wr1_ref, wr0_ref, x_ref, hx_ref, lg_ref = rest
    else:
        wg_ref, wu_ref, wd_ref, x_ref = rest
    i = pl.program_id(0) + blk0
    row = _mod_row(mod_ref, i == 0)
    pc = pconv_ref[...]
    u = pc[:, CONV_CH:2 * CONV_CH] * pc[:, 2 * CONV_CH:]
    prow, nrow = _halo_rows(prev_ref, next_ref, i, nblk)
    prow = prow[:, CONV_CH:2 * CONV_CH] * prow[:, 2 * CONV_CH:]
    nrow = nrow[:, CONV_CH:2 * CONV_CH] * nrow[:, 2 * CONV_CH:]
    up, un = _shift_rows(u, prow, nrow)
    cw = cw_ref[...]
    ya = pc[:, :CONV_CH] * (up * cw[0:1, :] + u * cw[1:2, :] + un * cw[2:3, :])
    o = of_ref[...] + ob_ref[...]
    ms = _group_sum(o * o, bd_ref[...]) * (1.0 / DN_HEAD_DIM)
    z = z_ref[...]
    yb = o * lax.rsqrt(ms + EPS) * ng_ref[...] * (z * _sigmoid(z))
    mix = jnp.concatenate([ya, yb, yc_ref[...]], axis=1)
    x = jnp.where(i == 0, ctx_ref[...], h_ref[...]) + row[:, 2 * D:3 * D] * _mm(mix, wout_ref[...])
    hx = _rmsnorm_rows(x) * (1.0 + row[:, 4 * D:5 * D]) + row[:, 3 * D:4 * D]
    if not with_router:
        hb = hx.astype(BF16)
        acc = jnp.zeros((hb.shape[0], D), F32)
        for f in range(0, D_FF, FF_CHUNK):
            fe = min(f + FF_CHUNK, D_FF)
            g = _mm(hb, wg_ref[:, f:fe])
            u_ = _mm(hb, wu_ref[:, f:fe])
            acc = acc + _mm(g * _sigmoid(g) * u_, wd_ref[f:fe, :])
        x_ref[...] = x + row[:, 5 * D:6 * D] * acc
    else:
        x_ref[...] = x
        hx_ref[...] = hx
        h1, h0 = _split2(hx)
        d = functools.partial(lax.dot_general, dimension_numbers=_NN, preferred_element_type=F32)
        lg = d(h1, wr1_ref[...]) + (d(h0, wr1_ref[...]) + d(h1, wr0_ref[...]))
        lane = lax.broadcasted_iota(jnp.int32, lg.shape, 1)
        lanef = lane.astype(F32)
        lg = jnp.where(lane < N_EXPERTS, lg, -jnp.inf)
        m1 = jnp.max(lg, axis=-1, keepdims=True)
        i1 = jnp.min(jnp.where(lg == m1, lanef, float(LANE)), axis=-1, keepdims=True)
        rest = jnp.where(lanef == i1, -jnp.inf, lg)
        m2 = jnp.max(rest, axis=-1, keepdims=True)
        i2 = jnp.min(jnp.where(rest == m2, lanef, float(LANE)), axis=-1, keepdims=True)
        e2 = jnp.exp(m2 - m1)
        g1 = 1.0 / (1.0 + e2)
        lg_ref[...] = jnp.where(lane == 0, g1, jnp.where(lane == 1, e2 * g1, jnp.where(lane == 2, i1, i2)))


def _mixer_finish(ctx_src, lat_src, lat_blk0, mod, pconv, conv_w, o_f, o_b, pz, norm_g, yc, w_out, bd, ctx_len,
                  with_ctx, router=None, ffn=None):
    assert (router is None) != (ffn is None)
    t = pconv.shape[0]
    nblk = t // TM
    blk0 = 0 if with_ctx else ctx_len // TM
    rows = t - blk0 * TM
    r8 = TM // 8
    w = pconv.shape[1]
    cur = lambda i: (i + blk0, 0)
    out_cur = lambda i: (i, 0)
    const = lambda i: (0, 0)
    ng = jnp.tile(norm_g.reshape(1, DN_HEAD_DIM), (1, DN_HEADS))
    in_specs = [
        pl.BlockSpec((TM, D), const),
        pl.BlockSpec((TM, D), lambda i: (jnp.maximum(i + blk0 - 1, 0) + lat_blk0, 0)),
        pl.BlockSpec((8, N_MOD * D), const),
        pl.BlockSpec((TM, w), cur),
        pl.BlockSpec((8, w), lambda i: (jnp.maximum((i + blk0) * r8 - 1, 0), 0)),
        pl.BlockSpec((8, w), lambda i: (jnp.minimum((i + blk0 + 1) * r8, t // 8 - 1), 0)),
        pl.BlockSpec((3, CONV_CH), const),
        pl.BlockSpec((TM, DN_DIM), cur),
        pl.BlockSpec((TM, DN_DIM), cur),
        pl.BlockSpec((TM, DN_DIM), cur),
        pl.BlockSpec((1, DN_DIM), const),
        pl.BlockSpec((TM, ATT_DIM), cur),
        pl.BlockSpec((MIX_DIM, D), const),
        pl.BlockSpec(bd.shape, const),
    ]
    args = [ctx_src, lat_src, mod, pconv, pconv, pconv, conv_w, o_f, o_b, pz, ng, yc, w_out, bd]
    out_specs = [pl.BlockSpec((TM, D), out_cur)]
    out_shape = [jax.ShapeDtypeStruct((rows, D), F32)]
    if router is not None:
        in_specs += [pl.BlockSpec((D, LANE), const)] * 2
        args += list(router)
        out_specs += [pl.BlockSpec((TM, D), out_cur), pl.BlockSpec((TM, LANE), out_cur)]
        out_shape += [jax.ShapeDtypeStruct((rows, D), F32), jax.ShapeDtypeStruct((rows, LANE), F32)]
    else:
        once = dict(pipeline_mode=pl.Buffered(1))
        in_specs += [pl.BlockSpec(w_.shape, const, **once) for w_ in ffn]
        args += list(ffn)
    return pl.pallas_call(
        functools.partial(_mixfin_kernel, nblk=nblk, blk0=blk0, with_router=router is not None),
        grid=(rows // TM,),
        in_specs=in_specs,
        out_specs=out_specs,
        out_shape=out_shape,
        compiler_params=_params(("arbitrary",), VMEM_LIMIT),
        name="mixer_finish",
    )(*args)


def _moe_kernel(be_ref, nu_ref, xs_ref, wg_ref, wu_ref, wd_ref, *rest, boff):
    y_ref, acc_ref = rest[-2:]
    b = pl.program_id(0) + boff
    f = pl.program_id(1)

    @pl.when(b < nu_ref[0])
    def _():
        xs = xs_ref[...].astype(BF16)
        part = jnp.zeros((MOE_TM, D), F32)
        for c in range(0, MOE_TF, MOE_SUB):
            g = _mm(xs, wg_ref[0, :, c:c + MOE_SUB])
            u = _mm(xs, wu_ref[0, :, c:c + MOE_SUB])
            part = part + _mm(g * _sigmoid(g) * u, wd_ref[0, c:c + MOE_SUB, :])

        @pl.when(f == 0)
        def _():
            acc_ref[...] = part

        @pl.when(f != 0)
        def _():
            acc_ref[...] += part

        @pl.when(f == pl.num_programs(1) - 1)
        def _():
            y_ref[...] = acc_ref[...]

    @pl.when(b >= nu_ref[0])
    def _():
        y_ref[...] = jnp.zeros(y_ref.shape, F32)


def _moe_experts(xs_parts, blk_e, n_used, wg, wu, wd):
    cap = sum(xs.shape[0] for xs in xs_parts)
    nf = D_FF_EXPERT // MOE_TF
    y = None
    boff = 0
    for xs in xs_parts:
        nblk_p = xs.shape[0] // MOE_TM

        def fidx(b, f, nu):
            return jnp.where(b < nu[0], f, nf - 1)

        g = lambda b, boff=boff: b + boff
        in_specs = [
            pl.BlockSpec((MOE_TM, D), lambda b, f, be, nu: (b, 0)),
            pl.BlockSpec((1, D, MOE_TF), lambda b, f, be, nu, g=g: (be[g(b)], 0, fidx(g(b), f, nu))),
            pl.BlockSpec((1, D, MOE_TF), lambda b, f, be, nu, g=g: (be[g(b)], 0, fidx(g(b), f, nu))),
            pl.BlockSpec((1, MOE_TF, D), lambda b, f, be, nu, g=g: (be[g(b)], fidx(g(b), f, nu), 0)),
        ]
        args = [blk_e, n_used, xs, wg, wu, wd]
        aliases = {}
        if y is not None:
            in_specs.append(pl.BlockSpec(memory_space=pl.ANY))
            aliases = {len(args): 0}
            args.append(y)
        grid_spec = pltpu.PrefetchScalarGridSpec(
            num_scalar_prefetch=2,
            grid=(nblk_p, nf),
            in_specs=in_specs,
            out_specs=pl.BlockSpec((MOE_TM, D), lambda b, f, be, nu, g=g: (g(b), 0)),
            scratch_shapes=[pltpu.VMEM((MOE_TM, D), F32)],
        )
        y = pl.pallas_call(
            functools.partial(_moe_kernel, boff=boff),
            grid_spec=grid_spec,
            out_shape=jax.ShapeDtypeStruct((cap, D), F32),
            input_output_aliases=aliases,
            compiler_params=_params(("arbitrary", "arbitrary"), VMEM_LIMIT),
            name="moe_experts",
        )(*args)
        boff += nblk_p
    return y


def _moe_route(top_e):
    n = top_e.shape[0]
    a = n * TOP_K
    flat_e = top_e.reshape(a)
    onehot = (flat_e[:, None] == jnp.arange(N_EXPERTS, dtype=flat_e.dtype)[None, :]).astype(jnp.int32)
    counts = jnp.sum(onehot, axis=0)
    padded = (counts + MOE_TM - 1) // MOE_TM * MOE_TM
    pad_ends = jnp.cumsum(padded)
    pad_starts = pad_ends - padded
    dest = jnp.sum(onehot * (jnp.cumsum(onehot, axis=0) - onehot + pad_starts[None, :]), axis=1)
    cap = a + N_EXPERTS * MOE_TM
    nblk = cap // MOE_TM
    row_tok = (jnp.arange(cap, dtype=jnp.int32) % n).at[dest].set(
        jnp.arange(a, dtype=jnp.int32) // TOP_K, unique_indices=True, mode="promise_in_bounds")
    blk_start = jnp.arange(nblk, dtype=jnp.int32) * MOE_TM
    blk_e = jnp.minimum(jnp.sum((pad_ends[None, :] <= blk_start[:, None]).astype(jnp.int32), axis=1),
                        N_EXPERTS - 1)
    n_used = (pad_ends[-1] // MOE_TM).astype(jnp.int32).reshape(1)
    last_e = blk_e[jnp.maximum(n_used[0] - 1, 0)]
    blk_e = jnp.where(jnp.arange(nblk) < n_used[0], blk_e, last_e)
    return dest.reshape(n, TOP_K), row_tok, blk_e, n_used


def _final_kernel(x_ref, y0_ref, y1_ref, gt_ref, mod_ref, fg_ref, *rest):
    o_ref = rest[-1]
    mod = mod_ref[...]
    gt = gt_ref[...]
    f = gt[:, 0:1] * y0_ref[...] + gt[:, 1:2] * y1_ref[...]
    x = x_ref[...] + mod[0:1, 5 * D:6 * D] * f
    o_ref[...] = _rmsnorm_rows(x) * fg_ref[...]


def _moe_combine_final(x, y_parts, gt, mod, final_g):
    n = x.shape[0]
    const = lambda i: (0, 0)
    out = None
    boff = 0
    for y0, y1 in y_parts:
        nblk_p = y0.shape[0] // TM
        glob = lambda i, off=boff: (i + off, 0)
        loc = lambda i: (i, 0)
        in_specs = [pl.BlockSpec((TM, D), glob), pl.BlockSpec((TM, D), loc), pl.BlockSpec((TM, D), loc),
                    pl.BlockSpec((TM, LANE), glob), pl.BlockSpec((8, N_MOD * D), const),
                    pl.BlockSpec((1, D), const)]
        args = [x, y0, y1, gt, mod, final_g.reshape(1, D)]
        aliases = {}
        if out is not None:
            in_specs.append(pl.BlockSpec(memory_space=pl.ANY))
            aliases = {len(args): 0}
            args.append(out)
        out = pl.pallas_call(
            _final_kernel,
            grid=(nblk_p,),
            in_specs=in_specs,
            out_specs=pl.BlockSpec((TM, D), glob),
            out_shape=jax.ShapeDtypeStruct((n, D), F32),
            input_output_aliases=aliases,
            compiler_params=_params(("arbitrary",), VMEM_LIMIT),
            name="moe_combine_final",
        )(*args)
        boff += nblk_p
    return out


def _rope_tables(n):
    lane = jnp.arange(LANE, dtype=jnp.int32) % HEAD_DIM
    inv = ROPE_BASE ** (-jnp.arange(0, AXIS_DIM, 2, dtype=F32) / AXIS_DIM)
    freq = inv[lane % (AXIS_DIM // 2)]
    row_axis = (lane // AXIS_DIM) == 0
    sign = jnp.where((lane % AXIS_DIM) < AXIS_DIM // 2, -1.0, 1.0)
    reps = ATT_BLOCK // GRID_W

    def tables(count, on_axis):
        ang = jnp.arange(count, dtype=F32)[:, None] * freq[None, :]
        return jnp.where(on_axis, jnp.cos(ang), 0.0), jnp.where(on_axis, jnp.sin(ang) * sign, 0.0)

    by_block = lambda t: jnp.pad(t.reshape(-1, reps, LANE), ((0, 0), (0, 8 - reps), (0, 0)))
    cosr, sinr = tables(n // GRID_W, row_axis)
    cosc, sinc = tables(GRID_W, ~row_axis)
    return by_block(cosr), by_block(sinr), cosc, sinc


def _prep_w_in(w):
    pad = jnp.zeros((D, LANE - N_AB), w.dtype)
    return jnp.concatenate([w[:, :_C_Z], w[:, _C_A:_C_Q], pad, w[:, _C_Z:_C_A], w[:, _C_Q:_C_END]],
                           axis=1).astype(BF16)


def kernel(x, c, ctx, c_ctx, w_mod, b_mod, w_in, w_out, conv_w, dn_conv_w, dn_a_log, dn_dt_bias, dn_norm_g,
           attn_sink, ffn_w_gate, ffn_w_up, ffn_w_down, moe_router, moe_w_gate, moe_w_up, moe_w_down,
           final_norm_g):
    bsz, n, d = x.shape
    ctx_len = ctx.shape[1]
    depth = w_in.shape[0]
    assert bsz == 1 and d == D and ctx_len == TM and n % TM == 0 and n % GRID_W == 0
    rope = _rope_tables(n)
    mods = _mod_vectors(c, c_ctx, w_mod, b_mod)
    bd = _head_blockdiag(DN_DIM, DN_HEAD_DIM)
    stream = (ctx[0], x[0], 0)
    for layer in range(depth):
        last = layer == depth - 1
        mod = mods[layer]
        pconv, pz, pq, pkv, qn, kn, vv, gb = _in_proj(*stream, mod, _prep_w_in(w_in[layer]), dn_conv_w[layer],
                                                      dn_a_log[layer], dn_dt_bias[layer], bd)
        to_bf16 = []
        if layer % 2 == 0:
            to_bf16 += [w[layer // 2] for w in (ffn_w_gate, ffn_w_up, ffn_w_down)]
        if (layer + 1) % 2 == 1 and layer + 1 < depth:
            j = (layer + 1) // 2
            to_bf16 += [w[j].reshape(-1, w.shape[-1]) for w in (moe_w_gate, moe_w_up, moe_w_down)]
        o_f, o_b, *cast = _delta_net(qn, kn, vv, gb, ctx_len, tuple(to_bf16))
        if layer % 2 == 0:
            ffn_bf16, cast = cast[:3], cast[3:]
        if cast:
            moe_bf16 = [c.reshape(w.shape[1:]) for c, w in zip(cast, (moe_w_gate, moe_w_up, moe_w_down))]
        sink = jnp.zeros((8, LANE), F32).at[0, :ATT_HEADS].set(attn_sink[layer])
        yc = _attention(pq, pkv, rope, sink, ctx_len, with_ctx=not last)
        router = None
        if layer % 2 == 1:
            wr = jnp.zeros((D, LANE), F32).at[:, :N_EXPERTS].set(moe_router[layer // 2])
            wr1 = wr.astype(BF16)
            router = (wr1, (wr - wr1.astype(F32)).astype(BF16))
        outs = _mixer_finish(*stream, mod, pconv, conv_w[layer], o_f, o_b, pz, dn_norm_g[layer], yc,
                             w_out[layer].astype(BF16), bd, ctx_len, with_ctx=not last, router=router,
                             ffn=ffn_bf16 if layer % 2 == 0 else None)
        if layer % 2 == 0:
            assert not last
            h, = outs
            stream = (h, h, ctx_len // TM)
        else:
            assert last
            x1, hx, route = outs
            i = layer // 2
            dest, row_tok, blk_e, n_used = _moe_route(route[:, 2:2 + TOP_K].astype(jnp.int32))
            take = lambda rows_, idx: rows_.at[idx].get(mode="promise_in_bounds")
            parts = lambda a, k: (a[:a.shape[0] // k], a[a.shape[0] // k:])
            xs_parts = [take(hx, r) for r in parts(row_tok, MOE_FIRST_PART)]
            xs_parts, dest_parts = lax.optimization_barrier((xs_parts, parts(dest, COMBINE_FIRST_PART)))
            y = _moe_experts(xs_parts, blk_e, n_used, *moe_bf16)
            h = _moe_combine_final(x1, [(take(y, dd[:, 0]), take(y, dd[:, 1])) for dd in dest_parts],
                                   route, mod, final_norm_g)
    return h.reshape(bsz, n, d)
```

```python
import functools

import jax
import jax.numpy as jnp
from jax import lax
from jax.experimental import pallas as pl
from jax.experimental.pallas import tpu as pltpu

F32 = jnp.float32
BF16 = jnp.bfloat16

D = 1024
N_MOD = 6
EPS = 1e-6
NEG = -1e30
GRID_W = 64

CONV_CH = 256
DN_HEADS = 6
DN_HEAD_DIM = 64
DN_DIM = DN_HEADS * DN_HEAD_DIM
DN_CHUNK = 64
DN_SUB = 16
DN_STEP_CHUNKS = 4
ATT_HEADS = 6
ATT_KV_HEADS = 2
ATT_GROUP = ATT_HEADS // ATT_KV_HEADS
HEAD_DIM = 64
ATT_DIM = ATT_HEADS * HEAD_DIM
ATT_KV_DIM = ATT_KV_HEADS * HEAD_DIM
ATT_BLOCK = 128
ROPE_BASE = 10000.0
AXIS_DIM = HEAD_DIM // 2
MIX_DIM = CONV_CH + DN_DIM + ATT_DIM

D_FF = 2816
N_EXPERTS = 8
TOP_K = 2
D_FF_EXPERT = 3584

TM = 256
FF_CHUNK = 512
MOE_TM = 512
MOE_TF = 1792
MOE_SUB = 256
MOE_FIRST_PART = 4
LANE = 128
VMEM_LIMIT = 56 * 1024 * 1024

_C_QKV = 3 * CONV_CH
_C_Z = _C_QKV + 3 * DN_DIM
_C_A = _C_Z + DN_DIM
_C_Q = _C_A + 4 * DN_HEADS
_C_K = _C_Q + ATT_DIM
_C_V = _C_K + ATT_KV_DIM
_C_END = _C_V + ATT_KV_DIM
N_AB = 4 * DN_HEADS


def _params(sem=None, vmem=None):
    kw = {}
    if sem is not None:
        kw["dimension_semantics"] = sem
    if vmem is not None:
        kw["vmem_limit_bytes"] = vmem
    return pltpu.CompilerParams(**kw)


def _split2(a):
    hi = a.astype(BF16)
    lo = (a - hi.astype(F32)).astype(BF16)
    return hi, lo


def _split3(a):
    hi = a.astype(BF16)
    r = a - hi.astype(F32)
    mid = r.astype(BF16)
    lo = (r - mid.astype(F32)).astype(BF16)
    return hi, mid, lo


_NN = (((1,), (0,)), ((), ()))
_NT = (((1,), (1,)), ((), ()))
_TN = (((0,), (0,)), ((), ()))


def _mm(a, b, dims=_NN):
    return lax.dot_general(a.astype(BF16), b.astype(BF16), dims, preferred_element_type=F32)


def _mm3(a, b, dims=_NN):
    a1, a0 = _split2(a)
    b1, b0 = _split2(b)
    d = functools.partial(lax.dot_general, dimension_numbers=dims, preferred_element_type=F32)
    return d(a1, b1) + (d(a1, b0) + d(a0, b1))


_BNN = (((2,), (1,)), ((0,), (0,)))
_BNT = (((2,), (2,)), ((0,), (0,)))
_BTN = (((1,), (1,)), ((0,), (0,)))


def _bmm(a, b, dims=_BNN):
    return lax.dot_general(a.astype(BF16), b.astype(BF16), dims, preferred_element_type=F32)


def _sigmoid(x):
    return 1.0 / (1.0 + jnp.exp(-x))


def _softplus(x):
    return jnp.maximum(x, 0.0) + jnp.log1p(jnp.exp(-jnp.abs(x)))


def _mod_row(mod_ref, is_ctx):
    mod = mod_ref[...]
    return jnp.where(is_ctx, mod[1:2, :], mod[0:1, :])


def _rmsnorm_rows(x):
    return x * lax.rsqrt(jnp.mean(x * x, axis=-1, keepdims=True) + EPS)


def _shift_rows(u, prow, nrow):
    n = u.shape[0]
    rid = lax.broadcasted_iota(jnp.int32, u.shape, 0)
    up = jnp.where(rid == 0, prow, pltpu.roll(u, 1, 0))
    un = jnp.where(rid == n - 1, nrow, pltpu.roll(u, n - 1, 0))
    return up, un


def _same_group(shape, group):
    sh = group.bit_length() - 1
    assert 1 << sh == group
    return (lax.broadcasted_iota(jnp.int32, shape, 0) >> sh) == (lax.broadcasted_iota(jnp.int32, shape, 1) >> sh)


def _head_blockdiag(n, group):
    g = jnp.arange(n, dtype=jnp.int32) // group
    return (g[:, None] == g[None, :]).astype(BF16)


def _group_sum(t, bd):
    return _mm(t, bd)


MOD_TN = 1536


def _mod_kernel(s_ref, w_ref, b_ref, o_ref):
    s = s_ref[...]
    s = s * _sigmoid(s)
    o_ref[0] = _mm3(s, w_ref[0]) + b_ref[0]


def _mod_vectors(c, c_ctx, w_mod, b_mod):
    depth = w_mod.shape[0]
    s = jnp.zeros((8, D), F32).at[0].set(c[0]).at[1].set(c_ctx)
    return pl.pallas_call(
        _mod_kernel,
        grid=(depth, N_MOD * D // MOD_TN),
        in_specs=[
            pl.BlockSpec((8, D), lambda l, j: (0, 0)),
            pl.BlockSpec((1, D, MOD_TN), lambda l, j: (l, 0, j)),
            pl.BlockSpec((1, 1, MOD_TN), lambda l, j: (l, 0, j)),
        ],
        out_specs=pl.BlockSpec((1, 8, MOD_TN), lambda l, j: (l, 0, j)),
        out_shape=jax.ShapeDtypeStruct((depth, 8, N_MOD * D), F32),
        compiler_params=_params(("arbitrary", "arbitrary"), VMEM_LIMIT),
        name="mod_vectors",
    )(s, w_mod, b_mod.reshape(depth, 1, N_MOD * D))


def _halo_valid(i, nblk):
    return jnp.logical_and(i != 0, i != 1), jnp.logical_and(i != 0, i != nblk - 1)


def _halo_rows(prev_ref, next_ref, i, nblk):
    pvalid, nvalid = _halo_valid(i, nblk)
    prow = jnp.where(pvalid, prev_ref[7:8, :], 0.0)
    nrow = jnp.where(nvalid, next_ref[0:1, :], 0.0)
    return prow, nrow


def _in_kernel(ctx_ref, h_ref, hprev_ref, hnext_ref, mod_ref, w_ref, cw_ref, alog_ref, dtb_ref, bd_ref,
               pconv_ref, pz_ref, pq_ref, pkv_ref, q_ref, k_ref, v_ref, gb_ref, *, nblk):
    i = pl.program_id(0)
    row = _mod_row(mod_ref, i == 0)
    norm_mod = lambda x: _rmsnorm_rows(x) * (1.0 + row[:, D:2 * D]) + row[:, 0:D]
    h1 = norm_mod(jnp.where(i == 0, ctx_ref[...], h_ref[...])).astype(BF16)
    halo = norm_mod(jnp.concatenate([hprev_ref[...], hnext_ref[...]], axis=0)).astype(BF16)
    d = functools.partial(lax.dot_general, dimension_numbers=_NN, preferred_element_type=F32)
    tm = h1.shape[0]
    c0 = 3 * CONV_CH
    c1 = c0 + 3 * DN_DIM
    c2 = c1 + LANE
    c3 = c2 + DN_DIM + ATT_DIM
    pconv_ref[...] = d(h1, w_ref[:, 0:c0])
    zq = d(h1, w_ref[:, c2:c3])
    pz_ref[...] = zq[:, 0:DN_DIM]
    pq_ref[...] = zq[:, DN_DIM:]
    pkv_ref[...] = d(h1, w_ref[:, c3:c3 + 2 * ATT_KV_DIM])
    qkv_ab = d(jnp.concatenate([h1, halo], axis=0), w_ref[:, c0:c2])
    qkv = qkv_ab[:, 0:3 * DN_DIM]
    ab = qkv_ab[0:tm, 3 * DN_DIM:]

    u = qkv[0:tm, :]
    pvalid, nvalid = _halo_valid(i, nblk)
    prow = jnp.where(pvalid, qkv[tm + 7:tm + 8, :], 0.0)
    nrow = jnp.where(nvalid, qkv[tm + 8:tm + 9, :], 0.0)
    up, un = _shift_rows(u, prow, nrow)
    cw = cw_ref[...]
    y = up * cw[0:1, :] + u * cw[1:2, :] + un * cw[2:3, :]
    y = y * _sigmoid(y)
    q = y[:, 0:DN_DIM]
    k = y[:, DN_DIM:2 * DN_DIM]
    bd = bd_ref[...]
    q_ref[...] = q * lax.rsqrt(_group_sum(q * q, bd) + 1e-6) * (DN_HEAD_DIM ** -0.5)
    k_ref[...] = k * lax.rsqrt(_group_sum(k * k, bd) + 1e-6)
    v_ref[...] = y[:, 2 * DN_DIM:3 * DN_DIM]
    g = -jnp.exp(alog_ref[...]) * _softplus(ab + dtb_ref[...])
    lane = lax.broadcasted_iota(jnp.int32, ab.shape, 1)
    gb_ref[...] = jnp.where(lane < 2 * DN_HEADS, g, _sigmoid(ab))


def _in_proj(ctx_src, lat_src, lat_blk0, mod, w_main, dn_conv_w, a_log, dt_bias, bd):
    r8 = TM // 8
    last8 = lat_src.shape[0] // 8 - 1
    nblk = 1 + lat_src.shape[0] // TM - lat_blk0
    t = nblk * TM
    alog = jnp.zeros((1, LANE), F32).at[0, :2 * DN_HEADS].set(a_log.reshape(-1))
    dtb = jnp.zeros((1, LANE), F32).at[0, :2 * DN_HEADS].set(dt_bias.reshape(-1))
    widths = (3 * CONV_CH, DN_DIM, ATT_DIM, 2 * ATT_KV_DIM, DN_DIM, DN_DIM, DN_DIM, LANE)
    const = lambda i: (0, 0)
    lat = lambda i: jnp.maximum(i - 1, 0) + lat_blk0
    return pl.pallas_call(
        functools.partial(_in_kernel, nblk=nblk),
        grid=(nblk,),
        in_specs=[
            pl.BlockSpec((TM, D), const),
            pl.BlockSpec((TM, D), lambda i: (lat(i), 0)),
            pl.BlockSpec((8, D), lambda i: (jnp.maximum(lat(i) * r8 - 1, 0), 0)),
            pl.BlockSpec((8, D), lambda i: (jnp.minimum((lat(i) + 1) * r8, last8), 0)),
            pl.BlockSpec((8, N_MOD * D), const),
            pl.BlockSpec(w_main.shape, const),
            pl.BlockSpec((3, 3 * DN_DIM), const),
            pl.BlockSpec((1, LANE), const),
            pl.BlockSpec((1, LANE), const),
            pl.BlockSpec(bd.shape, const),
        ],
        out_specs=[pl.BlockSpec((TM, w), lambda i: (i, 0)) for w in widths],
        out_shape=[jax.ShapeDtypeStruct((t, w), F32) for w in widths],
        compiler_params=_params(("arbitrary",), VMEM_LIMIT),
        name="in_proj",
    )(ctx_src, lat_src, lat_src, lat_src, mod, w_main, dn_conv_w, alog, dtb, bd)


def _dn_block(fwd_refs, bwd_refs, of_ref, ob_ref, s_ref):
    c_ = DN_CHUNK
    nh = DN_HEADS
    nchunks = fwd_refs[0].shape[0] // c_
    nb = nchunks * nh
    rows = lambda g: slice(g * c_, (g + 1) * c_)
    lanes = lambda h: slice(h * DN_HEAD_DIM, (h + 1) * DN_HEAD_DIM)
    ri = lax.broadcasted_iota(jnp.int32, (c_, c_), 0)
    ci = lax.broadcasted_iota(jnp.int32, (c_, c_), 1)
    same_sub = _same_group((c_, c_), DN_SUB)
    eye = jnp.where(ri == ci, 1.0, 0.0)
    shape3 = (2 * nb, c_, c_)
    delta = lax.broadcasted_iota(jnp.int32, shape3, 1) - lax.broadcasted_iota(jnp.int32, shape3, 2)
    delta = jnp.where(lax.broadcasted_iota(jnp.int32, shape3, 0) >= nb, -delta, delta)
    incl = delta >= 0
    strict = delta > 0

    def stack(fn):
        return jnp.stack([fn(d, refs, g, h) for d, refs in enumerate((fwd_refs, bwd_refs))
                          for g in range(nchunks) for h in range(nh)])

    gbs = [refs[3][...] for refs in (fwd_refs, bwd_refs)]
    tris = [jnp.where(ri >= ci, 1.0, 0.0).astype(BF16), jnp.where(ri <= ci, 1.0, 0.0).astype(BF16)]
    gcs = [[_cumsum_rows(tris[d], gbs[d][rows(g), :]) for g in range(nchunks)] for d in range(2)]
    gcts = [[gc.T for gc in gcs[d]] for d in range(2)]
    q = stack(lambda d, refs, g, h: refs[0][rows(g), lanes(h)])
    k = stack(lambda d, refs, g, h: refs[1][rows(g), lanes(h)])
    v = stack(lambda d, refs, g, h: refs[2][rows(g), lanes(h)])
    col = lambda d, h: d * nh + h
    gcol = stack(lambda d, refs, g, h: gcs[d][g][:, col(d, h):col(d, h) + 1])
    grow = stack(lambda d, refs, g, h: gcts[d][g][col(d, h):col(d, h) + 1, :])
    beta = stack(lambda d, refs, g, h: gbs[d][rows(g), 2 * nh + col(d, h):2 * nh + col(d, h) + 1])
    glast = jnp.concatenate([gcol[:nb, c_ - 1:c_, :], gcol[nb:, 0:1, :]], axis=0)
    decay = jnp.where(incl, jnp.exp(jnp.where(incl, gcol - grow, 0.0)), 0.0)
    eg = jnp.exp(gcol)
    kb = k * beta
    a = jnp.where(strict, _bmm(kb, k, _BNT) * decay, 0.0)
    qk = jnp.where(incl, _bmm(q, k, _BNT) * decay, 0.0)
    ad = jnp.where(same_sub, a, 0.0)
    ao = a - ad
    p = eye - ad
    n2 = _bmm(ad, ad)
    p = p + _bmm(p, n2)
    n4 = _bmm(n2, n2)
    p = p + _bmm(p, n4)
    n8 = _bmm(n4, n4)
    dinv = p + _bmm(p, n8)
    m = _bmm(dinv, ao)
    m2 = _bmm(m, m)
    y = _bmm(dinv, jnp.concatenate([v * beta, kb * eg], axis=-1))
    z = y + _bmm(m2, y)
    x = z - _bmm(m, z)
    u = x[:, :, :DN_HEAD_DIM]
    w = x[:, :, DN_HEAD_DIM:]
    qg = q * eg
    kd = k * jnp.exp(glast - gcol)
    gl = jnp.exp(glast)
    s = s_ref[...]
    for t in range(nchunks):
        gf, gr = t, nchunks - 1 - t
        step = lambda a: jnp.concatenate([a[gf * nh:(gf + 1) * nh], a[nb + gr * nh:nb + (gr + 1) * nh]], axis=0)
        v_new = step(u) - _bmm(step(w), s)
        o = _bmm(step(qg), s) + _bmm(step(qk), v_new)
        s = s * step(gl) + _bmm(step(kd), v_new, _BTN)
        for h in range(nh):
            of_ref[rows(gf), lanes(h)] = o[h]
            ob_ref[rows(gr), lanes(h)] = o[nh + h]
    s_ref[...] = s


def _cumsum_rows(tri_bf16, g):
    g2, g1, g0 = _split3(g)
    d = functools.partial(lax.dot_general, dimension_numbers=_NN, preferred_element_type=F32)
    return d(tri_bf16, g2) + (d(tri_bf16, g1) + d(tri_bf16, g0))


def _dn_kernel(qf, kf, vf, gf, qb, kb, vb, gbb, *rest, n_cast):
    cast_in = rest[:n_cast]
    of_ref, ob_ref = rest[n_cast:n_cast + 2]
    cast_out = rest[n_cast + 2:2 * n_cast + 2]
    s_ref = rest[-1]

    @pl.when(pl.program_id(0) == 0)
    def _():
        s_ref[...] = jnp.zeros(s_ref.shape, F32)

    _dn_block((qf, kf, vf, gf), (qb, kb, vb, gbb), of_ref, ob_ref, s_ref)
    for src, dst in zip(cast_in, cast_out):
        dst[...] = src[...].astype(BF16)


def _delta_net(q, k, v, gb, ctx_len, to_bf16=()):
    t = q.shape[0]
    rows = DN_STEP_CHUNKS * DN_CHUNK
    assert ctx_len == rows and t % rows == 0
    nstep = t // rows
    cast_specs = []
    for m in to_bf16:
        rb = -(-m.shape[0] // nstep)
        rb = -(-rb // 16) * 16
        nb = -(-m.shape[0] // rb)
        cast_specs.append(pl.BlockSpec((rb, m.shape[1]), lambda s, nb=nb: (jnp.minimum(s, nb - 1), 0)))

    def fwd(s):
        return (s, 0)

    def bwd(s):
        return (jnp.where(s == 0, 0, nstep - s), 0)

    wide = lambda im: pl.BlockSpec((rows, DN_DIM), im)
    narrow = lambda im: pl.BlockSpec((rows, LANE), im)
    return pl.pallas_call(
        functools.partial(_dn_kernel, n_cast=len(to_bf16)),
        grid=(nstep,),
        in_specs=[wide(fwd), wide(fwd), wide(fwd), narrow(fwd), wide(bwd), wide(bwd), wide(bwd), narrow(bwd)]
        + cast_specs,
        out_specs=[wide(fwd), wide(bwd)] + cast_specs,
        out_shape=[jax.ShapeDtypeStruct((t, DN_DIM), F32)] * 2
        + [jax.ShapeDtypeStruct(m.shape, BF16) for m in to_bf16],
        scratch_shapes=[pltpu.VMEM((2 * DN_HEADS, DN_HEAD_DIM, DN_HEAD_DIM), F32)],
        compiler_params=_params(("arbitrary",), VMEM_LIMIT),
        name="delta_net",
    )(q, k, v, gb, q, k, v, gb, *to_bf16)


def _rope(x, cos, sin):
    w = x.shape[1]
    lane = lax.broadcasted_iota(jnp.int32, x.shape, 1)
    first_half = (lane & (AXIS_DIM - 1)) < (AXIS_DIM // 2)
    swapped = jnp.where(first_half, pltpu.roll(x, w - AXIS_DIM // 2, 1), pltpu.roll(x, AXIS_DIM // 2, 1))
    return x * cos + swapped * sin


LOG2E = 1.4426950408889634


def _softmax_av(s, sink, vals):
    m = jnp.maximum(jnp.max(s, axis=-1, keepdims=True), sink)
    p = jnp.exp2(s - m)
    denom = jnp.sum(p, axis=-1, keepdims=True) + jnp.exp2(sink - m)
    return _mm(p, vals) / denom


def _attend(q, keys, vals, band, sink_all, o_ref, row0=0):
    b = q.shape[0]
    for kvh in range(ATT_KV_HEADS):
        kl = slice(kvh * HEAD_DIM, (kvh + 1) * HEAD_DIM)
        heads = range(kvh * ATT_GROUP, (kvh + 1) * ATT_GROUP)
        qs = jnp.concatenate([q[:, h * HEAD_DIM:(h + 1) * HEAD_DIM] for h in heads], axis=0)
        sink = jnp.concatenate([jnp.broadcast_to(sink_all[0:1, h:h + 1], (b, 1)) for h in heads], axis=0)
        s = _mm(qs, keys[:, kl], _NT)
        if band is not None:
            kb = ATT_BLOCK
            s = jnp.concatenate([jnp.where(band[0], s[:, 0:kb], NEG), s[:, kb:2 * kb],
                                 jnp.where(band[1], s[:, 2 * kb:3 * kb], NEG), s[:, 3 * kb:]], axis=1)
        o = _softmax_av(s, sink * LOG2E, vals[:, kl])
        for g, h in enumerate(heads):
            o_ref[row0:row0 + b, h * HEAD_DIM:(h + 1) * HEAD_DIM] = o[g * b:(g + 1) * b, :]


def _band_valid(first, last):
    b = ATT_BLOCK
    c = lax.broadcasted_iota(jnp.int32, (1, b), 1)
    r = lax.broadcasted_iota(jnp.int32, (ATT_GROUP * b, 1), 0) & (b - 1)
    prev_ok = jnp.where(first, -1, c) >= r
    next_ok = jnp.where(last, b, c) <= r
    return prev_ok, next_ok


def _rope_block(rowtab_ref, coltab_ref, blk):
    rt = rowtab_ref[blk]
    ct = coltab_ref[...]
    reps = ATT_BLOCK // GRID_W
    rows = jnp.concatenate([jnp.broadcast_to(rt[g:g + 1, :], (GRID_W, LANE)) for g in range(reps)], axis=0)
    return rows + jnp.concatenate([ct] * reps, axis=0)


def _attn_kernel(q_ref, kp_ref, kc_ref, kn_ref, kctx_ref, cosr_ref, sinr_ref, cosc_ref, sinc_ref, sink_ref, o_ref,
                 *, nb, with_ctx):
    j = pl.program_id(0)
    b = ATT_BLOCK
    scale = HEAD_DIM ** -0.5 * LOG2E

    @pl.when(j == 0)
    def _():
        if with_ctx:
            kvx = kctx_ref[...]
            _attend(q_ref[...] * scale, kvx[:, :ATT_KV_DIM], kvx[:, ATT_KV_DIM:], None, sink_ref[...], o_ref)
        else:
            o_ref[...] = jnp.zeros(o_ref.shape, F32)

    @pl.when(j > 0)
    def _():
        b0 = 2 * (j - 1)
        blocks = (jnp.maximum(b0 - 1, 0), b0, b0 + 1, jnp.minimum(b0 + 2, nb - 1))
        cos = [_rope_block(cosr_ref, cosc_ref, blk) for blk in blocks]
        sin = [_rope_block(sinr_ref, sinc_ref, blk) for blk in blocks]
        kvc = kc_ref[...]
        kv = (kp_ref[...], kvc[0:b, :], kvc[b:2 * b, :], kn_ref[...])
        kvx = kctx_ref[...]
        keys = [_rope(t[:, :ATT_KV_DIM], c_, s_) for t, c_, s_ in zip(kv, cos, sin)]
        q_all = q_ref[...]
        sink_all = sink_ref[...]
        for sub in range(2):
            q = q_all[sub * b:(sub + 1) * b, :]
            q = jnp.concatenate([_rope(q[:, l * LANE:(l + 1) * LANE], cos[1 + sub], sin[1 + sub])
                                 for l in range(ATT_DIM // LANE)], axis=1)
            kcat = jnp.concatenate(keys[sub:sub + 3] + [kvx[:, :ATT_KV_DIM]], axis=0)
            vcat = jnp.concatenate([t[:, ATT_KV_DIM:] for t in kv[sub:sub + 3]] + [kvx[:, ATT_KV_DIM:]], axis=0)
            band = _band_valid(first=(b0 + sub == 0), last=(b0 + sub == nb - 1))
            _attend(q * scale, kcat, vcat, band, sink_all, o_ref, row0=sub * b)


def _attention(pq, pkv, rope, sink, ctx_len, with_ctx):
    t = pq.shape[0]
    n = t - ctx_len
    nb = n // ATT_BLOCK
    step = 2 * ATT_BLOCK
    assert ctx_len == step and nb % 2 == 0
    off = ctx_len // ATT_BLOCK
    half = lambda im: pl.BlockSpec((ATT_BLOCK, 2 * ATT_KV_DIM), im)
    lat = lambda j: 2 * jnp.maximum(j - 1, 0)
    prv = lambda j: (jnp.maximum(lat(j) - 1, 0) + off, 0)
    nxt = lambda j: (jnp.minimum(lat(j) + 2, nb - 1) + off, 0)
    cur = lambda j: (j, 0)
    whole = lambda a: pl.BlockSpec(a.shape, lambda j: (0,) * a.ndim)
    return pl.pallas_call(
        functools.partial(_attn_kernel, nb=nb, with_ctx=with_ctx),
        grid=(1 + nb // 2,),
        in_specs=[
            pl.BlockSpec((step, ATT_DIM), cur),
            half(prv), pl.BlockSpec((step, 2 * ATT_KV_DIM), cur), half(nxt),
            pl.BlockSpec((ctx_len, 2 * ATT_KV_DIM), lambda j: (0, 0)),
        ] + [whole(a) for a in rope] + [pl.BlockSpec((8, LANE), lambda j: (0, 0))],
        out_specs=pl.BlockSpec((step, ATT_DIM), cur),
        out_shape=jax.ShapeDtypeStruct((t, ATT_DIM), F32),
        compiler_params=_params(("arbitrary",), VMEM_LIMIT),
        name="attention",
    )(pq, pkv, pkv, pkv, pkv, *rope, sink)


def _mixfin_kernel(ctx_ref, h_ref, mod_ref, pconv_ref, prev_ref, next_ref, cw_ref, of_ref, ob_ref, z_ref, ng_ref,
                   yc_ref, wout_ref, bd_ref, *rest, nblk, blk0, with_router):
    if with_router:
        wr1_ref, wr0_ref, x_ref, hx_ref, lg_ref = rest
    else:
        wg_ref, wu_ref, wd_ref, x_ref = rest
    i = pl.program_id(0) + blk0
    row = _mod_row(mod_ref, i == 0)
    pc = pconv_ref[...]
    u = pc[:, CONV_CH:2 * CONV_CH] * pc[:, 2 * CONV_CH:]
    prow, nrow = _halo_rows(prev_ref, next_ref, i, nblk)
    prow = prow[:, CONV_CH:2 * CONV_CH] * prow[:, 2 * CONV_CH:]
    nrow = nrow[:, CONV_CH:2 * CONV_CH] * nrow[:, 2 * CONV_CH:]
    up, un = _shift_rows(u, prow, nrow)
    cw = cw_ref[...]
    ya = pc[:, :CONV_CH] * (up * cw[0:1, :] + u * cw[1:2, :] + un * cw[2:3, :])
    o = of_ref[...] + ob_ref[...]
    ms = _group_sum(o * o, bd_ref[...]) * (1.0 / DN_HEAD_DIM)
    z = z_ref[...]
    yb = o * lax.rsqrt(ms + EPS) * ng_ref[...] * (z * _sigmoid(z))
    mix = jnp.concatenate([ya, yb, yc_ref[...]], axis=1)
    x = jnp.where(i == 0, ctx_ref[...], h_ref[...]) + row[:, 2 * D:3 * D] * _mm(mix, wout_ref[...])
    hx = _rmsnorm_rows(x) * (1.0 + row[:, 4 * D:5 * D]) + row[:, 3 * D:4 * D]
    if not with_router:
        hb = hx.astype(BF16)
        acc = jnp.zeros((hb.shape[0], D), F32)
        for f in range(0, D_FF, FF_CHUNK):
            fe = min(f + FF_CHUNK, D_FF)
            g = _mm(hb, wg_ref[:, f:fe])
            u_ = _mm(hb, wu_ref[:, f:fe])
            acc = acc + _mm(g * _sigmoid(g) * u_, wd_ref[f:fe, :])
        x_ref[...] = x + row[:, 5 * D:6 * D] * acc
    else:
        x_ref[...] = x
        hx_ref[...] = hx
        h1, h0 = _split2(hx)
        d = functools.partial(lax.dot_general, dimension_numbers=_NN, preferred_element_type=F32)
        lg = d(h1, wr1_ref[...]) + (d(h0, wr1_ref[...]) + d(h1, wr0_ref[...]))
        lane = lax.broadcasted_iota(jnp.int32, lg.shape, 1)
        lanef = lane.astype(F32)
        lg = jnp.where(lane < N_EXPERTS, lg, -jnp.inf)
        m1 = jnp.max(lg, axis=-1, keepdims=True)
        i1 = jnp.min(jnp.where(lg == m1, lanef, float(LANE)), axis=-1, keepdims=True)
        rest = jnp.where(lanef == i1, -jnp.inf, lg)
        m2 = jnp.max(rest, axis=-1, keepdims=True)
        i2 = jnp.min(jnp.where(rest == m2, lanef, float(LANE)), axis=-1, keepdims=True)
        e2 = jnp.exp(m2 - m1)
        g1 = 1.0 / (1.0 + e2)
        lg_ref[...] = jnp.where(lane == 0, g1, jnp.where(lane == 1, e2 * g1, jnp.where(lane == 2, i1, i2)))


def _mixer_finish(ctx_src, lat_src, lat_blk0, mod, pconv, conv_w, o_f, o_b, pz, norm_g, yc, w_out, bd, ctx_len,
                  with_ctx, router=None, ffn=None):
    assert (router is None) != (ffn is None)
    t = pconv.shape[0]
    nblk = t // TM
    blk0 = 0 if with_ctx else ctx_len // TM
    rows = t - blk0 * TM
    r8 = TM // 8
    w = pconv.shape[1]
    cur = lambda i: (i + blk0, 0)
    out_cur = lambda i: (i, 0)
    const = lambda i: (0, 0)
    ng = jnp.tile(norm_g.reshape(1, DN_HEAD_DIM), (1, DN_HEADS))
    in_specs = [
        pl.BlockSpec((TM, D), const),
        pl.BlockSpec((TM, D), lambda i: (jnp.maximum(i + blk0 - 1, 0) + lat_blk0, 0)),
        pl.BlockSpec((8, N_MOD * D), const),
        pl.BlockSpec((TM, w), cur),
        pl.BlockSpec((8, w), lambda i: (jnp.maximum((i + blk0) * r8 - 1, 0), 0)),
        pl.BlockSpec((8, w), lambda i: (jnp.minimum((i + blk0 + 1) * r8, t // 8 - 1), 0)),
        pl.BlockSpec((3, CONV_CH), const),
        pl.BlockSpec((TM, DN_DIM), cur),
        pl.BlockSpec((TM, DN_DIM), cur),
        pl.BlockSpec((TM, DN_DIM), cur),
        pl.BlockSpec((1, DN_DIM), const),
        pl.BlockSpec((TM, ATT_DIM), cur),
        pl.BlockSpec((MIX_DIM, D), const),
        pl.BlockSpec(bd.shape, const),
    ]
    args = [ctx_src, lat_src, mod, pconv, pconv, pconv, conv_w, o_f, o_b, pz, ng, yc, w_out, bd]
    out_specs = [pl.BlockSpec((TM, D), out_cur)]
    out_shape = [jax.ShapeDtypeStruct((rows, D), F32)]
    if router is not None:
        in_specs += [pl.BlockSpec((D, LANE), const)] * 2
        args += list(router)
        out_specs += [pl.BlockSpec((TM, D), out_cur), pl.BlockSpec((TM, LANE), out_cur)]
        out_shape += [jax.ShapeDtypeStruct((rows, D), F32), jax.ShapeDtypeStruct((rows, LANE), F32)]
    else:
        once = dict(pipeline_mode=pl.Buffered(1))
        in_specs += [pl.BlockSpec(w_.shape, const, **once) for w_ in ffn]
        args += list(ffn)
    return pl.pallas_call(
        functools.partial(_mixfin_kernel, nblk=nblk, blk0=blk0, with_router=router is not None),
        grid=(rows // TM,),
        in_specs=in_specs,
        out_specs=out_specs,
        out_shape=out_shape,
        compiler_params=_params(("arbitrary",), VMEM_LIMIT),
        name="mixer_finish",
    )(*args)


def _moe_kernel(be_ref, nu_ref, xs_ref, wg_ref, wu_ref, wd_ref, *rest, boff):
    y_ref, acc_ref = rest[-2:]
    b = pl.program_id(0) + boff
    f = pl.program_id(1)

    @pl.when(b < nu_ref[0])
    def _():
        xs = xs_ref[...].astype(BF16)
        part = jnp.zeros((MOE_TM, D), F32)
        for c in range(0, MOE_TF, MOE_SUB):
            g = _mm(xs, wg_ref[0, :, c:c + MOE_SUB])
            u = _mm(xs, wu_ref[0, :, c:c + MOE_SUB])
            part = part + _mm(g * _sigmoid(g) * u, wd_ref[0, c:c + MOE_SUB, :])

        @pl.when(f == 0)
        def _():
            acc_ref[...] = part

        @pl.when(f != 0)
        def _():
            acc_ref[...] += part

        @pl.when(f == pl.num_programs(1) - 1)
        def _():
            y_ref[...] = acc_ref[...]

    @pl.when(b >= nu_ref[0])
    def _():
        y_ref[...] = jnp.zeros(y_ref.shape, F32)


def _moe_experts(xs_parts, blk_e, n_used, wg, wu, wd):
    cap = sum(xs.shape[0] for xs in xs_parts)
    nf = D_FF_EXPERT // MOE_TF
    y = None
    boff = 0
    for xs in xs_parts:
        nblk_p = xs.shape[0] // MOE_TM

        def fidx(b, f, nu):
            return jnp.where(b < nu[0], f, nf - 1)

        g = lambda b, boff=boff: b + boff
        in_specs = [
            pl.BlockSpec((MOE_TM, D), lambda b, f, be, nu: (b, 0)),
            pl.BlockSpec((1, D, MOE_TF), lambda b, f, be, nu, g=g: (be[g(b)], 0, fidx(g(b), f, nu))),
            pl.BlockSpec((1, D, MOE_TF), lambda b, f, be, nu, g=g: (be[g(b)], 0, fidx(g(b), f, nu))),
            pl.BlockSpec((1, MOE_TF, D), lambda b, f, be, nu, g=g: (be[g(b)], fidx(g(b), f, nu), 0)),
        ]
        args = [blk_e, n_used, xs, wg, wu, wd]
        aliases = {}
        if y is not None:
            in_specs.append(pl.BlockSpec(memory_space=pl.ANY))
            aliases = {len(args): 0}
            args.append(y)
        grid_spec = pltpu.PrefetchScalarGridSpec(
            num_scalar_prefetch=2,
            grid=(nblk_p, nf),
            in_specs=in_specs,
            out_specs=pl.BlockSpec((MOE_TM, D), lambda b, f, be, nu, g=g: (g(b), 0)),
            scratch_shapes=[pltpu.VMEM((MOE_TM, D), F32)],
        )
        y = pl.pallas_call(
            functools.partial(_moe_kernel, boff=boff),
            grid_spec=grid_spec,
            out_shape=jax.ShapeDtypeStruct((cap, D), F32),
            input_output_aliases=aliases,
            compiler_params=_params(("arbitrary", "arbitrary"), VMEM_LIMIT),
            name="moe_experts",
        )(*args)
        boff += nblk_p
    return y


def _moe_route(top_e):
    n = top_e.shape[0]
    a = n * TOP_K
    flat_e = top_e.reshape(a)
    onehot = (flat_e[:, None] == jnp.arange(N_EXPERTS, dtype=flat_e.dtype)[None, :]).astype(jnp.int32)
    counts = jnp.sum(onehot, axis=0)
    padded = (counts + MOE_TM - 1) // MOE_TM * MOE_TM
    pad_ends = jnp.cumsum(padded)
    pad_starts = pad_ends - padded
    dest = jnp.sum(onehot * (jnp.cumsum(onehot, axis=0) - onehot + pad_starts[None, :]), axis=1)
    cap = a + N_EXPERTS * MOE_TM
    nblk = cap // MOE_TM
    row_tok = (jnp.arange(cap, dtype=jnp.int32) % n).at[dest].set(
        jnp.arange(a, dtype=jnp.int32) // TOP_K, unique_indices=True, mode="promise_in_bounds")
    blk_start = jnp.arange(nblk, dtype=jnp.int32) * MOE_TM
    blk_e = jnp.minimum(jnp.sum((pad_ends[None, :] <= blk_start[:, None]).astype(jnp.int32), axis=1),
                        N_EXPERTS - 1)
    n_used = (pad_ends[-1] // MOE_TM).astype(jnp.int32).reshape(1)
    last_e = blk_e[jnp.maximum(n_used[0] - 1, 0)]
    blk_e = jnp.where(jnp.arange(nblk) < n_used[0], blk_e, last_e)
    return dest.reshape(n, TOP_K), row_tok, blk_e, n_used


def _final_kernel(x_ref, y0_ref, y1_ref, gt_ref, mod_ref, fg_ref, *rest):
    o_ref = rest[-1]
    mod = mod_ref[...]
    gt = gt_ref[...]
    f = gt[:, 0:1] * y0_ref[...] + gt[:, 1:2] * y1_ref[...]
    x = x_ref[...] + mod[0:1, 5 * D:6 * D] * f
    o_ref[...] = _rmsnorm_rows(x) * fg_ref[...]


def _moe_combine_final(x, y_parts, gt, mod, final_g):
    n = x.shape[0]
    const = lambda i: (0, 0)
    out = None
    boff = 0
    for y0, y1 in y_parts:
        nblk_p = y0.shape[0] // TM
        glob = lambda i, off=boff: (i + off, 0)
        loc = lambda i: (i, 0)
        in_specs = [pl.BlockSpec((TM, D), glob), pl.BlockSpec((TM, D), loc), pl.BlockSpec((TM, D), loc),
                    pl.BlockSpec((TM, LANE), glob), pl.BlockSpec((8, N_MOD * D), const),
                    pl.BlockSpec((1, D), const)]
        args = [x, y0, y1, gt, mod, final_g.reshape(1, D)]
        aliases = {}
        if out is not None:
            in_specs.append(pl.BlockSpec(memory_space=pl.ANY))
            aliases = {len(args): 0}
            args.append(out)
        out = pl.pallas_call(
            _final_kernel,
            grid=(nblk_p,),
            in_specs=in_specs,
            out_specs=pl.BlockSpec((TM, D), glob),
            out_shape=jax.ShapeDtypeStruct((n, D), F32),
            input_output_aliases=aliases,
            compiler_params=_params(("arbitrary",), VMEM_LIMIT),
            name="moe_combine_final",
        )(*args)
        boff += nblk_p
    return out


def _rope_tables(n):
    lane = jnp.arange(LANE, dtype=jnp.int32) % HEAD_DIM
    inv = ROPE_BASE ** (-jnp.arange(0, AXIS_DIM, 2, dtype=F32) / AXIS_DIM)
    freq = inv[lane % (AXIS_DIM // 2)]
    row_axis = (lane // AXIS_DIM) == 0
    sign = jnp.where((lane % AXIS_DIM) < AXIS_DIM // 2, -1.0, 1.0)
    reps = ATT_BLOCK // GRID_W

    def tables(count, on_axis):
        ang = jnp.arange(count, dtype=F32)[:, None] * freq[None, :]
        return jnp.where(on_axis, jnp.cos(ang), 0.0), jnp.where(on_axis, jnp.sin(ang) * sign, 0.0)

    by_block = lambda t: jnp.pad(t.reshape(-1, reps, LANE), ((0, 0), (0, 8 - reps), (0, 0)))
    cosr, sinr = tables(n // GRID_W, row_axis)
    cosc, sinc = tables(GRID_W, ~row_axis)
    return by_block(cosr), by_block(sinr), cosc, sinc


def _prep_w_in(w):
    pad = jnp.zeros((D, LANE - N_AB), w.dtype)
    return jnp.concatenate([w[:, :_C_Z], w[:, _C_A:_C_Q], pad, w[:, _C_Z:_C_A], w[:, _C_Q:_C_END]],
                           axis=1).astype(BF16)


def kernel(x, c, ctx, c_ctx, w_mod, b_mod, w_in, w_out, conv_w, dn_conv_w, dn_a_log, dn_dt_bias, dn_norm_g,
           attn_sink, ffn_w_gate, ffn_w_up, ffn_w_down, moe_router, moe_w_gate, moe_w_up, moe_w_down,
           final_norm_g):
    bsz, n, d = x.shape
    ctx_len = ctx.shape[1]
    depth = w_in.shape[0]
    assert bsz == 1 and d == D and ctx_len == TM and n % TM == 0 and n % GRID_W == 0
    rope = _rope_tables(n)
    mods = _mod_vectors(c, c_ctx, w_mod, b_mod)
    bd = _head_blockdiag(DN_DIM, DN_HEAD_DIM)
    stream = (ctx[0], x[0], 0)
    for layer in range(depth):
        last = layer == depth - 1
        mod = mods[layer]
        pconv, pz, pq, pkv, qn, kn, vv, gb = _in_proj(*stream, mod, _prep_w_in(w_in[layer]), dn_conv_w[layer],
                                                      dn_a_log[layer], dn_dt_bias[layer], bd)
        to_bf16 = []
        if layer % 2 == 0:
            to_bf16 += [w[layer // 2] for w in (ffn_w_gate, ffn_w_up, ffn_w_down)]
        if (layer + 1) % 2 == 1 and layer + 1 < depth:
            j = (layer + 1) // 2
            to_bf16 += [w[j].reshape(-1, w.shape[-1]) for w in (moe_w_gate, moe_w_up, moe_w_down)]
        o_f, o_b, *cast = _delta_net(qn, kn, vv, gb, ctx_len, tuple(to_bf16))
        if layer % 2 == 0:
            ffn_bf16, cast = cast[:3], cast[3:]
        if cast:
            moe_bf16 = [c.reshape(w.shape[1:]) for c, w in zip(cast, (moe_w_gate, moe_w_up, moe_w_down))]
        sink = jnp.zeros((8, LANE), F32).at[0, :ATT_HEADS].set(attn_sink[layer])
        yc = _attention(pq, pkv, rope, sink, ctx_len, with_ctx=not last)
        router = None
        if layer % 2 == 1:
            wr = jnp.zeros((D, LANE), F32).at[:, :N_EXPERTS].set(moe_router[layer // 2])
            wr1 = wr.astype(BF16)
            router = (wr1, (wr - wr1.astype(F32)).astype(BF16))
        outs = _mixer_finish(*stream, mod, pconv, conv_w[layer], o_f, o_b, pz, dn_norm_g[layer], yc,
                             w_out[layer].astype(BF16), bd, ctx_len, with_ctx=not last, router=router,
                             ffn=ffn_bf16 if layer % 2 == 0 else None)
        if layer % 2 == 0:
            assert not last
            h, = outs
            stream = (h, h, ctx_len // TM)
        else:
            assert last
            x1, hx, route = outs
            i = layer // 2
            dest, row_tok, blk_e, n_used = _moe_route(route[:, 2:2 + TOP_K].astype(jnp.int32))
            take = lambda rows_, idx: rows_.at[idx].get(mode="promise_in_bounds")
            parts = lambda a: (a[:a.shape[0] // MOE_FIRST_PART], a[a.shape[0] // MOE_FIRST_PART:])
            y = _moe_experts([take(hx, r) for r in parts(row_tok)], blk_e, n_used, *moe_bf16)
            h = _moe_combine_final(x1, [(take(y, dd[:, 0]), take(y, dd[:, 1])) for dd in parts(dest)],
                                   route, mod, final_norm_g)
    return h.reshape(bsz, n, d)
```

```python
import functools

import jax
import jax.numpy as jnp
from jax import lax
from jax.experimental import pallas as pl
from jax.experimental.pallas import tpu as pltpu

F32 = jnp.float32
BF16 = jnp.bfloat16

D = 1024
N_MOD = 6
EPS = 1e-6
NEG = -1e30
GRID_W = 64

CONV_CH = 256
DN_HEADS = 6
DN_HEAD_DIM = 64
DN_DIM = DN_HEADS * DN_HEAD_DIM
DN_CHUNK = 64
DN_SUB = 16
DN_STEP_CHUNKS = 4
ATT_HEADS = 6
ATT_KV_HEADS = 2
ATT_GROUP = ATT_HEADS // ATT_KV_HEADS
HEAD_DIM = 64
ATT_DIM = ATT_HEADS * HEAD_DIM
ATT_KV_DIM = ATT_KV_HEADS * HEAD_DIM
ATT_BLOCK = 128
ROPE_BASE = 10000.0
AXIS_DIM = HEAD_DIM // 2
MIX_DIM = CONV_CH + DN_DIM + ATT_DIM

D_FF = 2816
N_EXPERTS = 8
TOP_K = 2
D_FF_EXPERT = 3584

TM = 256
FF_CHUNK = 512
MOE_TM = 512
MOE_TF = 1792
MOE_SUB = 256
LANE = 128
VMEM_LIMIT = 56 * 1024 * 1024

_C_QKV = 3 * CONV_CH
_C_Z = _C_QKV + 3 * DN_DIM
_C_A = _C_Z + DN_DIM
_C_Q = _C_A + 4 * DN_HEADS
_C_K = _C_Q + ATT_DIM
_C_V = _C_K + ATT_KV_DIM
_C_END = _C_V + ATT_KV_DIM
N_AB = 4 * DN_HEADS


def _params(sem=None, vmem=None):
    kw = {}
    if sem is not None:
        kw["dimension_semantics"] = sem
    if vmem is not None:
        kw["vmem_limit_bytes"] = vmem
    return pltpu.CompilerParams(**kw)


def _split2(a):
    hi = a.astype(BF16)
    lo = (a - hi.astype(F32)).astype(BF16)
    return hi, lo


def _split3(a):
    hi = a.astype(BF16)
    r = a - hi.astype(F32)
    mid = r.astype(BF16)
    lo = (r - mid.astype(F32)).astype(BF16)
    return hi, mid, lo


_NN = (((1,), (0,)), ((), ()))
_NT = (((1,), (1,)), ((), ()))
_TN = (((0,), (0,)), ((), ()))


def _mm(a, b, dims=_NN):
    return lax.dot_general(a.astype(BF16), b.astype(BF16), dims, preferred_element_type=F32)


def _mm3(a, b, dims=_NN):
    a1, a0 = _split2(a)
    b1, b0 = _split2(b)
    d = functools.partial(lax.dot_general, dimension_numbers=dims, preferred_element_type=F32)
    return d(a1, b1) + (d(a1, b0) + d(a0, b1))


_BNN = (((2,), (1,)), ((0,), (0,)))
_BNT = (((2,), (2,)), ((0,), (0,)))
_BTN = (((1,), (1,)), ((0,), (0,)))


def _bmm(a, b, dims=_BNN):
    return lax.dot_general(a.astype(BF16), b.astype(BF16), dims, preferred_element_type=F32)


def _sigmoid(x):
    return 1.0 / (1.0 + jnp.exp(-x))


def _softplus(x):
    return jnp.maximum(x, 0.0) + jnp.log1p(jnp.exp(-jnp.abs(x)))


def _mod_row(mod_ref, is_ctx):
    mod = mod_ref[...]
    return jnp.where(is_ctx, mod[1:2, :], mod[0:1, :])


def _rmsnorm_rows(x):
    return x * lax.rsqrt(jnp.mean(x * x, axis=-1, keepdims=True) + EPS)


def _shift_rows(u, prow, nrow):
    n = u.shape[0]
    rid = lax.broadcasted_iota(jnp.int32, u.shape, 0)
    up = jnp.where(rid == 0, prow, pltpu.roll(u, 1, 0))
    un = jnp.where(rid == n - 1, nrow, pltpu.roll(u, n - 1, 0))
    return up, un


def _same_group(shape, group):
    sh = group.bit_length() - 1
    assert 1 << sh == group
    return (lax.broadcasted_iota(jnp.int32, shape, 0) >> sh) == (lax.broadcasted_iota(jnp.int32, shape, 1) >> sh)


def _head_blockdiag(n, group):
    g = jnp.arange(n, dtype=jnp.int32) // group
    return (g[:, None] == g[None, :]).astype(BF16)


def _group_sum(t, bd):
    return _mm(t, bd)


MOD_TN = 1536


def _mod_kernel(s_ref, w_ref, b_ref, o_ref):
    s = s_ref[...]
    s = s * _sigmoid(s)
    o_ref[0] = _mm3(s, w_ref[0]) + b_ref[0]


def _mod_vectors(c, c_ctx, w_mod, b_mod):
    depth = w_mod.shape[0]
    s = jnp.zeros((8, D), F32).at[0].set(c[0]).at[1].set(c_ctx)
    return pl.pallas_call(
        _mod_kernel,
        grid=(depth, N_MOD * D // MOD_TN),
        in_specs=[
            pl.BlockSpec((8, D), lambda l, j: (0, 0)),
            pl.BlockSpec((1, D, MOD_TN), lambda l, j: (l, 0, j)),
            pl.BlockSpec((1, 1, MOD_TN), lambda l, j: (l, 0, j)),
        ],
        out_specs=pl.BlockSpec((1, 8, MOD_TN), lambda l, j: (l, 0, j)),
        out_shape=jax.ShapeDtypeStruct((depth, 8, N_MOD * D), F32),
        compiler_params=_params(("arbitrary", "arbitrary"), VMEM_LIMIT),
        name="mod_vectors",
    )(s, w_mod, b_mod.reshape(depth, 1, N_MOD * D))


def _halo_valid(i, nblk):
    return jnp.logical_and(i != 0, i != 1), jnp.logical_and(i != 0, i != nblk - 1)


def _halo_rows(prev_ref, next_ref, i, nblk):
    pvalid, nvalid = _halo_valid(i, nblk)
    prow = jnp.where(pvalid, prev_ref[7:8, :], 0.0)
    nrow = jnp.where(nvalid, next_ref[0:1, :], 0.0)
    return prow, nrow


def _in_kernel(ctx_ref, h_ref, hprev_ref, hnext_ref, mod_ref, w_ref, cw_ref, alog_ref, dtb_ref, bd_ref,
               pconv_ref, pz_ref, pq_ref, pkv_ref, q_ref, k_ref, v_ref, gb_ref, *, nblk):
    i = pl.program_id(0)
    row = _mod_row(mod_ref, i == 0)
    norm_mod = lambda x: _rmsnorm_rows(x) * (1.0 + row[:, D:2 * D]) + row[:, 0:D]
    h1 = norm_mod(jnp.where(i == 0, ctx_ref[...], h_ref[...])).astype(BF16)
    halo = norm_mod(jnp.concatenate([hprev_ref[...], hnext_ref[...]], axis=0)).astype(BF16)
    d = functools.partial(lax.dot_general, dimension_numbers=_NN, preferred_element_type=F32)
    tm = h1.shape[0]
    c0 = 3 * CONV_CH
    c1 = c0 + 3 * DN_DIM
    c2 = c1 + LANE
    c3 = c2 + DN_DIM + ATT_DIM
    pconv_ref[...] = d(h1, w_ref[:, 0:c0])
    zq = d(h1, w_ref[:, c2:c3])
    pz_ref[...] = zq[:, 0:DN_DIM]
    pq_ref[...] = zq[:, DN_DIM:]
    pkv_ref[...] = d(h1, w_ref[:, c3:c3 + 2 * ATT_KV_DIM])
    qkv_ab = d(jnp.concatenate([h1, halo], axis=0), w_ref[:, c0:c2])
    qkv = qkv_ab[:, 0:3 * DN_DIM]
    ab = qkv_ab[0:tm, 3 * DN_DIM:]

    u = qkv[0:tm, :]
    pvalid, nvalid = _halo_valid(i, nblk)
    prow = jnp.where(pvalid, qkv[tm + 7:tm + 8, :], 0.0)
    nrow = jnp.where(nvalid, qkv[tm + 8:tm + 9, :], 0.0)
    up, un = _shift_rows(u, prow, nrow)
    cw = cw_ref[...]
    y = up * cw[0:1, :] + u * cw[1:2, :] + un * cw[2:3, :]
    y = y * _sigmoid(y)
    q = y[:, 0:DN_DIM]
    k = y[:, DN_DIM:2 * DN_DIM]
    bd = bd_ref[...]
    q_ref[...] = q * lax.rsqrt(_group_sum(q * q, bd) + 1e-6) * (DN_HEAD_DIM ** -0.5)
    k_ref[...] = k * lax.rsqrt(_group_sum(k * k, bd) + 1e-6)
    v_ref[...] = y[:, 2 * DN_DIM:3 * DN_DIM]
    g = -jnp.exp(alog_ref[...]) * _softplus(ab + dtb_ref[...])
    lane = lax.broadcasted_iota(jnp.int32, ab.shape, 1)
    gb_ref[...] = jnp.where(lane < 2 * DN_HEADS, g, _sigmoid(ab))


def _in_proj(ctx_src, lat_src, lat_blk0, mod, w_main, dn_conv_w, a_log, dt_bias, bd):
    r8 = TM // 8
    last8 = lat_src.shape[0] // 8 - 1
    nblk = 1 + lat_src.shape[0] // TM - lat_blk0
    t = nblk * TM
    alog = jnp.zeros((1, LANE), F32).at[0, :2 * DN_HEADS].set(a_log.reshape(-1))
    dtb = jnp.zeros((1, LANE), F32).at[0, :2 * DN_HEADS].set(dt_bias.reshape(-1))
    widths = (3 * CONV_CH, DN_DIM, ATT_DIM, 2 * ATT_KV_DIM, DN_DIM, DN_DIM, DN_DIM, LANE)
    const = lambda i: (0, 0)
    lat = lambda i: jnp.maximum(i - 1, 0) + lat_blk0
    return pl.pallas_call(
        functools.partial(_in_kernel, nblk=nblk),
        grid=(nblk,),
        in_specs=[
            pl.BlockSpec((TM, D), const),
            pl.BlockSpec((TM, D), lambda i: (lat(i), 0)),
            pl.BlockSpec((8, D), lambda i: (jnp.maximum(lat(i) * r8 - 1, 0), 0)),
            pl.BlockSpec((8, D), lambda i: (jnp.minimum((lat(i) + 1) * r8, last8), 0)),
            pl.BlockSpec((8, N_MOD * D), const),
            pl.BlockSpec(w_main.shape, const),
            pl.BlockSpec((3, 3 * DN_DIM), const),
            pl.BlockSpec((1, LANE), const),
            pl.BlockSpec((1, LANE), const),
            pl.BlockSpec(bd.shape, const),
        ],
        out_specs=[pl.BlockSpec((TM, w), lambda i: (i, 0)) for w in widths],
        out_shape=[jax.ShapeDtypeStruct((t, w), F32) for w in widths],
        compiler_params=_params(("arbitrary",), VMEM_LIMIT),
        name="in_proj",
    )(ctx_src, lat_src, lat_src, lat_src, mod, w_main, dn_conv_w, alog, dtb, bd)


def _dn_block(fwd_refs, bwd_refs, of_ref, ob_ref, s_ref):
    c_ = DN_CHUNK
    nh = DN_HEADS
    nchunks = fwd_refs[0].shape[0] // c_
    nb = nchunks * nh
    rows = lambda g: slice(g * c_, (g + 1) * c_)
    lanes = lambda h: slice(h * DN_HEAD_DIM, (h + 1) * DN_HEAD_DIM)
    ri = lax.broadcasted_iota(jnp.int32, (c_, c_), 0)
    ci = lax.broadcasted_iota(jnp.int32, (c_, c_), 1)
    same_sub = _same_group((c_, c_), DN_SUB)
    eye = jnp.where(ri == ci, 1.0, 0.0)
    shape3 = (2 * nb, c_, c_)
    delta = lax.broadcasted_iota(jnp.int32, shape3, 1) - lax.broadcasted_iota(jnp.int32, shape3, 2)
    delta = jnp.where(lax.broadcasted_iota(jnp.int32, shape3, 0) >= nb, -delta, delta)
    incl = delta >= 0
    strict = delta > 0

    def stack(fn):
        return jnp.stack([fn(d, refs, g, h) for d, refs in enumerate((fwd_refs, bwd_refs))
                          for g in range(nchunks) for h in range(nh)])

    gbs = [refs[3][...] for refs in (fwd_refs, bwd_refs)]
    tris = [jnp.where(ri >= ci, 1.0, 0.0).astype(BF16), jnp.where(ri <= ci, 1.0, 0.0).astype(BF16)]
    gcs = [[_cumsum_rows(tris[d], gbs[d][rows(g), :]) for g in range(nchunks)] for d in range(2)]
    gcts = [[gc.T for gc in gcs[d]] for d in range(2)]
    q = stack(lambda d, refs, g, h: refs[0][rows(g), lanes(h)])
    k = stack(lambda d, refs, g, h: refs[1][rows(g), lanes(h)])
    v = stack(lambda d, refs, g, h: refs[2][rows(g), lanes(h)])
    col = lambda d, h: d * nh + h
    gcol = stack(lambda d, refs, g, h: gcs[d][g][:, col(d, h):col(d, h) + 1])
    grow = stack(lambda d, refs, g, h: gcts[d][g][col(d, h):col(d, h) + 1, :])
    beta = stack(lambda d, refs, g, h: gbs[d][rows(g), 2 * nh + col(d, h):2 * nh + col(d, h) + 1])
    glast = jnp.concatenate([gcol[:nb, c_ - 1:c_, :], gcol[nb:, 0:1, :]], axis=0)
    decay = jnp.where(incl, jnp.exp(jnp.where(incl, gcol - grow, 0.0)), 0.0)
    eg = jnp.exp(gcol)
    kb = k * beta
    a = jnp.where(strict, _bmm(kb, k, _BNT) * decay, 0.0)
    qk = jnp.where(incl, _bmm(q, k, _BNT) * decay, 0.0)
    ad = jnp.where(same_sub, a, 0.0)
    ao = a - ad
    p = eye - ad
    n2 = _bmm(ad, ad)
    p = p + _bmm(p, n2)
    n4 = _bmm(n2, n2)
    p = p + _bmm(p, n4)
    n8 = _bmm(n4, n4)
    dinv = p + _bmm(p, n8)
    m = _bmm(dinv, ao)
    m2 = _bmm(m, m)
    y = _bmm(dinv, jnp.concatenate([v * beta, kb * eg], axis=-1))
    z = y + _bmm(m2, y)
    x = z - _bmm(m, z)
    u = x[:, :, :DN_HEAD_DIM]
    w = x[:, :, DN_HEAD_DIM:]
    qg = q * eg
    kd = k * jnp.exp(glast - gcol)
    gl = jnp.exp(glast)
    s = s_ref[...]
    for t in range(nchunks):
        gf, gr = t, nchunks - 1 - t
        step = lambda a: jnp.concatenate([a[gf * nh:(gf + 1) * nh], a[nb + gr * nh:nb + (gr + 1) * nh]], axis=0)
        v_new = step(u) - _bmm(step(w), s)
        o = _bmm(step(qg), s) + _bmm(step(qk), v_new)
        s = s * step(gl) + _bmm(step(kd), v_new, _BTN)
        for h in range(nh):
            of_ref[rows(gf), lanes(h)] = o[h]
            ob_ref[rows(gr), lanes(h)] = o[nh + h]
    s_ref[...] = s


def _cumsum_rows(tri_bf16, g):
    g2, g1, g0 = _split3(g)
    d = functools.partial(lax.dot_general, dimension_numbers=_NN, preferred_element_type=F32)
    return d(tri_bf16, g2) + (d(tri_bf16, g1) + d(tri_bf16, g0))


def _dn_kernel(qf, kf, vf, gf, qb, kb, vb, gbb, *rest, n_cast):
    cast_in = rest[:n_cast]
    of_ref, ob_ref = rest[n_cast:n_cast + 2]
    cast_out = rest[n_cast + 2:2 * n_cast + 2]
    s_ref = rest[-1]

    @pl.when(pl.program_id(0) == 0)
    def _():
        s_ref[...] = jnp.zeros(s_ref.shape, F32)

    _dn_block((qf, kf, vf, gf), (qb, kb, vb, gbb), of_ref, ob_ref, s_ref)
    for src, dst in zip(cast_in, cast_out):
        dst[...] = src[...].astype(BF16)


def _delta_net(q, k, v, gb, ctx_len, to_bf16=()):
    t = q.shape[0]
    rows = DN_STEP_CHUNKS * DN_CHUNK
    assert ctx_len == rows and t % rows == 0
    nstep = t // rows
    cast_specs = []
    for m in to_bf16:
        rb = -(-m.shape[0] // nstep)
        rb = -(-rb // 16) * 16
        nb = -(-m.shape[0] // rb)
        cast_specs.append(pl.BlockSpec((rb, m.shape[1]), lambda s, nb=nb: (jnp.minimum(s, nb - 1), 0)))

    def fwd(s):
        return (s, 0)

    def bwd(s):
        return (jnp.where(s == 0, 0, nstep - s), 0)

    wide = lambda im: pl.BlockSpec((rows, DN_DIM), im)
    narrow = lambda im: pl.BlockSpec((rows, LANE), im)
    return pl.pallas_call(
        functools.partial(_dn_kernel, n_cast=len(to_bf16)),
        grid=(nstep,),
        in_specs=[wide(fwd), wide(fwd), wide(fwd), narrow(fwd), wide(bwd), wide(bwd), wide(bwd), narrow(bwd)]
        + cast_specs,
        out_specs=[wide(fwd), wide(bwd)] + cast_specs,
        out_shape=[jax.ShapeDtypeStruct((t, DN_DIM), F32)] * 2
        + [jax.ShapeDtypeStruct(m.shape, BF16) for m in to_bf16],
        scratch_shapes=[pltpu.VMEM((2 * DN_HEADS, DN_HEAD_DIM, DN_HEAD_DIM), F32)],
        compiler_params=_params(("arbitrary",), VMEM_LIMIT),
        name="delta_net",
    )(q, k, v, gb, q, k, v, gb, *to_bf16)


def _rope(x, cos, sin):
    w = x.shape[1]
    lane = lax.broadcasted_iota(jnp.int32, x.shape, 1)
    first_half = (lane & (AXIS_DIM - 1)) < (AXIS_DIM // 2)
    swapped = jnp.where(first_half, pltpu.roll(x, w - AXIS_DIM // 2, 1), pltpu.roll(x, AXIS_DIM // 2, 1))
    return x * cos + swapped * sin


LOG2E = 1.4426950408889634


def _softmax_av(s, sink, vals):
    m = jnp.maximum(jnp.max(s, axis=-1, keepdims=True), sink)
    p = jnp.exp2(s - m)
    denom = jnp.sum(p, axis=-1, keepdims=True) + jnp.exp2(sink - m)
    return _mm(p, vals) / denom


def _attend(q, keys, vals, band, sink_all, o_ref, row0=0):
    b = q.shape[0]
    for kvh in range(ATT_KV_HEADS):
        kl = slice(kvh * HEAD_DIM, (kvh + 1) * HEAD_DIM)
        heads = range(kvh * ATT_GROUP, (kvh + 1) * ATT_GROUP)
        qs = jnp.concatenate([q[:, h * HEAD_DIM:(h + 1) * HEAD_DIM] for h in heads], axis=0)
        sink = jnp.concatenate([jnp.broadcast_to(sink_all[0:1, h:h + 1], (b, 1)) for h in heads], axis=0)
        s = _mm(qs, keys[:, kl], _NT)
        if band is not None:
            kb = ATT_BLOCK
            s = jnp.concatenate([jnp.where(band[0], s[:, 0:kb], NEG), s[:, kb:2 * kb],
                                 jnp.where(band[1], s[:, 2 * kb:3 * kb], NEG), s[:, 3 * kb:]], axis=1)
        o = _softmax_av(s, sink * LOG2E, vals[:, kl])
        for g, h in enumerate(heads):
            o_ref[row0:row0 + b, h * HEAD_DIM:(h + 1) * HEAD_DIM] = o[g * b:(g + 1) * b, :]


def _band_valid(first, last):
    b = ATT_BLOCK
    c = lax.broadcasted_iota(jnp.int32, (1, b), 1)
    r = lax.broadcasted_iota(jnp.int32, (ATT_GROUP * b, 1), 0) & (b - 1)
    prev_ok = jnp.where(first, -1, c) >= r
    next_ok = jnp.where(last, b, c) <= r
    return prev_ok, next_ok


def _rope_block(rowtab_ref, coltab_ref, blk):
    rt = rowtab_ref[blk]
    ct = coltab_ref[...]
    reps = ATT_BLOCK // GRID_W
    rows = jnp.concatenate([jnp.broadcast_to(rt[g:g + 1, :], (GRID_W, LANE)) for g in range(reps)], axis=0)
    return rows + jnp.concatenate([ct] * reps, axis=0)


def _attn_kernel(q_ref, kp_ref, kc_ref, kn_ref, kctx_ref, cosr_ref, sinr_ref, cosc_ref, sinc_ref, sink_ref, o_ref,
                 *, nb, with_ctx):
    j = pl.program_id(0)
    b = ATT_BLOCK
    scale = HEAD_DIM ** -0.5 * LOG2E

    @pl.when(j == 0)
    def _():
        if with_ctx:
            kvx = kctx_ref[...]
            _attend(q_ref[...] * scale, kvx[:, :ATT_KV_DIM], kvx[:, ATT_KV_DIM:], None, sink_ref[...], o_ref)
        else:
            o_ref[...] = jnp.zeros(o_ref.shape, F32)

    @pl.when(j > 0)
    def _():
        b0 = 2 * (j - 1)
        blocks = (jnp.maximum(b0 - 1, 0), b0, b0 + 1, jnp.minimum(b0 + 2, nb - 1))
        cos = [_rope_block(cosr_ref, cosc_ref, blk) for blk in blocks]
        sin = [_rope_block(sinr_ref, sinc_ref, blk) for blk in blocks]
        kvc = kc_ref[...]
        kv = (kp_ref[...], kvc[0:b, :], kvc[b:2 * b, :], kn_ref[...])
        kvx = kctx_ref[...]
        keys = [_rope(t[:, :ATT_KV_DIM], c_, s_) for t, c_, s_ in zip(kv, cos, sin)]
        q_all = q_ref[...]
        sink_all = sink_ref[...]
        for sub in range(2):
            q = q_all[sub * b:(sub + 1) * b, :]
            q = jnp.concatenate([_rope(q[:, l * LANE:(l + 1) * LANE], cos[1 + sub], sin[1 + sub])
                                 for l in range(ATT_DIM // LANE)], axis=1)
            kcat = jnp.concatenate(keys[sub:sub + 3] + [kvx[:, :ATT_KV_DIM]], axis=0)
            vcat = jnp.concatenate([t[:, ATT_KV_DIM:] for t in kv[sub:sub + 3]] + [kvx[:, ATT_KV_DIM:]], axis=0)
            band = _band_valid(first=(b0 + sub == 0), last=(b0 + sub == nb - 1))
            _attend(q * scale, kcat, vcat, band, sink_all, o_ref, row0=sub * b)


def _attention(pq, pkv, rope, sink, ctx_len, with_ctx):
    t = pq.shape[0]
    n = t - ctx_len
    nb = n // ATT_BLOCK
    step = 2 * ATT_BLOCK
    assert ctx_len == step and nb % 2 == 0
    off = ctx_len // ATT_BLOCK
    half = lambda im: pl.BlockSpec((ATT_BLOCK, 2 * ATT_KV_DIM), im)
    lat = lambda j: 2 * jnp.maximum(j - 1, 0)
    prv = lambda j: (jnp.maximum(lat(j) - 1, 0) + off, 0)
    nxt = lambda j: (jnp.minimum(lat(j) + 2, nb - 1) + off, 0)
    cur = lambda j: (j, 0)
    whole = lambda a: pl.BlockSpec(a.shape, lambda j: (0,) * a.ndim)
    return pl.pallas_call(
        functools.partial(_attn_kernel, nb=nb, with_ctx=with_ctx),
        grid=(1 + nb // 2,),
        in_specs=[
            pl.BlockSpec((step, ATT_DIM), cur),
            half(prv), pl.BlockSpec((step, 2 * ATT_KV_DIM), cur), half(nxt),
            pl.BlockSpec((ctx_len, 2 * ATT_KV_DIM), lambda j: (0, 0)),
        ] + [whole(a) for a in rope] + [pl.BlockSpec((8, LANE), lambda j: (0, 0))],
        out_specs=pl.BlockSpec((step, ATT_DIM), cur),
        out_shape=jax.ShapeDtypeStruct((t, ATT_DIM), F32),
        compiler_params=_params(("arbitrary",), VMEM_LIMIT),
        name="attention",
    )(pq, pkv, pkv, pkv, pkv, *rope, sink)


def _mixfin_kernel(ctx_ref, h_ref, mod_ref, pconv_ref, prev_ref, next_ref, cw_ref, of_ref, ob_ref, z_ref, ng_ref,
                   yc_ref, wout_ref, bd_ref, *rest, nblk, blk0, with_router):
    if with_router:
        wr1_ref, wr0_ref, x_ref, hx_ref, lg_ref = rest
    else:
        wg_ref, wu_ref, wd_ref, x_ref = rest
    i = pl.program_id(0) + blk0
    row = _mod_row(mod_ref, i == 0)
    pc = pconv_ref[...]
    u = pc[:, CONV_CH:2 * CONV_CH] * pc[:, 2 * CONV_CH:]
    prow, nrow = _halo_rows(prev_ref, next_ref, i, nblk)
    prow = prow[:, CONV_CH:2 * CONV_CH] * prow[:, 2 * CONV_CH:]
    nrow = nrow[:, CONV_CH:2 * CONV_CH] * nrow[:, 2 * CONV_CH:]
    up, un = _shift_rows(u, prow, nrow)
    cw = cw_ref[...]
    ya = pc[:, :CONV_CH] * (up * cw[0:1, :] + u * cw[1:2, :] + un * cw[2:3, :])
    o = of_ref[...] + ob_ref[...]
    ms = _group_sum(o * o, bd_ref[...]) * (1.0 / DN_HEAD_DIM)
    z = z_ref[...]
    yb = o * lax.rsqrt(ms + EPS) * ng_ref[...] * (z * _sigmoid(z))
    mix = jnp.concatenate([ya, yb, yc_ref[...]], axis=1)
    x = jnp.where(i == 0, ctx_ref[...], h_ref[...]) + row[:, 2 * D:3 * D] * _mm(mix, wout_ref[...])
    hx = _rmsnorm_rows(x) * (1.0 + row[:, 4 * D:5 * D]) + row[:, 3 * D:4 * D]
    if not with_router:
        hb = hx.astype(BF16)
        acc = jnp.zeros((hb.shape[0], D), F32)
        for f in range(0, D_FF, FF_CHUNK):
            fe = min(f + FF_CHUNK, D_FF)
            g = _mm(hb, wg_ref[:, f:fe])
            u_ = _mm(hb, wu_ref[:, f:fe])
            acc = acc + _mm(g * _sigmoid(g) * u_, wd_ref[f:fe, :])
        x_ref[...] = x + row[:, 5 * D:6 * D] * acc
    else:
        x_ref[...] = x
        hx_ref[...] = hx
        h1, h0 = _split2(hx)
        d = functools.partial(lax.dot_general, dimension_numbers=_NN, preferred_element_type=F32)
        lg = d(h1, wr1_ref[...]) + (d(h0, wr1_ref[...]) + d(h1, wr0_ref[...]))
        lane = lax.broadcasted_iota(jnp.int32, lg.shape, 1)
        lanef = lane.astype(F32)
        lg = jnp.where(lane < N_EXPERTS, lg, -jnp.inf)
        m1 = jnp.max(lg, axis=-1, keepdims=True)
        i1 = jnp.min(jnp.where(lg == m1, lanef, float(LANE)), axis=-1, keepdims=True)
        rest = jnp.where(lanef == i1, -jnp.inf, lg)
        m2 = jnp.max(rest, axis=-1, keepdims=True)
        i2 = jnp.min(jnp.where(rest == m2, lanef, float(LANE)), axis=-1, keepdims=True)
        e2 = jnp.exp(m2 - m1)
        g1 = 1.0 / (1.0 + e2)
        lg_ref[...] = jnp.where(lane == 0, g1, jnp.where(lane == 1, e2 * g1, jnp.where(lane == 2, i1, i2)))


def _mixer_finish(ctx_src, lat_src, lat_blk0, mod, pconv, conv_w, o_f, o_b, pz, norm_g, yc, w_out, bd, ctx_len,
                  with_ctx, router=None, ffn=None):
    assert (router is None) != (ffn is None)
    t = pconv.shape[0]
    nblk = t // TM
    blk0 = 0 if with_ctx else ctx_len // TM
    rows = t - blk0 * TM
    r8 = TM // 8
    w = pconv.shape[1]
    cur = lambda i: (i + blk0, 0)
    out_cur = lambda i: (i, 0)
    const = lambda i: (0, 0)
    ng = jnp.tile(norm_g.reshape(1, DN_HEAD_DIM), (1, DN_HEADS))
    in_specs = [
        pl.BlockSpec((TM, D), const),
        pl.BlockSpec((TM, D), lambda i: (jnp.maximum(i + blk0 - 1, 0) + lat_blk0, 0)),
        pl.BlockSpec((8, N_MOD * D), const),
        pl.BlockSpec((TM, w), cur),
        pl.BlockSpec((8, w), lambda i: (jnp.maximum((i + blk0) * r8 - 1, 0), 0)),
        pl.BlockSpec((8, w), lambda i: (jnp.minimum((i + blk0 + 1) * r8, t // 8 - 1), 0)),
        pl.BlockSpec((3, CONV_CH), const),
        pl.BlockSpec((TM, DN_DIM), cur),
        pl.BlockSpec((TM, DN_DIM), cur),
        pl.BlockSpec((TM, DN_DIM), cur),
        pl.BlockSpec((1, DN_DIM), const),
        pl.BlockSpec((TM, ATT_DIM), cur),
        pl.BlockSpec((MIX_DIM, D), const),
        pl.BlockSpec(bd.shape, const),
    ]
    args = [ctx_src, lat_src, mod, pconv, pconv, pconv, conv_w, o_f, o_b, pz, ng, yc, w_out, bd]
    out_specs = [pl.BlockSpec((TM, D), out_cur)]
    out_shape = [jax.ShapeDtypeStruct((rows, D), F32)]
    if router is not None:
        in_specs += [pl.BlockSpec((D, LANE), const)] * 2
        args += list(router)
        out_specs += [pl.BlockSpec((TM, D), out_cur), pl.BlockSpec((TM, LANE), out_cur)]
        out_shape += [jax.ShapeDtypeStruct((rows, D), F32), jax.ShapeDtypeStruct((rows, LANE), F32)]
    else:
        once = dict(pipeline_mode=pl.Buffered(1))
        in_specs += [pl.BlockSpec(w_.shape, const, **once) for w_ in ffn]
        args += list(ffn)
    return pl.pallas_call(
        functools.partial(_mixfin_kernel, nblk=nblk, blk0=blk0, with_router=router is not None),
        grid=(rows // TM,),
        in_specs=in_specs,
        out_specs=out_specs,
        out_shape=out_shape,
        compiler_params=_params(("arbitrary",), VMEM_LIMIT),
        name="mixer_finish",
    )(*args)


def _moe_kernel(be_ref, nu_ref, xs_ref, wg_ref, wu_ref, wd_ref, y_ref, acc_ref):
    b = pl.program_id(0)
    f = pl.program_id(1)

    @pl.when(b < nu_ref[0])
    def _():
        xs = xs_ref[...].astype(BF16)
        part = jnp.zeros((MOE_TM, D), F32)
        for c in range(0, MOE_TF, MOE_SUB):
            g = _mm(xs, wg_ref[0, :, c:c + MOE_SUB])
            u = _mm(xs, wu_ref[0, :, c:c + MOE_SUB])
            part = part + _mm(g * _sigmoid(g) * u, wd_ref[0, c:c + MOE_SUB, :])

        @pl.when(f == 0)
        def _():
            acc_ref[...] = part

        @pl.when(f != 0)
        def _():
            acc_ref[...] += part

        @pl.when(f == pl.num_programs(1) - 1)
        def _():
            y_ref[...] = acc_ref[...]

    @pl.when(b >= nu_ref[0])
    def _():
        y_ref[...] = jnp.zeros(y_ref.shape, F32)


def _moe_experts(xs, blk_e, n_used, wg, wu, wd):
    cap = xs.shape[0]
    nblk = cap // MOE_TM
    nf = D_FF_EXPERT // MOE_TF

    def fidx(b, f, nu):
        return jnp.where(b < nu[0], f, nf - 1)

    grid_spec = pltpu.PrefetchScalarGridSpec(
        num_scalar_prefetch=2,
        grid=(nblk, nf),
        in_specs=[
            pl.BlockSpec((MOE_TM, D), lambda b, f, be, nu: (b, 0)),
            pl.BlockSpec((1, D, MOE_TF), lambda b, f, be, nu: (be[b], 0, fidx(b, f, nu))),
            pl.BlockSpec((1, D, MOE_TF), lambda b, f, be, nu: (be[b], 0, fidx(b, f, nu))),
            pl.BlockSpec((1, MOE_TF, D), lambda b, f, be, nu: (be[b], fidx(b, f, nu), 0)),
        ],
        out_specs=pl.BlockSpec((MOE_TM, D), lambda b, f, be, nu: (b, 0)),
        scratch_shapes=[pltpu.VMEM((MOE_TM, D), F32)],
    )
    return pl.pallas_call(
        _moe_kernel,
        grid_spec=grid_spec,
        out_shape=jax.ShapeDtypeStruct((cap, D), F32),
        compiler_params=_params(("arbitrary", "arbitrary"), VMEM_LIMIT),
        name="moe_experts",
    )(blk_e, n_used, xs, wg, wu, wd)


def _moe_route(top_e):
    n = top_e.shape[0]
    a = n * TOP_K
    flat_e = top_e.reshape(a)
    onehot = (flat_e[:, None] == jnp.arange(N_EXPERTS, dtype=flat_e.dtype)[None, :]).astype(jnp.int32)
    counts = jnp.sum(onehot, axis=0)
    padded = (counts + MOE_TM - 1) // MOE_TM * MOE_TM
    pad_ends = jnp.cumsum(padded)
    pad_starts = pad_ends - padded
    dest = jnp.sum(onehot * (jnp.cumsum(onehot, axis=0) - onehot + pad_starts[None, :]), axis=1)
    cap = a + N_EXPERTS * MOE_TM
    nblk = cap // MOE_TM
    row_tok = (jnp.arange(cap, dtype=jnp.int32) % n).at[dest].set(
        jnp.arange(a, dtype=jnp.int32) // TOP_K, unique_indices=True, mode="promise_in_bounds")
    blk_start = jnp.arange(nblk, dtype=jnp.int32) * MOE_TM
    blk_e = jnp.minimum(jnp.sum((pad_ends[None, :] <= blk_start[:, None]).astype(jnp.int32), axis=1),
                        N_EXPERTS - 1)
    n_used = (pad_ends[-1] // MOE_TM).astype(jnp.int32).reshape(1)
    last_e = blk_e[jnp.maximum(n_used[0] - 1, 0)]
    blk_e = jnp.where(jnp.arange(nblk) < n_used[0], blk_e, last_e)
    return dest.reshape(n, TOP_K), row_tok, blk_e, n_used


def _final_kernel(x_ref, y0_ref, y1_ref, gt_ref, mod_ref, fg_ref, o_ref):
    mod = mod_ref[...]
    gt = gt_ref[...]
    f = gt[:, 0:1] * y0_ref[...] + gt[:, 1:2] * y1_ref[...]
    x = x_ref[...] + mod[0:1, 5 * D:6 * D] * f
    o_ref[...] = _rmsnorm_rows(x) * fg_ref[...]


def _moe_combine_final(x, y0, y1, gt, mod, final_g):
    n = x.shape[0]
    row = lambda i: (i, 0)
    const = lambda i: (0, 0)
    return pl.pallas_call(
        _final_kernel,
        grid=(n // TM,),
        in_specs=[pl.BlockSpec((TM, D), row)] * 3
        + [pl.BlockSpec((TM, LANE), row), pl.BlockSpec((8, N_MOD * D), const), pl.BlockSpec((1, D), const)],
        out_specs=pl.BlockSpec((TM, D), row),
        out_shape=jax.ShapeDtypeStruct((n, D), F32),
        compiler_params=_params(("arbitrary",), VMEM_LIMIT),
        name="moe_combine_final",
    )(x, y0, y1, gt, mod, final_g.reshape(1, D))


def _rope_tables(n):
    lane = jnp.arange(LANE, dtype=jnp.int32) % HEAD_DIM
    inv = ROPE_BASE ** (-jnp.arange(0, AXIS_DIM, 2, dtype=F32) / AXIS_DIM)
    freq = inv[lane % (AXIS_DIM // 2)]
    row_axis = (lane // AXIS_DIM) == 0
    sign = jnp.where((lane % AXIS_DIM) < AXIS_DIM // 2, -1.0, 1.0)
    reps = ATT_BLOCK // GRID_W

    def tables(count, on_axis):
        ang = jnp.arange(count, dtype=F32)[:, None] * freq[None, :]
        return jnp.where(on_axis, jnp.cos(ang), 0.0), jnp.where(on_axis, jnp.sin(ang) * sign, 0.0)

    by_block = lambda t: jnp.pad(t.reshape(-1, reps, LANE), ((0, 0), (0, 8 - reps), (0, 0)))
    cosr, sinr = tables(n // GRID_W, row_axis)
    cosc, sinc = tables(GRID_W, ~row_axis)
    return by_block(cosr), by_block(sinr), cosc, sinc


def _prep_w_in(w):
    pad = jnp.zeros((D, LANE - N_AB), w.dtype)
    return jnp.concatenate([w[:, :_C_Z], w[:, _C_A:_C_Q], pad, w[:, _C_Z:_C_A], w[:, _C_Q:_C_END]],
                           axis=1).astype(BF16)


def kernel(x, c, ctx, c_ctx, w_mod, b_mod, w_in, w_out, conv_w, dn_conv_w, dn_a_log, dn_dt_bias, dn_norm_g,
           attn_sink, ffn_w_gate, ffn_w_up, ffn_w_down, moe_router, moe_w_gate, moe_w_up, moe_w_down,
           final_norm_g):
    bsz, n, d = x.shape
    ctx_len = ctx.shape[1]
    depth = w_in.shape[0]
    assert bsz == 1 and d == D and ctx_len == TM and n % TM == 0 and n % GRID_W == 0
    rope = _rope_tables(n)
    mods = _mod_vectors(c, c_ctx, w_mod, b_mod)
    bd = _head_blockdiag(DN_DIM, DN_HEAD_DIM)
    stream = (ctx[0], x[0], 0)
    for layer in range(depth):
        last = layer == depth - 1
        mod = mods[layer]
        pconv, pz, pq, pkv, qn, kn, vv, gb = _in_proj(*stream, mod, _prep_w_in(w_in[layer]), dn_conv_w[layer],
                                                      dn_a_log[layer], dn_dt_bias[layer], bd)
        to_bf16 = []
        if layer % 2 == 0:
            to_bf16 += [w[layer // 2] for w in (ffn_w_gate, ffn_w_up, ffn_w_down)]
        if (layer + 1) % 2 == 1 and layer + 1 < depth:
            j = (layer + 1) // 2
            to_bf16 += [w[j].reshape(-1, w.shape[-1]) for w in (moe_w_gate, moe_w_up, moe_w_down)]
        o_f, o_b, *cast = _delta_net(qn, kn, vv, gb, ctx_len, tuple(to_bf16))
        if layer % 2 == 0:
            ffn_bf16, cast = cast[:3], cast[3:]
        if cast:
            moe_bf16 = [c.reshape(w.shape[1:]) for c, w in zip(cast, (moe_w_gate, moe_w_up, moe_w_down))]
        sink = jnp.zeros((8, LANE), F32).at[0, :ATT_HEADS].set(attn_sink[layer])
        yc = _attention(pq, pkv, rope, sink, ctx_len, with_ctx=not last)
        router = None
        if layer % 2 == 1:
            wr = jnp.zeros((D, LANE), F32).at[:, :N_EXPERTS].set(moe_router[layer // 2])
            wr1 = wr.astype(BF16)
            router = (wr1, (wr - wr1.astype(F32)).astype(BF16))
        outs = _mixer_finish(*stream, mod, pconv, conv_w[layer], o_f, o_b, pz, dn_norm_g[layer], yc,
                             w_out[layer].astype(BF16), bd, ctx_len, with_ctx=not last, router=router,
                             ffn=ffn_bf16 if layer % 2 == 0 else None)
        if layer % 2 == 0:
            assert not last
            h, = outs
            stream = (h, h, ctx_len // TM)
        else:
            assert last
            x1, hx, route = outs
            i = layer // 2
            dest, row_tok, blk_e, n_used = _moe_route(route[:, 2:2 + TOP_K].astype(jnp.int32))
            take = lambda rows_, idx: rows_.at[idx].get(mode="promise_in_bounds")
            y = _moe_experts(take(hx, row_tok), blk_e, n_used, *moe_bf16)
            h = _moe_combine_final(x1, take(y, dest[:, 0]), take(y, dest[:, 1]), route, mod, final_norm_g)
    return h.reshape(bsz, n, d)
```

```python
import functools

import jax
import jax.numpy as jnp
from jax import lax
from jax.experimental import pallas as pl
from jax.experimental.pallas import tpu as pltpu

F32 = jnp.float32
BF16 = jnp.bfloat16

D = 1024
N_MOD = 6
EPS = 1e-6
NEG = -1e30
GRID_W = 64

CONV_CH = 256
DN_HEADS = 6
DN_HEAD_DIM = 64
DN_DIM = DN_HEADS * DN_HEAD_DIM
DN_CHUNK = 64
DN_SUB = 16
DN_STEP_CHUNKS = 4
ATT_HEADS = 6
ATT_KV_HEADS = 2
ATT_GROUP = ATT_HEADS // ATT_KV_HEADS
HEAD_DIM = 64
ATT_DIM = ATT_HEADS * HEAD_DIM
ATT_KV_DIM = ATT_KV_HEADS * HEAD_DIM
ATT_BLOCK = 128
ROPE_BASE = 10000.0
AXIS_DIM = HEAD_DIM // 2
MIX_DIM = CONV_CH + DN_DIM + ATT_DIM

D_FF = 2816
N_EXPERTS = 8
TOP_K = 2
D_FF_EXPERT = 3584

TM = 256
FF_CHUNK = 512
MOE_TM = 512
MOE_TF = 1792
MOE_SUB = 256
LANE = 128
VMEM_LIMIT = 56 * 1024 * 1024

_C_QKV = 3 * CONV_CH
_C_Z = _C_QKV + 3 * DN_DIM
_C_A = _C_Z + DN_DIM
_C_Q = _C_A + 4 * DN_HEADS
_C_K = _C_Q + ATT_DIM
_C_V = _C_K + ATT_KV_DIM
_C_END = _C_V + ATT_KV_DIM
N_AB = 4 * DN_HEADS


def _params(sem=None, vmem=None):
    kw = {}
    if sem is not None:
        kw["dimension_semantics"] = sem
    if vmem is not None:
        kw["vmem_limit_bytes"] = vmem
    return pltpu.CompilerParams(**kw)


def _split2(a):
    hi = a.astype(BF16)
    lo = (a - hi.astype(F32)).astype(BF16)
    return hi, lo


def _split3(a):
    hi = a.astype(BF16)
    r = a - hi.astype(F32)
    mid = r.astype(BF16)
    lo = (r - mid.astype(F32)).astype(BF16)
    return hi, mid, lo


_NN = (((1,), (0,)), ((), ()))
_NT = (((1,), (1,)), ((), ()))
_TN = (((0,), (0,)), ((), ()))


def _mm(a, b, dims=_NN):
    return lax.dot_general(a.astype(BF16), b.astype(BF16), dims, preferred_element_type=F32)


def _mm3(a, b, dims=_NN):
    a1, a0 = _split2(a)
    b1, b0 = _split2(b)
    d = functools.partial(lax.dot_general, dimension_numbers=dims, preferred_element_type=F32)
    return d(a1, b1) + (d(a1, b0) + d(a0, b1))


_BNN = (((2,), (1,)), ((0,), (0,)))
_BNT = (((2,), (2,)), ((0,), (0,)))
_BTN = (((1,), (1,)), ((0,), (0,)))


def _bmm(a, b, dims=_BNN):
    return lax.dot_general(a.astype(BF16), b.astype(BF16), dims, preferred_element_type=F32)


def _sigmoid(x):
    return 1.0 / (1.0 + jnp.exp(-x))


def _softplus(x):
    return jnp.maximum(x, 0.0) + jnp.log1p(jnp.exp(-jnp.abs(x)))


def _mod_row(mod_ref, is_ctx):
    mod = mod_ref[...]
    return jnp.where(is_ctx, mod[1:2, :], mod[0:1, :])


def _rmsnorm_rows(x):
    return x * lax.rsqrt(jnp.mean(x * x, axis=-1, keepdims=True) + EPS)


def _shift_rows(u, prow, nrow):
    n = u.shape[0]
    rid = lax.broadcasted_iota(jnp.int32, u.shape, 0)
    up = jnp.where(rid == 0, prow, pltpu.roll(u, 1, 0))
    un = jnp.where(rid == n - 1, nrow, pltpu.roll(u, n - 1, 0))
    return up, un


def _same_group(shape, group):
    sh = group.bit_length() - 1
    assert 1 << sh == group
    return (lax.broadcasted_iota(jnp.int32, shape, 0) >> sh) == (lax.broadcasted_iota(jnp.int32, shape, 1) >> sh)


def _head_blockdiag(n, group):
    g = jnp.arange(n, dtype=jnp.int32) // group
    return (g[:, None] == g[None, :]).astype(BF16)


def _group_sum(t, bd):
    return _mm(t, bd)


MOD_TN = 1536


def _mod_kernel(s_ref, w_ref, b_ref, o_ref):
    s = s_ref[...]
    s = s * _sigmoid(s)
    o_ref[0] = _mm3(s, w_ref[0]) + b_ref[0]


def _mod_vectors(c, c_ctx, w_mod, b_mod):
    depth = w_mod.shape[0]
    s = jnp.zeros((8, D), F32).at[0].set(c[0]).at[1].set(c_ctx)
    return pl.pallas_call(
        _mod_kernel,
        grid=(depth, N_MOD * D // MOD_TN),
        in_specs=[
            pl.BlockSpec((8, D), lambda l, j: (0, 0)),
            pl.BlockSpec((1, D, MOD_TN), lambda l, j: (l, 0, j)),
            pl.BlockSpec((1, 1, MOD_TN), lambda l, j: (l, 0, j)),
        ],
        out_specs=pl.BlockSpec((1, 8, MOD_TN), lambda l, j: (l, 0, j)),
        out_shape=jax.ShapeDtypeStruct((depth, 8, N_MOD * D), F32),
        compiler_params=_params(("arbitrary", "arbitrary"), VMEM_LIMIT),
        name="mod_vectors",
    )(s, w_mod, b_mod.reshape(depth, 1, N_MOD * D))


def _halo_valid(i, nblk):
    return jnp.logical_and(i != 0, i != 1), jnp.logical_and(i != 0, i != nblk - 1)


def _halo_rows(prev_ref, next_ref, i, nblk):
    pvalid, nvalid = _halo_valid(i, nblk)
    prow = jnp.where(pvalid, prev_ref[7:8, :], 0.0)
    nrow = jnp.where(nvalid, next_ref[0:1, :], 0.0)
    return prow, nrow


def _in_kernel(ctx_ref, h_ref, hprev_ref, hnext_ref, mod_ref, w_ref, cw_ref, alog_ref, dtb_ref, bd_ref,
               pconv_ref, pz_ref, pq_ref, pkv_ref, q_ref, k_ref, v_ref, gb_ref, *, nblk):
    i = pl.program_id(0)
    row = _mod_row(mod_ref, i == 0)
    norm_mod = lambda x: _rmsnorm_rows(x) * (1.0 + row[:, D:2 * D]) + row[:, 0:D]
    h1 = norm_mod(jnp.where(i == 0, ctx_ref[...], h_ref[...])).astype(BF16)
    halo = norm_mod(jnp.concatenate([hprev_ref[...], hnext_ref[...]], axis=0)).astype(BF16)
    d = functools.partial(lax.dot_general, dimension_numbers=_NN, preferred_element_type=F32)
    tm = h1.shape[0]
    c0 = 3 * CONV_CH
    c1 = c0 + 3 * DN_DIM
    c2 = c1 + DN_DIM
    c3 = c2 + LANE + ATT_DIM
    pconv_ref[...] = d(h1, w_ref[:, 0:c0])
    abq = d(h1, w_ref[:, c2:c3])
    ab = abq[:, 0:LANE]
    pq_ref[...] = abq[:, LANE:]
    pkv_ref[...] = d(h1, w_ref[:, c3:c3 + 2 * ATT_KV_DIM])
    qkvz = d(jnp.concatenate([h1, halo], axis=0), w_ref[:, c0:c2])
    qkv = qkvz[:, 0:3 * DN_DIM]
    pz_ref[...] = qkvz[0:tm, 3 * DN_DIM:]

    u = qkv[0:tm, :]
    pvalid, nvalid = _halo_valid(i, nblk)
    prow = jnp.where(pvalid, qkv[tm + 7:tm + 8, :], 0.0)
    nrow = jnp.where(nvalid, qkv[tm + 8:tm + 9, :], 0.0)
    up, un = _shift_rows(u, prow, nrow)
    cw = cw_ref[...]
    y = up * cw[0:1, :] + u * cw[1:2, :] + un * cw[2:3, :]
    y = y * _sigmoid(y)
    q = y[:, 0:DN_DIM]
    k = y[:, DN_DIM:2 * DN_DIM]
    bd = bd_ref[...]
    q_ref[...] = q * lax.rsqrt(_group_sum(q * q, bd) + 1e-6) * (DN_HEAD_DIM ** -0.5)
    k_ref[...] = k * lax.rsqrt(_group_sum(k * k, bd) + 1e-6)
    v_ref[...] = y[:, 2 * DN_DIM:3 * DN_DIM]
    g = -jnp.exp(alog_ref[...]) * _softplus(ab + dtb_ref[...])
    lane = lax.broadcasted_iota(jnp.int32, ab.shape, 1)
    gb_ref[...] = jnp.where(lane < 2 * DN_HEADS, g, _sigmoid(ab))


def _in_proj(ctx_src, lat_src, lat_blk0, mod, w_main, dn_conv_w, a_log, dt_bias, bd):
    r8 = TM // 8
    last8 = lat_src.shape[0] // 8 - 1
    nblk = 1 + lat_src.shape[0] // TM - lat_blk0
    t = nblk * TM
    alog = jnp.zeros((1, LANE), F32).at[0, :2 * DN_HEADS].set(a_log.reshape(-1))
    dtb = jnp.zeros((1, LANE), F32).at[0, :2 * DN_HEADS].set(dt_bias.reshape(-1))
    widths = (3 * CONV_CH, DN_DIM, ATT_DIM, 2 * ATT_KV_DIM, DN_DIM, DN_DIM, DN_DIM, LANE)
    const = lambda i: (0, 0)
    lat = lambda i: jnp.maximum(i - 1, 0) + lat_blk0
    return pl.pallas_call(
        functools.partial(_in_kernel, nblk=nblk),
        grid=(nblk,),
        in_specs=[
            pl.BlockSpec((TM, D), const),
            pl.BlockSpec((TM, D), lambda i: (lat(i), 0)),
            pl.BlockSpec((8, D), lambda i: (jnp.maximum(lat(i) * r8 - 1, 0), 0)),
            pl.BlockSpec((8, D), lambda i: (jnp.minimum((lat(i) + 1) * r8, last8), 0)),
            pl.BlockSpec((8, N_MOD * D), const),
            pl.BlockSpec(w_main.shape, const),
            pl.BlockSpec((3, 3 * DN_DIM), const),
            pl.BlockSpec((1, LANE), const),
            pl.BlockSpec((1, LANE), const),
            pl.BlockSpec(bd.shape, const),
        ],
        out_specs=[pl.BlockSpec((TM, w), lambda i: (i, 0)) for w in widths],
        out_shape=[jax.ShapeDtypeStruct((t, w), F32) for w in widths],
        compiler_params=_params(("arbitrary",), VMEM_LIMIT),
        name="in_proj",
    )(ctx_src, lat_src, lat_src, lat_src, mod, w_main, dn_conv_w, alog, dtb, bd)


def _dn_block(fwd_refs, bwd_refs, of_ref, ob_ref, s_ref):
    c_ = DN_CHUNK
    nh = DN_HEADS
    nchunks = fwd_refs[0].shape[0] // c_
    nb = nchunks * nh
    rows = lambda g: slice(g * c_, (g + 1) * c_)
    lanes = lambda h: slice(h * DN_HEAD_DIM, (h + 1) * DN_HEAD_DIM)
    ri = lax.broadcasted_iota(jnp.int32, (c_, c_), 0)
    ci = lax.broadcasted_iota(jnp.int32, (c_, c_), 1)
    same_sub = _same_group((c_, c_), DN_SUB)
    eye = jnp.where(ri == ci, 1.0, 0.0)
    shape3 = (2 * nb, c_, c_)
    delta = lax.broadcasted_iota(jnp.int32, shape3, 1) - lax.broadcasted_iota(jnp.int32, shape3, 2)
    delta = jnp.where(lax.broadcasted_iota(jnp.int32, shape3, 0) >= nb, -delta, delta)
    incl = delta >= 0
    strict = delta > 0

    def stack(fn):
        return jnp.stack([fn(d, refs, g, h) for d, refs in enumerate((fwd_refs, bwd_refs))
                          for g in range(nchunks) for h in range(nh)])

    gbs = [refs[3][...] for refs in (fwd_refs, bwd_refs)]
    tris = [jnp.where(ri >= ci, 1.0, 0.0).astype(BF16), jnp.where(ri <= ci, 1.0, 0.0).astype(BF16)]
    gcs = [[_cumsum_rows(tris[d], gbs[d][rows(g), :]) for g in range(nchunks)] for d in range(2)]
    gcts = [[gc.T for gc in gcs[d]] for d in range(2)]
    q = stack(lambda d, refs, g, h: refs[0][rows(g), lanes(h)])
    k = stack(lambda d, refs, g, h: refs[1][rows(g), lanes(h)])
    v = stack(lambda d, refs, g, h: refs[2][rows(g), lanes(h)])
    col = lambda d, h: d * nh + h
    gcol = stack(lambda d, refs, g, h: gcs[d][g][:, col(d, h):col(d, h) + 1])
    grow = stack(lambda d, refs, g, h: gcts[d][g][col(d, h):col(d, h) + 1, :])
    beta = stack(lambda d, refs, g, h: gbs[d][rows(g), 2 * nh + col(d, h):2 * nh + col(d, h) + 1])
    glast = jnp.concatenate([gcol[:nb, c_ - 1:c_, :], gcol[nb:, 0:1, :]], axis=0)
    decay = jnp.where(incl, jnp.exp(jnp.where(incl, gcol - grow, 0.0)), 0.0)
    eg = jnp.exp(gcol)
    kb = k * beta
    a = jnp.where(strict, _bmm(kb, k, _BNT) * decay, 0.0)
    qk = jnp.where(incl, _bmm(q, k, _BNT) * decay, 0.0)
    ad = jnp.where(same_sub, a, 0.0)
    ao = a - ad
    p = eye - ad
    n2 = _bmm(ad, ad)
    p = p + _bmm(p, n2)
    n4 = _bmm(n2, n2)
    p = p + _bmm(p, n4)
    n8 = _bmm(n4, n4)
    dinv = p + _bmm(p, n8)
    m = _bmm(dinv, ao)
    m2 = _bmm(m, m)
    y = _bmm(dinv, jnp.concatenate([v * beta, kb * eg], axis=-1))
    z = y + _bmm(m2, y)
    x = z - _bmm(m, z)
    u = x[:, :, :DN_HEAD_DIM]
    w = x[:, :, DN_HEAD_DIM:]
    qg = q * eg
    kd = k * jnp.exp(glast - gcol)
    gl = jnp.exp(glast)
    s = s_ref[...]
    for t in range(nchunks):
        gf, gr = t, nchunks - 1 - t
        step = lambda a: jnp.concatenate([a[gf * nh:(gf + 1) * nh], a[nb + gr * nh:nb + (gr + 1) * nh]], axis=0)
        v_new = step(u) - _bmm(step(w), s)
        o = _bmm(step(qg), s) + _bmm(step(qk), v_new)
        s = s * step(gl) + _bmm(step(kd), v_new, _BTN)
        for h in range(nh):
            of_ref[rows(gf), lanes(h)] = o[h]
            ob_ref[rows(gr), lanes(h)] = o[nh + h]
    s_ref[...] = s


def _cumsum_rows(tri_bf16, g):
    g2, g1, g0 = _split3(g)
    d = functools.partial(lax.dot_general, dimension_numbers=_NN, preferred_element_type=F32)
    return d(tri_bf16, g2) + (d(tri_bf16, g1) + d(tri_bf16, g0))


def _dn_kernel(qf, kf, vf, gf, qb, kb, vb, gbb, *rest, n_cast):
    cast_in = rest[:n_cast]
    of_ref, ob_ref = rest[n_cast:n_cast + 2]
    cast_out = rest[n_cast + 2:2 * n_cast + 2]
    s_ref = rest[-1]

    @pl.when(pl.program_id(0) == 0)
    def _():
        s_ref[...] = jnp.zeros(s_ref.shape, F32)

    _dn_block((qf, kf, vf, gf), (qb, kb, vb, gbb), of_ref, ob_ref, s_ref)
    for src, dst in zip(cast_in, cast_out):
        dst[...] = src[...].astype(BF16)


def _delta_net(q, k, v, gb, ctx_len, to_bf16=()):
    t = q.shape[0]
    rows = DN_STEP_CHUNKS * DN_CHUNK
    assert ctx_len == rows and t % rows == 0
    nstep = t // rows
    cast_specs = []
    for m in to_bf16:
        rb = -(-m.shape[0] // nstep)
        rb = -(-rb // 16) * 16
        nb = -(-m.shape[0] // rb)
        cast_specs.append(pl.BlockSpec((rb, m.shape[1]), lambda s, nb=nb: (jnp.minimum(s, nb - 1), 0)))

    def fwd(s):
        return (s, 0)

    def bwd(s):
        return (jnp.where(s == 0, 0, nstep - s), 0)

    wide = lambda im: pl.BlockSpec((rows, DN_DIM), im)
    narrow = lambda im: pl.BlockSpec((rows, LANE), im)
    return pl.pallas_call(
        functools.partial(_dn_kernel, n_cast=len(to_bf16)),
        grid=(nstep,),
        in_specs=[wide(fwd), wide(fwd), wide(fwd), narrow(fwd), wide(bwd), wide(bwd), wide(bwd), narrow(bwd)]
        + cast_specs,
        out_specs=[wide(fwd), wide(bwd)] + cast_specs,
        out_shape=[jax.ShapeDtypeStruct((t, DN_DIM), F32)] * 2
        + [jax.ShapeDtypeStruct(m.shape, BF16) for m in to_bf16],
        scratch_shapes=[pltpu.VMEM((2 * DN_HEADS, DN_HEAD_DIM, DN_HEAD_DIM), F32)],
        compiler_params=_params(("arbitrary",), VMEM_LIMIT),
        name="delta_net",
    )(q, k, v, gb, q, k, v, gb, *to_bf16)


def _rope(x, cos, sin):
    w = x.shape[1]
    lane = lax.broadcasted_iota(jnp.int32, x.shape, 1)
    first_half = (lane & (AXIS_DIM - 1)) < (AXIS_DIM // 2)
    swapped = jnp.where(first_half, pltpu.roll(x, w - AXIS_DIM // 2, 1), pltpu.roll(x, AXIS_DIM // 2, 1))
    return x * cos + swapped * sin


LOG2E = 1.4426950408889634


def _softmax_av(s, sink, vals):
    m = jnp.maximum(jnp.max(s, axis=-1, keepdims=True), sink)
    p = jnp.exp2(s - m)
    denom = jnp.sum(p, axis=-1, keepdims=True) + jnp.exp2(sink - m)
    return _mm(p, vals) / denom


def _attend(q, keys, vals, band, sink_all, o_ref, row0=0):
    b = q.shape[0]
    for kvh in range(ATT_KV_HEADS):
        kl = slice(kvh * HEAD_DIM, (kvh + 1) * HEAD_DIM)
        heads = range(kvh * ATT_GROUP, (kvh + 1) * ATT_GROUP)
        qs = jnp.concatenate([q[:, h * HEAD_DIM:(h + 1) * HEAD_DIM] for h in heads], axis=0)
        sink = jnp.concatenate([jnp.broadcast_to(sink_all[0:1, h:h + 1], (b, 1)) for h in heads], axis=0)
        s = _mm(qs, keys[:, kl], _NT)
        if band is not None:
            kb = ATT_BLOCK
            s = jnp.concatenate([jnp.where(band[0], s[:, 0:kb], NEG), s[:, kb:2 * kb],
                                 jnp.where(band[1], s[:, 2 * kb:3 * kb], NEG), s[:, 3 * kb:]], axis=1)
        o = _softmax_av(s, sink * LOG2E, vals[:, kl])
        for g, h in enumerate(heads):
            o_ref[row0:row0 + b, h * HEAD_DIM:(h + 1) * HEAD_DIM] = o[g * b:(g + 1) * b, :]


def _band_valid(first, last):
    b = ATT_BLOCK
    c = lax.broadcasted_iota(jnp.int32, (1, b), 1)
    r = lax.broadcasted_iota(jnp.int32, (ATT_GROUP * b, 1), 0) & (b - 1)
    prev_ok = jnp.where(first, -1, c) >= r
    next_ok = jnp.where(last, b, c) <= r
    return prev_ok, next_ok


def _rope_block(rowtab_ref, coltab_ref, blk):
    rt = rowtab_ref[blk]
    ct = coltab_ref[...]
    reps = ATT_BLOCK // GRID_W
    rows = jnp.concatenate([jnp.broadcast_to(rt[g:g + 1, :], (GRID_W, LANE)) for g in range(reps)], axis=0)
    return rows + jnp.concatenate([ct] * reps, axis=0)


def _attn_kernel(q_ref, kp_ref, kc_ref, kn_ref, kctx_ref, cosr_ref, sinr_ref, cosc_ref, sinc_ref, sink_ref, o_ref,
                 *, nb, with_ctx):
    j = pl.program_id(0)
    b = ATT_BLOCK
    scale = HEAD_DIM ** -0.5 * LOG2E

    @pl.when(j == 0)
    def _():
        if with_ctx:
            kvx = kctx_ref[...]
            _attend(q_ref[...] * scale, kvx[:, :ATT_KV_DIM], kvx[:, ATT_KV_DIM:], None, sink_ref[...], o_ref)
        else:
            o_ref[...] = jnp.zeros(o_ref.shape, F32)

    @pl.when(j > 0)
    def _():
        b0 = 2 * (j - 1)
        blocks = (jnp.maximum(b0 - 1, 0), b0, b0 + 1, jnp.minimum(b0 + 2, nb - 1))
        cos = [_rope_block(cosr_ref, cosc_ref, blk) for blk in blocks]
        sin = [_rope_block(sinr_ref, sinc_ref, blk) for blk in blocks]
        kvc = kc_ref[...]
        kv = (kp_ref[...], kvc[0:b, :], kvc[b:2 * b, :], kn_ref[...])
        kvx = kctx_ref[...]
        keys = [_rope(t[:, :ATT_KV_DIM], c_, s_) for t, c_, s_ in zip(kv, cos, sin)]
        q_all = q_ref[...]
        sink_all = sink_ref[...]
        for sub in range(2):
            q = q_all[sub * b:(sub + 1) * b, :]
            q = jnp.concatenate([_rope(q[:, l * LANE:(l + 1) * LANE], cos[1 + sub], sin[1 + sub])
                                 for l in range(ATT_DIM // LANE)], axis=1)
            kcat = jnp.concatenate(keys[sub:sub + 3] + [kvx[:, :ATT_KV_DIM]], axis=0)
            vcat = jnp.concatenate([t[:, ATT_KV_DIM:] for t in kv[sub:sub + 3]] + [kvx[:, ATT_KV_DIM:]], axis=0)
            band = _band_valid(first=(b0 + sub == 0), last=(b0 + sub == nb - 1))
            _attend(q * scale, kcat, vcat, band, sink_all, o_ref, row0=sub * b)


def _attention(pq, pkv, rope, sink, ctx_len, with_ctx):
    t = pq.shape[0]
    n = t - ctx_len
    nb = n // ATT_BLOCK
    step = 2 * ATT_BLOCK
    assert ctx_len == step and nb % 2 == 0
    off = ctx_len // ATT_BLOCK
    half = lambda im: pl.BlockSpec((ATT_BLOCK, 2 * ATT_KV_DIM), im)
    lat = lambda j: 2 * jnp.maximum(j - 1, 0)
    prv = lambda j: (jnp.maximum(lat(j) - 1, 0) + off, 0)
    nxt = lambda j: (jnp.minimum(lat(j) + 2, nb - 1) + off, 0)
    cur = lambda j: (j, 0)
    whole = lambda a: pl.BlockSpec(a.shape, lambda j: (0,) * a.ndim)
    return pl.pallas_call(
        functools.partial(_attn_kernel, nb=nb, with_ctx=with_ctx),
        grid=(1 + nb // 2,),
        in_specs=[
            pl.BlockSpec((step, ATT_DIM), cur),
            half(prv), pl.BlockSpec((step, 2 * ATT_KV_DIM), cur), half(nxt),
            pl.BlockSpec((ctx_len, 2 * ATT_KV_DIM), lambda j: (0, 0)),
        ] + [whole(a) for a in rope] + [pl.BlockSpec((8, LANE), lambda j: (0, 0))],
        out_specs=pl.BlockSpec((step, ATT_DIM), cur),
        out_shape=jax.ShapeDtypeStruct((t, ATT_DIM), F32),
        compiler_params=_params(("arbitrary",), VMEM_LIMIT),
        name="attention",
    )(pq, pkv, pkv, pkv, pkv, *rope, sink)


def _mixfin_kernel(ctx_ref, h_ref, mod_ref, pconv_ref, prev_ref, next_ref, cw_ref, of_ref, ob_ref, z_ref, ng_ref,
                   yc_ref, wout_ref, bd_ref, *rest, nblk, blk0, with_router):
    if with_router:
        wr1_ref, wr0_ref, x_ref, hx_ref, lg_ref = rest
    else:
        wg_ref, wu_ref, wd_ref, x_ref = rest
    i = pl.program_id(0) + blk0
    row = _mod_row(mod_ref, i == 0)
    pc = pconv_ref[...]
    u = pc[:, CONV_CH:2 * CONV_CH] * pc[:, 2 * CONV_CH:]
    prow, nrow = _halo_rows(prev_ref, next_ref, i, nblk)
    prow = prow[:, CONV_CH:2 * CONV_CH] * prow[:, 2 * CONV_CH:]
    nrow = nrow[:, CONV_CH:2 * CONV_CH] * nrow[:, 2 * CONV_CH:]
    up, un = _shift_rows(u, prow, nrow)
    cw = cw_ref[...]
    ya = pc[:, :CONV_CH] * (up * cw[0:1, :] + u * cw[1:2, :] + un * cw[2:3, :])
    o = of_ref[...] + ob_ref[...]
    ms = _group_sum(o * o, bd_ref[...]) * (1.0 / DN_HEAD_DIM)
    z = z_ref[...]
    yb = o * lax.rsqrt(ms + EPS) * ng_ref[...] * (z * _sigmoid(z))
    mix = jnp.concatenate([ya, yb, yc_ref[...]], axis=1)
    x = jnp.where(i == 0, ctx_ref[...], h_ref[...]) + row[:, 2 * D:3 * D] * _mm(mix, wout_ref[...])
    hx = _rmsnorm_rows(x) * (1.0 + row[:, 4 * D:5 * D]) + row[:, 3 * D:4 * D]
    if not with_router:
        hb = hx.astype(BF16)
        acc = jnp.zeros((hb.shape[0], D), F32)
        for f in range(0, D_FF, FF_CHUNK):
            fe = min(f + FF_CHUNK, D_FF)
            g = _mm(hb, wg_ref[:, f:fe])
            u_ = _mm(hb, wu_ref[:, f:fe])
            acc = acc + _mm(g * _sigmoid(g) * u_, wd_ref[f:fe, :])
        x_ref[...] = x + row[:, 5 * D:6 * D] * acc
    else:
        x_ref[...] = x
        hx_ref[...] = hx
        h1, h0 = _split2(hx)
        d = functools.partial(lax.dot_general, dimension_numbers=_NN, preferred_element_type=F32)
        lg = d(h1, wr1_ref[...]) + (d(h0, wr1_ref[...]) + d(h1, wr0_ref[...]))
        lane = lax.broadcasted_iota(jnp.int32, lg.shape, 1)
        lanef = lane.astype(F32)
        lg = jnp.where(lane < N_EXPERTS, lg, -jnp.inf)
        m1 = jnp.max(lg, axis=-1, keepdims=True)
        i1 = jnp.min(jnp.where(lg == m1, lanef, float(LANE)), axis=-1, keepdims=True)
        rest = jnp.where(lanef == i1, -jnp.inf, lg)
        m2 = jnp.max(rest, axis=-1, keepdims=True)
        i2 = jnp.min(jnp.where(rest == m2, lanef, float(LANE)), axis=-1, keepdims=True)
        e2 = jnp.exp(m2 - m1)
        g1 = 1.0 / (1.0 + e2)
        lg_ref[...] = jnp.where(lane == 0, g1, jnp.where(lane == 1, e2 * g1, jnp.where(lane == 2, i1, i2)))


def _mixer_finish(ctx_src, lat_src, lat_blk0, mod, pconv, conv_w, o_f, o_b, pz, norm_g, yc, w_out, bd, ctx_len,
                  with_ctx, router=None, ffn=None):
    assert (router is None) != (ffn is None)
    t = pconv.shape[0]
    nblk = t // TM
    blk0 = 0 if with_ctx else ctx_len // TM
    rows = t - blk0 * TM
    r8 = TM // 8
    w = pconv.shape[1]
    cur = lambda i: (i + blk0, 0)
    out_cur = lambda i: (i, 0)
    const = lambda i: (0, 0)
    ng = jnp.tile(norm_g.reshape(1, DN_HEAD_DIM), (1, DN_HEADS))
    in_specs = [
        pl.BlockSpec((TM, D), const),
        pl.BlockSpec((TM, D), lambda i: (jnp.maximum(i + blk0 - 1, 0) + lat_blk0, 0)),
        pl.BlockSpec((8, N_MOD * D), const),
        pl.BlockSpec((TM, w), cur),
        pl.BlockSpec((8, w), lambda i: (jnp.maximum((i + blk0) * r8 - 1, 0), 0)),
        pl.BlockSpec((8, w), lambda i: (jnp.minimum((i + blk0 + 1) * r8, t // 8 - 1), 0)),
        pl.BlockSpec((3, CONV_CH), const),
        pl.BlockSpec((TM, DN_DIM), cur),
        pl.BlockSpec((TM, DN_DIM), cur),
        pl.BlockSpec((TM, DN_DIM), cur),
        pl.BlockSpec((1, DN_DIM), const),
        pl.BlockSpec((TM, ATT_DIM), cur),
        pl.BlockSpec((MIX_DIM, D), const),
        pl.BlockSpec(bd.shape, const),
    ]
    args = [ctx_src, lat_src, mod, pconv, pconv, pconv, conv_w, o_f, o_b, pz, ng, yc, w_out, bd]
    out_specs = [pl.BlockSpec((TM, D), out_cur)]
    out_shape = [jax.ShapeDtypeStruct((rows, D), F32)]
    if router is not None:
        in_specs += [pl.BlockSpec((D, LANE), const)] * 2
        args += list(router)
        out_specs += [pl.BlockSpec((TM, D), out_cur), pl.BlockSpec((TM, LANE), out_cur)]
        out_shape += [jax.ShapeDtypeStruct((rows, D), F32), jax.ShapeDtypeStruct((rows, LANE), F32)]
    else:
        once = dict(pipeline_mode=pl.Buffered(1))
        in_specs += [pl.BlockSpec(w_.shape, const, **once) for w_ in ffn]
        args += list(ffn)
    return pl.pallas_call(
        functools.partial(_mixfin_kernel, nblk=nblk, blk0=blk0, with_router=router is not None),
        grid=(rows // TM,),
        in_specs=in_specs,
        out_specs=out_specs,
        out_shape=out_shape,
        compiler_params=_params(("arbitrary",), VMEM_LIMIT),
        name="mixer_finish",
    )(*args)


def _moe_kernel(be_ref, nu_ref, xs_ref, wg_ref, wu_ref, wd_ref, y_ref, acc_ref):
    b = pl.program_id(0)
    f = pl.program_id(1)

    @pl.when(b < nu_ref[0])
    def _():
        xs = xs_ref[...].astype(BF16)
        part = jnp.zeros((MOE_TM, D), F32)
        for c in range(0, MOE_TF, MOE_SUB):
            g = _mm(xs, wg_ref[0, :, c:c + MOE_SUB])
            u = _mm(xs, wu_ref[0, :, c:c + MOE_SUB])
            part = part + _mm(g * _sigmoid(g) * u, wd_ref[0, c:c + MOE_SUB, :])

        @pl.when(f == 0)
        def _():
            acc_ref[...] = part

        @pl.when(f != 0)
        def _():
            acc_ref[...] += part

        @pl.when(f == pl.num_programs(1) - 1)
        def _():
            y_ref[...] = acc_ref[...]

    @pl.when(b >= nu_ref[0])
    def _():
        y_ref[...] = jnp.zeros(y_ref.shape, F32)


def _moe_experts(xs, blk_e, n_used, wg, wu, wd):
    cap = xs.shape[0]
    nblk = cap // MOE_TM
    nf = D_FF_EXPERT // MOE_TF

    def fidx(b, f, nu):
        return jnp.where(b < nu[0], f, nf - 1)

    grid_spec = pltpu.PrefetchScalarGridSpec(
        num_scalar_prefetch=2,
        grid=(nblk, nf),
        in_specs=[
            pl.BlockSpec((MOE_TM, D), lambda b, f, be, nu: (b, 0)),
            pl.BlockSpec((1, D, MOE_TF), lambda b, f, be, nu: (be[b], 0, fidx(b, f, nu))),
            pl.BlockSpec((1, D, MOE_TF), lambda b, f, be, nu: (be[b], 0, fidx(b, f, nu))),
            pl.BlockSpec((1, MOE_TF, D), lambda b, f, be, nu: (be[b], fidx(b, f, nu), 0)),
        ],
        out_specs=pl.BlockSpec((MOE_TM, D), lambda b, f, be, nu: (b, 0)),
        scratch_shapes=[pltpu.VMEM((MOE_TM, D), F32)],
    )
    return pl.pallas_call(
        _moe_kernel,
        grid_spec=grid_spec,
        out_shape=jax.ShapeDtypeStruct((cap, D), F32),
        compiler_params=_params(("arbitrary", "arbitrary"), VMEM_LIMIT),
        name="moe_experts",
    )(blk_e, n_used, xs, wg, wu, wd)


def _moe_route(choices):
    n = choices[0].shape[0]
    a = n * TOP_K
    flat_e = jnp.concatenate(choices)
    onehot = (flat_e[:, None] == jnp.arange(N_EXPERTS, dtype=flat_e.dtype)[None, :]).astype(jnp.int32)
    counts = jnp.sum(onehot, axis=0)
    padded = (counts + MOE_TM - 1) // MOE_TM * MOE_TM
    pad_ends = jnp.cumsum(padded)
    pad_starts = pad_ends - padded
    dest = jnp.sum(onehot * (jnp.cumsum(onehot, axis=0) - onehot + pad_starts[None, :]), axis=1)
    cap = a + N_EXPERTS * MOE_TM
    nblk = cap // MOE_TM
    row_tok = (jnp.arange(cap, dtype=jnp.int32) % n).at[dest].set(
        jnp.arange(a, dtype=jnp.int32) % n, unique_indices=True, mode="promise_in_bounds")
    blk_start = jnp.arange(nblk, dtype=jnp.int32) * MOE_TM
    blk_e = jnp.minimum(jnp.sum((pad_ends[None, :] <= blk_start[:, None]).astype(jnp.int32), axis=1),
                        N_EXPERTS - 1)
    n_used = (pad_ends[-1] // MOE_TM).astype(jnp.int32).reshape(1)
    last_e = blk_e[jnp.maximum(n_used[0] - 1, 0)]
    blk_e = jnp.where(jnp.arange(nblk) < n_used[0], blk_e, last_e)
    return [dest[s * n:(s + 1) * n] for s in range(TOP_K)], row_tok, blk_e, n_used


def _final_kernel(x_ref, y0_ref, y1_ref, gt_ref, mod_ref, fg_ref, o_ref):
    mod = mod_ref[...]
    gt = gt_ref[...]
    f = gt[:, 0:1] * y0_ref[...] + gt[:, 1:2] * y1_ref[...]
    x = x_ref[...] + mod[0:1, 5 * D:6 * D] * f
    o_ref[...] = _rmsnorm_rows(x) * fg_ref[...]


def _moe_combine_final(x, y0, y1, gt, mod, final_g):
    n = x.shape[0]
    row = lambda i: (i, 0)
    const = lambda i: (0, 0)
    return pl.pallas_call(
        _final_kernel,
        grid=(n // TM,),
        in_specs=[pl.BlockSpec((TM, D), row)] * 3
        + [pl.BlockSpec((TM, LANE), row), pl.BlockSpec((8, N_MOD * D), const), pl.BlockSpec((1, D), const)],
        out_specs=pl.BlockSpec((TM, D), row),
        out_shape=jax.ShapeDtypeStruct((n, D), F32),
        compiler_params=_params(("arbitrary",), VMEM_LIMIT),
        name="moe_combine_final",
    )(x, y0, y1, gt, mod, final_g.reshape(1, D))


def _rope_tables(n):
    lane = jnp.arange(LANE, dtype=jnp.int32) % HEAD_DIM
    inv = ROPE_BASE ** (-jnp.arange(0, AXIS_DIM, 2, dtype=F32) / AXIS_DIM)
    freq = inv[lane % (AXIS_DIM // 2)]
    row_axis = (lane // AXIS_DIM) == 0
    sign = jnp.where((lane % AXIS_DIM) < AXIS_DIM // 2, -1.0, 1.0)
    reps = ATT_BLOCK // GRID_W

    def tables(count, on_axis):
        ang = jnp.arange(count, dtype=F32)[:, None] * freq[None, :]
        return jnp.where(on_axis, jnp.cos(ang), 0.0), jnp.where(on_axis, jnp.sin(ang) * sign, 0.0)

    by_block = lambda t: jnp.pad(t.reshape(-1, reps, LANE), ((0, 0), (0, 8 - reps), (0, 0)))
    cosr, sinr = tables(n // GRID_W, row_axis)
    cosc, sinc = tables(GRID_W, ~row_axis)
    return by_block(cosr), by_block(sinr), cosc, sinc


def _prep_w_in(w):
    pad = jnp.zeros((D, LANE - N_AB), w.dtype)
    return jnp.concatenate([w[:, :_C_Q], pad, w[:, _C_Q:_C_END]], axis=1).astype(BF16)


def kernel(x, c, ctx, c_ctx, w_mod, b_mod, w_in, w_out, conv_w, dn_conv_w, dn_a_log, dn_dt_bias, dn_norm_g,
           attn_sink, ffn_w_gate, ffn_w_up, ffn_w_down, moe_router, moe_w_gate, moe_w_up, moe_w_down,
           final_norm_g):
    bsz, n, d = x.shape
    ctx_len = ctx.shape[1]
    depth = w_in.shape[0]
    assert bsz == 1 and d == D and ctx_len == TM and n % TM == 0 and n % GRID_W == 0
    rope = _rope_tables(n)
    mods = _mod_vectors(c, c_ctx, w_mod, b_mod)
    bd = _head_blockdiag(DN_DIM, DN_HEAD_DIM)
    stream = (ctx[0], x[0], 0)
    for layer in range(depth):
        last = layer == depth - 1
        mod = mods[layer]
        pconv, pz, pq, pkv, qn, kn, vv, gb = _in_proj(*stream, mod, _prep_w_in(w_in[layer]), dn_conv_w[layer],
                                                      dn_a_log[layer], dn_dt_bias[layer], bd)
        to_bf16 = []
        if layer % 2 == 0:
            to_bf16 += [w[layer // 2] for w in (ffn_w_gate, ffn_w_up, ffn_w_down)]
        if (layer + 1) % 2 == 1 and layer + 1 < depth:
            j = (layer + 1) // 2
            to_bf16 += [w[j].reshape(-1, w.shape[-1]) for w in (moe_w_gate, moe_w_up, moe_w_down)]
        o_f, o_b, *cast = _delta_net(qn, kn, vv, gb, ctx_len, tuple(to_bf16))
        if layer % 2 == 0:
            ffn_bf16, cast = cast[:3], cast[3:]
        if cast:
            moe_bf16 = [c.reshape(w.shape[1:]) for c, w in zip(cast, (moe_w_gate, moe_w_up, moe_w_down))]
        sink = jnp.zeros((8, LANE), F32).at[0, :ATT_HEADS].set(attn_sink[layer])
        yc = _attention(pq, pkv, rope, sink, ctx_len, with_ctx=not last)
        router = None
        if layer % 2 == 1:
            wr = jnp.zeros((D, LANE), F32).at[:, :N_EXPERTS].set(moe_router[layer // 2])
            wr1 = wr.astype(BF16)
            router = (wr1, (wr - wr1.astype(F32)).astype(BF16))
        outs = _mixer_finish(*stream, mod, pconv, conv_w[layer], o_f, o_b, pz, dn_norm_g[layer], yc,
                             w_out[layer].astype(BF16), bd, ctx_len, with_ctx=not last, router=router,
                             ffn=ffn_bf16 if layer % 2 == 0 else None)
        if layer % 2 == 0:
            assert not last
            h, = outs
            stream = (h, h, ctx_len // TM)
        else:
            assert last
            x1, hx, route = outs
            dest, row_tok, blk_e, n_used = _moe_route([route[:, 2 + s].astype(jnp.int32) for s in range(TOP_K)])
            take = lambda rows_, idx: rows_.at[idx].get(mode="promise_in_bounds")
            y = _moe_experts(take(hx, row_tok), blk_e, n_used, *moe_bf16)
            h = _moe_combine_final(x1, take(y, dest[0]), take(y, dest[1]), route, mod, final_norm_g)
    return h.reshape(bsz, n, d)
```

```python
import functools

import jax
import jax.numpy as jnp
from jax import lax
from jax.experimental import pallas as pl
from jax.experimental.pallas import tpu as pltpu

F32 = jnp.float32
BF16 = jnp.bfloat16

D = 1024
N_MOD = 6
EPS = 1e-6
NEG = -1e30
GRID_W = 64

CONV_CH = 256
DN_HEADS = 6
DN_HEAD_DIM = 64
DN_DIM = DN_HEADS * DN_HEAD_DIM
DN_CHUNK = 64
DN_SUB = 16
DN_STEP_CHUNKS = 4
ATT_HEADS = 6
ATT_KV_HEADS = 2
ATT_GROUP = ATT_HEADS // ATT_KV_HEADS
HEAD_DIM = 64
ATT_DIM = ATT_HEADS * HEAD_DIM
ATT_KV_DIM = ATT_KV_HEADS * HEAD_DIM
ATT_BLOCK = 128
ROPE_BASE = 10000.0
AXIS_DIM = HEAD_DIM // 2
MIX_DIM = CONV_CH + DN_DIM + ATT_DIM

D_FF = 2816
N_EXPERTS = 8
TOP_K = 2
D_FF_EXPERT = 3584

TM = 256
FF_CHUNK = 512
MOE_TM = 512
MOE_TF = 1792
MOE_SUB = 256
LANE = 128
VMEM_LIMIT = 56 * 1024 * 1024

_C_QKV = 3 * CONV_CH
_C_Z = _C_QKV + 3 * DN_DIM
_C_A = _C_Z + DN_DIM
_C_Q = _C_A + 4 * DN_HEADS
_C_K = _C_Q + ATT_DIM
_C_V = _C_K + ATT_KV_DIM
_C_END = _C_V + ATT_KV_DIM
N_AB = 4 * DN_HEADS


def _params(sem=None, vmem=None):
    kw = {}
    if sem is not None:
        kw["dimension_semantics"] = sem
    if vmem is not None:
        kw["vmem_limit_bytes"] = vmem
    return pltpu.CompilerParams(**kw)


def _split2(a):
    hi = a.astype(BF16)
    lo = (a - hi.astype(F32)).astype(BF16)
    return hi, lo


def _split3(a):
    hi = a.astype(BF16)
    r = a - hi.astype(F32)
    mid = r.astype(BF16)
    lo = (r - mid.astype(F32)).astype(BF16)
    return hi, mid, lo


_NN = (((1,), (0,)), ((), ()))
_NT = (((1,), (1,)), ((), ()))
_TN = (((0,), (0,)), ((), ()))


def _mm(a, b, dims=_NN):
    return lax.dot_general(a.astype(BF16), b.astype(BF16), dims, preferred_element_type=F32)


def _mm3(a, b, dims=_NN):
    a1, a0 = _split2(a)
    b1, b0 = _split2(b)
    d = functools.partial(lax.dot_general, dimension_numbers=dims, preferred_element_type=F32)
    return d(a1, b1) + (d(a1, b0) + d(a0, b1))


_BNN = (((2,), (1,)), ((0,), (0,)))
_BNT = (((2,), (2,)), ((0,), (0,)))
_BTN = (((1,), (1,)), ((0,), (0,)))


def _bmm(a, b, dims=_BNN):
    return lax.dot_general(a.astype(BF16), b.astype(BF16), dims, preferred_element_type=F32)


def _sigmoid(x):
    return 1.0 / (1.0 + jnp.exp(-x))


def _softplus(x):
    return jnp.maximum(x, 0.0) + jnp.log1p(jnp.exp(-jnp.abs(x)))


def _mod_row(mod_ref, is_ctx):
    mod = mod_ref[...]
    return jnp.where(is_ctx, mod[1:2, :], mod[0:1, :])


def _rmsnorm_rows(x):
    return x * lax.rsqrt(jnp.mean(x * x, axis=-1, keepdims=True) + EPS)


def _shift_rows(u, prow, nrow):
    n = u.shape[0]
    rid = lax.broadcasted_iota(jnp.int32, u.shape, 0)
    up = jnp.where(rid == 0, prow, pltpu.roll(u, 1, 0))
    un = jnp.where(rid == n - 1, nrow, pltpu.roll(u, n - 1, 0))
    return up, un


def _same_group(shape, group):
    sh = group.bit_length() - 1
    assert 1 << sh == group
    return (lax.broadcasted_iota(jnp.int32, shape, 0) >> sh) == (lax.broadcasted_iota(jnp.int32, shape, 1) >> sh)


def _head_blockdiag(n, group):
    g = jnp.arange(n, dtype=jnp.int32) // group
    return (g[:, None] == g[None, :]).astype(BF16)


def _group_sum(t, bd):
    return _mm(t, bd)


MOD_TN = 1536


def _mod_kernel(s_ref, w_ref, b_ref, o_ref):
    s = s_ref[...]
    s = s * _sigmoid(s)
    o_ref[0] = _mm3(s, w_ref[0]) + b_ref[0]


def _mod_vectors(c, c_ctx, w_mod, b_mod):
    depth = w_mod.shape[0]
    s = jnp.zeros((8, D), F32).at[0].set(c[0]).at[1].set(c_ctx)
    return pl.pallas_call(
        _mod_kernel,
        grid=(depth, N_MOD * D // MOD_TN),
        in_specs=[
            pl.BlockSpec((8, D), lambda l, j: (0, 0)),
            pl.BlockSpec((1, D, MOD_TN), lambda l, j: (l, 0, j)),
            pl.BlockSpec((1, 1, MOD_TN), lambda l, j: (l, 0, j)),
        ],
        out_specs=pl.BlockSpec((1, 8, MOD_TN), lambda l, j: (l, 0, j)),
        out_shape=jax.ShapeDtypeStruct((depth, 8, N_MOD * D), F32),
        compiler_params=_params(("arbitrary", "arbitrary"), VMEM_LIMIT),
        name="mod_vectors",
    )(s, w_mod, b_mod.reshape(depth, 1, N_MOD * D))


def _halo_valid(i, nblk):
    return jnp.logical_and(i != 0, i != 1), jnp.logical_and(i != 0, i != nblk - 1)


def _halo_rows(prev_ref, next_ref, i, nblk):
    pvalid, nvalid = _halo_valid(i, nblk)
    prow = jnp.where(pvalid, prev_ref[7:8, :], 0.0)
    nrow = jnp.where(nvalid, next_ref[0:1, :], 0.0)
    return prow, nrow


def _in_kernel(ctx_ref, h_ref, hprev_ref, hnext_ref, mod_ref, w_ref, cw_ref, alog_ref, dtb_ref, bd_ref,
               pconv_ref, pz_ref, pq_ref, pkv_ref, q_ref, k_ref, v_ref, gb_ref, *, nblk):
    i = pl.program_id(0)
    row = _mod_row(mod_ref, i == 0)
    norm_mod = lambda x: _rmsnorm_rows(x) * (1.0 + row[:, D:2 * D]) + row[:, 0:D]
    h1 = norm_mod(jnp.where(i == 0, ctx_ref[...], h_ref[...])).astype(BF16)
    halo = norm_mod(jnp.concatenate([hprev_ref[...], hnext_ref[...]], axis=0)).astype(BF16)
    d = functools.partial(lax.dot_general, dimension_numbers=_NN, preferred_element_type=F32)
    tm = h1.shape[0]
    c0 = 3 * CONV_CH
    c1 = c0 + 3 * DN_DIM
    c2 = c1 + LANE
    c3 = c2 + DN_DIM + ATT_DIM
    pconv_ref[...] = d(h1, w_ref[:, 0:c0])
    zq = d(h1, w_ref[:, c2:c3])
    pz_ref[...] = zq[:, 0:DN_DIM]
    pq_ref[...] = zq[:, DN_DIM:]
    pkv_ref[...] = d(h1, w_ref[:, c3:c3 + 2 * ATT_KV_DIM])
    qkv_ab = d(jnp.concatenate([h1, halo], axis=0), w_ref[:, c0:c2])
    qkv = qkv_ab[:, 0:3 * DN_DIM]
    ab = qkv_ab[0:tm, 3 * DN_DIM:]

    u = qkv[0:tm, :]
    pvalid, nvalid = _halo_valid(i, nblk)
    prow = jnp.where(pvalid, qkv[tm + 7:tm + 8, :], 0.0)
    nrow = jnp.where(nvalid, qkv[tm + 8:tm + 9, :], 0.0)
    up, un = _shift_rows(u, prow, nrow)
    cw = cw_ref[...]
    y = up * cw[0:1, :] + u * cw[1:2, :] + un * cw[2:3, :]
    y = y * _sigmoid(y)
    q = y[:, 0:DN_DIM]
    k = y[:, DN_DIM:2 * DN_DIM]
    bd = bd_ref[...]
    q_ref[...] = q * lax.rsqrt(_group_sum(q * q, bd) + 1e-6) * (DN_HEAD_DIM ** -0.5)
    k_ref[...] = k * lax.rsqrt(_group_sum(k * k, bd) + 1e-6)
    v_ref[...] = y[:, 2 * DN_DIM:3 * DN_DIM]
    g = -jnp.exp(alog_ref[...]) * _softplus(ab + dtb_ref[...])
    lane = lax.broadcasted_iota(jnp.int32, ab.shape, 1)
    gb_ref[...] = jnp.where(lane < 2 * DN_HEADS, g, _sigmoid(ab))


def _in_proj(ctx_src, lat_src, lat_blk0, mod, w_main, dn_conv_w, a_log, dt_bias, bd):
    r8 = TM // 8
    last8 = lat_src.shape[0] // 8 - 1
    nblk = 1 + lat_src.shape[0] // TM - lat_blk0
    t = nblk * TM
    alog = jnp.zeros((1, LANE), F32).at[0, :2 * DN_HEADS].set(a_log.reshape(-1))
    dtb = jnp.zeros((1, LANE), F32).at[0, :2 * DN_HEADS].set(dt_bias.reshape(-1))
    widths = (3 * CONV_CH, DN_DIM, ATT_DIM, 2 * ATT_KV_DIM, DN_DIM, DN_DIM, DN_DIM, LANE)
    const = lambda i: (0, 0)
    lat = lambda i: jnp.maximum(i - 1, 0) + lat_blk0
    return pl.pallas_call(
        functools.partial(_in_kernel, nblk=nblk),
        grid=(nblk,),
        in_specs=[
            pl.BlockSpec((TM, D), const),
            pl.BlockSpec((TM, D), lambda i: (lat(i), 0)),
            pl.BlockSpec((8, D), lambda i: (jnp.maximum(lat(i) * r8 - 1, 0), 0)),
            pl.BlockSpec((8, D), lambda i: (jnp.minimum((lat(i) + 1) * r8, last8), 0)),
            pl.BlockSpec((8, N_MOD * D), const),
            pl.BlockSpec(w_main.shape, const),
            pl.BlockSpec((3, 3 * DN_DIM), const),
            pl.BlockSpec((1, LANE), const),
            pl.BlockSpec((1, LANE), const),
            pl.BlockSpec(bd.shape, const),
        ],
        out_specs=[pl.BlockSpec((TM, w), lambda i: (i, 0)) for w in widths],
        out_shape=[jax.ShapeDtypeStruct((t, w), F32) for w in widths],
        compiler_params=_params(("arbitrary",), VMEM_LIMIT),
        name="in_proj",
    )(ctx_src, lat_src, lat_src, lat_src, mod, w_main, dn_conv_w, alog, dtb, bd)


def _dn_block(fwd_refs, bwd_refs, of_ref, ob_ref, s_ref):
    c_ = DN_CHUNK
    nh = DN_HEADS
    nchunks = fwd_refs[0].shape[0] // c_
    nb = nchunks * nh
    rows = lambda g: slice(g * c_, (g + 1) * c_)
    lanes = lambda h: slice(h * DN_HEAD_DIM, (h + 1) * DN_HEAD_DIM)
    ri = lax.broadcasted_iota(jnp.int32, (c_, c_), 0)
    ci = lax.broadcasted_iota(jnp.int32, (c_, c_), 1)
    same_sub = _same_group((c_, c_), DN_SUB)
    eye = jnp.where(ri == ci, 1.0, 0.0)
    shape3 = (2 * nb, c_, c_)
    delta = lax.broadcasted_iota(jnp.int32, shape3, 1) - lax.broadcasted_iota(jnp.int32, shape3, 2)
    delta = jnp.where(lax.broadcasted_iota(jnp.int32, shape3, 0) >= nb, -delta, delta)
    incl = delta >= 0
    strict = delta > 0

    def stack(fn):
        return jnp.stack([fn(d, refs, g, h) for d, refs in enumerate((fwd_refs, bwd_refs))
                          for g in range(nchunks) for h in range(nh)])

    gbs = [refs[3][...] for refs in (fwd_refs, bwd_refs)]
    tris = [jnp.where(ri >= ci, 1.0, 0.0).astype(BF16), jnp.where(ri <= ci, 1.0, 0.0).astype(BF16)]
    gcs = [[_cumsum_rows(tris[d], gbs[d][rows(g), :]) for g in range(nchunks)] for d in range(2)]
    gcts = [[gc.T for gc in gcs[d]] for d in range(2)]
    q = stack(lambda d, refs, g, h: refs[0][rows(g), lanes(h)])
    k = stack(lambda d, refs, g, h: refs[1][rows(g), lanes(h)])
    v = stack(lambda d, refs, g, h: refs[2][rows(g), lanes(h)])
    col = lambda d, h: d * nh + h
    gcol = stack(lambda d, refs, g, h: gcs[d][g][:, col(d, h):col(d, h) + 1])
    grow = stack(lambda d, refs, g, h: gcts[d][g][col(d, h):col(d, h) + 1, :])
    beta = stack(lambda d, refs, g, h: gbs[d][rows(g), 2 * nh + col(d, h):2 * nh + col(d, h) + 1])
    glast = jnp.concatenate([gcol[:nb, c_ - 1:c_, :], gcol[nb:, 0:1, :]], axis=0)
    decay = jnp.where(incl, jnp.exp(jnp.where(incl, gcol - grow, 0.0)), 0.0)
    eg = jnp.exp(gcol)
    kb = k * beta
    a = jnp.where(strict, _bmm(kb, k, _BNT) * decay, 0.0)
    qk = jnp.where(incl, _bmm(q, k, _BNT) * decay, 0.0)
    ad = jnp.where(same_sub, a, 0.0)
    ao = a - ad
    p = eye - ad
    n2 = _bmm(ad, ad)
    p = p + _bmm(p, n2)
    n4 = _bmm(n2, n2)
    p = p + _bmm(p, n4)
    n8 = _bmm(n4, n4)
    dinv = p + _bmm(p, n8)
    m = _bmm(dinv, ao)
    m2 = _bmm(m, m)
    y = _bmm(dinv, jnp.concatenate([v * beta, kb * eg], axis=-1))
    z = y + _bmm(m2, y)
    x = z - _bmm(m, z)
    u = x[:, :, :DN_HEAD_DIM]
    w = x[:, :, DN_HEAD_DIM:]
    qg = q * eg
    kd = k * jnp.exp(glast - gcol)
    gl = jnp.exp(glast)
    s = s_ref[...]
    for t in range(nchunks):
        gf, gr = t, nchunks - 1 - t
        step = lambda a: jnp.concatenate([a[gf * nh:(gf + 1) * nh], a[nb + gr * nh:nb + (gr + 1) * nh]], axis=0)
        v_new = step(u) - _bmm(step(w), s)
        o = _bmm(step(qg), s) + _bmm(step(qk), v_new)
        s = s * step(gl) + _bmm(step(kd), v_new, _BTN)
        for h in range(nh):
            of_ref[rows(gf), lanes(h)] = o[h]
            ob_ref[rows(gr), lanes(h)] = o[nh + h]
    s_ref[...] = s


def _cumsum_rows(tri_bf16, g):
    g2, g1, g0 = _split3(g)
    d = functools.partial(lax.dot_general, dimension_numbers=_NN, preferred_element_type=F32)
    return d(tri_bf16, g2) + (d(tri_bf16, g1) + d(tri_bf16, g0))


def _dn_kernel(qf, kf, vf, gf, qb, kb, vb, gbb, *rest, n_cast):
    cast_in = rest[:n_cast]
    of_ref, ob_ref = rest[n_cast:n_cast + 2]
    cast_out = rest[n_cast + 2:2 * n_cast + 2]
    s_ref = rest[-1]

    @pl.when(pl.program_id(0) == 0)
    def _():
        s_ref[...] = jnp.zeros(s_ref.shape, F32)

    _dn_block((qf, kf, vf, gf), (qb, kb, vb, gbb), of_ref, ob_ref, s_ref)
    for src, dst in zip(cast_in, cast_out):
        dst[...] = src[...].astype(BF16)


def _delta_net(q, k, v, gb, ctx_len, to_bf16=()):
    t = q.shape[0]
    rows = DN_STEP_CHUNKS * DN_CHUNK
    assert ctx_len == rows and t % rows == 0
    nstep = t // rows
    cast_specs = []
    for m in to_bf16:
        rb = -(-m.shape[0] // nstep)
        rb = -(-rb // 16) * 16
        nb = -(-m.shape[0] // rb)
        cast_specs.append(pl.BlockSpec((rb, m.shape[1]), lambda s, nb=nb: (jnp.minimum(s, nb - 1), 0)))

    def fwd(s):
        return (s, 0)

    def bwd(s):
        return (jnp.where(s == 0, 0, nstep - s), 0)

    wide = lambda im: pl.BlockSpec((rows, DN_DIM), im)
    narrow = lambda im: pl.BlockSpec((rows, LANE), im)
    return pl.pallas_call(
        functools.partial(_dn_kernel, n_cast=len(to_bf16)),
        grid=(nstep,),
        in_specs=[wide(fwd), wide(fwd), wide(fwd), narrow(fwd), wide(bwd), wide(bwd), wide(bwd), narrow(bwd)]
        + cast_specs,
        out_specs=[wide(fwd), wide(bwd)] + cast_specs,
        out_shape=[jax.ShapeDtypeStruct((t, DN_DIM), F32)] * 2
        + [jax.ShapeDtypeStruct(m.shape, BF16) for m in to_bf16],
        scratch_shapes=[pltpu.VMEM((2 * DN_HEADS, DN_HEAD_DIM, DN_HEAD_DIM), F32)],
        compiler_params=_params(("arbitrary",), VMEM_LIMIT),
        name="delta_net",
    )(q, k, v, gb, q, k, v, gb, *to_bf16)


def _rope(x, cos, sin):
    w = x.shape[1]
    lane = lax.broadcasted_iota(jnp.int32, x.shape, 1)
    first_half = (lane & (AXIS_DIM - 1)) < (AXIS_DIM // 2)
    swapped = jnp.where(first_half, pltpu.roll(x, w - AXIS_DIM // 2, 1), pltpu.roll(x, AXIS_DIM // 2, 1))
    return x * cos + swapped * sin


LOG2E = 1.4426950408889634


def _softmax_av(s, sink, vals):
    m = jnp.maximum(jnp.max(s, axis=-1, keepdims=True), sink)
    p = jnp.exp2(s - m)
    denom = jnp.sum(p, axis=-1, keepdims=True) + jnp.exp2(sink - m)
    return _mm(p, vals) / denom


def _attend(q, keys, vals, band, sink_all, o_ref, row0=0):
    b = q.shape[0]
    for kvh in range(ATT_KV_HEADS):
        kl = slice(kvh * HEAD_DIM, (kvh + 1) * HEAD_DIM)
        heads = range(kvh * ATT_GROUP, (kvh + 1) * ATT_GROUP)
        qs = jnp.concatenate([q[:, h * HEAD_DIM:(h + 1) * HEAD_DIM] for h in heads], axis=0)
        sink = jnp.concatenate([jnp.broadcast_to(sink_all[0:1, h:h + 1], (b, 1)) for h in heads], axis=0)
        s = _mm(qs, keys[:, kl], _NT)
        if band is not None:
            kb = ATT_BLOCK
            s = jnp.concatenate([jnp.where(band[0], s[:, 0:kb], NEG), s[:, kb:2 * kb],
                                 jnp.where(band[1], s[:, 2 * kb:3 * kb], NEG), s[:, 3 * kb:]], axis=1)
        o = _softmax_av(s, sink * LOG2E, vals[:, kl])
        for g, h in enumerate(heads):
            o_ref[row0:row0 + b, h * HEAD_DIM:(h + 1) * HEAD_DIM] = o[g * b:(g + 1) * b, :]


def _band_valid(first, last):
    b = ATT_BLOCK
    c = lax.broadcasted_iota(jnp.int32, (1, b), 1)
    r = lax.broadcasted_iota(jnp.int32, (ATT_GROUP * b, 1), 0) & (b - 1)
    prev_ok = jnp.where(first, -1, c) >= r
    next_ok = jnp.where(last, b, c) <= r
    return prev_ok, next_ok


def _rope_block(rowtab_ref, coltab_ref, blk):
    rt = rowtab_ref[blk]
    ct = coltab_ref[...]
    reps = ATT_BLOCK // GRID_W
    rows = jnp.concatenate([jnp.broadcast_to(rt[g:g + 1, :], (GRID_W, LANE)) for g in range(reps)], axis=0)
    return rows + jnp.concatenate([ct] * reps, axis=0)


def _attn_kernel(q_ref, kp_ref, kc_ref, kn_ref, kctx_ref, cosr_ref, sinr_ref, cosc_ref, sinc_ref, sink_ref, o_ref,
                 *, nb, with_ctx):
    j = pl.program_id(0)
    b = ATT_BLOCK
    scale = HEAD_DIM ** -0.5 * LOG2E

    @pl.when(j == 0)
    def _():
        if with_ctx:
            kvx = kctx_ref[...]
            _attend(q_ref[...] * scale, kvx[:, :ATT_KV_DIM], kvx[:, ATT_KV_DIM:], None, sink_ref[...], o_ref)
        else:
            o_ref[...] = jnp.zeros(o_ref.shape, F32)

    @pl.when(j > 0)
    def _():
        b0 = 2 * (j - 1)
        blocks = (jnp.maximum(b0 - 1, 0), b0, b0 + 1, jnp.minimum(b0 + 2, nb - 1))
        cos = [_rope_block(cosr_ref, cosc_ref, blk) for blk in blocks]
        sin = [_rope_block(sinr_ref, sinc_ref, blk) for blk in blocks]
        kvc = kc_ref[...]
        kv = (kp_ref[...], kvc[0:b, :], kvc[b:2 * b, :], kn_ref[...])
        kvx = kctx_ref[...]
        keys = [_rope(t[:, :ATT_KV_DIM], c_, s_) for t, c_, s_ in zip(kv, cos, sin)]
        q_all = q_ref[...]
        sink_all = sink_ref[...]
        for sub in range(2):
            q = q_all[sub * b:(sub + 1) * b, :]
            q = jnp.concatenate([_rope(q[:, l * LANE:(l + 1) * LANE], cos[1 + sub], sin[1 + sub])
                                 for l in range(ATT_DIM // LANE)], axis=1)
            kcat = jnp.concatenate(keys[sub:sub + 3] + [kvx[:, :ATT_KV_DIM]], axis=0)
            vcat = jnp.concatenate([t[:, ATT_KV_DIM:] for t in kv[sub:sub + 3]] + [kvx[:, ATT_KV_DIM:]], axis=0)
            band = _band_valid(first=(b0 + sub == 0), last=(b0 + sub == nb - 1))
            _attend(q * scale, kcat, vcat, band, sink_all, o_ref, row0=sub * b)


def _attention(pq, pkv, rope, sink, ctx_len, with_ctx):
    t = pq.shape[0]
    n = t - ctx_len
    nb = n // ATT_BLOCK
    step = 2 * ATT_BLOCK
    assert ctx_len == step and nb % 2 == 0
    off = ctx_len // ATT_BLOCK
    half = lambda im: pl.BlockSpec((ATT_BLOCK, 2 * ATT_KV_DIM), im)
    lat = lambda j: 2 * jnp.maximum(j - 1, 0)
    prv = lambda j: (jnp.maximum(lat(j) - 1, 0) + off, 0)
    nxt = lambda j: (jnp.minimum(lat(j) + 2, nb - 1) + off, 0)
    cur = lambda j: (j, 0)
    whole = lambda a: pl.BlockSpec(a.shape, lambda j: (0,) * a.ndim)
    return pl.pallas_call(
        functools.partial(_attn_kernel, nb=nb, with_ctx=with_ctx),
        grid=(1 + nb // 2,),
        in_specs=[
            pl.BlockSpec((step, ATT_DIM), cur),
            half(prv), pl.BlockSpec((step, 2 * ATT_KV_DIM), cur), half(nxt),
            pl.BlockSpec((ctx_len, 2 * ATT_KV_DIM), lambda j: (0, 0)),
        ] + [whole(a) for a in rope] + [pl.BlockSpec((8, LANE), lambda j: (0, 0))],
        out_specs=pl.BlockSpec((step, ATT_DIM), cur),
        out_shape=jax.ShapeDtypeStruct((t, ATT_DIM), F32),
        compiler_params=_params(("arbitrary",), VMEM_LIMIT),
        name="attention",
    )(pq, pkv, pkv, pkv, pkv, *rope, sink)


def _mixfin_kernel(ctx_ref, h_ref, mod_ref, pconv_ref, prev_ref, next_ref, cw_ref, of_ref, ob_ref, z_ref, ng_ref,
                   yc_ref, wout_ref, bd_ref, *rest, nblk, blk0, with_router):
    if with_router:
        wr1_ref, wr0_ref, x_ref, hx_ref, lg_ref = rest
    else:
        wg_ref, wu_ref, wd_ref, x_ref = rest
    i = pl.program_id(0) + blk0
    row = _mod_row(mod_ref, i == 0)
    pc = pconv_ref[...]
    u = pc[:, CONV_CH:2 * CONV_CH] * pc[:, 2 * CONV_CH:]
    prow, nrow = _halo_rows(prev_ref, next_ref, i, nblk)
    prow = prow[:, CONV_CH:2 * CONV_CH] * prow[:, 2 * CONV_CH:]
    nrow = nrow[:, CONV_CH:2 * CONV_CH] * nrow[:, 2 * CONV_CH:]
    up, un = _shift_rows(u, prow, nrow)
    cw = cw_ref[...]
    ya = pc[:, :CONV_CH] * (up * cw[0:1, :] + u * cw[1:2, :] + un * cw[2:3, :])
    o = of_ref[...] + ob_ref[...]
    ms = _group_sum(o * o, bd_ref[...]) * (1.0 / DN_HEAD_DIM)
    z = z_ref[...]
    yb = o * lax.rsqrt(ms + EPS) * ng_ref[...] * (z * _sigmoid(z))
    mix = jnp.concatenate([ya, yb, yc_ref[...]], axis=1)
    x = jnp.where(i == 0, ctx_ref[...], h_ref[...]) + row[:, 2 * D:3 * D] * _mm(mix, wout_ref[...])
    hx = _rmsnorm_rows(x) * (1.0 + row[:, 4 * D:5 * D]) + row[:, 3 * D:4 * D]
    if not with_router:
        hb = hx.astype(BF16)
        acc = jnp.zeros((hb.shape[0], D), F32)
        for f in range(0, D_FF, FF_CHUNK):
            fe = min(f + FF_CHUNK, D_FF)
            g = _mm(hb, wg_ref[:, f:fe])
            u_ = _mm(hb, wu_ref[:, f:fe])
            acc = acc + _mm(g * _sigmoid(g) * u_, wd_ref[f:fe, :])
        x_ref[...] = x + row[:, 5 * D:6 * D] * acc
    else:
        x_ref[...] = x
        hx_ref[...] = hx
        h1, h0 = _split2(hx)
        d = functools.partial(lax.dot_general, dimension_numbers=_NN, preferred_element_type=F32)
        lg = d(h1, wr1_ref[...]) + (d(h0, wr1_ref[...]) + d(h1, wr0_ref[...]))
        lane = lax.broadcasted_iota(jnp.int32, lg.shape, 1)
        lanef = lane.astype(F32)
        lg = jnp.where(lane < N_EXPERTS, lg, -jnp.inf)
        m1 = jnp.max(lg, axis=-1, keepdims=True)
        i1 = jnp.min(jnp.where(lg == m1, lanef, float(LANE)), axis=-1, keepdims=True)
        rest = jnp.where(lanef == i1, -jnp.inf, lg)
        m2 = jnp.max(rest, axis=-1, keepdims=True)
        i2 = jnp.min(jnp.where(rest == m2, lanef, float(LANE)), axis=-1, keepdims=True)
        e2 = jnp.exp(m2 - m1)
        g1 = 1.0 / (1.0 + e2)
        lg_ref[...] = jnp.where(lane == 0, g1, jnp.where(lane == 1, e2 * g1, jnp.where(lane == 2, i1, i2)))


def _mixer_finish(ctx_src, lat_src, lat_blk0, mod, pconv, conv_w, o_f, o_b, pz, norm_g, yc, w_out, bd, ctx_len,
                  with_ctx, router=None, ffn=None):
    assert (router is None) != (ffn is None)
    t = pconv.shape[0]
    nblk = t // TM
    blk0 = 0 if with_ctx else ctx_len // TM
    rows = t - blk0 * TM
    r8 = TM // 8
    w = pconv.shape[1]
    cur = lambda i: (i + blk0, 0)
    out_cur = lambda i: (i, 0)
    const = lambda i: (0, 0)
    ng = jnp.tile(norm_g.reshape(1, DN_HEAD_DIM), (1, DN_HEADS))
    in_specs = [
        pl.BlockSpec((TM, D), const),
        pl.BlockSpec((TM, D), lambda i: (jnp.maximum(i + blk0 - 1, 0) + lat_blk0, 0)),
        pl.BlockSpec((8, N_MOD * D), const),
        pl.BlockSpec((TM, w), cur),
        pl.BlockSpec((8, w), lambda i: (jnp.maximum((i + blk0) * r8 - 1, 0), 0)),
        pl.BlockSpec((8, w), lambda i: (jnp.minimum((i + blk0 + 1) * r8, t // 8 - 1), 0)),
        pl.BlockSpec((3, CONV_CH), const),
        pl.BlockSpec((TM, DN_DIM), cur),
        pl.BlockSpec((TM, DN_DIM), cur),
        pl.BlockSpec((TM, DN_DIM), cur),
        pl.BlockSpec((1, DN_DIM), const),
        pl.BlockSpec((TM, ATT_DIM), cur),
        pl.BlockSpec((MIX_DIM, D), const),
        pl.BlockSpec(bd.shape, const),
    ]
    args = [ctx_src, lat_src, mod, pconv, pconv, pconv, conv_w, o_f, o_b, pz, ng, yc, w_out, bd]
    out_specs = [pl.BlockSpec((TM, D), out_cur)]
    out_shape = [jax.ShapeDtypeStruct((rows, D), F32)]
    if router is not None:
        in_specs += [pl.BlockSpec((D, LANE), const)] * 2
        args += list(router)
        out_specs += [pl.BlockSpec((TM, D), out_cur), pl.BlockSpec((TM, LANE), out_cur)]
        out_shape += [jax.ShapeDtypeStruct((rows, D), F32), jax.ShapeDtypeStruct((rows, LANE), F32)]
    else:
        once = dict(pipeline_mode=pl.Buffered(1))
        in_specs += [pl.BlockSpec(w_.shape, const, **once) for w_ in ffn]
        args += list(ffn)
    return pl.pallas_call(
        functools.partial(_mixfin_kernel, nblk=nblk, blk0=blk0, with_router=router is not None),
        grid=(rows // TM,),
        in_specs=in_specs,
        out_specs=out_specs,
        out_shape=out_shape,
        compiler_params=_params(("arbitrary",), VMEM_LIMIT),
        name="mixer_finish",
    )(*args)


def _moe_kernel(be_ref, nu_ref, xs_ref, wg_ref, wu_ref, wd_ref, y_ref, acc_ref):
    b = pl.program_id(0)
    f = pl.program_id(1)

    @pl.when(b < nu_ref[0])
    def _():
        xs = xs_ref[...].astype(BF16)
        part = jnp.zeros((MOE_TM, D), F32)
        for c in range(0, MOE_TF, MOE_SUB):
            g = _mm(xs, wg_ref[0, :, c:c + MOE_SUB])
            u = _mm(xs, wu_ref[0, :, c:c + MOE_SUB])
            part = part + _mm(g * _sigmoid(g) * u, wd_ref[0, c:c + MOE_SUB, :])

        @pl.when(f == 0)
        def _():
            acc_ref[...] = part

        @pl.when(f != 0)
        def _():
            acc_ref[...] += part

        @pl.when(f == pl.num_programs(1) - 1)
        def _():
            y_ref[...] = acc_ref[...]

    @pl.when(b >= nu_ref[0])
    def _():
        y_ref[...] = jnp.zeros(y_ref.shape, F32)


def _moe_experts(xs, blk_e, n_used, wg, wu, wd):
    cap = xs.shape[0]
    nblk = cap // MOE_TM
    nf = D_FF_EXPERT // MOE_TF

    def fidx(b, f, nu):
        return jnp.where(b < nu[0], f, nf - 1)

    grid_spec = pltpu.PrefetchScalarGridSpec(
        num_scalar_prefetch=2,
        grid=(nblk, nf),
        in_specs=[
            pl.BlockSpec((MOE_TM, D), lambda b, f, be, nu: (b, 0)),
            pl.BlockSpec((1, D, MOE_TF), lambda b, f, be, nu: (be[b], 0, fidx(b, f, nu))),
            pl.BlockSpec((1, D, MOE_TF), lambda b, f, be, nu: (be[b], 0, fidx(b, f, nu))),
            pl.BlockSpec((1, MOE_TF, D), lambda b, f, be, nu: (be[b], fidx(b, f, nu), 0)),
        ],
        out_specs=pl.BlockSpec((MOE_TM, D), lambda b, f, be, nu: (b, 0)),
        scratch_shapes=[pltpu.VMEM((MOE_TM, D), F32)],
    )
    return pl.pallas_call(
        _moe_kernel,
        grid_spec=grid_spec,
        out_shape=jax.ShapeDtypeStruct((cap, D), F32),
        compiler_params=_params(("arbitrary", "arbitrary"), VMEM_LIMIT),
        name="moe_experts",
    )(blk_e, n_used, xs, wg, wu, wd)


def _moe_route(choices):
    n = choices[0].shape[0]
    a = n * TOP_K
    flat_e = jnp.concatenate(choices)
    onehot = (flat_e[:, None] == jnp.arange(N_EXPERTS, dtype=flat_e.dtype)[None, :]).astype(jnp.int32)
    counts = jnp.sum(onehot, axis=0)
    padded = (counts + MOE_TM - 1) // MOE_TM * MOE_TM
    pad_ends = jnp.cumsum(padded)
    pad_starts = pad_ends - padded
    dest = jnp.sum(onehot * (jnp.cumsum(onehot, axis=0) - onehot + pad_starts[None, :]), axis=1)
    cap = a + N_EXPERTS * MOE_TM
    nblk = cap // MOE_TM
    row_tok = (jnp.arange(cap, dtype=jnp.int32) % n).at[dest].set(
        jnp.arange(a, dtype=jnp.int32) % n, unique_indices=True, mode="promise_in_bounds")
    blk_start = jnp.arange(nblk, dtype=jnp.int32) * MOE_TM
    blk_e = jnp.minimum(jnp.sum((pad_ends[None, :] <= blk_start[:, None]).astype(jnp.int32), axis=1),
                        N_EXPERTS - 1)
    n_used = (pad_ends[-1] // MOE_TM).astype(jnp.int32).reshape(1)
    last_e = blk_e[jnp.maximum(n_used[0] - 1, 0)]
    blk_e = jnp.where(jnp.arange(nblk) < n_used[0], blk_e, last_e)
    return [dest[s * n:(s + 1) * n] for s in range(TOP_K)], row_tok, blk_e, n_used


def _final_kernel(x_ref, y0_ref, y1_ref, gt_ref, mod_ref, fg_ref, o_ref):
    mod = mod_ref[...]
    gt = gt_ref[...]
    f = gt[:, 0:1] * y0_ref[...] + gt[:, 1:2] * y1_ref[...]
    x = x_ref[...] + mod[0:1, 5 * D:6 * D] * f
    o_ref[...] = _rmsnorm_rows(x) * fg_ref[...]


def _moe_combine_final(x, y0, y1, gt, mod, final_g):
    n = x.shape[0]
    row = lambda i: (i, 0)
    const = lambda i: (0, 0)
    return pl.pallas_call(
        _final_kernel,
        grid=(n // TM,),
        in_specs=[pl.BlockSpec((TM, D), row)] * 3
        + [pl.BlockSpec((TM, LANE), row), pl.BlockSpec((8, N_MOD * D), const), pl.BlockSpec((1, D), const)],
        out_specs=pl.BlockSpec((TM, D), row),
        out_shape=jax.ShapeDtypeStruct((n, D), F32),
        compiler_params=_params(("arbitrary",), VMEM_LIMIT),
        name="moe_combine_final",
    )(x, y0, y1, gt, mod, final_g.reshape(1, D))


def _rope_tables(n):
    lane = jnp.arange(LANE, dtype=jnp.int32) % HEAD_DIM
    inv = ROPE_BASE ** (-jnp.arange(0, AXIS_DIM, 2, dtype=F32) / AXIS_DIM)
    freq = inv[lane % (AXIS_DIM // 2)]
    row_axis = (lane // AXIS_DIM) == 0
    sign = jnp.where((lane % AXIS_DIM) < AXIS_DIM // 2, -1.0, 1.0)
    reps = ATT_BLOCK // GRID_W

    def tables(count, on_axis):
        ang = jnp.arange(count, dtype=F32)[:, None] * freq[None, :]
        return jnp.where(on_axis, jnp.cos(ang), 0.0), jnp.where(on_axis, jnp.sin(ang) * sign, 0.0)

    by_block = lambda t: jnp.pad(t.reshape(-1, reps, LANE), ((0, 0), (0, 8 - reps), (0, 0)))
    cosr, sinr = tables(n // GRID_W, row_axis)
    cosc, sinc = tables(GRID_W, ~row_axis)
    return by_block(cosr), by_block(sinr), cosc, sinc


def _prep_w_in(w):
    pad = jnp.zeros((D, LANE - N_AB), w.dtype)
    return jnp.concatenate([w[:, :_C_Z], w[:, _C_A:_C_Q], pad, w[:, _C_Z:_C_A], w[:, _C_Q:_C_END]],
                           axis=1).astype(BF16)


def kernel(x, c, ctx, c_ctx, w_mod, b_mod, w_in, w_out, conv_w, dn_conv_w, dn_a_log, dn_dt_bias, dn_norm_g,
           attn_sink, ffn_w_gate, ffn_w_up, ffn_w_down, moe_router, moe_w_gate, moe_w_up, moe_w_down,
           final_norm_g):
    bsz, n, d = x.shape
    ctx_len = ctx.shape[1]
    depth = w_in.shape[0]
    assert bsz == 1 and d == D and ctx_len == TM and n % TM == 0 and n % GRID_W == 0
    rope = _rope_tables(n)
    mods = _mod_vectors(c, c_ctx, w_mod, b_mod)
    bd = _head_blockdiag(DN_DIM, DN_HEAD_DIM)
    stream = (ctx[0], x[0], 0)
    for layer in range(depth):
        last = layer == depth - 1
        mod = mods[layer]
        pconv, pz, pq, pkv, qn, kn, vv, gb = _in_proj(*stream, mod, _prep_w_in(w_in[layer]), dn_conv_w[layer],
                                                      dn_a_log[layer], dn_dt_bias[layer], bd)
        to_bf16 = []
        if layer % 2 == 0:
            to_bf16 += [w[layer // 2] for w in (ffn_w_gate, ffn_w_up, ffn_w_down)]
        if (layer + 1) % 2 == 1 and layer + 1 < depth:
            j = (layer + 1) // 2
            to_bf16 += [w[j].reshape(-1, w.shape[-1]) for w in (moe_w_gate, moe_w_up, moe_w_down)]
        o_f, o_b, *cast = _delta_net(qn, kn, vv, gb, ctx_len, tuple(to_bf16))
        if layer % 2 == 0:
            ffn_bf16, cast = cast[:3], cast[3:]
        if cast:
            moe_bf16 = [c.reshape(w.shape[1:]) for c, w in zip(cast, (moe_w_gate, moe_w_up, moe_w_down))]
        sink = jnp.zeros((8, LANE), F32).at[0, :ATT_HEADS].set(attn_sink[layer])
        yc = _attention(pq, pkv, rope, sink, ctx_len, with_ctx=not last)
        router = None
        if layer % 2 == 1:
            wr = jnp.zeros((D, LANE), F32).at[:, :N_EXPERTS].set(moe_router[layer // 2])
            wr1 = wr.astype(BF16)
            router = (wr1, (wr - wr1.astype(F32)).astype(BF16))
        outs = _mixer_finish(*stream, mod, pconv, conv_w[layer], o_f, o_b, pz, dn_norm_g[layer], yc,
                             w_out[layer].astype(BF16), bd, ctx_len, with_ctx=not last, router=router,
                             ffn=ffn_bf16 if layer % 2 == 0 else None)
        if layer % 2 == 0:
            assert not last
            h, = outs
            stream = (h, h, ctx_len // TM)
        else:
            assert last
            x1, hx, route = outs
            dest, row_tok, blk_e, n_used = _moe_route([route[:, 2 + s].astype(jnp.int32) for s in range(TOP_K)])
            take = lambda rows_, idx: rows_.at[idx].get(mode="promise_in_bounds")
            y = _moe_experts(take(hx, row_tok), blk_e, n_used, *moe_bf16)
            h = _moe_combine_final(x1, take(y, dest[0]), take(y, dest[1]), route, mod, final_norm_g)
    return h.reshape(bsz, n, d)
```

```python
import functools

import jax
import jax.numpy as jnp
from jax import lax
from jax.experimental import pallas as pl
from jax.experimental.pallas import tpu as pltpu

F32 = jnp.float32
BF16 = jnp.bfloat16

D = 1024
N_MOD = 6
EPS = 1e-6
NEG = -1e30
GRID_W = 64

CONV_CH = 256
DN_HEADS = 6
DN_HEAD_DIM = 64
DN_DIM = DN_HEADS * DN_HEAD_DIM
DN_CHUNK = 64
DN_SUB = 16
DN_STEP_CHUNKS = 4
ATT_HEADS = 6
ATT_KV_HEADS = 2
ATT_GROUP = ATT_HEADS // ATT_KV_HEADS
HEAD_DIM = 64
ATT_DIM = ATT_HEADS * HEAD_DIM
ATT_KV_DIM = ATT_KV_HEADS * HEAD_DIM
ATT_BLOCK = 128
ROPE_BASE = 10000.0
AXIS_DIM = HEAD_DIM // 2
MIX_DIM = CONV_CH + DN_DIM + ATT_DIM

D_FF = 2816
N_EXPERTS = 8
TOP_K = 2
D_FF_EXPERT = 3584

TM = 256
FF_CHUNK = 512
MOE_TM = 512
MOE_TF = 1792
MOE_SUB = 256
LANE = 128
VMEM_LIMIT = 56 * 1024 * 1024

_C_QKV = 3 * CONV_CH
_C_Z = _C_QKV + 3 * DN_DIM
_C_A = _C_Z + DN_DIM
_C_Q = _C_A + 4 * DN_HEADS
_C_K = _C_Q + ATT_DIM
_C_V = _C_K + ATT_KV_DIM
_C_END = _C_V + ATT_KV_DIM
N_AB = 4 * DN_HEADS


def _params(sem=None, vmem=None):
    kw = {}
    if sem is not None:
        kw["dimension_semantics"] = sem
    if vmem is not None:
        kw["vmem_limit_bytes"] = vmem
    return pltpu.CompilerParams(**kw)


def _split2(a):
    hi = a.astype(BF16)
    lo = (a - hi.astype(F32)).astype(BF16)
    return hi, lo


def _split3(a):
    hi = a.astype(BF16)
    r = a - hi.astype(F32)
    mid = r.astype(BF16)
    lo = (r - mid.astype(F32)).astype(BF16)
    return hi, mid, lo


_NN = (((1,), (0,)), ((), ()))
_NT = (((1,), (1,)), ((), ()))
_TN = (((0,), (0,)), ((), ()))


def _mm(a, b, dims=_NN):
    return lax.dot_general(a.astype(BF16), b.astype(BF16), dims, preferred_element_type=F32)


def _mm3(a, b, dims=_NN):
    a1, a0 = _split2(a)
    b1, b0 = _split2(b)
    d = functools.partial(lax.dot_general, dimension_numbers=dims, preferred_element_type=F32)
    return d(a1, b1) + (d(a1, b0) + d(a0, b1))


_BNN = (((2,), (1,)), ((0,), (0,)))
_BNT = (((2,), (2,)), ((0,), (0,)))
_BTN = (((1,), (1,)), ((0,), (0,)))


def _bmm(a, b, dims=_BNN):
    return lax.dot_general(a.astype(BF16), b.astype(BF16), dims, preferred_element_type=F32)


def _sigmoid(x):
    return 1.0 / (1.0 + jnp.exp(-x))


def _softplus(x):
    return jnp.maximum(x, 0.0) + jnp.log1p(jnp.exp(-jnp.abs(x)))


def _mod_row(mod_ref, is_ctx):
    mod = mod_ref[...]
    return jnp.where(is_ctx, mod[1:2, :], mod[0:1, :])


def _rmsnorm_rows(x):
    return x * lax.rsqrt(jnp.mean(x * x, axis=-1, keepdims=True) + EPS)


def _shift_rows(u, prow, nrow):
    n = u.shape[0]
    rid = lax.broadcasted_iota(jnp.int32, u.shape, 0)
    up = jnp.where(rid == 0, prow, pltpu.roll(u, 1, 0))
    un = jnp.where(rid == n - 1, nrow, pltpu.roll(u, n - 1, 0))
    return up, un


def _same_group(shape, group):
    sh = group.bit_length() - 1
    assert 1 << sh == group
    return (lax.broadcasted_iota(jnp.int32, shape, 0) >> sh) == (lax.broadcasted_iota(jnp.int32, shape, 1) >> sh)


def _head_blockdiag(n, group):
    g = jnp.arange(n, dtype=jnp.int32) // group
    return (g[:, None] == g[None, :]).astype(BF16)


def _group_sum(t, bd):
    return _mm(t, bd)


MOD_TN = 1536


def _mod_kernel(s_ref, w_ref, b_ref, o_ref):
    s = s_ref[...]
    s = s * _sigmoid(s)
    o_ref[0] = _mm3(s, w_ref[0]) + b_ref[0]


def _mod_vectors(c, c_ctx, w_mod, b_mod):
    depth = w_mod.shape[0]
    s = jnp.zeros((8, D), F32).at[0].set(c[0]).at[1].set(c_ctx)
    return pl.pallas_call(
        _mod_kernel,
        grid=(depth, N_MOD * D // MOD_TN),
        in_specs=[
            pl.BlockSpec((8, D), lambda l, j: (0, 0)),
            pl.BlockSpec((1, D, MOD_TN), lambda l, j: (l, 0, j)),
            pl.BlockSpec((1, 1, MOD_TN), lambda l, j: (l, 0, j)),
        ],
        out_specs=pl.BlockSpec((1, 8, MOD_TN), lambda l, j: (l, 0, j)),
        out_shape=jax.ShapeDtypeStruct((depth, 8, N_MOD * D), F32),
        compiler_params=_params(("arbitrary", "arbitrary"), VMEM_LIMIT),
        name="mod_vectors",
    )(s, w_mod, b_mod.reshape(depth, 1, N_MOD * D))


def _halo_valid(i, nblk):
    return jnp.logical_and(i != 0, i != 1), jnp.logical_and(i != 0, i != nblk - 1)


def _halo_rows(prev_ref, next_ref, i, nblk):
    pvalid, nvalid = _halo_valid(i, nblk)
    prow = jnp.where(pvalid, prev_ref[7:8, :], 0.0)
    nrow = jnp.where(nvalid, next_ref[0:1, :], 0.0)
    return prow, nrow


def _in_kernel(ctx_ref, h_ref, hprev_ref, hnext_ref, mod_ref, w_ref, cw_ref, alog_ref, dtb_ref, bd_ref,
               pconv_ref, pz_ref, pq_ref, pkv_ref, q_ref, k_ref, v_ref, gb_ref, *, nblk):
    i = pl.program_id(0)
    row = _mod_row(mod_ref, i == 0)
    norm_mod = lambda x: _rmsnorm_rows(x) * (1.0 + row[:, D:2 * D]) + row[:, 0:D]
    h1 = norm_mod(jnp.where(i == 0, ctx_ref[...], h_ref[...])).astype(BF16)
    halo = norm_mod(jnp.concatenate([hprev_ref[...], hnext_ref[...]], axis=0)).astype(BF16)
    d = functools.partial(lax.dot_general, dimension_numbers=_NN, preferred_element_type=F32)
    tm = h1.shape[0]
    c0 = 3 * CONV_CH
    c1 = c0 + 3 * DN_DIM
    c2 = c1 + LANE
    c3 = c2 + DN_DIM + ATT_DIM
    pconv_ref[...] = d(h1, w_ref[:, 0:c0])
    zq = d(h1, w_ref[:, c2:c3])
    pz_ref[...] = zq[:, 0:DN_DIM]
    pq_ref[...] = zq[:, DN_DIM:]
    pkv_ref[...] = d(h1, w_ref[:, c3:c3 + 2 * ATT_KV_DIM])
    qkv_ab = d(jnp.concatenate([h1, halo], axis=0), w_ref[:, c0:c2])
    qkv = qkv_ab[:, 0:3 * DN_DIM]
    ab = qkv_ab[0:tm, 3 * DN_DIM:]

    u = qkv[0:tm, :]
    pvalid, nvalid = _halo_valid(i, nblk)
    prow = jnp.where(pvalid, qkv[tm + 7:tm + 8, :], 0.0)
    nrow = jnp.where(nvalid, qkv[tm + 8:tm + 9, :], 0.0)
    up, un = _shift_rows(u, prow, nrow)
    cw = cw_ref[...]
    y = up * cw[0:1, :] + u * cw[1:2, :] + un * cw[2:3, :]
    y = y * _sigmoid(y)
    q = y[:, 0:DN_DIM]
    k = y[:, DN_DIM:2 * DN_DIM]
    bd = bd_ref[...]
    q_ref[...] = q * lax.rsqrt(_group_sum(q * q, bd) + 1e-6) * (DN_HEAD_DIM ** -0.5)
    k_ref[...] = k * lax.rsqrt(_group_sum(k * k, bd) + 1e-6)
    v_ref[...] = y[:, 2 * DN_DIM:3 * DN_DIM]
    g = -jnp.exp(alog_ref[...]) * _softplus(ab + dtb_ref[...])
    lane = lax.broadcasted_iota(jnp.int32, ab.shape, 1)
    gb_ref[...] = jnp.where(lane < 2 * DN_HEADS, g, _sigmoid(ab))


def _in_proj(ctx_src, lat_src, lat_blk0, mod, w_main, dn_conv_w, a_log, dt_bias, bd):
    r8 = TM // 8
    last8 = lat_src.shape[0] // 8 - 1
    nblk = 1 + lat_src.shape[0] // TM - lat_blk0
    t = nblk * TM
    alog = jnp.zeros((1, LANE), F32).at[0, :2 * DN_HEADS].set(a_log.reshape(-1))
    dtb = jnp.zeros((1, LANE), F32).at[0, :2 * DN_HEADS].set(dt_bias.reshape(-1))
    widths = (3 * CONV_CH, DN_DIM, ATT_DIM, 2 * ATT_KV_DIM, DN_DIM, DN_DIM, DN_DIM, LANE)
    const = lambda i: (0, 0)
    lat = lambda i: jnp.maximum(i - 1, 0) + lat_blk0
    return pl.pallas_call(
        functools.partial(_in_kernel, nblk=nblk),
        grid=(nblk,),
        in_specs=[
            pl.BlockSpec((TM, D), const),
            pl.BlockSpec((TM, D), lambda i: (lat(i), 0)),
            pl.BlockSpec((8, D), lambda i: (jnp.maximum(lat(i) * r8 - 1, 0), 0)),
            pl.BlockSpec((8, D), lambda i: (jnp.minimum((lat(i) + 1) * r8, last8), 0)),
            pl.BlockSpec((8, N_MOD * D), const),
            pl.BlockSpec(w_main.shape, const),
            pl.BlockSpec((3, 3 * DN_DIM), const),
            pl.BlockSpec((1, LANE), const),
            pl.BlockSpec((1, LANE), const),
            pl.BlockSpec(bd.shape, const),
        ],
        out_specs=[pl.BlockSpec((TM, w), lambda i: (i, 0)) for w in widths],
        out_shape=[jax.ShapeDtypeStruct((t, w), F32) for w in widths],
        compiler_params=_params(("arbitrary",), VMEM_LIMIT),
        name="in_proj",
    )(ctx_src, lat_src, lat_src, lat_src, mod, w_main, dn_conv_w, alog, dtb, bd)


def _dn_block(fwd_refs, bwd_refs, of_ref, ob_ref, s_ref):
    c_ = DN_CHUNK
    nh = DN_HEADS
    nchunks = fwd_refs[0].shape[0] // c_
    nb = nchunks * nh
    rows = lambda g: slice(g * c_, (g + 1) * c_)
    lanes = lambda h: slice(h * DN_HEAD_DIM, (h + 1) * DN_HEAD_DIM)
    ri = lax.broadcasted_iota(jnp.int32, (c_, c_), 0)
    ci = lax.broadcasted_iota(jnp.int32, (c_, c_), 1)
    same_sub = _same_group((c_, c_), DN_SUB)
    eye = jnp.where(ri == ci, 1.0, 0.0)
    shape3 = (2 * nb, c_, c_)
    delta = lax.broadcasted_iota(jnp.int32, shape3, 1) - lax.broadcasted_iota(jnp.int32, shape3, 2)
    delta = jnp.where(lax.broadcasted_iota(jnp.int32, shape3, 0) >= nb, -delta, delta)
    incl = delta >= 0
    strict = delta > 0

    def stack(fn):
        return jnp.stack([fn(d, refs, g, h) for d, refs in enumerate((fwd_refs, bwd_refs))
                          for g in range(nchunks) for h in range(nh)])

    gbs = [refs[3][...] for refs in (fwd_refs, bwd_refs)]
    tris = [jnp.where(ri >= ci, 1.0, 0.0).astype(BF16), jnp.where(ri <= ci, 1.0, 0.0).astype(BF16)]
    gcs = [[_cumsum_rows(tris[d], gbs[d][rows(g), :]) for g in range(nchunks)] for d in range(2)]
    gcts = [[gc.T for gc in gcs[d]] for d in range(2)]
    q = stack(lambda d, refs, g, h: refs[0][rows(g), lanes(h)])
    k = stack(lambda d, refs, g, h: refs[1][rows(g), lanes(h)])
    v = stack(lambda d, refs, g, h: refs[2][rows(g), lanes(h)])
    col = lambda d, h: d * nh + h
    gcol = stack(lambda d, refs, g, h: gcs[d][g][:, col(d, h):col(d, h) + 1])
    grow = stack(lambda d, refs, g, h: gcts[d][g][col(d, h):col(d, h) + 1, :])
    beta = stack(lambda d, refs, g, h: gbs[d][rows(g), 2 * nh + col(d, h):2 * nh + col(d, h) + 1])
    glast = jnp.concatenate([gcol[:nb, c_ - 1:c_, :], gcol[nb:, 0:1, :]], axis=0)
    decay = jnp.where(incl, jnp.exp(jnp.where(incl, gcol - grow, 0.0)), 0.0)
    eg = jnp.exp(gcol)
    kb = k * beta
    a = jnp.where(strict, _bmm(kb, k, _BNT) * decay, 0.0)
    qk = jnp.where(incl, _bmm(q, k, _BNT) * decay, 0.0)
    ad = jnp.where(same_sub, a, 0.0)
    ao = a - ad
    p = eye - ad
    n2 = _bmm(ad, ad)
    p = p + _bmm(p, n2)
    n4 = _bmm(n2, n2)
    p = p + _bmm(p, n4)
    n8 = _bmm(n4, n4)
    dinv = p + _bmm(p, n8)
    m = _bmm(dinv, ao)
    m2 = _bmm(m, m)
    y = _bmm(dinv, jnp.concatenate([v * beta, kb * eg], axis=-1))
    z = y + _bmm(m2, y)
    x = z - _bmm(m, z)
    u = x[:, :, :DN_HEAD_DIM]
    w = x[:, :, DN_HEAD_DIM:]
    qg = q * eg
    kd = k * jnp.exp(glast - gcol)
    gl = jnp.exp(glast)
    s = s_ref[...]
    for t in range(nchunks):
        gf, gr = t, nchunks - 1 - t
        step = lambda a: jnp.concatenate([a[gf * nh:(gf + 1) * nh], a[nb + gr * nh:nb + (gr + 1) * nh]], axis=0)
        v_new = step(u) - _bmm(step(w), s)
        o = _bmm(step(qg), s) + _bmm(step(qk), v_new)
        s = s * step(gl) + _bmm(step(kd), v_new, _BTN)
        for h in range(nh):
            of_ref[rows(gf), lanes(h)] = o[h]
            ob_ref[rows(gr), lanes(h)] = o[nh + h]
    s_ref[...] = s


def _cumsum_rows(tri_bf16, g):
    g2, g1, g0 = _split3(g)
    d = functools.partial(lax.dot_general, dimension_numbers=_NN, preferred_element_type=F32)
    return d(tri_bf16, g2) + (d(tri_bf16, g1) + d(tri_bf16, g0))


def _dn_kernel(qf, kf, vf, gf, qb, kb, vb, gbb, *rest, n_cast):
    cast_in = rest[:n_cast]
    of_ref, ob_ref = rest[n_cast:n_cast + 2]
    cast_out = rest[n_cast + 2:2 * n_cast + 2]
    s_ref = rest[-1]

    @pl.when(pl.program_id(0) == 0)
    def _():
        s_ref[...] = jnp.zeros(s_ref.shape, F32)

    _dn_block((qf, kf, vf, gf), (qb, kb, vb, gbb), of_ref, ob_ref, s_ref)
    for src, dst in zip(cast_in, cast_out):
        dst[...] = src[...].astype(BF16)


def _delta_net(q, k, v, gb, ctx_len, to_bf16=()):
    t = q.shape[0]
    rows = DN_STEP_CHUNKS * DN_CHUNK
    assert ctx_len == rows and t % rows == 0
    nstep = t // rows
    cast_specs = []
    for m in to_bf16:
        rb = -(-m.shape[0] // nstep)
        rb = -(-rb // 16) * 16
        nb = -(-m.shape[0] // rb)
        cast_specs.append(pl.BlockSpec((rb, m.shape[1]), lambda s, nb=nb: (jnp.minimum(s, nb - 1), 0)))

    def fwd(s):
        return (s, 0)

    def bwd(s):
        return (jnp.where(s == 0, 0, nstep - s), 0)

    wide = lambda im: pl.BlockSpec((rows, DN_DIM), im)
    narrow = lambda im: pl.BlockSpec((rows, LANE), im)
    return pl.pallas_call(
        functools.partial(_dn_kernel, n_cast=len(to_bf16)),
        grid=(nstep,),
        in_specs=[wide(fwd), wide(fwd), wide(fwd), narrow(fwd), wide(bwd), wide(bwd), wide(bwd), narrow(bwd)]
        + cast_specs,
        out_specs=[wide(fwd), wide(bwd)] + cast_specs,
        out_shape=[jax.ShapeDtypeStruct((t, DN_DIM), F32)] * 2
        + [jax.ShapeDtypeStruct(m.shape, BF16) for m in to_bf16],
        scratch_shapes=[pltpu.VMEM((2 * DN_HEADS, DN_HEAD_DIM, DN_HEAD_DIM), F32)],
        compiler_params=_params(("arbitrary",), VMEM_LIMIT),
        name="delta_net",
    )(q, k, v, gb, q, k, v, gb, *to_bf16)


def _rope(x, cos, sin):
    w = x.shape[1]
    lane = lax.broadcasted_iota(jnp.int32, x.shape, 1)
    first_half = (lane & (AXIS_DIM - 1)) < (AXIS_DIM // 2)
    swapped = jnp.where(first_half, pltpu.roll(x, w - AXIS_DIM // 2, 1), pltpu.roll(x, AXIS_DIM // 2, 1))
    return x * cos + swapped * sin


LOG2E = 1.4426950408889634


def _softmax_av(s, sink, vals):
    m = jnp.maximum(jnp.max(s, axis=-1, keepdims=True), sink)
    p = jnp.exp2(s - m)
    denom = jnp.sum(p, axis=-1, keepdims=True) + jnp.exp2(sink - m)
    return _mm(p, vals) / denom


def _attend(q, keys, vals, band, sink_all, o_ref, row0=0):
    b = q.shape[0]
    for kvh in range(ATT_KV_HEADS):
        kl = slice(kvh * HEAD_DIM, (kvh + 1) * HEAD_DIM)
        heads = range(kvh * ATT_GROUP, (kvh + 1) * ATT_GROUP)
        qs = jnp.concatenate([q[:, h * HEAD_DIM:(h + 1) * HEAD_DIM] for h in heads], axis=0)
        sink = jnp.concatenate([jnp.broadcast_to(sink_all[0:1, h:h + 1], (b, 1)) for h in heads], axis=0)
        s = _mm(qs, keys[:, kl], _NT)
        if band is not None:
            kb = ATT_BLOCK
            s = jnp.concatenate([jnp.where(band[0], s[:, 0:kb], NEG), s[:, kb:2 * kb],
                                 jnp.where(band[1], s[:, 2 * kb:3 * kb], NEG), s[:, 3 * kb:]], axis=1)
        o = _softmax_av(s, sink * LOG2E, vals[:, kl])
        for g, h in enumerate(heads):
            o_ref[row0:row0 + b, h * HEAD_DIM:(h + 1) * HEAD_DIM] = o[g * b:(g + 1) * b, :]


def _band_valid(first, last):
    b = ATT_BLOCK
    c = lax.broadcasted_iota(jnp.int32, (1, b), 1)
    r = lax.broadcasted_iota(jnp.int32, (ATT_GROUP * b, 1), 0) & (b - 1)
    prev_ok = jnp.where(first, -1, c) >= r
    next_ok = jnp.where(last, b, c) <= r
    return prev_ok, next_ok


def _rope_block(rowtab_ref, coltab_ref, blk):
    rt = rowtab_ref[blk]
    ct = coltab_ref[...]
    reps = ATT_BLOCK // GRID_W
    rows = jnp.concatenate([jnp.broadcast_to(rt[g:g + 1, :], (GRID_W, LANE)) for g in range(reps)], axis=0)
    return rows + jnp.concatenate([ct] * reps, axis=0)


ATT_STEP_BLOCKS = 4


def _attn_ctx_kernel(q_ref, kctx_ref, sink_ref, o_ref):
    kvx = kctx_ref[...]
    scale = HEAD_DIM ** -0.5 * LOG2E
    _attend(q_ref[...] * scale, kvx[:, :ATT_KV_DIM], kvx[:, ATT_KV_DIM:], None, sink_ref[...], o_ref)


def _attn_lat_kernel(*refs, nb):
    ns = ATT_STEP_BLOCKS
    q_refs, kv_refs = refs[:ns], refs[ns:2 * ns + 2]
    kctx_ref, cosr_ref, sinr_ref, cosc_ref, sinc_ref, sink_ref, o_ref = refs[2 * ns + 2:]
    b = ATT_BLOCK
    scale = HEAD_DIM ** -0.5 * LOG2E
    b0 = ns * pl.program_id(0)
    blocks = [jnp.clip(b0 - 1 + t, 0, nb - 1) for t in range(ns + 2)]
    cos = [_rope_block(cosr_ref, cosc_ref, blk) for blk in blocks]
    sin = [_rope_block(sinr_ref, sinc_ref, blk) for blk in blocks]
    kv = [r[...] for r in kv_refs]
    kvx = kctx_ref[...]
    keys = [_rope(t[:, :ATT_KV_DIM], c_, s_) for t, c_, s_ in zip(kv, cos, sin)]
    sink_all = sink_ref[...]
    for sub in range(ns):
        q = q_refs[sub][...]
        q = jnp.concatenate([_rope(q[:, l * LANE:(l + 1) * LANE], cos[1 + sub], sin[1 + sub])
                             for l in range(ATT_DIM // LANE)], axis=1)
        kcat = jnp.concatenate(keys[sub:sub + 3] + [kvx[:, :ATT_KV_DIM]], axis=0)
        vcat = jnp.concatenate([t[:, ATT_KV_DIM:] for t in kv[sub:sub + 3]] + [kvx[:, ATT_KV_DIM:]], axis=0)
        band = _band_valid(first=(b0 + sub == 0), last=(b0 + sub == nb - 1))
        _attend(q * scale, kcat, vcat, band, sink_all, o_ref, row0=sub * b)


def _attention(pq, pkv, rope, sink, ctx_len, with_ctx):
    t = pq.shape[0]
    n = t - ctx_len
    nb = n // ATT_BLOCK
    ns = ATT_STEP_BLOCKS
    assert nb % ns == 0 and ctx_len % ATT_BLOCK == 0
    off = ctx_len // ATT_BLOCK
    whole = lambda a: pl.BlockSpec(a.shape, lambda j: (0,) * a.ndim)
    ctx_kv = pl.BlockSpec((ctx_len, 2 * ATT_KV_DIM), lambda j: (0, 0))
    sink_spec = pl.BlockSpec((8, LANE), lambda j: (0, 0))
    yc_ctx = None
    if with_ctx:
        yc_ctx = pl.pallas_call(
            _attn_ctx_kernel,
            grid=(1,),
            in_specs=[pl.BlockSpec((ctx_len, ATT_DIM), lambda j: (0, 0)), ctx_kv, sink_spec],
            out_specs=pl.BlockSpec((ctx_len, ATT_DIM), lambda j: (0, 0)),
            out_shape=jax.ShapeDtypeStruct((ctx_len, ATT_DIM), F32),
            compiler_params=_params(("arbitrary",), VMEM_LIMIT),
            name="context_attention",
        )(pq, pkv, sink)
    lat = lambda k: jnp.clip(k, 0, nb - 1) + off
    q_specs = [pl.BlockSpec((ATT_BLOCK, ATT_DIM), lambda j, s=s: (lat(ns * j + s), 0)) for s in range(ns)]
    kv_specs = [pl.BlockSpec((ATT_BLOCK, 2 * ATT_KV_DIM), lambda j, s=s: (lat(ns * j - 1 + s), 0))
                for s in range(ns + 2)]
    yc_lat = pl.pallas_call(
        functools.partial(_attn_lat_kernel, nb=nb),
        grid=(nb // ns,),
        in_specs=q_specs + kv_specs + [ctx_kv] + [whole(a) for a in rope] + [sink_spec],
        out_specs=pl.BlockSpec((ns * ATT_BLOCK, ATT_DIM), lambda j: (j, 0)),
        out_shape=jax.ShapeDtypeStruct((n, ATT_DIM), F32),
        compiler_params=_params(("arbitrary",), VMEM_LIMIT),
        name="attention",
    )(*([pq] * ns + [pkv] * (ns + 3) + list(rope) + [sink]))
    return yc_ctx, yc_lat


def _mixfin_kernel(ctx_ref, h_ref, mod_ref, pconv_ref, prev_ref, next_ref, cw_ref, of_ref, ob_ref, z_ref, ng_ref,
                   ycc_ref, ycl_ref, wout_ref, bd_ref, *rest, nblk, blk0, with_router):
    if with_router:
        wr1_ref, wr0_ref, x_ref, hx_ref, lg_ref = rest
    else:
        wg_ref, wu_ref, wd_ref, x_ref = rest
    i = pl.program_id(0) + blk0
    row = _mod_row(mod_ref, i == 0)
    pc = pconv_ref[...]
    u = pc[:, CONV_CH:2 * CONV_CH] * pc[:, 2 * CONV_CH:]
    prow, nrow = _halo_rows(prev_ref, next_ref, i, nblk)
    prow = prow[:, CONV_CH:2 * CONV_CH] * prow[:, 2 * CONV_CH:]
    nrow = nrow[:, CONV_CH:2 * CONV_CH] * nrow[:, 2 * CONV_CH:]
    up, un = _shift_rows(u, prow, nrow)
    cw = cw_ref[...]
    ya = pc[:, :CONV_CH] * (up * cw[0:1, :] + u * cw[1:2, :] + un * cw[2:3, :])
    o = of_ref[...] + ob_ref[...]
    ms = _group_sum(o * o, bd_ref[...]) * (1.0 / DN_HEAD_DIM)
    z = z_ref[...]
    yb = o * lax.rsqrt(ms + EPS) * ng_ref[...] * (z * _sigmoid(z))
    mix = jnp.concatenate([ya, yb, jnp.where(i == 0, ycc_ref[...], ycl_ref[...])], axis=1)
    x = jnp.where(i == 0, ctx_ref[...], h_ref[...]) + row[:, 2 * D:3 * D] * _mm(mix, wout_ref[...])
    hx = _rmsnorm_rows(x) * (1.0 + row[:, 4 * D:5 * D]) + row[:, 3 * D:4 * D]
    if not with_router:
        hb = hx.astype(BF16)
        acc = jnp.zeros((hb.shape[0], D), F32)
        for f in range(0, D_FF, FF_CHUNK):
            fe = min(f + FF_CHUNK, D_FF)
            g = _mm(hb, wg_ref[:, f:fe])
            u_ = _mm(hb, wu_ref[:, f:fe])
            acc = acc + _mm(g * _sigmoid(g) * u_, wd_ref[f:fe, :])
        x_ref[...] = x + row[:, 5 * D:6 * D] * acc
    else:
        x_ref[...] = x
        hx_ref[...] = hx
        h1, h0 = _split2(hx)
        d = functools.partial(lax.dot_general, dimension_numbers=_NN, preferred_element_type=F32)
        lg = d(h1, wr1_ref[...]) + (d(h0, wr1_ref[...]) + d(h1, wr0_ref[...]))
        lane = lax.broadcasted_iota(jnp.int32, lg.shape, 1)
        lanef = lane.astype(F32)
        lg = jnp.where(lane < N_EXPERTS, lg, -jnp.inf)
        m1 = jnp.max(lg, axis=-1, keepdims=True)
        i1 = jnp.min(jnp.where(lg == m1, lanef, float(LANE)), axis=-1, keepdims=True)
        rest = jnp.where(lanef == i1, -jnp.inf, lg)
        m2 = jnp.max(rest, axis=-1, keepdims=True)
        i2 = jnp.min(jnp.where(rest == m2, lanef, float(LANE)), axis=-1, keepdims=True)
        e2 = jnp.exp(m2 - m1)
        g1 = 1.0 / (1.0 + e2)
        lg_ref[...] = jnp.where(lane == 0, g1, jnp.where(lane == 1, e2 * g1, jnp.where(lane == 2, i1, i2)))


def _mixer_finish(ctx_src, lat_src, lat_blk0, mod, pconv, conv_w, o_f, o_b, pz, norm_g, yc, w_out, bd, ctx_len,
                  with_ctx, router=None, ffn=None):
    assert (router is None) != (ffn is None)
    t = pconv.shape[0]
    nblk = t // TM
    blk0 = 0 if with_ctx else ctx_len // TM
    rows = t - blk0 * TM
    r8 = TM // 8
    w = pconv.shape[1]
    cur = lambda i: (i + blk0, 0)
    out_cur = lambda i: (i, 0)
    const = lambda i: (0, 0)
    ng = jnp.tile(norm_g.reshape(1, DN_HEAD_DIM), (1, DN_HEADS))
    in_specs = [
        pl.BlockSpec((TM, D), const),
        pl.BlockSpec((TM, D), lambda i: (jnp.maximum(i + blk0 - 1, 0) + lat_blk0, 0)),
        pl.BlockSpec((8, N_MOD * D), const),
        pl.BlockSpec((TM, w), cur),
        pl.BlockSpec((8, w), lambda i: (jnp.maximum((i + blk0) * r8 - 1, 0), 0)),
        pl.BlockSpec((8, w), lambda i: (jnp.minimum((i + blk0 + 1) * r8, t // 8 - 1), 0)),
        pl.BlockSpec((3, CONV_CH), const),
        pl.BlockSpec((TM, DN_DIM), cur),
        pl.BlockSpec((TM, DN_DIM), cur),
        pl.BlockSpec((TM, DN_DIM), cur),
        pl.BlockSpec((1, DN_DIM), const),
        pl.BlockSpec((TM, ATT_DIM), const),
        pl.BlockSpec((TM, ATT_DIM), lambda i: (jnp.maximum(i + blk0 - 1, 0), 0)),
        pl.BlockSpec((MIX_DIM, D), const),
        pl.BlockSpec(bd.shape, const),
    ]
    yc_ctx, yc_lat = yc
    if yc_ctx is None:
        assert not with_ctx
        yc_ctx = yc_lat
    args = [ctx_src, lat_src, mod, pconv, pconv, pconv, conv_w, o_f, o_b, pz, ng, yc_ctx, yc_lat, w_out, bd]
    out_specs = [pl.BlockSpec((TM, D), out_cur)]
    out_shape = [jax.ShapeDtypeStruct((rows, D), F32)]
    if router is not None:
        in_specs += [pl.BlockSpec((D, LANE), const)] * 2
        args += list(router)
        out_specs += [pl.BlockSpec((TM, D), out_cur), pl.BlockSpec((TM, LANE), out_cur)]
        out_shape += [jax.ShapeDtypeStruct((rows, D), F32), jax.ShapeDtypeStruct((rows, LANE), F32)]
    else:
        once = dict(pipeline_mode=pl.Buffered(1))
        in_specs += [pl.BlockSpec(w_.shape, const, **once) for w_ in ffn]
        args += list(ffn)
    return pl.pallas_call(
        functools.partial(_mixfin_kernel, nblk=nblk, blk0=blk0, with_router=router is not None),
        grid=(rows // TM,),
        in_specs=in_specs,
        out_specs=out_specs,
        out_shape=out_shape,
        compiler_params=_params(("arbitrary",), VMEM_LIMIT),
        name="mixer_finish",
    )(*args)


def _moe_kernel(be_ref, nu_ref, xs_ref, wg_ref, wu_ref, wd_ref, y_ref, acc_ref):
    b = pl.program_id(0)
    f = pl.program_id(1)

    @pl.when(b < nu_ref[0])
    def _():
        xs = xs_ref[...].astype(BF16)
        part = jnp.zeros((MOE_TM, D), F32)
        for c in range(0, MOE_TF, MOE_SUB):
            g = _mm(xs, wg_ref[0, :, c:c + MOE_SUB])
            u = _mm(xs, wu_ref[0, :, c:c + MOE_SUB])
            part = part + _mm(g * _sigmoid(g) * u, wd_ref[0, c:c + MOE_SUB, :])

        @pl.when(f == 0)
        def _():
            acc_ref[...] = part

        @pl.when(f != 0)
        def _():
            acc_ref[...] += part

        @pl.when(f == pl.num_programs(1) - 1)
        def _():
            y_ref[...] = acc_ref[...]

    @pl.when(b >= nu_ref[0])
    def _():
        y_ref[...] = jnp.zeros(y_ref.shape, F32)


def _moe_experts(xs, blk_e, n_used, wg, wu, wd):
    cap = xs.shape[0]
    nblk = cap // MOE_TM
    nf = D_FF_EXPERT // MOE_TF

    def fidx(b, f, nu):
        return jnp.where(b < nu[0], f, nf - 1)

    grid_spec = pltpu.PrefetchScalarGridSpec(
        num_scalar_prefetch=2,
        grid=(nblk, nf),
        in_specs=[
            pl.BlockSpec((MOE_TM, D), lambda b, f, be, nu: (b, 0)),
            pl.BlockSpec((1, D, MOE_TF), lambda b, f, be, nu: (be[b], 0, fidx(b, f, nu))),
            pl.BlockSpec((1, D, MOE_TF), lambda b, f, be, nu: (be[b], 0, fidx(b, f, nu))),
            pl.BlockSpec((1, MOE_TF, D), lambda b, f, be, nu: (be[b], fidx(b, f, nu), 0)),
        ],
        out_specs=pl.BlockSpec((MOE_TM, D), lambda b, f, be, nu: (b, 0)),
        scratch_shapes=[pltpu.VMEM((MOE_TM, D), F32)],
    )
    return pl.pallas_call(
        _moe_kernel,
        grid_spec=grid_spec,
        out_shape=jax.ShapeDtypeStruct((cap, D), F32),
        compiler_params=_params(("arbitrary", "arbitrary"), VMEM_LIMIT),
        name="moe_experts",
    )(blk_e, n_used, xs, wg, wu, wd)


def _moe_route(choices):
    n = choices[0].shape[0]
    a = n * TOP_K
    flat_e = jnp.concatenate(choices)
    onehot = (flat_e[:, None] == jnp.arange(N_EXPERTS, dtype=flat_e.dtype)[None, :]).astype(jnp.int32)
    counts = jnp.sum(onehot, axis=0)
    padded = (counts + MOE_TM - 1) // MOE_TM * MOE_TM
    pad_ends = jnp.cumsum(padded)
    pad_starts = pad_ends - padded
    dest = jnp.sum(onehot * (jnp.cumsum(onehot, axis=0) - onehot + pad_starts[None, :]), axis=1)
    cap = a + N_EXPERTS * MOE_TM
    nblk = cap // MOE_TM
    row_tok = (jnp.arange(cap, dtype=jnp.int32) % n).at[dest].set(
        jnp.arange(a, dtype=jnp.int32) % n, unique_indices=True, mode="promise_in_bounds")
    blk_start = jnp.arange(nblk, dtype=jnp.int32) * MOE_TM
    blk_e = jnp.minimum(jnp.sum((pad_ends[None, :] <= blk_start[:, None]).astype(jnp.int32), axis=1),
                        N_EXPERTS - 1)
    n_used = (pad_ends[-1] // MOE_TM).astype(jnp.int32).reshape(1)
    last_e = blk_e[jnp.maximum(n_used[0] - 1, 0)]
    blk_e = jnp.where(jnp.arange(nblk) < n_used[0], blk_e, last_e)
    return [dest[s * n:(s + 1) * n] for s in range(TOP_K)], row_tok, blk_e, n_used


def _final_kernel(x_ref, y0_ref, y1_ref, gt_ref, mod_ref, fg_ref, o_ref):
    mod = mod_ref[...]
    gt = gt_ref[...]
    f = gt[:, 0:1] * y0_ref[...] + gt[:, 1:2] * y1_ref[...]
    x = x_ref[...] + mod[0:1, 5 * D:6 * D] * f
    o_ref[...] = _rmsnorm_rows(x) * fg_ref[...]


def _moe_combine_final(x, y0, y1, gt, mod, final_g):
    n = x.shape[0]
    row = lambda i: (i, 0)
    const = lambda i: (0, 0)
    return pl.pallas_call(
        _final_kernel,
        grid=(n // TM,),
        in_specs=[pl.BlockSpec((TM, D), row)] * 3
        + [pl.BlockSpec((TM, LANE), row), pl.BlockSpec((8, N_MOD * D), const), pl.BlockSpec((1, D), const)],
        out_specs=pl.BlockSpec((TM, D), row),
        out_shape=jax.ShapeDtypeStruct((n, D), F32),
        compiler_params=_params(("arbitrary",), VMEM_LIMIT),
        name="moe_combine_final",
    )(x, y0, y1, gt, mod, final_g.reshape(1, D))


def _rope_tables(n):
    lane = jnp.arange(LANE, dtype=jnp.int32) % HEAD_DIM
    inv = ROPE_BASE ** (-jnp.arange(0, AXIS_DIM, 2, dtype=F32) / AXIS_DIM)
    freq = inv[lane % (AXIS_DIM // 2)]
    row_axis = (lane // AXIS_DIM) == 0
    sign = jnp.where((lane % AXIS_DIM) < AXIS_DIM // 2, -1.0, 1.0)
    reps = ATT_BLOCK // GRID_W

    def tables(count, on_axis):
        ang = jnp.arange(count, dtype=F32)[:, None] * freq[None, :]
        return jnp.where(on_axis, jnp.cos(ang), 0.0), jnp.where(on_axis, jnp.sin(ang) * sign, 0.0)

    by_block = lambda t: jnp.pad(t.reshape(-1, reps, LANE), ((0, 0), (0, 8 - reps), (0, 0)))
    cosr, sinr = tables(n // GRID_W, row_axis)
    cosc, sinc = tables(GRID_W, ~row_axis)
    return by_block(cosr), by_block(sinr), cosc, sinc


def _prep_w_in(w):
    pad = jnp.zeros((D, LANE - N_AB), w.dtype)
    return jnp.concatenate([w[:, :_C_Z], w[:, _C_A:_C_Q], pad, w[:, _C_Z:_C_A], w[:, _C_Q:_C_END]],
                           axis=1).astype(BF16)


def kernel(x, c, ctx, c_ctx, w_mod, b_mod, w_in, w_out, conv_w, dn_conv_w, dn_a_log, dn_dt_bias, dn_norm_g,
           attn_sink, ffn_w_gate, ffn_w_up, ffn_w_down, moe_router, moe_w_gate, moe_w_up, moe_w_down,
           final_norm_g):
    bsz, n, d = x.shape
    ctx_len = ctx.shape[1]
    depth = w_in.shape[0]
    assert bsz == 1 and d == D and ctx_len == TM and n % TM == 0 and n % GRID_W == 0
    rope = _rope_tables(n)
    mods = _mod_vectors(c, c_ctx, w_mod, b_mod)
    bd = _head_blockdiag(DN_DIM, DN_HEAD_DIM)
    stream = (ctx[0], x[0], 0)
    for layer in range(depth):
        last = layer == depth - 1
        mod = mods[layer]
        pconv, pz, pq, pkv, qn, kn, vv, gb = _in_proj(*stream, mod, _prep_w_in(w_in[layer]), dn_conv_w[layer],
                                                      dn_a_log[layer], dn_dt_bias[layer], bd)
        to_bf16 = []
        if layer % 2 == 0:
            to_bf16 += [w[layer // 2] for w in (ffn_w_gate, ffn_w_up, ffn_w_down)]
        if (layer + 1) % 2 == 1 and layer + 1 < depth:
            j = (layer + 1) // 2
            to_bf16 += [w[j].reshape(-1, w.shape[-1]) for w in (moe_w_gate, moe_w_up, moe_w_down)]
        o_f, o_b, *cast = _delta_net(qn, kn, vv, gb, ctx_len, tuple(to_bf16))
        if layer % 2 == 0:
            ffn_bf16, cast = cast[:3], cast[3:]
        if cast:
            moe_bf16 = [c.reshape(w.shape[1:]) for c, w in zip(cast, (moe_w_gate, moe_w_up, moe_w_down))]
        sink = jnp.zeros((8, LANE), F32).at[0, :ATT_HEADS].set(attn_sink[layer])
        yc = _attention(pq, pkv, rope, sink, ctx_len, with_ctx=not last)
        router = None
        if layer % 2 == 1:
            wr = jnp.zeros((D, LANE), F32).at[:, :N_EXPERTS].set(moe_router[layer // 2])
            wr1 = wr.astype(BF16)
            router = (wr1, (wr - wr1.astype(F32)).astype(BF16))
        outs = _mixer_finish(*stream, mod, pconv, conv_w[layer], o_f, o_b, pz, dn_norm_g[layer], yc,
                             w_out[layer].astype(BF16), bd, ctx_len, with_ctx=not last, router=router,
                             ffn=ffn_bf16 if layer % 2 == 0 else None)
        if layer % 2 == 0:
            assert not last
            h, = outs
            stream = (h, h, ctx_len // TM)
        else:
            assert last
            x1, hx, route = outs
            dest, row_tok, blk_e, n_used = _moe_route([route[:, 2 + s].astype(jnp.int32) for s in range(TOP_K)])
            take = lambda rows_, idx: rows_.at[idx].get(mode="promise_in_bounds")
            y = _moe_experts(take(hx, row_tok), blk_e, n_used, *moe_bf16)
            h = _moe_combine_final(x1, take(y, dest[0]), take(y, dest[1]), route, mod, final_norm_g)
    return h.reshape(bsz, n, d)
```

```python
import functools

import jax
import jax.numpy as jnp
from jax import lax
from jax.experimental import pallas as pl
from jax.experimental.pallas import tpu as pltpu

F32 = jnp.float32
BF16 = jnp.bfloat16

D = 1024
N_MOD = 6
EPS = 1e-6
NEG = -1e30
GRID_W = 64

CONV_CH = 256
DN_HEADS = 6
DN_HEAD_DIM = 64
DN_DIM = DN_HEADS * DN_HEAD_DIM
DN_CHUNK = 64
DN_SUB = 16
DN_STEP_CHUNKS = 4
ATT_HEADS = 6
ATT_KV_HEADS = 2
ATT_GROUP = ATT_HEADS // ATT_KV_HEADS
HEAD_DIM = 64
ATT_DIM = ATT_HEADS * HEAD_DIM
ATT_KV_DIM = ATT_KV_HEADS * HEAD_DIM
ATT_BLOCK = 128
ROPE_BASE = 10000.0
AXIS_DIM = HEAD_DIM // 2
MIX_DIM = CONV_CH + DN_DIM + ATT_DIM

D_FF = 2816
N_EXPERTS = 8
TOP_K = 2
D_FF_EXPERT = 3584

TM = 256
FF_CHUNK = 1024
MOE_TM = 512
MOE_TF = 1792
MOE_SUB = 256
LANE = 128
VMEM_LIMIT = 56 * 1024 * 1024

_C_QKV = 3 * CONV_CH
_C_Z = _C_QKV + 3 * DN_DIM
_C_A = _C_Z + DN_DIM
_C_Q = _C_A + 4 * DN_HEADS
_C_K = _C_Q + ATT_DIM
_C_V = _C_K + ATT_KV_DIM
_C_END = _C_V + ATT_KV_DIM
N_AB = 4 * DN_HEADS


def _params(sem=None, vmem=None):
    kw = {}
    if sem is not None:
        kw["dimension_semantics"] = sem
    if vmem is not None:
        kw["vmem_limit_bytes"] = vmem
    return pltpu.CompilerParams(**kw)


def _split2(a):
    hi = a.astype(BF16)
    lo = (a - hi.astype(F32)).astype(BF16)
    return hi, lo


def _split3(a):
    hi = a.astype(BF16)
    r = a - hi.astype(F32)
    mid = r.astype(BF16)
    lo = (r - mid.astype(F32)).astype(BF16)
    return hi, mid, lo


_NN = (((1,), (0,)), ((), ()))
_NT = (((1,), (1,)), ((), ()))


def _mm(a, b, dims=_NN):
    return lax.dot_general(a.astype(BF16), b.astype(BF16), dims, preferred_element_type=F32)


def _mm3(a, b, dims=_NN):
    a1, a0 = _split2(a)
    b1, b0 = _split2(b)
    d = functools.partial(lax.dot_general, dimension_numbers=dims, preferred_element_type=F32)
    return d(a1, b1) + (d(a1, b0) + d(a0, b1))


_BNN = (((2,), (1,)), ((0,), (0,)))
_BNT = (((2,), (2,)), ((0,), (0,)))
_BTN = (((1,), (1,)), ((0,), (0,)))


def _bmm(a, b, dims=_BNN):
    return lax.dot_general(a.astype(BF16), b.astype(BF16), dims, preferred_element_type=F32)


def _sigmoid(x):
    return 1.0 / (1.0 + jnp.exp(-x))


def _softplus(x):
    return jnp.maximum(x, 0.0) + jnp.log1p(jnp.exp(-jnp.abs(x)))


def _mod_row(mod_ref, is_ctx):
    mod = mod_ref[...]
    return jnp.where(is_ctx, mod[1:2, :], mod[0:1, :])


def _rmsnorm_rows(x):
    return x * lax.rsqrt(jnp.mean(x * x, axis=-1, keepdims=True) + EPS)


def _shift_rows(u, prow, nrow):
    n = u.shape[0]
    rid = lax.broadcasted_iota(jnp.int32, u.shape, 0)
    up = jnp.where(rid == 0, prow, pltpu.roll(u, 1, 0))
    un = jnp.where(rid == n - 1, nrow, pltpu.roll(u, n - 1, 0))
    return up, un


def _same_group(shape, group):
    sh = group.bit_length() - 1
    assert 1 << sh == group
    return (lax.broadcasted_iota(jnp.int32, shape, 0) >> sh) == (lax.broadcasted_iota(jnp.int32, shape, 1) >> sh)


def _head_blockdiag(n, group):
    g = jnp.arange(n, dtype=jnp.int32) // group
    return (g[:, None] == g[None, :]).astype(BF16)


def _group_sum(t, bd):
    return _mm(t, bd)


MOD_TN = 1536


def _mod_kernel(s_ref, w_ref, b_ref, o_ref):
    s = s_ref[...]
    s = s * _sigmoid(s)
    o_ref[0] = _mm3(s, w_ref[0]) + b_ref[0]


def _mod_vectors(c, c_ctx, w_mod, b_mod):
    depth = w_mod.shape[0]
    s = jnp.zeros((8, D), F32).at[0].set(c[0]).at[1].set(c_ctx)
    return pl.pallas_call(
        _mod_kernel,
        grid=(depth, N_MOD * D // MOD_TN),
        in_specs=[
            pl.BlockSpec((8, D), lambda l, j: (0, 0)),
            pl.BlockSpec((1, D, MOD_TN), lambda l, j: (l, 0, j)),
            pl.BlockSpec((1, 1, MOD_TN), lambda l, j: (l, 0, j)),
        ],
        out_specs=pl.BlockSpec((1, 8, MOD_TN), lambda l, j: (l, 0, j)),
        out_shape=jax.ShapeDtypeStruct((depth, 8, N_MOD * D), F32),
        compiler_params=_params(("arbitrary", "arbitrary"), VMEM_LIMIT),
        name="mod_vectors",
    )(s, w_mod, b_mod.reshape(depth, 1, N_MOD * D))


def _halo_valid(i, nblk):
    return jnp.logical_and(i != 0, i != 1), jnp.logical_and(i != 0, i != nblk - 1)


def _halo_rows(prev_ref, next_ref, i, nblk):
    pvalid, nvalid = _halo_valid(i, nblk)
    prow = jnp.where(pvalid, prev_ref[7:8, :], 0.0)
    nrow = jnp.where(nvalid, next_ref[0:1, :], 0.0)
    return prow, nrow


def _in_kernel(ctx_ref, h_ref, hprev_ref, hnext_ref, mod_ref, w_ref, cw_ref, alog_ref, dtb_ref, bd_ref,
               pconv_ref, pz_ref, pq_ref, pkv_ref, q_ref, k_ref, v_ref, gb_ref, *, nblk):
    i = pl.program_id(0)
    row = _mod_row(mod_ref, i == 0)
    norm_mod = lambda x: _rmsnorm_rows(x) * (1.0 + row[:, D:2 * D]) + row[:, 0:D]
    h1 = norm_mod(jnp.where(i == 0, ctx_ref[...], h_ref[...])).astype(BF16)
    halo = norm_mod(jnp.concatenate([hprev_ref[...], hnext_ref[...]], axis=0)).astype(BF16)
    d = functools.partial(lax.dot_general, dimension_numbers=_NN, preferred_element_type=F32)
    tm = h1.shape[0]
    c0 = 3 * CONV_CH
    c1 = c0 + 3 * DN_DIM
    c2 = c1 + LANE
    c3 = c2 + DN_DIM + ATT_DIM
    pconv_ref[...] = d(h1, w_ref[:, 0:c0])
    zq = d(h1, w_ref[:, c2:c3])
    pz_ref[...] = zq[:, 0:DN_DIM]
    pq_ref[...] = zq[:, DN_DIM:]
    pkv_ref[...] = d(h1, w_ref[:, c3:c3 + 2 * ATT_KV_DIM])
    qkv_ab = d(jnp.concatenate([h1, halo], axis=0), w_ref[:, c0:c2])
    qkv = qkv_ab[:, 0:3 * DN_DIM]
    ab = qkv_ab[0:tm, 3 * DN_DIM:]

    u = qkv[0:tm, :]
    pvalid, nvalid = _halo_valid(i, nblk)
    prow = jnp.where(pvalid, qkv[tm + 7:tm + 8, :], 0.0)
    nrow = jnp.where(nvalid, qkv[tm + 8:tm + 9, :], 0.0)
    up, un = _shift_rows(u, prow, nrow)
    cw = cw_ref[...]
    y = up * cw[0:1, :] + u * cw[1:2, :] + un * cw[2:3, :]
    y = y * _sigmoid(y)
    q = y[:, 0:DN_DIM]
    k = y[:, DN_DIM:2 * DN_DIM]
    bd = bd_ref[...]
    q_ref[...] = q * lax.rsqrt(_group_sum(q * q, bd) + 1e-6) * (DN_HEAD_DIM ** -0.5)
    k_ref[...] = k * lax.rsqrt(_group_sum(k * k, bd) + 1e-6)
    v_ref[...] = y[:, 2 * DN_DIM:3 * DN_DIM]
    g = -jnp.exp(alog_ref[...]) * _softplus(ab + dtb_ref[...])
    lane = lax.broadcasted_iota(jnp.int32, ab.shape, 1)
    gb_ref[...] = jnp.where(lane < 2 * DN_HEADS, g, _sigmoid(ab))


def _in_proj(ctx_src, lat_src, lat_blk0, mod, w_main, dn_conv_w, a_log, dt_bias, bd):
    r8 = TM // 8
    last8 = lat_src.shape[0] // 8 - 1
    nblk = 1 + lat_src.shape[0] // TM - lat_blk0
    t = nblk * TM
    alog = jnp.zeros((1, LANE), F32).at[0, :2 * DN_HEADS].set(a_log.reshape(-1))
    dtb = jnp.zeros((1, LANE), F32).at[0, :2 * DN_HEADS].set(dt_bias.reshape(-1))
    widths = (3 * CONV_CH, DN_DIM, ATT_DIM, 2 * ATT_KV_DIM, DN_DIM, DN_DIM, DN_DIM, LANE)
    const = lambda i: (0, 0)
    lat = lambda i: jnp.maximum(i - 1, 0) + lat_blk0
    return pl.pallas_call(
        functools.partial(_in_kernel, nblk=nblk),
        grid=(nblk,),
        in_specs=[
            pl.BlockSpec((TM, D), const),
            pl.BlockSpec((TM, D), lambda i: (lat(i), 0)),
            pl.BlockSpec((8, D), lambda i: (jnp.maximum(lat(i) * r8 - 1, 0), 0)),
            pl.BlockSpec((8, D), lambda i: (jnp.minimum((lat(i) + 1) * r8, last8), 0)),
            pl.BlockSpec((8, N_MOD * D), const),
            pl.BlockSpec(w_main.shape, const),
            pl.BlockSpec((3, 3 * DN_DIM), const),
            pl.BlockSpec((1, LANE), const),
            pl.BlockSpec((1, LANE), const),
            pl.BlockSpec(bd.shape, const),
        ],
        out_specs=[pl.BlockSpec((TM, w), lambda i: (i, 0)) for w in widths],
        out_shape=[jax.ShapeDtypeStruct((t, w), F32) for w in widths],
        compiler_params=_params(("arbitrary",), VMEM_LIMIT),
        name="in_proj",
    )(ctx_src, lat_src, lat_src, lat_src, mod, w_main, dn_conv_w, alog, dtb, bd)


def _dn_block(fwd_refs, bwd_refs, of_ref, ob_ref, s_ref):
    c_ = DN_CHUNK
    nh = DN_HEADS
    nchunks = fwd_refs[0].shape[0] // c_
    nb = nchunks * nh
    rows = lambda g: slice(g * c_, (g + 1) * c_)
    lanes = lambda h: slice(h * DN_HEAD_DIM, (h + 1) * DN_HEAD_DIM)
    ri = lax.broadcasted_iota(jnp.int32, (c_, c_), 0)
    ci = lax.broadcasted_iota(jnp.int32, (c_, c_), 1)
    same_sub = _same_group((c_, c_), DN_SUB)
    eye = jnp.where(ri == ci, 1.0, 0.0)
    shape3 = (2 * nb, c_, c_)
    delta = lax.broadcasted_iota(jnp.int32, shape3, 1) - lax.broadcasted_iota(jnp.int32, shape3, 2)
    delta = jnp.where(lax.broadcasted_iota(jnp.int32, shape3, 0) >= nb, -delta, delta)
    incl = delta >= 0
    strict = delta > 0

    def stack(fn):
        return jnp.stack([fn(d, refs, g, h) for d, refs in enumerate((fwd_refs, bwd_refs))
                          for g in range(nchunks) for h in range(nh)])

    gbs = [refs[3][...] for refs in (fwd_refs, bwd_refs)]
    tris = [jnp.where(ri >= ci, 1.0, 0.0).astype(BF16), jnp.where(ri <= ci, 1.0, 0.0).astype(BF16)]
    gcs = [[_cumsum_rows(tris[d], gbs[d][rows(g), :]) for g in range(nchunks)] for d in range(2)]
    gcts = [[gc.T for gc in gcs[d]] for d in range(2)]
    q = stack(lambda d, refs, g, h: refs[0][rows(g), lanes(h)])
    k = stack(lambda d, refs, g, h: refs[1][rows(g), lanes(h)])
    v = stack(lambda d, refs, g, h: refs[2][rows(g), lanes(h)])
    col = lambda d, h: d * nh + h
    gcol = stack(lambda d, refs, g, h: gcs[d][g][:, col(d, h):col(d, h) + 1])
    grow = stack(lambda d, refs, g, h: gcts[d][g][col(d, h):col(d, h) + 1, :])
    beta = stack(lambda d, refs, g, h: gbs[d][rows(g), 2 * nh + col(d, h):2 * nh + col(d, h) + 1])
    glast = jnp.concatenate([gcol[:nb, c_ - 1:c_, :], gcol[nb:, 0:1, :]], axis=0)
    decay = jnp.where(incl, jnp.exp(jnp.where(incl, gcol - grow, 0.0)), 0.0)
    eg = jnp.exp(gcol)
    kb = k * beta
    a = jnp.where(strict, _bmm(kb, k, _BNT) * decay, 0.0)
    qk = jnp.where(incl, _bmm(q, k, _BNT) * decay, 0.0)
    ad = jnp.where(same_sub, a, 0.0)
    ao = a - ad
    p = eye - ad
    n2 = _bmm(ad, ad)
    p = p + _bmm(p, n2)
    n4 = _bmm(n2, n2)
    p = p + _bmm(p, n4)
    n8 = _bmm(n4, n4)
    dinv = p + _bmm(p, n8)
    m = _bmm(dinv, ao)
    m2 = _bmm(m, m)
    y = _bmm(dinv, jnp.concatenate([v * beta, kb * eg], axis=-1))
    z = y + _bmm(m2, y)
    x = z - _bmm(m, z)
    u = x[:, :, :DN_HEAD_DIM]
    w = x[:, :, DN_HEAD_DIM:]
    qg = q * eg
    kd = k * jnp.exp(glast - gcol)
    gl = jnp.exp(glast)
    s = s_ref[...]
    for t in range(nchunks):
        gf, gr = t, nchunks - 1 - t
        step = lambda a: jnp.concatenate([a[gf * nh:(gf + 1) * nh], a[nb + gr * nh:nb + (gr + 1) * nh]], axis=0)
        v_new = step(u) - _bmm(step(w), s)
        o = _bmm(step(qg), s) + _bmm(step(qk), v_new)
        s = s * step(gl) + _bmm(step(kd), v_new, _BTN)
        for h in range(nh):
            of_ref[rows(gf), lanes(h)] = o[h]
            ob_ref[rows(gr), lanes(h)] = o[nh + h]
    s_ref[...] = s


def _cumsum_rows(tri_bf16, g):
    g2, g1, g0 = _split3(g)
    d = functools.partial(lax.dot_general, dimension_numbers=_NN, preferred_element_type=F32)
    return d(tri_bf16, g2) + (d(tri_bf16, g1) + d(tri_bf16, g0))


def _dn_kernel(qf, kf, vf, gf, qb, kb, vb, gbb, *rest, n_cast):
    cast_in = rest[:n_cast]
    of_ref, ob_ref = rest[n_cast:n_cast + 2]
    cast_out = rest[n_cast + 2:2 * n_cast + 2]
    s_ref = rest[-1]

    @pl.when(pl.program_id(0) == 0)
    def _():
        s_ref[...] = jnp.zeros(s_ref.shape, F32)

    _dn_block((qf, kf, vf, gf), (qb, kb, vb, gbb), of_ref, ob_ref, s_ref)
    for src, dst in zip(cast_in, cast_out):
        dst[...] = src[...].astype(BF16)


def _delta_net(q, k, v, gb, ctx_len, to_bf16=()):
    t = q.shape[0]
    rows = DN_STEP_CHUNKS * DN_CHUNK
    assert ctx_len == rows and t % rows == 0
    nstep = t // rows
    cast_specs = []
    for m in to_bf16:
        rb = -(-m.shape[0] // nstep)
        rb = -(-rb // 16) * 16
        nb = -(-m.shape[0] // rb)
        cast_specs.append(pl.BlockSpec((rb, m.shape[1]), lambda s, nb=nb: (jnp.minimum(s, nb - 1), 0)))

    def fwd(s):
        return (s, 0)

    def bwd(s):
        return (jnp.where(s == 0, 0, nstep - s), 0)

    wide = lambda im: pl.BlockSpec((rows, DN_DIM), im)
    narrow = lambda im: pl.BlockSpec((rows, LANE), im)
    return pl.pallas_call(
        functools.partial(_dn_kernel, n_cast=len(to_bf16)),
        grid=(nstep,),
        in_specs=[wide(fwd), wide(fwd), wide(fwd), narrow(fwd), wide(bwd), wide(bwd), wide(bwd), narrow(bwd)]
        + cast_specs,
        out_specs=[wide(fwd), wide(bwd)] + cast_specs,
        out_shape=[jax.ShapeDtypeStruct((t, DN_DIM), F32)] * 2
        + [jax.ShapeDtypeStruct(m.shape, BF16) for m in to_bf16],
        scratch_shapes=[pltpu.VMEM((2 * DN_HEADS, DN_HEAD_DIM, DN_HEAD_DIM), F32)],
        compiler_params=_params(("arbitrary",), VMEM_LIMIT),
        name="delta_net",
    )(q, k, v, gb, q, k, v, gb, *to_bf16)


def _rope(x, cos, sin):
    w = x.shape[1]
    lane = lax.broadcasted_iota(jnp.int32, x.shape, 1)
    first_half = (lane & (AXIS_DIM - 1)) < (AXIS_DIM // 2)
    swapped = jnp.where(first_half, pltpu.roll(x, w - AXIS_DIM // 2, 1), pltpu.roll(x, AXIS_DIM // 2, 1))
    return x * cos + swapped * sin


LOG2E = 1.4426950408889634


def _softmax_av(s, sink, vals):
    m = jnp.maximum(jnp.max(s, axis=-1, keepdims=True), sink)
    p = jnp.exp2(s - m)
    denom = jnp.sum(p, axis=-1, keepdims=True) + jnp.exp2(sink - m)
    return _mm(p, vals) / denom


def _attend(q, keys, vals, band, sink_all, o_ref, row0=0):
    b = q.shape[0]
    for kvh in range(ATT_KV_HEADS):
        kl = slice(kvh * HEAD_DIM, (kvh + 1) * HEAD_DIM)
        heads = range(kvh * ATT_GROUP, (kvh + 1) * ATT_GROUP)
        qs = jnp.concatenate([q[:, h * HEAD_DIM:(h + 1) * HEAD_DIM] for h in heads], axis=0)
        sink = jnp.concatenate([jnp.broadcast_to(sink_all[0:1, h:h + 1], (b, 1)) for h in heads], axis=0)
        s = _mm(qs, keys[:, kl], _NT)
        if band is not None:
            kb = ATT_BLOCK
            s = jnp.concatenate([jnp.where(band[0], s[:, 0:kb], NEG), s[:, kb:2 * kb],
                                 jnp.where(band[1], s[:, 2 * kb:3 * kb], NEG), s[:, 3 * kb:]], axis=1)
        o = _softmax_av(s, sink * LOG2E, vals[:, kl])
        for g, h in enumerate(heads):
            o_ref[row0:row0 + b, h * HEAD_DIM:(h + 1) * HEAD_DIM] = o[g * b:(g + 1) * b, :]


def _band_valid(first, last):
    b = ATT_BLOCK
    c = lax.broadcasted_iota(jnp.int32, (1, b), 1)
    r = lax.broadcasted_iota(jnp.int32, (ATT_GROUP * b, 1), 0) & (b - 1)
    prev_ok = jnp.where(first, -1, c) >= r
    next_ok = jnp.where(last, b, c) <= r
    return prev_ok, next_ok


def _rope_block(rowtab_ref, coltab_ref, blk):
    rt = rowtab_ref[blk]
    ct = coltab_ref[...]
    reps = ATT_BLOCK // GRID_W
    rows = jnp.concatenate([jnp.broadcast_to(rt[g:g + 1, :], (GRID_W, LANE)) for g in range(reps)], axis=0)
    return rows + jnp.concatenate([ct] * reps, axis=0)


ATT_STEP_BLOCKS = 4


def _attn_ctx_kernel(q_ref, kctx_ref, sink_ref, o_ref):
    kvx = kctx_ref[...]
    scale = HEAD_DIM ** -0.5 * LOG2E
    _attend(q_ref[...] * scale, kvx[:, :ATT_KV_DIM], kvx[:, ATT_KV_DIM:], None, sink_ref[...], o_ref)


def _attn_lat_kernel(*refs, nb):
    ns = ATT_STEP_BLOCKS
    q_refs, kv_refs = refs[:ns], refs[ns:2 * ns + 2]
    kctx_ref, cosr_ref, sinr_ref, cosc_ref, sinc_ref, sink_ref, o_ref = refs[2 * ns + 2:]
    b = ATT_BLOCK
    scale = HEAD_DIM ** -0.5 * LOG2E
    b0 = ns * pl.program_id(0)
    blocks = [jnp.clip(b0 - 1 + t, 0, nb - 1) for t in range(ns + 2)]
    cos = [_rope_block(cosr_ref, cosc_ref, blk) for blk in blocks]
    sin = [_rope_block(sinr_ref, sinc_ref, blk) for blk in blocks]
    kv = [r[...] for r in kv_refs]
    kvx = kctx_ref[...]
    keys = [_rope(t[:, :ATT_KV_DIM], c_, s_) for t, c_, s_ in zip(kv, cos, sin)]
    sink_all = sink_ref[...]
    for sub in range(ns):
        q = q_refs[sub][...]
        q = jnp.concatenate([_rope(q[:, l * LANE:(l + 1) * LANE], cos[1 + sub], sin[1 + sub])
                             for l in range(ATT_DIM // LANE)], axis=1)
        kcat = jnp.concatenate(keys[sub:sub + 3] + [kvx[:, :ATT_KV_DIM]], axis=0)
        vcat = jnp.concatenate([t[:, ATT_KV_DIM:] for t in kv[sub:sub + 3]] + [kvx[:, ATT_KV_DIM:]], axis=0)
        band = _band_valid(first=(b0 + sub == 0), last=(b0 + sub == nb - 1))
        _attend(q * scale, kcat, vcat, band, sink_all, o_ref, row0=sub * b)


def _attention(pq, pkv, rope, sink, ctx_len, with_ctx):
    t = pq.shape[0]
    n = t - ctx_len
    nb = n // ATT_BLOCK
    ns = ATT_STEP_BLOCKS
    assert nb % ns == 0 and ctx_len % ATT_BLOCK == 0
    off = ctx_len // ATT_BLOCK
    whole = lambda a: pl.BlockSpec(a.shape, lambda j: (0,) * a.ndim)
    ctx_kv = pl.BlockSpec((ctx_len, 2 * ATT_KV_DIM), lambda j: (0, 0))
    sink_spec = pl.BlockSpec((8, LANE), lambda j: (0, 0))
    yc_ctx = None
    if with_ctx:
        yc_ctx = pl.pallas_call(
            _attn_ctx_kernel,
            grid=(1,),
            in_specs=[pl.BlockSpec((ctx_len, ATT_DIM), lambda j: (0, 0)), ctx_kv, sink_spec],
            out_specs=pl.BlockSpec((ctx_len, ATT_DIM), lambda j: (0, 0)),
            out_shape=jax.ShapeDtypeStruct((ctx_len, ATT_DIM), F32),
            compiler_params=_params(("arbitrary",), VMEM_LIMIT),
            name="context_attention",
        )(pq, pkv, sink)
    lat = lambda k: jnp.clip(k, 0, nb - 1) + off
    q_specs = [pl.BlockSpec((ATT_BLOCK, ATT_DIM), lambda j, s=s: (lat(ns * j + s), 0)) for s in range(ns)]
    kv_specs = [pl.BlockSpec((ATT_BLOCK, 2 * ATT_KV_DIM), lambda j, s=s: (lat(ns * j - 1 + s), 0))
                for s in range(ns + 2)]
    yc_lat = pl.pallas_call(
        functools.partial(_attn_lat_kernel, nb=nb),
        grid=(nb // ns,),
        in_specs=q_specs + kv_specs + [ctx_kv] + [whole(a) for a in rope] + [sink_spec],
        out_specs=pl.BlockSpec((ns * ATT_BLOCK, ATT_DIM), lambda j: (j, 0)),
        out_shape=jax.ShapeDtypeStruct((n, ATT_DIM), F32),
        compiler_params=_params(("arbitrary",), VMEM_LIMIT),
        name="attention",
    )(*([pq] * ns + [pkv] * (ns + 3) + list(rope) + [sink]))
    return yc_ctx, yc_lat


def _mixfin_kernel(ctx_ref, h_ref, mod_ref, pconv_ref, prev_ref, next_ref, cw_ref, of_ref, ob_ref, z_ref, ng_ref,
                   ycc_ref, ycl_ref, wout_ref, bd_ref, *rest, nblk, blk0, with_router):
    if with_router:
        wr1_ref, wr0_ref, x_ref, hx_ref, lg_ref = rest
    else:
        wg_ref, wu_ref, wd_ref, x_ref = rest
    i = pl.program_id(0) + blk0
    row = _mod_row(mod_ref, i == 0)
    pc = pconv_ref[...]
    u = pc[:, CONV_CH:2 * CONV_CH] * pc[:, 2 * CONV_CH:]
    prow, nrow = _halo_rows(prev_ref, next_ref, i, nblk)
    prow = prow[:, CONV_CH:2 * CONV_CH] * prow[:, 2 * CONV_CH:]
    nrow = nrow[:, CONV_CH:2 * CONV_CH] * nrow[:, 2 * CONV_CH:]
    up, un = _shift_rows(u, prow, nrow)
    cw = cw_ref[...]
    ya = pc[:, :CONV_CH] * (up * cw[0:1, :] + u * cw[1:2, :] + un * cw[2:3, :])
    o = of_ref[...] + ob_ref[...]
    ms = _group_sum(o * o, bd_ref[...]) * (1.0 / DN_HEAD_DIM)
    z = z_ref[...]
    yb = o * lax.rsqrt(ms + EPS) * ng_ref[...] * (z * _sigmoid(z))
    mix = jnp.concatenate([ya, yb, jnp.where(i == 0, ycc_ref[...], ycl_ref[...])], axis=1)
    x = jnp.where(i == 0, ctx_ref[...], h_ref[...]) + row[:, 2 * D:3 * D] * _mm(mix, wout_ref[...])
    hx = _rmsnorm_rows(x) * (1.0 + row[:, 4 * D:5 * D]) + row[:, 3 * D:4 * D]
    if not with_router:
        hb = hx.astype(BF16)
        acc = jnp.zeros((hb.shape[0], D), F32)
        for f in range(0, D_FF, FF_CHUNK):
            fe = min(f + FF_CHUNK, D_FF)
            g = _mm(hb, wg_ref[:, f:fe])
            u_ = _mm(hb, wu_ref[:, f:fe])
            acc = acc + _mm(g * _sigmoid(g) * u_, wd_ref[f:fe, :])
        x_ref[...] = x + row[:, 5 * D:6 * D] * acc
    else:
        x_ref[...] = x
        hx_ref[...] = hx
        h1, h0 = _split2(hx)
        d = functools.partial(lax.dot_general, dimension_numbers=_NN, preferred_element_type=F32)
        lg = d(h1, wr1_ref[...]) + (d(h0, wr1_ref[...]) + d(h1, wr0_ref[...]))
        lane = lax.broadcasted_iota(jnp.int32, lg.shape, 1)
        lanef = lane.astype(F32)
        lg = jnp.where(lane < N_EXPERTS, lg, -jnp.inf)
        m1 = jnp.max(lg, axis=-1, keepdims=True)
        i1 = jnp.min(jnp.where(lg == m1, lanef, float(LANE)), axis=-1, keepdims=True)
        rest = jnp.where(lanef == i1, -jnp.inf, lg)
        m2 = jnp.max(rest, axis=-1, keepdims=True)
        i2 = jnp.min(jnp.where(rest == m2, lanef, float(LANE)), axis=-1, keepdims=True)
        e2 = jnp.exp(m2 - m1)
        g1 = 1.0 / (1.0 + e2)
        lg_ref[...] = jnp.where(lane == 0, g1, jnp.where(lane == 1, e2 * g1, jnp.where(lane == 2, i1, i2)))


def _mixer_finish(ctx_src, lat_src, lat_blk0, mod, pconv, conv_w, o_f, o_b, pz, norm_g, yc, w_out, bd, ctx_len,
                  with_ctx, router=None, ffn=None):
    assert (router is None) != (ffn is None)
    t = pconv.shape[0]
    nblk = t // TM
    blk0 = 0 if with_ctx else ctx_len // TM
    rows = t - blk0 * TM
    r8 = TM // 8
    w = pconv.shape[1]
    cur = lambda i: (i + blk0, 0)
    out_cur = lambda i: (i, 0)
    const = lambda i: (0, 0)
    ng = jnp.tile(norm_g.reshape(1, DN_HEAD_DIM), (1, DN_HEADS))
    in_specs = [
        pl.BlockSpec((TM, D), const),
        pl.BlockSpec((TM, D), lambda i: (jnp.maximum(i + blk0 - 1, 0) + lat_blk0, 0)),
        pl.BlockSpec((8, N_MOD * D), const),
        pl.BlockSpec((TM, w), cur),
        pl.BlockSpec((8, w), lambda i: (jnp.maximum((i + blk0) * r8 - 1, 0), 0)),
        pl.BlockSpec((8, w), lambda i: (jnp.minimum((i + blk0 + 1) * r8, t // 8 - 1), 0)),
        pl.BlockSpec((3, CONV_CH), const),
        pl.BlockSpec((TM, DN_DIM), cur),
        pl.BlockSpec((TM, DN_DIM), cur),
        pl.BlockSpec((TM, DN_DIM), cur),
        pl.BlockSpec((1, DN_DIM), const),
        pl.BlockSpec((TM, ATT_DIM), const),
        pl.BlockSpec((TM, ATT_DIM), lambda i: (jnp.maximum(i + blk0 - 1, 0), 0)),
        pl.BlockSpec((MIX_DIM, D), const),
        pl.BlockSpec(bd.shape, const),
    ]
    yc_ctx, yc_lat = yc
    if yc_ctx is None:
        assert not with_ctx
        yc_ctx = yc_lat
    args = [ctx_src, lat_src, mod, pconv, pconv, pconv, conv_w, o_f, o_b, pz, ng, yc_ctx, yc_lat, w_out, bd]
    out_specs = [pl.BlockSpec((TM, D), out_cur)]
    out_shape = [jax.ShapeDtypeStruct((rows, D), F32)]
    if router is not None:
        in_specs += [pl.BlockSpec((D, LANE), const)] * 2
        args += list(router)
        out_specs += [pl.BlockSpec((TM, D), out_cur), pl.BlockSpec((TM, LANE), out_cur)]
        out_shape += [jax.ShapeDtypeStruct((rows, D), F32), jax.ShapeDtypeStruct((rows, LANE), F32)]
    else:
        once = dict(pipeline_mode=pl.Buffered(1))
        in_specs += [pl.BlockSpec(w_.shape, const, **once) for w_ in ffn]
        args += list(ffn)
    return pl.pallas_call(
        functools.partial(_mixfin_kernel, nblk=nblk, blk0=blk0, with_router=router is not None),
        grid=(rows // TM,),
        in_specs=in_specs,
        out_specs=out_specs,
        out_shape=out_shape,
        compiler_params=_params(("arbitrary",), VMEM_LIMIT),
        name="mixer_finish",
    )(*args)


def _moe_kernel(be_ref, nu_ref, xs_ref, wg_ref, wu_ref, wd_ref, y_ref, acc_ref):
    b = pl.program_id(0)
    f = pl.program_id(1)

    @pl.when(b < nu_ref[0])
    def _():
        xs = xs_ref[...].astype(BF16)
        part = jnp.zeros((MOE_TM, D), F32)
        for c in range(0, MOE_TF, MOE_SUB):
            g = _mm(xs, wg_ref[0, :, c:c + MOE_SUB])
            u = _mm(xs, wu_ref[0, :, c:c + MOE_SUB])
            part = part + _mm(g * _sigmoid(g) * u, wd_ref[0, c:c + MOE_SUB, :])

        @pl.when(f == 0)
        def _():
            acc_ref[...] = part

        @pl.when(f != 0)
        def _():
            acc_ref[...] += part

        @pl.when(f == pl.num_programs(1) - 1)
        def _():
            y_ref[...] = acc_ref[...]

    @pl.when(b >= nu_ref[0])
    def _():
        y_ref[...] = jnp.zeros(y_ref.shape, F32)


def _moe_experts(xs, blk_e, n_used, wg, wu, wd):
    cap = xs.shape[0]
    nblk = cap // MOE_TM
    nf = D_FF_EXPERT // MOE_TF

    def fidx(b, f, nu):
        return jnp.where(b < nu[0], f, nf - 1)

    grid_spec = pltpu.PrefetchScalarGridSpec(
        num_scalar_prefetch=2,
        grid=(nblk, nf),
        in_specs=[
            pl.BlockSpec((MOE_TM, D), lambda b, f, be, nu: (b, 0)),
            pl.BlockSpec((1, D, MOE_TF), lambda b, f, be, nu: (be[b], 0, fidx(b, f, nu))),
            pl.BlockSpec((1, D, MOE_TF), lambda b, f, be, nu: (be[b], 0, fidx(b, f, nu))),
            pl.BlockSpec((1, MOE_TF, D), lambda b, f, be, nu: (be[b], fidx(b, f, nu), 0)),
        ],
        out_specs=pl.BlockSpec((MOE_TM, D), lambda b, f, be, nu: (b, 0)),
        scratch_shapes=[pltpu.VMEM((MOE_TM, D), F32)],
    )
    return pl.pallas_call(
        _moe_kernel,
        grid_spec=grid_spec,
        out_shape=jax.ShapeDtypeStruct((cap, D), F32),
        compiler_params=_params(("arbitrary", "arbitrary"), VMEM_LIMIT),
        name="moe_experts",
    )(blk_e, n_used, xs, wg, wu, wd)


def _moe_route(choices):
    n = choices[0].shape[0]
    a = n * TOP_K
    flat_e = jnp.concatenate(choices)
    onehot = (flat_e[:, None] == jnp.arange(N_EXPERTS, dtype=flat_e.dtype)[None, :]).astype(jnp.int32)
    counts = jnp.sum(onehot, axis=0)
    padded = (counts + MOE_TM - 1) // MOE_TM * MOE_TM
    pad_ends = jnp.cumsum(padded)
    pad_starts = pad_ends - padded
    dest = jnp.sum(onehot * (jnp.cumsum(onehot, axis=0) - onehot + pad_starts[None, :]), axis=1)
    cap = a + N_EXPERTS * MOE_TM
    nblk = cap // MOE_TM
    row_tok = (jnp.arange(cap, dtype=jnp.int32) % n).at[dest].set(
        jnp.arange(a, dtype=jnp.int32) % n, unique_indices=True, mode="promise_in_bounds")
    blk_start = jnp.arange(nblk, dtype=jnp.int32) * MOE_TM
    blk_e = jnp.minimum(jnp.sum((pad_ends[None, :] <= blk_start[:, None]).astype(jnp.int32), axis=1),
                        N_EXPERTS - 1)
    n_used = (pad_ends[-1] // MOE_TM).astype(jnp.int32).reshape(1)
    last_e = blk_e[jnp.maximum(n_used[0] - 1, 0)]
    blk_e = jnp.where(jnp.arange(nblk) < n_used[0], blk_e, last_e)
    return [dest[s * n:(s + 1) * n] for s in range(TOP_K)], row_tok, blk_e, n_used


def _final_kernel(x_ref, y0_ref, y1_ref, gt_ref, mod_ref, fg_ref, o_ref):
    mod = mod_ref[...]
    gt = gt_ref[...]
    f = gt[:, 0:1] * y0_ref[...] + gt[:, 1:2] * y1_ref[...]
    x = x_ref[...] + mod[0:1, 5 * D:6 * D] * f
    o_ref[...] = _rmsnorm_rows(x) * fg_ref[...]


def _moe_combine_final(x, y0, y1, gt, mod, final_g):
    n = x.shape[0]
    row = lambda i: (i, 0)
    const = lambda i: (0, 0)
    return pl.pallas_call(
        _final_kernel,
        grid=(n // TM,),
        in_specs=[pl.BlockSpec((TM, D), row)] * 3
        + [pl.BlockSpec((TM, LANE), row), pl.BlockSpec((8, N_MOD * D), const), pl.BlockSpec((1, D), const)],
        out_specs=pl.BlockSpec((TM, D), row),
        out_shape=jax.ShapeDtypeStruct((n, D), F32),
        compiler_params=_params(("arbitrary",), VMEM_LIMIT),
        name="moe_combine_final",
    )(x, y0, y1, gt, mod, final_g.reshape(1, D))


def _rope_tables(n):
    lane = jnp.arange(LANE, dtype=jnp.int32) % HEAD_DIM
    inv = ROPE_BASE ** (-jnp.arange(0, AXIS_DIM, 2, dtype=F32) / AXIS_DIM)
    freq = inv[lane % (AXIS_DIM // 2)]
    row_axis = (lane // AXIS_DIM) == 0
    sign = jnp.where((lane % AXIS_DIM) < AXIS_DIM // 2, -1.0, 1.0)
    reps = ATT_BLOCK // GRID_W

    def tables(count, on_axis):
        ang = jnp.arange(count, dtype=F32)[:, None] * freq[None, :]
        return jnp.where(on_axis, jnp.cos(ang), 0.0), jnp.where(on_axis, jnp.sin(ang) * sign, 0.0)

    by_block = lambda t: jnp.pad(t.reshape(-1, reps, LANE), ((0, 0), (0, 8 - reps), (0, 0)))
    cosr, sinr = tables(n // GRID_W, row_axis)
    cosc, sinc = tables(GRID_W, ~row_axis)
    return by_block(cosr), by_block(sinr), cosc, sinc


def _prep_w_in(w):
    pad = jnp.zeros((D, LANE - N_AB), w.dtype)
    return jnp.concatenate([w[:, :_C_Z], w[:, _C_A:_C_Q], pad, w[:, _C_Z:_C_A], w[:, _C_Q:_C_END]],
                           axis=1).astype(BF16)


def kernel(x, c, ctx, c_ctx, w_mod, b_mod, w_in, w_out, conv_w, dn_conv_w, dn_a_log, dn_dt_bias, dn_norm_g,
           attn_sink, ffn_w_gate, ffn_w_up, ffn_w_down, moe_router, moe_w_gate, moe_w_up, moe_w_down,
           final_norm_g):
    bsz, n, d = x.shape
    ctx_len = ctx.shape[1]
    depth = w_in.shape[0]
    assert bsz == 1 and d == D and ctx_len == TM and n % TM == 0 and n % GRID_W == 0
    rope = _rope_tables(n)
    mods = _mod_vectors(c, c_ctx, w_mod, b_mod)
    bd = _head_blockdiag(DN_DIM, DN_HEAD_DIM)
    stream = (ctx[0], x[0], 0)
    for layer in range(depth):
        last = layer == depth - 1
        mod = mods[layer]
        pconv, pz, pq, pkv, qn, kn, vv, gb = _in_proj(*stream, mod, _prep_w_in(w_in[layer]), dn_conv_w[layer],
                                                      dn_a_log[layer], dn_dt_bias[layer], bd)
        to_bf16 = []
        if layer % 2 == 0:
            to_bf16 += [w[layer // 2] for w in (ffn_w_gate, ffn_w_up, ffn_w_down)]
        if (layer + 1) % 2 == 1 and layer + 1 < depth:
            j = (layer + 1) // 2
            to_bf16 += [w[j].reshape(-1, w.shape[-1]) for w in (moe_w_gate, moe_w_up, moe_w_down)]
        o_f, o_b, *cast = _delta_net(qn, kn, vv, gb, ctx_len, tuple(to_bf16))
        if layer % 2 == 0:
            ffn_bf16, cast = cast[:3], cast[3:]
        if cast:
            moe_bf16 = [c.reshape(w.shape[1:]) for c, w in zip(cast, (moe_w_gate, moe_w_up, moe_w_down))]
        sink = jnp.zeros((8, LANE), F32).at[0, :ATT_HEADS].set(attn_sink[layer])
        yc = _attention(pq, pkv, rope, sink, ctx_len, with_ctx=not last)
        router = None
        if layer % 2 == 1:
            wr = jnp.zeros((D, LANE), F32).at[:, :N_EXPERTS].set(moe_router[layer // 2])
            wr1 = wr.astype(BF16)
            router = (wr1, (wr - wr1.astype(F32)).astype(BF16))
        outs = _mixer_finish(*stream, mod, pconv, conv_w[layer], o_f, o_b, pz, dn_norm_g[layer], yc,
                             w_out[layer].astype(BF16), bd, ctx_len, with_ctx=not last, router=router,
                             ffn=ffn_bf16 if layer % 2 == 0 else None)
        if layer % 2 == 0:
            assert not last
            h, = outs
            stream = (h, h, ctx_len // TM)
        else:
            assert last
            x1, hx, route = outs
            dest, row_tok, blk_e, n_used = _moe_route([route[:, 2 + s].astype(jnp.int32) for s in range(TOP_K)])
            take = lambda rows_, idx: rows_.at[idx].get(mode="promise_in_bounds")
            y = _moe_experts(take(hx, row_tok), blk_e, n_used, *moe_bf16)
            h = _moe_combine_final(x1, take(y, dest[0]), take(y, dest[1]), route, mod, final_norm_g)
    return h.reshape(bsz, n, d)
```

```python
import functools

import jax
import jax.numpy as jnp
from jax import lax
from jax.experimental import pallas as pl
from jax.experimental.pallas import tpu as pltpu

F32 = jnp.float32
BF16 = jnp.bfloat16

D = 1024
N_MOD = 6
EPS = 1e-6
NEG = -1e30
GRID_W = 64

CONV_CH = 256
DN_HEADS = 6
DN_HEAD_DIM = 64
DN_DIM = DN_HEADS * DN_HEAD_DIM
DN_CHUNK = 64
DN_SUB = 16
DN_STEP_CHUNKS = 4
ATT_HEADS = 6
ATT_KV_HEADS = 2
ATT_GROUP = ATT_HEADS // ATT_KV_HEADS
HEAD_DIM = 64
ATT_DIM = ATT_HEADS * HEAD_DIM
ATT_KV_DIM = ATT_KV_HEADS * HEAD_DIM
ATT_BLOCK = 128
ROPE_BASE = 10000.0
AXIS_DIM = HEAD_DIM // 2
MIX_DIM = CONV_CH + DN_DIM + ATT_DIM

D_FF = 2816
N_EXPERTS = 8
TOP_K = 2
D_FF_EXPERT = 3584

TM = 256
FF_CHUNK = 1024
MOE_TM = 512
MOE_TF = 1792
MOE_SUB = 1792
LANE = 128
VMEM_LIMIT = 56 * 1024 * 1024

_C_QKV = 3 * CONV_CH
_C_Z = _C_QKV + 3 * DN_DIM
_C_A = _C_Z + DN_DIM
_C_Q = _C_A + 4 * DN_HEADS
_C_K = _C_Q + ATT_DIM
_C_V = _C_K + ATT_KV_DIM
_C_END = _C_V + ATT_KV_DIM
N_AB = 4 * DN_HEADS


def _params(sem=None, vmem=None):
    kw = {}
    if sem is not None:
        kw["dimension_semantics"] = sem
    if vmem is not None:
        kw["vmem_limit_bytes"] = vmem
    return pltpu.CompilerParams(**kw)


def _split2(a):
    hi = a.astype(BF16)
    lo = (a - hi.astype(F32)).astype(BF16)
    return hi, lo


def _split3(a):
    hi = a.astype(BF16)
    r = a - hi.astype(F32)
    mid = r.astype(BF16)
    lo = (r - mid.astype(F32)).astype(BF16)
    return hi, mid, lo


_NN = (((1,), (0,)), ((), ()))
_NT = (((1,), (1,)), ((), ()))


def _mm(a, b, dims=_NN):
    return lax.dot_general(a.astype(BF16), b.astype(BF16), dims, preferred_element_type=F32)


def _mm3(a, b, dims=_NN):
    a1, a0 = _split2(a)
    b1, b0 = _split2(b)
    d = functools.partial(lax.dot_general, dimension_numbers=dims, preferred_element_type=F32)
    return d(a1, b1) + (d(a1, b0) + d(a0, b1))


_BNN = (((2,), (1,)), ((0,), (0,)))
_BNT = (((2,), (2,)), ((0,), (0,)))
_BTN = (((1,), (1,)), ((0,), (0,)))


def _bmm(a, b, dims=_BNN):
    return lax.dot_general(a.astype(BF16), b.astype(BF16), dims, preferred_element_type=F32)


def _sigmoid(x):
    return 1.0 / (1.0 + jnp.exp(-x))


def _softplus(x):
    return jnp.maximum(x, 0.0) + jnp.log1p(jnp.exp(-jnp.abs(x)))


def _mod_row(mod_ref, is_ctx):
    mod = mod_ref[...]
    return jnp.where(is_ctx, mod[1:2, :], mod[0:1, :])


def _rmsnorm_rows(x):
    return x * lax.rsqrt(jnp.mean(x * x, axis=-1, keepdims=True) + EPS)


def _shift_rows(u, prow, nrow):
    n = u.shape[0]
    rid = lax.broadcasted_iota(jnp.int32, u.shape, 0)
    up = jnp.where(rid == 0, prow, pltpu.roll(u, 1, 0))
    un = jnp.where(rid == n - 1, nrow, pltpu.roll(u, n - 1, 0))
    return up, un


def _same_group(shape, group):
    sh = group.bit_length() - 1
    assert 1 << sh == group
    return (lax.broadcasted_iota(jnp.int32, shape, 0) >> sh) == (lax.broadcasted_iota(jnp.int32, shape, 1) >> sh)


def _head_blockdiag(n, group):
    g = jnp.arange(n, dtype=jnp.int32) // group
    return (g[:, None] == g[None, :]).astype(BF16)


def _group_sum(t, bd):
    return _mm(t, bd)


MOD_TN = 1536


def _mod_kernel(s_ref, w_ref, b_ref, o_ref):
    s = s_ref[...]
    s = s * _sigmoid(s)
    o_ref[0] = _mm3(s, w_ref[0]) + b_ref[0]


def _mod_vectors(c, c_ctx, w_mod, b_mod):
    depth = w_mod.shape[0]
    s = jnp.zeros((8, D), F32).at[0].set(c[0]).at[1].set(c_ctx)
    return pl.pallas_call(
        _mod_kernel,
        grid=(depth, N_MOD * D // MOD_TN),
        in_specs=[
            pl.BlockSpec((8, D), lambda l, j: (0, 0)),
            pl.BlockSpec((1, D, MOD_TN), lambda l, j: (l, 0, j)),
            pl.BlockSpec((1, 1, MOD_TN), lambda l, j: (l, 0, j)),
        ],
        out_specs=pl.BlockSpec((1, 8, MOD_TN), lambda l, j: (l, 0, j)),
        out_shape=jax.ShapeDtypeStruct((depth, 8, N_MOD * D), F32),
        compiler_params=_params(("arbitrary", "arbitrary"), VMEM_LIMIT),
        name="mod_vectors",
    )(s, w_mod, b_mod.reshape(depth, 1, N_MOD * D))


def _halo_valid(i, nblk):
    return jnp.logical_and(i != 0, i != 1), jnp.logical_and(i != 0, i != nblk - 1)


def _halo_rows(prev_ref, next_ref, i, nblk):
    pvalid, nvalid = _halo_valid(i, nblk)
    prow = jnp.where(pvalid, prev_ref[7:8, :], 0.0)
    nrow = jnp.where(nvalid, next_ref[0:1, :], 0.0)
    return prow, nrow


def _in_kernel(ctx_ref, h_ref, hprev_ref, hnext_ref, mod_ref, w_ref, cw_ref, alog_ref, dtb_ref, bd_ref,
               pconv_ref, pz_ref, pq_ref, pkv_ref, q_ref, k_ref, v_ref, gb_ref, *, nblk):
    i = pl.program_id(0)
    row = _mod_row(mod_ref, i == 0)
    norm_mod = lambda x: _rmsnorm_rows(x) * (1.0 + row[:, D:2 * D]) + row[:, 0:D]
    h1 = norm_mod(jnp.where(i == 0, ctx_ref[...], h_ref[...])).astype(BF16)
    halo = norm_mod(jnp.concatenate([hprev_ref[...], hnext_ref[...]], axis=0)).astype(BF16)
    d = functools.partial(lax.dot_general, dimension_numbers=_NN, preferred_element_type=F32)
    tm = h1.shape[0]
    c0 = 3 * CONV_CH
    c1 = c0 + 3 * DN_DIM
    c2 = c1 + LANE
    c3 = c2 + DN_DIM + ATT_DIM
    pconv_ref[...] = d(h1, w_ref[:, 0:c0])
    zq = d(h1, w_ref[:, c2:c3])
    pz_ref[...] = zq[:, 0:DN_DIM]
    pq_ref[...] = zq[:, DN_DIM:]
    pkv_ref[...] = d(h1, w_ref[:, c3:c3 + 2 * ATT_KV_DIM])
    qkv_ab = d(jnp.concatenate([h1, halo], axis=0), w_ref[:, c0:c2])
    qkv = qkv_ab[:, 0:3 * DN_DIM]
    ab = qkv_ab[0:tm, 3 * DN_DIM:]

    u = qkv[0:tm, :]
    pvalid, nvalid = _halo_valid(i, nblk)
    prow = jnp.where(pvalid, qkv[tm + 7:tm + 8, :], 0.0)
    nrow = jnp.where(nvalid, qkv[tm + 8:tm + 9, :], 0.0)
    up, un = _shift_rows(u, prow, nrow)
    cw = cw_ref[...]
    y = up * cw[0:1, :] + u * cw[1:2, :] + un * cw[2:3, :]
    y = y * _sigmoid(y)
    q = y[:, 0:DN_DIM]
    k = y[:, DN_DIM:2 * DN_DIM]
    bd = bd_ref[...]
    q_ref[...] = q * lax.rsqrt(_group_sum(q * q, bd) + 1e-6) * (DN_HEAD_DIM ** -0.5)
    k_ref[...] = k * lax.rsqrt(_group_sum(k * k, bd) + 1e-6)
    v_ref[...] = y[:, 2 * DN_DIM:3 * DN_DIM]
    g = -jnp.exp(alog_ref[...]) * _softplus(ab + dtb_ref[...])
    lane = lax.broadcasted_iota(jnp.int32, ab.shape, 1)
    gb_ref[...] = jnp.where(lane < 2 * DN_HEADS, g, _sigmoid(ab))


def _in_proj(ctx_src, lat_src, lat_blk0, mod, w_main, dn_conv_w, a_log, dt_bias, bd):
    r8 = TM // 8
    last8 = lat_src.shape[0] // 8 - 1
    nblk = 1 + lat_src.shape[0] // TM - lat_blk0
    t = nblk * TM
    alog = jnp.zeros((1, LANE), F32).at[0, :2 * DN_HEADS].set(a_log.reshape(-1))
    dtb = jnp.zeros((1, LANE), F32).at[0, :2 * DN_HEADS].set(dt_bias.reshape(-1))
    widths = (3 * CONV_CH, DN_DIM, ATT_DIM, 2 * ATT_KV_DIM, DN_DIM, DN_DIM, DN_DIM, LANE)
    const = lambda i: (0, 0)
    lat = lambda i: jnp.maximum(i - 1, 0) + lat_blk0
    return pl.pallas_call(
        functools.partial(_in_kernel, nblk=nblk),
        grid=(nblk,),
        in_specs=[
            pl.BlockSpec((TM, D), const),
            pl.BlockSpec((TM, D), lambda i: (lat(i), 0)),
            pl.BlockSpec((8, D), lambda i: (jnp.maximum(lat(i) * r8 - 1, 0), 0)),
            pl.BlockSpec((8, D), lambda i: (jnp.minimum((lat(i) + 1) * r8, last8), 0)),
            pl.BlockSpec((8, N_MOD * D), const),
            pl.BlockSpec(w_main.shape, const),
            pl.BlockSpec((3, 3 * DN_DIM), const),
            pl.BlockSpec((1, LANE), const),
            pl.BlockSpec((1, LANE), const),
            pl.BlockSpec(bd.shape, const),
        ],
        out_specs=[pl.BlockSpec((TM, w), lambda i: (i, 0)) for w in widths],
        out_shape=[jax.ShapeDtypeStruct((t, w), F32) for w in widths],
        compiler_params=_params(("arbitrary",), VMEM_LIMIT),
        name="in_proj",
    )(ctx_src, lat_src, lat_src, lat_src, mod, w_main, dn_conv_w, alog, dtb, bd)


def _dn_block(fwd_refs, bwd_refs, of_ref, ob_ref, s_ref):
    c_ = DN_CHUNK
    nh = DN_HEADS
    nchunks = fwd_refs[0].shape[0] // c_
    nb = nchunks * nh
    rows = lambda g: slice(g * c_, (g + 1) * c_)
    lanes = lambda h: slice(h * DN_HEAD_DIM, (h + 1) * DN_HEAD_DIM)
    ri = lax.broadcasted_iota(jnp.int32, (c_, c_), 0)
    ci = lax.broadcasted_iota(jnp.int32, (c_, c_), 1)
    same_sub = _same_group((c_, c_), DN_SUB)
    eye = jnp.where(ri == ci, 1.0, 0.0)
    shape3 = (2 * nb, c_, c_)
    delta = lax.broadcasted_iota(jnp.int32, shape3, 1) - lax.broadcasted_iota(jnp.int32, shape3, 2)
    delta = jnp.where(lax.broadcasted_iota(jnp.int32, shape3, 0) >= nb, -delta, delta)
    incl = delta >= 0
    strict = delta > 0

    def stack(fn):
        return jnp.stack([fn(d, refs, g, h) for d, refs in enumerate((fwd_refs, bwd_refs))
                          for g in range(nchunks) for h in range(nh)])

    gbs = [refs[3][...] for refs in (fwd_refs, bwd_refs)]
    tris = [jnp.where(ri >= ci, 1.0, 0.0).astype(BF16), jnp.where(ri <= ci, 1.0, 0.0).astype(BF16)]
    gcs = [[_cumsum_rows(tris[d], gbs[d][rows(g), :]) for g in range(nchunks)] for d in range(2)]
    gcts = [[gc.T for gc in gcs[d]] for d in range(2)]
    q = stack(lambda d, refs, g, h: refs[0][rows(g), lanes(h)])
    k = stack(lambda d, refs, g, h: refs[1][rows(g), lanes(h)])
    v = stack(lambda d, refs, g, h: refs[2][rows(g), lanes(h)])
    col = lambda d, h: d * nh + h
    gcol = stack(lambda d, refs, g, h: gcs[d][g][:, col(d, h):col(d, h) + 1])
    grow = stack(lambda d, refs, g, h: gcts[d][g][col(d, h):col(d, h) + 1, :])
    beta = stack(lambda d, refs, g, h: gbs[d][rows(g), 2 * nh + col(d, h):2 * nh + col(d, h) + 1])
    glast = jnp.concatenate([gcol[:nb, c_ - 1:c_, :], gcol[nb:, 0:1, :]], axis=0)
    decay = jnp.where(incl, jnp.exp(jnp.where(incl, gcol - grow, 0.0)), 0.0)
    eg = jnp.exp(gcol)
    kb = k * beta
    a = jnp.where(strict, _bmm(kb, k, _BNT) * decay, 0.0)
    qk = jnp.where(incl, _bmm(q, k, _BNT) * decay, 0.0)
    ad = jnp.where(same_sub, a, 0.0)
    ao = a - ad
    p = eye - ad
    n2 = _bmm(ad, ad)
    p = p + _bmm(p, n2)
    n4 = _bmm(n2, n2)
    p = p + _bmm(p, n4)
    n8 = _bmm(n4, n4)
    dinv = p + _bmm(p, n8)
    m = _bmm(dinv, ao)
    m2 = _bmm(m, m)
    y = _bmm(dinv, jnp.concatenate([v * beta, kb * eg], axis=-1))
    z = y + _bmm(m2, y)
    x = z - _bmm(m, z)
    u = x[:, :, :DN_HEAD_DIM]
    w = x[:, :, DN_HEAD_DIM:]
    qg = q * eg
    kd = k * jnp.exp(glast - gcol)
    gl = jnp.exp(glast)
    s = s_ref[...]
    for t in range(nchunks):
        gf, gr = t, nchunks - 1 - t
        step = lambda a: jnp.concatenate([a[gf * nh:(gf + 1) * nh], a[nb + gr * nh:nb + (gr + 1) * nh]], axis=0)
        v_new = step(u) - _bmm(step(w), s)
        o = _bmm(step(qg), s) + _bmm(step(qk), v_new)
        s = s * step(gl) + _bmm(step(kd), v_new, _BTN)
        for h in range(nh):
            of_ref[rows(gf), lanes(h)] = o[h]
            ob_ref[rows(gr), lanes(h)] = o[nh + h]
    s_ref[...] = s


def _cumsum_rows(tri_bf16, g):
    g2, g1, g0 = _split3(g)
    d = functools.partial(lax.dot_general, dimension_numbers=_NN, preferred_element_type=F32)
    return d(tri_bf16, g2) + (d(tri_bf16, g1) + d(tri_bf16, g0))


def _dn_kernel(qf, kf, vf, gf, qb, kb, vb, gbb, *rest, n_cast):
    cast_in = rest[:n_cast]
    of_ref, ob_ref = rest[n_cast:n_cast + 2]
    cast_out = rest[n_cast + 2:2 * n_cast + 2]
    s_ref = rest[-1]

    @pl.when(pl.program_id(0) == 0)
    def _():
        s_ref[...] = jnp.zeros(s_ref.shape, F32)

    _dn_block((qf, kf, vf, gf), (qb, kb, vb, gbb), of_ref, ob_ref, s_ref)
    for src, dst in zip(cast_in, cast_out):
        dst[...] = src[...].astype(BF16)


def _delta_net(q, k, v, gb, ctx_len, to_bf16=()):
    t = q.shape[0]
    rows = DN_STEP_CHUNKS * DN_CHUNK
    assert ctx_len == rows and t % rows == 0
    nstep = t // rows
    cast_specs = []
    for m in to_bf16:
        rb = -(-m.shape[0] // nstep)
        rb = -(-rb // 16) * 16
        nb = -(-m.shape[0] // rb)
        cast_specs.append(pl.BlockSpec((rb, m.shape[1]), lambda s, nb=nb: (jnp.minimum(s, nb - 1), 0)))

    def fwd(s):
        return (s, 0)

    def bwd(s):
        return (jnp.where(s == 0, 0, nstep - s), 0)

    wide = lambda im: pl.BlockSpec((rows, DN_DIM), im)
    narrow = lambda im: pl.BlockSpec((rows, LANE), im)
    return pl.pallas_call(
        functools.partial(_dn_kernel, n_cast=len(to_bf16)),
        grid=(nstep,),
        in_specs=[wide(fwd), wide(fwd), wide(fwd), narrow(fwd), wide(bwd), wide(bwd), wide(bwd), narrow(bwd)]
        + cast_specs,
        out_specs=[wide(fwd), wide(bwd)] + cast_specs,
        out_shape=[jax.ShapeDtypeStruct((t, DN_DIM), F32)] * 2
        + [jax.ShapeDtypeStruct(m.shape, BF16) for m in to_bf16],
        scratch_shapes=[pltpu.VMEM((2 * DN_HEADS, DN_HEAD_DIM, DN_HEAD_DIM), F32)],
        compiler_params=_params(("arbitrary",), VMEM_LIMIT),
        name="delta_net",
    )(q, k, v, gb, q, k, v, gb, *to_bf16)


def _rope(x, cos, sin):
    w = x.shape[1]
    lane = lax.broadcasted_iota(jnp.int32, x.shape, 1)
    first_half = (lane & (AXIS_DIM - 1)) < (AXIS_DIM // 2)
    swapped = jnp.where(first_half, pltpu.roll(x, w - AXIS_DIM // 2, 1), pltpu.roll(x, AXIS_DIM // 2, 1))
    return x * cos + swapped * sin


LOG2E = 1.4426950408889634


def _softmax_av(s, sink, vals):
    m = jnp.maximum(jnp.max(s, axis=-1, keepdims=True), sink)
    p = jnp.exp2(s - m)
    denom = jnp.sum(p, axis=-1, keepdims=True) + jnp.exp2(sink - m)
    return _mm(p, vals) / denom


def _attend(q, keys, vals, band, sink_all, o_ref, row0=0):
    b = q.shape[0]
    for kvh in range(ATT_KV_HEADS):
        kl = slice(kvh * HEAD_DIM, (kvh + 1) * HEAD_DIM)
        heads = range(kvh * ATT_GROUP, (kvh + 1) * ATT_GROUP)
        qs = jnp.concatenate([q[:, h * HEAD_DIM:(h + 1) * HEAD_DIM] for h in heads], axis=0)
        sink = jnp.concatenate([jnp.broadcast_to(sink_all[0:1, h:h + 1], (b, 1)) for h in heads], axis=0)
        s = _mm(qs, keys[:, kl], _NT)
        if band is not None:
            kb = ATT_BLOCK
            s = jnp.concatenate([jnp.where(band[0], s[:, 0:kb], NEG), s[:, kb:2 * kb],
                                 jnp.where(band[1], s[:, 2 * kb:3 * kb], NEG), s[:, 3 * kb:]], axis=1)
        o = _softmax_av(s, sink * LOG2E, vals[:, kl])
        for g, h in enumerate(heads):
            o_ref[row0:row0 + b, h * HEAD_DIM:(h + 1) * HEAD_DIM] = o[g * b:(g + 1) * b, :]


def _band_valid(first, last):
    b = ATT_BLOCK
    c = lax.broadcasted_iota(jnp.int32, (1, b), 1)
    r = lax.broadcasted_iota(jnp.int32, (ATT_GROUP * b, 1), 0) & (b - 1)
    prev_ok = jnp.where(first, -1, c) >= r
    next_ok = jnp.where(last, b, c) <= r
    return prev_ok, next_ok


def _rope_block(rowtab_ref, coltab_ref, blk):
    rt = rowtab_ref[blk]
    ct = coltab_ref[...]
    reps = ATT_BLOCK // GRID_W
    rows = jnp.concatenate([jnp.broadcast_to(rt[g:g + 1, :], (GRID_W, LANE)) for g in range(reps)], axis=0)
    return rows + jnp.concatenate([ct] * reps, axis=0)


ATT_STEP_BLOCKS = 4


def _attn_ctx_kernel(q_ref, kctx_ref, sink_ref, o_ref):
    kvx = kctx_ref[...]
    scale = HEAD_DIM ** -0.5 * LOG2E
    _attend(q_ref[...] * scale, kvx[:, :ATT_KV_DIM], kvx[:, ATT_KV_DIM:], None, sink_ref[...], o_ref)


def _attn_lat_kernel(*refs, nb):
    ns = ATT_STEP_BLOCKS
    q_refs, kv_refs = refs[:ns], refs[ns:2 * ns + 2]
    kctx_ref, cosr_ref, sinr_ref, cosc_ref, sinc_ref, sink_ref, o_ref = refs[2 * ns + 2:]
    b = ATT_BLOCK
    scale = HEAD_DIM ** -0.5 * LOG2E
    b0 = ns * pl.program_id(0)
    blocks = [jnp.clip(b0 - 1 + t, 0, nb - 1) for t in range(ns + 2)]
    cos = [_rope_block(cosr_ref, cosc_ref, blk) for blk in blocks]
    sin = [_rope_block(sinr_ref, sinc_ref, blk) for blk in blocks]
    kv = [r[...] for r in kv_refs]
    kvx = kctx_ref[...]
    keys = [_rope(t[:, :ATT_KV_DIM], c_, s_) for t, c_, s_ in zip(kv, cos, sin)]
    sink_all = sink_ref[...]
    for sub in range(ns):
        q = q_refs[sub][...]
        q = jnp.concatenate([_rope(q[:, l * LANE:(l + 1) * LANE], cos[1 + sub], sin[1 + sub])
                             for l in range(ATT_DIM // LANE)], axis=1)
        kcat = jnp.concatenate(keys[sub:sub + 3] + [kvx[:, :ATT_KV_DIM]], axis=0)
        vcat = jnp.concatenate([t[:, ATT_KV_DIM:] for t in kv[sub:sub + 3]] + [kvx[:, ATT_KV_DIM:]], axis=0)
        band = _band_valid(first=(b0 + sub == 0), last=(b0 + sub == nb - 1))
        _attend(q * scale, kcat, vcat, band, sink_all, o_ref, row0=sub * b)


def _attention(pq, pkv, rope, sink, ctx_len, with_ctx):
    t = pq.shape[0]
    n = t - ctx_len
    nb = n // ATT_BLOCK
    ns = ATT_STEP_BLOCKS
    assert nb % ns == 0 and ctx_len % ATT_BLOCK == 0
    off = ctx_len // ATT_BLOCK
    whole = lambda a: pl.BlockSpec(a.shape, lambda j: (0,) * a.ndim)
    ctx_kv = pl.BlockSpec((ctx_len, 2 * ATT_KV_DIM), lambda j: (0, 0))
    sink_spec = pl.BlockSpec((8, LANE), lambda j: (0, 0))
    yc_ctx = None
    if with_ctx:
        yc_ctx = pl.pallas_call(
            _attn_ctx_kernel,
            grid=(1,),
            in_specs=[pl.BlockSpec((ctx_len, ATT_DIM), lambda j: (0, 0)), ctx_kv, sink_spec],
            out_specs=pl.BlockSpec((ctx_len, ATT_DIM), lambda j: (0, 0)),
            out_shape=jax.ShapeDtypeStruct((ctx_len, ATT_DIM), F32),
            compiler_params=_params(("arbitrary",), VMEM_LIMIT),
            name="context_attention",
        )(pq, pkv, sink)
    lat = lambda k: jnp.clip(k, 0, nb - 1) + off
    q_specs = [pl.BlockSpec((ATT_BLOCK, ATT_DIM), lambda j, s=s: (lat(ns * j + s), 0)) for s in range(ns)]
    kv_specs = [pl.BlockSpec((ATT_BLOCK, 2 * ATT_KV_DIM), lambda j, s=s: (lat(ns * j - 1 + s), 0))
                for s in range(ns + 2)]
    yc_lat = pl.pallas_call(
        functools.partial(_attn_lat_kernel, nb=nb),
        grid=(nb // ns,),
        in_specs=q_specs + kv_specs + [ctx_kv] + [whole(a) for a in rope] + [sink_spec],
        out_specs=pl.BlockSpec((ns * ATT_BLOCK, ATT_DIM), lambda j: (j, 0)),
        out_shape=jax.ShapeDtypeStruct((n, ATT_DIM), F32),
        compiler_params=_params(("arbitrary",), VMEM_LIMIT),
        name="attention",
    )(*([pq] * ns + [pkv] * (ns + 3) + list(rope) + [sink]))
    return yc_ctx, yc_lat


def _mixfin_kernel(ctx_ref, h_ref, mod_ref, pconv_ref, prev_ref, next_ref, cw_ref, of_ref, ob_ref, z_ref, ng_ref,
                   ycc_ref, ycl_ref, wout_ref, bd_ref, *rest, nblk, blk0, with_router):
    if with_router:
        wr1_ref, wr0_ref, x_ref, hx_ref, lg_ref = rest
    else:
        wg_ref, wu_ref, wd_ref, x_ref = rest
    i = pl.program_id(0) + blk0
    row = _mod_row(mod_ref, i == 0)
    pc = pconv_ref[...]
    u = pc[:, CONV_CH:2 * CONV_CH] * pc[:, 2 * CONV_CH:]
    prow, nrow = _halo_rows(prev_ref, next_ref, i, nblk)
    prow = prow[:, CONV_CH:2 * CONV_CH] * prow[:, 2 * CONV_CH:]
    nrow = nrow[:, CONV_CH:2 * CONV_CH] * nrow[:, 2 * CONV_CH:]
    up, un = _shift_rows(u, prow, nrow)
    cw = cw_ref[...]
    ya = pc[:, :CONV_CH] * (up * cw[0:1, :] + u * cw[1:2, :] + un * cw[2:3, :])
    o = of_ref[...] + ob_ref[...]
    ms = _group_sum(o * o, bd_ref[...]) * (1.0 / DN_HEAD_DIM)
    z = z_ref[...]
    yb = o * lax.rsqrt(ms + EPS) * ng_ref[...] * (z * _sigmoid(z))
    mix = jnp.concatenate([ya, yb, jnp.where(i == 0, ycc_ref[...], ycl_ref[...])], axis=1)
    x = jnp.where(i == 0, ctx_ref[...], h_ref[...]) + row[:, 2 * D:3 * D] * _mm(mix, wout_ref[...])
    hx = _rmsnorm_rows(x) * (1.0 + row[:, 4 * D:5 * D]) + row[:, 3 * D:4 * D]
    if not with_router:
        hb = hx.astype(BF16)
        acc = jnp.zeros((hb.shape[0], D), F32)
        for f in range(0, D_FF, FF_CHUNK):
            fe = min(f + FF_CHUNK, D_FF)
            g = _mm(hb, wg_ref[:, f:fe])
            u_ = _mm(hb, wu_ref[:, f:fe])
            acc = acc + _mm(g * _sigmoid(g) * u_, wd_ref[f:fe, :])
        x_ref[...] = x + row[:, 5 * D:6 * D] * acc
    else:
        x_ref[...] = x
        hx_ref[...] = hx
        h1, h0 = _split2(hx)
        d = functools.partial(lax.dot_general, dimension_numbers=_NN, preferred_element_type=F32)
        lg = d(h1, wr1_ref[...]) + (d(h0, wr1_ref[...]) + d(h1, wr0_ref[...]))
        lane = lax.broadcasted_iota(jnp.int32, lg.shape, 1)
        lanef = lane.astype(F32)
        lg = jnp.where(lane < N_EXPERTS, lg, -jnp.inf)
        m1 = jnp.max(lg, axis=-1, keepdims=True)
        i1 = jnp.min(jnp.where(lg == m1, lanef, float(LANE)), axis=-1, keepdims=True)
        rest = jnp.where(lanef == i1, -jnp.inf, lg)
        m2 = jnp.max(rest, axis=-1, keepdims=True)
        i2 = jnp.min(jnp.where(rest == m2, lanef, float(LANE)), axis=-1, keepdims=True)
        e2 = jnp.exp(m2 - m1)
        g1 = 1.0 / (1.0 + e2)
        lg_ref[...] = jnp.where(lane == 0, g1, jnp.where(lane == 1, e2 * g1, jnp.where(lane == 2, i1, i2)))


def _mixer_finish(ctx_src, lat_src, lat_blk0, mod, pconv, conv_w, o_f, o_b, pz, norm_g, yc, w_out, bd, ctx_len,
                  with_ctx, router=None, ffn=None):
    assert (router is None) != (ffn is None)
    t = pconv.shape[0]
    nblk = t // TM
    blk0 = 0 if with_ctx else ctx_len // TM
    rows = t - blk0 * TM
    r8 = TM // 8
    w = pconv.shape[1]
    cur = lambda i: (i + blk0, 0)
    out_cur = lambda i: (i, 0)
    const = lambda i: (0, 0)
    ng = jnp.tile(norm_g.reshape(1, DN_HEAD_DIM), (1, DN_HEADS))
    in_specs = [
        pl.BlockSpec((TM, D), const),
        pl.BlockSpec((TM, D), lambda i: (jnp.maximum(i + blk0 - 1, 0) + lat_blk0, 0)),
        pl.BlockSpec((8, N_MOD * D), const),
        pl.BlockSpec((TM, w), cur),
        pl.BlockSpec((8, w), lambda i: (jnp.maximum((i + blk0) * r8 - 1, 0), 0)),
        pl.BlockSpec((8, w), lambda i: (jnp.minimum((i + blk0 + 1) * r8, t // 8 - 1), 0)),
        pl.BlockSpec((3, CONV_CH), const),
        pl.BlockSpec((TM, DN_DIM), cur),
        pl.BlockSpec((TM, DN_DIM), cur),
        pl.BlockSpec((TM, DN_DIM), cur),
        pl.BlockSpec((1, DN_DIM), const),
        pl.BlockSpec((TM, ATT_DIM), const),
        pl.BlockSpec((TM, ATT_DIM), lambda i: (jnp.maximum(i + blk0 - 1, 0), 0)),
        pl.BlockSpec((MIX_DIM, D), const),
        pl.BlockSpec(bd.shape, const),
    ]
    yc_ctx, yc_lat = yc
    if yc_ctx is None:
        assert not with_ctx
        yc_ctx = yc_lat
    args = [ctx_src, lat_src, mod, pconv, pconv, pconv, conv_w, o_f, o_b, pz, ng, yc_ctx, yc_lat, w_out, bd]
    out_specs = [pl.BlockSpec((TM, D), out_cur)]
    out_shape = [jax.ShapeDtypeStruct((rows, D), F32)]
    if router is not None:
        in_specs += [pl.BlockSpec((D, LANE), const)] * 2
        args += list(router)
        out_specs += [pl.BlockSpec((TM, D), out_cur), pl.BlockSpec((TM, LANE), out_cur)]
        out_shape += [jax.ShapeDtypeStruct((rows, D), F32), jax.ShapeDtypeStruct((rows, LANE), F32)]
    else:
        once = dict(pipeline_mode=pl.Buffered(1))
        in_specs += [pl.BlockSpec(w_.shape, const, **once) for w_ in ffn]
        args += list(ffn)
    return pl.pallas_call(
        functools.partial(_mixfin_kernel, nblk=nblk, blk0=blk0, with_router=router is not None),
        grid=(rows // TM,),
        in_specs=in_specs,
        out_specs=out_specs,
        out_shape=out_shape,
        compiler_params=_params(("arbitrary",), VMEM_LIMIT),
        name="mixer_finish",
    )(*args)


def _moe_kernel(be_ref, nu_ref, xs_ref, wg_ref, wu_ref, wd_ref, y_ref, acc_ref):
    b = pl.program_id(0)
    f = pl.program_id(1)

    @pl.when(b < nu_ref[0])
    def _():
        xs = xs_ref[...].astype(BF16)
        part = jnp.zeros((MOE_TM, D), F32)
        for c in range(0, MOE_TF, MOE_SUB):
            g = _mm(xs, wg_ref[0, :, c:c + MOE_SUB])
            u = _mm(xs, wu_ref[0, :, c:c + MOE_SUB])
            part = part + _mm(g * _sigmoid(g) * u, wd_ref[0, c:c + MOE_SUB, :])

        @pl.when(f == 0)
        def _():
            acc_ref[...] = part

        @pl.when(f != 0)
        def _():
            acc_ref[...] += part

        @pl.when(f == pl.num_programs(1) - 1)
        def _():
            y_ref[...] = acc_ref[...]

    @pl.when(b >= nu_ref[0])
    def _():
        y_ref[...] = jnp.zeros(y_ref.shape, F32)


def _moe_experts(xs, blk_e, n_used, wg, wu, wd):
    cap = xs.shape[0]
    nblk = cap // MOE_TM
    nf = D_FF_EXPERT // MOE_TF

    def fidx(b, f, nu):
        return jnp.where(b < nu[0], f, nf - 1)

    grid_spec = pltpu.PrefetchScalarGridSpec(
        num_scalar_prefetch=2,
        grid=(nblk, nf),
        in_specs=[
            pl.BlockSpec((MOE_TM, D), lambda b, f, be, nu: (b, 0)),
            pl.BlockSpec((1, D, MOE_TF), lambda b, f, be, nu: (be[b], 0, fidx(b, f, nu))),
            pl.BlockSpec((1, D, MOE_TF), lambda b, f, be, nu: (be[b], 0, fidx(b, f, nu))),
            pl.BlockSpec((1, MOE_TF, D), lambda b, f, be, nu: (be[b], fidx(b, f, nu), 0)),
        ],
        out_specs=pl.BlockSpec((MOE_TM, D), lambda b, f, be, nu: (b, 0)),
        scratch_shapes=[pltpu.VMEM((MOE_TM, D), F32)],
    )
    return pl.pallas_call(
        _moe_kernel,
        grid_spec=grid_spec,
        out_shape=jax.ShapeDtypeStruct((cap, D), F32),
        compiler_params=_params(("arbitrary", "arbitrary"), VMEM_LIMIT),
        name="moe_experts",
    )(blk_e, n_used, xs, wg, wu, wd)


def _moe_route(choices):
    n = choices[0].shape[0]
    a = n * TOP_K
    flat_e = jnp.concatenate(choices)
    onehot = (flat_e[:, None] == jnp.arange(N_EXPERTS, dtype=flat_e.dtype)[None, :]).astype(jnp.int32)
    counts = jnp.sum(onehot, axis=0)
    padded = (counts + MOE_TM - 1) // MOE_TM * MOE_TM
    pad_ends = jnp.cumsum(padded)
    pad_starts = pad_ends - padded
    dest = jnp.sum(onehot * (jnp.cumsum(onehot, axis=0) - onehot + pad_starts[None, :]), axis=1)
    cap = a + N_EXPERTS * MOE_TM
    nblk = cap // MOE_TM
    row_tok = (jnp.arange(cap, dtype=jnp.int32) % n).at[dest].set(
        jnp.arange(a, dtype=jnp.int32) % n, unique_indices=True, mode="promise_in_bounds")
    blk_start = jnp.arange(nblk, dtype=jnp.int32) * MOE_TM
    blk_e = jnp.minimum(jnp.sum((pad_ends[None, :] <= blk_start[:, None]).astype(jnp.int32), axis=1),
                        N_EXPERTS - 1)
    n_used = (pad_ends[-1] // MOE_TM).astype(jnp.int32).reshape(1)
    last_e = blk_e[jnp.maximum(n_used[0] - 1, 0)]
    blk_e = jnp.where(jnp.arange(nblk) < n_used[0], blk_e, last_e)
    return [dest[s * n:(s + 1) * n] for s in range(TOP_K)], row_tok, blk_e, n_used


def _final_kernel(x_ref, y0_ref, y1_ref, gt_ref, mod_ref, fg_ref, o_ref):
    mod = mod_ref[...]
    gt = gt_ref[...]
    f = gt[:, 0:1] * y0_ref[...] + gt[:, 1:2] * y1_ref[...]
    x = x_ref[...] + mod[0:1, 5 * D:6 * D] * f
    o_ref[...] = _rmsnorm_rows(x) * fg_ref[...]


def _moe_combine_final(x, y0, y1, gt, mod, final_g):
    n = x.shape[0]
    row = lambda i: (i, 0)
    const = lambda i: (0, 0)
    return pl.pallas_call(
        _final_kernel,
        grid=(n // TM,),
        in_specs=[pl.BlockSpec((TM, D), row)] * 3
        + [pl.BlockSpec((TM, LANE), row), pl.BlockSpec((8, N_MOD * D), const), pl.BlockSpec((1, D), const)],
        out_specs=pl.BlockSpec((TM, D), row),
        out_shape=jax.ShapeDtypeStruct((n, D), F32),
        compiler_params=_params(("arbitrary",), VMEM_LIMIT),
        name="moe_combine_final",
    )(x, y0, y1, gt, mod, final_g.reshape(1, D))


def _rope_tables(n):
    lane = jnp.arange(LANE, dtype=jnp.int32) % HEAD_DIM
    inv = ROPE_BASE ** (-jnp.arange(0, AXIS_DIM, 2, dtype=F32) / AXIS_DIM)
    freq = inv[lane % (AXIS_DIM // 2)]
    row_axis = (lane // AXIS_DIM) == 0
    sign = jnp.where((lane % AXIS_DIM) < AXIS_DIM // 2, -1.0, 1.0)
    reps = ATT_BLOCK // GRID_W

    def tables(count, on_axis):
        ang = jnp.arange(count, dtype=F32)[:, None] * freq[None, :]
        return jnp.where(on_axis, jnp.cos(ang), 0.0), jnp.where(on_axis, jnp.sin(ang) * sign, 0.0)

    by_block = lambda t: jnp.pad(t.reshape(-1, reps, LANE), ((0, 0), (0, 8 - reps), (0, 0)))
    cosr, sinr = tables(n // GRID_W, row_axis)
    cosc, sinc = tables(GRID_W, ~row_axis)
    return by_block(cosr), by_block(sinr), cosc, sinc


def _prep_w_in(w):
    pad = jnp.zeros((D, LANE - N_AB), w.dtype)
    return jnp.concatenate([w[:, :_C_Z], w[:, _C_A:_C_Q], pad, w[:, _C_Z:_C_A], w[:, _C_Q:_C_END]],
                           axis=1).astype(BF16)


def kernel(x, c, ctx, c_ctx, w_mod, b_mod, w_in, w_out, conv_w, dn_conv_w, dn_a_log, dn_dt_bias, dn_norm_g,
           attn_sink, ffn_w_gate, ffn_w_up, ffn_w_down, moe_router, moe_w_gate, moe_w_up, moe_w_down,
           final_norm_g):
    bsz, n, d = x.shape
    ctx_len = ctx.shape[1]
    depth = w_in.shape[0]
    assert bsz == 1 and d == D and ctx_len == TM and n % TM == 0 and n % GRID_W == 0
    rope = _rope_tables(n)
    mods = _mod_vectors(c, c_ctx, w_mod, b_mod)
    bd = _head_blockdiag(DN_DIM, DN_HEAD_DIM)
    stream = (ctx[0], x[0], 0)
    for layer in range(depth):
        last = layer == depth - 1
        mod = mods[layer]
        pconv, pz, pq, pkv, qn, kn, vv, gb = _in_proj(*stream, mod, _prep_w_in(w_in[layer]), dn_conv_w[layer],
                                                      dn_a_log[layer], dn_dt_bias[layer], bd)
        to_bf16 = []
        if layer % 2 == 0:
            to_bf16 += [w[layer // 2] for w in (ffn_w_gate, ffn_w_up, ffn_w_down)]
        if (layer + 1) % 2 == 1 and layer + 1 < depth:
            j = (layer + 1) // 2
            to_bf16 += [w[j].reshape(-1, w.shape[-1]) for w in (moe_w_gate, moe_w_up, moe_w_down)]
        o_f, o_b, *cast = _delta_net(qn, kn, vv, gb, ctx_len, tuple(to_bf16))
        if layer % 2 == 0:
            ffn_bf16, cast = cast[:3], cast[3:]
        if cast:
            moe_bf16 = [c.reshape(w.shape[1:]) for c, w in zip(cast, (moe_w_gate, moe_w_up, moe_w_down))]
        sink = jnp.zeros((8, LANE), F32).at[0, :ATT_HEADS].set(attn_sink[layer])
        yc = _attention(pq, pkv, rope, sink, ctx_len, with_ctx=not last)
        router = None
        if layer % 2 == 1:
            wr = jnp.zeros((D, LANE), F32).at[:, :N_EXPERTS].set(moe_router[layer // 2])
            wr1 = wr.astype(BF16)
            router = (wr1, (wr - wr1.astype(F32)).astype(BF16))
        outs = _mixer_finish(*stream, mod, pconv, conv_w[layer], o_f, o_b, pz, dn_norm_g[layer], yc,
                             w_out[layer].astype(BF16), bd, ctx_len, with_ctx=not last, router=router,
                             ffn=ffn_bf16 if layer % 2 == 0 else None)
        if layer % 2 == 0:
            assert not last
            h, = outs
            stream = (h, h, ctx_len // TM)
        else:
            assert last
            x1, hx, route = outs
            dest, row_tok, blk_e, n_used = _moe_route([route[:, 2 + s].astype(jnp.int32) for s in range(TOP_K)])
            take = lambda rows_, idx: rows_.at[idx].get(mode="promise_in_bounds")
            y = _moe_experts(take(hx, row_tok), blk_e, n_used, *moe_bf16)
            h = _moe_combine_final(x1, take(y, dest[0]), take(y, dest[1]), route, mod, final_norm_g)
    return h.reshape(bsz, n, d)
```

```python
import functools

import jax
import jax.numpy as jnp
from jax import lax
from jax.experimental import pallas as pl
from jax.experimental.pallas import tpu as pltpu

F32 = jnp.float32
BF16 = jnp.bfloat16

D = 1024
N_MOD = 6
EPS = 1e-6
NEG = -1e30
GRID_W = 64

CONV_CH = 256
DN_HEADS = 6
DN_HEAD_DIM = 64
DN_DIM = DN_HEADS * DN_HEAD_DIM
DN_CHUNK = 64
DN_SUB = 16
DN_STEP_CHUNKS = 4
ATT_HEADS = 6
ATT_KV_HEADS = 2
ATT_GROUP = ATT_HEADS // ATT_KV_HEADS
HEAD_DIM = 64
ATT_DIM = ATT_HEADS * HEAD_DIM
ATT_KV_DIM = ATT_KV_HEADS * HEAD_DIM
ATT_BLOCK = 128
ROPE_BASE = 10000.0
AXIS_DIM = HEAD_DIM // 2
MIX_DIM = CONV_CH + DN_DIM + ATT_DIM

D_FF = 2816
N_EXPERTS = 8
TOP_K = 2
D_FF_EXPERT = 3584

TM = 256
FF_CHUNK = 1024
MOE_TM = 512
MOE_TF = 1792
MOE_SUB = 256
COMBINE_TM = 1024
LANE = 128
VMEM_LIMIT = 56 * 1024 * 1024

_C_QKV = 3 * CONV_CH
_C_Z = _C_QKV + 3 * DN_DIM
_C_A = _C_Z + DN_DIM
_C_Q = _C_A + 4 * DN_HEADS
_C_K = _C_Q + ATT_DIM
_C_V = _C_K + ATT_KV_DIM
_C_END = _C_V + ATT_KV_DIM
N_AB = 4 * DN_HEADS


def _params(sem=None, vmem=None):
    kw = {}
    if sem is not None:
        kw["dimension_semantics"] = sem
    if vmem is not None:
        kw["vmem_limit_bytes"] = vmem
    return pltpu.CompilerParams(**kw)


def _split2(a):
    hi = a.astype(BF16)
    lo = (a - hi.astype(F32)).astype(BF16)
    return hi, lo


def _split3(a):
    hi = a.astype(BF16)
    r = a - hi.astype(F32)
    mid = r.astype(BF16)
    lo = (r - mid.astype(F32)).astype(BF16)
    return hi, mid, lo


_NN = (((1,), (0,)), ((), ()))
_NT = (((1,), (1,)), ((), ()))


def _mm(a, b, dims=_NN):
    return lax.dot_general(a.astype(BF16), b.astype(BF16), dims, preferred_element_type=F32)


def _mm3(a, b, dims=_NN):
    a1, a0 = _split2(a)
    b1, b0 = _split2(b)
    d = functools.partial(lax.dot_general, dimension_numbers=dims, preferred_element_type=F32)
    return d(a1, b1) + (d(a1, b0) + d(a0, b1))


_BNN = (((2,), (1,)), ((0,), (0,)))
_BNT = (((2,), (2,)), ((0,), (0,)))
_BTN = (((1,), (1,)), ((0,), (0,)))


def _bmm(a, b, dims=_BNN):
    return lax.dot_general(a.astype(BF16), b.astype(BF16), dims, preferred_element_type=F32)


def _sigmoid(x):
    return 1.0 / (1.0 + jnp.exp(-x))


def _softplus(x):
    return jnp.maximum(x, 0.0) + jnp.log1p(jnp.exp(-jnp.abs(x)))


def _mod_row(mod_ref, is_ctx):
    mod = mod_ref[...]
    return jnp.where(is_ctx, mod[1:2, :], mod[0:1, :])


def _rmsnorm_rows(x):
    return x * lax.rsqrt(jnp.mean(x * x, axis=-1, keepdims=True) + EPS)


def _shift_rows(u, prow, nrow):
    n = u.shape[0]
    rid = lax.broadcasted_iota(jnp.int32, u.shape, 0)
    up = jnp.where(rid == 0, prow, pltpu.roll(u, 1, 0))
    un = jnp.where(rid == n - 1, nrow, pltpu.roll(u, n - 1, 0))
    return up, un


def _same_group(shape, group):
    sh = group.bit_length() - 1
    assert 1 << sh == group
    return (lax.broadcasted_iota(jnp.int32, shape, 0) >> sh) == (lax.broadcasted_iota(jnp.int32, shape, 1) >> sh)


def _head_blockdiag(n, group):
    g = jnp.arange(n, dtype=jnp.int32) // group
    return (g[:, None] == g[None, :]).astype(BF16)


def _group_sum(t, bd):
    return _mm(t, bd)


MOD_TN = 1536


def _mod_kernel(s_ref, w_ref, b_ref, o_ref):
    s = s_ref[...]
    s = s * _sigmoid(s)
    o_ref[0] = _mm3(s, w_ref[0]) + b_ref[0]


def _mod_vectors(c, c_ctx, w_mod, b_mod):
    depth = w_mod.shape[0]
    s = jnp.zeros((8, D), F32).at[0].set(c[0]).at[1].set(c_ctx)
    return pl.pallas_call(
        _mod_kernel,
        grid=(depth, N_MOD * D // MOD_TN),
        in_specs=[
            pl.BlockSpec((8, D), lambda l, j: (0, 0)),
            pl.BlockSpec((1, D, MOD_TN), lambda l, j: (l, 0, j)),
            pl.BlockSpec((1, 1, MOD_TN), lambda l, j: (l, 0, j)),
        ],
        out_specs=pl.BlockSpec((1, 8, MOD_TN), lambda l, j: (l, 0, j)),
        out_shape=jax.ShapeDtypeStruct((depth, 8, N_MOD * D), F32),
        compiler_params=_params(("arbitrary", "arbitrary"), VMEM_LIMIT),
        name="mod_vectors",
    )(s, w_mod, b_mod.reshape(depth, 1, N_MOD * D))


def _halo_valid(i, nblk):
    return jnp.logical_and(i != 0, i != 1), jnp.logical_and(i != 0, i != nblk - 1)


def _halo_rows(prev_ref, next_ref, i, nblk):
    pvalid, nvalid = _halo_valid(i, nblk)
    prow = jnp.where(pvalid, prev_ref[7:8, :], 0.0)
    nrow = jnp.where(nvalid, next_ref[0:1, :], 0.0)
    return prow, nrow


def _in_kernel(ctx_ref, h_ref, hprev_ref, hnext_ref, mod_ref, w_ref, cw_ref, alog_ref, dtb_ref, bd_ref,
               pconv_ref, pz_ref, pq_ref, pkv_ref, q_ref, k_ref, v_ref, gb_ref, *, nblk):
    i = pl.program_id(0)
    row = _mod_row(mod_ref, i == 0)
    norm_mod = lambda x: _rmsnorm_rows(x) * (1.0 + row[:, D:2 * D]) + row[:, 0:D]
    h1 = norm_mod(jnp.where(i == 0, ctx_ref[...], h_ref[...])).astype(BF16)
    halo = norm_mod(jnp.concatenate([hprev_ref[...], hnext_ref[...]], axis=0)).astype(BF16)
    d = functools.partial(lax.dot_general, dimension_numbers=_NN, preferred_element_type=F32)
    tm = h1.shape[0]
    c0 = 3 * CONV_CH
    c1 = c0 + 3 * DN_DIM
    c2 = c1 + LANE
    c3 = c2 + DN_DIM + ATT_DIM
    pconv_ref[...] = d(h1, w_ref[:, 0:c0])
    zq = d(h1, w_ref[:, c2:c3])
    pz_ref[...] = zq[:, 0:DN_DIM]
    pq_ref[...] = zq[:, DN_DIM:]
    pkv_ref[...] = d(h1, w_ref[:, c3:c3 + 2 * ATT_KV_DIM])
    qkv_ab = d(jnp.concatenate([h1, halo], axis=0), w_ref[:, c0:c2])
    qkv = qkv_ab[:, 0:3 * DN_DIM]
    ab = qkv_ab[0:tm, 3 * DN_DIM:]

    u = qkv[0:tm, :]
    pvalid, nvalid = _halo_valid(i, nblk)
    prow = jnp.where(pvalid, qkv[tm + 7:tm + 8, :], 0.0)
    nrow = jnp.where(nvalid, qkv[tm + 8:tm + 9, :], 0.0)
    up, un = _shift_rows(u, prow, nrow)
    cw = cw_ref[...]
    y = up * cw[0:1, :] + u * cw[1:2, :] + un * cw[2:3, :]
    y = y * _sigmoid(y)
    q = y[:, 0:DN_DIM]
    k = y[:, DN_DIM:2 * DN_DIM]
    bd = bd_ref[...]
    q_ref[...] = q * lax.rsqrt(_group_sum(q * q, bd) + 1e-6) * (DN_HEAD_DIM ** -0.5)
    k_ref[...] = k * lax.rsqrt(_group_sum(k * k, bd) + 1e-6)
    v_ref[...] = y[:, 2 * DN_DIM:3 * DN_DIM]
    g = -jnp.exp(alog_ref[...]) * _softplus(ab + dtb_ref[...])
    lane = lax.broadcasted_iota(jnp.int32, ab.shape, 1)
    gb_ref[...] = jnp.where(lane < 2 * DN_HEADS, g, _sigmoid(ab))


def _in_proj(ctx_src, lat_src, lat_blk0, mod, w_main, dn_conv_w, a_log, dt_bias, bd):
    r8 = TM // 8
    last8 = lat_src.shape[0] // 8 - 1
    nblk = 1 + lat_src.shape[0] // TM - lat_blk0
    t = nblk * TM
    alog = jnp.zeros((1, LANE), F32).at[0, :2 * DN_HEADS].set(a_log.reshape(-1))
    dtb = jnp.zeros((1, LANE), F32).at[0, :2 * DN_HEADS].set(dt_bias.reshape(-1))
    widths = (3 * CONV_CH, DN_DIM, ATT_DIM, 2 * ATT_KV_DIM, DN_DIM, DN_DIM, DN_DIM, LANE)
    const = lambda i: (0, 0)
    lat = lambda i: jnp.maximum(i - 1, 0) + lat_blk0
    return pl.pallas_call(
        functools.partial(_in_kernel, nblk=nblk),
        grid=(nblk,),
        in_specs=[
            pl.BlockSpec((TM, D), const),
            pl.BlockSpec((TM, D), lambda i: (lat(i), 0)),
            pl.BlockSpec((8, D), lambda i: (jnp.maximum(lat(i) * r8 - 1, 0), 0)),
            pl.BlockSpec((8, D), lambda i: (jnp.minimum((lat(i) + 1) * r8, last8), 0)),
            pl.BlockSpec((8, N_MOD * D), const),
            pl.BlockSpec(w_main.shape, const),
            pl.BlockSpec((3, 3 * DN_DIM), const),
            pl.BlockSpec((1, LANE), const),
            pl.BlockSpec((1, LANE), const),
            pl.BlockSpec(bd.shape, const),
        ],
        out_specs=[pl.BlockSpec((TM, w), lambda i: (i, 0)) for w in widths],
        out_shape=[jax.ShapeDtypeStruct((t, w), F32) for w in widths],
        compiler_params=_params(("arbitrary",), VMEM_LIMIT),
        name="in_proj",
    )(ctx_src, lat_src, lat_src, lat_src, mod, w_main, dn_conv_w, alog, dtb, bd)


def _dn_block(fwd_refs, bwd_refs, of_ref, ob_ref, s_ref):
    c_ = DN_CHUNK
    nh = DN_HEADS
    nchunks = fwd_refs[0].shape[0] // c_
    nb = nchunks * nh
    rows = lambda g: slice(g * c_, (g + 1) * c_)
    lanes = lambda h: slice(h * DN_HEAD_DIM, (h + 1) * DN_HEAD_DIM)
    ri = lax.broadcasted_iota(jnp.int32, (c_, c_), 0)
    ci = lax.broadcasted_iota(jnp.int32, (c_, c_), 1)
    same_sub = _same_group((c_, c_), DN_SUB)
    eye = jnp.where(ri == ci, 1.0, 0.0)
    shape3 = (2 * nb, c_, c_)
    delta = lax.broadcasted_iota(jnp.int32, shape3, 1) - lax.broadcasted_iota(jnp.int32, shape3, 2)
    delta = jnp.where(lax.broadcasted_iota(jnp.int32, shape3, 0) >= nb, -delta, delta)
    incl = delta >= 0
    strict = delta > 0

    def stack(fn):
        return jnp.stack([fn(d, refs, g, h) for d, refs in enumerate((fwd_refs, bwd_refs))
                          for g in range(nchunks) for h in range(nh)])

    gbs = [refs[3][...] for refs in (fwd_refs, bwd_refs)]
    tris = [jnp.where(ri >= ci, 1.0, 0.0).astype(BF16), jnp.where(ri <= ci, 1.0, 0.0).astype(BF16)]
    gcs = [[_cumsum_rows(tris[d], gbs[d][rows(g), :]) for g in range(nchunks)] for d in range(2)]
    gcts = [[gc.T for gc in gcs[d]] for d in range(2)]
    q = stack(lambda d, refs, g, h: refs[0][rows(g), lanes(h)])
    k = stack(lambda d, refs, g, h: refs[1][rows(g), lanes(h)])
    v = stack(lambda d, refs, g, h: refs[2][rows(g), lanes(h)])
    col = lambda d, h: d * nh + h
    gcol = stack(lambda d, refs, g, h: gcs[d][g][:, col(d, h):col(d, h) + 1])
    grow = stack(lambda d, refs, g, h: gcts[d][g][col(d, h):col(d, h) + 1, :])
    beta = stack(lambda d, refs, g, h: gbs[d][rows(g), 2 * nh + col(d, h):2 * nh + col(d, h) + 1])
    glast = jnp.concatenate([gcol[:nb, c_ - 1:c_, :], gcol[nb:, 0:1, :]], axis=0)
    decay = jnp.where(incl, jnp.exp(jnp.where(incl, gcol - grow, 0.0)), 0.0)
    eg = jnp.exp(gcol)
    kb = k * beta
    a = jnp.where(strict, _bmm(kb, k, _BNT) * decay, 0.0)
    qk = jnp.where(incl, _bmm(q, k, _BNT) * decay, 0.0)
    ad = jnp.where(same_sub, a, 0.0)
    ao = a - ad
    p = eye - ad
    n2 = _bmm(ad, ad)
    p = p + _bmm(p, n2)
    n4 = _bmm(n2, n2)
    p = p + _bmm(p, n4)
    n8 = _bmm(n4, n4)
    dinv = p + _bmm(p, n8)
    m = _bmm(dinv, ao)
    m2 = _bmm(m, m)
    y = _bmm(dinv, jnp.concatenate([v * beta, kb * eg], axis=-1))
    z = y + _bmm(m2, y)
    x = z - _bmm(m, z)
    u = x[:, :, :DN_HEAD_DIM]
    w = x[:, :, DN_HEAD_DIM:]
    qg = q * eg
    kd = k * jnp.exp(glast - gcol)
    gl = jnp.exp(glast)
    s = s_ref[...]
    for t in range(nchunks):
        gf, gr = t, nchunks - 1 - t
        step = lambda a: jnp.concatenate([a[gf * nh:(gf + 1) * nh], a[nb + gr * nh:nb + (gr + 1) * nh]], axis=0)
        v_new = step(u) - _bmm(step(w), s)
        o = _bmm(step(qg), s) + _bmm(step(qk), v_new)
        s = s * step(gl) + _bmm(step(kd), v_new, _BTN)
        for h in range(nh):
            of_ref[rows(gf), lanes(h)] = o[h]
            ob_ref[rows(gr), lanes(h)] = o[nh + h]
    s_ref[...] = s


def _cumsum_rows(tri_bf16, g):
    g2, g1, g0 = _split3(g)
    d = functools.partial(lax.dot_general, dimension_numbers=_NN, preferred_element_type=F32)
    return d(tri_bf16, g2) + (d(tri_bf16, g1) + d(tri_bf16, g0))


def _dn_kernel(qf, kf, vf, gf, qb, kb, vb, gbb, *rest, n_cast):
    cast_in = rest[:n_cast]
    of_ref, ob_ref = rest[n_cast:n_cast + 2]
    cast_out = rest[n_cast + 2:2 * n_cast + 2]
    s_ref = rest[-1]

    @pl.when(pl.program_id(0) == 0)
    def _():
        s_ref[...] = jnp.zeros(s_ref.shape, F32)

    _dn_block((qf, kf, vf, gf), (qb, kb, vb, gbb), of_ref, ob_ref, s_ref)
    for src, dst in zip(cast_in, cast_out):
        dst[...] = src[...].astype(BF16)


def _delta_net(q, k, v, gb, ctx_len, to_bf16=()):
    t = q.shape[0]
    rows = DN_STEP_CHUNKS * DN_CHUNK
    assert ctx_len == rows and t % rows == 0
    nstep = t // rows
    cast_specs = []
    for m in to_bf16:
        rb = -(-m.shape[0] // nstep)
        rb = -(-rb // 16) * 16
        nb = -(-m.shape[0] // rb)
        cast_specs.append(pl.BlockSpec((rb, m.shape[1]), lambda s, nb=nb: (jnp.minimum(s, nb - 1), 0)))

    def fwd(s):
        return (s, 0)

    def bwd(s):
        return (jnp.where(s == 0, 0, nstep - s), 0)

    wide = lambda im: pl.BlockSpec((rows, DN_DIM), im)
    narrow = lambda im: pl.BlockSpec((rows, LANE), im)
    return pl.pallas_call(
        functools.partial(_dn_kernel, n_cast=len(to_bf16)),
        grid=(nstep,),
        in_specs=[wide(fwd), wide(fwd), wide(fwd), narrow(fwd), wide(bwd), wide(bwd), wide(bwd), narrow(bwd)]
        + cast_specs,
        out_specs=[wide(fwd), wide(bwd)] + cast_specs,
        out_shape=[jax.ShapeDtypeStruct((t, DN_DIM), F32)] * 2
        + [jax.ShapeDtypeStruct(m.shape, BF16) for m in to_bf16],
        scratch_shapes=[pltpu.VMEM((2 * DN_HEADS, DN_HEAD_DIM, DN_HEAD_DIM), F32)],
        compiler_params=_params(("arbitrary",), VMEM_LIMIT),
        name="delta_net",
    )(q, k, v, gb, q, k, v, gb, *to_bf16)


def _rope(x, cos, sin):
    w = x.shape[1]
    lane = lax.broadcasted_iota(jnp.int32, x.shape, 1)
    first_half = (lane & (AXIS_DIM - 1)) < (AXIS_DIM // 2)
    swapped = jnp.where(first_half, pltpu.roll(x, w - AXIS_DIM // 2, 1), pltpu.roll(x, AXIS_DIM // 2, 1))
    return x * cos + swapped * sin


LOG2E = 1.4426950408889634


def _softmax_av(s, sink, vals):
    m = jnp.maximum(jnp.max(s, axis=-1, keepdims=True), sink)
    p = jnp.exp2(s - m)
    denom = jnp.sum(p, axis=-1, keepdims=True) + jnp.exp2(sink - m)
    return _mm(p, vals) / denom


def _attend(q, keys, vals, band, sink_all, o_ref, row0=0):
    b = q.shape[0]
    for kvh in range(ATT_KV_HEADS):
        kl = slice(kvh * HEAD_DIM, (kvh + 1) * HEAD_DIM)
        heads = range(kvh * ATT_GROUP, (kvh + 1) * ATT_GROUP)
        qs = jnp.concatenate([q[:, h * HEAD_DIM:(h + 1) * HEAD_DIM] for h in heads], axis=0)
        sink = jnp.concatenate([jnp.broadcast_to(sink_all[0:1, h:h + 1], (b, 1)) for h in heads], axis=0)
        s = _mm(qs, keys[:, kl], _NT)
        if band is not None:
            kb = ATT_BLOCK
            s = jnp.concatenate([jnp.where(band[0], s[:, 0:kb], NEG), s[:, kb:2 * kb],
                                 jnp.where(band[1], s[:, 2 * kb:3 * kb], NEG), s[:, 3 * kb:]], axis=1)
        o = _softmax_av(s, sink * LOG2E, vals[:, kl])
        for g, h in enumerate(heads):
            o_ref[row0:row0 + b, h * HEAD_DIM:(h + 1) * HEAD_DIM] = o[g * b:(g + 1) * b, :]


def _band_valid(first, last):
    b = ATT_BLOCK
    c = lax.broadcasted_iota(jnp.int32, (1, b), 1)
    r = lax.broadcasted_iota(jnp.int32, (ATT_GROUP * b, 1), 0) & (b - 1)
    prev_ok = jnp.where(first, -1, c) >= r
    next_ok = jnp.where(last, b, c) <= r
    return prev_ok, next_ok


def _rope_block(rowtab_ref, coltab_ref, blk):
    rt = rowtab_ref[blk]
    ct = coltab_ref[...]
    reps = ATT_BLOCK // GRID_W
    rows = jnp.concatenate([jnp.broadcast_to(rt[g:g + 1, :], (GRID_W, LANE)) for g in range(reps)], axis=0)
    return rows + jnp.concatenate([ct] * reps, axis=0)


ATT_STEP_BLOCKS = 4


def _attn_ctx_kernel(q_ref, kctx_ref, sink_ref, o_ref):
    kvx = kctx_ref[...]
    scale = HEAD_DIM ** -0.5 * LOG2E
    _attend(q_ref[...] * scale, kvx[:, :ATT_KV_DIM], kvx[:, ATT_KV_DIM:], None, sink_ref[...], o_ref)


def _attn_lat_kernel(*refs, nb):
    ns = ATT_STEP_BLOCKS
    q_refs, kv_refs = refs[:ns], refs[ns:2 * ns + 2]
    kctx_ref, cosr_ref, sinr_ref, cosc_ref, sinc_ref, sink_ref, o_ref = refs[2 * ns + 2:]
    b = ATT_BLOCK
    scale = HEAD_DIM ** -0.5 * LOG2E
    b0 = ns * pl.program_id(0)
    blocks = [jnp.clip(b0 - 1 + t, 0, nb - 1) for t in range(ns + 2)]
    cos = [_rope_block(cosr_ref, cosc_ref, blk) for blk in blocks]
    sin = [_rope_block(sinr_ref, sinc_ref, blk) for blk in blocks]
    kv = [r[...] for r in kv_refs]
    kvx = kctx_ref[...]
    keys = [_rope(t[:, :ATT_KV_DIM], c_, s_) for t, c_, s_ in zip(kv, cos, sin)]
    sink_all = sink_ref[...]
    for sub in range(ns):
        q = q_refs[sub][...]
        q = jnp.concatenate([_rope(q[:, l * LANE:(l + 1) * LANE], cos[1 + sub], sin[1 + sub])
                             for l in range(ATT_DIM // LANE)], axis=1)
        kcat = jnp.concatenate(keys[sub:sub + 3] + [kvx[:, :ATT_KV_DIM]], axis=0)
        vcat = jnp.concatenate([t[:, ATT_KV_DIM:] for t in kv[sub:sub + 3]] + [kvx[:, ATT_KV_DIM:]], axis=0)
        band = _band_valid(first=(b0 + sub == 0), last=(b0 + sub == nb - 1))
        _attend(q * scale, kcat, vcat, band, sink_all, o_ref, row0=sub * b)


def _attention(pq, pkv, rope, sink, ctx_len, with_ctx):
    t = pq.shape[0]
    n = t - ctx_len
    nb = n // ATT_BLOCK
    ns = ATT_STEP_BLOCKS
    assert nb % ns == 0 and ctx_len % ATT_BLOCK == 0
    off = ctx_len // ATT_BLOCK
    whole = lambda a: pl.BlockSpec(a.shape, lambda j: (0,) * a.ndim)
    ctx_kv = pl.BlockSpec((ctx_len, 2 * ATT_KV_DIM), lambda j: (0, 0))
    sink_spec = pl.BlockSpec((8, LANE), lambda j: (0, 0))
    yc_ctx = None
    if with_ctx:
        yc_ctx = pl.pallas_call(
            _attn_ctx_kernel,
            grid=(1,),
            in_specs=[pl.BlockSpec((ctx_len, ATT_DIM), lambda j: (0, 0)), ctx_kv, sink_spec],
            out_specs=pl.BlockSpec((ctx_len, ATT_DIM), lambda j: (0, 0)),
            out_shape=jax.ShapeDtypeStruct((ctx_len, ATT_DIM), F32),
            compiler_params=_params(("arbitrary",), VMEM_LIMIT),
            name="context_attention",
        )(pq, pkv, sink)
    lat = lambda k: jnp.clip(k, 0, nb - 1) + off
    q_specs = [pl.BlockSpec((ATT_BLOCK, ATT_DIM), lambda j, s=s: (lat(ns * j + s), 0)) for s in range(ns)]
    kv_specs = [pl.BlockSpec((ATT_BLOCK, 2 * ATT_KV_DIM), lambda j, s=s: (lat(ns * j - 1 + s), 0))
                for s in range(ns + 2)]
    yc_lat = pl.pallas_call(
        functools.partial(_attn_lat_kernel, nb=nb),
        grid=(nb // ns,),
        in_specs=q_specs + kv_specs + [ctx_kv] + [whole(a) for a in rope] + [sink_spec],
        out_specs=pl.BlockSpec((ns * ATT_BLOCK, ATT_DIM), lambda j: (j, 0)),
        out_shape=jax.ShapeDtypeStruct((n, ATT_DIM), F32),
        compiler_params=_params(("arbitrary",), VMEM_LIMIT),
        name="attention",
    )(*([pq] * ns + [pkv] * (ns + 3) + list(rope) + [sink]))
    return yc_ctx, yc_lat


def _mixfin_kernel(ctx_ref, h_ref, mod_ref, pconv_ref, prev_ref, next_ref, cw_ref, of_ref, ob_ref, z_ref, ng_ref,
                   ycc_ref, ycl_ref, wout_ref, bd_ref, *rest, nblk, blk0, with_router):
    if with_router:
        wr1_ref, wr0_ref, x_ref, hx_ref, lg_ref = rest
    else:
        wg_ref, wu_ref, wd_ref, x_ref = rest
    i = pl.program_id(0) + blk0
    row = _mod_row(mod_ref, i == 0)
    pc = pconv_ref[...]
    u = pc[:, CONV_CH:2 * CONV_CH] * pc[:, 2 * CONV_CH:]
    prow, nrow = _halo_rows(prev_ref, next_ref, i, nblk)
    prow = prow[:, CONV_CH:2 * CONV_CH] * prow[:, 2 * CONV_CH:]
    nrow = nrow[:, CONV_CH:2 * CONV_CH] * nrow[:, 2 * CONV_CH:]
    up, un = _shift_rows(u, prow, nrow)
    cw = cw_ref[...]
    ya = pc[:, :CONV_CH] * (up * cw[0:1, :] + u * cw[1:2, :] + un * cw[2:3, :])
    o = of_ref[...] + ob_ref[...]
    ms = _group_sum(o * o, bd_ref[...]) * (1.0 / DN_HEAD_DIM)
    z = z_ref[...]
    yb = o * lax.rsqrt(ms + EPS) * ng_ref[...] * (z * _sigmoid(z))
    mix = jnp.concatenate([ya, yb, jnp.where(i == 0, ycc_ref[...], ycl_ref[...])], axis=1)
    x = jnp.where(i == 0, ctx_ref[...], h_ref[...]) + row[:, 2 * D:3 * D] * _mm(mix, wout_ref[...])
    hx = _rmsnorm_rows(x) * (1.0 + row[:, 4 * D:5 * D]) + row[:, 3 * D:4 * D]
    if not with_router:
        hb = hx.astype(BF16)
        acc = jnp.zeros((hb.shape[0], D), F32)
        for f in range(0, D_FF, FF_CHUNK):
            fe = min(f + FF_CHUNK, D_FF)
            g = _mm(hb, wg_ref[:, f:fe])
            u_ = _mm(hb, wu_ref[:, f:fe])
            acc = acc + _mm(g * _sigmoid(g) * u_, wd_ref[f:fe, :])
        x_ref[...] = x + row[:, 5 * D:6 * D] * acc
    else:
        x_ref[...] = x
        hx_ref[...] = hx
        h1, h0 = _split2(hx)
        d = functools.partial(lax.dot_general, dimension_numbers=_NN, preferred_element_type=F32)
        lg = d(h1, wr1_ref[...]) + (d(h0, wr1_ref[...]) + d(h1, wr0_ref[...]))
        lane = lax.broadcasted_iota(jnp.int32, lg.shape, 1)
        lanef = lane.astype(F32)
        lg = jnp.where(lane < N_EXPERTS, lg, -jnp.inf)
        m1 = jnp.max(lg, axis=-1, keepdims=True)
        i1 = jnp.min(jnp.where(lg == m1, lanef, float(LANE)), axis=-1, keepdims=True)
        rest = jnp.where(lanef == i1, -jnp.inf, lg)
        m2 = jnp.max(rest, axis=-1, keepdims=True)
        i2 = jnp.min(jnp.where(rest == m2, lanef, float(LANE)), axis=-1, keepdims=True)
        e2 = jnp.exp(m2 - m1)
        g1 = 1.0 / (1.0 + e2)
        lg_ref[...] = jnp.where(lane == 0, g1, jnp.where(lane == 1, e2 * g1, jnp.where(lane == 2, i1, i2)))


def _mixer_finish(ctx_src, lat_src, lat_blk0, mod, pconv, conv_w, o_f, o_b, pz, norm_g, yc, w_out, bd, ctx_len,
                  with_ctx, router=None, ffn=None):
    assert (router is None) != (ffn is None)
    t = pconv.shape[0]
    nblk = t // TM
    blk0 = 0 if with_ctx else ctx_len // TM
    rows = t - blk0 * TM
    r8 = TM // 8
    w = pconv.shape[1]
    cur = lambda i: (i + blk0, 0)
    out_cur = lambda i: (i, 0)
    const = lambda i: (0, 0)
    ng = jnp.tile(norm_g.reshape(1, DN_HEAD_DIM), (1, DN_HEADS))
    in_specs = [
        pl.BlockSpec((TM, D), const),
        pl.BlockSpec((TM, D), lambda i: (jnp.maximum(i + blk0 - 1, 0) + lat_blk0, 0)),
        pl.BlockSpec((8, N_MOD * D), const),
        pl.BlockSpec((TM, w), cur),
        pl.BlockSpec((8, w), lambda i: (jnp.maximum((i + blk0) * r8 - 1, 0), 0)),
        pl.BlockSpec((8, w), lambda i: (jnp.minimum((i + blk0 + 1) * r8, t // 8 - 1), 0)),
        pl.BlockSpec((3, CONV_CH), const),
        pl.BlockSpec((TM, DN_DIM), cur),
        pl.BlockSpec((TM, DN_DIM), cur),
        pl.BlockSpec((TM, DN_DIM), cur),
        pl.BlockSpec((1, DN_DIM), const),
        pl.BlockSpec((TM, ATT_DIM), const),
        pl.BlockSpec((TM, ATT_DIM), lambda i: (jnp.maximum(i + blk0 - 1, 0), 0)),
        pl.BlockSpec((MIX_DIM, D), const),
        pl.BlockSpec(bd.shape, const),
    ]
    yc_ctx, yc_lat = yc
    if yc_ctx is None:
        assert not with_ctx
        yc_ctx = yc_lat
    args = [ctx_src, lat_src, mod, pconv, pconv, pconv, conv_w, o_f, o_b, pz, ng, yc_ctx, yc_lat, w_out, bd]
    out_specs = [pl.BlockSpec((TM, D), out_cur)]
    out_shape = [jax.ShapeDtypeStruct((rows, D), F32)]
    if router is not None:
        in_specs += [pl.BlockSpec((D, LANE), const)] * 2
        args += list(router)
        out_specs += [pl.BlockSpec((TM, D), out_cur), pl.BlockSpec((TM, LANE), out_cur)]
        out_shape += [jax.ShapeDtypeStruct((rows, D), F32), jax.ShapeDtypeStruct((rows, LANE), F32)]
    else:
        once = dict(pipeline_mode=pl.Buffered(1))
        in_specs += [pl.BlockSpec(w_.shape, const, **once) for w_ in ffn]
        args += list(ffn)
    return pl.pallas_call(
        functools.partial(_mixfin_kernel, nblk=nblk, blk0=blk0, with_router=router is not None),
        grid=(rows // TM,),
        in_specs=in_specs,
        out_specs=out_specs,
        out_shape=out_shape,
        compiler_params=_params(("arbitrary",), VMEM_LIMIT),
        name="mixer_finish",
    )(*args)


def _moe_kernel(be_ref, nu_ref, xs_ref, wg_ref, wu_ref, wd_ref, y_ref, acc_ref):
    b = pl.program_id(0)
    f = pl.program_id(1)

    @pl.when(b < nu_ref[0])
    def _():
        xs = xs_ref[...].astype(BF16)
        part = jnp.zeros((MOE_TM, D), F32)
        for c in range(0, MOE_TF, MOE_SUB):
            g = _mm(xs, wg_ref[0, :, c:c + MOE_SUB])
            u = _mm(xs, wu_ref[0, :, c:c + MOE_SUB])
            part = part + _mm(g * _sigmoid(g) * u, wd_ref[0, c:c + MOE_SUB, :])

        @pl.when(f == 0)
        def _():
            acc_ref[...] = part

        @pl.when(f != 0)
        def _():
            acc_ref[...] += part

        @pl.when(f == pl.num_programs(1) - 1)
        def _():
            y_ref[...] = acc_ref[...]

    @pl.when(b >= nu_ref[0])
    def _():
        y_ref[...] = jnp.zeros(y_ref.shape, F32)


def _moe_experts(xs, blk_e, n_used, wg, wu, wd):
    cap = xs.shape[0]
    nblk = cap // MOE_TM
    nf = D_FF_EXPERT // MOE_TF

    def fidx(b, f, nu):
        return jnp.where(b < nu[0], f, nf - 1)

    grid_spec = pltpu.PrefetchScalarGridSpec(
        num_scalar_prefetch=2,
        grid=(nblk, nf),
        in_specs=[
            pl.BlockSpec((MOE_TM, D), lambda b, f, be, nu: (b, 0)),
            pl.BlockSpec((1, D, MOE_TF), lambda b, f, be, nu: (be[b], 0, fidx(b, f, nu))),
            pl.BlockSpec((1, D, MOE_TF), lambda b, f, be, nu: (be[b], 0, fidx(b, f, nu))),
            pl.BlockSpec((1, MOE_TF, D), lambda b, f, be, nu: (be[b], fidx(b, f, nu), 0)),
        ],
        out_specs=pl.BlockSpec((MOE_TM, D), lambda b, f, be, nu: (b, 0)),
        scratch_shapes=[pltpu.VMEM((MOE_TM, D), F32)],
    )
    return pl.pallas_call(
        _moe_kernel,
        grid_spec=grid_spec,
        out_shape=jax.ShapeDtypeStruct((cap, D), F32),
        compiler_params=_params(("arbitrary", "arbitrary"), VMEM_LIMIT),
        name="moe_experts",
    )(blk_e, n_used, xs, wg, wu, wd)


def _moe_route(choices):
    n = choices[0].shape[0]
    a = n * TOP_K
    flat_e = jnp.concatenate(choices)
    onehot = (flat_e[:, None] == jnp.arange(N_EXPERTS, dtype=flat_e.dtype)[None, :]).astype(jnp.int32)
    counts = jnp.sum(onehot, axis=0)
    padded = (counts + MOE_TM - 1) // MOE_TM * MOE_TM
    pad_ends = jnp.cumsum(padded)
    pad_starts = pad_ends - padded
    dest = jnp.sum(onehot * (jnp.cumsum(onehot, axis=0) - onehot + pad_starts[None, :]), axis=1)
    cap = a + N_EXPERTS * MOE_TM
    nblk = cap // MOE_TM
    row_tok = (jnp.arange(cap, dtype=jnp.int32) % n).at[dest].set(
        jnp.arange(a, dtype=jnp.int32) % n, unique_indices=True, mode="promise_in_bounds")
    blk_start = jnp.arange(nblk, dtype=jnp.int32) * MOE_TM
    blk_e = jnp.minimum(jnp.sum((pad_ends[None, :] <= blk_start[:, None]).astype(jnp.int32), axis=1),
                        N_EXPERTS - 1)
    n_used = (pad_ends[-1] // MOE_TM).astype(jnp.int32).reshape(1)
    last_e = blk_e[jnp.maximum(n_used[0] - 1, 0)]
    blk_e = jnp.where(jnp.arange(nblk) < n_used[0], blk_e, last_e)
    return [dest[s * n:(s + 1) * n] for s in range(TOP_K)], row_tok, blk_e, n_used


def _final_kernel(x_ref, y0_ref, y1_ref, gt_ref, mod_ref, fg_ref, o_ref):
    mod = mod_ref[...]
    gt = gt_ref[...]
    f = gt[:, 0:1] * y0_ref[...] + gt[:, 1:2] * y1_ref[...]
    x = x_ref[...] + mod[0:1, 5 * D:6 * D] * f
    o_ref[...] = _rmsnorm_rows(x) * fg_ref[...]


def _moe_combine_final(x, y0, y1, gt, mod, final_g):
    n = x.shape[0]
    tm = COMBINE_TM
    assert n % tm == 0
    row = lambda i: (i, 0)
    const = lambda i: (0, 0)
    return pl.pallas_call(
        _final_kernel,
        grid=(n // tm,),
        in_specs=[pl.BlockSpec((tm, D), row)] * 3
        + [pl.BlockSpec((tm, LANE), row), pl.BlockSpec((8, N_MOD * D), const), pl.BlockSpec((1, D), const)],
        out_specs=pl.BlockSpec((tm, D), row),
        out_shape=jax.ShapeDtypeStruct((n, D), F32),
        compiler_params=_params(("arbitrary",), VMEM_LIMIT),
        name="moe_combine_final",
    )(x, y0, y1, gt, mod, final_g.reshape(1, D))


def _rope_tables(n):
    lane = jnp.arange(LANE, dtype=jnp.int32) % HEAD_DIM
    inv = ROPE_BASE ** (-jnp.arange(0, AXIS_DIM, 2, dtype=F32) / AXIS_DIM)
    freq = inv[lane % (AXIS_DIM // 2)]
    row_axis = (lane // AXIS_DIM) == 0
    sign = jnp.where((lane % AXIS_DIM) < AXIS_DIM // 2, -1.0, 1.0)
    reps = ATT_BLOCK // GRID_W

    def tables(count, on_axis):
        ang = jnp.arange(count, dtype=F32)[:, None] * freq[None, :]
        return jnp.where(on_axis, jnp.cos(ang), 0.0), jnp.where(on_axis, jnp.sin(ang) * sign, 0.0)

    by_block = lambda t: jnp.pad(t.reshape(-1, reps, LANE), ((0, 0), (0, 8 - reps), (0, 0)))
    cosr, sinr = tables(n // GRID_W, row_axis)
    cosc, sinc = tables(GRID_W, ~row_axis)
    return by_block(cosr), by_block(sinr), cosc, sinc


def _prep_w_in(w):
    pad = jnp.zeros((D, LANE - N_AB), w.dtype)
    return jnp.concatenate([w[:, :_C_Z], w[:, _C_A:_C_Q], pad, w[:, _C_Z:_C_A], w[:, _C_Q:_C_END]],
                           axis=1).astype(BF16)


def kernel(x, c, ctx, c_ctx, w_mod, b_mod, w_in, w_out, conv_w, dn_conv_w, dn_a_log, dn_dt_bias, dn_norm_g,
           attn_sink, ffn_w_gate, ffn_w_up, ffn_w_down, moe_router, moe_w_gate, moe_w_up, moe_w_down,
           final_norm_g):
    bsz, n, d = x.shape
    ctx_len = ctx.shape[1]
    depth = w_in.shape[0]
    assert bsz == 1 and d == D and ctx_len == TM and n % TM == 0 and n % GRID_W == 0
    rope = _rope_tables(n)
    mods = _mod_vectors(c, c_ctx, w_mod, b_mod)
    bd = _head_blockdiag(DN_DIM, DN_HEAD_DIM)
    stream = (ctx[0], x[0], 0)
    for layer in range(depth):
        last = layer == depth - 1
        mod = mods[layer]
        pconv, pz, pq, pkv, qn, kn, vv, gb = _in_proj(*stream, mod, _prep_w_in(w_in[layer]), dn_conv_w[layer],
                                                      dn_a_log[layer], dn_dt_bias[layer], bd)
        to_bf16 = []
        if layer % 2 == 0:
            to_bf16 += [w[layer // 2] for w in (ffn_w_gate, ffn_w_up, ffn_w_down)]
        if (layer + 1) % 2 == 1 and layer + 1 < depth:
            j = (layer + 1) // 2
            to_bf16 += [w[j].reshape(-1, w.shape[-1]) for w in (moe_w_gate, moe_w_up, moe_w_down)]
        o_f, o_b, *cast = _delta_net(qn, kn, vv, gb, ctx_len, tuple(to_bf16))
        if layer % 2 == 0:
            ffn_bf16, cast = cast[:3], cast[3:]
        if cast:
            moe_bf16 = [c.reshape(w.shape[1:]) for c, w in zip(cast, (moe_w_gate, moe_w_up, moe_w_down))]
        sink = jnp.zeros((8, LANE), F32).at[0, :ATT_HEADS].set(attn_sink[layer])
        yc = _attention(pq, pkv, rope, sink, ctx_len, with_ctx=not last)
        router = None
        if layer % 2 == 1:
            wr = jnp.zeros((D, LANE), F32).at[:, :N_EXPERTS].set(moe_router[layer // 2])
            wr1 = wr.astype(BF16)
            router = (wr1, (wr - wr1.astype(F32)).astype(BF16))
        outs = _mixer_finish(*stream, mod, pconv, conv_w[layer], o_f, o_b, pz, dn_norm_g[layer], yc,
                             w_out[layer].astype(BF16), bd, ctx_len, with_ctx=not last, router=router,
                             ffn=ffn_bf16 if layer % 2 == 0 else None)
        if layer % 2 == 0:
            assert not last
            h, = outs
            stream = (h, h, ctx_len // TM)
        else:
            assert last
            x1, hx, route = outs
            dest, row_tok, blk_e, n_used = _moe_route([route[:, 2 + s].astype(jnp.int32) for s in range(TOP_K)])
            take = lambda rows_, idx: rows_.at[idx].get(mode="promise_in_bounds")
            y = _moe_experts(take(hx, row_tok), blk_e, n_used, *moe_bf16)
            h = _moe_combine_final(x1, take(y, dest[0]), take(y, dest[1]), route, mod, final_norm_g)
    return h.reshape(bsz, n, d)
```

```python
import functools

import jax
import jax.numpy as jnp
from jax import lax
from jax.experimental import pallas as pl
from jax.experimental.pallas import tpu as pltpu

F32 = jnp.float32
BF16 = jnp.bfloat16

D = 1024
N_MOD = 6
EPS = 1e-6
NEG = -1e30
GRID_W = 64

CONV_CH = 256
DN_HEADS = 6
DN_HEAD_DIM = 64
DN_DIM = DN_HEADS * DN_HEAD_DIM
DN_CHUNK = 64
DN_SUB = 16
DN_STEP_CHUNKS = 4
ATT_HEADS = 6
ATT_KV_HEADS = 2
ATT_GROUP = ATT_HEADS // ATT_KV_HEADS
HEAD_DIM = 64
ATT_DIM = ATT_HEADS * HEAD_DIM
ATT_KV_DIM = ATT_KV_HEADS * HEAD_DIM
ATT_BLOCK = 128
ROPE_BASE = 10000.0
AXIS_DIM = HEAD_DIM // 2
MIX_DIM = CONV_CH + DN_DIM + ATT_DIM

D_FF = 2816
N_EXPERTS = 8
TOP_K = 2
D_FF_EXPERT = 3584

TM = 256
FF_CHUNK = 1024
MOE_TM = 512
MOE_TF = 1792
MOE_SUB = 256
COMBINE_TM = 1024
LANE = 128
VMEM_LIMIT = 56 * 1024 * 1024

_C_QKV = 3 * CONV_CH
_C_Z = _C_QKV + 3 * DN_DIM
_C_A = _C_Z + DN_DIM
_C_Q = _C_A + 4 * DN_HEADS
_C_K = _C_Q + ATT_DIM
_C_V = _C_K + ATT_KV_DIM
_C_END = _C_V + ATT_KV_DIM
N_AB = 4 * DN_HEADS


def _params(sem=None, vmem=None):
    kw = {}
    if sem is not None:
        kw["dimension_semantics"] = sem
    if vmem is not None:
        kw["vmem_limit_bytes"] = vmem
    return pltpu.CompilerParams(**kw)


def _split2(a):
    hi = a.astype(BF16)
    lo = (a - hi.astype(F32)).astype(BF16)
    return hi, lo


def _split3(a):
    hi = a.astype(BF16)
    r = a - hi.astype(F32)
    mid = r.astype(BF16)
    lo = (r - mid.astype(F32)).astype(BF16)
    return hi, mid, lo


_NN = (((1,), (0,)), ((), ()))
_NT = (((1,), (1,)), ((), ()))


def _mm(a, b, dims=_NN):
    return lax.dot_general(a.astype(BF16), b.astype(BF16), dims, preferred_element_type=F32)


def _mm3(a, b, dims=_NN):
    a1, a0 = _split2(a)
    b1, b0 = _split2(b)
    d = functools.partial(lax.dot_general, dimension_numbers=dims, preferred_element_type=F32)
    return d(a1, b1) + (d(a1, b0) + d(a0, b1))


_BNN = (((2,), (1,)), ((0,), (0,)))
_BNT = (((2,), (2,)), ((0,), (0,)))
_BTN = (((1,), (1,)), ((0,), (0,)))


def _bmm(a, b, dims=_BNN):
    return lax.dot_general(a.astype(BF16), b.astype(BF16), dims, preferred_element_type=F32)


def _sigmoid(x):
    return 1.0 / (1.0 + jnp.exp(-x))


def _softplus(x):
    return jnp.maximum(x, 0.0) + jnp.log1p(jnp.exp(-jnp.abs(x)))


def _mod_row(mod_ref, is_ctx):
    mod = mod_ref[...]
    return jnp.where(is_ctx, mod[1:2, :], mod[0:1, :])


def _rmsnorm_rows(x):
    return x * lax.rsqrt(jnp.mean(x * x, axis=-1, keepdims=True) + EPS)


def _shift_rows(u, prow, nrow):
    n = u.shape[0]
    rid = lax.broadcasted_iota(jnp.int32, u.shape, 0)
    up = jnp.where(rid == 0, prow, pltpu.roll(u, 1, 0))
    un = jnp.where(rid == n - 1, nrow, pltpu.roll(u, n - 1, 0))
    return up, un


def _same_group(shape, group):
    sh = group.bit_length() - 1
    assert 1 << sh == group
    return (lax.broadcasted_iota(jnp.int32, shape, 0) >> sh) == (lax.broadcasted_iota(jnp.int32, shape, 1) >> sh)


def _head_blockdiag(n, group):
    g = jnp.arange(n, dtype=jnp.int32) // group
    return (g[:, None] == g[None, :]).astype(BF16)


def _group_sum(t, bd):
    return _mm(t, bd)


MOD_TN = 1536


def _mod_kernel(s_ref, w_ref, b_ref, o_ref):
    s = s_ref[...]
    s = s * _sigmoid(s)
    o_ref[0] = _mm3(s, w_ref[0]) + b_ref[0]


def _mod_vectors(c, c_ctx, w_mod, b_mod):
    depth = w_mod.shape[0]
    s = jnp.zeros((8, D), F32).at[0].set(c[0]).at[1].set(c_ctx)
    return pl.pallas_call(
        _mod_kernel,
        grid=(depth, N_MOD * D // MOD_TN),
        in_specs=[
            pl.BlockSpec((8, D), lambda l, j: (0, 0)),
            pl.BlockSpec((1, D, MOD_TN), lambda l, j: (l, 0, j)),
            pl.BlockSpec((1, 1, MOD_TN), lambda l, j: (l, 0, j)),
        ],
        out_specs=pl.BlockSpec((1, 8, MOD_TN), lambda l, j: (l, 0, j)),
        out_shape=jax.ShapeDtypeStruct((depth, 8, N_MOD * D), F32),
        compiler_params=_params(("arbitrary", "arbitrary"), VMEM_LIMIT),
        name="mod_vectors",
    )(s, w_mod, b_mod.reshape(depth, 1, N_MOD * D))


def _halo_valid(i, nblk):
    return jnp.logical_and(i != 0, i != 1), jnp.logical_and(i != 0, i != nblk - 1)


def _halo_rows(prev_ref, next_ref, i, nblk):
    pvalid, nvalid = _halo_valid(i, nblk)
    prow = jnp.where(pvalid, prev_ref[7:8, :], 0.0)
    nrow = jnp.where(nvalid, next_ref[0:1, :], 0.0)
    return prow, nrow


def _in_kernel(ctx_ref, h_ref, hprev_ref, hnext_ref, mod_ref, w_ref, cw_ref, alog_ref, dtb_ref, bd_ref,
               pconv_ref, pz_ref, pq_ref, pkv_ref, q_ref, k_ref, v_ref, gb_ref, *, nblk):
    i = pl.program_id(0)
    row = _mod_row(mod_ref, i == 0)
    norm_mod = lambda x: _rmsnorm_rows(x) * (1.0 + row[:, D:2 * D]) + row[:, 0:D]
    h1 = norm_mod(jnp.where(i == 0, ctx_ref[...], h_ref[...])).astype(BF16)
    halo = norm_mod(jnp.concatenate([hprev_ref[...], hnext_ref[...]], axis=0)).astype(BF16)
    d = functools.partial(lax.dot_general, dimension_numbers=_NN, preferred_element_type=F32)
    tm = h1.shape[0]
    c0 = 3 * CONV_CH
    c1 = c0 + 3 * DN_DIM
    c2 = c1 + LANE
    c3 = c2 + DN_DIM + ATT_DIM
    pconv_ref[...] = d(h1, w_ref[:, 0:c0])
    zq = d(h1, w_ref[:, c2:c3])
    pz_ref[...] = zq[:, 0:DN_DIM]
    pq_ref[...] = zq[:, DN_DIM:]
    pkv_ref[...] = d(h1, w_ref[:, c3:c3 + 2 * ATT_KV_DIM])
    qkv_ab = d(jnp.concatenate([h1, halo], axis=0), w_ref[:, c0:c2])
    qkv = qkv_ab[:, 0:3 * DN_DIM]
    ab = qkv_ab[0:tm, 3 * DN_DIM:]

    u = qkv[0:tm, :]
    pvalid, nvalid = _halo_valid(i, nblk)
    prow = jnp.where(pvalid, qkv[tm + 7:tm + 8, :], 0.0)
    nrow = jnp.where(nvalid, qkv[tm + 8:tm + 9, :], 0.0)
    up, un = _shift_rows(u, prow, nrow)
    cw = cw_ref[...]
    y = up * cw[0:1, :] + u * cw[1:2, :] + un * cw[2:3, :]
    y = y * _sigmoid(y)
    q = y[:, 0:DN_DIM]
    k = y[:, DN_DIM:2 * DN_DIM]
    bd = bd_ref[...]
    q_ref[...] = q * lax.rsqrt(_group_sum(q * q, bd) + 1e-6) * (DN_HEAD_DIM ** -0.5)
    k_ref[...] = k * lax.rsqrt(_group_sum(k * k, bd) + 1e-6)
    v_ref[...] = y[:, 2 * DN_DIM:3 * DN_DIM]
    g = -jnp.exp(alog_ref[...]) * _softplus(ab + dtb_ref[...])
    lane = lax.broadcasted_iota(jnp.int32, ab.shape, 1)
    gb_ref[...] = jnp.where(lane < 2 * DN_HEADS, g, _sigmoid(ab))


def _in_proj(ctx_src, lat_src, lat_blk0, mod, w_main, dn_conv_w, a_log, dt_bias, bd):
    r8 = TM // 8
    last8 = lat_src.shape[0] // 8 - 1
    nblk = 1 + lat_src.shape[0] // TM - lat_blk0
    t = nblk * TM
    alog = jnp.zeros((1, LANE), F32).at[0, :2 * DN_HEADS].set(a_log.reshape(-1))
    dtb = jnp.zeros((1, LANE), F32).at[0, :2 * DN_HEADS].set(dt_bias.reshape(-1))
    widths = (3 * CONV_CH, DN_DIM, ATT_DIM, 2 * ATT_KV_DIM, DN_DIM, DN_DIM, DN_DIM, LANE)
    const = lambda i: (0, 0)
    lat = lambda i: jnp.maximum(i - 1, 0) + lat_blk0
    return pl.pallas_call(
        functools.partial(_in_kernel, nblk=nblk),
        grid=(nblk,),
        in_specs=[
            pl.BlockSpec((TM, D), const),
            pl.BlockSpec((TM, D), lambda i: (lat(i), 0)),
            pl.BlockSpec((8, D), lambda i: (jnp.maximum(lat(i) * r8 - 1, 0), 0)),
            pl.BlockSpec((8, D), lambda i: (jnp.minimum((lat(i) + 1) * r8, last8), 0)),
            pl.BlockSpec((8, N_MOD * D), const),
            pl.BlockSpec(w_main.shape, const),
            pl.BlockSpec((3, 3 * DN_DIM), const),
            pl.BlockSpec((1, LANE), const),
            pl.BlockSpec((1, LANE), const),
            pl.BlockSpec(bd.shape, const),
        ],
        out_specs=[pl.BlockSpec((TM, w), lambda i: (i, 0)) for w in widths],
        out_shape=[jax.ShapeDtypeStruct((t, w), F32) for w in widths],
        compiler_params=_params(("arbitrary",), VMEM_LIMIT),
        name="in_proj",
    )(ctx_src, lat_src, lat_src, lat_src, mod, w_main, dn_conv_w, alog, dtb, bd)


def _dn_block(fwd_refs, bwd_refs, of_ref, ob_ref, s_ref):
    c_ = DN_CHUNK
    nh = DN_HEADS
    nchunks = fwd_refs[0].shape[0] // c_
    nb = nchunks * nh
    rows = lambda g: slice(g * c_, (g + 1) * c_)
    lanes = lambda h: slice(h * DN_HEAD_DIM, (h + 1) * DN_HEAD_DIM)
    ri = lax.broadcasted_iota(jnp.int32, (c_, c_), 0)
    ci = lax.broadcasted_iota(jnp.int32, (c_, c_), 1)
    same_sub = _same_group((c_, c_), DN_SUB)
    eye = jnp.where(ri == ci, 1.0, 0.0)
    shape3 = (2 * nb, c_, c_)
    delta = lax.broadcasted_iota(jnp.int32, shape3, 1) - lax.broadcasted_iota(jnp.int32, shape3, 2)
    delta = jnp.where(lax.broadcasted_iota(jnp.int32, shape3, 0) >= nb, -delta, delta)
    incl = delta >= 0
    strict = delta > 0

    def stack(fn):
        return jnp.stack([fn(d, refs, g, h) for d, refs in enumerate((fwd_refs, bwd_refs))
                          for g in range(nchunks) for h in range(nh)])

    gbs = [refs[3][...] for refs in (fwd_refs, bwd_refs)]
    tris = [jnp.where(ri >= ci, 1.0, 0.0).astype(BF16), jnp.where(ri <= ci, 1.0, 0.0).astype(BF16)]
    gcs = [[_cumsum_rows(tris[d], gbs[d][rows(g), :]) for g in range(nchunks)] for d in range(2)]
    gcts = [[gc.T for gc in gcs[d]] for d in range(2)]
    q = stack(lambda d, refs, g, h: refs[0][rows(g), lanes(h)])
    k = stack(lambda d, refs, g, h: refs[1][rows(g), lanes(h)])
    v = stack(lambda d, refs, g, h: refs[2][rows(g), lanes(h)])
    col = lambda d, h: d * nh + h
    gcol = stack(lambda d, refs, g, h: gcs[d][g][:, col(d, h):col(d, h) + 1])
    grow = stack(lambda d, refs, g, h: gcts[d][g][col(d, h):col(d, h) + 1, :])
    beta = stack(lambda d, refs, g, h: gbs[d][rows(g), 2 * nh + col(d, h):2 * nh + col(d, h) + 1])
    glast = jnp.concatenate([gcol[:nb, c_ - 1:c_, :], gcol[nb:, 0:1, :]], axis=0)
    decay = jnp.where(incl, jnp.exp(jnp.where(incl, gcol - grow, 0.0)), 0.0)
    eg = jnp.exp(gcol)
    kb = k * beta
    a = jnp.where(strict, _bmm(kb, k, _BNT) * decay, 0.0)
    qk = jnp.where(incl, _bmm(q, k, _BNT) * decay, 0.0)
    ad = jnp.where(same_sub, a, 0.0)
    ao = a - ad
    p = eye - ad
    n2 = _bmm(ad, ad)
    p = p + _bmm(p, n2)
    n4 = _bmm(n2, n2)
    p = p + _bmm(p, n4)
    n8 = _bmm(n4, n4)
    dinv = p + _bmm(p, n8)
    m = _bmm(dinv, ao)
    m2 = _bmm(m, m)
    y = _bmm(dinv, jnp.concatenate([v * beta, kb * eg], axis=-1))
    z = y + _bmm(m2, y)
    x = z - _bmm(m, z)
    u = x[:, :, :DN_HEAD_DIM]
    w = x[:, :, DN_HEAD_DIM:]
    qg = q * eg
    kd = k * jnp.exp(glast - gcol)
    gl = jnp.exp(glast)
    s = s_ref[...]
    for t in range(nchunks):
        gf, gr = t, nchunks - 1 - t
        step = lambda a: jnp.concatenate([a[gf * nh:(gf + 1) * nh], a[nb + gr * nh:nb + (gr + 1) * nh]], axis=0)
        v_new = step(u) - _bmm(step(w), s)
        o = _bmm(step(qg), s) + _bmm(step(qk), v_new)
        s = s * step(gl) + _bmm(step(kd), v_new, _BTN)
        for h in range(nh):
            of_ref[rows(gf), lanes(h)] = o[h]
            ob_ref[rows(gr), lanes(h)] = o[nh + h]
    s_ref[...] = s


def _cumsum_rows(tri_bf16, g):
    g2, g1, g0 = _split3(g)
    d = functools.partial(lax.dot_general, dimension_numbers=_NN, preferred_element_type=F32)
    return d(tri_bf16, g2) + (d(tri_bf16, g1) + d(tri_bf16, g0))


def _dn_kernel(qf, kf, vf, gf, qb, kb, vb, gbb, *rest, n_cast):
    cast_in = rest[:n_cast]
    of_ref, ob_ref = rest[n_cast:n_cast + 2]
    cast_out = rest[n_cast + 2:2 * n_cast + 2]
    s_ref = rest[-1]

    @pl.when(pl.program_id(0) == 0)
    def _():
        s_ref[...] = jnp.zeros(s_ref.shape, F32)

    _dn_block((qf, kf, vf, gf), (qb, kb, vb, gbb), of_ref, ob_ref, s_ref)
    for src, dst in zip(cast_in, cast_out):
        dst[...] = src[...].astype(BF16)


def _delta_net(q, k, v, gb, ctx_len, to_bf16=()):
    t = q.shape[0]
    rows = DN_STEP_CHUNKS * DN_CHUNK
    assert ctx_len == rows and t % rows == 0
    nstep = t // rows
    cast_specs = []
    for m in to_bf16:
        rb = -(-m.shape[0] // nstep)
        rb = -(-rb // 16) * 16
        nb = -(-m.shape[0] // rb)
        cast_specs.append(pl.BlockSpec((rb, m.shape[1]), lambda s, nb=nb: (jnp.minimum(s, nb - 1), 0)))

    def fwd(s):
        return (s, 0)

    def bwd(s):
        return (jnp.where(s == 0, 0, nstep - s), 0)

    wide = lambda im: pl.BlockSpec((rows, DN_DIM), im)
    narrow = lambda im: pl.BlockSpec((rows, LANE), im)
    return pl.pallas_call(
        functools.partial(_dn_kernel, n_cast=len(to_bf16)),
        grid=(nstep,),
        in_specs=[wide(fwd), wide(fwd), wide(fwd), narrow(fwd), wide(bwd), wide(bwd), wide(bwd), narrow(bwd)]
        + cast_specs,
        out_specs=[wide(fwd), wide(bwd)] + cast_specs,
        out_shape=[jax.ShapeDtypeStruct((t, DN_DIM), F32)] * 2
        + [jax.ShapeDtypeStruct(m.shape, BF16) for m in to_bf16],
        scratch_shapes=[pltpu.VMEM((2 * DN_HEADS, DN_HEAD_DIM, DN_HEAD_DIM), F32)],
        compiler_params=_params(("arbitrary",), VMEM_LIMIT),
        name="delta_net",
    )(q, k, v, gb, q, k, v, gb, *to_bf16)


def _rope(x, cos, sin):
    w = x.shape[1]
    lane = lax.broadcasted_iota(jnp.int32, x.shape, 1)
    first_half = (lane & (AXIS_DIM - 1)) < (AXIS_DIM // 2)
    swapped = jnp.where(first_half, pltpu.roll(x, w - AXIS_DIM // 2, 1), pltpu.roll(x, AXIS_DIM // 2, 1))
    return x * cos + swapped * sin


LOG2E = 1.4426950408889634


def _softmax_av(s, sink, vals):
    m = jnp.maximum(jnp.max(s, axis=-1, keepdims=True), sink)
    p = jnp.exp2(s - m)
    denom = jnp.sum(p, axis=-1, keepdims=True) + jnp.exp2(sink - m)
    return _mm(p, vals) / denom


def _attend(q, keys, vals, band, sink_all, o_ref, row0=0):
    b = q.shape[0]
    for kvh in range(ATT_KV_HEADS):
        kl = slice(kvh * HEAD_DIM, (kvh + 1) * HEAD_DIM)
        heads = range(kvh * ATT_GROUP, (kvh + 1) * ATT_GROUP)
        qs = jnp.concatenate([q[:, h * HEAD_DIM:(h + 1) * HEAD_DIM] for h in heads], axis=0)
        sink = jnp.concatenate([jnp.broadcast_to(sink_all[0:1, h:h + 1], (b, 1)) for h in heads], axis=0)
        s = _mm(qs, keys[:, kl], _NT)
        if band is not None:
            kb = ATT_BLOCK
            s = jnp.concatenate([jnp.where(band[0], s[:, 0:kb], NEG), s[:, kb:2 * kb],
                                 jnp.where(band[1], s[:, 2 * kb:3 * kb], NEG), s[:, 3 * kb:]], axis=1)
        o = _softmax_av(s, sink * LOG2E, vals[:, kl])
        for g, h in enumerate(heads):
            o_ref[row0:row0 + b, h * HEAD_DIM:(h + 1) * HEAD_DIM] = o[g * b:(g + 1) * b, :]


def _band_valid(first, last):
    b = ATT_BLOCK
    c = lax.broadcasted_iota(jnp.int32, (1, b), 1)
    r = lax.broadcasted_iota(jnp.int32, (ATT_GROUP * b, 1), 0) & (b - 1)
    prev_ok = jnp.where(first, -1, c) >= r
    next_ok = jnp.where(last, b, c) <= r
    return prev_ok, next_ok


def _rope_block(rowtab_ref, coltab_ref, blk):
    rt = rowtab_ref[blk]
    ct = coltab_ref[...]
    reps = ATT_BLOCK // GRID_W
    rows = jnp.concatenate([jnp.broadcast_to(rt[g:g + 1, :], (GRID_W, LANE)) for g in range(reps)], axis=0)
    return rows + jnp.concatenate([ct] * reps, axis=0)


ATT_STEP_BLOCKS = 4


def _attn_ctx_kernel(q_ref, kctx_ref, sink_ref, o_ref):
    kvx = kctx_ref[...]
    scale = HEAD_DIM ** -0.5 * LOG2E
    _attend(q_ref[...] * scale, kvx[:, :ATT_KV_DIM], kvx[:, ATT_KV_DIM:], None, sink_ref[...], o_ref)


def _attn_lat_kernel(*refs, nb):
    ns = ATT_STEP_BLOCKS
    q_refs, kv_refs = refs[:ns], refs[ns:2 * ns + 2]
    kctx_ref, cosr_ref, sinr_ref, cosc_ref, sinc_ref, sink_ref, o_ref = refs[2 * ns + 2:]
    b = ATT_BLOCK
    scale = HEAD_DIM ** -0.5 * LOG2E
    b0 = ns * pl.program_id(0)
    blocks = [jnp.clip(b0 - 1 + t, 0, nb - 1) for t in range(ns + 2)]
    cos = [_rope_block(cosr_ref, cosc_ref, blk) for blk in blocks]
    sin = [_rope_block(sinr_ref, sinc_ref, blk) for blk in blocks]
    kv = [r[...] for r in kv_refs]
    kvx = kctx_ref[...]
    keys = [_rope(t[:, :ATT_KV_DIM], c_, s_) for t, c_, s_ in zip(kv, cos, sin)]
    sink_all = sink_ref[...]
    for sub in range(ns):
        q = q_refs[sub][...]
        q = jnp.concatenate([_rope(q[:, l * LANE:(l + 1) * LANE], cos[1 + sub], sin[1 + sub])
                             for l in range(ATT_DIM // LANE)], axis=1)
        kcat = jnp.concatenate(keys[sub:sub + 3] + [kvx[:, :ATT_KV_DIM]], axis=0)
        vcat = jnp.concatenate([t[:, ATT_KV_DIM:] for t in kv[sub:sub + 3]] + [kvx[:, ATT_KV_DIM:]], axis=0)
        band = _band_valid(first=(b0 + sub == 0), last=(b0 + sub == nb - 1))
        _attend(q * scale, kcat, vcat, band, sink_all, o_ref, row0=sub * b)


def _attention(pq, pkv, rope, sink, ctx_len, with_ctx):
    t = pq.shape[0]
    n = t - ctx_len
    nb = n // ATT_BLOCK
    ns = ATT_STEP_BLOCKS
    assert nb % ns == 0 and ctx_len % ATT_BLOCK == 0
    off = ctx_len // ATT_BLOCK
    whole = lambda a: pl.BlockSpec(a.shape, lambda j: (0,) * a.ndim)
    ctx_kv = pl.BlockSpec((ctx_len, 2 * ATT_KV_DIM), lambda j: (0, 0))
    sink_spec = pl.BlockSpec((8, LANE), lambda j: (0, 0))
    yc_ctx = None
    if with_ctx:
        yc_ctx = pl.pallas_call(
            _attn_ctx_kernel,
            grid=(1,),
            in_specs=[pl.BlockSpec((ctx_len, ATT_DIM), lambda j: (0, 0)), ctx_kv, sink_spec],
            out_specs=pl.BlockSpec((ctx_len, ATT_DIM), lambda j: (0, 0)),
            out_shape=jax.ShapeDtypeStruct((ctx_len, ATT_DIM), F32),
            compiler_params=_params(("arbitrary",), VMEM_LIMIT),
            name="context_attention",
        )(pq, pkv, sink)
    lat = lambda k: jnp.clip(k, 0, nb - 1) + off
    q_specs = [pl.BlockSpec((ATT_BLOCK, ATT_DIM), lambda j, s=s: (lat(ns * j + s), 0)) for s in range(ns)]
    kv_specs = [pl.BlockSpec((ATT_BLOCK, 2 * ATT_KV_DIM), lambda j, s=s: (lat(ns * j - 1 + s), 0))
                for s in range(ns + 2)]
    yc_lat = pl.pallas_call(
        functools.partial(_attn_lat_kernel, nb=nb),
        grid=(nb // ns,),
        in_specs=q_specs + kv_specs + [ctx_kv] + [whole(a) for a in rope] + [sink_spec],
        out_specs=pl.BlockSpec((ns * ATT_BLOCK, ATT_DIM), lambda j: (j, 0)),
        out_shape=jax.ShapeDtypeStruct((n, ATT_DIM), F32),
        compiler_params=_params(("arbitrary",), VMEM_LIMIT),
        name="attention",
    )(*([pq] * ns + [pkv] * (ns + 3) + list(rope) + [sink]))
    return yc_ctx, yc_lat


def _mixfin_kernel(ctx_ref, h_ref, mod_ref, pconv_ref, prev_ref, next_ref, cw_ref, of_ref, ob_ref, z_ref, ng_ref,
                   ycc_ref, ycl_ref, wout_ref, bd_ref, *rest, nblk, blk0, with_router):
    if with_router:
        wr1_ref, wr0_ref, x_ref, hx_ref, lg_ref = rest
    else:
        wg_ref, wu_ref, wd_ref, x_ref = rest
    i = pl.program_id(0) + blk0
    row = _mod_row(mod_ref, i == 0)
    pc = pconv_ref[...]
    u = pc[:, CONV_CH:2 * CONV_CH] * pc[:, 2 * CONV_CH:]
    prow, nrow = _halo_rows(prev_ref, next_ref, i, nblk)
    prow = prow[:, CONV_CH:2 * CONV_CH] * prow[:, 2 * CONV_CH:]
    nrow = nrow[:, CONV_CH:2 * CONV_CH] * nrow[:, 2 * CONV_CH:]
    up, un = _shift_rows(u, prow, nrow)
    cw = cw_ref[...]
    ya = pc[:, :CONV_CH] * (up * cw[0:1, :] + u * cw[1:2, :] + un * cw[2:3, :])
    o = of_ref[...] + ob_ref[...]
    ms = _group_sum(o * o, bd_ref[...]) * (1.0 / DN_HEAD_DIM)
    z = z_ref[...]
    yb = o * lax.rsqrt(ms + EPS) * ng_ref[...] * (z * _sigmoid(z))
    mix = jnp.concatenate([ya, yb, jnp.where(i == 0, ycc_ref[...], ycl_ref[...])], axis=1)
    x = jnp.where(i == 0, ctx_ref[...], h_ref[...]) + row[:, 2 * D:3 * D] * _mm(mix, wout_ref[...])
    hx = _rmsnorm_rows(x) * (1.0 + row[:, 4 * D:5 * D]) + row[:, 3 * D:4 * D]
    if not with_router:
        hb = hx.astype(BF16)
        acc = jnp.zeros((hb.shape[0], D), F32)
        for f in range(0, D_FF, FF_CHUNK):
            fe = min(f + FF_CHUNK, D_FF)
            g = _mm(hb, wg_ref[:, f:fe])
            u_ = _mm(hb, wu_ref[:, f:fe])
            acc = acc + _mm(g * _sigmoid(g) * u_, wd_ref[f:fe, :])
        x_ref[...] = x + row[:, 5 * D:6 * D] * acc
    else:
        x_ref[...] = x
        hx_ref[...] = hx
        h1, h0 = _split2(hx)
        d = functools.partial(lax.dot_general, dimension_numbers=_NN, preferred_element_type=F32)
        lg = d(h1, wr1_ref[...]) + (d(h0, wr1_ref[...]) + d(h1, wr0_ref[...]))
        lane = lax.broadcasted_iota(jnp.int32, lg.shape, 1)
        lanef = lane.astype(F32)
        lg = jnp.where(lane < N_EXPERTS, lg, -jnp.inf)
        m1 = jnp.max(lg, axis=-1, keepdims=True)
        i1 = jnp.min(jnp.where(lg == m1, lanef, float(LANE)), axis=-1, keepdims=True)
        rest = jnp.where(lanef == i1, -jnp.inf, lg)
        m2 = jnp.max(rest, axis=-1, keepdims=True)
        i2 = jnp.min(jnp.where(rest == m2, lanef, float(LANE)), axis=-1, keepdims=True)
        e2 = jnp.exp(m2 - m1)
        g1 = 1.0 / (1.0 + e2)
        lg_ref[...] = jnp.where(lane == 0, g1, jnp.where(lane == 1, e2 * g1, jnp.where(lane == 2, i1, i2)))


def _mixer_finish(ctx_src, lat_src, lat_blk0, mod, pconv, conv_w, o_f, o_b, pz, norm_g, yc, w_out, bd, ctx_len,
                  with_ctx, router=None, ffn=None):
    assert (router is None) != (ffn is None)
    t = pconv.shape[0]
    nblk = t // TM
    blk0 = 0 if with_ctx else ctx_len // TM
    rows = t - blk0 * TM
    r8 = TM // 8
    w = pconv.shape[1]
    cur = lambda i: (i + blk0, 0)
    out_cur = lambda i: (i, 0)
    const = lambda i: (0, 0)
    ng = jnp.tile(norm_g.reshape(1, DN_HEAD_DIM), (1, DN_HEADS))
    in_specs = [
        pl.BlockSpec((TM, D), const),
        pl.BlockSpec((TM, D), lambda i: (jnp.maximum(i + blk0 - 1, 0) + lat_blk0, 0)),
        pl.BlockSpec((8, N_MOD * D), const),
        pl.BlockSpec((TM, w), cur),
        pl.BlockSpec((8, w), lambda i: (jnp.maximum((i + blk0) * r8 - 1, 0), 0)),
        pl.BlockSpec((8, w), lambda i: (jnp.minimum((i + blk0 + 1) * r8, t // 8 - 1), 0)),
        pl.BlockSpec((3, CONV_CH), const),
        pl.BlockSpec((TM, DN_DIM), cur),
        pl.BlockSpec((TM, DN_DIM), cur),
        pl.BlockSpec((TM, DN_DIM), cur),
        pl.BlockSpec((1, DN_DIM), const),
        pl.BlockSpec((TM, ATT_DIM), const),
        pl.BlockSpec((TM, ATT_DIM), lambda i: (jnp.maximum(i + blk0 - 1, 0), 0)),
        pl.BlockSpec((MIX_DIM, D), const),
        pl.BlockSpec(bd.shape, const),
    ]
    yc_ctx, yc_lat = yc
    if yc_ctx is None:
        assert not with_ctx
        yc_ctx = yc_lat
    args = [ctx_src, lat_src, mod, pconv, pconv, pconv, conv_w, o_f, o_b, pz, ng, yc_ctx, yc_lat, w_out, bd]
    out_specs = [pl.BlockSpec((TM, D), out_cur)]
    out_shape = [jax.ShapeDtypeStruct((rows, D), F32)]
    if router is not None:
        in_specs += [pl.BlockSpec((D, LANE), const)] * 2
        args += list(router)
        out_specs += [pl.BlockSpec((TM, D), out_cur), pl.BlockSpec((TM, LANE), out_cur)]
        out_shape += [jax.ShapeDtypeStruct((rows, D), F32), jax.ShapeDtypeStruct((rows, LANE), F32)]
    else:
        once = dict(pipeline_mode=pl.Buffered(1))
        in_specs += [pl.BlockSpec(w_.shape, const, **once) for w_ in ffn]
        args += list(ffn)
    return pl.pallas_call(
        functools.partial(_mixfin_kernel, nblk=nblk, blk0=blk0, with_router=router is not None),
        grid=(rows // TM,),
        in_specs=in_specs,
        out_specs=out_specs,
        out_shape=out_shape,
        compiler_params=_params(("arbitrary",), VMEM_LIMIT),
        name="mixer_finish",
    )(*args)


def _moe_kernel(be_ref, nu_ref, xs_ref, wg_ref, wu_ref, wd_ref, y_ref, acc_ref):
    b = pl.program_id(0)
    f = pl.program_id(1)

    @pl.when(b < nu_ref[0])
    def _():
        xs = xs_ref[...].astype(BF16)
        part = jnp.zeros((MOE_TM, D), F32)
        for c in range(0, MOE_TF, MOE_SUB):
            g = _mm(xs, wg_ref[0, :, c:c + MOE_SUB])
            u = _mm(xs, wu_ref[0, :, c:c + MOE_SUB])
            part = part + _mm(g * _sigmoid(g) * u, wd_ref[0, c:c + MOE_SUB, :])

        @pl.when(f == 0)
        def _():
            acc_ref[...] = part

        @pl.when(f != 0)
        def _():
            acc_ref[...] += part

        @pl.when(f == pl.num_programs(1) - 1)
        def _():
            y_ref[...] = acc_ref[...]

    @pl.when(b >= nu_ref[0])
    def _():
        y_ref[...] = jnp.zeros(y_ref.shape, F32)


def _moe_experts(xs, blk_e, n_used, wg, wu, wd):
    cap = xs.shape[0]
    nblk = cap // MOE_TM
    nf = D_FF_EXPERT // MOE_TF

    def fidx(b, f, nu):
        return jnp.where(b < nu[0], f, nf - 1)

    grid_spec = pltpu.PrefetchScalarGridSpec(
        num_scalar_prefetch=2,
        grid=(nblk, nf),
        in_specs=[
            pl.BlockSpec((MOE_TM, D), lambda b, f, be, nu: (b, 0)),
            pl.BlockSpec((1, D, MOE_TF), lambda b, f, be, nu: (be[b], 0, fidx(b, f, nu))),
            pl.BlockSpec((1, D, MOE_TF), lambda b, f, be, nu: (be[b], 0, fidx(b, f, nu))),
            pl.BlockSpec((1, MOE_TF, D), lambda b, f, be, nu: (be[b], fidx(b, f, nu), 0)),
        ],
        out_specs=pl.BlockSpec((MOE_TM, D), lambda b, f, be, nu: (b, 0)),
        scratch_shapes=[pltpu.VMEM((MOE_TM, D), F32)],
    )
    return pl.pallas_call(
        _moe_kernel,
        grid_spec=grid_spec,
        out_shape=jax.ShapeDtypeStruct((cap, D), F32),
        compiler_params=_params(("arbitrary", "arbitrary"), VMEM_LIMIT),
        name="moe_experts",
    )(blk_e, n_used, xs, wg, wu, wd)


def _moe_route(choices):
    n = choices[0].shape[0]
    a = n * TOP_K
    flat_e = jnp.concatenate(choices)
    onehot = (flat_e[:, None] == jnp.arange(N_EXPERTS, dtype=flat_e.dtype)[None, :]).astype(jnp.int32)
    counts = jnp.sum(onehot, axis=0)
    padded = (counts + MOE_TM - 1) // MOE_TM * MOE_TM
    pad_ends = jnp.cumsum(padded)
    pad_starts = pad_ends - padded
    dest = jnp.sum(onehot * (jnp.cumsum(onehot, axis=0) - onehot + pad_starts[None, :]), axis=1)
    cap = a + N_EXPERTS * MOE_TM
    nblk = cap // MOE_TM
    row_tok = (jnp.arange(cap, dtype=jnp.int32) % n).at[dest].set(
        jnp.arange(a, dtype=jnp.int32) % n, unique_indices=True, mode="promise_in_bounds")
    blk_start = jnp.arange(nblk, dtype=jnp.int32) * MOE_TM
    blk_e = jnp.minimum(jnp.sum((pad_ends[None, :] <= blk_start[:, None]).astype(jnp.int32), axis=1),
                        N_EXPERTS - 1)
    n_used = (pad_ends[-1] // MOE_TM).astype(jnp.int32).reshape(1)
    last_e = blk_e[jnp.maximum(n_used[0] - 1, 0)]
    blk_e = jnp.where(jnp.arange(nblk) < n_used[0], blk_e, last_e)
    return dest, row_tok, blk_e, n_used


def _final_kernel(x_ref, y0_ref, y1_ref, gt_ref, mod_ref, fg_ref, o_ref):
    mod = mod_ref[...]
    gt = gt_ref[...]
    f = gt[:, 0:1] * y0_ref[...] + gt[:, 1:2] * y1_ref[...]
    x = x_ref[...] + mod[0:1, 5 * D:6 * D] * f
    o_ref[...] = _rmsnorm_rows(x) * fg_ref[...]


def _moe_combine_final(x, y01, gt, mod, final_g):
    n = x.shape[0]
    tm = COMBINE_TM
    assert n % tm == 0
    row = lambda i: (i, 0)
    second = lambda i: (i + n // tm, 0)
    const = lambda i: (0, 0)
    return pl.pallas_call(
        _final_kernel,
        grid=(n // tm,),
        in_specs=[pl.BlockSpec((tm, D), row), pl.BlockSpec((tm, D), row), pl.BlockSpec((tm, D), second),
                  pl.BlockSpec((tm, LANE), row), pl.BlockSpec((8, N_MOD * D), const), pl.BlockSpec((1, D), const)],
        out_specs=pl.BlockSpec((tm, D), row),
        out_shape=jax.ShapeDtypeStruct((n, D), F32),
        compiler_params=_params(("arbitrary",), VMEM_LIMIT),
        name="moe_combine_final",
    )(x, y01, y01, gt, mod, final_g.reshape(1, D))


def _rope_tables(n):
    lane = jnp.arange(LANE, dtype=jnp.int32) % HEAD_DIM
    inv = ROPE_BASE ** (-jnp.arange(0, AXIS_DIM, 2, dtype=F32) / AXIS_DIM)
    freq = inv[lane % (AXIS_DIM // 2)]
    row_axis = (lane // AXIS_DIM) == 0
    sign = jnp.where((lane % AXIS_DIM) < AXIS_DIM // 2, -1.0, 1.0)
    reps = ATT_BLOCK // GRID_W

    def tables(count, on_axis):
        ang = jnp.arange(count, dtype=F32)[:, None] * freq[None, :]
        return jnp.where(on_axis, jnp.cos(ang), 0.0), jnp.where(on_axis, jnp.sin(ang) * sign, 0.0)

    by_block = lambda t: jnp.pad(t.reshape(-1, reps, LANE), ((0, 0), (0, 8 - reps), (0, 0)))
    cosr, sinr = tables(n // GRID_W, row_axis)
    cosc, sinc = tables(GRID_W, ~row_axis)
    return by_block(cosr), by_block(sinr), cosc, sinc


def _prep_w_in(w):
    pad = jnp.zeros((D, LANE - N_AB), w.dtype)
    return jnp.concatenate([w[:, :_C_Z], w[:, _C_A:_C_Q], pad, w[:, _C_Z:_C_A], w[:, _C_Q:_C_END]],
                           axis=1).astype(BF16)


def kernel(x, c, ctx, c_ctx, w_mod, b_mod, w_in, w_out, conv_w, dn_conv_w, dn_a_log, dn_dt_bias, dn_norm_g,
           attn_sink, ffn_w_gate, ffn_w_up, ffn_w_down, moe_router, moe_w_gate, moe_w_up, moe_w_down,
           final_norm_g):
    bsz, n, d = x.shape
    ctx_len = ctx.shape[1]
    depth = w_in.shape[0]
    assert bsz == 1 and d == D and ctx_len == TM and n % TM == 0 and n % GRID_W == 0
    rope = _rope_tables(n)
    mods = _mod_vectors(c, c_ctx, w_mod, b_mod)
    bd = _head_blockdiag(DN_DIM, DN_HEAD_DIM)
    stream = (ctx[0], x[0], 0)
    for layer in range(depth):
        last = layer == depth - 1
        mod = mods[layer]
        pconv, pz, pq, pkv, qn, kn, vv, gb = _in_proj(*stream, mod, _prep_w_in(w_in[layer]), dn_conv_w[layer],
                                                      dn_a_log[layer], dn_dt_bias[layer], bd)
        to_bf16 = []
        if layer % 2 == 0:
            to_bf16 += [w[layer // 2] for w in (ffn_w_gate, ffn_w_up, ffn_w_down)]
        if (layer + 1) % 2 == 1 and layer + 1 < depth:
            j = (layer + 1) // 2
            to_bf16 += [w[j].reshape(-1, w.shape[-1]) for w in (moe_w_gate, moe_w_up, moe_w_down)]
        o_f, o_b, *cast = _delta_net(qn, kn, vv, gb, ctx_len, tuple(to_bf16))
        if layer % 2 == 0:
            ffn_bf16, cast = cast[:3], cast[3:]
        if cast:
            moe_bf16 = [c.reshape(w.shape[1:]) for c, w in zip(cast, (moe_w_gate, moe_w_up, moe_w_down))]
        sink = jnp.zeros((8, LANE), F32).at[0, :ATT_HEADS].set(attn_sink[layer])
        yc = _attention(pq, pkv, rope, sink, ctx_len, with_ctx=not last)
        router = None
        if layer % 2 == 1:
            wr = jnp.zeros((D, LANE), F32).at[:, :N_EXPERTS].set(moe_router[layer // 2])
            wr1 = wr.astype(BF16)
            router = (wr1, (wr - wr1.astype(F32)).astype(BF16))
        outs = _mixer_finish(*stream, mod, pconv, conv_w[layer], o_f, o_b, pz, dn_norm_g[layer], yc,
                             w_out[layer].astype(BF16), bd, ctx_len, with_ctx=not last, router=router,
                             ffn=ffn_bf16 if layer % 2 == 0 else None)
        if layer % 2 == 0:
            assert not last
            h, = outs
            stream = (h, h, ctx_len // TM)
        else:
            assert last
            x1, hx, route = outs
            dest, row_tok, blk_e, n_used = _moe_route([route[:, 2 + s].astype(jnp.int32) for s in range(TOP_K)])
            take = lambda rows_, idx: rows_.at[idx].get(mode="promise_in_bounds")
            y = _moe_experts(take(hx, row_tok), blk_e, n_used, *moe_bf16)
            h = _moe_combine_final(x1, take(y, dest), route, mod, final_norm_g)
    return h.reshape(bsz, n, d)
```

```python
import functools

import jax
import jax.numpy as jnp
from jax import lax
from jax.experimental import pallas as pl
from jax.experimental.pallas import tpu as pltpu

F32 = jnp.float32
BF16 = jnp.bfloat16

D = 1024
N_MOD = 6
EPS = 1e-6
NEG = -1e30
GRID_W = 64

CONV_CH = 256
DN_HEADS = 6
DN_HEAD_DIM = 64
DN_DIM = DN_HEADS * DN_HEAD_DIM
DN_CHUNK = 64
DN_SUB = 16
DN_STEP_CHUNKS = 4
ATT_HEADS = 6
ATT_KV_HEADS = 2
ATT_GROUP = ATT_HEADS // ATT_KV_HEADS
HEAD_DIM = 64
ATT_DIM = ATT_HEADS * HEAD_DIM
ATT_KV_DIM = ATT_KV_HEADS * HEAD_DIM
ATT_BLOCK = 128
ROPE_BASE = 10000.0
AXIS_DIM = HEAD_DIM // 2
MIX_DIM = CONV_CH + DN_DIM + ATT_DIM

D_FF = 2816
N_EXPERTS = 8
TOP_K = 2
D_FF_EXPERT = 3584

TM = 256
FF_CHUNK = 1024
MOE_TM = 512
MOE_TF = 1792
MOE_SUB = 256
COMBINE_TM = 1024
LANE = 128
VMEM_LIMIT = 56 * 1024 * 1024

_C_QKV = 3 * CONV_CH
_C_Z = _C_QKV + 3 * DN_DIM
_C_A = _C_Z + DN_DIM
_C_Q = _C_A + 4 * DN_HEADS
_C_K = _C_Q + ATT_DIM
_C_V = _C_K + ATT_KV_DIM
_C_END = _C_V + ATT_KV_DIM
N_AB = 4 * DN_HEADS


def _params(sem=None, vmem=None):
    kw = {}
    if sem is not None:
        kw["dimension_semantics"] = sem
    if vmem is not None:
        kw["vmem_limit_bytes"] = vmem
    return pltpu.CompilerParams(**kw)


def _split2(a):
    hi = a.astype(BF16)
    lo = (a - hi.astype(F32)).astype(BF16)
    return hi, lo


def _split3(a):
    hi = a.astype(BF16)
    r = a - hi.astype(F32)
    mid = r.astype(BF16)
    lo = (r - mid.astype(F32)).astype(BF16)
    return hi, mid, lo


_NN = (((1,), (0,)), ((), ()))
_NT = (((1,), (1,)), ((), ()))


def _mm(a, b, dims=_NN):
    return lax.dot_general(a.astype(BF16), b.astype(BF16), dims, preferred_element_type=F32)


def _mm3(a, b, dims=_NN):
    a1, a0 = _split2(a)
    b1, b0 = _split2(b)
    d = functools.partial(lax.dot_general, dimension_numbers=dims, preferred_element_type=F32)
    return d(a1, b1) + (d(a1, b0) + d(a0, b1))


_BNN = (((2,), (1,)), ((0,), (0,)))
_BNT = (((2,), (2,)), ((0,), (0,)))
_BTN = (((1,), (1,)), ((0,), (0,)))


def _bmm(a, b, dims=_BNN):
    return lax.dot_general(a.astype(BF16), b.astype(BF16), dims, preferred_element_type=F32)


def _sigmoid(x):
    return 1.0 / (1.0 + jnp.exp(-x))


def _softplus(x):
    return jnp.maximum(x, 0.0) + jnp.log1p(jnp.exp(-jnp.abs(x)))


def _mod_row(mod_ref, is_ctx):
    mod = mod_ref[...]
    return jnp.where(is_ctx, mod[1:2, :], mod[0:1, :])


def _rmsnorm_rows(x):
    return x * lax.rsqrt(jnp.mean(x * x, axis=-1, keepdims=True) + EPS)


def _shift_rows(u, prow, nrow):
    n = u.shape[0]
    rid = lax.broadcasted_iota(jnp.int32, u.shape, 0)
    up = jnp.where(rid == 0, prow, pltpu.roll(u, 1, 0))
    un = jnp.where(rid == n - 1, nrow, pltpu.roll(u, n - 1, 0))
    return up, un


def _same_group(shape, group):
    sh = group.bit_length() - 1
    assert 1 << sh == group
    return (lax.broadcasted_iota(jnp.int32, shape, 0) >> sh) == (lax.broadcasted_iota(jnp.int32, shape, 1) >> sh)


def _head_blockdiag(n, group):
    g = jnp.arange(n, dtype=jnp.int32) // group
    return (g[:, None] == g[None, :]).astype(BF16)


def _group_sum(t, bd):
    return _mm(t, bd)


MOD_TN = 1536


def _mod_kernel(s_ref, w_ref, b_ref, o_ref):
    s = s_ref[...]
    s = s * _sigmoid(s)
    o_ref[0] = _mm3(s, w_ref[0]) + b_ref[0]


def _mod_vectors(c, c_ctx, w_mod, b_mod):
    depth = w_mod.shape[0]
    s = jnp.zeros((8, D), F32).at[0].set(c[0]).at[1].set(c_ctx)
    return pl.pallas_call(
        _mod_kernel,
        grid=(depth, N_MOD * D // MOD_TN),
        in_specs=[
            pl.BlockSpec((8, D), lambda l, j: (0, 0)),
            pl.BlockSpec((1, D, MOD_TN), lambda l, j: (l, 0, j)),
            pl.BlockSpec((1, 1, MOD_TN), lambda l, j: (l, 0, j)),
        ],
        out_specs=pl.BlockSpec((1, 8, MOD_TN), lambda l, j: (l, 0, j)),
        out_shape=jax.ShapeDtypeStruct((depth, 8, N_MOD * D), F32),
        compiler_params=_params(("arbitrary", "arbitrary"), VMEM_LIMIT),
        name="mod_vectors",
    )(s, w_mod, b_mod.reshape(depth, 1, N_MOD * D))


def _halo_valid(i, nblk):
    return jnp.logical_and(i != 0, i != 1), jnp.logical_and(i != 0, i != nblk - 1)


def _halo_rows(prev_ref, next_ref, i, nblk):
    pvalid, nvalid = _halo_valid(i, nblk)
    prow = jnp.where(pvalid, prev_ref[7:8, :], 0.0)
    nrow = jnp.where(nvalid, next_ref[0:1, :], 0.0)
    return prow, nrow


def _in_kernel(ctx_ref, h_ref, hprev_ref, hnext_ref, mod_ref, w_ref, cw_ref, alog_ref, dtb_ref, bd_ref,
               pconv_ref, pz_ref, pq_ref, pkv_ref, q_ref, k_ref, v_ref, gb_ref, *, nblk):
    i = pl.program_id(0)
    row = _mod_row(mod_ref, i == 0)
    norm_mod = lambda x: _rmsnorm_rows(x) * (1.0 + row[:, D:2 * D]) + row[:, 0:D]
    h1 = norm_mod(jnp.where(i == 0, ctx_ref[...], h_ref[...])).astype(BF16)
    halo = norm_mod(jnp.concatenate([hprev_ref[...], hnext_ref[...]], axis=0)).astype(BF16)
    d = functools.partial(lax.dot_general, dimension_numbers=_NN, preferred_element_type=F32)
    tm = h1.shape[0]
    c0 = 3 * CONV_CH
    c1 = c0 + 3 * DN_DIM
    c2 = c1 + LANE
    c3 = c2 + DN_DIM + ATT_DIM
    pconv_ref[...] = d(h1, w_ref[:, 0:c0])
    zq = d(h1, w_ref[:, c2:c3])
    pz_ref[...] = zq[:, 0:DN_DIM]
    pq_ref[...] = zq[:, DN_DIM:]
    pkv_ref[...] = d(h1, w_ref[:, c3:c3 + 2 * ATT_KV_DIM])
    qkv_ab = d(jnp.concatenate([h1, halo], axis=0), w_ref[:, c0:c2])
    qkv = qkv_ab[:, 0:3 * DN_DIM]
    ab = qkv_ab[0:tm, 3 * DN_DIM:]

    u = qkv[0:tm, :]
    pvalid, nvalid = _halo_valid(i, nblk)
    prow = jnp.where(pvalid, qkv[tm + 7:tm + 8, :], 0.0)
    nrow = jnp.where(nvalid, qkv[tm + 8:tm + 9, :], 0.0)
    up, un = _shift_rows(u, prow, nrow)
    cw = cw_ref[...]
    y = up * cw[0:1, :] + u * cw[1:2, :] + un * cw[2:3, :]
    y = y * _sigmoid(y)
    q = y[:, 0:DN_DIM]
    k = y[:, DN_DIM:2 * DN_DIM]
    bd = bd_ref[...]
    q_ref[...] = q * lax.rsqrt(_group_sum(q * q, bd) + 1e-6) * (DN_HEAD_DIM ** -0.5)
    k_ref[...] = k * lax.rsqrt(_group_sum(k * k, bd) + 1e-6)
    v_ref[...] = y[:, 2 * DN_DIM:3 * DN_DIM]
    g = -jnp.exp(alog_ref[...]) * _softplus(ab + dtb_ref[...])
    lane = lax.broadcasted_iota(jnp.int32, ab.shape, 1)
    gb_ref[...] = jnp.where(lane < 2 * DN_HEADS, g, _sigmoid(ab))


def _in_proj(ctx_src, lat_src, lat_blk0, mod, w_main, dn_conv_w, a_log, dt_bias, bd):
    r8 = TM // 8
    last8 = lat_src.shape[0] // 8 - 1
    nblk = 1 + lat_src.shape[0] // TM - lat_blk0
    t = nblk * TM
    alog = jnp.zeros((1, LANE), F32).at[0, :2 * DN_HEADS].set(a_log.reshape(-1))
    dtb = jnp.zeros((1, LANE), F32).at[0, :2 * DN_HEADS].set(dt_bias.reshape(-1))
    widths = (3 * CONV_CH, DN_DIM, ATT_DIM, 2 * ATT_KV_DIM, DN_DIM, DN_DIM, DN_DIM, LANE)
    const = lambda i: (0, 0)
    lat = lambda i: jnp.maximum(i - 1, 0) + lat_blk0
    return pl.pallas_call(
        functools.partial(_in_kernel, nblk=nblk),
        grid=(nblk,),
        in_specs=[
            pl.BlockSpec((TM, D), const),
            pl.BlockSpec((TM, D), lambda i: (lat(i), 0)),
            pl.BlockSpec((8, D), lambda i: (jnp.maximum(lat(i) * r8 - 1, 0), 0)),
            pl.BlockSpec((8, D), lambda i: (jnp.minimum((lat(i) + 1) * r8, last8), 0)),
            pl.BlockSpec((8, N_MOD * D), const),
            pl.BlockSpec(w_main.shape, const),
            pl.BlockSpec((3, 3 * DN_DIM), const),
            pl.BlockSpec((1, LANE), const),
            pl.BlockSpec((1, LANE), const),
            pl.BlockSpec(bd.shape, const),
        ],
        out_specs=[pl.BlockSpec((TM, w), lambda i: (i, 0)) for w in widths],
        out_shape=[jax.ShapeDtypeStruct((t, w), F32) for w in widths],
        compiler_params=_params(("arbitrary",), VMEM_LIMIT),
        name="in_proj",
    )(ctx_src, lat_src, lat_src, lat_src, mod, w_main, dn_conv_w, alog, dtb, bd)


def _dn_block(fwd_refs, bwd_refs, of_ref, ob_ref, s_ref):
    c_ = DN_CHUNK
    nh = DN_HEADS
    nchunks = fwd_refs[0].shape[0] // c_
    nb = nchunks * nh
    rows = lambda g: slice(g * c_, (g + 1) * c_)
    lanes = lambda h: slice(h * DN_HEAD_DIM, (h + 1) * DN_HEAD_DIM)
    ri = lax.broadcasted_iota(jnp.int32, (c_, c_), 0)
    ci = lax.broadcasted_iota(jnp.int32, (c_, c_), 1)
    same_sub = _same_group((c_, c_), DN_SUB)
    eye = jnp.where(ri == ci, 1.0, 0.0)
    shape3 = (2 * nb, c_, c_)
    delta = lax.broadcasted_iota(jnp.int32, shape3, 1) - lax.broadcasted_iota(jnp.int32, shape3, 2)
    delta = jnp.where(lax.broadcasted_iota(jnp.int32, shape3, 0) >= nb, -delta, delta)
    incl = delta >= 0
    strict = delta > 0

    def stack(fn):
        return jnp.stack([fn(d, refs, g, h) for d, refs in enumerate((fwd_refs, bwd_refs))
                          for g in range(nchunks) for h in range(nh)])

    gbs = [refs[3][...] for refs in (fwd_refs, bwd_refs)]
    tris = [jnp.where(ri >= ci, 1.0, 0.0).astype(BF16), jnp.where(ri <= ci, 1.0, 0.0).astype(BF16)]
    gcs = [[_cumsum_rows(tris[d], gbs[d][rows(g), :]) for g in range(nchunks)] for d in range(2)]
    gcts = [[gc.T for gc in gcs[d]] for d in range(2)]
    q = stack(lambda d, refs, g, h: refs[0][rows(g), lanes(h)])
    k = stack(lambda d, refs, g, h: refs[1][rows(g), lanes(h)])
    v = stack(lambda d, refs, g, h: refs[2][rows(g), lanes(h)])
    col = lambda d, h: d * nh + h
    gcol = stack(lambda d, refs, g, h: gcs[d][g][:, col(d, h):col(d, h) + 1])
    grow = stack(lambda d, refs, g, h: gcts[d][g][col(d, h):col(d, h) + 1, :])
    beta = stack(lambda d, refs, g, h: gbs[d][rows(g), 2 * nh + col(d, h):2 * nh + col(d, h) + 1])
    glast = jnp.concatenate([gcol[:nb, c_ - 1:c_, :], gcol[nb:, 0:1, :]], axis=0)
    decay = jnp.where(incl, jnp.exp(jnp.where(incl, gcol - grow, 0.0)), 0.0)
    eg = jnp.exp(gcol)
    kb = k * beta
    a = jnp.where(strict, _bmm(kb, k, _BNT) * decay, 0.0)
    qk = jnp.where(incl, _bmm(q, k, _BNT) * decay, 0.0)
    ad = jnp.where(same_sub, a, 0.0)
    ao = a - ad
    p = eye - ad
    n2 = _bmm(ad, ad)
    p = p + _bmm(p, n2)
    n4 = _bmm(n2, n2)
    p = p + _bmm(p, n4)
    n8 = _bmm(n4, n4)
    dinv = p + _bmm(p, n8)
    m = _bmm(dinv, ao)
    m2 = _bmm(m, m)
    y = _bmm(dinv, jnp.concatenate([v * beta, kb * eg], axis=-1))
    z = y + _bmm(m2, y)
    x = z - _bmm(m, z)
    u = x[:, :, :DN_HEAD_DIM]
    w = x[:, :, DN_HEAD_DIM:]
    qg = q * eg
    kd = k * jnp.exp(glast - gcol)
    gl = jnp.exp(glast)
    s = s_ref[...]
    for t in range(nchunks):
        gf, gr = t, nchunks - 1 - t
        step = lambda a: jnp.concatenate([a[gf * nh:(gf + 1) * nh], a[nb + gr * nh:nb + (gr + 1) * nh]], axis=0)
        v_new = step(u) - _bmm(step(w), s)
        o = _bmm(step(qg), s) + _bmm(step(qk), v_new)
        s = s * step(gl) + _bmm(step(kd), v_new, _BTN)
        for h in range(nh):
            of_ref[rows(gf), lanes(h)] = o[h]
            ob_ref[rows(gr), lanes(h)] = o[nh + h]
    s_ref[...] = s


def _cumsum_rows(tri_bf16, g):
    g2, g1, g0 = _split3(g)
    d = functools.partial(lax.dot_general, dimension_numbers=_NN, preferred_element_type=F32)
    return d(tri_bf16, g2) + (d(tri_bf16, g1) + d(tri_bf16, g0))


def _dn_kernel(qf, kf, vf, gf, qb, kb, vb, gbb, *rest, n_cast):
    cast_in = rest[:n_cast]
    of_ref, ob_ref = rest[n_cast:n_cast + 2]
    cast_out = rest[n_cast + 2:2 * n_cast + 2]
    s_ref = rest[-1]

    @pl.when(pl.program_id(0) == 0)
    def _():
        s_ref[...] = jnp.zeros(s_ref.shape, F32)

    _dn_block((qf, kf, vf, gf), (qb, kb, vb, gbb), of_ref, ob_ref, s_ref)
    for src, dst in zip(cast_in, cast_out):
        dst[...] = src[...].astype(BF16)


def _delta_net(q, k, v, gb, ctx_len, to_bf16=()):
    t = q.shape[0]
    rows = DN_STEP_CHUNKS * DN_CHUNK
    assert ctx_len == rows and t % rows == 0
    nstep = t // rows
    cast_specs = []
    for m in to_bf16:
        rb = -(-m.shape[0] // nstep)
        rb = -(-rb // 16) * 16
        nb = -(-m.shape[0] // rb)
        cast_specs.append(pl.BlockSpec((rb, m.shape[1]), lambda s, nb=nb: (jnp.minimum(s, nb - 1), 0)))

    def fwd(s):
        return (s, 0)

    def bwd(s):
        return (jnp.where(s == 0, 0, nstep - s), 0)

    wide = lambda im: pl.BlockSpec((rows, DN_DIM), im)
    narrow = lambda im: pl.BlockSpec((rows, LANE), im)
    return pl.pallas_call(
        functools.partial(_dn_kernel, n_cast=len(to_bf16)),
        grid=(nstep,),
        in_specs=[wide(fwd), wide(fwd), wide(fwd), narrow(fwd), wide(bwd), wide(bwd), wide(bwd), narrow(bwd)]
        + cast_specs,
        out_specs=[wide(fwd), wide(bwd)] + cast_specs,
        out_shape=[jax.ShapeDtypeStruct((t, DN_DIM), F32)] * 2
        + [jax.ShapeDtypeStruct(m.shape, BF16) for m in to_bf16],
        scratch_shapes=[pltpu.VMEM((2 * DN_HEADS, DN_HEAD_DIM, DN_HEAD_DIM), F32)],
        compiler_params=_params(("arbitrary",), VMEM_LIMIT),
        name="delta_net",
    )(q, k, v, gb, q, k, v, gb, *to_bf16)


def _rope(x, cos, sin):
    w = x.shape[1]
    lane = lax.broadcasted_iota(jnp.int32, x.shape, 1)
    first_half = (lane & (AXIS_DIM - 1)) < (AXIS_DIM // 2)
    swapped = jnp.where(first_half, pltpu.roll(x, w - AXIS_DIM // 2, 1), pltpu.roll(x, AXIS_DIM // 2, 1))
    return x * cos + swapped * sin


LOG2E = 1.4426950408889634


def _softmax_av(s, sink, vals):
    m = jnp.maximum(jnp.max(s, axis=-1, keepdims=True), sink)
    p = jnp.exp2(s - m)
    denom = jnp.sum(p, axis=-1, keepdims=True) + jnp.exp2(sink - m)
    return _mm(p, vals) / denom


def _attend(q, keys, vals, band, sink_all, o_ref, row0=0):
    b = q.shape[0]
    for kvh in range(ATT_KV_HEADS):
        kl = slice(kvh * HEAD_DIM, (kvh + 1) * HEAD_DIM)
        heads = range(kvh * ATT_GROUP, (kvh + 1) * ATT_GROUP)
        qs = jnp.concatenate([q[:, h * HEAD_DIM:(h + 1) * HEAD_DIM] for h in heads], axis=0)
        sink = jnp.concatenate([jnp.broadcast_to(sink_all[0:1, h:h + 1], (b, 1)) for h in heads], axis=0)
        s = _mm(qs, keys[:, kl], _NT)
        if band is not None:
            kb = ATT_BLOCK
            s = jnp.concatenate([jnp.where(band[0], s[:, 0:kb], NEG), s[:, kb:2 * kb],
                                 jnp.where(band[1], s[:, 2 * kb:3 * kb], NEG), s[:, 3 * kb:]], axis=1)
        o = _softmax_av(s, sink * LOG2E, vals[:, kl])
        for g, h in enumerate(heads):
            o_ref[row0:row0 + b, h * HEAD_DIM:(h + 1) * HEAD_DIM] = o[g * b:(g + 1) * b, :]


def _band_valid(first, last):
    b = ATT_BLOCK
    c = lax.broadcasted_iota(jnp.int32, (1, b), 1)
    r = lax.broadcasted_iota(jnp.int32, (ATT_GROUP * b, 1), 0) & (b - 1)
    prev_ok = jnp.where(first, -1, c) >= r
    next_ok = jnp.where(last, b, c) <= r
    return prev_ok, next_ok


def _rope_block(rowtab_ref, coltab_ref, blk):
    rt = rowtab_ref[blk]
    ct = coltab_ref[...]
    reps = ATT_BLOCK // GRID_W
    rows = jnp.concatenate([jnp.broadcast_to(rt[g:g + 1, :], (GRID_W, LANE)) for g in range(reps)], axis=0)
    return rows + jnp.concatenate([ct] * reps, axis=0)


ATT_STEP_BLOCKS = 4


def _attn_ctx_kernel(q_ref, kctx_ref, sink_ref, o_ref):
    kvx = kctx_ref[...]
    scale = HEAD_DIM ** -0.5 * LOG2E
    _attend(q_ref[...] * scale, kvx[:, :ATT_KV_DIM], kvx[:, ATT_KV_DIM:], None, sink_ref[...], o_ref)


def _attn_lat_kernel(*refs, nb):
    ns = ATT_STEP_BLOCKS
    q_refs, kv_refs = refs[:ns], refs[ns:2 * ns + 2]
    kctx_ref, cosr_ref, sinr_ref, cosc_ref, sinc_ref, sink_ref, o_ref = refs[2 * ns + 2:]
    b = ATT_BLOCK
    scale = HEAD_DIM ** -0.5 * LOG2E
    b0 = ns * pl.program_id(0)
    blocks = [jnp.clip(b0 - 1 + t, 0, nb - 1) for t in range(ns + 2)]
    cos = [_rope_block(cosr_ref, cosc_ref, blk) for blk in blocks]
    sin = [_rope_block(sinr_ref, sinc_ref, blk) for blk in blocks]
    kv = [r[...] for r in kv_refs]
    kvx = kctx_ref[...]
    keys = [_rope(t[:, :ATT_KV_DIM], c_, s_) for t, c_, s_ in zip(kv, cos, sin)]
    sink_all = sink_ref[...]
    for sub in range(ns):
        q = q_refs[sub][...]
        q = jnp.concatenate([_rope(q[:, l * LANE:(l + 1) * LANE], cos[1 + sub], sin[1 + sub])
                             for l in range(ATT_DIM // LANE)], axis=1)
        kcat = jnp.concatenate(keys[sub:sub + 3] + [kvx[:, :ATT_KV_DIM]], axis=0)
        vcat = jnp.concatenate([t[:, ATT_KV_DIM:] for t in kv[sub:sub + 3]] + [kvx[:, ATT_KV_DIM:]], axis=0)
        band = _band_valid(first=(b0 + sub == 0), last=(b0 + sub == nb - 1))
        _attend(q * scale, kcat, vcat, band, sink_all, o_ref, row0=sub * b)


def _attention(pq, pkv, rope, sink, ctx_len, with_ctx):
    t = pq.shape[0]
    n = t - ctx_len
    nb = n // ATT_BLOCK
    ns = ATT_STEP_BLOCKS
    assert nb % ns == 0 and ctx_len % ATT_BLOCK == 0
    off = ctx_len // ATT_BLOCK
    whole = lambda a: pl.BlockSpec(a.shape, lambda j: (0,) * a.ndim)
    ctx_kv = pl.BlockSpec((ctx_len, 2 * ATT_KV_DIM), lambda j: (0, 0))
    sink_spec = pl.BlockSpec((8, LANE), lambda j: (0, 0))
    yc_ctx = None
    if with_ctx:
        yc_ctx = pl.pallas_call(
            _attn_ctx_kernel,
            grid=(1,),
            in_specs=[pl.BlockSpec((ctx_len, ATT_DIM), lambda j: (0, 0)), ctx_kv, sink_spec],
            out_specs=pl.BlockSpec((ctx_len, ATT_DIM), lambda j: (0, 0)),
            out_shape=jax.ShapeDtypeStruct((ctx_len, ATT_DIM), F32),
            compiler_params=_params(("arbitrary",), VMEM_LIMIT),
            name="context_attention",
        )(pq, pkv, sink)
    lat = lambda k: jnp.clip(k, 0, nb - 1) + off
    q_specs = [pl.BlockSpec((ATT_BLOCK, ATT_DIM), lambda j, s=s: (lat(ns * j + s), 0)) for s in range(ns)]
    kv_specs = [pl.BlockSpec((ATT_BLOCK, 2 * ATT_KV_DIM), lambda j, s=s: (lat(ns * j - 1 + s), 0))
                for s in range(ns + 2)]
    yc_lat = pl.pallas_call(
        functools.partial(_attn_lat_kernel, nb=nb),
        grid=(nb // ns,),
        in_specs=q_specs + kv_specs + [ctx_kv] + [whole(a) for a in rope] + [sink_spec],
        out_specs=pl.BlockSpec((ns * ATT_BLOCK, ATT_DIM), lambda j: (j, 0)),
        out_shape=jax.ShapeDtypeStruct((n, ATT_DIM), F32),
        compiler_params=_params(("arbitrary",), VMEM_LIMIT),
        name="attention",
    )(*([pq] * ns + [pkv] * (ns + 3) + list(rope) + [sink]))
    return yc_ctx, yc_lat


def _mixfin_kernel(ctx_ref, h_ref, mod_ref, pconv_ref, prev_ref, next_ref, cw_ref, of_ref, ob_ref, z_ref, ng_ref,
                   ycc_ref, ycl_ref, wout_ref, bd_ref, *rest, nblk, blk0, with_router):
    if with_router:
        wr1_ref, wr0_ref, x_ref, hx_ref, lg_ref = rest
    else:
        wg_ref, wu_ref, wd_ref, x_ref = rest
    i = pl.program_id(0) + blk0
    row = _mod_row(mod_ref, i == 0)
    pc = pconv_ref[...]
    u = pc[:, CONV_CH:2 * CONV_CH] * pc[:, 2 * CONV_CH:]
    prow, nrow = _halo_rows(prev_ref, next_ref, i, nblk)
    prow = prow[:, CONV_CH:2 * CONV_CH] * prow[:, 2 * CONV_CH:]
    nrow = nrow[:, CONV_CH:2 * CONV_CH] * nrow[:, 2 * CONV_CH:]
    up, un = _shift_rows(u, prow, nrow)
    cw = cw_ref[...]
    ya = pc[:, :CONV_CH] * (up * cw[0:1, :] + u * cw[1:2, :] + un * cw[2:3, :])
    o = of_ref[...] + ob_ref[...]
    ms = _group_sum(o * o, bd_ref[...]) * (1.0 / DN_HEAD_DIM)
    z = z_ref[...]
    yb = o * lax.rsqrt(ms + EPS) * ng_ref[...] * (z * _sigmoid(z))
    mix = jnp.concatenate([ya, yb, jnp.where(i == 0, ycc_ref[...], ycl_ref[...])], axis=1)
    x = jnp.where(i == 0, ctx_ref[...], h_ref[...]) + row[:, 2 * D:3 * D] * _mm(mix, wout_ref[...])
    hx = _rmsnorm_rows(x) * (1.0 + row[:, 4 * D:5 * D]) + row[:, 3 * D:4 * D]
    if not with_router:
        hb = hx.astype(BF16)
        acc = jnp.zeros((hb.shape[0], D), F32)
        for f in range(0, D_FF, FF_CHUNK):
            fe = min(f + FF_CHUNK, D_FF)
            g = _mm(hb, wg_ref[:, f:fe])
            u_ = _mm(hb, wu_ref[:, f:fe])
            acc = acc + _mm(g * _sigmoid(g) * u_, wd_ref[f:fe, :])
        x_ref[...] = x + row[:, 5 * D:6 * D] * acc
    else:
        x_ref[...] = x
        hx_ref[...] = hx
        h1, h0 = _split2(hx)
        d = functools.partial(lax.dot_general, dimension_numbers=_NN, preferred_element_type=F32)
        lg = d(h1, wr1_ref[...]) + (d(h0, wr1_ref[...]) + d(h1, wr0_ref[...]))
        lane = lax.broadcasted_iota(jnp.int32, lg.shape, 1)
        lanef = lane.astype(F32)
        lg = jnp.where(lane < N_EXPERTS, lg, -jnp.inf)
        m1 = jnp.max(lg, axis=-1, keepdims=True)
        i1 = jnp.min(jnp.where(lg == m1, lanef, float(LANE)), axis=-1, keepdims=True)
        rest = jnp.where(lanef == i1, -jnp.inf, lg)
        m2 = jnp.max(rest, axis=-1, keepdims=True)
        i2 = jnp.min(jnp.where(rest == m2, lanef, float(LANE)), axis=-1, keepdims=True)
        e2 = jnp.exp(m2 - m1)
        g1 = 1.0 / (1.0 + e2)
        lg_ref[...] = jnp.where(lane == 0, g1, jnp.where(lane == 1, e2 * g1, jnp.where(lane == 2, i1, i2)))


def _mixer_finish(ctx_src, lat_src, lat_blk0, mod, pconv, conv_w, o_f, o_b, pz, norm_g, yc, w_out, bd, ctx_len,
                  with_ctx, router=None, ffn=None):
    assert (router is None) != (ffn is None)
    t = pconv.shape[0]
    nblk = t // TM
    blk0 = 0 if with_ctx else ctx_len // TM
    rows = t - blk0 * TM
    r8 = TM // 8
    w = pconv.shape[1]
    cur = lambda i: (i + blk0, 0)
    out_cur = lambda i: (i, 0)
    const = lambda i: (0, 0)
    ng = jnp.tile(norm_g.reshape(1, DN_HEAD_DIM), (1, DN_HEADS))
    in_specs = [
        pl.BlockSpec((TM, D), const),
        pl.BlockSpec((TM, D), lambda i: (jnp.maximum(i + blk0 - 1, 0) + lat_blk0, 0)),
        pl.BlockSpec((8, N_MOD * D), const),
        pl.BlockSpec((TM, w), cur),
        pl.BlockSpec((8, w), lambda i: (jnp.maximum((i + blk0) * r8 - 1, 0), 0)),
        pl.BlockSpec((8, w), lambda i: (jnp.minimum((i + blk0 + 1) * r8, t // 8 - 1), 0)),
        pl.BlockSpec((3, CONV_CH), const),
        pl.BlockSpec((TM, DN_DIM), cur),
        pl.BlockSpec((TM, DN_DIM), cur),
        pl.BlockSpec((TM, DN_DIM), cur),
        pl.BlockSpec((1, DN_DIM), const),
        pl.BlockSpec((TM, ATT_DIM), const),
        pl.BlockSpec((TM, ATT_DIM), lambda i: (jnp.maximum(i + blk0 - 1, 0), 0)),
        pl.BlockSpec((MIX_DIM, D), const),
        pl.BlockSpec(bd.shape, const),
    ]
    yc_ctx, yc_lat = yc
    if yc_ctx is None:
        assert not with_ctx
        yc_ctx = yc_lat
    args = [ctx_src, lat_src, mod, pconv, pconv, pconv, conv_w, o_f, o_b, pz, ng, yc_ctx, yc_lat, w_out, bd]
    out_specs = [pl.BlockSpec((TM, D), out_cur)]
    out_shape = [jax.ShapeDtypeStruct((rows, D), F32)]
    if router is not None:
        in_specs += [pl.BlockSpec((D, LANE), const)] * 2
        args += list(router)
        out_specs += [pl.BlockSpec((TM, D), out_cur), pl.BlockSpec((TM, LANE), out_cur)]
        out_shape += [jax.ShapeDtypeStruct((rows, D), F32), jax.ShapeDtypeStruct((rows, LANE), F32)]
    else:
        once = dict(pipeline_mode=pl.Buffered(1))
        in_specs += [pl.BlockSpec(w_.shape, const, **once) for w_ in ffn]
        args += list(ffn)
    return pl.pallas_call(
        functools.partial(_mixfin_kernel, nblk=nblk, blk0=blk0, with_router=router is not None),
        grid=(rows // TM,),
        in_specs=in_specs,
        out_specs=out_specs,
        out_shape=out_shape,
        compiler_params=_params(("arbitrary",), VMEM_LIMIT),
        name="mixer_finish",
    )(*args)


def _moe_kernel(be_ref, nu_ref, xs_ref, wg_ref, wu_ref, wd_ref, y_ref, acc_ref):
    b = pl.program_id(0)
    f = pl.program_id(1)

    @pl.when(b < nu_ref[0])
    def _():
        xs = xs_ref[...].astype(BF16)
        part = jnp.zeros((MOE_TM, D), F32)
        for c in range(0, MOE_TF, MOE_SUB):
            g = _mm(xs, wg_ref[0, :, c:c + MOE_SUB])
            u = _mm(xs, wu_ref[0, :, c:c + MOE_SUB])
            part = part + _mm(g * _sigmoid(g) * u, wd_ref[0, c:c + MOE_SUB, :])

        @pl.when(f == 0)
        def _():
            acc_ref[...] = part

        @pl.when(f != 0)
        def _():
            acc_ref[...] += part

        @pl.when(f == pl.num_programs(1) - 1)
        def _():
            y_ref[...] = acc_ref[...]

    @pl.when(b >= nu_ref[0])
    def _():
        y_ref[...] = jnp.zeros(y_ref.shape, F32)


def _moe_experts(xs, blk_e, n_used, wg, wu, wd):
    cap = xs.shape[0]
    nblk = cap // MOE_TM
    nf = D_FF_EXPERT // MOE_TF

    def fidx(b, f, nu):
        return jnp.where(b < nu[0], f, nf - 1)

    grid_spec = pltpu.PrefetchScalarGridSpec(
        num_scalar_prefetch=2,
        grid=(nblk, nf),
        in_specs=[
            pl.BlockSpec((MOE_TM, D), lambda b, f, be, nu: (b, 0)),
            pl.BlockSpec((1, D, MOE_TF), lambda b, f, be, nu: (be[b], 0, fidx(b, f, nu))),
            pl.BlockSpec((1, D, MOE_TF), lambda b, f, be, nu: (be[b], 0, fidx(b, f, nu))),
            pl.BlockSpec((1, MOE_TF, D), lambda b, f, be, nu: (be[b], fidx(b, f, nu), 0)),
        ],
        out_specs=pl.BlockSpec((MOE_TM, D), lambda b, f, be, nu: (b, 0)),
        scratch_shapes=[pltpu.VMEM((MOE_TM, D), F32)],
    )
    return pl.pallas_call(
        _moe_kernel,
        grid_spec=grid_spec,
        out_shape=jax.ShapeDtypeStruct((cap, D), F32),
        compiler_params=_params(("arbitrary", "arbitrary"), VMEM_LIMIT),
        name="moe_experts",
    )(blk_e, n_used, xs, wg, wu, wd)


def _row_tok_kernel(dest_ref, o_ref, *, n, cap):
    def fill(i, c):
        i = i.astype(jnp.int32)
        o_ref[i] = i
        o_ref[n + i] = i
        return c

    lax.fori_loop(0, n, fill, 0, unroll=8)

    def fill_tail(i, c):
        i = i.astype(jnp.int32)
        o_ref[2 * n + i] = i
        return c

    lax.fori_loop(0, cap - 2 * n, fill_tail, 0, unroll=8)

    def put(i, c):
        i = i.astype(jnp.int32)
        o_ref[dest_ref[i]] = i
        o_ref[dest_ref[n + i]] = i
        return c

    lax.fori_loop(0, n, put, 0, unroll=8)


def _row_tok(dest, n, cap):
    return pl.pallas_call(
        functools.partial(_row_tok_kernel, n=n, cap=cap),
        in_specs=[pl.BlockSpec(memory_space=pltpu.SMEM)],
        out_specs=pl.BlockSpec(memory_space=pltpu.SMEM),
        out_shape=jax.ShapeDtypeStruct((cap,), jnp.int32),
        name="row_tok",
    )(dest)


def _moe_route(choices):
    n = choices[0].shape[0]
    a = n * TOP_K
    flat_e = jnp.concatenate(choices)
    onehot = (flat_e[:, None] == jnp.arange(N_EXPERTS, dtype=flat_e.dtype)[None, :]).astype(jnp.int32)
    counts = jnp.sum(onehot, axis=0)
    padded = (counts + MOE_TM - 1) // MOE_TM * MOE_TM
    pad_ends = jnp.cumsum(padded)
    pad_starts = pad_ends - padded
    dest = jnp.sum(onehot * (jnp.cumsum(onehot, axis=0) - onehot + pad_starts[None, :]), axis=1)
    cap = a + N_EXPERTS * MOE_TM
    nblk = cap // MOE_TM
    row_tok = _row_tok(dest.astype(jnp.int32), n, cap)
    blk_start = jnp.arange(nblk, dtype=jnp.int32) * MOE_TM
    blk_e = jnp.minimum(jnp.sum((pad_ends[None, :] <= blk_start[:, None]).astype(jnp.int32), axis=1),
                        N_EXPERTS - 1)
    n_used = (pad_ends[-1] // MOE_TM).astype(jnp.int32).reshape(1)
    last_e = blk_e[jnp.maximum(n_used[0] - 1, 0)]
    blk_e = jnp.where(jnp.arange(nblk) < n_used[0], blk_e, last_e)
    return dest, row_tok, blk_e, n_used


def _final_kernel(x_ref, y0_ref, y1_ref, gt_ref, mod_ref, fg_ref, o_ref):
    mod = mod_ref[...]
    gt = gt_ref[...]
    f = gt[:, 0:1] * y0_ref[...] + gt[:, 1:2] * y1_ref[...]
    x = x_ref[...] + mod[0:1, 5 * D:6 * D] * f
    o_ref[...] = _rmsnorm_rows(x) * fg_ref[...]


def _moe_combine_final(x, y01, gt, mod, final_g):
    n = x.shape[0]
    tm = COMBINE_TM
    assert n % tm == 0
    row = lambda i: (i, 0)
    second = lambda i: (i + n // tm, 0)
    const = lambda i: (0, 0)
    return pl.pallas_call(
        _final_kernel,
        grid=(n // tm,),
        in_specs=[pl.BlockSpec((tm, D), row), pl.BlockSpec((tm, D), row), pl.BlockSpec((tm, D), second),
                  pl.BlockSpec((tm, LANE), row), pl.BlockSpec((8, N_MOD * D), const), pl.BlockSpec((1, D), const)],
        out_specs=pl.BlockSpec((tm, D), row),
        out_shape=jax.ShapeDtypeStruct((n, D), F32),
        compiler_params=_params(("arbitrary",), VMEM_LIMIT),
        name="moe_combine_final",
    )(x, y01, y01, gt, mod, final_g.reshape(1, D))


def _rope_tables(n):
    lane = jnp.arange(LANE, dtype=jnp.int32) % HEAD_DIM
    inv = ROPE_BASE ** (-jnp.arange(0, AXIS_DIM, 2, dtype=F32) / AXIS_DIM)
    freq = inv[lane % (AXIS_DIM // 2)]
    row_axis = (lane // AXIS_DIM) == 0
    sign = jnp.where((lane % AXIS_DIM) < AXIS_DIM // 2, -1.0, 1.0)
    reps = ATT_BLOCK // GRID_W

    def tables(count, on_axis):
        ang = jnp.arange(count, dtype=F32)[:, None] * freq[None, :]
        return jnp.where(on_axis, jnp.cos(ang), 0.0), jnp.where(on_axis, jnp.sin(ang) * sign, 0.0)

    by_block = lambda t: jnp.pad(t.reshape(-1, reps, LANE), ((0, 0), (0, 8 - reps), (0, 0)))
    cosr, sinr = tables(n // GRID_W, row_axis)
    cosc, sinc = tables(GRID_W, ~row_axis)
    return by_block(cosr), by_block(sinr), cosc, sinc


def _prep_w_in(w):
    pad = jnp.zeros((D, LANE - N_AB), w.dtype)
    return jnp.concatenate([w[:, :_C_Z], w[:, _C_A:_C_Q], pad, w[:, _C_Z:_C_A], w[:, _C_Q:_C_END]],
                           axis=1).astype(BF16)


def kernel(x, c, ctx, c_ctx, w_mod, b_mod, w_in, w_out, conv_w, dn_conv_w, dn_a_log, dn_dt_bias, dn_norm_g,
           attn_sink, ffn_w_gate, ffn_w_up, ffn_w_down, moe_router, moe_w_gate, moe_w_up, moe_w_down,
           final_norm_g):
    bsz, n, d = x.shape
    ctx_len = ctx.shape[1]
    depth = w_in.shape[0]
    assert bsz == 1 and d == D and ctx_len == TM and n % TM == 0 and n % GRID_W == 0
    rope = _rope_tables(n)
    mods = _mod_vectors(c, c_ctx, w_mod, b_mod)
    bd = _head_blockdiag(DN_DIM, DN_HEAD_DIM)
    stream = (ctx[0], x[0], 0)
    for layer in range(depth):
        last = layer == depth - 1
        mod = mods[layer]
        pconv, pz, pq, pkv, qn, kn, vv, gb = _in_proj(*stream, mod, _prep_w_in(w_in[layer]), dn_conv_w[layer],
                                                      dn_a_log[layer], dn_dt_bias[layer], bd)
        to_bf16 = []
        if layer % 2 == 0:
            to_bf16 += [w[layer // 2] for w in (ffn_w_gate, ffn_w_up, ffn_w_down)]
        if (layer + 1) % 2 == 1 and layer + 1 < depth:
            j = (layer + 1) // 2
            to_bf16 += [w[j].reshape(-1, w.shape[-1]) for w in (moe_w_gate, moe_w_up, moe_w_down)]
        o_f, o_b, *cast = _delta_net(qn, kn, vv, gb, ctx_len, tuple(to_bf16))
        if layer % 2 == 0:
            ffn_bf16, cast = cast[:3], cast[3:]
        if cast:
            moe_bf16 = [c.reshape(w.shape[1:]) for c, w in zip(cast, (moe_w_gate, moe_w_up, moe_w_down))]
        sink = jnp.zeros((8, LANE), F32).at[0, :ATT_HEADS].set(attn_sink[layer])
        yc = _attention(pq, pkv, rope, sink, ctx_len, with_ctx=not last)
        router = None
        if layer % 2 == 1:
            wr = jnp.zeros((D, LANE), F32).at[:, :N_EXPERTS].set(moe_router[layer // 2])
            wr1 = wr.astype(BF16)
            router = (wr1, (wr - wr1.astype(F32)).astype(BF16))
        outs = _mixer_finish(*stream, mod, pconv, conv_w[layer], o_f, o_b, pz, dn_norm_g[layer], yc,
                             w_out[layer].astype(BF16), bd, ctx_len, with_ctx=not last, router=router,
                             ffn=ffn_bf16 if layer % 2 == 0 else None)
        if layer % 2 == 0:
            assert not last
            h, = outs
            stream = (h, h, ctx_len // TM)
        else:
            assert last
            x1, hx, route = outs
            dest, row_tok, blk_e, n_used = _moe_route([route[:, 2 + s].astype(jnp.int32) for s in range(TOP_K)])
            take = lambda rows_, idx: rows_.at[idx].get(mode="promise_in_bounds")
            y = _moe_experts(take(hx, row_tok), blk_e, n_used, *moe_bf16)
            h = _moe_combine_final(x1, take(y, dest), route, mod, final_norm_g)
    return h.reshape(bsz, n, d)
```

```python
import functools

import jax
import jax.numpy as jnp
from jax import lax
from jax.experimental import pallas as pl
from jax.experimental.pallas import tpu as pltpu

F32 = jnp.float32
BF16 = jnp.bfloat16

D = 1024
N_MOD = 6
EPS = 1e-6
NEG = -1e30
GRID_W = 64

CONV_CH = 256
DN_HEADS = 6
DN_HEAD_DIM = 64
DN_DIM = DN_HEADS * DN_HEAD_DIM
DN_CHUNK = 64
DN_SUB = 16
DN_STEP_CHUNKS = 4
ATT_HEADS = 6
ATT_KV_HEADS = 2
ATT_GROUP = ATT_HEADS // ATT_KV_HEADS
HEAD_DIM = 64
ATT_DIM = ATT_HEADS * HEAD_DIM
ATT_KV_DIM = ATT_KV_HEADS * HEAD_DIM
ATT_BLOCK = 128
ROPE_BASE = 10000.0
AXIS_DIM = HEAD_DIM // 2
MIX_DIM = CONV_CH + DN_DIM + ATT_DIM

D_FF = 2816
N_EXPERTS = 8
TOP_K = 2
D_FF_EXPERT = 3584

TM = 256
FF_CHUNK = 1024
MOE_TM = 512
MOE_TF = 1792
MOE_SUB = 256
COMBINE_TM = 1024
LANE = 128
VMEM_LIMIT = 56 * 1024 * 1024

_C_QKV = 3 * CONV_CH
_C_Z = _C_QKV + 3 * DN_DIM
_C_A = _C_Z + DN_DIM
_C_Q = _C_A + 4 * DN_HEADS
_C_K = _C_Q + ATT_DIM
_C_V = _C_K + ATT_KV_DIM
_C_END = _C_V + ATT_KV_DIM
N_AB = 4 * DN_HEADS


def _params(sem=None, vmem=None):
    kw = {}
    if sem is not None:
        kw["dimension_semantics"] = sem
    if vmem is not None:
        kw["vmem_limit_bytes"] = vmem
    return pltpu.CompilerParams(**kw)


def _split2(a):
    hi = a.astype(BF16)
    lo = (a - hi.astype(F32)).astype(BF16)
    return hi, lo


def _split3(a):
    hi = a.astype(BF16)
    r = a - hi.astype(F32)
    mid = r.astype(BF16)
    lo = (r - mid.astype(F32)).astype(BF16)
    return hi, mid, lo


_NN = (((1,), (0,)), ((), ()))
_NT = (((1,), (1,)), ((), ()))


def _mm(a, b, dims=_NN):
    return lax.dot_general(a.astype(BF16), b.astype(BF16), dims, preferred_element_type=F32)


def _mm3(a, b, dims=_NN):
    a1, a0 = _split2(a)
    b1, b0 = _split2(b)
    d = functools.partial(lax.dot_general, dimension_numbers=dims, preferred_element_type=F32)
    return d(a1, b1) + (d(a1, b0) + d(a0, b1))


_BNN = (((2,), (1,)), ((0,), (0,)))
_BNT = (((2,), (2,)), ((0,), (0,)))
_BTN = (((1,), (1,)), ((0,), (0,)))


def _bmm(a, b, dims=_BNN):
    return lax.dot_general(a.astype(BF16), b.astype(BF16), dims, preferred_element_type=F32)


def _sigmoid(x):
    return 1.0 / (1.0 + jnp.exp(-x))


def _softplus(x):
    return jnp.maximum(x, 0.0) + jnp.log1p(jnp.exp(-jnp.abs(x)))


def _mod_row(mod_ref, is_ctx):
    mod = mod_ref[...]
    return jnp.where(is_ctx, mod[1:2, :], mod[0:1, :])


def _rmsnorm_rows(x):
    return x * lax.rsqrt(jnp.mean(x * x, axis=-1, keepdims=True) + EPS)


def _shift_rows(u, prow, nrow):
    n = u.shape[0]
    rid = lax.broadcasted_iota(jnp.int32, u.shape, 0)
    up = jnp.where(rid == 0, prow, pltpu.roll(u, 1, 0))
    un = jnp.where(rid == n - 1, nrow, pltpu.roll(u, n - 1, 0))
    return up, un


def _same_group(shape, group):
    sh = group.bit_length() - 1
    assert 1 << sh == group
    return (lax.broadcasted_iota(jnp.int32, shape, 0) >> sh) == (lax.broadcasted_iota(jnp.int32, shape, 1) >> sh)


def _head_blockdiag(n, group):
    g = jnp.arange(n, dtype=jnp.int32) // group
    return (g[:, None] == g[None, :]).astype(BF16)


def _group_sum(t, bd):
    return _mm(t, bd)


MOD_TN = 1536


def _mod_kernel(s_ref, w_ref, b_ref, o_ref):
    s = s_ref[...]
    s = s * _sigmoid(s)
    o_ref[0] = _mm3(s, w_ref[0]) + b_ref[0]


def _mod_vectors(c, c_ctx, w_mod, b_mod):
    depth = w_mod.shape[0]
    s = jnp.zeros((8, D), F32).at[0].set(c[0]).at[1].set(c_ctx)
    return pl.pallas_call(
        _mod_kernel,
        grid=(depth, N_MOD * D // MOD_TN),
        in_specs=[
            pl.BlockSpec((8, D), lambda l, j: (0, 0)),
            pl.BlockSpec((1, D, MOD_TN), lambda l, j: (l, 0, j)),
            pl.BlockSpec((1, 1, MOD_TN), lambda l, j: (l, 0, j)),
        ],
        out_specs=pl.BlockSpec((1, 8, MOD_TN), lambda l, j: (l, 0, j)),
        out_shape=jax.ShapeDtypeStruct((depth, 8, N_MOD * D), F32),
        compiler_params=_params(("arbitrary", "arbitrary"), VMEM_LIMIT),
        name="mod_vectors",
    )(s, w_mod, b_mod.reshape(depth, 1, N_MOD * D))


def _halo_valid(i, nblk):
    return jnp.logical_and(i != 0, i != 1), jnp.logical_and(i != 0, i != nblk - 1)


def _halo_rows(prev_ref, next_ref, i, nblk):
    pvalid, nvalid = _halo_valid(i, nblk)
    prow = jnp.where(pvalid, prev_ref[7:8, :], 0.0)
    nrow = jnp.where(nvalid, next_ref[0:1, :], 0.0)
    return prow, nrow


def _in_kernel(ctx_ref, h_ref, hprev_ref, hnext_ref, mod_ref, w_ref, cw_ref, alog_ref, dtb_ref, bd_ref,
               pconv_ref, pz_ref, pq_ref, pkv_ref, q_ref, k_ref, v_ref, gb_ref, *, nblk):
    i = pl.program_id(0)
    row = _mod_row(mod_ref, i == 0)
    norm_mod = lambda x: _rmsnorm_rows(x) * (1.0 + row[:, D:2 * D]) + row[:, 0:D]
    h1 = norm_mod(jnp.where(i == 0, ctx_ref[...], h_ref[...])).astype(BF16)
    halo = norm_mod(jnp.concatenate([hprev_ref[...], hnext_ref[...]], axis=0)).astype(BF16)
    d = functools.partial(lax.dot_general, dimension_numbers=_NN, preferred_element_type=F32)
    tm = h1.shape[0]
    c0 = 3 * CONV_CH
    c1 = c0 + 3 * DN_DIM
    c2 = c1 + LANE
    c3 = c2 + DN_DIM + ATT_DIM
    pconv_ref[...] = d(h1, w_ref[:, 0:c0])
    zq = d(h1, w_ref[:, c2:c3])
    pz_ref[...] = zq[:, 0:DN_DIM]
    pq_ref[...] = zq[:, DN_DIM:]
    pkv_ref[...] = d(h1, w_ref[:, c3:c3 + 2 * ATT_KV_DIM])
    qkv_ab = d(jnp.concatenate([h1, halo], axis=0), w_ref[:, c0:c2])
    qkv = qkv_ab[:, 0:3 * DN_DIM]
    ab = qkv_ab[0:tm, 3 * DN_DIM:]

    u = qkv[0:tm, :]
    pvalid, nvalid = _halo_valid(i, nblk)
    prow = jnp.where(pvalid, qkv[tm + 7:tm + 8, :], 0.0)
    nrow = jnp.where(nvalid, qkv[tm + 8:tm + 9, :], 0.0)
    up, un = _shift_rows(u, prow, nrow)
    cw = cw_ref[...]
    y = up * cw[0:1, :] + u * cw[1:2, :] + un * cw[2:3, :]
    y = y * _sigmoid(y)
    q = y[:, 0:DN_DIM]
    k = y[:, DN_DIM:2 * DN_DIM]
    bd = bd_ref[...]
    q_ref[...] = q * lax.rsqrt(_group_sum(q * q, bd) + 1e-6) * (DN_HEAD_DIM ** -0.5)
    k_ref[...] = k * lax.rsqrt(_group_sum(k * k, bd) + 1e-6)
    v_ref[...] = y[:, 2 * DN_DIM:3 * DN_DIM]
    g = -jnp.exp(alog_ref[...]) * _softplus(ab + dtb_ref[...])
    lane = lax.broadcasted_iota(jnp.int32, ab.shape, 1)
    gb_ref[...] = jnp.where(lane < 2 * DN_HEADS, g, _sigmoid(ab))


def _in_proj(ctx_src, lat_src, lat_blk0, mod, w_main, dn_conv_w, a_log, dt_bias, bd):
    r8 = TM // 8
    last8 = lat_src.shape[0] // 8 - 1
    nblk = 1 + lat_src.shape[0] // TM - lat_blk0
    t = nblk * TM
    alog = jnp.zeros((1, LANE), F32).at[0, :2 * DN_HEADS].set(a_log.reshape(-1))
    dtb = jnp.zeros((1, LANE), F32).at[0, :2 * DN_HEADS].set(dt_bias.reshape(-1))
    widths = (3 * CONV_CH, DN_DIM, ATT_DIM, 2 * ATT_KV_DIM, DN_DIM, DN_DIM, DN_DIM, LANE)
    const = lambda i: (0, 0)
    lat = lambda i: jnp.maximum(i - 1, 0) + lat_blk0
    return pl.pallas_call(
        functools.partial(_in_kernel, nblk=nblk),
        grid=(nblk,),
        in_specs=[
            pl.BlockSpec((TM, D), const),
            pl.BlockSpec((TM, D), lambda i: (lat(i), 0)),
            pl.BlockSpec((8, D), lambda i: (jnp.maximum(lat(i) * r8 - 1, 0), 0)),
            pl.BlockSpec((8, D), lambda i: (jnp.minimum((lat(i) + 1) * r8, last8), 0)),
            pl.BlockSpec((8, N_MOD * D), const),
            pl.BlockSpec(w_main.shape, const),
            pl.BlockSpec((3, 3 * DN_DIM), const),
            pl.BlockSpec((1, LANE), const),
            pl.BlockSpec((1, LANE), const),
            pl.BlockSpec(bd.shape, const),
        ],
        out_specs=[pl.BlockSpec((TM, w), lambda i: (i, 0)) for w in widths],
        out_shape=[jax.ShapeDtypeStruct((t, w), F32) for w in widths],
        compiler_params=_params(("arbitrary",), VMEM_LIMIT),
        name="in_proj",
    )(ctx_src, lat_src, lat_src, lat_src, mod, w_main, dn_conv_w, alog, dtb, bd)


def _dn_block(fwd_refs, bwd_refs, of_ref, ob_ref, s_ref):
    c_ = DN_CHUNK
    nh = DN_HEADS
    nchunks = fwd_refs[0].shape[0] // c_
    nb = nchunks * nh
    rows = lambda g: slice(g * c_, (g + 1) * c_)
    lanes = lambda h: slice(h * DN_HEAD_DIM, (h + 1) * DN_HEAD_DIM)
    ri = lax.broadcasted_iota(jnp.int32, (c_, c_), 0)
    ci = lax.broadcasted_iota(jnp.int32, (c_, c_), 1)
    same_sub = _same_group((c_, c_), DN_SUB)
    eye = jnp.where(ri == ci, 1.0, 0.0)
    shape3 = (2 * nb, c_, c_)
    delta = lax.broadcasted_iota(jnp.int32, shape3, 1) - lax.broadcasted_iota(jnp.int32, shape3, 2)
    delta = jnp.where(lax.broadcasted_iota(jnp.int32, shape3, 0) >= nb, -delta, delta)
    incl = delta >= 0
    strict = delta > 0

    def stack(fn):
        return jnp.stack([fn(d, refs, g, h) for d, refs in enumerate((fwd_refs, bwd_refs))
                          for g in range(nchunks) for h in range(nh)])

    gbs = [refs[3][...] for refs in (fwd_refs, bwd_refs)]
    tris = [jnp.where(ri >= ci, 1.0, 0.0).astype(BF16), jnp.where(ri <= ci, 1.0, 0.0).astype(BF16)]
    gcs = [[_cumsum_rows(tris[d], gbs[d][rows(g), :]) for g in range(nchunks)] for d in range(2)]
    gcts = [[gc.T for gc in gcs[d]] for d in range(2)]
    q = stack(lambda d, refs, g, h: refs[0][rows(g), lanes(h)])
    k = stack(lambda d, refs, g, h: refs[1][rows(g), lanes(h)])
    v = stack(lambda d, refs, g, h: refs[2][rows(g), lanes(h)])
    col = lambda d, h: d * nh + h
    gcol = stack(lambda d, refs, g, h: gcs[d][g][:, col(d, h):col(d, h) + 1])
    grow = stack(lambda d, refs, g, h: gcts[d][g][col(d, h):col(d, h) + 1, :])
    beta = stack(lambda d, refs, g, h: gbs[d][rows(g), 2 * nh + col(d, h):2 * nh + col(d, h) + 1])
    glast = jnp.concatenate([gcol[:nb, c_ - 1:c_, :], gcol[nb:, 0:1, :]], axis=0)
    decay = jnp.where(incl, jnp.exp(jnp.where(incl, gcol - grow, 0.0)), 0.0)
    eg = jnp.exp(gcol)
    kb = k * beta
    a = jnp.where(strict, _bmm(kb, k, _BNT) * decay, 0.0)
    qk = jnp.where(incl, _bmm(q, k, _BNT) * decay, 0.0)
    ad = jnp.where(same_sub, a, 0.0)
    ao = a - ad
    p = eye - ad
    n2 = _bmm(ad, ad)
    p = p + _bmm(p, n2)
    n4 = _bmm(n2, n2)
    p = p + _bmm(p, n4)
    n8 = _bmm(n4, n4)
    dinv = p + _bmm(p, n8)
    m = _bmm(dinv, ao)
    m2 = _bmm(m, m)
    y = _bmm(dinv, jnp.concatenate([v * beta, kb * eg], axis=-1))
    z = y + _bmm(m2, y)
    x = z - _bmm(m, z)
    u = x[:, :, :DN_HEAD_DIM]
    w = x[:, :, DN_HEAD_DIM:]
    qg = q * eg
    kd = k * jnp.exp(glast - gcol)
    gl = jnp.exp(glast)
    s = s_ref[...]
    for t in range(nchunks):
        gf, gr = t, nchunks - 1 - t
        step = lambda a: jnp.concatenate([a[gf * nh:(gf + 1) * nh], a[nb + gr * nh:nb + (gr + 1) * nh]], axis=0)
        v_new = step(u) - _bmm(step(w), s)
        o = _bmm(step(qg), s) + _bmm(step(qk), v_new)
        s = s * step(gl) + _bmm(step(kd), v_new, _BTN)
        for h in range(nh):
            of_ref[rows(gf), lanes(h)] = o[h]
            ob_ref[rows(gr), lanes(h)] = o[nh + h]
    s_ref[...] = s


def _cumsum_rows(tri_bf16, g):
    g2, g1, g0 = _split3(g)
    d = functools.partial(lax.dot_general, dimension_numbers=_NN, preferred_element_type=F32)
    return d(tri_bf16, g2) + (d(tri_bf16, g1) + d(tri_bf16, g0))


def _dn_kernel(qf, kf, vf, gf, qb, kb, vb, gbb, *rest, n_cast):
    cast_in = rest[:n_cast]
    of_ref, ob_ref = rest[n_cast:n_cast + 2]
    cast_out = rest[n_cast + 2:2 * n_cast + 2]
    s_ref = rest[-1]

    @pl.when(pl.program_id(0) == 0)
    def _():
        s_ref[...] = jnp.zeros(s_ref.shape, F32)

    _dn_block((qf, kf, vf, gf), (qb, kb, vb, gbb), of_ref, ob_ref, s_ref)
    for src, dst in zip(cast_in, cast_out):
        dst[...] = src[...].astype(BF16)


def _delta_net(q, k, v, gb, ctx_len, to_bf16=()):
    t = q.shape[0]
    rows = DN_STEP_CHUNKS * DN_CHUNK
    assert ctx_len == rows and t % rows == 0
    nstep = t // rows
    cast_specs = []
    for m in to_bf16:
        rb = -(-m.shape[0] // nstep)
        rb = -(-rb // 16) * 16
        nb = -(-m.shape[0] // rb)
        cast_specs.append(pl.BlockSpec((rb, m.shape[1]), lambda s, nb=nb: (jnp.minimum(s, nb - 1), 0)))

    def fwd(s):
        return (s, 0)

    def bwd(s):
        return (jnp.where(s == 0, 0, nstep - s), 0)

    wide = lambda im: pl.BlockSpec((rows, DN_DIM), im)
    narrow = lambda im: pl.BlockSpec((rows, LANE), im)
    return pl.pallas_call(
        functools.partial(_dn_kernel, n_cast=len(to_bf16)),
        grid=(nstep,),
        in_specs=[wide(fwd), wide(fwd), wide(fwd), narrow(fwd), wide(bwd), wide(bwd), wide(bwd), narrow(bwd)]
        + cast_specs,
        out_specs=[wide(fwd), wide(bwd)] + cast_specs,
        out_shape=[jax.ShapeDtypeStruct((t, DN_DIM), F32)] * 2
        + [jax.ShapeDtypeStruct(m.shape, BF16) for m in to_bf16],
        scratch_shapes=[pltpu.VMEM((2 * DN_HEADS, DN_HEAD_DIM, DN_HEAD_DIM), F32)],
        compiler_params=_params(("arbitrary",), VMEM_LIMIT),
        name="delta_net",
    )(q, k, v, gb, q, k, v, gb, *to_bf16)


def _rope(x, cos, sin):
    w = x.shape[1]
    lane = lax.broadcasted_iota(jnp.int32, x.shape, 1)
    first_half = (lane & (AXIS_DIM - 1)) < (AXIS_DIM // 2)
    swapped = jnp.where(first_half, pltpu.roll(x, w - AXIS_DIM // 2, 1), pltpu.roll(x, AXIS_DIM // 2, 1))
    return x * cos + swapped * sin


LOG2E = 1.4426950408889634


def _softmax_av(s, sink, vals):
    m = jnp.maximum(jnp.max(s, axis=-1, keepdims=True), sink)
    p = jnp.exp2(s - m)
    denom = jnp.sum(p, axis=-1, keepdims=True) + jnp.exp2(sink - m)
    return _mm(p, vals) / denom


def _attend(q, keys, vals, band, sink_all, o_ref, row0=0):
    b = q.shape[0]
    for kvh in range(ATT_KV_HEADS):
        kl = slice(kvh * HEAD_DIM, (kvh + 1) * HEAD_DIM)
        heads = range(kvh * ATT_GROUP, (kvh + 1) * ATT_GROUP)
        qs = jnp.concatenate([q[:, h * HEAD_DIM:(h + 1) * HEAD_DIM] for h in heads], axis=0)
        sink = jnp.concatenate([jnp.broadcast_to(sink_all[0:1, h:h + 1], (b, 1)) for h in heads], axis=0)
        s = _mm(qs, keys[:, kl], _NT)
        if band is not None:
            kb = ATT_BLOCK
            s = jnp.concatenate([jnp.where(band[0], s[:, 0:kb], NEG), s[:, kb:2 * kb],
                                 jnp.where(band[1], s[:, 2 * kb:3 * kb], NEG), s[:, 3 * kb:]], axis=1)
        o = _softmax_av(s, sink * LOG2E, vals[:, kl])
        for g, h in enumerate(heads):
            o_ref[row0:row0 + b, h * HEAD_DIM:(h + 1) * HEAD_DIM] = o[g * b:(g + 1) * b, :]


def _band_valid(first, last):
    b = ATT_BLOCK
    c = lax.broadcasted_iota(jnp.int32, (1, b), 1)
    r = lax.broadcasted_iota(jnp.int32, (ATT_GROUP * b, 1), 0) & (b - 1)
    prev_ok = jnp.where(first, -1, c) >= r
    next_ok = jnp.where(last, b, c) <= r
    return prev_ok, next_ok


def _rope_block(rowtab_ref, coltab_ref, blk):
    rt = rowtab_ref[blk]
    ct = coltab_ref[...]
    reps = ATT_BLOCK // GRID_W
    rows = jnp.concatenate([jnp.broadcast_to(rt[g:g + 1, :], (GRID_W, LANE)) for g in range(reps)], axis=0)
    return rows + jnp.concatenate([ct] * reps, axis=0)


ATT_STEP_BLOCKS = 4


def _attn_ctx_kernel(q_ref, kctx_ref, sink_ref, o_ref):
    kvx = kctx_ref[...]
    scale = HEAD_DIM ** -0.5 * LOG2E
    _attend(q_ref[...] * scale, kvx[:, :ATT_KV_DIM], kvx[:, ATT_KV_DIM:], None, sink_ref[...], o_ref)


def _attn_lat_kernel(*refs, nb):
    ns = ATT_STEP_BLOCKS
    q_refs, kv_refs = refs[:ns], refs[ns:2 * ns + 2]
    kctx_ref, cosr_ref, sinr_ref, cosc_ref, sinc_ref, sink_ref, o_ref = refs[2 * ns + 2:]
    b = ATT_BLOCK
    scale = HEAD_DIM ** -0.5 * LOG2E
    b0 = ns * pl.program_id(0)
    blocks = [jnp.clip(b0 - 1 + t, 0, nb - 1) for t in range(ns + 2)]
    cos = [_rope_block(cosr_ref, cosc_ref, blk) for blk in blocks]
    sin = [_rope_block(sinr_ref, sinc_ref, blk) for blk in blocks]
    kv = [r[...] for r in kv_refs]
    kvx = kctx_ref[...]
    keys = [_rope(t[:, :ATT_KV_DIM], c_, s_) for t, c_, s_ in zip(kv, cos, sin)]
    sink_all = sink_ref[...]
    for sub in range(ns):
        q = q_refs[sub][...]
        q = jnp.concatenate([_rope(q[:, l * LANE:(l + 1) * LANE], cos[1 + sub], sin[1 + sub])
                             for l in range(ATT_DIM // LANE)], axis=1)
        kcat = jnp.concatenate(keys[sub:sub + 3] + [kvx[:, :ATT_KV_DIM]], axis=0)
        vcat = jnp.concatenate([t[:, ATT_KV_DIM:] for t in kv[sub:sub + 3]] + [kvx[:, ATT_KV_DIM:]], axis=0)
        band = _band_valid(first=(b0 + sub == 0), last=(b0 + sub == nb - 1))
        _attend(q * scale, kcat, vcat, band, sink_all, o_ref, row0=sub * b)


def _attention(pq, pkv, rope, sink, ctx_len, with_ctx):
    t = pq.shape[0]
    n = t - ctx_len
    nb = n // ATT_BLOCK
    ns = ATT_STEP_BLOCKS
    assert nb % ns == 0 and ctx_len % ATT_BLOCK == 0
    off = ctx_len // ATT_BLOCK
    whole = lambda a: pl.BlockSpec(a.shape, lambda j: (0,) * a.ndim)
    ctx_kv = pl.BlockSpec((ctx_len, 2 * ATT_KV_DIM), lambda j: (0, 0))
    sink_spec = pl.BlockSpec((8, LANE), lambda j: (0, 0))
    yc_ctx = None
    if with_ctx:
        yc_ctx = pl.pallas_call(
            _attn_ctx_kernel,
            grid=(1,),
            in_specs=[pl.BlockSpec((ctx_len, ATT_DIM), lambda j: (0, 0)), ctx_kv, sink_spec],
            out_specs=pl.BlockSpec((ctx_len, ATT_DIM), lambda j: (0, 0)),
            out_shape=jax.ShapeDtypeStruct((ctx_len, ATT_DIM), F32),
            compiler_params=_params(("arbitrary",), VMEM_LIMIT),
            name="context_attention",
        )(pq, pkv, sink)
    lat = lambda k: jnp.clip(k, 0, nb - 1) + off
    q_specs = [pl.BlockSpec((ATT_BLOCK, ATT_DIM), lambda j, s=s: (lat(ns * j + s), 0)) for s in range(ns)]
    kv_specs = [pl.BlockSpec((ATT_BLOCK, 2 * ATT_KV_DIM), lambda j, s=s: (lat(ns * j - 1 + s), 0))
                for s in range(ns + 2)]
    yc_lat = pl.pallas_call(
        functools.partial(_attn_lat_kernel, nb=nb),
        grid=(nb // ns,),
        in_specs=q_specs + kv_specs + [ctx_kv] + [whole(a) for a in rope] + [sink_spec],
        out_specs=pl.BlockSpec((ns * ATT_BLOCK, ATT_DIM), lambda j: (j, 0)),
        out_shape=jax.ShapeDtypeStruct((n, ATT_DIM), F32),
        compiler_params=_params(("arbitrary",), VMEM_LIMIT),
        name="attention",
    )(*([pq] * ns + [pkv] * (ns + 3) + list(rope) + [sink]))
    return yc_ctx, yc_lat


def _mixfin_kernel(ctx_ref, h_ref, mod_ref, pconv_ref, prev_ref, next_ref, cw_ref, of_ref, ob_ref, z_ref, ng_ref,
                   ycc_ref, ycl_ref, wout_ref, bd_ref, *rest, nblk, blk0, with_router):
    if with_router:
        wr1_ref, wr0_ref, x_ref, hx_ref, lg_ref = rest
    else:
        wg_ref, wu_ref, wd_ref, x_ref = rest
    i = pl.program_id(0) + blk0
    row = _mod_row(mod_ref, i == 0)
    pc = pconv_ref[...]
    u = pc[:, CONV_CH:2 * CONV_CH] * pc[:, 2 * CONV_CH:]
    prow, nrow = _halo_rows(prev_ref, next_ref, i, nblk)
    prow = prow[:, CONV_CH:2 * CONV_CH] * prow[:, 2 * CONV_CH:]
    nrow = nrow[:, CONV_CH:2 * CONV_CH] * nrow[:, 2 * CONV_CH:]
    up, un = _shift_rows(u, prow, nrow)
    cw = cw_ref[...]
    ya = pc[:, :CONV_CH] * (up * cw[0:1, :] + u * cw[1:2, :] + un * cw[2:3, :])
    o = of_ref[...] + ob_ref[...]
    ms = _group_sum(o * o, bd_ref[...]) * (1.0 / DN_HEAD_DIM)
    z = z_ref[...]
    yb = o * lax.rsqrt(ms + EPS) * ng_ref[...] * (z * _sigmoid(z))
    mix = jnp.concatenate([ya, yb, jnp.where(i == 0, ycc_ref[...], ycl_ref[...])], axis=1)
    x = jnp.where(i == 0, ctx_ref[...], h_ref[...]) + row[:, 2 * D:3 * D] * _mm(mix, wout_ref[...])
    hx = _rmsnorm_rows(x) * (1.0 + row[:, 4 * D:5 * D]) + row[:, 3 * D:4 * D]
    if not with_router:
        hb = hx.astype(BF16)
        acc = jnp.zeros((hb.shape[0], D), F32)
        for f in range(0, D_FF, FF_CHUNK):
            fe = min(f + FF_CHUNK, D_FF)
            g = _mm(hb, wg_ref[:, f:fe])
            u_ = _mm(hb, wu_ref[:, f:fe])
            acc = acc + _mm(g * _sigmoid(g) * u_, wd_ref[f:fe, :])
        x_ref[...] = x + row[:, 5 * D:6 * D] * acc
    else:
        x_ref[...] = x
        hx_ref[...] = hx
        h1, h0 = _split2(hx)
        d = functools.partial(lax.dot_general, dimension_numbers=_NN, preferred_element_type=F32)
        lg = d(h1, wr1_ref[...]) + (d(h0, wr1_ref[...]) + d(h1, wr0_ref[...]))
        lane = lax.broadcasted_iota(jnp.int32, lg.shape, 1)
        lanef = lane.astype(F32)
        lg = jnp.where(lane < N_EXPERTS, lg, -jnp.inf)
        m1 = jnp.max(lg, axis=-1, keepdims=True)
        i1 = jnp.min(jnp.where(lg == m1, lanef, float(LANE)), axis=-1, keepdims=True)
        rest = jnp.where(lanef == i1, -jnp.inf, lg)
        m2 = jnp.max(rest, axis=-1, keepdims=True)
        i2 = jnp.min(jnp.where(rest == m2, lanef, float(LANE)), axis=-1, keepdims=True)
        e2 = jnp.exp(m2 - m1)
        g1 = 1.0 / (1.0 + e2)
        lg_ref[...] = jnp.where(lane == 0, g1, jnp.where(lane == 1, e2 * g1, jnp.where(lane == 2, i1, i2)))


def _mixer_finish(ctx_src, lat_src, lat_blk0, mod, pconv, conv_w, o_f, o_b, pz, norm_g, yc, w_out, bd, ctx_len,
                  with_ctx, router=None, ffn=None):
    assert (router is None) != (ffn is None)
    t = pconv.shape[0]
    nblk = t // TM
    blk0 = 0 if with_ctx else ctx_len // TM
    rows = t - blk0 * TM
    r8 = TM // 8
    w = pconv.shape[1]
    cur = lambda i: (i + blk0, 0)
    out_cur = lambda i: (i, 0)
    const = lambda i: (0, 0)
    ng = jnp.tile(norm_g.reshape(1, DN_HEAD_DIM), (1, DN_HEADS))
    in_specs = [
        pl.BlockSpec((TM, D), const),
        pl.BlockSpec((TM, D), lambda i: (jnp.maximum(i + blk0 - 1, 0) + lat_blk0, 0)),
        pl.BlockSpec((8, N_MOD * D), const),
        pl.BlockSpec((TM, w), cur),
        pl.BlockSpec((8, w), lambda i: (jnp.maximum((i + blk0) * r8 - 1, 0), 0)),
        pl.BlockSpec((8, w), lambda i: (jnp.minimum((i + blk0 + 1) * r8, t // 8 - 1), 0)),
        pl.BlockSpec((3, CONV_CH), const),
        pl.BlockSpec((TM, DN_DIM), cur),
        pl.BlockSpec((TM, DN_DIM), cur),
        pl.BlockSpec((TM, DN_DIM), cur),
        pl.BlockSpec((1, DN_DIM), const),
        pl.BlockSpec((TM, ATT_DIM), const),
        pl.BlockSpec((TM, ATT_DIM), lambda i: (jnp.maximum(i + blk0 - 1, 0), 0)),
        pl.BlockSpec((MIX_DIM, D), const),
        pl.BlockSpec(bd.shape, const),
    ]
    yc_ctx, yc_lat = yc
    if yc_ctx is None:
        assert not with_ctx
        yc_ctx = yc_lat
    args = [ctx_src, lat_src, mod, pconv, pconv, pconv, conv_w, o_f, o_b, pz, ng, yc_ctx, yc_lat, w_out, bd]
    out_specs = [pl.BlockSpec((TM, D), out_cur)]
    out_shape = [jax.ShapeDtypeStruct((rows, D), F32)]
    if router is not None:
        in_specs += [pl.BlockSpec((D, LANE), const)] * 2
        args += list(router)
        out_specs += [pl.BlockSpec((TM, D), out_cur), pl.BlockSpec((TM, LANE), out_cur)]
        out_shape += [jax.ShapeDtypeStruct((rows, D), F32), jax.ShapeDtypeStruct((rows, LANE), F32)]
    else:
        once = dict(pipeline_mode=pl.Buffered(1))
        in_specs += [pl.BlockSpec(w_.shape, const, **once) for w_ in ffn]
        args += list(ffn)
    return pl.pallas_call(
        functools.partial(_mixfin_kernel, nblk=nblk, blk0=blk0, with_router=router is not None),
        grid=(rows // TM,),
        in_specs=in_specs,
        out_specs=out_specs,
        out_shape=out_shape,
        compiler_params=_params(("arbitrary",), VMEM_LIMIT),
        name="mixer_finish",
    )(*args)


def _moe_kernel(be_ref, nu_ref, xs_ref, wg_ref, wu_ref, wd_ref, y_ref, acc_ref):
    b = pl.program_id(0)
    f = pl.program_id(1)

    @pl.when(b < nu_ref[0])
    def _():
        xs = xs_ref[...].astype(BF16)
        part = jnp.zeros((MOE_TM, D), F32)
        for c in range(0, MOE_TF, MOE_SUB):
            g = _mm(xs, wg_ref[0, :, c:c + MOE_SUB])
            u = _mm(xs, wu_ref[0, :, c:c + MOE_SUB])
            part = part + _mm(g * _sigmoid(g) * u, wd_ref[0, c:c + MOE_SUB, :])

        @pl.when(f == 0)
        def _():
            acc_ref[...] = part

        @pl.when(f != 0)
        def _():
            acc_ref[...] += part

        @pl.when(f == pl.num_programs(1) - 1)
        def _():
            y_ref[...] = acc_ref[...]

    @pl.when(b >= nu_ref[0])
    def _():
        y_ref[...] = jnp.zeros(y_ref.shape, F32)


def _moe_experts(xs, blk_e, n_used, wg, wu, wd):
    cap = xs.shape[0]
    nblk = cap // MOE_TM
    nf = D_FF_EXPERT // MOE_TF

    def fidx(b, f, nu):
        return jnp.where(b < nu[0], f, nf - 1)

    grid_spec = pltpu.PrefetchScalarGridSpec(
        num_scalar_prefetch=2,
        grid=(nblk, nf),
        in_specs=[
            pl.BlockSpec((MOE_TM, D), lambda b, f, be, nu: (b, 0)),
            pl.BlockSpec((1, D, MOE_TF), lambda b, f, be, nu: (be[b], 0, fidx(b, f, nu))),
            pl.BlockSpec((1, D, MOE_TF), lambda b, f, be, nu: (be[b], 0, fidx(b, f, nu))),
            pl.BlockSpec((1, MOE_TF, D), lambda b, f, be, nu: (be[b], fidx(b, f, nu), 0)),
        ],
        out_specs=pl.BlockSpec((MOE_TM, D), lambda b, f, be, nu: (b, 0)),
        scratch_shapes=[pltpu.VMEM((MOE_TM, D), F32)],
    )
    return pl.pallas_call(
        _moe_kernel,
        grid_spec=grid_spec,
        out_shape=jax.ShapeDtypeStruct((cap, D), F32),
        compiler_params=_params(("arbitrary", "arbitrary"), VMEM_LIMIT),
        name="moe_experts",
    )(blk_e, n_used, xs, wg, wu, wd)


def _row_tok_kernel(dest_ref, ends_ref, o_ref, *, n, cap):
    tail = cap - 2 * n

    def fill_tail(i, c):
        i = i.astype(jnp.int32)
        o_ref[2 * n + i] = i
        return c

    lax.fori_loop(0, tail, fill_tail, 0, unroll=8)
    for e in range(N_EXPERTS):
        base = jnp.maximum(ends_ref[e] - MOE_TM, 0)

        def fill(i, c, base=base):
            i = i.astype(jnp.int32)
            o_ref[base + i] = i
            return c

        lax.fori_loop(0, MOE_TM, fill, 0, unroll=8)

    def put(i, c):
        i = i.astype(jnp.int32)
        o_ref[dest_ref[i]] = i
        o_ref[dest_ref[n + i]] = i
        return c

    lax.fori_loop(0, n, put, 0, unroll=8)


def _row_tok(dest, pad_ends, n, cap):
    return pl.pallas_call(
        functools.partial(_row_tok_kernel, n=n, cap=cap),
        in_specs=[pl.BlockSpec(memory_space=pltpu.SMEM)] * 2,
        out_specs=pl.BlockSpec(memory_space=pltpu.SMEM),
        out_shape=jax.ShapeDtypeStruct((cap,), jnp.int32),
        name="row_tok",
    )(dest, pad_ends)


def _moe_route(choices):
    n = choices[0].shape[0]
    a = n * TOP_K
    flat_e = jnp.concatenate(choices)
    onehot = (flat_e[:, None] == jnp.arange(N_EXPERTS, dtype=flat_e.dtype)[None, :]).astype(jnp.int32)
    counts = jnp.sum(onehot, axis=0)
    padded = (counts + MOE_TM - 1) // MOE_TM * MOE_TM
    pad_ends = jnp.cumsum(padded)
    pad_starts = pad_ends - padded
    dest = jnp.sum(onehot * (jnp.cumsum(onehot, axis=0) - onehot + pad_starts[None, :]), axis=1)
    cap = a + N_EXPERTS * MOE_TM
    nblk = cap // MOE_TM
    row_tok = _row_tok(dest.astype(jnp.int32), pad_ends.astype(jnp.int32), n, cap)
    blk_start = jnp.arange(nblk, dtype=jnp.int32) * MOE_TM
    blk_e = jnp.minimum(jnp.sum((pad_ends[None, :] <= blk_start[:, None]).astype(jnp.int32), axis=1),
                        N_EXPERTS - 1)
    n_used = (pad_ends[-1] // MOE_TM).astype(jnp.int32).reshape(1)
    last_e = blk_e[jnp.maximum(n_used[0] - 1, 0)]
    blk_e = jnp.where(jnp.arange(nblk) < n_used[0], blk_e, last_e)
    return dest, row_tok, blk_e, n_used


def _final_kernel(x_ref, y0_ref, y1_ref, gt_ref, mod_ref, fg_ref, o_ref):
    mod = mod_ref[...]
    gt = gt_ref[...]
    f = gt[:, 0:1] * y0_ref[...] + gt[:, 1:2] * y1_ref[...]
    x = x_ref[...] + mod[0:1, 5 * D:6 * D] * f
    o_ref[...] = _rmsnorm_rows(x) * fg_ref[...]


def _moe_combine_final(x, y01, gt, mod, final_g):
    n = x.shape[0]
    tm = COMBINE_TM
    assert n % tm == 0
    row = lambda i: (i, 0)
    second = lambda i: (i + n // tm, 0)
    const = lambda i: (0, 0)
    return pl.pallas_call(
        _final_kernel,
        grid=(n // tm,),
        in_specs=[pl.BlockSpec((tm, D), row), pl.BlockSpec((tm, D), row), pl.BlockSpec((tm, D), second),
                  pl.BlockSpec((tm, LANE), row), pl.BlockSpec((8, N_MOD * D), const), pl.BlockSpec((1, D), const)],
        out_specs=pl.BlockSpec((tm, D), row),
        out_shape=jax.ShapeDtypeStruct((n, D), F32),
        compiler_params=_params(("arbitrary",), VMEM_LIMIT),
        name="moe_combine_final",
    )(x, y01, y01, gt, mod, final_g.reshape(1, D))


def _rope_tables(n):
    lane = jnp.arange(LANE, dtype=jnp.int32) % HEAD_DIM
    inv = ROPE_BASE ** (-jnp.arange(0, AXIS_DIM, 2, dtype=F32) / AXIS_DIM)
    freq = inv[lane % (AXIS_DIM // 2)]
    row_axis = (lane // AXIS_DIM) == 0
    sign = jnp.where((lane % AXIS_DIM) < AXIS_DIM // 2, -1.0, 1.0)
    reps = ATT_BLOCK // GRID_W

    def tables(count, on_axis):
        ang = jnp.arange(count, dtype=F32)[:, None] * freq[None, :]
        return jnp.where(on_axis, jnp.cos(ang), 0.0), jnp.where(on_axis, jnp.sin(ang) * sign, 0.0)

    by_block = lambda t: jnp.pad(t.reshape(-1, reps, LANE), ((0, 0), (0, 8 - reps), (0, 0)))
    cosr, sinr = tables(n // GRID_W, row_axis)
    cosc, sinc = tables(GRID_W, ~row_axis)
    return by_block(cosr), by_block(sinr), cosc, sinc


def _prep_w_in(w):
    pad = jnp.zeros((D, LANE - N_AB), w.dtype)
    return jnp.concatenate([w[:, :_C_Z], w[:, _C_A:_C_Q], pad, w[:, _C_Z:_C_A], w[:, _C_Q:_C_END]],
                           axis=1).astype(BF16)


def kernel(x, c, ctx, c_ctx, w_mod, b_mod, w_in, w_out, conv_w, dn_conv_w, dn_a_log, dn_dt_bias, dn_norm_g,
           attn_sink, ffn_w_gate, ffn_w_up, ffn_w_down, moe_router, moe_w_gate, moe_w_up, moe_w_down,
           final_norm_g):
    bsz, n, d = x.shape
    ctx_len = ctx.shape[1]
    depth = w_in.shape[0]
    assert bsz == 1 and d == D and ctx_len == TM and n % TM == 0 and n % GRID_W == 0
    rope = _rope_tables(n)
    mods = _mod_vectors(c, c_ctx, w_mod, b_mod)
    bd = _head_blockdiag(DN_DIM, DN_HEAD_DIM)
    stream = (ctx[0], x[0], 0)
    for layer in range(depth):
        last = layer == depth - 1
        mod = mods[layer]
        pconv, pz, pq, pkv, qn, kn, vv, gb = _in_proj(*stream, mod, _prep_w_in(w_in[layer]), dn_conv_w[layer],
                                                      dn_a_log[layer], dn_dt_bias[layer], bd)
        to_bf16 = []
        if layer % 2 == 0:
            to_bf16 += [w[layer // 2] for w in (ffn_w_gate, ffn_w_up, ffn_w_down)]
        if (layer + 1) % 2 == 1 and layer + 1 < depth:
            j = (layer + 1) // 2
            to_bf16 += [w[j].reshape(-1, w.shape[-1]) for w in (moe_w_gate, moe_w_up, moe_w_down)]
        o_f, o_b, *cast = _delta_net(qn, kn, vv, gb, ctx_len, tuple(to_bf16))
        if layer % 2 == 0:
            ffn_bf16, cast = cast[:3], cast[3:]
        if cast:
            moe_bf16 = [c.reshape(w.shape[1:]) for c, w in zip(cast, (moe_w_gate, moe_w_up, moe_w_down))]
        sink = jnp.zeros((8, LANE), F32).at[0, :ATT_HEADS].set(attn_sink[layer])
        yc = _attention(pq, pkv, rope, sink, ctx_len, with_ctx=not last)
        router = None
        if layer % 2 == 1:
            wr = jnp.zeros((D, LANE), F32).at[:, :N_EXPERTS].set(moe_router[layer // 2])
            wr1 = wr.astype(BF16)
            router = (wr1, (wr - wr1.astype(F32)).astype(BF16))
        outs = _mixer_finish(*stream, mod, pconv, conv_w[layer], o_f, o_b, pz, dn_norm_g[layer], yc,
                             w_out[layer].astype(BF16), bd, ctx_len, with_ctx=not last, router=router,
                             ffn=ffn_bf16 if layer % 2 == 0 else None)
        if layer % 2 == 0:
            assert not last
            h, = outs
            stream = (h, h, ctx_len // TM)
        else:
            assert last
            x1, hx, route = outs
            dest, row_tok, blk_e, n_used = _moe_route([route[:, 2 + s].astype(jnp.int32) for s in range(TOP_K)])
            take = lambda rows_, idx: rows_.at[idx].get(mode="promise_in_bounds")
            y = _moe_experts(take(hx, row_tok), blk_e, n_used, *moe_bf16)
            h = _moe_combine_final(x1, take(y, dest), route, mod, final_norm_g)
    return h.reshape(bsz, n, d)
```
